```python
import jax, jax.numpy as jnp
from jax import lax
import numpy as np

D_MODEL = 1024
BATCH = 8
SEQ = 2048
DEPTH = 2

HEAD_DIM = 64
N_MIXERS = 4
GROUP_WIDTH = D_MODEL // N_MIXERS
D_MIX = N_MIXERS * GROUP_WIDTH
FOX_HEADS = GROUP_WIDTH // HEAD_DIM
FOX_BLOCK = 128
CONV_CHANNELS = GROUP_WIDTH
CONV_WIDTH = 31
SWA_Q_HEADS = GROUP_WIDTH // HEAD_DIM
SWA_KV_HEADS = SWA_Q_HEADS // 2
SWA_WINDOW = 128
SGU_GROUPS = GROUP_WIDTH // HEAD_DIM
SGU_CHUNK = 128
D_FF = 11 * D_MODEL // 4
FFN_CONV_WIDTH = 3
EPS = 1e-6

FOX_COLS = 3 * GROUP_WIDTH + FOX_HEADS
CONV_COLS = 2 * CONV_CHANNELS
SWA_COLS = SWA_Q_HEADS * HEAD_DIM + 2 * SWA_KV_HEADS * HEAD_DIM
SGU_COLS = 2 * GROUP_WIDTH
IN_COLS = FOX_COLS + CONV_COLS + SWA_COLS + SGU_COLS

kernel_name = "hybrid_parallel_heads_fox_conformer_swa_sgu"


def rms_norm(x, g):
    xf = x.astype(jnp.float32)
    y = xf * lax.rsqrt(jnp.mean(xf * xf, axis=-1, keepdims=True) + EPS)
    return (y * g.astype(jnp.float32)).astype(x.dtype)


def layer_norm(x, g, b):
    xf = x.astype(jnp.float32)
    mu = jnp.mean(xf, axis=-1, keepdims=True)
    xc = xf - mu
    var = jnp.mean(xc * xc, axis=-1, keepdims=True)
    y = xc * lax.rsqrt(var + EPS) * g.astype(jnp.float32) + b.astype(jnp.float32)
    return y.astype(x.dtype)


def causal_depthwise_conv(x, w, b):
    k_width, ch = w.shape
    y = lax.conv_general_dilated(
        x, w[:, None, :].astype(x.dtype), window_strides=(1,), padding=[(k_width - 1, 0)],
        dimension_numbers=('NWC', 'WIO', 'NWC'), feature_group_count=ch)
    return y + b.astype(x.dtype)


def fox_attention(q, k, v, fg_logit, b_f):
    bsz, seq, heads, dh = q.shape
    log_f = jax.nn.log_sigmoid(fg_logit.astype(jnp.float32) + b_f.astype(jnp.float32))
    cum = jnp.cumsum(log_f, axis=1)
    nb = seq // FOX_BLOCK
    qb = q.reshape(bsz, nb, FOX_BLOCK, heads, dh).transpose(1, 0, 2, 3, 4)
    cb = cum.reshape(bsz, nb, FOX_BLOCK, heads).transpose(1, 0, 2, 3)
    cum_k = cum.transpose(0, 2, 1)
    key_pos = jnp.arange(seq)
    scale = HEAD_DIM ** -0.5

    def block(args):
        q_blk, c_blk, i = args
        s = jnp.einsum('bqhd,bkhd->bhqk', q_blk, k, preferred_element_type=jnp.float32) * scale
        s = s + c_blk.transpose(0, 2, 1)[..., None] - cum_k[:, :, None, :]
        q_pos = i * FOX_BLOCK + jnp.arange(FOX_BLOCK)
        s = jnp.where(key_pos[None, :] <= q_pos[:, None], s, -jnp.inf)
        p = jax.nn.softmax(s, axis=-1).astype(v.dtype)
        return jnp.einsum('bhqk,bkhd->bqhd', p, v)

    out = lax.map(block, (qb, cb, jnp.arange(nb)))
    return out.transpose(1, 0, 2, 3, 4).reshape(bsz, seq, heads * dh)


def swa_attention(q, k, v, sinks):
    bsz, seq, hq, dh = q.shape
    hkv = k.shape[2]
    grp = hq // hkv
    win = SWA_WINDOW
    nb = seq // win
    qb = q.reshape(bsz, nb, win, hkv, grp, dh)
    pad = jnp.zeros((bsz, win, hkv, dh), k.dtype)
    kp = jnp.concatenate([pad, k], axis=1).reshape(bsz, nb + 1, win, hkv, dh)
    vp = jnp.concatenate([pad.astype(v.dtype), v], axis=1).reshape(bsz, nb + 1, win, hkv, dh)
    kw = jnp.concatenate([kp[:, :-1], kp[:, 1:]], axis=2)
    vw = jnp.concatenate([vp[:, :-1], vp[:, 1:]], axis=2)
    s = jnp.einsum('bnqhgd,bnkhd->bnhgqk', qb, kw, preferred_element_type=jnp.float32) * (HEAD_DIM ** -0.5)
    qi = jnp.arange(win)[:, None]
    kj = jnp.arange(2 * win)[None, :]
    blk = jnp.arange(nb)[:, None, None]
    mask = (kj > qi) & (kj <= qi + win) & (blk * win + kj >= win)
    s = jnp.where(mask[None, :, None, None], s, -jnp.inf)
    sink = sinks.astype(jnp.float32).reshape(hkv, grp)[None, None, :, :, None, None]
    m = jnp.maximum(jnp.max(s, axis=-1, keepdims=True), sink)
    p = jnp.exp(s - m)
    p = p / (jnp.sum(p, axis=-1, keepdims=True) + jnp.exp(sink - m))
    out = jnp.einsum('bnhgqk,bnkhd->bnqhgd', p.astype(v.dtype), vw)
    return out.reshape(bsz, seq, hq * dh)


def conformer_conv(z, conv_w, conv_b, ln_g, ln_b, pw_w, pw_b):
    a, g = jnp.split(z, 2, axis=-1)
    h = a * jax.nn.sigmoid(g)
    h = causal_depthwise_conv(h, conv_w, conv_b)
    h = jax.nn.silu(layer_norm(h, ln_g, ln_b))
    return h @ pw_w + pw_b


def spatial_gating(z, ln_g, ln_b, w_s, b_s):
    z = jax.nn.gelu(z)
    u, v = jnp.split(z, 2, axis=-1)
    v = layer_norm(v, ln_g, ln_b)
    bsz, seq, _ = v.shape
    nc = seq // SGU_CHUNK
    gd = GROUP_WIDTH // SGU_GROUPS
    v = v.reshape(bsz, nc, SGU_CHUNK, SGU_GROUPS, gd)
    tri = jnp.tril(jnp.ones((SGU_CHUNK, SGU_CHUNK), dtype=bool))
    w = jnp.where(tri[None], w_s, 0).astype(v.dtype)
    mix = jnp.einsum('gts,bnsgc->bntgc', w, v) + b_s.T.astype(v.dtype)[:, :, None]
    return u * mix.reshape(bsz, seq, GROUP_WIDTH)


def hybrid_mixer(h, w_in, b_fgate, conv_w, conv_b, conv_ln_g, conv_ln_b, conv_pw_w, conv_pw_b,
                 swa_sinks, sgu_ln_g, sgu_ln_b, sgu_w, sgu_b, g_group, w_out):
    bsz, seq, _ = h.shape
    z = h @ w_in
    z_fox, z_conv, z_swa, z_sgu = jnp.split(
        z, [FOX_COLS, FOX_COLS + CONV_COLS, FOX_COLS + CONV_COLS + SWA_COLS], axis=-1)
    fq, fk, fv, ff = jnp.split(z_fox, [GROUP_WIDTH, 2 * GROUP_WIDTH, 3 * GROUP_WIDTH], axis=-1)
    hs = (bsz, seq, FOX_HEADS, HEAD_DIM)
    y_fox = fox_attention(fq.reshape(hs), fk.reshape(hs), fv.reshape(hs), ff, b_fgate)
    y_conv = conformer_conv(z_conv, conv_w, conv_b, conv_ln_g, conv_ln_b, conv_pw_w, conv_pw_b)
    nq = SWA_Q_HEADS * HEAD_DIM
    nkv = SWA_KV_HEADS * HEAD_DIM
    sq, sk, sv = jnp.split(z_swa, [nq, nq + nkv], axis=-1)
    y_swa = swa_attention(sq.reshape(bsz, seq, SWA_Q_HEADS, HEAD_DIM),
                          sk.reshape(bsz, seq, SWA_KV_HEADS, HEAD_DIM),
                          sv.reshape(bsz, seq, SWA_KV_HEADS, HEAD_DIM), swa_sinks)
    y_sgu = spatial_gating(z_sgu, sgu_ln_g, sgu_ln_b, sgu_w, sgu_b)
    y = jnp.stack([y_fox, y_conv, y_swa, y_sgu], axis=2)
    y = rms_norm(y, g_group.reshape(N_MIXERS, GROUP_WIDTH))
    return y.reshape(bsz, seq, D_MIX) @ w_out


def conv_ffn(h, w_up, conv_w, conv_b, w_down):
    u = causal_depthwise_conv(h @ w_up, conv_w, conv_b)
    g, val = jnp.split(u, 2, axis=-1)
    return (jax.nn.silu(g) * val) @ w_down


def _fwd_setup_inputs(seed: int = 0) -> dict:
    key = jax.random.key(seed)
    ks = jax.random.split(key, 32)
    L = DEPTH

    def nrm(k, shape, scale):
        return jax.random.normal(k, shape, jnp.float32) * scale

    return {
        "x": nrm(ks[0], (BATCH, SEQ, D_MODEL), 1.0),
        "c": nrm(ks[1], (BATCH, D_MODEL), 1.0),
        "w_ada": nrm(ks[2], (L, D_MODEL, 6 * D_MODEL), 0.5 * D_MODEL ** -0.5),
        "b_ada": nrm(ks[3], (L, 6 * D_MODEL), 0.02),
        "g_pre_mix": 1.0 + nrm(ks[4], (L, D_MODEL), 0.05),
        "g_post_mix": 1.0 + nrm(ks[5], (L, D_MODEL), 0.05),
        "g_pre_ffn": 1.0 + nrm(ks[6], (L, D_MODEL), 0.05),
        "g_post_ffn": 1.0 + nrm(ks[7], (L, D_MODEL), 0.05),
        "w_in": nrm(ks[8], (L, D_MODEL, IN_COLS), D_MODEL ** -0.5),
        "b_fgate": 2.0 + nrm(ks[9], (L, FOX_HEADS), 0.1),
        "conv_w": nrm(ks[10], (L, CONV_WIDTH, CONV_CHANNELS), CONV_WIDTH ** -0.5),
        "conv_b": nrm(ks[11], (L, CONV_CHANNELS), 0.02),
        "conv_ln_g": 1.0 + nrm(ks[12], (L, CONV_CHANNELS), 0.05),
        "conv_ln_b": nrm(ks[13], (L, CONV_CHANNELS), 0.02),
        "conv_pw_w": nrm(ks[14], (L, CONV_CHANNELS, CONV_CHANNELS), CONV_CHANNELS ** -0.5),
        "conv_pw_b": nrm(ks[15], (L, CONV_CHANNELS), 0.02),
        "swa_sinks": nrm(ks[16], (L, SWA_Q_HEADS), 0.5),
        "sgu_ln_g": 1.0 + nrm(ks[17], (L, GROUP_WIDTH), 0.05),
        "sgu_ln_b": nrm(ks[18], (L, GROUP_WIDTH), 0.02),
        "sgu_w": nrm(ks[19], (L, SGU_GROUPS, SGU_CHUNK, SGU_CHUNK), SGU_CHUNK ** -0.5),
        "sgu_b": 1.0 + nrm(ks[20], (L, SGU_GROUPS, SGU_CHUNK), 0.05),
        "g_group": 1.0 + nrm(ks[21], (L, D_MIX), 0.05),
        "w_out": nrm(ks[22], (L, D_MIX, D_MODEL), D_MIX ** -0.5),
        "ffn_w_up": nrm(ks[23], (L, D_MODEL, 2 * D_FF), D_MODEL ** -0.5),
        "ffn_conv_w": nrm(ks[24], (L, FFN_CONV_WIDTH, 2 * D_FF), FFN_CONV_WIDTH ** -0.5),
        "ffn_conv_b": nrm(ks[25], (L, 2 * D_FF), 0.02),
        "ffn_w_down": nrm(ks[26], (L, D_FF, D_MODEL), D_FF ** -0.5),
    }


def _fwd_reference(x, c, w_ada, b_ada, g_pre_mix, g_post_mix, g_pre_ffn, g_post_ffn, w_in, b_fgate,
              conv_w, conv_b, conv_ln_g, conv_ln_b, conv_pw_w, conv_pw_b, swa_sinks,
              sgu_ln_g, sgu_ln_b, sgu_w, sgu_b, g_group, w_out,
              ffn_w_up, ffn_conv_w, ffn_conv_b, ffn_w_down):
    c_act = jax.nn.silu(c)
    for l in range(DEPTH):
        mod = (c_act @ w_ada[l] + b_ada[l])[:, None, :]
        sh1, sc1, ga1, sh2, sc2, ga2 = jnp.split(mod, 6, axis=-1)
        h = rms_norm(x, g_pre_mix[l]) * (1.0 + sc1) + sh1
        y = hybrid_mixer(h, w_in[l], b_fgate[l], conv_w[l], conv_b[l], conv_ln_g[l], conv_ln_b[l],
                         conv_pw_w[l], conv_pw_b[l], swa_sinks[l], sgu_ln_g[l], sgu_ln_b[l],
                         sgu_w[l], sgu_b[l], g_group[l], w_out[l])
        x = x + ga1 * rms_norm(y, g_post_mix[l])
        h = rms_norm(x, g_pre_ffn[l]) * (1.0 + sc2) + sh2
        y = conv_ffn(h, ffn_w_up[l], ffn_conv_w[l], ffn_conv_b[l], ffn_w_down[l])
        x = x + ga2 * rms_norm(y, g_post_ffn[l])
    return x


import jax as _jax
import jax.numpy as _jnp

TWIN_FORMAT = 'train_step'
FWD_PARAMS = ['x', 'c', 'w_ada', 'b_ada', 'g_pre_mix', 'g_post_mix', 'g_pre_ffn', 'g_post_ffn', 'w_in', 'b_fgate', 'conv_w', 'conv_b', 'conv_ln_g', 'conv_ln_b', 'conv_pw_w', 'conv_pw_b', 'swa_sinks', 'sgu_ln_g', 'sgu_ln_b', 'sgu_w', 'sgu_b', 'g_group', 'w_out', 'ffn_w_up', 'ffn_conv_w', 'ffn_conv_b', 'ffn_w_down']
TWIN_WEIGHTS = ['w_ada', 'b_ada', 'g_pre_mix', 'g_post_mix', 'g_pre_ffn', 'g_post_ffn', 'w_in', 'b_fgate', 'conv_w', 'conv_b', 'conv_ln_g', 'conv_ln_b', 'conv_pw_w', 'conv_pw_b', 'swa_sinks', 'sgu_ln_g', 'sgu_ln_b', 'sgu_w', 'sgu_b', 'g_group', 'w_out', 'ffn_w_up', 'ffn_conv_w', 'ffn_conv_b', 'ffn_w_down']
TWIN_DIFF_INPUT = 'x'
TWIN_INPUTS = ['x', 'c', 'w_ada', 'b_ada', 'g_pre_mix', 'g_post_mix', 'g_pre_ffn', 'g_post_ffn', 'w_in', 'b_fgate', 'conv_w', 'conv_b', 'conv_ln_g', 'conv_ln_b', 'conv_pw_w', 'conv_pw_b', 'swa_sinks', 'sgu_ln_g', 'sgu_ln_b', 'sgu_w', 'sgu_b', 'g_group', 'w_out', 'ffn_w_up', 'ffn_conv_w', 'ffn_conv_b', 'ffn_w_down', 'loss_target', 'm_w_ada', 'm_b_ada', 'm_g_pre_mix', 'm_g_post_mix', 'm_g_pre_ffn', 'm_g_post_ffn', 'm_w_in', 'm_b_fgate', 'm_conv_w', 'm_conv_b', 'm_conv_ln_g', 'm_conv_ln_b', 'm_conv_pw_w', 'm_conv_pw_b', 'm_swa_sinks', 'm_sgu_ln_g', 'm_sgu_ln_b', 'm_sgu_w', 'm_sgu_b', 'm_g_group', 'm_w_out', 'm_ffn_w_up', 'm_ffn_conv_w', 'm_ffn_conv_b', 'm_ffn_w_down', 'v_w_ada', 'v_b_ada', 'v_g_pre_mix', 'v_g_post_mix', 'v_g_pre_ffn', 'v_g_post_ffn', 'v_w_in', 'v_b_fgate', 'v_conv_w', 'v_conv_b', 'v_conv_ln_g', 'v_conv_ln_b', 'v_conv_pw_w', 'v_conv_pw_b', 'v_swa_sinks', 'v_sgu_ln_g', 'v_sgu_ln_b', 'v_sgu_w', 'v_sgu_b', 'v_g_group', 'v_w_out', 'v_ffn_w_up', 'v_ffn_conv_w', 'v_ffn_conv_b', 'v_ffn_w_down']
TWIN_OUTPUTS = ['loss', 'grad_x', 'grad_w_ada', 'grad_b_ada', 'grad_g_pre_mix', 'grad_g_post_mix', 'grad_g_pre_ffn', 'grad_g_post_ffn', 'grad_w_in', 'grad_b_fgate', 'grad_conv_w', 'grad_conv_b', 'grad_conv_ln_g', 'grad_conv_ln_b', 'grad_conv_pw_w', 'grad_conv_pw_b', 'grad_swa_sinks', 'grad_sgu_ln_g', 'grad_sgu_ln_b', 'grad_sgu_w', 'grad_sgu_b', 'grad_g_group', 'grad_w_out', 'grad_ffn_w_up', 'grad_ffn_conv_w', 'grad_ffn_conv_b', 'grad_ffn_w_down', 'delta_w_ada', 'delta_b_ada', 'delta_g_pre_mix', 'delta_g_post_mix', 'delta_g_pre_ffn', 'delta_g_post_ffn', 'delta_w_in', 'delta_b_fgate', 'delta_conv_w', 'delta_conv_b', 'delta_conv_ln_g', 'delta_conv_ln_b', 'delta_conv_pw_w', 'delta_conv_pw_b', 'delta_swa_sinks', 'delta_sgu_ln_g', 'delta_sgu_ln_b', 'delta_sgu_w', 'delta_sgu_b', 'delta_g_group', 'delta_w_out', 'delta_ffn_w_up', 'delta_ffn_conv_w', 'delta_ffn_conv_b', 'delta_ffn_w_down', 'new_m_w_ada', 'new_m_b_ada', 'new_m_g_pre_mix', 'new_m_g_post_mix', 'new_m_g_pre_ffn', 'new_m_g_post_ffn', 'new_m_w_in', 'new_m_b_fgate', 'new_m_conv_w', 'new_m_conv_b', 'new_m_conv_ln_g', 'new_m_conv_ln_b', 'new_m_conv_pw_w', 'new_m_conv_pw_b', 'new_m_swa_sinks', 'new_m_sgu_ln_g', 'new_m_sgu_ln_b', 'new_m_sgu_w', 'new_m_sgu_b', 'new_m_g_group', 'new_m_w_out', 'new_m_ffn_w_up', 'new_m_ffn_conv_w', 'new_m_ffn_conv_b', 'new_m_ffn_w_down', 'new_v_w_ada', 'new_v_b_ada', 'new_v_g_pre_mix', 'new_v_g_post_mix', 'new_v_g_pre_ffn', 'new_v_g_post_ffn', 'new_v_w_in', 'new_v_b_fgate', 'new_v_conv_w', 'new_v_conv_b', 'new_v_conv_ln_g', 'new_v_conv_ln_b', 'new_v_conv_pw_w', 'new_v_conv_pw_b', 'new_v_swa_sinks', 'new_v_sgu_ln_g', 'new_v_sgu_ln_b', 'new_v_sgu_w', 'new_v_sgu_b', 'new_v_g_group', 'new_v_w_out', 'new_v_ffn_w_up', 'new_v_ffn_conv_w', 'new_v_ffn_conv_b', 'new_v_ffn_w_down']
TWIN_LEAF_KINDS = {'loss': 'loss', 'grad_x': 'grad_x', 'grad_w_ada': 'grad_w', 'grad_b_ada': 'grad_w', 'grad_g_pre_mix': 'grad_w', 'grad_g_post_mix': 'grad_w', 'grad_g_pre_ffn': 'grad_w', 'grad_g_post_ffn': 'grad_w', 'grad_w_in': 'grad_w', 'grad_b_fgate': 'grad_w', 'grad_conv_w': 'grad_w', 'grad_conv_b': 'grad_w', 'grad_conv_ln_g': 'grad_w', 'grad_conv_ln_b': 'grad_w', 'grad_conv_pw_w': 'grad_w', 'grad_conv_pw_b': 'grad_w', 'grad_swa_sinks': 'grad_w', 'grad_sgu_ln_g': 'grad_w', 'grad_sgu_ln_b': 'grad_w', 'grad_sgu_w': 'grad_w', 'grad_sgu_b': 'grad_w', 'grad_g_group': 'grad_w', 'grad_w_out': 'grad_w', 'grad_ffn_w_up': 'grad_w', 'grad_ffn_conv_w': 'grad_w', 'grad_ffn_conv_b': 'grad_w', 'grad_ffn_w_down': 'grad_w', 'delta_w_ada': 'delta_w', 'delta_b_ada': 'delta_w', 'delta_g_pre_mix': 'delta_w', 'delta_g_post_mix': 'delta_w', 'delta_g_pre_ffn': 'delta_w', 'delta_g_post_ffn': 'delta_w', 'delta_w_in': 'delta_w', 'delta_b_fgate': 'delta_w', 'delta_conv_w': 'delta_w', 'delta_conv_b': 'delta_w', 'delta_conv_ln_g': 'delta_w', 'delta_conv_ln_b': 'delta_w', 'delta_conv_pw_w': 'delta_w', 'delta_conv_pw_b': 'delta_w', 'delta_swa_sinks': 'delta_w', 'delta_sgu_ln_g': 'delta_w', 'delta_sgu_ln_b': 'delta_w', 'delta_sgu_w': 'delta_w', 'delta_sgu_b': 'delta_w', 'delta_g_group': 'delta_w', 'delta_w_out': 'delta_w', 'delta_ffn_w_up': 'delta_w', 'delta_ffn_conv_w': 'delta_w', 'delta_ffn_conv_b': 'delta_w', 'delta_ffn_w_down': 'delta_w', 'new_m_w_ada': 'new_m', 'new_m_b_ada': 'new_m', 'new_m_g_pre_mix': 'new_m', 'new_m_g_post_mix': 'new_m', 'new_m_g_pre_ffn': 'new_m', 'new_m_g_post_ffn': 'new_m', 'new_m_w_in': 'new_m', 'new_m_b_fgate': 'new_m', 'new_m_conv_w': 'new_m', 'new_m_conv_b': 'new_m', 'new_m_conv_ln_g': 'new_m', 'new_m_conv_ln_b': 'new_m', 'new_m_conv_pw_w': 'new_m', 'new_m_conv_pw_b': 'new_m', 'new_m_swa_sinks': 'new_m', 'new_m_sgu_ln_g': 'new_m', 'new_m_sgu_ln_b': 'new_m', 'new_m_sgu_w': 'new_m', 'new_m_sgu_b': 'new_m', 'new_m_g_group': 'new_m', 'new_m_w_out': 'new_m', 'new_m_ffn_w_up': 'new_m', 'new_m_ffn_conv_w': 'new_m', 'new_m_ffn_conv_b': 'new_m', 'new_m_ffn_w_down': 'new_m', 'new_v_w_ada': 'new_v', 'new_v_b_ada': 'new_v', 'new_v_g_pre_mix': 'new_v', 'new_v_g_post_mix': 'new_v', 'new_v_g_pre_ffn': 'new_v', 'new_v_g_post_ffn': 'new_v', 'new_v_w_in': 'new_v', 'new_v_b_fgate': 'new_v', 'new_v_conv_w': 'new_v', 'new_v_conv_b': 'new_v', 'new_v_conv_ln_g': 'new_v', 'new_v_conv_ln_b': 'new_v', 'new_v_conv_pw_w': 'new_v', 'new_v_conv_pw_b': 'new_v', 'new_v_swa_sinks': 'new_v', 'new_v_sgu_ln_g': 'new_v', 'new_v_sgu_ln_b': 'new_v', 'new_v_sgu_w': 'new_v', 'new_v_sgu_b': 'new_v', 'new_v_g_group': 'new_v', 'new_v_w_out': 'new_v', 'new_v_ffn_w_up': 'new_v', 'new_v_ffn_conv_w': 'new_v', 'new_v_ffn_conv_b': 'new_v', 'new_v_ffn_w_down': 'new_v'}


def _forward(args):
    return _fwd_reference(*[args[k] for k in FWD_PARAMS])


def _output_shape():
    out = _jax.eval_shape(lambda: _forward(_fwd_setup_inputs(0)))
    return out.shape, out.dtype

N_MICROBATCH = 1
ADAM_LR = 0.001
ADAM_B1 = 0.9
ADAM_B2 = 0.999
ADAM_EPS = 1e-08
ADAM_WD = 0.01
ADAM_STEP = 10
PER_EXAMPLE_BATCH_AXIS = {'x': 0, 'c': 0, 'loss_target': 0}
SHARED_INPUTS = []
_WEIGHT_DTYPES = {'w_ada': _jnp.float32, 'b_ada': _jnp.float32, 'g_pre_mix': _jnp.float32, 'g_post_mix': _jnp.float32, 'g_pre_ffn': _jnp.float32, 'g_post_ffn': _jnp.float32, 'w_in': _jnp.float32, 'b_fgate': _jnp.float32, 'conv_w': _jnp.float32, 'conv_b': _jnp.float32, 'conv_ln_g': _jnp.float32, 'conv_ln_b': _jnp.float32, 'conv_pw_w': _jnp.float32, 'conv_pw_b': _jnp.float32, 'swa_sinks': _jnp.float32, 'sgu_ln_g': _jnp.float32, 'sgu_ln_b': _jnp.float32, 'sgu_w': _jnp.float32, 'sgu_b': _jnp.float32, 'g_group': _jnp.float32, 'w_out': _jnp.float32, 'ffn_w_up': _jnp.float32, 'ffn_conv_w': _jnp.float32, 'ffn_conv_b': _jnp.float32, 'ffn_w_down': _jnp.float32}
MOMENT_SCALE = {'w_ada': 1.086150e+00, 'b_ada': 1.946875e+00, 'g_pre_mix': 1.708051e-01, 'g_post_mix': 2.042399e+00, 'g_pre_ffn': 1.117498e-01, 'g_post_ffn': 1.882574e+00, 'w_in': 3.214061e-01, 'b_fgate': 3.837349e-01, 'conv_w': 1.934671e-01, 'conv_b': 1.192519e+00, 'conv_ln_g': 5.322451e-01, 'conv_ln_b': 7.282178e-01, 'conv_pw_w': 3.577348e-01, 'conv_pw_b': 1.551712e+00, 'swa_sinks': 2.780119e-02, 'sgu_ln_g': 4.182666e-02, 'sgu_ln_b': 4.571662e-02, 'sgu_w': 2.775264e-02, 'sgu_b': 4.307835e-02, 'g_group': 5.217498e-01, 'w_out': 5.458475e-01, 'ffn_w_up': 6.520372e-02, 'ffn_conv_w': 7.623725e-02, 'ffn_conv_b': 1.677285e-01, 'ffn_w_down': 1.278538e-01}


def _to_microbatches(a, axis):
    t = _jnp.moveaxis(a, axis, 0)
    t = t.reshape((N_MICROBATCH, t.shape[0] // N_MICROBATCH) + t.shape[1:])
    return _jnp.moveaxis(t, 1, axis + 1)


def setup_inputs(seed: int = 0) -> dict:
    inp = _fwd_setup_inputs(seed)
    key = _jax.random.fold_in(_jax.random.key(seed), 7919)
    shape, _ = _output_shape()
    out = dict(inp)
    out["loss_target"] = _jax.random.normal(_jax.random.fold_in(key, 0), shape, _jnp.float32)
    for i, name in enumerate(TWIN_WEIGHTS):
        w = inp[name].astype(_jnp.float32)
        if MOMENT_SCALE is None:
            s = _jnp.sqrt(_jnp.mean(_jnp.square(w)) + 1e-30)
        else:
            s = MOMENT_SCALE[name]
        km, kv = _jax.random.split(_jax.random.fold_in(key, i + 1))
        out[name] = w
        out["m_" + name] = s * _jax.random.normal(km, w.shape, _jnp.float32)
        out["v_" + name] = (s * s) * _jax.random.uniform(kv, w.shape, _jnp.float32, 0.5, 1.5)
    if N_MICROBATCH > 1:
        for name, axis in PER_EXAMPLE_BATCH_AXIS.items():
            out[name] = _to_microbatches(out[name], axis)
    return {'x': out['x'], 'c': out['c'], 'w_ada': out['w_ada'], 'b_ada': out['b_ada'], 'g_pre_mix': out['g_pre_mix'], 'g_post_mix': out['g_post_mix'], 'g_pre_ffn': out['g_pre_ffn'], 'g_post_ffn': out['g_post_ffn'], 'w_in': out['w_in'], 'b_fgate': out['b_fgate'], 'conv_w': out['conv_w'], 'conv_b': out['conv_b'], 'conv_ln_g': out['conv_ln_g'], 'conv_ln_b': out['conv_ln_b'], 'conv_pw_w': out['conv_pw_w'], 'conv_pw_b': out['conv_pw_b'], 'swa_sinks': out['swa_sinks'], 'sgu_ln_g': out['sgu_ln_g'], 'sgu_ln_b': out['sgu_ln_b'], 'sgu_w': out['sgu_w'], 'sgu_b': out['sgu_b'], 'g_group': out['g_group'], 'w_out': out['w_out'], 'ffn_w_up': out['ffn_w_up'], 'ffn_conv_w': out['ffn_conv_w'], 'ffn_conv_b': out['ffn_conv_b'], 'ffn_w_down': out['ffn_w_down'], 'loss_target': out['loss_target'], 'm_w_ada': out['m_w_ada'], 'm_b_ada': out['m_b_ada'], 'm_g_pre_mix': out['m_g_pre_mix'], 'm_g_post_mix': out['m_g_post_mix'], 'm_g_pre_ffn': out['m_g_pre_ffn'], 'm_g_post_ffn': out['m_g_post_ffn'], 'm_w_in': out['m_w_in'], 'm_b_fgate': out['m_b_fgate'], 'm_conv_w': out['m_conv_w'], 'm_conv_b': out['m_conv_b'], 'm_conv_ln_g': out['m_conv_ln_g'], 'm_conv_ln_b': out['m_conv_ln_b'], 'm_conv_pw_w': out['m_conv_pw_w'], 'm_conv_pw_b': out['m_conv_pw_b'], 'm_swa_sinks': out['m_swa_sinks'], 'm_sgu_ln_g': out['m_sgu_ln_g'], 'm_sgu_ln_b': out['m_sgu_ln_b'], 'm_sgu_w': out['m_sgu_w'], 'm_sgu_b': out['m_sgu_b'], 'm_g_group': out['m_g_group'], 'm_w_out': out['m_w_out'], 'm_ffn_w_up': out['m_ffn_w_up'], 'm_ffn_conv_w': out['m_ffn_conv_w'], 'm_ffn_conv_b': out['m_ffn_conv_b'], 'm_ffn_w_down': out['m_ffn_w_down'], 'v_w_ada': out['v_w_ada'], 'v_b_ada': out['v_b_ada'], 'v_g_pre_mix': out['v_g_pre_mix'], 'v_g_post_mix': out['v_g_post_mix'], 'v_g_pre_ffn': out['v_g_pre_ffn'], 'v_g_post_ffn': out['v_g_post_ffn'], 'v_w_in': out['v_w_in'], 'v_b_fgate': out['v_b_fgate'], 'v_conv_w': out['v_conv_w'], 'v_conv_b': out['v_conv_b'], 'v_conv_ln_g': out['v_conv_ln_g'], 'v_conv_ln_b': out['v_conv_ln_b'], 'v_conv_pw_w': out['v_conv_pw_w'], 'v_conv_pw_b': out['v_conv_pw_b'], 'v_swa_sinks': out['v_swa_sinks'], 'v_sgu_ln_g': out['v_sgu_ln_g'], 'v_sgu_ln_b': out['v_sgu_ln_b'], 'v_sgu_w': out['v_sgu_w'], 'v_sgu_b': out['v_sgu_b'], 'v_g_group': out['v_g_group'], 'v_w_out': out['v_w_out'], 'v_ffn_w_up': out['v_ffn_w_up'], 'v_ffn_conv_w': out['v_ffn_conv_w'], 'v_ffn_conv_b': out['v_ffn_conv_b'], 'v_ffn_w_down': out['v_ffn_w_down']}


def _loss(weights, diff, rest, loss_target):
    with _jax.named_scope("forward"):
        args = {**rest, TWIN_DIFF_INPUT: diff, **{k: w.astype(_WEIGHT_DTYPES[k]) for k, w in weights.items()}}
        y = _forward(args)
    with _jax.named_scope("loss_head"):
        err = _jnp.square(y.astype(_jnp.float32) - loss_target)
        return 0.5 * _jnp.sum(_jnp.mean(err, axis=-1)) if err.ndim else 0.5 * err


def _adamw(w, g, m, v):
    m = ADAM_B1 * m + (1.0 - ADAM_B1) * g
    v = ADAM_B2 * v + (1.0 - ADAM_B2) * _jnp.square(g)
    m_hat = m / (1.0 - ADAM_B1 ** ADAM_STEP)
    v_hat = v / (1.0 - ADAM_B2 ** ADAM_STEP)
    delta = -ADAM_LR * (m_hat / (_jnp.sqrt(v_hat) + ADAM_EPS) + ADAM_WD * w)
    return delta, m, v


def reference(x, c, w_ada, b_ada, g_pre_mix, g_post_mix, g_pre_ffn, g_post_ffn, w_in, b_fgate, conv_w, conv_b, conv_ln_g, conv_ln_b, conv_pw_w, conv_pw_b, swa_sinks, sgu_ln_g, sgu_ln_b, sgu_w, sgu_b, g_group, w_out, ffn_w_up, ffn_conv_w, ffn_conv_b, ffn_w_down, loss_target, m_w_ada, m_b_ada, m_g_pre_mix, m_g_post_mix, m_g_pre_ffn, m_g_post_ffn, m_w_in, m_b_fgate, m_conv_w, m_conv_b, m_conv_ln_g, m_conv_ln_b, m_conv_pw_w, m_conv_pw_b, m_swa_sinks, m_sgu_ln_g, m_sgu_ln_b, m_sgu_w, m_sgu_b, m_g_group, m_w_out, m_ffn_w_up, m_ffn_conv_w, m_ffn_conv_b, m_ffn_w_down, v_w_ada, v_b_ada, v_g_pre_mix, v_g_post_mix, v_g_pre_ffn, v_g_post_ffn, v_w_in, v_b_fgate, v_conv_w, v_conv_b, v_conv_ln_g, v_conv_ln_b, v_conv_pw_w, v_conv_pw_b, v_swa_sinks, v_sgu_ln_g, v_sgu_ln_b, v_sgu_w, v_sgu_b, v_g_group, v_w_out, v_ffn_w_up, v_ffn_conv_w, v_ffn_conv_b, v_ffn_w_down):
    given = dict(x=x, c=c, w_ada=w_ada, b_ada=b_ada, g_pre_mix=g_pre_mix, g_post_mix=g_post_mix, g_pre_ffn=g_pre_ffn, g_post_ffn=g_post_ffn, w_in=w_in, b_fgate=b_fgate, conv_w=conv_w, conv_b=conv_b, conv_ln_g=conv_ln_g, conv_ln_b=conv_ln_b, conv_pw_w=conv_pw_w, conv_pw_b=conv_pw_b, swa_sinks=swa_sinks, sgu_ln_g=sgu_ln_g, sgu_ln_b=sgu_ln_b, sgu_w=sgu_w, sgu_b=sgu_b, g_group=g_group, w_out=w_out, ffn_w_up=ffn_w_up, ffn_conv_w=ffn_conv_w, ffn_conv_b=ffn_conv_b, ffn_w_down=ffn_w_down, loss_target=loss_target, m_w_ada=m_w_ada, m_b_ada=m_b_ada, m_g_pre_mix=m_g_pre_mix, m_g_post_mix=m_g_post_mix, m_g_pre_ffn=m_g_pre_ffn, m_g_post_ffn=m_g_post_ffn, m_w_in=m_w_in, m_b_fgate=m_b_fgate, m_conv_w=m_conv_w, m_conv_b=m_conv_b, m_conv_ln_g=m_conv_ln_g, m_conv_ln_b=m_conv_ln_b, m_conv_pw_w=m_conv_pw_w, m_conv_pw_b=m_conv_pw_b, m_swa_sinks=m_swa_sinks, m_sgu_ln_g=m_sgu_ln_g, m_sgu_ln_b=m_sgu_ln_b, m_sgu_w=m_sgu_w, m_sgu_b=m_sgu_b, m_g_group=m_g_group, m_w_out=m_w_out, m_ffn_w_up=m_ffn_w_up, m_ffn_conv_w=m_ffn_conv_w, m_ffn_conv_b=m_ffn_conv_b, m_ffn_w_down=m_ffn_w_down, v_w_ada=v_w_ada, v_b_ada=v_b_ada, v_g_pre_mix=v_g_pre_mix, v_g_post_mix=v_g_post_mix, v_g_pre_ffn=v_g_pre_ffn, v_g_post_ffn=v_g_post_ffn, v_w_in=v_w_in, v_b_fgate=v_b_fgate, v_conv_w=v_conv_w, v_conv_b=v_conv_b, v_conv_ln_g=v_conv_ln_g, v_conv_ln_b=v_conv_ln_b, v_conv_pw_w=v_conv_pw_w, v_conv_pw_b=v_conv_pw_b, v_swa_sinks=v_swa_sinks, v_sgu_ln_g=v_sgu_ln_g, v_sgu_ln_b=v_sgu_ln_b, v_sgu_w=v_sgu_w, v_sgu_b=v_sgu_b, v_g_group=v_g_group, v_w_out=v_w_out, v_ffn_w_up=v_ffn_w_up, v_ffn_conv_w=v_ffn_conv_w, v_ffn_conv_b=v_ffn_conv_b, v_ffn_w_down=v_ffn_w_down)
    weights = {n: given[n] for n in TWIN_WEIGHTS}
    shared = {n: given[n] for n in SHARED_INPUTS}
    per_example = {n: given[n] for n in ['x', 'c']}
    grad_fn = _jax.value_and_grad(_loss, argnums=(0, 1))

    def one_microbatch(ex, loss_target):
        ex = dict(ex)
        diff = ex.pop(TWIN_DIFF_INPUT)
        return grad_fn(weights, diff, {**shared, **ex}, loss_target)

    if N_MICROBATCH == 1:
        loss, (grad_w, grad_x) = one_microbatch(per_example, given["loss_target"])
    else:
        def body(carry, xs):
            loss_sum, grad_sum = carry
            l_k, (gw_k, gx_k) = one_microbatch(xs[0], xs[1])
            with _jax.named_scope("update"):
                return (loss_sum + l_k, _jax.tree.map(_jnp.add, grad_sum, gw_k)), gx_k

        init = (_jnp.zeros((), _jnp.float32), _jax.tree.map(_jnp.zeros_like, weights))
        (loss, grad_w), grad_x = _jax.lax.scan(body, init, (per_example, given["loss_target"]))
    with _jax.named_scope("update"):
        delta_w, new_m, new_v = {}, {}, {}
        for n in TWIN_WEIGHTS:
            delta_w[n], new_m[n], new_v[n] = _adamw(weights[n], grad_w[n], given["m_" + n], given["v_" + n])
    return (loss, grad_x, *[grad_w[n] for n in TWIN_WEIGHTS], *[delta_w[n] for n in TWIN_WEIGHTS],
            *[new_m[n] for n in TWIN_WEIGHTS], *[new_v[n] for n in TWIN_WEIGHTS])
```

```python
import functools

import jax
import jax.numpy as jnp
from jax import lax
from jax.experimental import pallas as pl
from jax.experimental.pallas import tpu as pltpu

F32, BF16 = jnp.float32, jnp.bfloat16
SDS = jax.ShapeDtypeStruct
MESH = pl.DeviceIdType.MESH

N_DEV = 8
D = 1024
GW = 256
HD = 64
N_LAYER = 2
ZW = 2432
FF_BLK = 704
FF_NBLK = 4
CONV_K = 31
CONV_HALO = 32
FFN_HALO = 8
EPS = 1e-6
NEG = -1e30
SCALE = HD ** -0.5
VMEM_LIMIT_V7X = 56 * 1024 * 1024
TM = 512
TQ = 256
WIN = 128

ADAM_LR, ADAM_B1, ADAM_B2, ADAM_EPS, ADAM_WD, ADAM_STEP = 0.001, 0.9, 0.999, 1e-08, 0.01, 10

Z_FQ, Z_FK, Z_FV, Z_CA, Z_CG, Z_SQ = 0, 1, 2, 3, 4, 5
Z_SK, Z_SV = 12, 13
Z_GU, Z_GV = 7, 8
Z_FG = 18


def _cp(sem=None):
    return pltpu.CompilerParams(dimension_semantics=sem, vmem_limit_bytes=VMEM_LIMIT_V7X)


def _vec(arr3, idx, ngrid):
    w = arr3.shape[-1]
    if ngrid == 1:
        return pl.BlockSpec((None, 1, w), lambda i: (idx, 0, 0))
    return pl.BlockSpec((None, 1, w), lambda i, j: (idx, 0, 0))


def _sigmoid(x):
    return jax.nn.sigmoid(x)


def _silu(x):
    return x * _sigmoid(x)


def _dsilu(x):
    s = _sigmoid(x)
    return s * (1.0 + x * (1.0 - s))


_G0, _G1 = 0.7978845608028654, 0.044715


def _gelu(x):
    return 0.5 * x * (1.0 + jnp.tanh(_G0 * (x + _G1 * x * x * x)))


def _dgelu(x):
    t = jnp.tanh(_G0 * (x + _G1 * x * x * x))
    return 0.5 * (1.0 + t) + 0.5 * x * (1.0 - t * t) * (_G0 * (1.0 + 3.0 * _G1 * x * x))


def _rstd(x):
    return lax.rsqrt(jnp.mean(x * x, axis=-1, keepdims=True) + EPS)


def _rms_bwd(xh, r, t):
    return r * (t - xh * jnp.mean(t * xh, axis=-1, keepdims=True))


def _ln_stats(x):
    mu = jnp.mean(x, axis=-1, keepdims=True)
    xc = x - mu
    rstd = lax.rsqrt(jnp.mean(xc * xc, axis=-1, keepdims=True) + EPS)
    return xc * rstd, rstd


def _ln_bwd(xh, rstd, dxh):
    return rstd * (dxh - jnp.mean(dxh, axis=-1, keepdims=True) - xh * jnp.mean(dxh * xh, axis=-1, keepdims=True))


def _colsum(x):
    return jnp.sum(x, axis=0, keepdims=True)


def _dot(a, b, kind):
    dn = {"nn": (((1,), (0,)), ((), ())), "nt": (((1,), (1,)), ((), ())), "tn": (((0,), (0,)), ((), ()))}[kind]
    return lax.dot_general(a.astype(BF16), b.astype(BF16), dn, preferred_element_type=F32)


def _exchange(arrs, modes, name):
    n = len(arrs)
    outs = [SDS((N_DEV,) + a.shape, a.dtype) if m == "bcast" else SDS(a.shape, a.dtype) for a, m in zip(arrs, modes)]

    def body(*refs):
        ins, dst = refs[:n], refs[n:2 * n]
        send, recv, loc = refs[2 * n:]
        x, y, c = lax.axis_index("x"), lax.axis_index("y"), lax.axis_index("c")
        me = 4 * x + 2 * y + c

        def src(a, j):
            return ins[a] if modes[a] == "bcast" else ins[a].at[j]

        local = [pltpu.make_async_copy(src(a, me), dst[a].at[me], loc.at[a]) for a in range(n)]
        for cp in local:
            cp.start()
        sent, landed = [], []
        for k in (2, 4, 6, 3, 5, 7, 1):
            px = 1 - x if k & 4 else x
            py = 1 - y if k & 2 else y
            pc = 1 - c if k & 1 else c
            peer = 4 * px + 2 * py + pc
            for a in range(n):
                cp = pltpu.make_async_remote_copy(src_ref=src(a, peer), dst_ref=dst[a].at[me], send_sem=send.at[a, k - 1],
                                                  recv_sem=recv.at[a, k - 1], device_id=(px, py, pc), device_id_type=MESH)
                cp.start()
                sent.append(cp)
                landed.append(pltpu.make_async_remote_copy(src_ref=src(a, peer), dst_ref=dst[a].at[peer],
                                                           send_sem=send.at[a, k - 1], recv_sem=recv.at[a, k - 1],
                                                           device_id=(px, py, pc), device_id_type=MESH))
        for cp in landed:
            cp.wait_recv()
        for cp in sent:
            cp.wait_send()
        for cp in local:
            cp.wait()

    hbm = pl.BlockSpec(memory_space=pltpu.HBM)
    return pl.pallas_call(
        body, name=name, out_shape=outs, in_specs=[hbm] * n, out_specs=[hbm] * n,
        scratch_shapes=[pltpu.SemaphoreType.DMA((n, N_DEV - 1)), pltpu.SemaphoreType.DMA((n, N_DEV - 1)),
                        pltpu.SemaphoreType.DMA((n,))],
        compiler_params=pltpu.CompilerParams(has_side_effects=True),
    )(*arrs)


def _matmul(a, b, kind, out_shape, out_dtype, grid, a_spec, b_spec, o_spec, acc_shape, name):
    nk = grid[2]

    def body(a_ref, b_ref, o_ref, *scratch):
        prod = _dot(a_ref[...], b_ref[...], kind)
        if nk == 1:
            o_ref[...] = prod.astype(out_dtype)
        else:
            acc = scratch[0]
            k = pl.program_id(2)

            @pl.when(k == 0)
            def _():
                acc[...] = prod

            @pl.when(k > 0)
            def _():
                acc[...] += prod

            @pl.when(k == nk - 1)
            def _():
                o_ref[...] = acc[...].astype(out_dtype)

    return pl.pallas_call(
        body, name=name, grid=grid, in_specs=[a_spec, b_spec], out_specs=o_spec, out_shape=SDS(out_shape, out_dtype),
        scratch_shapes=[] if nk == 1 else [pltpu.VMEM(acc_shape, F32)],
        compiler_params=_cp(("parallel", "parallel", "arbitrary")))(a, b)


def _bs(shape, fn):
    return pl.BlockSpec(shape, fn)


def _mm_rows(a, w, kind, n_out, out_dtype, name):
    s, k = a.shape
    tm = min(TM, s)
    return _matmul(a, w, kind, (s, n_out), out_dtype, (s // tm, 1, 1),
                   _bs((tm, k), lambda i, j, kk: (i, 0)), _bs(w.shape, lambda i, j, kk: (0, 0)),
                   _bs((tm, n_out), lambda i, j, kk: (i, 0)), None, name)


def _mm_wgrad(a, dy, out_dtype, name):
    s, k = a.shape
    n = dy.shape[1]
    tk = min(TM, s)
    return _matmul(a, dy, "tn", (k, n), out_dtype, (1, 1, s // tk),
                   _bs((tk, k), lambda i, j, kk: (kk, 0)), _bs((tk, n), lambda i, j, kk: (kk, 0)),
                   _bs((k, n), lambda i, j, kk: (0, 0)), (k, n), name)


def _ada_fwd(c_all, w_ada):
    ncol = w_ada.shape[2]

    def body(c_ref, w_ref, o_ref):
        ca = _silu(c_ref[...])
        ca = jnp.concatenate([ca, jnp.zeros_like(ca)], axis=0)
        o_ref[...] = _dot(ca, w_ref[...], "nn")[:N_DEV, :]

    return pl.pallas_call(
        body, name="ada_fwd", grid=(N_LAYER,),
        in_specs=[pl.BlockSpec((N_DEV, D), lambda l: (0, 0)), pl.BlockSpec((None, D, ncol), lambda l: (l, 0, 0))],
        out_specs=pl.BlockSpec((None, N_DEV, ncol), lambda l: (l, 0, 0)),
        out_shape=SDS((N_LAYER, N_DEV, ncol), F32), compiler_params=_cp(("parallel",)))(c_all, w_ada)


def _ada_finish(m_mine, b_ada):
    def body(m_ref, b_ref, o_ref):
        o_ref[...] = m_ref[...] + b_ref[...]

    return pl.pallas_call(body, name="ada_finish", out_shape=SDS(b_ada.shape, F32))(m_mine, b_ada)


def _ada_bwd(c_all, dmod_cols):
    ncol = dmod_cols.shape[2]

    def body(c_ref, d_ref, o_ref):
        ca = _silu(c_ref[...])
        ca = jnp.concatenate([ca, jnp.zeros_like(ca)], axis=0)
        dm = d_ref[...]
        dm = jnp.concatenate([dm, jnp.zeros_like(dm)], axis=0)
        o_ref[...] = _dot(ca, dm, "tn")

    return pl.pallas_call(
        body, name="ada_bwd", grid=(N_LAYER,),
        in_specs=[pl.BlockSpec((N_DEV, D), lambda l: (0, 0)), pl.BlockSpec((None, N_DEV, ncol), lambda l: (l, 0, 0))],
        out_specs=pl.BlockSpec((None, D, ncol), lambda l: (l, 0, 0)),
        out_shape=SDS((N_LAYER, D, ncol), F32), compiler_params=_cp(("parallel",)))(c_all, dmod_cols)


def _rows(s):
    tm = min(TM, s)
    return tm, pl.BlockSpec((tm, D), lambda i: (i, 0))


def _rms_mod(x, g, sc, sh, name):
    s = x.shape[0]
    tm, row = _rows(s)

    def body(x_ref, g_ref, sc_ref, sh_ref, h_ref):
        xf = x_ref[...]
        h_ref[...] = (xf * _rstd(xf) * (g_ref[...] * (1.0 + sc_ref[...])) + sh_ref[...]).astype(BF16)

    return pl.pallas_call(
        body, name=name, grid=(s // tm,), in_specs=[row, _vec(*g, 1), _vec(*sc, 1), _vec(*sh, 1)], out_specs=row,
        out_shape=SDS((s, D), BF16), compiler_params=_cp(("parallel",)))(x, g[0], sc[0], sh[0])


def _post(xres, o, ga, gpost, gn, scn, shn, name):
    s = xres.shape[0]
    tm, row = _rows(s)

    def body(x_ref, o_ref, ga_ref, gp_ref, gn_ref, sc_ref, sh_ref, xn_ref, h_ref):
        of = o_ref[...]
        xn = x_ref[...] + ga_ref[...] * (of * _rstd(of) * gp_ref[...])
        xn_ref[...] = xn
        h_ref[...] = (xn * _rstd(xn) * (gn_ref[...] * (1.0 + sc_ref[...])) + sh_ref[...]).astype(BF16)

    return pl.pallas_call(
        body, name=name, grid=(s // tm,),
        in_specs=[row, row, _vec(*ga, 1), _vec(*gpost, 1), _vec(*gn, 1), _vec(*scn, 1), _vec(*shn, 1)],
        out_specs=[row, row], out_shape=[SDS((s, D), F32), SDS((s, D), BF16)],
        compiler_params=_cp(("parallel",)))(xres, o, ga[0], gpost[0], gn[0], scn[0], shn[0])


def _post_loss(xres, o, ga, gpost, target, name):
    s = xres.shape[0]
    tm, row = _rows(s)

    def body(x_ref, o_ref, ga_ref, gp_ref, t_ref, dy_ref, loss_ref):
        of = o_ref[...]
        err = x_ref[...] + ga_ref[...] * (of * _rstd(of) * gp_ref[...]) - t_ref[...]
        dy_ref[...] = err * (1.0 / D)

        @pl.when(pl.program_id(0) == 0)
        def _():
            loss_ref[...] = jnp.zeros_like(loss_ref)

        loss_ref[...] += jnp.sum(jnp.mean(err * err, axis=-1, keepdims=True), axis=0, keepdims=True) * 0.5

    return pl.pallas_call(
        body, name=name, grid=(s // tm,), in_specs=[row, row, _vec(*ga, 1), _vec(*gpost, 1), row],
        out_specs=[row, pl.BlockSpec((8, 128), lambda i: (0, 0))], out_shape=[SDS((s, D), F32), SDS((8, 128), F32)],
        compiler_params=_cp(("arbitrary",)))(xres, o, ga[0], gpost[0], target)


def _acc(ref, val, first):
    @pl.when(first)
    def _():
        ref[...] = val

    @pl.when(jnp.logical_not(first))
    def _():
        ref[...] += val


def _post_bwd(dxn, o, ga, gpost, name):
    s = dxn.shape[0]
    tm, row = _rows(s)
    vec = pl.BlockSpec((1, D), lambda i: (0, 0))

    def body(d_ref, o_ref, ga_ref, gp_ref, do_ref, dga_ref, dgp_ref):
        of, dx = o_ref[...], d_ref[...]
        r = _rstd(of)
        oh = of * r
        do_ref[...] = _rms_bwd(oh, r, dx * (ga_ref[...] * gp_ref[...])).astype(BF16)
        cs = _colsum(dx * oh)
        first = pl.program_id(0) == 0
        _acc(dga_ref, cs * gp_ref[...], first)
        _acc(dgp_ref, cs * ga_ref[...], first)

    return pl.pallas_call(
        body, name=name, grid=(s // tm,), in_specs=[row, row, _vec(*ga, 1), _vec(*gpost, 1)], out_specs=[row, vec, vec],
        out_shape=[SDS((s, D), BF16), SDS((1, D), F32), SDS((1, D), F32)],
        compiler_params=_cp(("arbitrary",)))(dxn, o, ga[0], gpost[0])


def _pre_bwd(dh, x, dres, g, sc, name):
    s = x.shape[0]
    tm, row = _rows(s)
    vec = pl.BlockSpec((1, D), lambda i: (0, 0))

    def body(dh_ref, x_ref, dr_ref, g_ref, sc_ref, dx_ref, dsh_ref, dsc_ref, dg_ref):
        xf, d = x_ref[...], dh_ref[...]
        r = _rstd(xf)
        xh = xf * r
        dx_ref[...] = dr_ref[...] + _rms_bwd(xh, r, d * (g_ref[...] * (1.0 + sc_ref[...])))
        cs = _colsum(d * xh)
        first = pl.program_id(0) == 0
        _acc(dsh_ref, _colsum(d), first)
        _acc(dsc_ref, cs * g_ref[...], first)
        _acc(dg_ref, cs * (1.0 + sc_ref[...]), first)

    return pl.pallas_call(
        body, name=name, grid=(s // tm,), in_specs=[row, row, row, _vec(*g, 1), _vec(*sc, 1)],
        out_specs=[row, vec, vec, vec],
        out_shape=[SDS((s, D), F32), SDS((1, D), F32), SDS((1, D), F32), SDS((1, D), F32)],
        compiler_params=_cp(("arbitrary",)))(dh, x, dres, g[0], sc[0])


def _gnorm(ys, gg, name):
    s = ys[0].shape[0]
    tm = min(TM, s)
    yb = pl.BlockSpec((tm, GW), lambda i: (i, 0))

    def body(y0, y1, y2, y3, g_ref, o_ref):
        for i, yr in enumerate((y0, y1, y2, y3)):
            y = yr[...]
            o_ref[:, GW * i:GW * (i + 1)] = (y * _rstd(y) * g_ref[:, GW * i:GW * (i + 1)]).astype(BF16)

    return pl.pallas_call(
        body, name=name, grid=(s // tm,), in_specs=[yb] * 4 + [_vec(*gg, 1)], out_specs=pl.BlockSpec((tm, D), lambda i: (i, 0)),
        out_shape=SDS((s, D), BF16), compiler_params=_cp(("parallel",)))(*ys, gg[0])


def _gnorm_bwd(dyn, ys, gg, name):
    s = ys[0].shape[0]
    tm = min(TM, s)
    yb = pl.BlockSpec((tm, GW), lambda i: (i, 0))

    def body(d_ref, y0, y1, y2, y3, g_ref, o0, o1, o2, o3, dg_ref):
        first = pl.program_id(0) == 0
        for i, (yr, orf) in enumerate(zip((y0, y1, y2, y3), (o0, o1, o2, o3))):
            y = yr[...]
            d = d_ref[:, GW * i:GW * (i + 1)]
            r = _rstd(y)
            yh = y * r
            orf[...] = _rms_bwd(yh, r, d * g_ref[:, GW * i:GW * (i + 1)])
            cs = _colsum(d * yh)

            @pl.when(first)
            def _():
                dg_ref[:, GW * i:GW * (i + 1)] = cs

            @pl.when(jnp.logical_not(first))
            def _():
                dg_ref[:, GW * i:GW * (i + 1)] += cs

    return pl.pallas_call(
        body, name=name, grid=(s // tm,), in_specs=[pl.BlockSpec((tm, D), lambda i: (i, 0))] + [yb] * 4 + [_vec(*gg, 1)],
        out_specs=[yb] * 4 + [pl.BlockSpec((1, D), lambda i: (0, 0))],
        out_shape=[SDS((s, GW), F32)] * 4 + [SDS((1, D), F32)], compiler_params=_cp(("arbitrary",)))(dyn, *ys, gg[0])


def _lane_put(acc, col, h):
    lane = lax.broadcasted_iota(jnp.int32, acc.shape, 1)
    return jnp.where(lane == h, col, acc)


def _fgate(z, bf, name):
    s = z.shape[0]

    def body(z_ref, b_ref, fc_ref, fr_ref):
        xg = z_ref[...] + b_ref[...]
        lf = jnp.minimum(xg, 0.0) - jnp.log(1.0 + jnp.exp(-jnp.abs(xg)))
        lane = lax.broadcasted_iota(jnp.int32, lf.shape, 1)
        row = lax.broadcasted_iota(jnp.int32, lf.shape, 0)
        f = jnp.where(lane < 4, lf, 0.0)
        sh = 1
        while sh < s:
            f = f + jnp.where(row >= sh, pltpu.roll(f, sh, 0), 0.0)
            sh *= 2
        fc_ref[...] = f
        fr_ref[...] = f.T

    return pl.pallas_call(
        body, name=name, grid=(1,),
        in_specs=[pl.BlockSpec((s, 128), lambda i: (0, Z_FG)), pl.BlockSpec((1, 128), lambda i: (0, 0))],
        out_specs=[pl.BlockSpec((s, 128), lambda i: (0, 0)), pl.BlockSpec((128, s), lambda i: (0, 0))],
        out_shape=[SDS((s, 128), F32), SDS((128, s), F32)], compiler_params=_cp(("arbitrary",)))(z, bf)


def _fgate_bwd(z, bf, dfrow, dfcol, name):
    s = z.shape[0]

    def body(z_ref, b_ref, d_ref, dc_ref, dz_ref, db_ref):
        d = jnp.concatenate([d_ref[...], jnp.zeros((120, s), F32)], axis=0).T + dc_ref[...]
        row = lax.broadcasted_iota(jnp.int32, d.shape, 0)
        lane = lax.broadcasted_iota(jnp.int32, d.shape, 1)
        sh = 1
        while sh < s:
            d = d + jnp.where(row < s - sh, pltpu.roll(d, s - sh, 0), 0.0)
            sh *= 2
        xg = z_ref[...] + b_ref[...]
        dz = jnp.where(lane < 4, d * _sigmoid(-xg), 0.0)
        dz_ref[...] = dz.astype(BF16)
        db_ref[...] = _colsum(dz)

    return pl.pallas_call(
        body, name=name, grid=(1,),
        in_specs=[pl.BlockSpec((s, 128), lambda i: (0, Z_FG)), pl.BlockSpec((1, 128), lambda i: (0, 0)),
                  pl.BlockSpec((8, s), lambda i: (0, 0)), pl.BlockSpec((s, 128), lambda i: (0, 0))],
        out_specs=[pl.BlockSpec((s, 128), lambda i: (0, 0)), pl.BlockSpec((1, 128), lambda i: (0, 0))],
        out_shape=[SDS((s, 128), BF16), SDS((1, 128), F32)], compiler_params=_cp(("arbitrary",)))(z, bf, dfrow, dfcol)


def _fox_scores(q_ref, k_ref, fc_ref, fr_ref, h, i, tq, s):
    q = q_ref[:, HD * h:HD * (h + 1)] * SCALE
    sc = _dot(q, k_ref[:, HD * h:HD * (h + 1)], "nt")
    sc = sc + fc_ref[:, h:h + 1] - fr_ref[h:h + 1, :]
    qpos = i * tq + lax.broadcasted_iota(jnp.int32, (tq, s), 0)
    kpos = lax.broadcasted_iota(jnp.int32, (tq, s), 1)
    return q, jnp.where(kpos <= qpos, sc, NEG)


def _fox_fwd(z, fcol, frow, name):
    s = z.shape[0]
    tq = min(TQ, s)

    def body(q_ref, k_ref, v_ref, fc_ref, fr_ref, y_ref, l_ref):
        i = pl.program_id(0)
        lse = jnp.zeros((tq, 128), F32)
        for h in range(4):
            _, sc = _fox_scores(q_ref, k_ref, fc_ref, fr_ref, h, i, tq, s)
            m = jnp.max(sc, axis=-1, keepdims=True)
            p = jnp.exp(sc - m)
            l = jnp.sum(p, axis=-1, keepdims=True)
            y_ref[:, HD * h:HD * (h + 1)] = _dot(p, v_ref[:, HD * h:HD * (h + 1)], "nn") / l
            lse = _lane_put(lse, m + jnp.log(l), h)
        l_ref[...] = lse

    return pl.pallas_call(
        body, name=name, grid=(s // tq,),
        in_specs=[pl.BlockSpec((tq, GW), lambda i: (i, Z_FQ)), pl.BlockSpec((s, GW), lambda i: (0, Z_FK)),
                  pl.BlockSpec((s, GW), lambda i: (0, Z_FV)), pl.BlockSpec((tq, 128), lambda i: (i, 0)),
                  pl.BlockSpec((8, s), lambda i: (0, 0))],
        out_specs=[pl.BlockSpec((tq, GW), lambda i: (i, 0)), pl.BlockSpec((tq, 128), lambda i: (i, 0))],
        out_shape=[SDS((s, GW), F32), SDS((s, 128), F32)], compiler_params=_cp(("parallel",)))(z, z, z, fcol, frow)


def _fox_bwd(z, fcol, frow, lse, y, dy, name):
    s = z.shape[0]
    tq = min(TQ, s)

    def body(q_ref, k_ref, v_ref, fc_ref, fr_ref, l_ref, y_ref, dy_ref, dq_ref, dk_ref, dv_ref, df_ref, dfq_ref):
        i = pl.program_id(0)

        @pl.when(i == 0)
        def _():
            dk_ref[...] = jnp.zeros_like(dk_ref)
            dv_ref[...] = jnp.zeros_like(dv_ref)
            df_ref[...] = jnp.zeros_like(df_ref)

        dfq = jnp.zeros((tq, 128), F32)
        for h in range(4):
            hs = slice(HD * h, HD * (h + 1))
            q, sc = _fox_scores(q_ref, k_ref, fc_ref, fr_ref, h, i, tq, s)
            p = jnp.exp(sc - l_ref[:, h:h + 1])
            dyh = dy_ref[:, hs]
            dd = jnp.sum(dyh * y_ref[:, hs], axis=-1, keepdims=True)
            ds = p * (_dot(dyh, v_ref[:, hs], "nt") - dd)
            dq_ref[:, hs] = _dot(ds, k_ref[:, hs], "nn") * SCALE
            dk_ref[:, hs] += _dot(ds, q, "tn")
            dv_ref[:, hs] += _dot(p, dyh, "tn")
            df_ref[h:h + 1, :] -= _colsum(ds)
            dfq = _lane_put(dfq, jnp.sum(ds, axis=-1, keepdims=True), h)
        dfq_ref[...] = dfq

    tile = lambda w: pl.BlockSpec((tq, w), lambda i: (i, 0))
    full = pl.BlockSpec((s, GW), lambda i: (0, 0))
    return pl.pallas_call(
        body, name=name, grid=(s // tq,),
        in_specs=[pl.BlockSpec((tq, GW), lambda i: (i, Z_FQ)), pl.BlockSpec((s, GW), lambda i: (0, Z_FK)),
                  pl.BlockSpec((s, GW), lambda i: (0, Z_FV)), tile(128), pl.BlockSpec((8, s), lambda i: (0, 0)),
                  tile(128), tile(GW), tile(GW)],
        out_specs=[tile(GW), full, full, pl.BlockSpec((8, s), lambda i: (0, 0)), tile(128)],
        out_shape=[SDS((s, GW), F32), SDS((s, GW), F32), SDS((s, GW), F32), SDS((8, s), F32), SDS((s, 128), F32)],
        compiler_params=_cp(("arbitrary",)))(z, z, z, fcol, frow, lse, y, dy)


def _swa_block(q_ref, k_ref, v_ref, n):
    qs = pl.multiple_of(n * WIN, WIN)
    ks = pl.multiple_of(jnp.maximum(n - 1, 0) * WIN, WIN)
    qb = q_ref[pl.ds(qs, WIN), :]
    kb = k_ref[pl.ds(ks, 2 * WIN), :]
    vb = v_ref[pl.ds(ks, 2 * WIN), :]
    dist = (qs + lax.broadcasted_iota(jnp.int32, (WIN, 2 * WIN), 0)) - (ks + lax.broadcasted_iota(jnp.int32, (WIN, 2 * WIN), 1))
    return qs, ks, qb, kb, vb, (dist >= 0) & (dist < WIN)


def _swa_fwd(z, sinks, name):
    s = z.shape[0]

    def body(sink_ref, q_ref, k_ref, v_ref, y_ref, l_ref):
        def step(n, carry):
            qs, ks, qb, kb, vb, valid = _swa_block(q_ref, k_ref, v_ref, n)
            lse = jnp.zeros((WIN, 128), F32)
            for h in range(4):
                kv = slice(HD * (h // 2), HD * (h // 2 + 1))
                sc = jnp.where(valid, _dot(qb[:, HD * h:HD * (h + 1)] * SCALE, kb[:, kv], "nt"), NEG)
                sink = sink_ref[h]
                m = jnp.maximum(jnp.max(sc, axis=-1, keepdims=True), sink)
                p = jnp.exp(sc - m)
                den = jnp.sum(p, axis=-1, keepdims=True) + jnp.exp(sink - m)
                y_ref[pl.ds(qs, WIN), HD * h:HD * (h + 1)] = _dot(p, vb[:, kv], "nn") / den
                lse = _lane_put(lse, m + jnp.log(den), h)
            l_ref[pl.ds(qs, WIN), :] = lse
            return carry

        lax.fori_loop(0, s // WIN, step, 0)

    return pl.pallas_call(
        body, name=name, grid=(1,),
        in_specs=[pl.BlockSpec(memory_space=pltpu.SMEM), pl.BlockSpec((s, GW), lambda i: (0, Z_SQ)),
                  pl.BlockSpec((s, 128), lambda i: (0, Z_SK)), pl.BlockSpec((s, 128), lambda i: (0, Z_SV))],
        out_specs=[pl.BlockSpec((s, GW), lambda i: (0, 0)), pl.BlockSpec((s, 128), lambda i: (0, 0))],
        out_shape=[SDS((s, GW), F32), SDS((s, 128), F32)], compiler_params=_cp(("arbitrary",)))(sinks, z, z, z)


def _swa_bwd(z, sinks, lse, y, dy, name):
    s = z.shape[0]

    def body(sink_ref, q_ref, k_ref, v_ref, l_ref, y_ref, dy_ref, dq_ref, dk_ref, dv_ref, dsink_ref):
        dk_ref[...] = jnp.zeros_like(dk_ref)
        dv_ref[...] = jnp.zeros_like(dv_ref)
        dsink_ref[...] = jnp.zeros_like(dsink_ref)

        def step(n, carry):
            qs, ks, qb, kb, vb, valid = _swa_block(q_ref, k_ref, v_ref, n)
            lse_b = l_ref[pl.ds(qs, WIN), :]
            yb = y_ref[pl.ds(qs, WIN), :]
            dyb = dy_ref[pl.ds(qs, WIN), :]
            dsink = jnp.zeros((1, 128), F32)
            for h in range(4):
                hs = slice(HD * h, HD * (h + 1))
                kv = slice(HD * (h // 2), HD * (h // 2 + 1))
                q = qb[:, hs] * SCALE
                sc = jnp.where(valid, _dot(q, kb[:, kv], "nt"), NEG)
                lh = lse_b[:, h:h + 1]
                p = jnp.exp(sc - lh)
                dd = jnp.sum(dyb[:, hs] * yb[:, hs], axis=-1, keepdims=True)
                ds = p * (_dot(dyb[:, hs], vb[:, kv], "nt") - dd)
                dq_ref[pl.ds(qs, WIN), hs] = _dot(ds, kb[:, kv], "nn") * SCALE
                dk_ref[pl.ds(ks, 2 * WIN), kv] += _dot(ds, q, "tn")
                dv_ref[pl.ds(ks, 2 * WIN), kv] += _dot(p, dyb[:, hs], "tn")
                dsink = _lane_put(dsink, dsink[:, h:h + 1] - jnp.sum(jnp.exp(sink_ref[h] - lh) * dd, axis=0, keepdims=True), h)
            dsink_ref[...] += dsink
            return carry

        lax.fori_loop(0, s // WIN, step, 0)

    full = lambda w: pl.BlockSpec((s, w), lambda i: (0, 0))
    return pl.pallas_call(
        body, name=name, grid=(1,),
        in_specs=[pl.BlockSpec(memory_space=pltpu.SMEM), pl.BlockSpec((s, GW), lambda i: (0, Z_SQ)),
                  pl.BlockSpec((s, 128), lambda i: (0, Z_SK)), pl.BlockSpec((s, 128), lambda i: (0, Z_SV)),
                  full(128), full(GW), full(GW)],
        out_specs=[full(GW), full(128), full(128), pl.BlockSpec((1, 128), lambda i: (0, 0))],
        out_shape=[SDS((s, GW), F32), SDS((s, 128), F32), SDS((s, 128), F32), SDS((1, 128), F32)],
        compiler_params=_cp(("arbitrary",)))(sinks, z, z, z, lse, y, dy)


def _delayed(win, shift, halo):
    return win[halo:, :] if shift == 0 else pltpu.roll(win, shift, 0)[halo:, :]


def _prev_halo(width, halo, tm, col):
    return pl.BlockSpec((halo, width), lambda i: (jnp.maximum(i * (tm // halo) - 1, 0), col))


def _glu_window(a_ref, g_ref, ah_ref, gh_ref):
    keep = (pl.program_id(0) > 0).astype(F32)
    a = jnp.concatenate([ah_ref[...] * keep, a_ref[...]], axis=0)
    g = jnp.concatenate([gh_ref[...], g_ref[...]], axis=0)
    return a * _sigmoid(g)


def _conv_fwd(z, cw, cb, lg, lb, pw, pb, name):
    s = z.shape[0]
    tm = min(TM, s)

    def body(a_ref, g_ref, ah_ref, gh_ref, w_ref, b_ref, lg_ref, lb_ref, pw_ref, pb_ref, y_ref, hc_ref):
        hg = _glu_window(a_ref, g_ref, ah_ref, gh_ref)
        hc = jnp.zeros((tm, GW), F32) + b_ref[...]
        for k in range(CONV_K):
            hc = hc + w_ref[k:k + 1, :] * _delayed(hg, CONV_K - 1 - k, CONV_HALO)
        hc_ref[...] = hc
        xh, _ = _ln_stats(hc)
        y_ref[...] = _dot(_silu(xh * lg_ref[...] + lb_ref[...]), pw_ref[...], "nn") + pb_ref[...]

    tile = lambda col: pl.BlockSpec((tm, GW), lambda i: (i, col))
    whole = lambda a: pl.BlockSpec(a.shape, lambda i: (0, 0))
    return pl.pallas_call(
        body, name=name, grid=(s // tm,),
        in_specs=[tile(Z_CA), tile(Z_CG), _prev_halo(GW, CONV_HALO, tm, Z_CA), _prev_halo(GW, CONV_HALO, tm, Z_CG),
                  whole(cw), whole(cb), whole(lg), whole(lb), whole(pw), whole(pb)],
        out_specs=[tile(0), tile(0)], out_shape=[SDS((s, GW), F32), SDS((s, GW), F32)],
        compiler_params=_cp(("parallel",)))(z, z, z, z, cw, cb, lg, lb, pw, pb)


def _conv_bwd_a(z, hc, dy, cw, lg, lb, pw, name):
    s = z.shape[0]
    tm = min(TM, s)

    def body(a_ref, g_ref, ah_ref, gh_ref, hc_ref, dy_ref, lg_ref, lb_ref, pw_ref,
             dhc_ref, dpw_ref, dpb_ref, dlg_ref, dlb_ref, dcw_ref, dcb_ref):
        first = pl.program_id(0) == 0
        dy = dy_ref[...]
        xh, rstd = _ln_stats(hc_ref[...])
        hn = xh * lg_ref[...] + lb_ref[...]
        dhn = _dot(dy, pw_ref[...], "nt") * _dsilu(hn)
        dhc = _ln_bwd(xh, rstd, dhn * lg_ref[...])
        dhc_ref[...] = dhc
        _acc(dpw_ref, _dot(_silu(hn), dy, "tn"), first)
        _acc(dpb_ref, _colsum(dy), first)
        _acc(dlg_ref, _colsum(dhn * xh), first)
        _acc(dlb_ref, _colsum(dhn), first)
        _acc(dcb_ref, _colsum(dhc), first)
        hg = _glu_window(a_ref, g_ref, ah_ref, gh_ref)

        @pl.when(first)
        def _():
            dcw_ref[...] = jnp.zeros_like(dcw_ref)

        for k in range(CONV_K):
            dcw_ref[k:k + 1, :] += _colsum(dhc * _delayed(hg, CONV_K - 1 - k, CONV_HALO))

    tile = lambda col: pl.BlockSpec((tm, GW), lambda i: (i, col))
    whole = lambda shape: pl.BlockSpec(shape, lambda i: (0, 0))
    return pl.pallas_call(
        body, name=name, grid=(s // tm,),
        in_specs=[tile(Z_CA), tile(Z_CG), _prev_halo(GW, CONV_HALO, tm, Z_CA), _prev_halo(GW, CONV_HALO, tm, Z_CG),
                  tile(0), tile(0), whole(lg.shape), whole(lb.shape), whole(pw.shape)],
        out_specs=[tile(0), whole((GW, GW)), whole((1, GW)), whole((1, GW)), whole((1, GW)), whole((32, GW)), whole((1, GW))],
        out_shape=[SDS((s, GW), F32), SDS((GW, GW), F32), SDS((1, GW), F32), SDS((1, GW), F32), SDS((1, GW), F32),
                   SDS((32, GW), F32), SDS((1, GW), F32)],
        compiler_params=_cp(("arbitrary",)))(z, z, z, z, hc, dy, lg, lb, pw)


def _conv_bwd_b(z, dhc, cw, name):
    s = z.shape[0]
    tm = min(TM, s)
    nt = s // tm

    def body(a_ref, g_ref, d_ref, dn_ref, w_ref, da_ref, dg_ref):
        keep = (pl.program_id(0) < nt - 1).astype(F32)
        win = jnp.concatenate([d_ref[...], dn_ref[...] * keep], axis=0)
        dhg = jnp.zeros((tm, GW), F32)
        for k in range(CONV_K):
            sh = CONV_K - 1 - k
            dhg = dhg + w_ref[k:k + 1, :] * (win[:tm, :] if sh == 0 else pltpu.roll(win, tm + CONV_HALO - sh, 0)[:tm, :])
        sg = _sigmoid(g_ref[...])
        da_ref[...] = (dhg * sg).astype(BF16)
        dg_ref[...] = (dhg * a_ref[...] * sg * (1.0 - sg)).astype(BF16)

    tile = lambda col: pl.BlockSpec((tm, GW), lambda i: (i, col))
    nxt = pl.BlockSpec((CONV_HALO, GW), lambda i: (jnp.minimum((i + 1) * (tm // CONV_HALO), s // CONV_HALO - 1), 0))
    return pl.pallas_call(
        body, name=name, grid=(nt,),
        in_specs=[tile(Z_CA), tile(Z_CG), tile(0), nxt, pl.BlockSpec(cw.shape, lambda i: (0, 0))],
        out_specs=[tile(0), tile(0)], out_shape=[SDS((s, GW), BF16), SDS((s, GW), BF16)],
        compiler_params=_cp(("parallel",)))(z, z, dhc, dhc, cw)


def _sgu_chunk(zu, zv, lg, lb, wcat, bfull):
    u, v = _gelu(zu), _gelu(zv)
    xh, rstd = _ln_stats(v)
    vn = xh * lg + lb
    lane = lax.shift_right_logical(lax.broadcasted_iota(jnp.int32, (WIN, GW), 1), 6)
    r = jnp.concatenate([jnp.where(lane == g, vn, 0.0) for g in range(4)], axis=0)
    mix = _dot(wcat, r, "nn") + bfull
    return u, xh, rstd, r, mix, lane


def _tril4(w):
    t = lax.broadcasted_iota(jnp.int32, w.shape, 0)
    sidx = lax.broadcasted_iota(jnp.int32, w.shape, 1) & (WIN - 1)
    return jnp.where(sidx <= t, w, 0.0)


def _sgu_fwd(z, lg, lb, wcat, bfull, name):
    s = z.shape[0]
    tm = min(TM, s)

    def body(u_ref, v_ref, lg_ref, lb_ref, w_ref, b_ref, y_ref):
        w = _tril4(w_ref[...])
        for n in range(tm // WIN):
            rows = slice(WIN * n, WIN * (n + 1))
            u, _, _, _, mix, _ = _sgu_chunk(u_ref[rows, :], v_ref[rows, :], lg_ref[...], lb_ref[...], w, b_ref[...])
            y_ref[rows, :] = u * mix

    tile = lambda col: pl.BlockSpec((tm, GW), lambda i: (i, col))
    whole = lambda a: pl.BlockSpec(a.shape, lambda i: (0, 0))
    return pl.pallas_call(
        body, name=name, grid=(s // tm,), in_specs=[tile(Z_GU), tile(Z_GV), whole(lg), whole(lb), whole(wcat), whole(bfull)],
        out_specs=tile(0), out_shape=SDS((s, GW), F32), compiler_params=_cp(("parallel",)))(z, z, lg, lb, wcat, bfull)


def _sgu_bwd(z, dy, lg, lb, wcat, bfull, name):
    s = z.shape[0]
    tm = min(TM, s)

    def body(u_ref, v_ref, dy_ref, lg_ref, lb_ref, w_ref, b_ref, du_ref, dv_ref, dw_ref, db_ref, dlg_ref, dlb_ref):
        first = pl.program_id(0) == 0
        w = _tril4(w_ref[...])
        wt = w.T
        dw = jnp.zeros((WIN, 4 * WIN), F32)
        db = jnp.zeros((WIN, 128), F32)
        dlg = jnp.zeros((1, GW), F32)
        dlb = jnp.zeros((1, GW), F32)
        for n in range(tm // WIN):
            rows = slice(WIN * n, WIN * (n + 1))
            zu, zv, dout = u_ref[rows, :], v_ref[rows, :], dy_ref[rows, :]
            u, xh, rstd, r, mix, lane = _sgu_chunk(zu, zv, lg_ref[...], lb_ref[...], w, b_ref[...])
            dmix = dout * u
            du_ref[rows, :] = (dout * mix * _dgelu(zu)).astype(BF16)
            dw = dw + _dot(dmix, r, "nt")
            for g in range(4):
                db = _lane_put(db, db[:, g:g + 1] + jnp.sum(dmix[:, HD * g:HD * (g + 1)], axis=1, keepdims=True), g)
            dr = _dot(wt, dmix, "nn")
            dvn = jnp.zeros((WIN, GW), F32)
            for g in range(4):
                dvn = dvn + jnp.where(lane == g, dr[WIN * g:WIN * (g + 1), :], 0.0)
            dlg = dlg + _colsum(dvn * xh)
            dlb = dlb + _colsum(dvn)
            dv_ref[rows, :] = (_ln_bwd(xh, rstd, dvn * lg_ref[...]) * _dgelu(zv)).astype(BF16)
        _acc(dw_ref, _tril4(dw), first)
        _acc(db_ref, db, first)
        _acc(dlg_ref, dlg, first)
        _acc(dlb_ref, dlb, first)

    tile = lambda col: pl.BlockSpec((tm, GW), lambda i: (i, col))
    whole = lambda shape: pl.BlockSpec(shape, lambda i: (0, 0))
    return pl.pallas_call(
        body, name=name, grid=(s // tm,),
        in_specs=[tile(Z_GU), tile(Z_GV), tile(0), whole(lg.shape), whole(lb.shape), whole(wcat.shape), whole(bfull.shape)],
        out_specs=[tile(0), tile(0), whole((WIN, 4 * WIN)), whole((WIN, 128)), whole((1, GW)), whole((1, GW))],
        out_shape=[SDS((s, GW), BF16), SDS((s, GW), BF16), SDS((WIN, 4 * WIN), F32), SDS((WIN, 128), F32),
                   SDS((1, GW), F32), SDS((1, GW), F32)],
        compiler_params=_cp(("arbitrary",)))(z, z, dy, lg, lb, wcat, bfull)


def _conv3(win, w, b, tm):
    return (w[2:3, :] * win[FFN_HALO:, :] + w[1:2, :] * pltpu.roll(win, 1, 0)[FFN_HALO:, :]
            + w[0:1, :] * pltpu.roll(win, 2, 0)[FFN_HALO:, :] + b)


def _ffn_specs(s, tm):
    main = pl.BlockSpec((2, None, tm, FF_BLK), lambda j, i: (0, j, i, 0))
    prev = pl.BlockSpec((2, None, FFN_HALO, FF_BLK), lambda j, i: (0, j, jnp.maximum(i * (tm // FFN_HALO) - 1, 0), 0))
    nxt = pl.BlockSpec((2, None, FFN_HALO, FF_BLK),
                       lambda j, i: (0, j, jnp.minimum((i + 1) * (tm // FFN_HALO), s // FFN_HALO - 1), 0))
    wsp = pl.BlockSpec((2, None, 3, FF_BLK), lambda j, i: (0, j, 0, 0))
    bsp = pl.BlockSpec((2, None, 1, FF_BLK), lambda j, i: (0, j, 0, 0))
    return main, prev, nxt, wsp, bsp


def _ffn_windows(u_ref, uh_ref):
    keep = (pl.program_id(1) > 0).astype(F32)
    return [jnp.concatenate([uh_ref[p] * keep, u_ref[p]], axis=0) for p in range(2)]


def _ffn_act(u4, w4, b4, name):
    s = u4.shape[2]
    tm = min(TM, s)
    main, prev, _, wsp, bsp = _ffn_specs(s, tm)

    def body(u_ref, uh_ref, w_ref, b_ref, o_ref):
        gw, vw = _ffn_windows(u_ref, uh_ref)
        o_ref[...] = (_silu(_conv3(gw, w_ref[0], b_ref[0], tm)) * _conv3(vw, w_ref[1], b_ref[1], tm)).astype(BF16)

    return pl.pallas_call(
        body, name=name, grid=(FF_NBLK, s // tm), in_specs=[main, prev, wsp, bsp],
        out_specs=pl.BlockSpec((None, tm, FF_BLK), lambda j, i: (j, i, 0)), out_shape=SDS((FF_NBLK, s, FF_BLK), BF16),
        compiler_params=_cp(("parallel", "parallel")))(u4, u4, w4, b4)


def _ffn_gate_bwd(u4, dact, w4, b4, name):
    s = u4.shape[2]
    tm = min(TM, s)
    main, prev, _, wsp, bsp = _ffn_specs(s, tm)

    def body(u_ref, uh_ref, d_ref, w_ref, b_ref, duc_ref, dw_ref, db_ref):
        first = pl.program_id(1) == 0
        wins = _ffn_windows(u_ref, uh_ref)
        gc = _conv3(wins[0], w_ref[0], b_ref[0], tm)
        vc = _conv3(wins[1], w_ref[1], b_ref[1], tm)
        d = d_ref[...]
        duc = (d * vc * _dsilu(gc), d * _silu(gc))
        for p in range(2):
            duc_ref[p] = duc[p]

            @pl.when(first)
            def _():
                db_ref[p] = _colsum(duc[p])
                for k in range(3):
                    dw_ref[p, k:k + 1, :] = _colsum(duc[p] * (wins[p] if k == 2 else pltpu.roll(wins[p], 2 - k, 0))[FFN_HALO:, :])

            @pl.when(jnp.logical_not(first))
            def _():
                db_ref[p] += _colsum(duc[p])
                for k in range(3):
                    dw_ref[p, k:k + 1, :] += _colsum(duc[p] * (wins[p] if k == 2 else pltpu.roll(wins[p], 2 - k, 0))[FFN_HALO:, :])

    return pl.pallas_call(
        body, name=name, grid=(FF_NBLK, s // tm),
        in_specs=[main, prev, pl.BlockSpec((None, tm, FF_BLK), lambda j, i: (j, i, 0)), wsp, bsp],
        out_specs=[main, wsp, bsp],
        out_shape=[SDS(u4.shape, F32), SDS((2, FF_NBLK, 3, FF_BLK), F32), SDS((2, FF_NBLK, 1, FF_BLK), F32)],
        compiler_params=_cp(("parallel", "arbitrary")))(u4, u4, dact, w4, b4)


def _ffn_conv_bwd(duc, w4, name):
    s = duc.shape[2]
    tm = min(TM, s)
    nt = s // tm
    main, _, nxt, wsp, _ = _ffn_specs(s, tm)

    def body(d_ref, dn_ref, w_ref, o_ref):
        keep = (pl.program_id(1) < nt - 1).astype(F32)
        for p in range(2):
            win = jnp.concatenate([d_ref[p], dn_ref[p] * keep], axis=0)
            w = w_ref[p]
            o_ref[p] = (w[2:3, :] * win[:tm, :] + w[1:2, :] * pltpu.roll(win, tm + FFN_HALO - 1, 0)[:tm, :]
                        + w[0:1, :] * pltpu.roll(win, tm + FFN_HALO - 2, 0)[:tm, :]).astype(BF16)

    return pl.pallas_call(
        body, name=name, grid=(FF_NBLK, nt), in_specs=[main, nxt, wsp], out_specs=main, out_shape=SDS(duc.shape, BF16),
        compiler_params=_cp(("parallel", "parallel")))(duc, duc, w4)


def _sum8(parts, name):
    _, r, c = parts.shape
    tr = r
    for cand in (512, 256, 128, 64, 32, 16):
        if r % cand == 0 and r > cand:
            tr = cand
            break

    def body(p_ref, o_ref):
        acc = p_ref[0].astype(F32)
        for j in range(1, N_DEV):
            acc = acc + p_ref[j].astype(F32)
        o_ref[...] = acc

    return pl.pallas_call(
        body, name=name, grid=(r // tr,), in_specs=[pl.BlockSpec((N_DEV, tr, c), lambda i: (0, i, 0))],
        out_specs=pl.BlockSpec((tr, c), lambda i: (i, 0)), out_shape=SDS((r, c), F32),
        compiler_params=_cp(("parallel",)))(parts)


def _sum8_small(parts, name):
    n = len(parts)

    def body(*refs):
        for p_ref, o_ref in zip(refs[:n], refs[n:]):
            acc = p_ref[0]
            for j in range(1, N_DEV):
                acc = acc + p_ref[j]
            o_ref[...] = acc

    return pl.pallas_call(body, name=name, out_shape=[SDS(p.shape[1:], F32) for p in parts], compiler_params=_cp())(*parts)


def _adamw_math(w, g, m, v):
    m = ADAM_B1 * m + (1.0 - ADAM_B1) * g
    v = ADAM_B2 * v + (1.0 - ADAM_B2) * (g * g)
    m_hat = m / (1.0 - ADAM_B1 ** ADAM_STEP)
    v_hat = v / (1.0 - ADAM_B2 ** ADAM_STEP)
    return -ADAM_LR * (m_hat / (jnp.sqrt(v_hat) + ADAM_EPS) + ADAM_WD * w), m, v


def _adamw(w, g, m, v, name):
    r, c = w.shape
    tr = r
    for cand in (256, 128, 64, 32, 16, 8):
        if r % cand == 0 and r > cand:
            tr = cand
            break

    def body(w_ref, g_ref, m_ref, v_ref, d_ref, mo_ref, vo_ref):
        d_ref[...], mo_ref[...], vo_ref[...] = _adamw_math(w_ref[...], g_ref[...], m_ref[...], v_ref[...])

    blk = pl.BlockSpec((tr, c), lambda i: (i, 0))
    return pl.pallas_call(body, name=name, grid=(r // tr,), in_specs=[blk] * 4, out_specs=[blk] * 3,
                          out_shape=[SDS((r, c), F32)] * 3, compiler_params=_cp(("parallel",)))(w, g, m, v)


def _adamw_small(ws, gs, ms, vs, name):
    n = len(ws)

    def body(*refs):
        ins, outs = refs[:4 * n], refs[4 * n:]
        for i in range(n):
            d, m, v = _adamw_math(ins[i][...], ins[n + i][...], ins[2 * n + i][...], ins[3 * n + i][...])
            outs[i][...], outs[n + i][...], outs[2 * n + i][...] = d, m, v

    shapes = [SDS(w.shape, F32) for w in ws]
    res = pl.pallas_call(body, name=name, out_shape=shapes * 3, compiler_params=_cp())(*ws, *gs, *ms, *vs)
    return res[:n], res[n:2 * n], res[2 * n:]


def _perm_in(w):
    pad = jnp.zeros(w.shape[:-1] + (ZW - 2308,), w.dtype)
    return jnp.concatenate([w[..., :768], w[..., 772:], w[..., 768:772], pad], axis=-1)


def _unperm_in(g):
    return jnp.concatenate([g[..., :768], g[..., 2304:2308], g[..., 768:2304]], axis=-1)


def _wcat(sgu_w):
    return sgu_w.transpose(1, 0, 2).reshape(WIN, 4 * WIN)


def _layer_fwd(l, x, h1, mod, p, wg, last, target, nxt):
    s = x.shape[0]
    tag = f"_l{l}"
    mrow = lambda k: (mod, 6 * l + k)
    z = _mm_rows(h1, wg["w_in"], "nn", ZW, F32, "mm_z" + tag)
    fcol, frow = _fgate(z, p["bf"], "fgate" + tag)
    y_fox, lse_fox = _fox_fwd(z, fcol, frow, "fox_fwd" + tag)
    y_conv, hc = _conv_fwd(z, wg["conv_w"], p["conv_b"], p["conv_ln_g"], p["conv_ln_b"], wg["conv_pw_w"], p["conv_pw_b"],
                           "conv_fwd" + tag)
    y_swa, lse_swa = _swa_fwd(z, p["sinks"], "swa_fwd" + tag)
    y_sgu = _sgu_fwd(z, p["sgu_ln_g"], p["sgu_ln_b"], p["wcat"], p["bfull"], "sgu_fwd" + tag)
    ys = (y_fox, y_conv, y_swa, y_sgu)
    yn = _gnorm(ys, (p["g_group"], l), "gnorm" + tag)
    o = _mm_rows(yn, wg["w_out"], "nn", D, F32, "mm_o" + tag)
    x1, h2 = _post(x, o, mrow(2), (p["g_post_mix"], l), (p["g_pre_ffn"], l), mrow(4), mrow(3), "post_mix" + tag)
    tm = min(TM, s)
    u = _matmul(h2, wg["w_up"], "nn", (N_DEV, s, FF_BLK), F32, (N_DEV, s // tm, 1),
                _bs((tm, D), lambda j, i, k: (i, 0)), _bs((None, D, FF_BLK), lambda j, i, k: (j, 0, 0)),
                _bs((None, tm, FF_BLK), lambda j, i, k: (j, i, 0)), None, "mm_u" + tag)
    u4 = u.reshape(2, FF_NBLK, s, FF_BLK)
    act = _ffn_act(u4, wg["ffn_conv_w"], p["ffn_conv_b"], "ffn_act" + tag)
    f = _matmul(act, wg["w_down"], "nn", (s, D), F32, (s // tm, 1, FF_NBLK),
                _bs((None, tm, FF_BLK), lambda i, j, k: (k, i, 0)), _bs((FF_BLK, D), lambda i, j, k: (k, 0)),
                _bs((tm, D), lambda i, j, k: (i, 0)), (tm, D), "mm_f" + tag)
    if last:
        out = _post_loss(x1, f, mrow(5), (p["g_post_ffn"], l), target, "post_loss")
    else:
        out = _post(x1, f, mrow(5), (p["g_post_ffn"], l), *nxt, "post_ffn" + tag)
    saved = dict(x=x, h1=h1, z=z, fcol=fcol, frow=frow, lse_fox=lse_fox, hc=hc, lse_swa=lse_swa, ys=ys, yn=yn, o=o, x1=x1,
                 h2=h2, u4=u4, act=act, f=f)
    return out, saved


def _layer_bwd(l, dx2, sv, mod, p, wg):
    s = dx2.shape[0]
    tm = min(TM, s)
    tag = f"_l{l}"
    mrow = lambda k: (mod, 6 * l + k)
    g = {}
    df, g["ga2"], g["g_post_ffn"] = _post_bwd(dx2, sv["f"], mrow(5), (p["g_post_ffn"], l), "post_ffn_bwd" + tag)
    dact = _matmul(df, wg["w_down"], "nt", (FF_NBLK, s, FF_BLK), F32, (FF_NBLK, s // tm, 1),
                   _bs((tm, D), lambda j, i, k: (i, 0)), _bs((FF_BLK, D), lambda j, i, k: (j, 0)),
                   _bs((None, tm, FF_BLK), lambda j, i, k: (j, i, 0)), None, "mm_dact" + tag)
    tk = tm
    g["w_down"] = _matmul(sv["act"], df, "tn", (FF_NBLK * FF_BLK, D), BF16, (FF_NBLK, 1, s // tk),
                          _bs((None, tk, FF_BLK), lambda j, i, k: (j, k, 0)), _bs((tk, D), lambda j, i, k: (k, 0)),
                          _bs((FF_BLK, D), lambda j, i, k: (j, 0)), (FF_BLK, D), "mm_dwdown" + tag)
    duc, g["ffn_conv_w"], g["ffn_conv_b"] = _ffn_gate_bwd(sv["u4"], dact, wg["ffn_conv_w"], p["ffn_conv_b"], "ffn_gate_bwd" + tag)
    du = _ffn_conv_bwd(duc, wg["ffn_conv_w"], "ffn_conv_bwd" + tag).reshape(N_DEV, s, FF_BLK)
    dh2 = _matmul(du, wg["w_up"], "nt", (s, D), F32, (s // tm, 1, N_DEV),
                  _bs((None, tm, FF_BLK), lambda i, j, k: (k, i, 0)), _bs((None, D, FF_BLK), lambda i, j, k: (k, 0, 0)),
                  _bs((tm, D), lambda i, j, k: (i, 0)), (tm, D), "mm_dh2" + tag)
    g["w_up"] = _matmul(sv["h2"], du, "tn", (N_DEV, D, FF_BLK), BF16, (N_DEV, 1, s // tk),
                        _bs((tk, D), lambda j, i, k: (k, 0)), _bs((None, tk, FF_BLK), lambda j, i, k: (j, k, 0)),
                        _bs((None, D, FF_BLK), lambda j, i, k: (j, 0, 0)), (D, FF_BLK), "mm_dwup" + tag)
    dx1, g["sh2"], g["sc2"], g["g_pre_ffn"] = _pre_bwd(dh2, sv["x1"], dx2, (p["g_pre_ffn"], l), mrow(4), "pre_ffn_bwd" + tag)
    do, g["ga1"], g["g_post_mix"] = _post_bwd(dx1, sv["o"], mrow(2), (p["g_post_mix"], l), "post_mix_bwd" + tag)
    dyn = _mm_rows(do, wg["w_out"], "nt", D, F32, "mm_dyn" + tag)
    g["w_out"] = _mm_wgrad(sv["yn"], do, BF16, "mm_dwout" + tag)
    dy_fox, dy_conv, dy_swa, dy_sgu, g["g_group"] = _gnorm_bwd(dyn, sv["ys"], (p["g_group"], l), "gnorm_bwd" + tag)
    z = sv["z"]
    dq_f, dk_f, dv_f, dfrow, dfcol = _fox_bwd(z, sv["fcol"], sv["frow"], sv["lse_fox"], sv["ys"][0], dy_fox, "fox_bwd" + tag)
    dgate, g["bf"] = _fgate_bwd(z, p["bf"], dfrow, dfcol, "fgate_bwd" + tag)
    dhc, g["conv_pw_w"], g["conv_pw_b"], g["conv_ln_g"], g["conv_ln_b"], g["conv_w"], g["conv_b"] = _conv_bwd_a(
        z, sv["hc"], dy_conv, wg["conv_w"], p["conv_ln_g"], p["conv_ln_b"], wg["conv_pw_w"], "conv_bwd_a" + tag)
    da_c, dg_c = _conv_bwd_b(z, dhc, wg["conv_w"], "conv_bwd_b" + tag)
    dq_s, dk_s, dv_s, g["sinks"] = _swa_bwd(z, p["sinks"], sv["lse_swa"], sv["ys"][2], dy_swa, "swa_bwd" + tag)
    du_g, dv_g, g["wcat"], g["sgu_bcol"], g["sgu_ln_g"], g["sgu_ln_b"] = _sgu_bwd(
        z, dy_sgu, p["sgu_ln_g"], p["sgu_ln_b"], p["wcat"], p["bfull"], "sgu_bwd" + tag)
    dz = jnp.concatenate([dq_f.astype(BF16), dk_f.astype(BF16), dv_f.astype(BF16), da_c, dg_c, dq_s.astype(BF16), dk_s.astype(BF16),
                          dv_s.astype(BF16), du_g, dv_g, dgate], axis=1)
    dh1 = _mm_rows(dz, wg["w_in"], "nt", D, F32, "mm_dh1" + tag)
    g["w_in"] = _mm_wgrad(sv["h1"], dz, BF16, "mm_dwin" + tag)
    dx, g["sh1"], g["sc1"], g["g_pre_mix"] = _pre_bwd(dh1, sv["x"], dx1, (p["g_pre_mix"], l), mrow(1), "pre_mix_bwd" + tag)
    return dx, g


def _layer_params(l, small, conv_w_full, conv_pw_full, ffn_conv_w_full):
    bf = jnp.pad(small["b_fgate"][l][None, :], ((0, 0), (0, 124)))
    p = dict(
        bf=bf, conv_b=small["conv_b"][l][None], conv_ln_g=small["conv_ln_g"][l][None], conv_ln_b=small["conv_ln_b"][l][None],
        conv_pw_b=small["conv_pw_b"][l][None], sinks=small["swa_sinks"][l], sgu_ln_g=small["sgu_ln_g"][l][None],
        sgu_ln_b=small["sgu_ln_b"][l][None], wcat=_wcat(small["sgu_w"][l]),
        bfull=jnp.repeat(small["sgu_b"][l].T, HD, axis=1),
        ffn_conv_b=small["ffn_conv_b"][l].reshape(2, FF_NBLK, 1, FF_BLK),
        g_group=small["g_group"].reshape(N_LAYER, 1, D), g_post_mix=small["g_post_mix"].reshape(N_LAYER, 1, D),
        g_pre_ffn=small["g_pre_ffn"].reshape(N_LAYER, 1, D), g_post_ffn=small["g_post_ffn"].reshape(N_LAYER, 1, D),
        g_pre_mix=small["g_pre_mix"].reshape(N_LAYER, 1, D))
    wsmall = dict(conv_w=conv_w_full[l], conv_pw_w=conv_pw_full[l].astype(BF16),
                  ffn_conv_w=ffn_conv_w_full[l].reshape(3, 2, FF_NBLK, FF_BLK).transpose(1, 2, 0, 3))
    return p, wsmall


def _local_step(x, target, mod, small, wbig, conv_w_full, conv_pw_full, ffn_conv_w_full):
    ps, wgs = [], []
    for l in range(N_LAYER):
        p, wsmall = _layer_params(l, small, conv_w_full, conv_pw_full, ffn_conv_w_full)
        ps.append(p)
        wgs.append({**wbig[l], **wsmall})
    h = _rms_mod(x, (ps[0]["g_pre_mix"], 0), (mod, 1), (mod, 0), "rms_mod_l0")
    saved = []
    for l in range(N_LAYER):
        last = l == N_LAYER - 1
        nxt = None if last else ((ps[l]["g_pre_mix"], l + 1), (mod, 6 * (l + 1) + 1), (mod, 6 * (l + 1)))
        out, sv = _layer_fwd(l, x, h, mod, ps[l], wgs[l], last, target, nxt)
        saved.append(sv)
        if not last:
            x, h = out
    dx, loss = out
    grads = [None] * N_LAYER
    for l in reversed(range(N_LAYER)):
        dx, grads[l] = _layer_bwd(l, dx, saved[l], mod, ps[l], wgs[l])
    return loss, dx, grads


_SMALL = ("b_ada", "g_pre_mix", "g_post_mix", "g_pre_ffn", "g_post_ffn", "b_fgate", "conv_b", "conv_ln_g", "conv_ln_b",
          "conv_pw_b", "swa_sinks", "sgu_ln_g", "sgu_ln_b", "sgu_w", "sgu_b", "g_group", "ffn_conv_b")
_WEIGHTS = ("w_ada", "b_ada", "g_pre_mix", "g_post_mix", "g_pre_ffn", "g_post_ffn", "w_in", "b_fgate", "conv_w", "conv_b",
            "conv_ln_g", "conv_ln_b", "conv_pw_w", "conv_pw_b", "swa_sinks", "sgu_ln_g", "sgu_ln_b", "sgu_w", "sgu_b",
            "g_group", "w_out", "ffn_w_up", "ffn_conv_w", "ffn_conv_b", "ffn_w_down")


def _pad_rows(a, mult):
    r = (-a.shape[0]) % mult
    return a if r == 0 else jnp.concatenate([a, jnp.zeros((r,) + a.shape[1:], a.dtype)], axis=0)


def _view2d(a):
    if a.ndim == 2:
        return a
    return a.reshape(-1, a.shape[-1])


def kernel(x, c, w_ada, b_ada, g_pre_mix, g_post_mix, g_pre_ffn, g_post_ffn, w_in, b_fgate, conv_w, conv_b, conv_ln_g, conv_ln_b, conv_pw_w, conv_pw_b, swa_sinks, sgu_ln_g, sgu_ln_b, sgu_w, sgu_b, g_group, w_out, ffn_w_up, ffn_conv_w, ffn_conv_b, ffn_w_down, loss_target, m_w_ada, m_b_ada, m_g_pre_mix, m_g_post_mix, m_g_pre_ffn, m_g_post_ffn, m_w_in, m_b_fgate, m_conv_w, m_conv_b, m_conv_ln_g, m_conv_ln_b, m_conv_pw_w, m_conv_pw_b, m_swa_sinks, m_sgu_ln_g, m_sgu_ln_b, m_sgu_w, m_sgu_b, m_g_group, m_w_out, m_ffn_w_up, m_ffn_conv_w, m_ffn_conv_b, m_ffn_w_down, v_w_ada, v_b_ada, v_g_pre_mix, v_g_post_mix, v_g_pre_ffn, v_g_post_ffn, v_w_in, v_b_fgate, v_conv_w, v_conv_b, v_conv_ln_g, v_conv_ln_b, v_conv_pw_w, v_conv_pw_b, v_swa_sinks, v_sgu_ln_g, v_sgu_ln_b, v_sgu_w, v_sgu_b, v_g_group, v_w_out, v_ffn_w_up, v_ffn_conv_w, v_ffn_conv_b, v_ffn_w_down):
    env = dict(locals())
    w = {n: env[n] for n in _WEIGHTS}
    mom = {n: env["m_" + n] for n in _WEIGHTS}
    var = {n: env["v_" + n] for n in _WEIGHTS}
    me = 4 * lax.axis_index("x") + 2 * lax.axis_index("y") + lax.axis_index("c")
    x2, target = x[0], loss_target[0]

    (c_all,) = _exchange([c], ["bcast"], "gather_c")
    c_all = c_all.reshape(N_DEV, D)
    (m_all,) = _exchange([_ada_fwd(c_all, w_ada)], ["bcast"], "gather_mod")
    m_mine = lax.dynamic_index_in_dim(m_all, me, axis=2, keepdims=False)
    mod = _ada_finish(m_mine.transpose(1, 0, 2).reshape(N_LAYER, 6 * D), b_ada).reshape(6 * N_LAYER, 1, D)

    shards = []
    for l in range(N_LAYER):
        shards += [_perm_in(w_in[l]).astype(BF16), w_out[l].astype(BF16), ffn_w_up[l].astype(BF16), ffn_w_down[l].astype(BF16)]
    gathered = _exchange(shards + [conv_w, conv_pw_w, ffn_conv_w], ["bcast"] * (4 * N_LAYER + 3), "gather_weights")
    wbig = []
    for l in range(N_LAYER):
        gi, go, gu, gd = gathered[4 * l:4 * l + 4]
        wbig.append(dict(w_in=gi.reshape(D, ZW), w_out=go.reshape(D, D), w_up=gu, w_down=gd.reshape(FF_NBLK * FF_BLK, D)))
    conv_w_full = gathered[-3].transpose(1, 2, 0, 3).reshape(N_LAYER, CONV_K, GW)
    conv_pw_full = gathered[-2].transpose(1, 0, 2, 3).reshape(N_LAYER, GW, GW)
    ffn_conv_w_full = gathered[-1].transpose(1, 2, 0, 3).reshape(N_LAYER, 3, N_DEV * FF_BLK)

    small = {n: w[n] for n in _SMALL}
    loss8, dx, grads = _local_step(x2, target, mod, small, wbig, conv_w_full, conv_pw_full, ffn_conv_w_full)
    loss = lax.psum(loss8[0, 0], ("x", "y", "c"))
    grad_x = dx[None]

    big = []
    for l in range(N_LAYER):
        gl = grads[l]
        big += [gl["w_in"].reshape(N_DEV, D // N_DEV, ZW), gl["w_out"].reshape(N_DEV, D // N_DEV, D), gl["w_up"],
                gl["w_down"].reshape(N_DEV, FF_NBLK * FF_BLK // N_DEV, D)]
    big_parts = _exchange(big, ["a2a"] * len(big), "exchange_big_grads")

    st = lambda key: jnp.stack([grads[l][key] for l in range(N_LAYER)])
    d_conv_w = st("conv_w")[:, :CONV_K, :].reshape(N_LAYER, CONV_K, N_DEV, GW // N_DEV).transpose(2, 0, 1, 3)
    d_pw_w = st("conv_pw_w").reshape(N_LAYER, N_DEV, GW // N_DEV, GW).transpose(1, 0, 2, 3)
    d_fcw = st("ffn_conv_w").reshape(N_LAYER, N_DEV, 3, FF_BLK).transpose(1, 0, 2, 3)
    rows_d = _pad_rows(jnp.concatenate(
        [grads[l][k] for l in range(N_LAYER) for k in ("sh1", "sc1", "ga1", "sh2", "sc2", "ga2")]
        + [grads[l][k] for k in ("g_pre_mix", "g_post_mix", "g_pre_ffn", "g_post_ffn", "g_group") for l in range(N_LAYER)],
        axis=0), 8)
    rows_gw = _pad_rows(jnp.concatenate(
        [grads[l][k] for k in ("conv_b", "conv_ln_g", "conv_ln_b", "conv_pw_b", "sgu_ln_g", "sgu_ln_b") for l in range(N_LAYER)],
        axis=0), 8)
    rows_128 = jnp.concatenate([_pad_rows(jnp.concatenate([grads[l]["bf"] for l in range(N_LAYER)]
                                                          + [grads[l]["sinks"] for l in range(N_LAYER)], axis=0), 8)]
                               + [grads[l]["sgu_bcol"] for l in range(N_LAYER)], axis=0)
    rows_w = jnp.concatenate([grads[l]["wcat"] for l in range(N_LAYER)], axis=0)
    rows_fb = st("ffn_conv_b").reshape(N_LAYER * N_DEV, FF_BLK)
    small_parts = _exchange([d_conv_w, d_pw_w, d_fcw, rows_d, rows_gw, rows_128, rows_w, rows_fb],
                            ["a2a"] * 3 + ["bcast"] * 5, "exchange_small_grads")
    s_conv_w, s_pw_w, s_fcw, s_d, s_gw, s_128, s_w, s_fb = _sum8_small(
        [p.reshape(N_DEV, -1, p.shape[-1]) for p in small_parts], "sum_small_grads")

    gr = {}
    big_sums = [_sum8(p.reshape(N_DEV, -1, p.shape[-1]), f"sum_big_{i}") for i, p in enumerate(big_parts)]
    gr["w_in"] = _unperm_in(jnp.stack([big_sums[4 * l] for l in range(N_LAYER)]))
    gr["w_out"] = jnp.stack([big_sums[4 * l + 1] for l in range(N_LAYER)])
    gr["ffn_w_up"] = jnp.stack([big_sums[4 * l + 2] for l in range(N_LAYER)])
    gr["ffn_w_down"] = jnp.stack([big_sums[4 * l + 3] for l in range(N_LAYER)])
    gr["conv_w"] = s_conv_w.reshape(N_LAYER, CONV_K, GW // N_DEV)
    gr["conv_pw_w"] = s_pw_w.reshape(N_LAYER, GW // N_DEV, GW)
    gr["ffn_conv_w"] = s_fcw.reshape(N_LAYER, 3, FF_BLK)
    gr["b_ada"] = s_d[:6 * N_LAYER].reshape(N_LAYER, 6 * D)
    for i, k in enumerate(("g_pre_mix", "g_post_mix", "g_pre_ffn", "g_post_ffn", "g_group")):
        gr[k] = s_d[6 * N_LAYER + 2 * i:6 * N_LAYER + 2 * i + 2]
    for i, k in enumerate(("conv_b", "conv_ln_g", "conv_ln_b", "conv_pw_b", "sgu_ln_g", "sgu_ln_b")):
        gr[k] = s_gw[2 * i:2 * i + 2]
    gr["b_fgate"] = s_128[0:2, :4]
    gr["swa_sinks"] = s_128[2:4, :4]
    gr["sgu_b"] = s_128[8:].reshape(N_LAYER, WIN, 128)[:, :, :4].transpose(0, 2, 1)
    gr["sgu_w"] = s_w.reshape(N_LAYER, WIN, 4, WIN).transpose(0, 2, 1, 3)
    gr["ffn_conv_b"] = s_fb.reshape(N_LAYER, N_DEV * FF_BLK)
    dmod_all = small_parts[3][:, :6 * N_LAYER, :].reshape(N_DEV, N_LAYER, 6 * D)
    ncol = 6 * D // N_DEV
    dmod_cols = lax.dynamic_slice_in_dim(dmod_all, me * ncol, ncol, axis=2).transpose(1, 0, 2)
    gr["w_ada"] = _ada_bwd(c_all, dmod_cols)

    delta, new_m, new_v = {}, {}, {}
    bigs = ("w_ada", "w_in", "w_out", "ffn_w_up", "ffn_w_down")
    for n in bigs:
        d, m2, v2 = _adamw(_view2d(w[n]), _view2d(gr[n]), _view2d(mom[n]), _view2d(var[n]), "adamw_" + n)
        delta[n], new_m[n], new_v[n] = d.reshape(w[n].shape), m2.reshape(w[n].shape), v2.reshape(w[n].shape)
    smalls = [n for n in _WEIGHTS if n not in bigs]
    ds, ms, vs = _adamw_small([_view2d(w[n]) for n in smalls], [_view2d(gr[n]) for n in smalls],
                              [_view2d(mom[n]) for n in smalls], [_view2d(var[n]) for n in smalls], "adamw_small")
    for i, n in enumerate(smalls):
        delta[n], new_m[n], new_v[n] = ds[i].reshape(w[n].shape), ms[i].reshape(w[n].shape), vs[i].reshape(w[n].shape)

    return (loss, grad_x, *[gr[n].reshape(w[n].shape) for n in _WEIGHTS], *[delta[n] for n in _WEIGHTS],
            *[new_m[n] for n in _WEIGHTS], *[new_v[n] for n in _WEIGHTS])
```

```python
import functools

import jax
import jax.numpy as jnp
from jax import lax
from jax.experimental import pallas as pl
from jax.experimental.pallas import tpu as pltpu

F32, BF16 = jnp.float32, jnp.bfloat16
SDS = jax.ShapeDtypeStruct
MESH = pl.DeviceIdType.MESH

N_DEV = 8
D = 1024
GW = 256
HD = 64
N_LAYER = 2
ZW = 2432
FF_BLK = 704
FF_NBLK = 4
CONV_K = 31
CONV_HALO = 32
FFN_HALO = 8
EPS = 1e-6
NEG = -1e30
SCALE = HD ** -0.5
VMEM_LIMIT_V7X = 56 * 1024 * 1024
TM = 512
TQ = 256
WIN = 128

ADAM_LR, ADAM_B1, ADAM_B2, ADAM_EPS, ADAM_WD, ADAM_STEP = 0.001, 0.9, 0.999, 1e-08, 0.01, 10

Z_FQ, Z_FK, Z_FV, Z_CA, Z_CG, Z_SQ = 0, 1, 2, 3, 4, 5
Z_SK, Z_SV = 12, 13
Z_GU, Z_GV = 7, 8
Z_FG = 18


def _cp(sem=None):
    return pltpu.CompilerParams(dimension_semantics=sem, vmem_limit_bytes=VMEM_LIMIT_V7X)


def _vec(arr3, idx, ngrid):
    w = arr3.shape[-1]
    if ngrid == 1:
        return pl.BlockSpec((None, 1, w), lambda i: (idx, 0, 0))
    return pl.BlockSpec((None, 1, w), lambda i, j: (idx, 0, 0))


def _sigmoid(x):
    return jax.nn.sigmoid(x)


def _silu(x):
    return x * _sigmoid(x)


def _dsilu(x):
    s = _sigmoid(x)
    return s * (1.0 + x * (1.0 - s))


_G0, _G1 = 0.7978845608028654, 0.044715


def _gelu(x):
    return 0.5 * x * (1.0 + jnp.tanh(_G0 * (x + _G1 * x * x * x)))


def _dgelu(x):
    t = jnp.tanh(_G0 * (x + _G1 * x * x * x))
    return 0.5 * (1.0 + t) + 0.5 * x * (1.0 - t * t) * (_G0 * (1.0 + 3.0 * _G1 * x * x))


def _rstd(x):
    return lax.rsqrt(jnp.mean(x * x, axis=-1, keepdims=True) + EPS)


def _rms_bwd(xh, r, t):
    return r * (t - xh * jnp.mean(t * xh, axis=-1, keepdims=True))


def _ln_stats(x):
    mu = jnp.mean(x, axis=-1, keepdims=True)
    xc = x - mu
    rstd = lax.rsqrt(jnp.mean(xc * xc, axis=-1, keepdims=True) + EPS)
    return xc * rstd, rstd


def _ln_bwd(xh, rstd, dxh):
    return rstd * (dxh - jnp.mean(dxh, axis=-1, keepdims=True) - xh * jnp.mean(dxh * xh, axis=-1, keepdims=True))


def _colsum(x):
    return jnp.sum(x, axis=0, keepdims=True)


def _dot(a, b, kind):
    dn = {"nn": (((1,), (0,)), ((), ())), "nt": (((1,), (1,)), ((), ())), "tn": (((0,), (0,)), ((), ()))}[kind]
    return lax.dot_general(a.astype(BF16), b.astype(BF16), dn, preferred_element_type=F32)


def _exchange(arrs, modes, name):
    n = len(arrs)
    outs = [SDS((N_DEV,) + a.shape, a.dtype) if m == "bcast" else SDS(a.shape, a.dtype) for a, m in zip(arrs, modes)]

    def body(*refs):
        ins, dst = refs[:n], refs[n:2 * n]
        send, recv, loc = refs[2 * n:]
        x, y, c = lax.axis_index("x"), lax.axis_index("y"), lax.axis_index("c")
        me = 4 * x + 2 * y + c

        def src(a, j):
            return ins[a] if modes[a] == "bcast" else ins[a].at[j]

        local = [pltpu.make_async_copy(src(a, me), dst[a].at[me], loc.at[a]) for a in range(n)]
        for cp in local:
            cp.start()
        sent, landed = [], []
        for k in (2, 4, 6, 3, 5, 7, 1):
            px = 1 - x if k & 4 else x
            py = 1 - y if k & 2 else y
            pc = 1 - c if k & 1 else c
            peer = 4 * px + 2 * py + pc
            for a in range(n):
                cp = pltpu.make_async_remote_copy(src_ref=src(a, peer), dst_ref=dst[a].at[me], send_sem=send.at[a, k - 1],
                                                  recv_sem=recv.at[a, k - 1], device_id=(px, py, pc), device_id_type=MESH)
                cp.start()
                sent.append(cp)
                landed.append(pltpu.make_async_remote_copy(src_ref=src(a, peer), dst_ref=dst[a].at[peer],
                                                           send_sem=send.at[a, k - 1], recv_sem=recv.at[a, k - 1],
                                                           device_id=(px, py, pc), device_id_type=MESH))
        for cp in landed:
            cp.wait_recv()
        for cp in sent:
            cp.wait_send()
        for cp in local:
            cp.wait()

    hbm = pl.BlockSpec(memory_space=pltpu.HBM)
    return pl.pallas_call(
        body, name=name, out_shape=outs, in_specs=[hbm] * n, out_specs=[hbm] * n,
        scratch_shapes=[pltpu.SemaphoreType.DMA((n, N_DEV - 1)), pltpu.SemaphoreType.DMA((n, N_DEV - 1)),
                        pltpu.SemaphoreType.DMA((n,))],
        compiler_params=pltpu.CompilerParams(has_side_effects=True),
    )(*arrs)


_PEER_ORDER = (2, 4, 6, 3, 5, 7, 1)
_HBM = pl.BlockSpec(memory_space=pltpu.HBM)
_SEM = pl.BlockSpec(memory_space=pltpu.SEMAPHORE)
_EFFECT = pltpu.SideEffectType.DATAFLOW_SIDE_EFFECTING


def _peer(k):
    x, y, c = lax.axis_index("x"), lax.axis_index("y"), lax.axis_index("c")
    px = 1 - x if k & 4 else x
    py = 1 - y if k & 2 else y
    pc = 1 - c if k & 1 else c
    return (px, py, pc), 4 * px + 2 * py + pc


def _my_id():
    return 4 * lax.axis_index("x") + 2 * lax.axis_index("y") + lax.axis_index("c")


def _split_copies(src_ref, land_ref, send, recv, loc, mode):
    me = _my_id()
    pick = (lambda j: src_ref) if mode == "bcast" else (lambda j: src_ref.at[j])
    local = pltpu.make_async_copy(pick(me), land_ref.at[me], loc)
    remote = []
    for k in _PEER_ORDER:
        dev, peer = _peer(k)
        out = pltpu.make_async_remote_copy(src_ref=pick(peer), dst_ref=land_ref.at[me], send_sem=send.at[k - 1],
                                           recv_sem=recv.at[k - 1], device_id=dev, device_id_type=MESH)
        arrive = pltpu.make_async_remote_copy(src_ref=pick(peer), dst_ref=land_ref.at[peer], send_sem=send.at[k - 1],
                                              recv_sem=recv.at[k - 1], device_id=dev, device_id_type=MESH)
        remote.append((out, arrive))
    return local, remote


class _Flight:
    def __init__(self, srcs, lands, sends, recvs, locs, modes, token):
        self.srcs, self.lands, self.sends, self.recvs, self.locs, self.modes, self.token = (
            list(srcs), list(lands), list(sends), list(recvs), list(locs), list(modes), token)


def _xchg_start(arrs, modes, name):
    n = len(arrs)
    lands = [lax.empty((N_DEV,) + a.shape if m == "bcast" else a.shape, a.dtype) for a, m in zip(arrs, modes)]

    def body(*refs):
        srcs, lnds = refs[:n], refs[n:2 * n]
        outs = refs[2 * n:]
        sends, recvs, locs, token = outs[:n], outs[n:2 * n], outs[2 * n:3 * n], outs[5 * n]
        for a in range(n):
            local, remote = _split_copies(srcs[a], lnds[a], sends[a], recvs[a], locs[a], modes[a])
            local.start()
            for out, _ in remote:
                out.start()
        token[...] = jnp.zeros_like(token)

    sem7 = pltpu.SemaphoreType.DMA((N_DEV - 1,))
    res = pl.pallas_call(
        body, name=name,
        out_shape=[sem7] * (2 * n) + [pltpu.SemaphoreType.DMA(())] * n + [pltpu.HBM(a.shape, a.dtype) for a in arrs]
        + [pltpu.HBM(b.shape, b.dtype) for b in lands] + [SDS((8, 128), F32)],
        in_specs=[_HBM] * (2 * n), out_specs=[_SEM] * (3 * n) + [_HBM] * (2 * n) + [pl.BlockSpec(memory_space=pltpu.VMEM)],
        input_output_aliases={i: 3 * n + i for i in range(2 * n)},
        compiler_params=pltpu.CompilerParams(has_side_effects=_EFFECT),
    )(*[pltpu.with_memory_space_constraint(a, pltpu.HBM) for a in arrs],
      *[pltpu.with_memory_space_constraint(b, pltpu.HBM) for b in lands])
    return _Flight(res[3 * n:4 * n], res[4 * n:5 * n], res[:n], res[n:2 * n], res[2 * n:3 * n], modes, res[5 * n])


def _xchg_wait(flight, idx, after, name):
    n = len(idx)
    modes = [flight.modes[i] for i in idx]

    def body(*refs):
        srcs, lnds = refs[:n], refs[n:2 * n]
        sends, recvs, locs = refs[2 * n:3 * n], refs[3 * n:4 * n], refs[4 * n:5 * n]
        for a in range(n):
            local, remote = _split_copies(srcs[a], lnds[a], sends[a], recvs[a], locs[a], modes[a])
            local.wait()
            for _, arrive in remote:
                arrive.wait_send()
                arrive.wait_recv()

    ops = ([flight.srcs[i] for i in idx] + [flight.lands[i] for i in idx] + [flight.sends[i] for i in idx]
           + [flight.recvs[i] for i in idx] + [flight.locs[i] for i in idx])
    res = pl.pallas_call(
        body, name=name, out_shape=[pltpu.HBM(o.shape, o.dtype) for o in ops[:2 * n]],
        in_specs=[_HBM] * (2 * n) + [_SEM] * (3 * n) + [pl.BlockSpec(memory_space=pl.ANY)], out_specs=[_HBM] * (2 * n),
        input_output_aliases={i: i for i in range(2 * n)},
        compiler_params=pltpu.CompilerParams(has_side_effects=_EFFECT),
    )(*ops, after)
    return res[n:]


class _Lazy:
    def __init__(self, fn):
        self.fn, self.val = fn, None

    def get(self, after):
        if self.val is None:
            self.val = self.fn(after)
        return self.val


def _matmul(a, b, kind, out_shape, out_dtype, grid, a_spec, b_spec, o_spec, acc_shape, name):
    nk = grid[2]

    def body(a_ref, b_ref, o_ref, *scratch):
        prod = _dot(a_ref[...], b_ref[...], kind)
        if nk == 1:
            o_ref[...] = prod.astype(out_dtype)
        else:
            acc = scratch[0]
            k = pl.program_id(2)

            @pl.when(k == 0)
            def _():
                acc[...] = prod

            @pl.when(k > 0)
            def _():
                acc[...] += prod

            @pl.when(k == nk - 1)
            def _():
                o_ref[...] = acc[...].astype(out_dtype)

    return pl.pallas_call(
        body, name=name, grid=grid, in_specs=[a_spec, b_spec], out_specs=o_spec, out_shape=SDS(out_shape, out_dtype),
        scratch_shapes=[] if nk == 1 else [pltpu.VMEM(acc_shape, F32)],
        compiler_params=_cp(("parallel", "parallel", "arbitrary")))(a, b)


def _bs(shape, fn):
    return pl.BlockSpec(shape, fn)


def _mm_rows(a, w, kind, n_out, out_dtype, name):
    s, k = a.shape
    tm = min(TM, s)
    return _matmul(a, w, kind, (s, n_out), out_dtype, (s // tm, 1, 1),
                   _bs((tm, k), lambda i, j, kk: (i, 0)), _bs(w.shape, lambda i, j, kk: (0, 0)),
                   _bs((tm, n_out), lambda i, j, kk: (i, 0)), None, name)


def _mm_wgrad(a, dy, out_dtype, name):
    s, k = a.shape
    n = dy.shape[1]
    tk = min(TM, s)
    return _matmul(a, dy, "tn", (k, n), out_dtype, (1, 1, s // tk),
                   _bs((tk, k), lambda i, j, kk: (kk, 0)), _bs((tk, n), lambda i, j, kk: (kk, 0)),
                   _bs((k, n), lambda i, j, kk: (0, 0)), (k, n), name)


def _ada_fwd(c_all, w_ada):
    ncol = w_ada.shape[2]

    def body(c_ref, w_ref, o_ref):
        ca = _silu(c_ref[...])
        ca = jnp.concatenate([ca, jnp.zeros_like(ca)], axis=0)
        o_ref[...] = _dot(ca, w_ref[...], "nn")[:N_DEV, :]

    return pl.pallas_call(
        body, name="ada_fwd", grid=(N_LAYER,),
        in_specs=[pl.BlockSpec((N_DEV, D), lambda l: (0, 0)), pl.BlockSpec((None, D, ncol), lambda l: (l, 0, 0))],
        out_specs=pl.BlockSpec((None, N_DEV, ncol), lambda l: (l, 0, 0)),
        out_shape=SDS((N_LAYER, N_DEV, ncol), F32), compiler_params=_cp(("parallel",)))(c_all, w_ada)


def _ada_finish(m_mine, b_ada):
    def body(m_ref, b_ref, o_ref, t_ref):
        o_ref[...] = m_ref[...] + b_ref[...]
        t_ref[...] = jnp.zeros_like(t_ref)

    return pl.pallas_call(body, name="ada_finish", out_shape=[SDS(b_ada.shape, F32), SDS((8, 128), F32)])(m_mine, b_ada)


def _ada_bwd(c_all, dmod_cols):
    ncol = dmod_cols.shape[2]

    def body(c_ref, d_ref, o_ref):
        ca = _silu(c_ref[...])
        ca = jnp.concatenate([ca, jnp.zeros_like(ca)], axis=0)
        dm = d_ref[...]
        dm = jnp.concatenate([dm, jnp.zeros_like(dm)], axis=0)
        o_ref[...] = _dot(ca, dm, "tn")

    return pl.pallas_call(
        body, name="ada_bwd", grid=(N_LAYER,),
        in_specs=[pl.BlockSpec((N_DEV, D), lambda l: (0, 0)), pl.BlockSpec((None, N_DEV, ncol), lambda l: (l, 0, 0))],
        out_specs=pl.BlockSpec((None, D, ncol), lambda l: (l, 0, 0)),
        out_shape=SDS((N_LAYER, D, ncol), F32), compiler_params=_cp(("parallel",)))(c_all, dmod_cols)


def _rows(s):
    tm = min(TM, s)
    return tm, pl.BlockSpec((tm, D), lambda i: (i, 0))


def _rms_mod(x, g, sc, sh, name):
    s = x.shape[0]
    tm, row = _rows(s)

    def body(x_ref, g_ref, sc_ref, sh_ref, h_ref):
        xf = x_ref[...]
        h_ref[...] = (xf * _rstd(xf) * (g_ref[...] * (1.0 + sc_ref[...])) + sh_ref[...]).astype(BF16)

    return pl.pallas_call(
        body, name=name, grid=(s // tm,), in_specs=[row, _vec(*g, 1), _vec(*sc, 1), _vec(*sh, 1)], out_specs=row,
        out_shape=SDS((s, D), BF16), compiler_params=_cp(("parallel",)))(x, g[0], sc[0], sh[0])


def _post(xres, o, ga, gpost, gn, scn, shn, name):
    s = xres.shape[0]
    tm, row = _rows(s)

    def body(x_ref, o_ref, ga_ref, gp_ref, gn_ref, sc_ref, sh_ref, xn_ref, h_ref):
        of = o_ref[...]
        xn = x_ref[...] + ga_ref[...] * (of * _rstd(of) * gp_ref[...])
        xn_ref[...] = xn
        h_ref[...] = (xn * _rstd(xn) * (gn_ref[...] * (1.0 + sc_ref[...])) + sh_ref[...]).astype(BF16)

    return pl.pallas_call(
        body, name=name, grid=(s // tm,),
        in_specs=[row, row, _vec(*ga, 1), _vec(*gpost, 1), _vec(*gn, 1), _vec(*scn, 1), _vec(*shn, 1)],
        out_specs=[row, row], out_shape=[SDS((s, D), F32), SDS((s, D), BF16)],
        compiler_params=_cp(("parallel",)))(xres, o, ga[0], gpost[0], gn[0], scn[0], shn[0])


def _post_loss(xres, o, ga, gpost, target, name):
    s = xres.shape[0]
    tm, row = _rows(s)

    def body(x_ref, o_ref, ga_ref, gp_ref, t_ref, dy_ref, loss_ref):
        of = o_ref[...]
        err = x_ref[...] + ga_ref[...] * (of * _rstd(of) * gp_ref[...]) - t_ref[...]
        dy_ref[...] = err * (1.0 / D)

        @pl.when(pl.program_id(0) == 0)
        def _():
            loss_ref[...] = jnp.zeros_like(loss_ref)

        loss_ref[...] += jnp.sum(jnp.mean(err * err, axis=-1, keepdims=True), axis=0, keepdims=True) * 0.5

    return pl.pallas_call(
        body, name=name, grid=(s // tm,), in_specs=[row, row, _vec(*ga, 1), _vec(*gpost, 1), row],
        out_specs=[row, pl.BlockSpec((8, 128), lambda i: (0, 0))], out_shape=[SDS((s, D), F32), SDS((8, 128), F32)],
        compiler_params=_cp(("arbitrary",)))(xres, o, ga[0], gpost[0], target)


def _acc(ref, val, first):
    @pl.when(first)
    def _():
        ref[...] = val

    @pl.when(jnp.logical_not(first))
    def _():
        ref[...] += val


def _post_bwd(dxn, o, ga, gpost, name):
    s = dxn.shape[0]
    tm, row = _rows(s)
    vec = pl.BlockSpec((1, D), lambda i: (0, 0))

    def body(d_ref, o_ref, ga_ref, gp_ref, do_ref, dga_ref, dgp_ref):
        of, dx = o_ref[...], d_ref[...]
        r = _rstd(of)
        oh = of * r
        do_ref[...] = _rms_bwd(oh, r, dx * (ga_ref[...] * gp_ref[...])).astype(BF16)
        cs = _colsum(dx * oh)
        first = pl.program_id(0) == 0
        _acc(dga_ref, cs * gp_ref[...], first)
        _acc(dgp_ref, cs * ga_ref[...], first)

    return pl.pallas_call(
        body, name=name, grid=(s // tm,), in_specs=[row, row, _vec(*ga, 1), _vec(*gpost, 1)], out_specs=[row, vec, vec],
        out_shape=[SDS((s, D), BF16), SDS((1, D), F32), SDS((1, D), F32)],
        compiler_params=_cp(("arbitrary",)))(dxn, o, ga[0], gpost[0])


def _pre_bwd(dh, x, dres, g, sc, name):
    s = x.shape[0]
    tm, row = _rows(s)
    vec = pl.BlockSpec((1, D), lambda i: (0, 0))

    def body(dh_ref, x_ref, dr_ref, g_ref, sc_ref, dx_ref, dsh_ref, dsc_ref, dg_ref):
        xf, d = x_ref[...], dh_ref[...]
        r = _rstd(xf)
        xh = xf * r
        dx_ref[...] = dr_ref[...] + _rms_bwd(xh, r, d * (g_ref[...] * (1.0 + sc_ref[...])))
        cs = _colsum(d * xh)
        first = pl.program_id(0) == 0
        _acc(dsh_ref, _colsum(d), first)
        _acc(dsc_ref, cs * g_ref[...], first)
        _acc(dg_ref, cs * (1.0 + sc_ref[...]), first)

    return pl.pallas_call(
        body, name=name, grid=(s // tm,), in_specs=[row, row, row, _vec(*g, 1), _vec(*sc, 1)],
        out_specs=[row, vec, vec, vec],
        out_shape=[SDS((s, D), F32), SDS((1, D), F32), SDS((1, D), F32), SDS((1, D), F32)],
        compiler_params=_cp(("arbitrary",)))(dh, x, dres, g[0], sc[0])


def _gnorm(ys, gg, name):
    s = ys[0].shape[0]
    tm = min(TM, s)
    yb = pl.BlockSpec((tm, GW), lambda i: (i, 0))

    def body(y0, y1, y2, y3, g_ref, o_ref):
        for i, yr in enumerate((y0, y1, y2, y3)):
            y = yr[...]
            o_ref[:, GW * i:GW * (i + 1)] = (y * _rstd(y) * g_ref[:, GW * i:GW * (i + 1)]).astype(BF16)

    return pl.pallas_call(
        body, name=name, grid=(s // tm,), in_specs=[yb] * 4 + [_vec(*gg, 1)], out_specs=pl.BlockSpec((tm, D), lambda i: (i, 0)),
        out_shape=SDS((s, D), BF16), compiler_params=_cp(("parallel",)))(*ys, gg[0])


def _gnorm_bwd(dyn, ys, gg, name):
    s = ys[0].shape[0]
    tm = min(TM, s)
    yb = pl.BlockSpec((tm, GW), lambda i: (i, 0))

    def body(d_ref, y0, y1, y2, y3, g_ref, o0, o1, o2, o3, dg_ref):
        first = pl.program_id(0) == 0
        for i, (yr, orf) in enumerate(zip((y0, y1, y2, y3), (o0, o1, o2, o3))):
            y = yr[...]
            d = d_ref[:, GW * i:GW * (i + 1)]
            r = _rstd(y)
            yh = y * r
            orf[...] = _rms_bwd(yh, r, d * g_ref[:, GW * i:GW * (i + 1)])
            cs = _colsum(d * yh)

            @pl.when(first)
            def _():
                dg_ref[:, GW * i:GW * (i + 1)] = cs

            @pl.when(jnp.logical_not(first))
            def _():
                dg_ref[:, GW * i:GW * (i + 1)] += cs

    return pl.pallas_call(
        body, name=name, grid=(s // tm,), in_specs=[pl.BlockSpec((tm, D), lambda i: (i, 0))] + [yb] * 4 + [_vec(*gg, 1)],
        out_specs=[yb] * 4 + [pl.BlockSpec((1, D), lambda i: (0, 0))],
        out_shape=[SDS((s, GW), F32)] * 4 + [SDS((1, D), F32)], compiler_params=_cp(("arbitrary",)))(dyn, *ys, gg[0])


def _lane_put(acc, col, h):
    lane = lax.broadcasted_iota(jnp.int32, acc.shape, 1)
    return jnp.where(lane == h, col, acc)


def _fgate(z, bf, name):
    s = z.shape[0]

    def body(z_ref, b_ref, fc_ref, fr_ref):
        xg = z_ref[...] + b_ref[...]
        lf = jnp.minimum(xg, 0.0) - jnp.log(1.0 + jnp.exp(-jnp.abs(xg)))
        lane = lax.broadcasted_iota(jnp.int32, lf.shape, 1)
        row = lax.broadcasted_iota(jnp.int32, lf.shape, 0)
        f = jnp.where(lane < 4, lf, 0.0)
        sh = 1
        while sh < s:
            f = f + jnp.where(row >= sh, pltpu.roll(f, sh, 0), 0.0)
            sh *= 2
        fc_ref[...] = f
        fr_ref[...] = f.T

    return pl.pallas_call(
        body, name=name, grid=(1,),
        in_specs=[pl.BlockSpec((s, 128), lambda i: (0, Z_FG)), pl.BlockSpec((1, 128), lambda i: (0, 0))],
        out_specs=[pl.BlockSpec((s, 128), lambda i: (0, 0)), pl.BlockSpec((128, s), lambda i: (0, 0))],
        out_shape=[SDS((s, 128), F32), SDS((128, s), F32)], compiler_params=_cp(("arbitrary",)))(z, bf)


def _fgate_bwd(z, bf, dfrow, dfcol, name):
    s = z.shape[0]

    def body(z_ref, b_ref, d_ref, dc_ref, dz_ref, db_ref):
        d = jnp.concatenate([d_ref[...], jnp.zeros((120, s), F32)], axis=0).T + dc_ref[...]
        row = lax.broadcasted_iota(jnp.int32, d.shape, 0)
        lane = lax.broadcasted_iota(jnp.int32, d.shape, 1)
        sh = 1
        while sh < s:
            d = d + jnp.where(row < s - sh, pltpu.roll(d, s - sh, 0), 0.0)
            sh *= 2
        xg = z_ref[...] + b_ref[...]
        dz = jnp.where(lane < 4, d * _sigmoid(-xg), 0.0)
        dz_ref[...] = dz.astype(BF16)
        db_ref[...] = _colsum(dz)

    return pl.pallas_call(
        body, name=name, grid=(1,),
        in_specs=[pl.BlockSpec((s, 128), lambda i: (0, Z_FG)), pl.BlockSpec((1, 128), lambda i: (0, 0)),
                  pl.BlockSpec((8, s), lambda i: (0, 0)), pl.BlockSpec((s, 128), lambda i: (0, 0))],
        out_specs=[pl.BlockSpec((s, 128), lambda i: (0, 0)), pl.BlockSpec((1, 128), lambda i: (0, 0))],
        out_shape=[SDS((s, 128), BF16), SDS((1, 128), F32)], compiler_params=_cp(("arbitrary",)))(z, bf, dfrow, dfcol)


def _fox_scores(q_ref, k_ref, fc_ref, fr_ref, h, i, tq, s):
    q = q_ref[:, HD * h:HD * (h + 1)] * SCALE
    sc = _dot(q, k_ref[:, HD * h:HD * (h + 1)], "nt")
    sc = sc + fc_ref[:, h:h + 1] - fr_ref[h:h + 1, :]
    qpos = i * tq + lax.broadcasted_iota(jnp.int32, (tq, s), 0)
    kpos = lax.broadcasted_iota(jnp.int32, (tq, s), 1)
    return q, jnp.where(kpos <= qpos, sc, NEG)


def _fox_fwd(z, fcol, frow, name):
    s = z.shape[0]
    tq = min(TQ, s)

    def body(q_ref, k_ref, v_ref, fc_ref, fr_ref, y_ref, l_ref):
        i = pl.program_id(0)
        lse = jnp.zeros((tq, 128), F32)
        for h in range(4):
            _, sc = _fox_scores(q_ref, k_ref, fc_ref, fr_ref, h, i, tq, s)
            m = jnp.max(sc, axis=-1, keepdims=True)
            p = jnp.exp(sc - m)
            l = jnp.sum(p, axis=-1, keepdims=True)
            y_ref[:, HD * h:HD * (h + 1)] = _dot(p, v_ref[:, HD * h:HD * (h + 1)], "nn") / l
            lse = _lane_put(lse, m + jnp.log(l), h)
        l_ref[...] = lse

    return pl.pallas_call(
        body, name=name, grid=(s // tq,),
        in_specs=[pl.BlockSpec((tq, GW), lambda i: (i, Z_FQ)), pl.BlockSpec((s, GW), lambda i: (0, Z_FK)),
                  pl.BlockSpec((s, GW), lambda i: (0, Z_FV)), pl.BlockSpec((tq, 128), lambda i: (i, 0)),
                  pl.BlockSpec((8, s), lambda i: (0, 0))],
        out_specs=[pl.BlockSpec((tq, GW), lambda i: (i, 0)), pl.BlockSpec((tq, 128), lambda i: (i, 0))],
        out_shape=[SDS((s, GW), F32), SDS((s, 128), F32)], compiler_params=_cp(("parallel",)))(z, z, z, fcol, frow)


def _fox_bwd(z, fcol, frow, lse, y, dy, name):
    s = z.shape[0]
    tq = min(TQ, s)

    def body(q_ref, k_ref, v_ref, fc_ref, fr_ref, l_ref, y_ref, dy_ref, dq_ref, dk_ref, dv_ref, df_ref, dfq_ref):
        i = pl.program_id(0)

        @pl.when(i == 0)
        def _():
            dk_ref[...] = jnp.zeros_like(dk_ref)
            dv_ref[...] = jnp.zeros_like(dv_ref)
            df_ref[...] = jnp.zeros_like(df_ref)

        dfq = jnp.zeros((tq, 128), F32)
        for h in range(4):
            hs = slice(HD * h, HD * (h + 1))
            q, sc = _fox_scores(q_ref, k_ref, fc_ref, fr_ref, h, i, tq, s)
            p = jnp.exp(sc - l_ref[:, h:h + 1])
            dyh = dy_ref[:, hs]
            dd = jnp.sum(dyh * y_ref[:, hs], axis=-1, keepdims=True)
            ds = p * (_dot(dyh, v_ref[:, hs], "nt") - dd)
            dq_ref[:, hs] = _dot(ds, k_ref[:, hs], "nn") * SCALE
            dk_ref[:, hs] += _dot(ds, q, "tn")
            dv_ref[:, hs] += _dot(p, dyh, "tn")
            df_ref[h:h + 1, :] -= _colsum(ds)
            dfq = _lane_put(dfq, jnp.sum(ds, axis=-1, keepdims=True), h)
        dfq_ref[...] = dfq

    tile = lambda w: pl.BlockSpec((tq, w), lambda i: (i, 0))
    full = pl.BlockSpec((s, GW), lambda i: (0, 0))
    return pl.pallas_call(
        body, name=name, grid=(s // tq,),
        in_specs=[pl.BlockSpec((tq, GW), lambda i: (i, Z_FQ)), pl.BlockSpec((s, GW), lambda i: (0, Z_FK)),
                  pl.BlockSpec((s, GW), lambda i: (0, Z_FV)), tile(128), pl.BlockSpec((8, s), lambda i: (0, 0)),
                  tile(128), tile(GW), tile(GW)],
        out_specs=[tile(GW), full, full, pl.BlockSpec((8, s), lambda i: (0, 0)), tile(128)],
        out_shape=[SDS((s, GW), F32), SDS((s, GW), F32), SDS((s, GW), F32), SDS((8, s), F32), SDS((s, 128), F32)],
        compiler_params=_cp(("arbitrary",)))(z, z, z, fcol, frow, lse, y, dy)


def _swa_block(q_ref, k_ref, v_ref, n):
    qs = pl.multiple_of(n * WIN, WIN)
    ks = pl.multiple_of(jnp.maximum(n - 1, 0) * WIN, WIN)
    qb = q_ref[pl.ds(qs, WIN), :]
    kb = k_ref[pl.ds(ks, 2 * WIN), :]
    vb = v_ref[pl.ds(ks, 2 * WIN), :]
    dist = (qs + lax.broadcasted_iota(jnp.int32, (WIN, 2 * WIN), 0)) - (ks + lax.broadcasted_iota(jnp.int32, (WIN, 2 * WIN), 1))
    return qs, ks, qb, kb, vb, (dist >= 0) & (dist < WIN)


def _swa_fwd(z, sinks, name):
    s = z.shape[0]

    def body(sink_ref, q_ref, k_ref, v_ref, y_ref, l_ref):
        def step(n, carry):
            qs, ks, qb, kb, vb, valid = _swa_block(q_ref, k_ref, v_ref, n)
            lse = jnp.zeros((WIN, 128), F32)
            for h in range(4):
                kv = slice(HD * (h // 2), HD * (h // 2 + 1))
                sc = jnp.where(valid, _dot(qb[:, HD * h:HD * (h + 1)] * SCALE, kb[:, kv], "nt"), NEG)
                sink = sink_ref[h]
                m = jnp.maximum(jnp.max(sc, axis=-1, keepdims=True), sink)
                p = jnp.exp(sc - m)
                den = jnp.sum(p, axis=-1, keepdims=True) + jnp.exp(sink - m)
                y_ref[pl.ds(qs, WIN), HD * h:HD * (h + 1)] = _dot(p, vb[:, kv], "nn") / den
                lse = _lane_put(lse, m + jnp.log(den), h)
            l_ref[pl.ds(qs, WIN), :] = lse
            return carry

        lax.fori_loop(0, s // WIN, step, 0)

    return pl.pallas_call(
        body, name=name, grid=(1,),
        in_specs=[pl.BlockSpec(memory_space=pltpu.SMEM), pl.BlockSpec((s, GW), lambda i: (0, Z_SQ)),
                  pl.BlockSpec((s, 128), lambda i: (0, Z_SK)), pl.BlockSpec((s, 128), lambda i: (0, Z_SV))],
        out_specs=[pl.BlockSpec((s, GW), lambda i: (0, 0)), pl.BlockSpec((s, 128), lambda i: (0, 0))],
        out_shape=[SDS((s, GW), F32), SDS((s, 128), F32)], compiler_params=_cp(("arbitrary",)))(sinks, z, z, z)


def _swa_bwd(z, sinks, lse, y, dy, name):
    s = z.shape[0]

    def body(sink_ref, q_ref, k_ref, v_ref, l_ref, y_ref, dy_ref, dq_ref, dk_ref, dv_ref, dsink_ref):
        dk_ref[...] = jnp.zeros_like(dk_ref)
        dv_ref[...] = jnp.zeros_like(dv_ref)
        dsink_ref[...] = jnp.zeros_like(dsink_ref)

        def step(n, carry):
            qs, ks, qb, kb, vb, valid = _swa_block(q_ref, k_ref, v_ref, n)
            lse_b = l_ref[pl.ds(qs, WIN), :]
            yb = y_ref[pl.ds(qs, WIN), :]
            dyb = dy_ref[pl.ds(qs, WIN), :]
            dsink = jnp.zeros((1, 128), F32)
            for h in range(4):
                hs = slice(HD * h, HD * (h + 1))
                kv = slice(HD * (h // 2), HD * (h // 2 + 1))
                q = qb[:, hs] * SCALE
                sc = jnp.where(valid, _dot(q, kb[:, kv], "nt"), NEG)
                lh = lse_b[:, h:h + 1]
                p = jnp.exp(sc - lh)
                dd = jnp.sum(dyb[:, hs] * yb[:, hs], axis=-1, keepdims=True)
                ds = p * (_dot(dyb[:, hs], vb[:, kv], "nt") - dd)
                dq_ref[pl.ds(qs, WIN), hs] = _dot(ds, kb[:, kv], "nn") * SCALE
                dk_ref[pl.ds(ks, 2 * WIN), kv] += _dot(ds, q, "tn")
                dv_ref[pl.ds(ks, 2 * WIN), kv] += _dot(p, dyb[:, hs], "tn")
                dsink = _lane_put(dsink, dsink[:, h:h + 1] - jnp.sum(jnp.exp(sink_ref[h] - lh) * dd, axis=0, keepdims=True), h)
            dsink_ref[...] += dsink
            return carry

        lax.fori_loop(0, s // WIN, step, 0)

    full = lambda w: pl.BlockSpec((s, w), lambda i: (0, 0))
    return pl.pallas_call(
        body, name=name, grid=(1,),
        in_specs=[pl.BlockSpec(memory_space=pltpu.SMEM), pl.BlockSpec((s, GW), lambda i: (0, Z_SQ)),
                  pl.BlockSpec((s, 128), lambda i: (0, Z_SK)), pl.BlockSpec((s, 128), lambda i: (0, Z_SV)),
                  full(128), full(GW), full(GW)],
        out_specs=[full(GW), full(128), full(128), pl.BlockSpec((1, 128), lambda i: (0, 0))],
        out_shape=[SDS((s, GW), F32), SDS((s, 128), F32), SDS((s, 128), F32), SDS((1, 128), F32)],
        compiler_params=_cp(("arbitrary",)))(sinks, z, z, z, lse, y, dy)


def _delayed(win, shift, halo):
    return win[halo:, :] if shift == 0 else pltpu.roll(win, shift, 0)[halo:, :]


def _prev_halo(width, halo, tm, col):
    return pl.BlockSpec((halo, width), lambda i: (jnp.maximum(i * (tm // halo) - 1, 0), col))


def _glu_window(a_ref, g_ref, ah_ref, gh_ref):
    keep = (pl.program_id(0) > 0).astype(F32)
    a = jnp.concatenate([ah_ref[...] * keep, a_ref[...]], axis=0)
    g = jnp.concatenate([gh_ref[...], g_ref[...]], axis=0)
    return a * _sigmoid(g)


def _conv_fwd(z, cw, cb, lg, lb, pw, pb, name):
    s = z.shape[0]
    tm = min(TM, s)

    def body(a_ref, g_ref, ah_ref, gh_ref, w_ref, b_ref, lg_ref, lb_ref, pw_ref, pb_ref, y_ref, hc_ref):
        hg = _glu_window(a_ref, g_ref, ah_ref, gh_ref)
        hc = jnp.zeros((tm, GW), F32) + b_ref[...]
        for k in range(CONV_K):
            hc = hc + w_ref[k:k + 1, :] * _delayed(hg, CONV_K - 1 - k, CONV_HALO)
        hc_ref[...] = hc
        xh, _ = _ln_stats(hc)
        y_ref[...] = _dot(_silu(xh * lg_ref[...] + lb_ref[...]), pw_ref[...], "nn") + pb_ref[...]

    tile = lambda col: pl.BlockSpec((tm, GW), lambda i: (i, col))
    whole = lambda a: pl.BlockSpec(a.shape, lambda i: (0, 0))
    return pl.pallas_call(
        body, name=name, grid=(s // tm,),
        in_specs=[tile(Z_CA), tile(Z_CG), _prev_halo(GW, CONV_HALO, tm, Z_CA), _prev_halo(GW, CONV_HALO, tm, Z_CG),
                  whole(cw), whole(cb), whole(lg), whole(lb), whole(pw), whole(pb)],
        out_specs=[tile(0), tile(0)], out_shape=[SDS((s, GW), F32), SDS((s, GW), F32)],
        compiler_params=_cp(("parallel",)))(z, z, z, z, cw, cb, lg, lb, pw, pb)


def _conv_bwd_a(z, hc, dy, cw, lg, lb, pw, name):
    s = z.shape[0]
    tm = min(TM, s)

    def body(a_ref, g_ref, ah_ref, gh_ref, hc_ref, dy_ref, lg_ref, lb_ref, pw_ref,
             dhc_ref, dpw_ref, dpb_ref, dlg_ref, dlb_ref, dcw_ref, dcb_ref):
        first = pl.program_id(0) == 0
        dy = dy_ref[...]
        xh, rstd = _ln_stats(hc_ref[...])
        hn = xh * lg_ref[...] + lb_ref[...]
        dhn = _dot(dy, pw_ref[...], "nt") * _dsilu(hn)
        dhc = _ln_bwd(xh, rstd, dhn * lg_ref[...])
        dhc_ref[...] = dhc
        _acc(dpw_ref, _dot(_silu(hn), dy, "tn"), first)
        _acc(dpb_ref, _colsum(dy), first)
        _acc(dlg_ref, _colsum(dhn * xh), first)
        _acc(dlb_ref, _colsum(dhn), first)
        _acc(dcb_ref, _colsum(dhc), first)
        hg = _glu_window(a_ref, g_ref, ah_ref, gh_ref)

        @pl.when(first)
        def _():
            dcw_ref[...] = jnp.zeros_like(dcw_ref)

        for k in range(CONV_K):
            dcw_ref[k:k + 1, :] += _colsum(dhc * _delayed(hg, CONV_K - 1 - k, CONV_HALO))

    tile = lambda col: pl.BlockSpec((tm, GW), lambda i: (i, col))
    whole = lambda shape: pl.BlockSpec(shape, lambda i: (0, 0))
    return pl.pallas_call(
        body, name=name, grid=(s // tm,),
        in_specs=[tile(Z_CA), tile(Z_CG), _prev_halo(GW, CONV_HALO, tm, Z_CA), _prev_halo(GW, CONV_HALO, tm, Z_CG),
                  tile(0), tile(0), whole(lg.shape), whole(lb.shape), whole(pw.shape)],
        out_specs=[tile(0), whole((GW, GW)), whole((1, GW)), whole((1, GW)), whole((1, GW)), whole((32, GW)), whole((1, GW))],
        out_shape=[SDS((s, GW), F32), SDS((GW, GW), F32), SDS((1, GW), F32), SDS((1, GW), F32), SDS((1, GW), F32),
                   SDS((32, GW), F32), SDS((1, GW), F32)],
        compiler_params=_cp(("arbitrary",)))(z, z, z, z, hc, dy, lg, lb, pw)


def _conv_bwd_b(z, dhc, cw, name):
    s = z.shape[0]
    tm = min(TM, s)
    nt = s // tm

    def body(a_ref, g_ref, d_ref, dn_ref, w_ref, da_ref, dg_ref):
        keep = (pl.program_id(0) < nt - 1).astype(F32)
        win = jnp.concatenate([d_ref[...], dn_ref[...] * keep], axis=0)
        dhg = jnp.zeros((tm, GW), F32)
        for k in range(CONV_K):
            sh = CONV_K - 1 - k
            dhg = dhg + w_ref[k:k + 1, :] * (win[:tm, :] if sh == 0 else pltpu.roll(win, tm + CONV_HALO - sh, 0)[:tm, :])
        sg = _sigmoid(g_ref[...])
        da_ref[...] = (dhg * sg).astype(BF16)
        dg_ref[...] = (dhg * a_ref[...] * sg * (1.0 - sg)).astype(BF16)

    tile = lambda col: pl.BlockSpec((tm, GW), lambda i: (i, col))
    nxt = pl.BlockSpec((CONV_HALO, GW), lambda i: (jnp.minimum((i + 1) * (tm // CONV_HALO), s // CONV_HALO - 1), 0))
    return pl.pallas_call(
        body, name=name, grid=(nt,),
        in_specs=[tile(Z_CA), tile(Z_CG), tile(0), nxt, pl.BlockSpec(cw.shape, lambda i: (0, 0))],
        out_specs=[tile(0), tile(0)], out_shape=[SDS((s, GW), BF16), SDS((s, GW), BF16)],
        compiler_params=_cp(("parallel",)))(z, z, dhc, dhc, cw)


def _sgu_chunk(zu, zv, lg, lb, wcat, bfull):
    u, v = _gelu(zu), _gelu(zv)
    xh, rstd = _ln_stats(v)
    vn = xh * lg + lb
    lane = lax.shift_right_logical(lax.broadcasted_iota(jnp.int32, (WIN, GW), 1), 6)
    r = jnp.concatenate([jnp.where(lane == g, vn, 0.0) for g in range(4)], axis=0)
    mix = _dot(wcat, r, "nn") + bfull
    return u, xh, rstd, r, mix, lane


def _tril4(w):
    t = lax.broadcasted_iota(jnp.int32, w.shape, 0)
    sidx = lax.broadcasted_iota(jnp.int32, w.shape, 1) & (WIN - 1)
    return jnp.where(sidx <= t, w, 0.0)


def _sgu_fwd(z, lg, lb, wcat, bfull, name):
    s = z.shape[0]
    tm = min(TM, s)

    def body(u_ref, v_ref, lg_ref, lb_ref, w_ref, b_ref, y_ref):
        w = _tril4(w_ref[...])
        for n in range(tm // WIN):
            rows = slice(WIN * n, WIN * (n + 1))
            u, _, _, _, mix, _ = _sgu_chunk(u_ref[rows, :], v_ref[rows, :], lg_ref[...], lb_ref[...], w, b_ref[...])
            y_ref[rows, :] = u * mix

    tile = lambda col: pl.BlockSpec((tm, GW), lambda i: (i, col))
    whole = lambda a: pl.BlockSpec(a.shape, lambda i: (0, 0))
    return pl.pallas_call(
        body, name=name, grid=(s // tm,), in_specs=[tile(Z_GU), tile(Z_GV), whole(lg), whole(lb), whole(wcat), whole(bfull)],
        out_specs=tile(0), out_shape=SDS((s, GW), F32), compiler_params=_cp(("parallel",)))(z, z, lg, lb, wcat, bfull)


def _sgu_bwd(z, dy, lg, lb, wcat, bfull, name):
    s = z.shape[0]
    tm = min(TM, s)

    def body(u_ref, v_ref, dy_ref, lg_ref, lb_ref, w_ref, b_ref, du_ref, dv_ref, dw_ref, db_ref, dlg_ref, dlb_ref):
        first = pl.program_id(0) == 0
        w = _tril4(w_ref[...])
        wt = w.T
        dw = jnp.zeros((WIN, 4 * WIN), F32)
        db = jnp.zeros((WIN, 128), F32)
        dlg = jnp.zeros((1, GW), F32)
        dlb = jnp.zeros((1, GW), F32)
        for n in range(tm // WIN):
            rows = slice(WIN * n, WIN * (n + 1))
            zu, zv, dout = u_ref[rows, :], v_ref[rows, :], dy_ref[rows, :]
            u, xh, rstd, r, mix, lane = _sgu_chunk(zu, zv, lg_ref[...], lb_ref[...], w, b_ref[...])
            dmix = dout * u
            du_ref[rows, :] = (dout * mix * _dgelu(zu)).astype(BF16)
            dw = dw + _dot(dmix, r, "nt")
            for g in range(4):
                db = _lane_put(db, db[:, g:g + 1] + jnp.sum(dmix[:, HD * g:HD * (g + 1)], axis=1, keepdims=True), g)
            dr = _dot(wt, dmix, "nn")
            dvn = jnp.zeros((WIN, GW), F32)
            for g in range(4):
                dvn = dvn + jnp.where(lane == g, dr[WIN * g:WIN * (g + 1), :], 0.0)
            dlg = dlg + _colsum(dvn * xh)
            dlb = dlb + _colsum(dvn)
            dv_ref[rows, :] = (_ln_bwd(xh, rstd, dvn * lg_ref[...]) * _dgelu(zv)).astype(BF16)
        _acc(dw_ref, _tril4(dw), first)
        _acc(db_ref, db, first)
        _acc(dlg_ref, dlg, first)
        _acc(dlb_ref, dlb, first)

    tile = lambda col: pl.BlockSpec((tm, GW), lambda i: (i, col))
    whole = lambda shape: pl.BlockSpec(shape, lambda i: (0, 0))
    return pl.pallas_call(
        body, name=name, grid=(s // tm,),
        in_specs=[tile(Z_GU), tile(Z_GV), tile(0), whole(lg.shape), whole(lb.shape), whole(wcat.shape), whole(bfull.shape)],
        out_specs=[tile(0), tile(0), whole((WIN, 4 * WIN)), whole((WIN, 128)), whole((1, GW)), whole((1, GW))],
        out_shape=[SDS((s, GW), BF16), SDS((s, GW), BF16), SDS((WIN, 4 * WIN), F32), SDS((WIN, 128), F32),
                   SDS((1, GW), F32), SDS((1, GW), F32)],
        compiler_params=_cp(("arbitrary",)))(z, z, dy, lg, lb, wcat, bfull)


def _conv3(win, w, b, tm):
    return (w[2:3, :] * win[FFN_HALO:, :] + w[1:2, :] * pltpu.roll(win, 1, 0)[FFN_HALO:, :]
            + w[0:1, :] * pltpu.roll(win, 2, 0)[FFN_HALO:, :] + b)


def _ffn_specs(s, tm):
    main = pl.BlockSpec((2, None, tm, FF_BLK), lambda j, i: (0, j, i, 0))
    prev = pl.BlockSpec((2, None, FFN_HALO, FF_BLK), lambda j, i: (0, j, jnp.maximum(i * (tm // FFN_HALO) - 1, 0), 0))
    nxt = pl.BlockSpec((2, None, FFN_HALO, FF_BLK),
                       lambda j, i: (0, j, jnp.minimum((i + 1) * (tm // FFN_HALO), s // FFN_HALO - 1), 0))
    wsp = pl.BlockSpec((2, None, 3, FF_BLK), lambda j, i: (0, j, 0, 0))
    bsp = pl.BlockSpec((2, None, 1, FF_BLK), lambda j, i: (0, j, 0, 0))
    return main, prev, nxt, wsp, bsp


def _ffn_windows(u_ref, uh_ref):
    keep = (pl.program_id(1) > 0).astype(F32)
    return [jnp.concatenate([uh_ref[p] * keep, u_ref[p]], axis=0) for p in range(2)]


def _ffn_act(u4, w4, b4, name):
    s = u4.shape[2]
    tm = min(TM, s)
    main, prev, _, wsp, bsp = _ffn_specs(s, tm)

    def body(u_ref, uh_ref, w_ref, b_ref, o_ref):
        gw, vw = _ffn_windows(u_ref, uh_ref)
        o_ref[...] = (_silu(_conv3(gw, w_ref[0], b_ref[0], tm)) * _conv3(vw, w_ref[1], b_ref[1], tm)).astype(BF16)

    return pl.pallas_call(
        body, name=name, grid=(FF_NBLK, s // tm), in_specs=[main, prev, wsp, bsp],
        out_specs=pl.BlockSpec((None, tm, FF_BLK), lambda j, i: (j, i, 0)), out_shape=SDS((FF_NBLK, s, FF_BLK), BF16),
        compiler_params=_cp(("parallel", "parallel")))(u4, u4, w4, b4)


def _ffn_gate_bwd(u4, dact, w4, b4, name):
    s = u4.shape[2]
    tm = min(TM, s)
    main, prev, _, wsp, bsp = _ffn_specs(s, tm)

    def body(u_ref, uh_ref, d_ref, w_ref, b_ref, duc_ref, dw_ref, db_ref):
        first = pl.program_id(1) == 0
        wins = _ffn_windows(u_ref, uh_ref)
        gc = _conv3(wins[0], w_ref[0], b_ref[0], tm)
        vc = _conv3(wins[1], w_ref[1], b_ref[1], tm)
        d = d_ref[...]
        duc = (d * vc * _dsilu(gc), d * _silu(gc))
        for p in range(2):
            duc_ref[p] = duc[p]

            @pl.when(first)
            def _():
                db_ref[p] = _colsum(duc[p])
                for k in range(3):
                    dw_ref[p, k:k + 1, :] = _colsum(duc[p] * (wins[p] if k == 2 else pltpu.roll(wins[p], 2 - k, 0))[FFN_HALO:, :])

            @pl.when(jnp.logical_not(first))
            def _():
                db_ref[p] += _colsum(duc[p])
                for k in range(3):
                    dw_ref[p, k:k + 1, :] += _colsum(duc[p] * (wins[p] if k == 2 else pltpu.roll(wins[p], 2 - k, 0))[FFN_HALO:, :])

    return pl.pallas_call(
        body, name=name, grid=(FF_NBLK, s // tm),
        in_specs=[main, prev, pl.BlockSpec((None, tm, FF_BLK), lambda j, i: (j, i, 0)), wsp, bsp],
        out_specs=[main, wsp, bsp],
        out_shape=[SDS(u4.shape, F32), SDS((2, FF_NBLK, 3, FF_BLK), F32), SDS((2, FF_NBLK, 1, FF_BLK), F32)],
        compiler_params=_cp(("parallel", "arbitrary")))(u4, u4, dact, w4, b4)


def _ffn_conv_bwd(duc, w4, name):
    s = duc.shape[2]
    tm = min(TM, s)
    nt = s // tm
    main, _, nxt, wsp, _ = _ffn_specs(s, tm)

    def body(d_ref, dn_ref, w_ref, o_ref):
        keep = (pl.program_id(1) < nt - 1).astype(F32)
        for p in range(2):
            win = jnp.concatenate([d_ref[p], dn_ref[p] * keep], axis=0)
            w = w_ref[p]
            o_ref[p] = (w[2:3, :] * win[:tm, :] + w[1:2, :] * pltpu.roll(win, tm + FFN_HALO - 1, 0)[:tm, :]
                        + w[0:1, :] * pltpu.roll(win, tm + FFN_HALO - 2, 0)[:tm, :]).astype(BF16)

    return pl.pallas_call(
        body, name=name, grid=(FF_NBLK, nt), in_specs=[main, nxt, wsp], out_specs=main, out_shape=SDS(duc.shape, BF16),
        compiler_params=_cp(("parallel", "parallel")))(duc, duc, w4)


def _sum8(parts, name):
    _, r, c = parts.shape
    tr = r
    for cand in (512, 256, 128, 64, 32, 16):
        if r % cand == 0 and r > cand:
            tr = cand
            break

    def body(p_ref, o_ref):
        acc = p_ref[0].astype(F32)
        for j in range(1, N_DEV):
            acc = acc + p_ref[j].astype(F32)
        o_ref[...] = acc

    return pl.pallas_call(
        body, name=name, grid=(r // tr,), in_specs=[pl.BlockSpec((N_DEV, tr, c), lambda i: (0, i, 0))],
        out_specs=pl.BlockSpec((tr, c), lambda i: (i, 0)), out_shape=SDS((r, c), F32),
        compiler_params=_cp(("parallel",)))(parts)


def _sum8_small(parts, name):
    n = len(parts)

    def body(*refs):
        for p_ref, o_ref in zip(refs[:n], refs[n:]):
            acc = p_ref[0]
            for j in range(1, N_DEV):
                acc = acc + p_ref[j]
            o_ref[...] = acc

    return pl.pallas_call(body, name=name, out_shape=[SDS(p.shape[1:], F32) for p in parts], compiler_params=_cp())(*parts)


def _adamw_math(w, g, m, v):
    m = ADAM_B1 * m + (1.0 - ADAM_B1) * g
    v = ADAM_B2 * v + (1.0 - ADAM_B2) * (g * g)
    m_hat = m / (1.0 - ADAM_B1 ** ADAM_STEP)
    v_hat = v / (1.0 - ADAM_B2 ** ADAM_STEP)
    return -ADAM_LR * (m_hat / (jnp.sqrt(v_hat) + ADAM_EPS) + ADAM_WD * w), m, v


def _adamw(w, g, m, v, name):
    r, c = w.shape
    tr = r
    for cand in (256, 128, 64, 32, 16, 8):
        if r % cand == 0 and r > cand:
            tr = cand
            break

    def body(w_ref, g_ref, m_ref, v_ref, d_ref, mo_ref, vo_ref):
        d_ref[...], mo_ref[...], vo_ref[...] = _adamw_math(w_ref[...], g_ref[...], m_ref[...], v_ref[...])

    blk = pl.BlockSpec((tr, c), lambda i: (i, 0))
    return pl.pallas_call(body, name=name, grid=(r // tr,), in_specs=[blk] * 4, out_specs=[blk] * 3,
                          out_shape=[SDS((r, c), F32)] * 3, compiler_params=_cp(("parallel",)))(w, g, m, v)


def _adamw_small(ws, gs, ms, vs, name):
    n = len(ws)

    def body(*refs):
        ins, outs = refs[:4 * n], refs[4 * n:]
        for i in range(n):
            d, m, v = _adamw_math(ins[i][...], ins[n + i][...], ins[2 * n + i][...], ins[3 * n + i][...])
            outs[i][...], outs[n + i][...], outs[2 * n + i][...] = d, m, v

    shapes = [SDS(w.shape, F32) for w in ws]
    res = pl.pallas_call(body, name=name, out_shape=shapes * 3, compiler_params=_cp())(*ws, *gs, *ms, *vs)
    return res[:n], res[n:2 * n], res[2 * n:]


def _perm_in(w):
    pad = jnp.zeros(w.shape[:-1] + (ZW - 2308,), w.dtype)
    return jnp.concatenate([w[..., :768], w[..., 772:], w[..., 768:772], pad], axis=-1)


def _unperm_in(g):
    return jnp.concatenate([g[..., :768], g[..., 2304:2308], g[..., 768:2304]], axis=-1)


def _wcat(sgu_w):
    return sgu_w.transpose(1, 0, 2).reshape(WIN, 4 * WIN)


def _layer_fwd(l, x, h1, mod, p, wg, last, target, nxt):
    s = x.shape[0]
    tag = f"_l{l}"
    mrow = lambda k: (mod, 6 * l + k)
    z = _mm_rows(h1, wg["w_in"].get(h1), "nn", ZW, F32, "mm_z" + tag)
    fcol, frow = _fgate(z, p["bf"], "fgate" + tag)
    y_fox, lse_fox = _fox_fwd(z, fcol, frow, "fox_fwd" + tag)
    y_conv, hc = _conv_fwd(z, wg["conv_w"], p["conv_b"], p["conv_ln_g"], p["conv_ln_b"], wg["conv_pw_w"], p["conv_pw_b"],
                           "conv_fwd" + tag)
    y_swa, lse_swa = _swa_fwd(z, p["sinks"], "swa_fwd" + tag)
    y_sgu = _sgu_fwd(z, p["sgu_ln_g"], p["sgu_ln_b"], p["wcat"], p["bfull"], "sgu_fwd" + tag)
    ys = (y_fox, y_conv, y_swa, y_sgu)
    yn = _gnorm(ys, (p["g_group"], l), "gnorm" + tag)
    o = _mm_rows(yn, wg["w_out"].get(yn), "nn", D, F32, "mm_o" + tag)
    x1, h2 = _post(x, o, mrow(2), (p["g_post_mix"], l), (p["g_pre_ffn"], l), mrow(4), mrow(3), "post_mix" + tag)
    tm = min(TM, s)
    u = _matmul(h2, wg["w_up"].get(h2), "nn", (N_DEV, s, FF_BLK), F32, (N_DEV, s // tm, 1),
                _bs((tm, D), lambda j, i, k: (i, 0)), _bs((None, D, FF_BLK), lambda j, i, k: (j, 0, 0)),
                _bs((None, tm, FF_BLK), lambda j, i, k: (j, i, 0)), None, "mm_u" + tag)
    u4 = u.reshape(2, FF_NBLK, s, FF_BLK)
    act = _ffn_act(u4, wg["ffn_conv_w"], p["ffn_conv_b"], "ffn_act" + tag)
    f = _matmul(act, wg["w_down"].get(act), "nn", (s, D), F32, (s // tm, 1, FF_NBLK),
                _bs((None, tm, FF_BLK), lambda i, j, k: (k, i, 0)), _bs((FF_BLK, D), lambda i, j, k: (k, 0)),
                _bs((tm, D), lambda i, j, k: (i, 0)), (tm, D), "mm_f" + tag)
    if last:
        out = _post_loss(x1, f, mrow(5), (p["g_post_ffn"], l), target, "post_loss")
    else:
        out = _post(x1, f, mrow(5), (p["g_post_ffn"], l), *nxt, "post_ffn" + tag)
    saved = dict(x=x, h1=h1, z=z, fcol=fcol, frow=frow, lse_fox=lse_fox, hc=hc, lse_swa=lse_swa, ys=ys, yn=yn, o=o, x1=x1,
                 h2=h2, u4=u4, act=act, f=f)
    return out, saved


def _tie(a, token):
    return a if token is None else a + token[0, 0]


def _layer_bwd(l, dx2, sv, mod, p, wg, emit):
    s = dx2.shape[0]
    tm = min(TM, s)
    tag = f"_l{l}"
    mrow = lambda k: (mod, 6 * l + k)
    g = {}
    df, g["ga2"], g["g_post_ffn"] = _post_bwd(dx2, sv["f"], mrow(5), (p["g_post_ffn"], l), "post_ffn_bwd" + tag)
    dact = _matmul(df, wg["w_down"].get(None), "nt", (FF_NBLK, s, FF_BLK), F32, (FF_NBLK, s // tm, 1),
                   _bs((tm, D), lambda j, i, k: (i, 0)), _bs((FF_BLK, D), lambda j, i, k: (j, 0)),
                   _bs((None, tm, FF_BLK), lambda j, i, k: (j, i, 0)), None, "mm_dact" + tag)
    tk = tm
    tok = emit("w_down", _matmul(sv["act"], df, "tn", (FF_NBLK * FF_BLK, D), BF16, (FF_NBLK, 1, s // tk),
                                 _bs((None, tk, FF_BLK), lambda j, i, k: (j, k, 0)), _bs((tk, D), lambda j, i, k: (k, 0)),
                                 _bs((FF_BLK, D), lambda j, i, k: (j, 0)), (FF_BLK, D), "mm_dwdown" + tag))
    duc, g["ffn_conv_w"], g["ffn_conv_b"] = _ffn_gate_bwd(sv["u4"], dact, wg["ffn_conv_w"], _tie(p["ffn_conv_b"], tok),
                                                          "ffn_gate_bwd" + tag)
    du = _ffn_conv_bwd(duc, wg["ffn_conv_w"], "ffn_conv_bwd" + tag).reshape(N_DEV, s, FF_BLK)
    dh2 = _matmul(du, wg["w_up"].get(None), "nt", (s, D), F32, (s // tm, 1, N_DEV),
                  _bs((None, tm, FF_BLK), lambda i, j, k: (k, i, 0)), _bs((None, D, FF_BLK), lambda i, j, k: (k, 0, 0)),
                  _bs((tm, D), lambda i, j, k: (i, 0)), (tm, D), "mm_dh2" + tag)
    tok = emit("w_up", _matmul(sv["h2"], du, "tn", (N_DEV, D, FF_BLK), BF16, (N_DEV, 1, s // tk),
                               _bs((tk, D), lambda j, i, k: (k, 0)), _bs((None, tk, FF_BLK), lambda j, i, k: (j, k, 0)),
                               _bs((None, D, FF_BLK), lambda j, i, k: (j, 0, 0)), (D, FF_BLK), "mm_dwup" + tag))
    dx1, g["sh2"], g["sc2"], g["g_pre_ffn"] = _pre_bwd(dh2, sv["x1"], dx2, (_tie(p["g_pre_ffn"], tok), l), mrow(4),
                                                       "pre_ffn_bwd" + tag)
    do, g["ga1"], g["g_post_mix"] = _post_bwd(dx1, sv["o"], mrow(2), (p["g_post_mix"], l), "post_mix_bwd" + tag)
    dyn = _mm_rows(do, wg["w_out"].get(None), "nt", D, F32, "mm_dyn" + tag)
    tok = emit("w_out", _mm_wgrad(sv["yn"], do, BF16, "mm_dwout" + tag))
    dy_fox, dy_conv, dy_swa, dy_sgu, g["g_group"] = _gnorm_bwd(dyn, sv["ys"], (_tie(p["g_group"], tok), l), "gnorm_bwd" + tag)
    z = sv["z"]
    dq_f, dk_f, dv_f, dfrow, dfcol = _fox_bwd(z, sv["fcol"], sv["frow"], sv["lse_fox"], sv["ys"][0], dy_fox, "fox_bwd" + tag)
    dgate, g["bf"] = _fgate_bwd(z, p["bf"], dfrow, dfcol, "fgate_bwd" + tag)
    dhc, g["conv_pw_w"], g["conv_pw_b"], g["conv_ln_g"], g["conv_ln_b"], g["conv_w"], g["conv_b"] = _conv_bwd_a(
        z, sv["hc"], dy_conv, wg["conv_w"], p["conv_ln_g"], p["conv_ln_b"], wg["conv_pw_w"], "conv_bwd_a" + tag)
    da_c, dg_c = _conv_bwd_b(z, dhc, wg["conv_w"], "conv_bwd_b" + tag)
    dq_s, dk_s, dv_s, g["sinks"] = _swa_bwd(z, p["sinks"], sv["lse_swa"], sv["ys"][2], dy_swa, "swa_bwd" + tag)
    du_g, dv_g, g["wcat"], g["sgu_bcol"], g["sgu_ln_g"], g["sgu_ln_b"] = _sgu_bwd(
        z, dy_sgu, p["sgu_ln_g"], p["sgu_ln_b"], p["wcat"], p["bfull"], "sgu_bwd" + tag)
    dz = jnp.concatenate([dq_f.astype(BF16), dk_f.astype(BF16), dv_f.astype(BF16), da_c, dg_c, dq_s.astype(BF16), dk_s.astype(BF16),
                          dv_s.astype(BF16), du_g, dv_g, dgate], axis=1)
    dh1 = _mm_rows(dz, wg["w_in"].get(None), "nt", D, F32, "mm_dh1" + tag)
    tok = emit("w_in", _mm_wgrad(sv["h1"], dz, BF16, "mm_dwin" + tag))
    dx, g["sh1"], g["sc1"], g["g_pre_mix"] = _pre_bwd(dh1, sv["x"], dx1, (_tie(p["g_pre_mix"], tok), l), mrow(1),
                                                      "pre_mix_bwd" + tag)
    return dx, g


def _layer_params(l, small, conv_w_full, conv_pw_full, ffn_conv_w_full):
    bf = jnp.pad(small["b_fgate"][l][None, :], ((0, 0), (0, 124)))
    p = dict(
        bf=bf, conv_b=small["conv_b"][l][None], conv_ln_g=small["conv_ln_g"][l][None], conv_ln_b=small["conv_ln_b"][l][None],
        conv_pw_b=small["conv_pw_b"][l][None], sinks=small["swa_sinks"][l], sgu_ln_g=small["sgu_ln_g"][l][None],
        sgu_ln_b=small["sgu_ln_b"][l][None], wcat=_wcat(small["sgu_w"][l]),
        bfull=jnp.repeat(small["sgu_b"][l].T, HD, axis=1),
        ffn_conv_b=small["ffn_conv_b"][l].reshape(2, FF_NBLK, 1, FF_BLK),
        g_group=small["g_group"].reshape(N_LAYER, 1, D), g_post_mix=small["g_post_mix"].reshape(N_LAYER, 1, D),
        g_pre_ffn=small["g_pre_ffn"].reshape(N_LAYER, 1, D), g_post_ffn=small["g_post_ffn"].reshape(N_LAYER, 1, D),
        g_pre_mix=small["g_pre_mix"].reshape(N_LAYER, 1, D))
    wsmall = dict(conv_w=conv_w_full[l], conv_pw_w=conv_pw_full[l].astype(BF16),
                  ffn_conv_w=ffn_conv_w_full[l].reshape(3, 2, FF_NBLK, FF_BLK).transpose(1, 2, 0, 3))
    return p, wsmall


def _local_step(x, target, mod, small, wbig, conv_w_full, conv_pw_full, ffn_conv_w_full, emit):
    ps, wgs = [], []
    for l in range(N_LAYER):
        p, wsmall = _layer_params(l, small, conv_w_full, conv_pw_full, ffn_conv_w_full)
        ps.append(p)
        wgs.append({**wbig[l], **wsmall})
    h = _rms_mod(x, (ps[0]["g_pre_mix"], 0), (mod, 1), (mod, 0), "rms_mod_l0")
    saved = []
    for l in range(N_LAYER):
        last = l == N_LAYER - 1
        nxt = None if last else ((ps[l]["g_pre_mix"], l + 1), (mod, 6 * (l + 1) + 1), (mod, 6 * (l + 1)))
        out, sv = _layer_fwd(l, x, h, mod, ps[l], wgs[l], last, target, nxt)
        saved.append(sv)
        if not last:
            x, h = out
    dx, loss = out
    grads = [None] * N_LAYER
    for l in reversed(range(N_LAYER)):
        dx, grads[l] = _layer_bwd(l, dx, saved[l], mod, ps[l], wgs[l], functools.partial(emit, l))
    return loss, dx, grads


_SMALL = ("b_ada", "g_pre_mix", "g_post_mix", "g_pre_ffn", "g_post_ffn", "b_fgate", "conv_b", "conv_ln_g", "conv_ln_b",
          "conv_pw_b", "swa_sinks", "sgu_ln_g", "sgu_ln_b", "sgu_w", "sgu_b", "g_group", "ffn_conv_b")
_WEIGHTS = ("w_ada", "b_ada", "g_pre_mix", "g_post_mix", "g_pre_ffn", "g_post_ffn", "w_in", "b_fgate", "conv_w", "conv_b",
            "conv_ln_g", "conv_ln_b", "conv_pw_w", "conv_pw_b", "swa_sinks", "sgu_ln_g", "sgu_ln_b", "sgu_w", "sgu_b",
            "g_group", "w_out", "ffn_w_up", "ffn_conv_w", "ffn_conv_b", "ffn_w_down")


def _pad_rows(a, mult):
    r = (-a.shape[0]) % mult
    return a if r == 0 else jnp.concatenate([a, jnp.zeros((r,) + a.shape[1:], a.dtype)], axis=0)


def _view2d(a):
    if a.ndim == 2:
        return a
    return a.reshape(-1, a.shape[-1])


def kernel(x, c, w_ada, b_ada, g_pre_mix, g_post_mix, g_pre_ffn, g_post_ffn, w_in, b_fgate, conv_w, conv_b, conv_ln_g, conv_ln_b, conv_pw_w, conv_pw_b, swa_sinks, sgu_ln_g, sgu_ln_b, sgu_w, sgu_b, g_group, w_out, ffn_w_up, ffn_conv_w, ffn_conv_b, ffn_w_down, loss_target, m_w_ada, m_b_ada, m_g_pre_mix, m_g_post_mix, m_g_pre_ffn, m_g_post_ffn, m_w_in, m_b_fgate, m_conv_w, m_conv_b, m_conv_ln_g, m_conv_ln_b, m_conv_pw_w, m_conv_pw_b, m_swa_sinks, m_sgu_ln_g, m_sgu_ln_b, m_sgu_w, m_sgu_b, m_g_group, m_w_out, m_ffn_w_up, m_ffn_conv_w, m_ffn_conv_b, m_ffn_w_down, v_w_ada, v_b_ada, v_g_pre_mix, v_g_post_mix, v_g_pre_ffn, v_g_post_ffn, v_w_in, v_b_fgate, v_conv_w, v_conv_b, v_conv_ln_g, v_conv_ln_b, v_conv_pw_w, v_conv_pw_b, v_swa_sinks, v_sgu_ln_g, v_sgu_ln_b, v_sgu_w, v_sgu_b, v_g_group, v_w_out, v_ffn_w_up, v_ffn_conv_w, v_ffn_conv_b, v_ffn_w_down):
    env = dict(locals())
    w = {n: env[n] for n in _WEIGHTS}
    mom = {n: env["m_" + n] for n in _WEIGHTS}
    var = {n: env["v_" + n] for n in _WEIGHTS}
    me = 4 * lax.axis_index("x") + 2 * lax.axis_index("y") + lax.axis_index("c")
    x2, target = x[0], loss_target[0]

    (c_all,) = _exchange([c], ["bcast"], "gather_c")
    c_all = c_all.reshape(N_DEV, D)
    (m_all,) = _exchange([_ada_fwd(c_all, w_ada)], ["bcast"], "gather_mod")
    m_mine = lax.dynamic_index_in_dim(m_all, me, axis=2, keepdims=False)
    mod, mod_token = _ada_finish(m_mine.transpose(1, 0, 2).reshape(N_LAYER, 6 * D), b_ada)
    mod = mod.reshape(6 * N_LAYER, 1, D)

    shards = [_tie(conv_w, mod_token), conv_pw_w, ffn_conv_w]
    for l in range(N_LAYER):
        shards += [_perm_in(w_in[l]).astype(BF16), w_out[l].astype(BF16), ffn_w_up[l].astype(BF16), ffn_w_down[l].astype(BF16)]
    flight = _xchg_start(shards, ["bcast"] * len(shards), "gather_weights_start")
    mod = _tie(mod, flight.token)
    g_cw, g_pw, g_fcw = _xchg_wait(flight, [0, 1, 2], mod, "gather_small_wait")
    conv_w_full = g_cw.transpose(1, 2, 0, 3).reshape(N_LAYER, CONV_K, GW)
    conv_pw_full = g_pw.transpose(1, 0, 2, 3).reshape(N_LAYER, GW, GW)
    ffn_conv_w_full = g_fcw.transpose(1, 2, 0, 3).reshape(N_LAYER, 3, N_DEV * FF_BLK)

    def lazy(i, shape, name):
        return _Lazy(lambda after: _xchg_wait(flight, [i], after, name)[0].reshape(shape))

    wbig = [dict(w_in=lazy(3 + 4 * l, (D, ZW), f"wait_w_in_l{l}"), w_out=lazy(4 + 4 * l, (D, D), f"wait_w_out_l{l}"),
                 w_up=lazy(5 + 4 * l, (N_DEV, D, FF_BLK), f"wait_w_up_l{l}"),
                 w_down=lazy(6 + 4 * l, (FF_NBLK * FF_BLK, D), f"wait_w_down_l{l}")) for l in range(N_LAYER)]

    grad_flights = []

    def emit(l, key, arr):
        fl = _xchg_start([arr.reshape(N_DEV, -1, arr.shape[-1])], ["a2a"], f"grad_start_{key}_l{l}")
        grad_flights.append(((l, key), fl))
        return fl.token

    small = {n: w[n] for n in _SMALL}
    loss8, dx, grads = _local_step(x2, target, mod, small, wbig, conv_w_full, conv_pw_full, ffn_conv_w_full, emit)
    loss = lax.psum(loss8[0, 0], ("x", "y", "c"))
    grad_x = dx[None]
    big_sum = {}
    for (l, key), fl in grad_flights:
        (parts,) = _xchg_wait(fl, [0], dx, f"grad_wait_{key}_l{l}")
        big_sum[(l, key)] = _sum8(parts, f"sum_{key}_l{l}")


    st = lambda key: jnp.stack([grads[l][key] for l in range(N_LAYER)])
    d_conv_w = st("conv_w")[:, :CONV_K, :].reshape(N_LAYER, CONV_K, N_DEV, GW // N_DEV).transpose(2, 0, 1, 3)
    d_pw_w = st("conv_pw_w").reshape(N_LAYER, N_DEV, GW // N_DEV, GW).transpose(1, 0, 2, 3)
    d_fcw = st("ffn_conv_w").reshape(N_LAYER, N_DEV, 3, FF_BLK).transpose(1, 0, 2, 3)
    rows_d = _pad_rows(jnp.concatenate(
        [grads[l][k] for l in range(N_LAYER) for k in ("sh1", "sc1", "ga1", "sh2", "sc2", "ga2")]
        + [grads[l][k] for k in ("g_pre_mix", "g_post_mix", "g_pre_ffn", "g_post_ffn", "g_group") for l in range(N_LAYER)],
        axis=0), 8)
    rows_gw = _pad_rows(jnp.concatenate(
        [grads[l][k] for k in ("conv_b", "conv_ln_g", "conv_ln_b", "conv_pw_b", "sgu_ln_g", "sgu_ln_b") for l in range(N_LAYER)],
        axis=0), 8)
    rows_128 = jnp.concatenate([_pad_rows(jnp.concatenate([grads[l]["bf"] for l in range(N_LAYER)]
                                                          + [grads[l]["sinks"] for l in range(N_LAYER)], axis=0), 8)]
                               + [grads[l]["sgu_bcol"] for l in range(N_LAYER)], axis=0)
    rows_w = jnp.concatenate([grads[l]["wcat"] for l in range(N_LAYER)], axis=0)
    rows_fb = st("ffn_conv_b").reshape(N_LAYER * N_DEV, FF_BLK)
    small_parts = _exchange([d_conv_w, d_pw_w, d_fcw, rows_d, rows_gw, rows_128, rows_w, rows_fb],
                            ["a2a"] * 3 + ["bcast"] * 5, "exchange_small_grads")
    s_conv_w, s_pw_w, s_fcw, s_d, s_gw, s_128, s_w, s_fb = _sum8_small(
        [p.reshape(N_DEV, -1, p.shape[-1]) for p in small_parts], "sum_small_grads")

    gr = {}
    gr["w_in"] = _unperm_in(jnp.stack([big_sum[(l, "w_in")] for l in range(N_LAYER)]))
    gr["w_out"] = jnp.stack([big_sum[(l, "w_out")] for l in range(N_LAYER)])
    gr["ffn_w_up"] = jnp.stack([big_sum[(l, "w_up")] for l in range(N_LAYER)])
    gr["ffn_w_down"] = jnp.stack([big_sum[(l, "w_down")] for l in range(N_LAYER)])
    gr["conv_w"] = s_conv_w.reshape(N_LAYER, CONV_K, GW // N_DEV)
    gr["conv_pw_w"] = s_pw_w.reshape(N_LAYER, GW // N_DEV, GW)
    gr["ffn_conv_w"] = s_fcw.reshape(N_LAYER, 3, FF_BLK)
    gr["b_ada"] = s_d[:6 * N_LAYER].reshape(N_LAYER, 6 * D)
    for i, k in enumerate(("g_pre_mix", "g_post_mix", "g_pre_ffn", "g_post_ffn", "g_group")):
        gr[k] = s_d[6 * N_LAYER + 2 * i:6 * N_LAYER + 2 * i + 2]
    for i, k in enumerate(("conv_b", "conv_ln_g", "conv_ln_b", "conv_pw_b", "sgu_ln_g", "sgu_ln_b")):
        gr[k] = s_gw[2 * i:2 * i + 2]
    gr["b_fgate"] = s_128[0:2, :4]
    gr["swa_sinks"] = s_128[2:4, :4]
    gr["sgu_b"] = s_128[8:].reshape(N_LAYER, WIN, 128)[:, :, :4].transpose(0, 2, 1)
    gr["sgu_w"] = s_w.reshape(N_LAYER, WIN, 4, WIN).transpose(0, 2, 1, 3)
    gr["ffn_conv_b"] = s_fb.reshape(N_LAYER, N_DEV * FF_BLK)
    dmod_all = small_parts[3][:, :6 * N_LAYER, :].reshape(N_DEV, N_LAYER, 6 * D)
    ncol = 6 * D // N_DEV
    dmod_cols = lax.dynamic_slice_in_dim(dmod_all, me * ncol, ncol, axis=2).transpose(1, 0, 2)
    gr["w_ada"] = _ada_bwd(c_all, dmod_cols)

    delta, new_m, new_v = {}, {}, {}
    bigs = ("w_ada", "w_in", "w_out", "ffn_w_up", "ffn_w_down")
    for n in bigs:
        d, m2, v2 = _adamw(_view2d(w[n]), _view2d(gr[n]), _view2d(mom[n]), _view2d(var[n]), "adamw_" + n)
        delta[n], new_m[n], new_v[n] = d.reshape(w[n].shape), m2.reshape(w[n].shape), v2.reshape(w[n].shape)
    smalls = [n for n in _WEIGHTS if n not in bigs]
    ds, ms, vs = _adamw_small([_view2d(w[n]) for n in smalls], [_view2d(gr[n]) for n in smalls],
                              [_view2d(mom[n]) for n in smalls], [_view2d(var[n]) for n in smalls], "adamw_small")
    for i, n in enumerate(smalls):
        delta[n], new_m[n], new_v[n] = ds[i].reshape(w[n].shape), ms[i].reshape(w[n].shape), vs[i].reshape(w[n].shape)

    return (loss, grad_x, *[gr[n].reshape(w[n].shape) for n in _WEIGHTS], *[delta[n] for n in _WEIGHTS],
            *[new_m[n] for n in _WEIGHTS], *[new_v[n] for n in _WEIGHTS])
```

```python
import functools

import jax
import jax.numpy as jnp
from jax import lax
from jax.experimental import pallas as pl
from jax.experimental.pallas import tpu as pltpu

F32, BF16 = jnp.float32, jnp.bfloat16
SDS = jax.ShapeDtypeStruct
MESH = pl.DeviceIdType.MESH

N_DEV = 8
D = 1024
GW = 256
HD = 64
N_LAYER = 2
ZW = 2432
FF_BLK = 704
FF_NBLK = 4
CONV_K = 31
CONV_HALO = 32
FFN_HALO = 16
EPS = 1e-6
NEG = -1e30
SCALE = HD ** -0.5
VMEM_LIMIT_V7X = 56 * 1024 * 1024
TM = 512
WGRAD_ROWS = 256
TQ = 256
WIN = 128

ADAM_LR, ADAM_B1, ADAM_B2, ADAM_EPS, ADAM_WD, ADAM_STEP = 0.001, 0.9, 0.999, 1e-08, 0.01, 10

Z_FQ, Z_FK, Z_FV, Z_CA, Z_CG, Z_SQ = 0, 1, 2, 3, 4, 5
Z_SK, Z_SV = 12, 13
Z_GU, Z_GV = 7, 8
Z_FG = 18


def _cp(sem=None):
    return pltpu.CompilerParams(dimension_semantics=sem, vmem_limit_bytes=VMEM_LIMIT_V7X)


def _vec(arr3, idx, ngrid):
    w = arr3.shape[-1]
    if ngrid == 1:
        return pl.BlockSpec((None, 1, w), lambda i: (idx, 0, 0))
    return pl.BlockSpec((None, 1, w), lambda i, j: (idx, 0, 0))


def _sigmoid(x):
    return jax.nn.sigmoid(x)


def _silu(x):
    return x * _sigmoid(x)


def _dsilu(x):
    s = _sigmoid(x)
    return s * (1.0 + x * (1.0 - s))


_G0, _G1 = 0.7978845608028654, 0.044715


def _gelu(x):
    return 0.5 * x * (1.0 + jnp.tanh(_G0 * (x + _G1 * x * x * x)))


def _dgelu(x):
    t = jnp.tanh(_G0 * (x + _G1 * x * x * x))
    return 0.5 * (1.0 + t) + 0.5 * x * (1.0 - t * t) * (_G0 * (1.0 + 3.0 * _G1 * x * x))


def _rstd(x):
    return lax.rsqrt(jnp.mean(x * x, axis=-1, keepdims=True) + EPS)


def _rms_bwd(xh, r, t):
    return r * (t - xh * jnp.mean(t * xh, axis=-1, keepdims=True))


def _ln_stats(x):
    mu = jnp.mean(x, axis=-1, keepdims=True)
    xc = x - mu
    rstd = lax.rsqrt(jnp.mean(xc * xc, axis=-1, keepdims=True) + EPS)
    return xc * rstd, rstd


def _ln_bwd(xh, rstd, dxh):
    return rstd * (dxh - jnp.mean(dxh, axis=-1, keepdims=True) - xh * jnp.mean(dxh * xh, axis=-1, keepdims=True))


def _colsum(x):
    return jnp.sum(x, axis=0, keepdims=True)


def _dot(a, b, kind):
    dn = {"nn": (((1,), (0,)), ((), ())), "nt": (((1,), (1,)), ((), ())), "tn": (((0,), (0,)), ((), ()))}[kind]
    return lax.dot_general(a.astype(BF16), b.astype(BF16), dn, preferred_element_type=F32)


def _exchange(arrs, modes, name):
    n = len(arrs)
    outs = [SDS((N_DEV,) + a.shape, a.dtype) if m == "bcast" else SDS(a.shape, a.dtype) for a, m in zip(arrs, modes)]

    def body(*refs):
        ins, dst = refs[:n], refs[n:2 * n]
        send, recv, loc = refs[2 * n:]
        x, y, c = lax.axis_index("x"), lax.axis_index("y"), lax.axis_index("c")
        me = 4 * x + 2 * y + c

        def src(a, j):
            return ins[a] if modes[a] == "bcast" else ins[a].at[j]

        local = [pltpu.make_async_copy(src(a, me), dst[a].at[me], loc.at[a]) for a in range(n)]
        for cp in local:
            cp.start()
        sent, landed = [], []
        for k in (2, 4, 6, 3, 5, 7, 1):
            px = 1 - x if k & 4 else x
            py = 1 - y if k & 2 else y
            pc = 1 - c if k & 1 else c
            peer = 4 * px + 2 * py + pc
            for a in range(n):
                cp = pltpu.make_async_remote_copy(src_ref=src(a, peer), dst_ref=dst[a].at[me], send_sem=send.at[a, k - 1],
                                                  recv_sem=recv.at[a, k - 1], device_id=(px, py, pc), device_id_type=MESH)
                cp.start()
                sent.append(cp)
                landed.append(pltpu.make_async_remote_copy(src_ref=src(a, peer), dst_ref=dst[a].at[peer],
                                                           send_sem=send.at[a, k - 1], recv_sem=recv.at[a, k - 1],
                                                           device_id=(px, py, pc), device_id_type=MESH))
        for cp in landed:
            cp.wait_recv()
        for cp in sent:
            cp.wait_send()
        for cp in local:
            cp.wait()

    hbm = pl.BlockSpec(memory_space=pltpu.HBM)
    return pl.pallas_call(
        body, name=name, out_shape=outs, in_specs=[hbm] * n, out_specs=[hbm] * n,
        scratch_shapes=[pltpu.SemaphoreType.DMA((n, N_DEV - 1)), pltpu.SemaphoreType.DMA((n, N_DEV - 1)),
                        pltpu.SemaphoreType.DMA((n,))],
        compiler_params=pltpu.CompilerParams(has_side_effects=True),
    )(*arrs)


_PEER_ORDER = (2, 4, 6, 3, 5, 7, 1)
_HBM = pl.BlockSpec(memory_space=pltpu.HBM)
_SEM = pl.BlockSpec(memory_space=pltpu.SEMAPHORE)
_EFFECT = pltpu.SideEffectType.DATAFLOW_SIDE_EFFECTING


def _peer(k):
    x, y, c = lax.axis_index("x"), lax.axis_index("y"), lax.axis_index("c")
    px = 1 - x if k & 4 else x
    py = 1 - y if k & 2 else y
    pc = 1 - c if k & 1 else c
    return (px, py, pc), 4 * px + 2 * py + pc


def _my_id():
    return 4 * lax.axis_index("x") + 2 * lax.axis_index("y") + lax.axis_index("c")


def _split_copies(src_ref, land_ref, send, recv, loc, mode):
    me = _my_id()
    pick = (lambda j: src_ref) if mode == "bcast" else (lambda j: src_ref.at[j])
    local = pltpu.make_async_copy(pick(me), land_ref.at[me], loc)
    remote = []
    for k in _PEER_ORDER:
        dev, peer = _peer(k)
        out = pltpu.make_async_remote_copy(src_ref=pick(peer), dst_ref=land_ref.at[me], send_sem=send.at[k - 1],
                                           recv_sem=recv.at[k - 1], device_id=dev, device_id_type=MESH)
        arrive = pltpu.make_async_remote_copy(src_ref=pick(peer), dst_ref=land_ref.at[peer], send_sem=send.at[k - 1],
                                              recv_sem=recv.at[k - 1], device_id=dev, device_id_type=MESH)
        remote.append((out, arrive))
    return local, remote


class _Flight:
    def __init__(self, srcs, lands, sends, recvs, locs, modes, token):
        self.srcs, self.lands, self.sends, self.recvs, self.locs, self.modes, self.token = (
            list(srcs), list(lands), list(sends), list(recvs), list(locs), list(modes), token)


def _xchg_start(arrs, modes, name):
    n = len(arrs)
    lands = [lax.empty((N_DEV,) + a.shape if m == "bcast" else a.shape, a.dtype) for a, m in zip(arrs, modes)]

    def body(*refs):
        srcs, lnds = refs[:n], refs[n:2 * n]
        outs = refs[2 * n:]
        sends, recvs, locs, token = outs[:n], outs[n:2 * n], outs[2 * n:3 * n], outs[5 * n]
        for a in range(n):
            local, remote = _split_copies(srcs[a], lnds[a], sends[a], recvs[a], locs[a], modes[a])
            local.start()
            for out, _ in remote:
                out.start()
        token[...] = jnp.zeros_like(token)

    sem7 = pltpu.SemaphoreType.DMA((N_DEV - 1,))
    res = pl.pallas_call(
        body, name=name,
        out_shape=[sem7] * (2 * n) + [pltpu.SemaphoreType.DMA(())] * n + [pltpu.HBM(a.shape, a.dtype) for a in arrs]
        + [pltpu.HBM(b.shape, b.dtype) for b in lands] + [SDS((8, 128), F32)],
        in_specs=[_HBM] * (2 * n), out_specs=[_SEM] * (3 * n) + [_HBM] * (2 * n) + [pl.BlockSpec(memory_space=pltpu.VMEM)],
        input_output_aliases={i: 3 * n + i for i in range(2 * n)},
        compiler_params=pltpu.CompilerParams(has_side_effects=_EFFECT),
    )(*[pltpu.with_memory_space_constraint(a, pltpu.HBM) for a in arrs],
      *[pltpu.with_memory_space_constraint(b, pltpu.HBM) for b in lands])
    return _Flight(res[3 * n:4 * n], res[4 * n:5 * n], res[:n], res[n:2 * n], res[2 * n:3 * n], modes, res[5 * n])


def _xchg_wait(flight, idx, after, name):
    n = len(idx)
    modes = [flight.modes[i] for i in idx]

    def body(*refs):
        srcs, lnds = refs[:n], refs[n:2 * n]
        sends, recvs, locs = refs[2 * n:3 * n], refs[3 * n:4 * n], refs[4 * n:5 * n]
        for a in range(n):
            local, remote = _split_copies(srcs[a], lnds[a], sends[a], recvs[a], locs[a], modes[a])
            local.wait()
            for _, arrive in remote:
                arrive.wait_send()
                arrive.wait_recv()

    ops = ([flight.srcs[i] for i in idx] + [flight.lands[i] for i in idx] + [flight.sends[i] for i in idx]
           + [flight.recvs[i] for i in idx] + [flight.locs[i] for i in idx])
    res = pl.pallas_call(
        body, name=name, out_shape=[pltpu.HBM(o.shape, o.dtype) for o in ops[:2 * n]],
        in_specs=[_HBM] * (2 * n) + [_SEM] * (3 * n) + [pl.BlockSpec(memory_space=pl.ANY)], out_specs=[_HBM] * (2 * n),
        input_output_aliases={i: i for i in range(2 * n)},
        compiler_params=pltpu.CompilerParams(has_side_effects=_EFFECT),
    )(*ops, after)
    return res[n:]


class _Lazy:
    def __init__(self, fn):
        self.fn, self.val = fn, None

    def get(self, after):
        if self.val is None:
            self.val = self.fn(after)
        return self.val


def _matmul(a, b, kind, out_shape, out_dtype, grid, a_spec, b_spec, o_spec, acc_shape, name):
    nk = grid[2]

    def body(a_ref, b_ref, o_ref, *scratch):
        prod = _dot(a_ref[...], b_ref[...], kind)
        if nk == 1:
            o_ref[...] = prod.astype(out_dtype)
        else:
            acc = scratch[0]
            k = pl.program_id(2)

            @pl.when(k == 0)
            def _():
                acc[...] = prod

            @pl.when(k > 0)
            def _():
                acc[...] += prod

            @pl.when(k == nk - 1)
            def _():
                o_ref[...] = acc[...].astype(out_dtype)

    return pl.pallas_call(
        body, name=name, grid=grid, in_specs=[a_spec, b_spec], out_specs=o_spec, out_shape=SDS(out_shape, out_dtype),
        scratch_shapes=[] if nk == 1 else [pltpu.VMEM(acc_shape, F32)],
        compiler_params=_cp(("parallel", "parallel", "arbitrary")))(a, b)


def _bs(shape, fn):
    return pl.BlockSpec(shape, fn)


def _mm_rows(a, w, kind, n_out, out_dtype, name):
    s, k = a.shape
    tm = min(TM, s)
    return _matmul(a, w, kind, (s, n_out), out_dtype, (s // tm, 1, 1),
                   _bs((tm, k), lambda i, j, kk: (i, 0)), _bs(w.shape, lambda i, j, kk: (0, 0)),
                   _bs((tm, n_out), lambda i, j, kk: (i, 0)), None, name)


def _mm_wgrad(a, dy, out_dtype, name):
    s, k = a.shape
    n = dy.shape[1]
    tko = min(WGRAD_ROWS, k)
    return _matmul(a, dy, "tn", (k, n), out_dtype, (k // tko, 1, 1),
                   _bs((s, tko), lambda i, j, kk: (0, i)), _bs((s, n), lambda i, j, kk: (0, 0)),
                   _bs((tko, n), lambda i, j, kk: (i, 0)), None, name)


def _ada_fwd(c_all, w_ada):
    ncol = w_ada.shape[2]

    def body(c_ref, w_ref, o_ref):
        ca = _silu(c_ref[...])
        ca = jnp.concatenate([ca, jnp.zeros_like(ca)], axis=0)
        o_ref[...] = _dot(ca, w_ref[...], "nn")[:N_DEV, :]

    return pl.pallas_call(
        body, name="ada_fwd", grid=(N_LAYER,),
        in_specs=[pl.BlockSpec((N_DEV, D), lambda l: (0, 0)), pl.BlockSpec((None, D, ncol), lambda l: (l, 0, 0))],
        out_specs=pl.BlockSpec((None, N_DEV, ncol), lambda l: (l, 0, 0)),
        out_shape=SDS((N_LAYER, N_DEV, ncol), F32), compiler_params=_cp(("parallel",)))(c_all, w_ada)


def _ada_finish(m_mine, b_ada):
    def body(m_ref, b_ref, o_ref, t_ref):
        o_ref[...] = m_ref[...] + b_ref[...]
        t_ref[...] = jnp.zeros_like(t_ref)

    return pl.pallas_call(body, name="ada_finish", out_shape=[SDS(b_ada.shape, F32), SDS((8, 128), F32)])(m_mine, b_ada)


def _ada_bwd(c_all, dmod_cols):
    ncol = dmod_cols.shape[2]

    def body(c_ref, d_ref, o_ref):
        ca = _silu(c_ref[...])
        ca = jnp.concatenate([ca, jnp.zeros_like(ca)], axis=0)
        dm = d_ref[...]
        dm = jnp.concatenate([dm, jnp.zeros_like(dm)], axis=0)
        o_ref[...] = _dot(ca, dm, "tn")

    return pl.pallas_call(
        body, name="ada_bwd", grid=(N_LAYER,),
        in_specs=[pl.BlockSpec((N_DEV, D), lambda l: (0, 0)), pl.BlockSpec((None, N_DEV, ncol), lambda l: (l, 0, 0))],
        out_specs=pl.BlockSpec((None, D, ncol), lambda l: (l, 0, 0)),
        out_shape=SDS((N_LAYER, D, ncol), F32), compiler_params=_cp(("parallel",)))(c_all, dmod_cols)


def _rows(s):
    tm = min(TM, s)
    return tm, pl.BlockSpec((tm, D), lambda i: (i, 0))


def _rms_mod(x, g, sc, sh, name):
    s = x.shape[0]
    tm, row = _rows(s)

    def body(x_ref, g_ref, sc_ref, sh_ref, h_ref):
        xf = x_ref[...]
        h_ref[...] = (xf * _rstd(xf) * (g_ref[...] * (1.0 + sc_ref[...])) + sh_ref[...]).astype(BF16)

    return pl.pallas_call(
        body, name=name, grid=(s // tm,), in_specs=[row, _vec(*g, 1), _vec(*sc, 1), _vec(*sh, 1)], out_specs=row,
        out_shape=SDS((s, D), BF16), compiler_params=_cp(("parallel",)))(x, g[0], sc[0], sh[0])


def _post(xres, o, ga, gpost, gn, scn, shn, name):
    s = xres.shape[0]
    tm, row = _rows(s)

    def body(x_ref, o_ref, ga_ref, gp_ref, gn_ref, sc_ref, sh_ref, xn_ref, h_ref):
        of = o_ref[...]
        xn = x_ref[...] + ga_ref[...] * (of * _rstd(of) * gp_ref[...])
        xn_ref[...] = xn
        h_ref[...] = (xn * _rstd(xn) * (gn_ref[...] * (1.0 + sc_ref[...])) + sh_ref[...]).astype(BF16)

    return pl.pallas_call(
        body, name=name, grid=(s // tm,),
        in_specs=[row, row, _vec(*ga, 1), _vec(*gpost, 1), _vec(*gn, 1), _vec(*scn, 1), _vec(*shn, 1)],
        out_specs=[row, row], out_shape=[SDS((s, D), F32), SDS((s, D), BF16)],
        compiler_params=_cp(("parallel",)))(xres, o, ga[0], gpost[0], gn[0], scn[0], shn[0])


def _post_loss(xres, o, ga, gpost, target, name):
    s = xres.shape[0]
    tm, row = _rows(s)

    def body(x_ref, o_ref, ga_ref, gp_ref, t_ref, dy_ref, loss_ref):
        of = o_ref[...]
        err = x_ref[...] + ga_ref[...] * (of * _rstd(of) * gp_ref[...]) - t_ref[...]
        dy_ref[...] = err * (1.0 / D)

        @pl.when(pl.program_id(0) == 0)
        def _():
            loss_ref[...] = jnp.zeros_like(loss_ref)

        loss_ref[...] += jnp.sum(jnp.mean(err * err, axis=-1, keepdims=True), axis=0, keepdims=True) * 0.5

    return pl.pallas_call(
        body, name=name, grid=(s // tm,), in_specs=[row, row, _vec(*ga, 1), _vec(*gpost, 1), row],
        out_specs=[row, pl.BlockSpec((8, 128), lambda i: (0, 0))], out_shape=[SDS((s, D), F32), SDS((8, 128), F32)],
        compiler_params=_cp(("arbitrary",)))(xres, o, ga[0], gpost[0], target)


def _acc(ref, val, first):
    @pl.when(first)
    def _():
        ref[...] = val

    @pl.when(jnp.logical_not(first))
    def _():
        ref[...] += val


def _post_bwd(dxn, o, ga, gpost, name):
    s = dxn.shape[0]
    tm, row = _rows(s)
    vec = pl.BlockSpec((1, D), lambda i: (0, 0))

    def body(d_ref, o_ref, ga_ref, gp_ref, do_ref, dga_ref, dgp_ref):
        of, dx = o_ref[...], d_ref[...]
        r = _rstd(of)
        oh = of * r
        do_ref[...] = _rms_bwd(oh, r, dx * (ga_ref[...] * gp_ref[...])).astype(BF16)
        cs = _colsum(dx * oh)
        first = pl.program_id(0) == 0
        _acc(dga_ref, cs * gp_ref[...], first)
        _acc(dgp_ref, cs * ga_ref[...], first)

    return pl.pallas_call(
        body, name=name, grid=(s // tm,), in_specs=[row, row, _vec(*ga, 1), _vec(*gpost, 1)], out_specs=[row, vec, vec],
        out_shape=[SDS((s, D), BF16), SDS((1, D), F32), SDS((1, D), F32)],
        compiler_params=_cp(("arbitrary",)))(dxn, o, ga[0], gpost[0])


def _pre_bwd(dh, x, dres, g, sc, name):
    s = x.shape[0]
    tm, row = _rows(s)
    vec = pl.BlockSpec((1, D), lambda i: (0, 0))

    def body(dh_ref, x_ref, dr_ref, g_ref, sc_ref, dx_ref, dsh_ref, dsc_ref, dg_ref):
        xf, d = x_ref[...], dh_ref[...]
        r = _rstd(xf)
        xh = xf * r
        dx_ref[...] = dr_ref[...] + _rms_bwd(xh, r, d * (g_ref[...] * (1.0 + sc_ref[...])))
        cs = _colsum(d * xh)
        first = pl.program_id(0) == 0
        _acc(dsh_ref, _colsum(d), first)
        _acc(dsc_ref, cs * g_ref[...], first)
        _acc(dg_ref, cs * (1.0 + sc_ref[...]), first)

    return pl.pallas_call(
        body, name=name, grid=(s // tm,), in_specs=[row, row, row, _vec(*g, 1), _vec(*sc, 1)],
        out_specs=[row, vec, vec, vec],
        out_shape=[SDS((s, D), F32), SDS((1, D), F32), SDS((1, D), F32), SDS((1, D), F32)],
        compiler_params=_cp(("arbitrary",)))(dh, x, dres, g[0], sc[0])


def _gnorm(ys, gg, name):
    s = ys[0].shape[0]
    tm = min(TM, s)
    yb = pl.BlockSpec((tm, GW), lambda i: (i, 0))

    def body(y0, y1, y2, y3, g_ref, o_ref):
        for i, yr in enumerate((y0, y1, y2, y3)):
            y = yr[...]
            o_ref[:, GW * i:GW * (i + 1)] = (y * _rstd(y) * g_ref[:, GW * i:GW * (i + 1)]).astype(BF16)

    return pl.pallas_call(
        body, name=name, grid=(s // tm,), in_specs=[yb] * 4 + [_vec(*gg, 1)], out_specs=pl.BlockSpec((tm, D), lambda i: (i, 0)),
        out_shape=SDS((s, D), BF16), compiler_params=_cp(("parallel",)))(*ys, gg[0])


def _gnorm_bwd(dyn, ys, gg, name):
    s = ys[0].shape[0]
    tm = min(TM, s)
    yb = pl.BlockSpec((tm, GW), lambda i: (i, 0))

    def body(d_ref, y0, y1, y2, y3, g_ref, o0, o1, o2, o3, dg_ref):
        first = pl.program_id(0) == 0
        for i, (yr, orf) in enumerate(zip((y0, y1, y2, y3), (o0, o1, o2, o3))):
            y = yr[...]
            d = d_ref[:, GW * i:GW * (i + 1)]
            r = _rstd(y)
            yh = y * r
            orf[...] = _rms_bwd(yh, r, d * g_ref[:, GW * i:GW * (i + 1)])
            cs = _colsum(d * yh)

            @pl.when(first)
            def _():
                dg_ref[:, GW * i:GW * (i + 1)] = cs

            @pl.when(jnp.logical_not(first))
            def _():
                dg_ref[:, GW * i:GW * (i + 1)] += cs

    return pl.pallas_call(
        body, name=name, grid=(s // tm,), in_specs=[pl.BlockSpec((tm, D), lambda i: (i, 0))] + [yb] * 4 + [_vec(*gg, 1)],
        out_specs=[yb] * 4 + [pl.BlockSpec((1, D), lambda i: (0, 0))],
        out_shape=[SDS((s, GW), F32)] * 4 + [SDS((1, D), F32)], compiler_params=_cp(("arbitrary",)))(dyn, *ys, gg[0])


def _lane_put(acc, col, h):
    lane = lax.broadcasted_iota(jnp.int32, acc.shape, 1)
    return jnp.where(lane == h, col, acc)


def _fgate(z, bf, name):
    s = z.shape[0]
    tq = min(TQ, s)
    nc = s // tq

    def body(z_ref, b_ref, fc_ref, fr_ref):
        xg = z_ref[...] + b_ref[...]
        lf = jnp.minimum(xg, 0.0) - jnp.log(1.0 + jnp.exp(-jnp.abs(xg)))
        lane = lax.broadcasted_iota(jnp.int32, lf.shape, 1)
        row = lax.broadcasted_iota(jnp.int32, lf.shape, 0)
        f = jnp.where(lane < 4, lf, 0.0)
        sh = 1
        while sh < s:
            f = f + jnp.where(row >= sh, pltpu.roll(f, sh, 0), 0.0)
            sh *= 2
        fc_ref[...] = f
        for c in range(nc):
            fr_ref[c] = f[tq * c:tq * (c + 1), :].T[:8, :]

    return pl.pallas_call(
        body, name=name, grid=(1,),
        in_specs=[pl.BlockSpec((s, 128), lambda i: (0, Z_FG)), pl.BlockSpec((1, 128), lambda i: (0, 0))],
        out_specs=[pl.BlockSpec((s, 128), lambda i: (0, 0)), pl.BlockSpec((nc, 8, tq), lambda i: (0, 0, 0))],
        out_shape=[SDS((s, 128), F32), SDS((nc, 8, tq), F32)], compiler_params=_cp(("arbitrary",)))(z, bf)


def _fgate_bwd(z, bf, dfrow, dfcol, name):
    s = z.shape[0]
    tq = min(TQ, s)
    nc = s // tq

    def body(z_ref, b_ref, d_ref, dc_ref, dz_ref, db_ref):
        pad = jnp.zeros((120, tq), F32)
        d = jnp.concatenate([jnp.concatenate([d_ref[c], pad], axis=0).T for c in range(nc)], axis=0) + dc_ref[...]
        row = lax.broadcasted_iota(jnp.int32, d.shape, 0)
        lane = lax.broadcasted_iota(jnp.int32, d.shape, 1)
        sh = 1
        while sh < s:
            d = d + jnp.where(row < s - sh, pltpu.roll(d, s - sh, 0), 0.0)
            sh *= 2
        xg = z_ref[...] + b_ref[...]
        dz = jnp.where(lane < 4, d * _sigmoid(-xg), 0.0)
        dz_ref[...] = dz.astype(BF16)
        db_ref[...] = _colsum(dz)

    return pl.pallas_call(
        body, name=name, grid=(1,),
        in_specs=[pl.BlockSpec((s, 128), lambda i: (0, Z_FG)), pl.BlockSpec((1, 128), lambda i: (0, 0)),
                  pl.BlockSpec((nc, 8, tq), lambda i: (0, 0, 0)), pl.BlockSpec((s, 128), lambda i: (0, 0))],
        out_specs=[pl.BlockSpec((s, 128), lambda i: (0, 0)), pl.BlockSpec((1, 128), lambda i: (0, 0))],
        out_shape=[SDS((s, 128), BF16), SDS((1, 128), F32)], compiler_params=_cp(("arbitrary",)))(z, bf, dfrow, dfcol)


def _fox_scores(q, k_ref, fc, fr_ref, h, i, c, tq):
    ks = pl.multiple_of(c * tq, tq)
    sc = _dot(q, k_ref[pl.ds(ks, tq), HD * h:HD * (h + 1)], "nt") + fc - fr_ref[c, h:h + 1, :]
    qpos = i * tq + lax.broadcasted_iota(jnp.int32, (tq, tq), 0)
    kpos = c * tq + lax.broadcasted_iota(jnp.int32, (tq, tq), 1)
    return ks, jnp.where(kpos <= qpos, sc, NEG)


def _fox_fwd(z, fcol, frow, name):
    s = z.shape[0]
    tq = min(TQ, s)
    nc = s // tq

    def body(q_ref, k_ref, v_ref, fc_ref, fr_ref, y_ref, l_ref):
        i = pl.program_id(0)
        qs = [q_ref[:, HD * h:HD * (h + 1)] * SCALE for h in range(4)]
        fcs = [fc_ref[:, h:h + 1] for h in range(4)]

        def step(c, carry):
            out = []
            for h in range(4):
                m, l, acc = carry[h]
                ks, sc = _fox_scores(qs[h], k_ref, fcs[h], fr_ref, h, i, c, tq)
                m_new = jnp.maximum(m, jnp.max(sc, axis=-1, keepdims=True))
                alpha = jnp.exp(m - m_new)
                p = jnp.exp(sc - m_new)
                out.append((m_new, alpha * l + jnp.sum(p, axis=-1, keepdims=True),
                            alpha * acc + _dot(p, v_ref[pl.ds(ks, tq), HD * h:HD * (h + 1)], "nn")))
            return tuple(out)

        init = tuple((jnp.full((tq, 1), NEG, F32), jnp.zeros((tq, 1), F32), jnp.zeros((tq, HD), F32)) for _ in range(4))
        res = lax.fori_loop(0, i + 1, step, init)
        lse = jnp.zeros((tq, 128), F32)
        for h in range(4):
            m, l, acc = res[h]
            y_ref[:, HD * h:HD * (h + 1)] = acc / l
            lse = _lane_put(lse, m + jnp.log(l), h)
        l_ref[...] = lse

    return pl.pallas_call(
        body, name=name, grid=(nc,),
        in_specs=[pl.BlockSpec((tq, GW), lambda i: (i, Z_FQ)), pl.BlockSpec((s, GW), lambda i: (0, Z_FK)),
                  pl.BlockSpec((s, GW), lambda i: (0, Z_FV)), pl.BlockSpec((tq, 128), lambda i: (i, 0)),
                  pl.BlockSpec((nc, 8, tq), lambda i: (0, 0, 0))],
        out_specs=[pl.BlockSpec((tq, GW), lambda i: (i, 0)), pl.BlockSpec((tq, 128), lambda i: (i, 0))],
        out_shape=[SDS((s, GW), F32), SDS((s, 128), F32)], compiler_params=_cp(("parallel",)))(z, z, z, fcol, frow)


def _fox_bwd(z, fcol, frow, lse, y, dy, name):
    s = z.shape[0]
    tq = min(TQ, s)
    nc = s // tq

    def body(q_ref, k_ref, v_ref, fc_ref, fr_ref, l_ref, y_ref, dy_ref, dq_ref, dk_ref, dv_ref, df_ref, dfq_ref):
        i = pl.program_id(0)

        @pl.when(i == 0)
        def _():
            dk_ref[...] = jnp.zeros_like(dk_ref)
            dv_ref[...] = jnp.zeros_like(dv_ref)
            df_ref[...] = jnp.zeros_like(df_ref)

        hss = [slice(HD * h, HD * (h + 1)) for h in range(4)]
        qs = [q_ref[:, hs] * SCALE for hs in hss]
        fcs = [fc_ref[:, h:h + 1] for h in range(4)]
        lses = [l_ref[:, h:h + 1] for h in range(4)]
        dys = [dy_ref[:, hs] for hs in hss]
        dds = [jnp.sum(dys[h] * y_ref[:, hss[h]], axis=-1, keepdims=True) for h in range(4)]

        def step(c, carry):
            out = []
            for h in range(4):
                dq, dfq = carry[h]
                ks, sc = _fox_scores(qs[h], k_ref, fcs[h], fr_ref, h, i, c, tq)
                p = jnp.exp(sc - lses[h])
                ds = p * (_dot(dys[h], v_ref[pl.ds(ks, tq), hss[h]], "nt") - dds[h])
                dk_ref[pl.ds(ks, tq), hss[h]] += _dot(ds, qs[h], "tn")
                dv_ref[pl.ds(ks, tq), hss[h]] += _dot(p, dys[h], "tn")
                df_ref[c, h:h + 1, :] -= _colsum(ds)
                out.append((dq + _dot(ds, k_ref[pl.ds(ks, tq), hss[h]], "nn"), dfq + jnp.sum(ds, axis=-1, keepdims=True)))
            return tuple(out)

        init = tuple((jnp.zeros((tq, HD), F32), jnp.zeros((tq, 1), F32)) for _ in range(4))
        res = lax.fori_loop(0, i + 1, step, init)
        dfq = jnp.zeros((tq, 128), F32)
        for h in range(4):
            dq_ref[:, hss[h]] = res[h][0] * SCALE
            dfq = _lane_put(dfq, res[h][1], h)
        dfq_ref[...] = dfq

    tile = lambda w: pl.BlockSpec((tq, w), lambda i: (i, 0))
    full = pl.BlockSpec((s, GW), lambda i: (0, 0))
    chunks = pl.BlockSpec((nc, 8, tq), lambda i: (0, 0, 0))
    return pl.pallas_call(
        body, name=name, grid=(nc,),
        in_specs=[pl.BlockSpec((tq, GW), lambda i: (i, Z_FQ)), pl.BlockSpec((s, GW), lambda i: (0, Z_FK)),
                  pl.BlockSpec((s, GW), lambda i: (0, Z_FV)), tile(128), chunks, tile(128), tile(GW), tile(GW)],
        out_specs=[tile(GW), full, full, chunks, tile(128)],
        out_shape=[SDS((s, GW), F32), SDS((s, GW), F32), SDS((s, GW), F32), SDS((nc, 8, tq), F32), SDS((s, 128), F32)],
        compiler_params=_cp(("arbitrary",)))(z, z, z, fcol, frow, lse, y, dy)


def _swa_block(q_ref, k_ref, v_ref, n):
    qs = pl.multiple_of(n * WIN, WIN)
    ks = pl.multiple_of(jnp.maximum(n - 1, 0) * WIN, WIN)
    qb = q_ref[pl.ds(qs, WIN), :]
    kb = k_ref[pl.ds(ks, 2 * WIN), :]
    vb = v_ref[pl.ds(ks, 2 * WIN), :]
    dist = (qs + lax.broadcasted_iota(jnp.int32, (WIN, 2 * WIN), 0)) - (ks + lax.broadcasted_iota(jnp.int32, (WIN, 2 * WIN), 1))
    return qs, ks, qb, kb, vb, (dist >= 0) & (dist < WIN)


def _swa_fwd(z, sinks, name):
    s = z.shape[0]

    def body(sink_ref, q_ref, k_ref, v_ref, y_ref, l_ref):
        def step(n, carry):
            qs, ks, qb, kb, vb, valid = _swa_block(q_ref, k_ref, v_ref, n)
            lse = jnp.zeros((WIN, 128), F32)
            for h in range(4):
                kv = slice(HD * (h // 2), HD * (h // 2 + 1))
                sc = jnp.where(valid, _dot(qb[:, HD * h:HD * (h + 1)] * SCALE, kb[:, kv], "nt"), NEG)
                sink = sink_ref[h]
                m = jnp.maximum(jnp.max(sc, axis=-1, keepdims=True), sink)
                p = jnp.exp(sc - m)
                den = jnp.sum(p, axis=-1, keepdims=True) + jnp.exp(sink - m)
                y_ref[pl.ds(qs, WIN), HD * h:HD * (h + 1)] = _dot(p, vb[:, kv], "nn") / den
                lse = _lane_put(lse, m + jnp.log(den), h)
            l_ref[pl.ds(qs, WIN), :] = lse
            return carry

        lax.fori_loop(0, s // WIN, step, 0)

    return pl.pallas_call(
        body, name=name, grid=(1,),
        in_specs=[pl.BlockSpec(memory_space=pltpu.SMEM), pl.BlockSpec((s, GW), lambda i: (0, Z_SQ)),
                  pl.BlockSpec((s, 128), lambda i: (0, Z_SK)), pl.BlockSpec((s, 128), lambda i: (0, Z_SV))],
        out_specs=[pl.BlockSpec((s, GW), lambda i: (0, 0)), pl.BlockSpec((s, 128), lambda i: (0, 0))],
        out_shape=[SDS((s, GW), F32), SDS((s, 128), F32)], compiler_params=_cp(("arbitrary",)))(sinks, z, z, z)


def _swa_bwd(z, sinks, lse, y, dy, name):
    s = z.shape[0]

    def body(sink_ref, q_ref, k_ref, v_ref, l_ref, y_ref, dy_ref, dq_ref, dk_ref, dv_ref, dsink_ref):
        dk_ref[...] = jnp.zeros_like(dk_ref)
        dv_ref[...] = jnp.zeros_like(dv_ref)
        dsink_ref[...] = jnp.zeros_like(dsink_ref)

        def step(n, carry):
            qs, ks, qb, kb, vb, valid = _swa_block(q_ref, k_ref, v_ref, n)
            lse_b = l_ref[pl.ds(qs, WIN), :]
            yb = y_ref[pl.ds(qs, WIN), :]
            dyb = dy_ref[pl.ds(qs, WIN), :]
            dsink = jnp.zeros((1, 128), F32)
            for h in range(4):
                hs = slice(HD * h, HD * (h + 1))
                kv = slice(HD * (h // 2), HD * (h // 2 + 1))
                q = qb[:, hs] * SCALE
                sc = jnp.where(valid, _dot(q, kb[:, kv], "nt"), NEG)
                lh = lse_b[:, h:h + 1]
                p = jnp.exp(sc - lh)
                dd = jnp.sum(dyb[:, hs] * yb[:, hs], axis=-1, keepdims=True)
                ds = p * (_dot(dyb[:, hs], vb[:, kv], "nt") - dd)
                dq_ref[pl.ds(qs, WIN), hs] = _dot(ds, kb[:, kv], "nn") * SCALE
                dk_ref[pl.ds(ks, 2 * WIN), kv] += _dot(ds, q, "tn")
                dv_ref[pl.ds(ks, 2 * WIN), kv] += _dot(p, dyb[:, hs], "tn")
                dsink = _lane_put(dsink, dsink[:, h:h + 1] - jnp.sum(jnp.exp(sink_ref[h] - lh) * dd, axis=0, keepdims=True), h)
            dsink_ref[...] += dsink
            return carry

        lax.fori_loop(0, s // WIN, step, 0)

    full = lambda w: pl.BlockSpec((s, w), lambda i: (0, 0))
    return pl.pallas_call(
        body, name=name, grid=(1,),
        in_specs=[pl.BlockSpec(memory_space=pltpu.SMEM), pl.BlockSpec((s, GW), lambda i: (0, Z_SQ)),
                  pl.BlockSpec((s, 128), lambda i: (0, Z_SK)), pl.BlockSpec((s, 128), lambda i: (0, Z_SV)),
                  full(128), full(GW), full(GW)],
        out_specs=[full(GW), full(128), full(128), pl.BlockSpec((1, 128), lambda i: (0, 0))],
        out_shape=[SDS((s, GW), F32), SDS((s, 128), F32), SDS((s, 128), F32), SDS((1, 128), F32)],
        compiler_params=_cp(("arbitrary",)))(sinks, z, z, z, lse, y, dy)


def _delayed(win, shift, halo):
    return win[halo:, :] if shift == 0 else pltpu.roll(win, shift, 0)[halo:, :]


def _prev_halo(width, halo, tm, col):
    return pl.BlockSpec((halo, width), lambda i: (jnp.maximum(i * (tm // halo) - 1, 0), col))


def _glu_window(a_ref, g_ref, ah_ref, gh_ref):
    keep = (pl.program_id(0) > 0).astype(F32)
    a = jnp.concatenate([ah_ref[...] * keep, a_ref[...]], axis=0)
    g = jnp.concatenate([gh_ref[...], g_ref[...]], axis=0)
    return a * _sigmoid(g)


def _conv_fwd(z, cw, cb, lg, lb, pw, pb, name):
    s = z.shape[0]
    tm = min(TM, s)

    def body(a_ref, g_ref, ah_ref, gh_ref, w_ref, b_ref, lg_ref, lb_ref, pw_ref, pb_ref, y_ref, hc_ref):
        hg = _glu_window(a_ref, g_ref, ah_ref, gh_ref)
        hc = jnp.zeros((tm, GW), F32) + b_ref[...]
        for k in range(CONV_K):
            hc = hc + w_ref[k:k + 1, :] * _delayed(hg, CONV_K - 1 - k, CONV_HALO)
        hc_ref[...] = hc
        xh, _ = _ln_stats(hc)
        y_ref[...] = _dot(_silu(xh * lg_ref[...] + lb_ref[...]), pw_ref[...], "nn") + pb_ref[...]

    tile = lambda col: pl.BlockSpec((tm, GW), lambda i: (i, col))
    whole = lambda a: pl.BlockSpec(a.shape, lambda i: (0, 0))
    return pl.pallas_call(
        body, name=name, grid=(s // tm,),
        in_specs=[tile(Z_CA), tile(Z_CG), _prev_halo(GW, CONV_HALO, tm, Z_CA), _prev_halo(GW, CONV_HALO, tm, Z_CG),
                  whole(cw), whole(cb), whole(lg), whole(lb), whole(pw), whole(pb)],
        out_specs=[tile(0), tile(0)], out_shape=[SDS((s, GW), F32), SDS((s, GW), F32)],
        compiler_params=_cp(("parallel",)))(z, z, z, z, cw, cb, lg, lb, pw, pb)


def _conv_bwd_a(z, hc, dy, cw, lg, lb, pw, name):
    s = z.shape[0]
    tm = min(TM, s)

    def body(a_ref, g_ref, ah_ref, gh_ref, hc_ref, dy_ref, lg_ref, lb_ref, pw_ref,
             dhc_ref, dpw_ref, dpb_ref, dlg_ref, dlb_ref, dcw_ref, dcb_ref):
        first = pl.program_id(0) == 0
        dy = dy_ref[...]
        xh, rstd = _ln_stats(hc_ref[...])
        hn = xh * lg_ref[...] + lb_ref[...]
        dhn = _dot(dy, pw_ref[...], "nt") * _dsilu(hn)
        dhc = _ln_bwd(xh, rstd, dhn * lg_ref[...])
        dhc_ref[...] = dhc
        _acc(dpw_ref, _dot(_silu(hn), dy, "tn"), first)
        _acc(dpb_ref, _colsum(dy), first)
        _acc(dlg_ref, _colsum(dhn * xh), first)
        _acc(dlb_ref, _colsum(dhn), first)
        _acc(dcb_ref, _colsum(dhc), first)
        hg = _glu_window(a_ref, g_ref, ah_ref, gh_ref)

        @pl.when(first)
        def _():
            dcw_ref[...] = jnp.zeros_like(dcw_ref)

        for k in range(CONV_K):
            dcw_ref[k:k + 1, :] += _colsum(dhc * _delayed(hg, CONV_K - 1 - k, CONV_HALO))

    tile = lambda col: pl.BlockSpec((tm, GW), lambda i: (i, col))
    whole = lambda shape: pl.BlockSpec(shape, lambda i: (0, 0))
    return pl.pallas_call(
        body, name=name, grid=(s // tm,),
        in_specs=[tile(Z_CA), tile(Z_CG), _prev_halo(GW, CONV_HALO, tm, Z_CA), _prev_halo(GW, CONV_HALO, tm, Z_CG),
                  tile(0), tile(0), whole(lg.shape), whole(lb.shape), whole(pw.shape)],
        out_specs=[tile(0), whole((GW, GW)), whole((1, GW)), whole((1, GW)), whole((1, GW)), whole((32, GW)), whole((1, GW))],
        out_shape=[SDS((s, GW), F32), SDS((GW, GW), F32), SDS((1, GW), F32), SDS((1, GW), F32), SDS((1, GW), F32),
                   SDS((32, GW), F32), SDS((1, GW), F32)],
        compiler_params=_cp(("arbitrary",)))(z, z, z, z, hc, dy, lg, lb, pw)


def _conv_bwd_b(z, dhc, cw, name):
    s = z.shape[0]
    tm = min(TM, s)
    nt = s // tm

    def body(a_ref, g_ref, d_ref, dn_ref, w_ref, da_ref, dg_ref):
        keep = (pl.program_id(0) < nt - 1).astype(F32)
        win = jnp.concatenate([d_ref[...], dn_ref[...] * keep], axis=0)
        dhg = jnp.zeros((tm, GW), F32)
        for k in range(CONV_K):
            sh = CONV_K - 1 - k
            dhg = dhg + w_ref[k:k + 1, :] * (win[:tm, :] if sh == 0 else pltpu.roll(win, tm + CONV_HALO - sh, 0)[:tm, :])
        sg = _sigmoid(g_ref[...])
        da_ref[...] = (dhg * sg).astype(BF16)
        dg_ref[...] = (dhg * a_ref[...] * sg * (1.0 - sg)).astype(BF16)

    tile = lambda col: pl.BlockSpec((tm, GW), lambda i: (i, col))
    nxt = pl.BlockSpec((CONV_HALO, GW), lambda i: (jnp.minimum((i + 1) * (tm // CONV_HALO), s // CONV_HALO - 1), 0))
    return pl.pallas_call(
        body, name=name, grid=(nt,),
        in_specs=[tile(Z_CA), tile(Z_CG), tile(0), nxt, pl.BlockSpec(cw.shape, lambda i: (0, 0))],
        out_specs=[tile(0), tile(0)], out_shape=[SDS((s, GW), BF16), SDS((s, GW), BF16)],
        compiler_params=_cp(("parallel",)))(z, z, dhc, dhc, cw)


def _sgu_chunk(zu, zv, lg, lb, wcat, bfull):
    u, v = _gelu(zu), _gelu(zv)
    xh, rstd = _ln_stats(v)
    vn = xh * lg + lb
    lane = lax.shift_right_logical(lax.broadcasted_iota(jnp.int32, (WIN, GW), 1), 6)
    r = jnp.concatenate([jnp.where(lane == g, vn, 0.0) for g in range(4)], axis=0)
    mix = _dot(wcat, r, "nn") + bfull
    return u, xh, rstd, r, mix, lane


def _tril4(w):
    t = lax.broadcasted_iota(jnp.int32, w.shape, 0)
    sidx = lax.broadcasted_iota(jnp.int32, w.shape, 1) & (WIN - 1)
    return jnp.where(sidx <= t, w, 0.0)


def _sgu_fwd(z, lg, lb, wcat, bfull, name):
    s = z.shape[0]
    tm = min(TM, s)

    def body(u_ref, v_ref, lg_ref, lb_ref, w_ref, b_ref, y_ref):
        w = _tril4(w_ref[...])
        for n in range(tm // WIN):
            rows = slice(WIN * n, WIN * (n + 1))
            u, _, _, _, mix, _ = _sgu_chunk(u_ref[rows, :], v_ref[rows, :], lg_ref[...], lb_ref[...], w, b_ref[...])
            y_ref[rows, :] = u * mix

    tile = lambda col: pl.BlockSpec((tm, GW), lambda i: (i, col))
    whole = lambda a: pl.BlockSpec(a.shape, lambda i: (0, 0))
    return pl.pallas_call(
        body, name=name, grid=(s // tm,), in_specs=[tile(Z_GU), tile(Z_GV), whole(lg), whole(lb), whole(wcat), whole(bfull)],
        out_specs=tile(0), out_shape=SDS((s, GW), F32), compiler_params=_cp(("parallel",)))(z, z, lg, lb, wcat, bfull)


def _sgu_bwd(z, dy, lg, lb, wcat, bfull, name):
    s = z.shape[0]
    tm = min(TM, s)

    def body(u_ref, v_ref, dy_ref, lg_ref, lb_ref, w_ref, b_ref, du_ref, dv_ref, dw_ref, db_ref, dlg_ref, dlb_ref):
        first = pl.program_id(0) == 0
        w = _tril4(w_ref[...])
        wt = w.T
        dw = jnp.zeros((WIN, 4 * WIN), F32)
        db = jnp.zeros((WIN, 128), F32)
        dlg = jnp.zeros((1, GW), F32)
        dlb = jnp.zeros((1, GW), F32)
        for n in range(tm // WIN):
            rows = slice(WIN * n, WIN * (n + 1))
            zu, zv, dout = u_ref[rows, :], v_ref[rows, :], dy_ref[rows, :]
            u, xh, rstd, r, mix, lane = _sgu_chunk(zu, zv, lg_ref[...], lb_ref[...], w, b_ref[...])
            dmix = dout * u
            du_ref[rows, :] = (dout * mix * _dgelu(zu)).astype(BF16)
            dw = dw + _dot(dmix, r, "nt")
            for g in range(4):
                db = _lane_put(db, db[:, g:g + 1] + jnp.sum(dmix[:, HD * g:HD * (g + 1)], axis=1, keepdims=True), g)
            dr = _dot(wt, dmix, "nn")
            dvn = jnp.zeros((WIN, GW), F32)
            for g in range(4):
                dvn = dvn + jnp.where(lane == g, dr[WIN * g:WIN * (g + 1), :], 0.0)
            dlg = dlg + _colsum(dvn * xh)
            dlb = dlb + _colsum(dvn)
            dv_ref[rows, :] = (_ln_bwd(xh, rstd, dvn * lg_ref[...]) * _dgelu(zv)).astype(BF16)
        _acc(dw_ref, _tril4(dw), first)
        _acc(db_ref, db, first)
        _acc(dlg_ref, dlg, first)
        _acc(dlb_ref, dlb, first)

    tile = lambda col: pl.BlockSpec((tm, GW), lambda i: (i, col))
    whole = lambda shape: pl.BlockSpec(shape, lambda i: (0, 0))
    return pl.pallas_call(
        body, name=name, grid=(s // tm,),
        in_specs=[tile(Z_GU), tile(Z_GV), tile(0), whole(lg.shape), whole(lb.shape), whole(wcat.shape), whole(bfull.shape)],
        out_specs=[tile(0), tile(0), whole((WIN, 4 * WIN)), whole((WIN, 128)), whole((1, GW)), whole((1, GW))],
        out_shape=[SDS((s, GW), BF16), SDS((s, GW), BF16), SDS((WIN, 4 * WIN), F32), SDS((WIN, 128), F32),
                   SDS((1, GW), F32), SDS((1, GW), F32)],
        compiler_params=_cp(("arbitrary",)))(z, z, dy, lg, lb, wcat, bfull)


def _conv3(win, w, b):
    return (w[2:3, :] * win[FFN_HALO:, :] + w[1:2, :] * pltpu.roll(win, 1, 0)[FFN_HALO:, :]
            + w[0:1, :] * pltpu.roll(win, 2, 0)[FFN_HALO:, :] + b)


def _ffn_specs(s, tm):
    main = pl.BlockSpec((2, None, tm, FF_BLK), lambda j, i: (0, j, i, 0))
    prev = pl.BlockSpec((2, None, FFN_HALO, FF_BLK), lambda j, i: (0, j, jnp.maximum(i * (tm // FFN_HALO) - 1, 0), 0))
    nxt = pl.BlockSpec((2, None, FFN_HALO, FF_BLK),
                       lambda j, i: (0, j, jnp.minimum((i + 1) * (tm // FFN_HALO), s // FFN_HALO - 1), 0))
    wsp = pl.BlockSpec((2, None, 3, FF_BLK), lambda j, i: (0, j, 0, 0))
    bsp = pl.BlockSpec((2, None, 1, FF_BLK), lambda j, i: (0, j, 0, 0))
    return main, prev, nxt, wsp, bsp


def _ffn_act(u4, w4, b4, name):
    s = u4.shape[2]
    tm = min(TM, s)
    main, prev, _, wsp, bsp = _ffn_specs(s, tm)

    def body(u_ref, uh_ref, w_ref, b_ref, o_ref):
        keep = (pl.program_id(1) > 0).astype(F32)
        gw, vw = [jnp.concatenate([uh_ref[p].astype(F32) * keep, u_ref[p].astype(F32)], axis=0) for p in range(2)]
        o_ref[...] = (_silu(_conv3(gw, w_ref[0], b_ref[0])) * _conv3(vw, w_ref[1], b_ref[1])).astype(BF16)

    return pl.pallas_call(
        body, name=name, grid=(FF_NBLK, s // tm), in_specs=[main, prev, wsp, bsp],
        out_specs=pl.BlockSpec((None, tm, FF_BLK), lambda j, i: (j, i, 0)), out_shape=SDS((FF_NBLK, s, FF_BLK), BF16),
        compiler_params=_cp(("parallel", "parallel")))(u4, u4, w4, b4)


def _ffn_bwd(u4, dact, w4, b4, name):
    s = u4.shape[2]
    tm = min(TM, s)
    nt = s // tm
    main, prev, nxt, wsp, bsp = _ffn_specs(s, tm)
    dmain = pl.BlockSpec((None, tm, FF_BLK), lambda j, i: (j, i, 0))
    dnext = pl.BlockSpec((None, FFN_HALO, FF_BLK), lambda j, i: (j, jnp.minimum((i + 1) * (tm // FFN_HALO), s // FFN_HALO - 1), 0))
    ext = tm + FFN_HALO

    def body(u_ref, up_ref, un_ref, d_ref, dn_ref, w_ref, b_ref, du_ref, dw_ref, db_ref):
        i = pl.program_id(1)
        first = i == 0
        keep_prev = (i > 0).astype(F32)
        keep_next = (i < nt - 1).astype(F32)
        wins = [jnp.concatenate([up_ref[p].astype(F32) * keep_prev, u_ref[p].astype(F32), un_ref[p].astype(F32)], axis=0)
                for p in range(2)]
        gc = _conv3(wins[0], w_ref[0], b_ref[0])
        vc = _conv3(wins[1], w_ref[1], b_ref[1])
        d = jnp.concatenate([d_ref[...].astype(F32), dn_ref[...].astype(F32) * keep_next], axis=0)
        duc = (d * vc * _dsilu(gc), d * _silu(gc))
        for p in range(2):
            w = w_ref[p]
            du_ref[p] = (w[2:3, :] * duc[p][:tm, :] + w[1:2, :] * pltpu.roll(duc[p], ext - 1, 0)[:tm, :]
                         + w[0:1, :] * pltpu.roll(duc[p], ext - 2, 0)[:tm, :]).astype(BF16)
            own = duc[p][:tm, :]
            taps = [_colsum(own * (wins[p] if k == 2 else pltpu.roll(wins[p], 2 - k, 0))[FFN_HALO:FFN_HALO + tm, :])
                    for k in range(3)]

            @pl.when(first)
            def _():
                db_ref[p] = _colsum(own)
                for k in range(3):
                    dw_ref[p, k:k + 1, :] = taps[k]

            @pl.when(jnp.logical_not(first))
            def _():
                db_ref[p] += _colsum(own)
                for k in range(3):
                    dw_ref[p, k:k + 1, :] += taps[k]

    return pl.pallas_call(
        body, name=name, grid=(FF_NBLK, nt), in_specs=[main, prev, nxt, dmain, dnext, wsp, bsp], out_specs=[main, wsp, bsp],
        out_shape=[SDS(u4.shape, BF16), SDS((2, FF_NBLK, 3, FF_BLK), F32), SDS((2, FF_NBLK, 1, FF_BLK), F32)],
        compiler_params=_cp(("parallel", "arbitrary")))(u4, u4, u4, dact, dact, w4, b4)


def _sum8(parts, name):
    _, r, c = parts.shape
    tr = r
    for cand in (512, 256, 128, 64, 32, 16):
        if r % cand == 0 and r > cand:
            tr = cand
            break

    def body(p_ref, o_ref):
        acc = p_ref[0].astype(F32)
        for j in range(1, N_DEV):
            acc = acc + p_ref[j].astype(F32)
        o_ref[...] = acc

    return pl.pallas_call(
        body, name=name, grid=(r // tr,), in_specs=[pl.BlockSpec((N_DEV, tr, c), lambda i: (0, i, 0))],
        out_specs=pl.BlockSpec((tr, c), lambda i: (i, 0)), out_shape=SDS((r, c), F32),
        compiler_params=_cp(("parallel",)))(parts)


def _sum8_small(parts, name):
    n = len(parts)

    def body(*refs):
        for p_ref, o_ref in zip(refs[:n], refs[n:]):
            acc = p_ref[0]
            for j in range(1, N_DEV):
                acc = acc + p_ref[j]
            o_ref[...] = acc

    return pl.pallas_call(body, name=name, out_shape=[SDS(p.shape[1:], F32) for p in parts], compiler_params=_cp())(*parts)


def _adamw_math(w, g, m, v):
    m = ADAM_B1 * m + (1.0 - ADAM_B1) * g
    v = ADAM_B2 * v + (1.0 - ADAM_B2) * (g * g)
    m_hat = m / (1.0 - ADAM_B1 ** ADAM_STEP)
    v_hat = v / (1.0 - ADAM_B2 ** ADAM_STEP)
    return -ADAM_LR * (m_hat / (jnp.sqrt(v_hat) + ADAM_EPS) + ADAM_WD * w), m, v


def _adamw(w, g, m, v, name):
    r, c = w.shape
    tr = r
    for cand in (256, 128, 64, 32, 16, 8):
        if r % cand == 0 and r > cand:
            tr = cand
            break

    def body(w_ref, g_ref, m_ref, v_ref, d_ref, mo_ref, vo_ref):
        d_ref[...], mo_ref[...], vo_ref[...] = _adamw_math(w_ref[...], g_ref[...], m_ref[...], v_ref[...])

    blk = pl.BlockSpec((tr, c), lambda i: (i, 0))
    return pl.pallas_call(body, name=name, grid=(r // tr,), in_specs=[blk] * 4, out_specs=[blk] * 3,
                          out_shape=[SDS((r, c), F32)] * 3, compiler_params=_cp(("parallel",)))(w, g, m, v)


def _adamw_small(ws, gs, ms, vs, name):
    n = len(ws)

    def body(*refs):
        ins, outs = refs[:4 * n], refs[4 * n:]
        for i in range(n):
            d, m, v = _adamw_math(ins[i][...], ins[n + i][...], ins[2 * n + i][...], ins[3 * n + i][...])
            outs[i][...], outs[n + i][...], outs[2 * n + i][...] = d, m, v

    shapes = [SDS(w.shape, F32) for w in ws]
    res = pl.pallas_call(body, name=name, out_shape=shapes * 3, compiler_params=_cp())(*ws, *gs, *ms, *vs)
    return res[:n], res[n:2 * n], res[2 * n:]


def _perm_in(w):
    pad = jnp.zeros(w.shape[:-1] + (ZW - 2308,), w.dtype)
    return jnp.concatenate([w[..., :768], w[..., 772:], w[..., 768:772], pad], axis=-1)


def _unperm_in(g):
    return jnp.concatenate([g[..., :768], g[..., 2304:2308], g[..., 768:2304]], axis=-1)


def _wcat(sgu_w):
    return sgu_w.transpose(1, 0, 2).reshape(WIN, 4 * WIN)


def _layer_fwd(l, x, h1, mod, p, wg, last, target, nxt):
    s = x.shape[0]
    tag = f"_l{l}"
    mrow = lambda k: (mod, 6 * l + k)
    z = _mm_rows(h1, wg["w_in"].get(h1), "nn", ZW, F32, "mm_z" + tag)
    fcol, frow = _fgate(z, p["bf"], "fgate" + tag)
    y_fox, lse_fox = _fox_fwd(z, fcol, frow, "fox_fwd" + tag)
    y_conv, hc = _conv_fwd(z, wg["conv_w"], p["conv_b"], p["conv_ln_g"], p["conv_ln_b"], wg["conv_pw_w"], p["conv_pw_b"],
                           "conv_fwd" + tag)
    y_swa, lse_swa = _swa_fwd(z, p["sinks"], "swa_fwd" + tag)
    y_sgu = _sgu_fwd(z, p["sgu_ln_g"], p["sgu_ln_b"], p["wcat"], p["bfull"], "sgu_fwd" + tag)
    ys = (y_fox, y_conv, y_swa, y_sgu)
    yn = _gnorm(ys, (p["g_group"], l), "gnorm" + tag)
    o = _mm_rows(yn, wg["w_out"].get(yn), "nn", D, F32, "mm_o" + tag)
    x1, h2 = _post(x, o, mrow(2), (p["g_post_mix"], l), (p["g_pre_ffn"], l), mrow(4), mrow(3), "post_mix" + tag)
    tm = min(TM, s)
    u = _matmul(h2, wg["w_up"].get(h2), "nn", (N_DEV, s, FF_BLK), BF16, (N_DEV, 1, 1),
                _bs((s, D), lambda j, i, k: (0, 0)), _bs((None, D, FF_BLK), lambda j, i, k: (j, 0, 0)),
                _bs((None, s, FF_BLK), lambda j, i, k: (j, 0, 0)), None, "mm_u" + tag)
    u4 = u.reshape(2, FF_NBLK, s, FF_BLK)
    act = _ffn_act(u4, wg["ffn_conv_w"], p["ffn_conv_b"], "ffn_act" + tag)
    f = _matmul(act, wg["w_down"].get(act), "nn", (s, D), F32, (1, 1, FF_NBLK),
                _bs((None, s, FF_BLK), lambda i, j, k: (k, 0, 0)), _bs((FF_BLK, D), lambda i, j, k: (k, 0)),
                _bs((s, D), lambda i, j, k: (0, 0)), (s, D), "mm_f" + tag)
    if last:
        out = _post_loss(x1, f, mrow(5), (p["g_post_ffn"], l), target, "post_loss")
    else:
        out = _post(x1, f, mrow(5), (p["g_post_ffn"], l), *nxt, "post_ffn" + tag)
    saved = dict(x=x, h1=h1, z=z, fcol=fcol, frow=frow, lse_fox=lse_fox, hc=hc, lse_swa=lse_swa, ys=ys, yn=yn, o=o, x1=x1,
                 h2=h2, u4=u4, act=act, f=f)
    return out, saved


def _tie(a, token):
    return a if token is None else a + token[0, 0]


def _layer_bwd(l, dx2, sv, mod, p, wg, emit):
    s = dx2.shape[0]
    tm = min(TM, s)
    tag = f"_l{l}"
    mrow = lambda k: (mod, 6 * l + k)
    g = {}
    df, g["ga2"], g["g_post_ffn"] = _post_bwd(dx2, sv["f"], mrow(5), (p["g_post_ffn"], l), "post_ffn_bwd" + tag)
    dact = _matmul(df, wg["w_down"].get(None), "nt", (FF_NBLK, s, FF_BLK), BF16, (FF_NBLK, 1, 1),
                   _bs((s, D), lambda j, i, k: (0, 0)), _bs((FF_BLK, D), lambda j, i, k: (j, 0)),
                   _bs((None, s, FF_BLK), lambda j, i, k: (j, 0, 0)), None, "mm_dact" + tag)
    tok = emit("w_down", _matmul(sv["act"], df, "tn", (FF_NBLK * FF_BLK, D), BF16, (FF_NBLK, 1, 1),
                                 _bs((None, s, FF_BLK), lambda j, i, k: (j, 0, 0)), _bs((s, D), lambda j, i, k: (0, 0)),
                                 _bs((FF_BLK, D), lambda j, i, k: (j, 0)), None, "mm_dwdown" + tag))
    du, g["ffn_conv_w"], g["ffn_conv_b"] = _ffn_bwd(sv["u4"], dact, wg["ffn_conv_w"], _tie(p["ffn_conv_b"], tok),
                                                    "ffn_bwd" + tag)
    du = du.reshape(N_DEV, s, FF_BLK)
    dh2 = _matmul(du, wg["w_up"].get(None), "nt", (s, D), F32, (1, 1, N_DEV),
                  _bs((None, s, FF_BLK), lambda i, j, k: (k, 0, 0)), _bs((None, D, FF_BLK), lambda i, j, k: (k, 0, 0)),
                  _bs((s, D), lambda i, j, k: (0, 0)), (s, D), "mm_dh2" + tag)
    tok = emit("w_up", _matmul(sv["h2"], du, "tn", (N_DEV, D, FF_BLK), BF16, (N_DEV, 1, 1),
                               _bs((s, D), lambda j, i, k: (0, 0)), _bs((None, s, FF_BLK), lambda j, i, k: (j, 0, 0)),
                               _bs((None, D, FF_BLK), lambda j, i, k: (j, 0, 0)), None, "mm_dwup" + tag))
    dx1, g["sh2"], g["sc2"], g["g_pre_ffn"] = _pre_bwd(dh2, sv["x1"], dx2, (_tie(p["g_pre_ffn"], tok), l), mrow(4),
                                                       "pre_ffn_bwd" + tag)
    do, g["ga1"], g["g_post_mix"] = _post_bwd(dx1, sv["o"], mrow(2), (p["g_post_mix"], l), "post_mix_bwd" + tag)
    dyn = _mm_rows(do, wg["w_out"].get(None), "nt", D, F32, "mm_dyn" + tag)
    tok = emit("w_out", _mm_wgrad(sv["yn"], do, BF16, "mm_dwout" + tag))
    dy_fox, dy_conv, dy_swa, dy_sgu, g["g_group"] = _gnorm_bwd(dyn, sv["ys"], (_tie(p["g_group"], tok), l), "gnorm_bwd" + tag)
    z = sv["z"]
    dq_f, dk_f, dv_f, dfrow, dfcol = _fox_bwd(z, sv["fcol"], sv["frow"], sv["lse_fox"], sv["ys"][0], dy_fox, "fox_bwd" + tag)
    dgate, g["bf"] = _fgate_bwd(z, p["bf"], dfrow, dfcol, "fgate_bwd" + tag)
    dhc, g["conv_pw_w"], g["conv_pw_b"], g["conv_ln_g"], g["conv_ln_b"], g["conv_w"], g["conv_b"] = _conv_bwd_a(
        z, sv["hc"], dy_conv, wg["conv_w"], p["conv_ln_g"], p["conv_ln_b"], wg["conv_pw_w"], "conv_bwd_a" + tag)
    da_c, dg_c = _conv_bwd_b(z, dhc, wg["conv_w"], "conv_bwd_b" + tag)
    dq_s, dk_s, dv_s, g["sinks"] = _swa_bwd(z, p["sinks"], sv["lse_swa"], sv["ys"][2], dy_swa, "swa_bwd" + tag)
    du_g, dv_g, g["wcat"], g["sgu_bcol"], g["sgu_ln_g"], g["sgu_ln_b"] = _sgu_bwd(
        z, dy_sgu, p["sgu_ln_g"], p["sgu_ln_b"], p["wcat"], p["bfull"], "sgu_bwd" + tag)
    dz = jnp.concatenate([dq_f.astype(BF16), dk_f.astype(BF16), dv_f.astype(BF16), da_c, dg_c, dq_s.astype(BF16), dk_s.astype(BF16),
                          dv_s.astype(BF16), du_g, dv_g, dgate], axis=1)
    dh1 = _mm_rows(dz, wg["w_in"].get(None), "nt", D, F32, "mm_dh1" + tag)
    tok = emit("w_in", _mm_wgrad(sv["h1"], dz, BF16, "mm_dwin" + tag))
    dx, g["sh1"], g["sc1"], g["g_pre_mix"] = _pre_bwd(dh1, sv["x"], dx1, (_tie(p["g_pre_mix"], tok), l), mrow(1),
                                                      "pre_mix_bwd" + tag)
    return dx, g


def _layer_params(l, small, conv_w_full, conv_pw_full, ffn_conv_w_full):
    bf = jnp.pad(small["b_fgate"][l][None, :], ((0, 0), (0, 124)))
    p = dict(
        bf=bf, conv_b=small["conv_b"][l][None], conv_ln_g=small["conv_ln_g"][l][None], conv_ln_b=small["conv_ln_b"][l][None],
        conv_pw_b=small["conv_pw_b"][l][None], sinks=small["swa_sinks"][l], sgu_ln_g=small["sgu_ln_g"][l][None],
        sgu_ln_b=small["sgu_ln_b"][l][None], wcat=_wcat(small["sgu_w"][l]),
        bfull=jnp.repeat(small["sgu_b"][l].T, HD, axis=1),
        ffn_conv_b=small["ffn_conv_b"][l].reshape(2, FF_NBLK, 1, FF_BLK),
        g_group=small["g_group"].reshape(N_LAYER, 1, D), g_post_mix=small["g_post_mix"].reshape(N_LAYER, 1, D),
        g_pre_ffn=small["g_pre_ffn"].reshape(N_LAYER, 1, D), g_post_ffn=small["g_post_ffn"].reshape(N_LAYER, 1, D),
        g_pre_mix=small["g_pre_mix"].reshape(N_LAYER, 1, D))
    wsmall = dict(conv_w=conv_w_full[l], conv_pw_w=conv_pw_full[l].astype(BF16),
                  ffn_conv_w=ffn_conv_w_full[l].reshape(3, 2, FF_NBLK, FF_BLK).transpose(1, 2, 0, 3))
    return p, wsmall


def _local_step(x, target, mod, small, wbig, conv_w_full, conv_pw_full, ffn_conv_w_full, emit):
    ps, wgs = [], []
    for l in range(N_LAYER):
        p, wsmall = _layer_params(l, small, conv_w_full, conv_pw_full, ffn_conv_w_full)
        ps.append(p)
        wgs.append({**wbig[l], **wsmall})
    h = _rms_mod(x, (ps[0]["g_pre_mix"], 0), (mod, 1), (mod, 0), "rms_mod_l0")
    saved = []
    for l in range(N_LAYER):
        last = l == N_LAYER - 1
        nxt = None if last else ((ps[l]["g_pre_mix"], l + 1), (mod, 6 * (l + 1) + 1), (mod, 6 * (l + 1)))
        out, sv = _layer_fwd(l, x, h, mod, ps[l], wgs[l], last, target, nxt)
        saved.append(sv)
        if not last:
            x, h = out
    dx, loss = out
    grads = [None] * N_LAYER
    for l in reversed(range(N_LAYER)):
        dx, grads[l] = _layer_bwd(l, dx, saved[l], mod, ps[l], wgs[l], functools.partial(emit, l))
    return loss, dx, grads


_SMALL = ("b_ada", "g_pre_mix", "g_post_mix", "g_pre_ffn", "g_post_ffn", "b_fgate", "conv_b", "conv_ln_g", "conv_ln_b",
          "conv_pw_b", "swa_sinks", "sgu_ln_g", "sgu_ln_b", "sgu_w", "sgu_b", "g_group", "ffn_conv_b")
_WEIGHTS = ("w_ada", "b_ada", "g_pre_mix", "g_post_mix", "g_pre_ffn", "g_post_ffn", "w_in", "b_fgate", "conv_w", "conv_b",
            "conv_ln_g", "conv_ln_b", "conv_pw_w", "conv_pw_b", "swa_sinks", "sgu_ln_g", "sgu_ln_b", "sgu_w", "sgu_b",
            "g_group", "w_out", "ffn_w_up", "ffn_conv_w", "ffn_conv_b", "ffn_w_down")


def _pad_rows(a, mult):
    r = (-a.shape[0]) % mult
    return a if r == 0 else jnp.concatenate([a, jnp.zeros((r,) + a.shape[1:], a.dtype)], axis=0)


def _view2d(a):
    if a.ndim == 2:
        return a
    return a.reshape(-1, a.shape[-1])


def kernel(x, c, w_ada, b_ada, g_pre_mix, g_post_mix, g_pre_ffn, g_post_ffn, w_in, b_fgate, conv_w, conv_b, conv_ln_g, conv_ln_b, conv_pw_w, conv_pw_b, swa_sinks, sgu_ln_g, sgu_ln_b, sgu_w, sgu_b, g_group, w_out, ffn_w_up, ffn_conv_w, ffn_conv_b, ffn_w_down, loss_target, m_w_ada, m_b_ada, m_g_pre_mix, m_g_post_mix, m_g_pre_ffn, m_g_post_ffn, m_w_in, m_b_fgate, m_conv_w, m_conv_b, m_conv_ln_g, m_conv_ln_b, m_conv_pw_w, m_conv_pw_b, m_swa_sinks, m_sgu_ln_g, m_sgu_ln_b, m_sgu_w, m_sgu_b, m_g_group, m_w_out, m_ffn_w_up, m_ffn_conv_w, m_ffn_conv_b, m_ffn_w_down, v_w_ada, v_b_ada, v_g_pre_mix, v_g_post_mix, v_g_pre_ffn, v_g_post_ffn, v_w_in, v_b_fgate, v_conv_w, v_conv_b, v_conv_ln_g, v_conv_ln_b, v_conv_pw_w, v_conv_pw_b, v_swa_sinks, v_sgu_ln_g, v_sgu_ln_b, v_sgu_w, v_sgu_b, v_g_group, v_w_out, v_ffn_w_up, v_ffn_conv_w, v_ffn_conv_b, v_ffn_w_down):
    env = dict(locals())
    w = {n: env[n] for n in _WEIGHTS}
    mom = {n: env["m_" + n] for n in _WEIGHTS}
    var = {n: env["v_" + n] for n in _WEIGHTS}
    me = 4 * lax.axis_index("x") + 2 * lax.axis_index("y") + lax.axis_index("c")
    x2, target = x[0], loss_target[0]

    (c_all,) = _exchange([c], ["bcast"], "gather_c")
    c_all = c_all.reshape(N_DEV, D)
    (m_all,) = _exchange([_ada_fwd(c_all, w_ada)], ["bcast"], "gather_mod")
    m_mine = lax.dynamic_index_in_dim(m_all, me, axis=2, keepdims=False)
    mod, mod_token = _ada_finish(m_mine.transpose(1, 0, 2).reshape(N_LAYER, 6 * D), b_ada)
    mod = mod.reshape(6 * N_LAYER, 1, D)

    shards = [_tie(conv_w, mod_token), conv_pw_w, ffn_conv_w]
    for l in range(N_LAYER):
        shards += [_perm_in(w_in[l]).astype(BF16), w_out[l].astype(BF16), ffn_w_up[l].astype(BF16), ffn_w_down[l].astype(BF16)]
    flight = _xchg_start(shards, ["bcast"] * len(shards), "gather_weights_start")
    mod = _tie(mod, flight.token)
    g_cw, g_pw, g_fcw = _xchg_wait(flight, [0, 1, 2], mod, "gather_small_wait")
    conv_w_full = g_cw.transpose(1, 2, 0, 3).reshape(N_LAYER, CONV_K, GW)
    conv_pw_full = g_pw.transpose(1, 0, 2, 3).reshape(N_LAYER, GW, GW)
    ffn_conv_w_full = g_fcw.transpose(1, 2, 0, 3).reshape(N_LAYER, 3, N_DEV * FF_BLK)

    def lazy(i, shape, name):
        return _Lazy(lambda after: _xchg_wait(flight, [i], after, name)[0].reshape(shape))

    wbig = [dict(w_in=lazy(3 + 4 * l, (D, ZW), f"wait_w_in_l{l}"), w_out=lazy(4 + 4 * l, (D, D), f"wait_w_out_l{l}"),
                 w_up=lazy(5 + 4 * l, (N_DEV, D, FF_BLK), f"wait_w_up_l{l}"),
                 w_down=lazy(6 + 4 * l, (FF_NBLK * FF_BLK, D), f"wait_w_down_l{l}")) for l in range(N_LAYER)]

    grad_flights = []

    def emit(l, key, arr):
        fl = _xchg_start([arr.reshape(N_DEV, -1, arr.shape[-1])], ["a2a"], f"grad_start_{key}_l{l}")
        grad_flights.append(((l, key), fl))
        return fl.token

    small = {n: w[n] for n in _SMALL}
    loss8, dx, grads = _local_step(x2, target, mod, small, wbig, conv_w_full, conv_pw_full, ffn_conv_w_full, emit)
    loss = lax.psum(loss8[0, 0], ("x", "y", "c"))
    grad_x = dx[None]
    big_sum = {}
    for (l, key), fl in grad_flights:
        (parts,) = _xchg_wait(fl, [0], dx, f"grad_wait_{key}_l{l}")
        big_sum[(l, key)] = _sum8(parts, f"sum_{key}_l{l}")


    st = lambda key: jnp.stack([grads[l][key] for l in range(N_LAYER)])
    d_conv_w = st("conv_w")[:, :CONV_K, :].reshape(N_LAYER, CONV_K, N_DEV, GW // N_DEV).transpose(2, 0, 1, 3)
    d_pw_w = st("conv_pw_w").reshape(N_LAYER, N_DEV, GW // N_DEV, GW).transpose(1, 0, 2, 3)
    d_fcw = st("ffn_conv_w").reshape(N_LAYER, N_DEV, 3, FF_BLK).transpose(1, 0, 2, 3)
    rows_d = _pad_rows(jnp.concatenate(
        [grads[l][k] for l in range(N_LAYER) for k in ("sh1", "sc1", "ga1", "sh2", "sc2", "ga2")]
        + [grads[l][k] for k in ("g_pre_mix", "g_post_mix", "g_pre_ffn", "g_post_ffn", "g_group") for l in range(N_LAYER)],
        axis=0), 8)
    rows_gw = _pad_rows(jnp.concatenate(
        [grads[l][k] for k in ("conv_b", "conv_ln_g", "conv_ln_b", "conv_pw_b", "sgu_ln_g", "sgu_ln_b") for l in range(N_LAYER)],
        axis=0), 8)
    rows_128 = jnp.concatenate([_pad_rows(jnp.concatenate([grads[l]["bf"] for l in range(N_LAYER)]
                                                          + [grads[l]["sinks"] for l in range(N_LAYER)], axis=0), 8)]
                               + [grads[l]["sgu_bcol"] for l in range(N_LAYER)], axis=0)
    rows_w = jnp.concatenate([grads[l]["wcat"] for l in range(N_LAYER)], axis=0)
    rows_fb = st("ffn_conv_b").reshape(N_LAYER * N_DEV, FF_BLK)
    small_parts = _exchange([d_conv_w, d_pw_w, d_fcw, rows_d, rows_gw, rows_128, rows_w, rows_fb],
                            ["a2a"] * 3 + ["bcast"] * 5, "exchange_small_grads")
    s_conv_w, s_pw_w, s_fcw, s_d, s_gw, s_128, s_w, s_fb = _sum8_small(
        [p.reshape(N_DEV, -1, p.shape[-1]) for p in small_parts], "sum_small_grads")

    gr = {}
    gr["w_in"] = _unperm_in(jnp.stack([big_sum[(l, "w_in")] for l in range(N_LAYER)]))
    gr["w_out"] = jnp.stack([big_sum[(l, "w_out")] for l in range(N_LAYER)])
    gr["ffn_w_up"] = jnp.stack([big_sum[(l, "w_up")] for l in range(N_LAYER)])
    gr["ffn_w_down"] = jnp.stack([big_sum[(l, "w_down")] for l in range(N_LAYER)])
    gr["conv_w"] = s_conv_w.reshape(N_LAYER, CONV_K, GW // N_DEV)
    gr["conv_pw_w"] = s_pw_w.reshape(N_LAYER, GW // N_DEV, GW)
    gr["ffn_conv_w"] = s_fcw.reshape(N_LAYER, 3, FF_BLK)
    gr["b_ada"] = s_d[:6 * N_LAYER].reshape(N_LAYER, 6 * D)
    for i, k in enumerate(("g_pre_mix", "g_post_mix", "g_pre_ffn", "g_post_ffn", "g_group")):
        gr[k] = s_d[6 * N_LAYER + 2 * i:6 * N_LAYER + 2 * i + 2]
    for i, k in enumerate(("conv_b", "conv_ln_g", "conv_ln_b", "conv_pw_b", "sgu_ln_g", "sgu_ln_b")):
        gr[k] = s_gw[2 * i:2 * i + 2]
    gr["b_fgate"] = s_128[0:2, :4]
    gr["swa_sinks"] = s_128[2:4, :4]
    gr["sgu_b"] = s_128[8:].reshape(N_LAYER, WIN, 128)[:, :, :4].transpose(0, 2, 1)
    gr["sgu_w"] = s_w.reshape(N_LAYER, WIN, 4, WIN).transpose(0, 2, 1, 3)
    gr["ffn_conv_b"] = s_fb.reshape(N_LAYER, N_DEV * FF_BLK)
    dmod_all = small_parts[3][:, :6 * N_LAYER, :].reshape(N_DEV, N_LAYER, 6 * D)
    ncol = 6 * D // N_DEV
    dmod_cols = lax.dynamic_slice_in_dim(dmod_all, me * ncol, ncol, axis=2).transpose(1, 0, 2)
    gr["w_ada"] = _ada_bwd(c_all, dmod_cols)

    delta, new_m, new_v = {}, {}, {}
    bigs = ("w_ada", "w_in", "w_out", "ffn_w_up", "ffn_w_down")
    for n in bigs:
        d, m2, v2 = _adamw(_view2d(w[n]), _view2d(gr[n]), _view2d(mom[n]), _view2d(var[n]), "adamw_" + n)
        delta[n], new_m[n], new_v[n] = d.reshape(w[n].shape), m2.reshape(w[n].shape), v2.reshape(w[n].shape)
    smalls = [n for n in _WEIGHTS if n not in bigs]
    ds, ms, vs = _adamw_small([_view2d(w[n]) for n in smalls], [_view2d(gr[n]) for n in smalls],
                              [_view2d(mom[n]) for n in smalls], [_view2d(var[n]) for n in smalls], "adamw_small")
    for i, n in enumerate(smalls):
        delta[n], new_m[n], new_v[n] = ds[i].reshape(w[n].shape), ms[i].reshape(w[n].shape), vs[i].reshape(w[n].shape)

    return (loss, grad_x, *[gr[n].reshape(w[n].shape) for n in _WEIGHTS], *[delta[n] for n in _WEIGHTS],
            *[new_m[n] for n in _WEIGHTS], *[new_v[n] for n in _WEIGHTS])
```

```python
import functools

import jax
import jax.numpy as jnp
from jax import lax
from jax.experimental import pallas as pl
from jax.experimental.pallas import tpu as pltpu

F32, BF16 = jnp.float32, jnp.bfloat16
SDS = jax.ShapeDtypeStruct
MESH = pl.DeviceIdType.MESH

N_DEV = 8
D = 1024
GW = 256
HD = 64
N_LAYER = 2
ZW = 2432
FF_BLK = 704
FF_NBLK = 4
CONV_K = 31
CONV_HALO = 32
FFN_HALO = 16
EPS = 1e-6
NEG = -1e30
SCALE = HD ** -0.5
VMEM_LIMIT_V7X = 56 * 1024 * 1024
TM = 512
WGRAD_ROWS = 256
TQ = 256
WIN = 128

ADAM_LR, ADAM_B1, ADAM_B2, ADAM_EPS, ADAM_WD, ADAM_STEP = 0.001, 0.9, 0.999, 1e-08, 0.01, 10

Z_FQ, Z_FK, Z_FV, Z_CA, Z_CG, Z_SQ = 0, 1, 2, 3, 4, 5
Z_SK, Z_SV = 12, 13
Z_GU, Z_GV = 7, 8
Z_FG = 18


def _cp(sem=None):
    return pltpu.CompilerParams(dimension_semantics=sem, vmem_limit_bytes=VMEM_LIMIT_V7X)


def _vec(arr3, idx, ngrid):
    w = arr3.shape[-1]
    if ngrid == 1:
        return pl.BlockSpec((None, 1, w), lambda i: (idx, 0, 0))
    return pl.BlockSpec((None, 1, w), lambda i, j: (idx, 0, 0))


def _sigmoid(x):
    return jax.nn.sigmoid(x)


def _silu(x):
    return x * _sigmoid(x)


def _dsilu(x):
    s = _sigmoid(x)
    return s * (1.0 + x * (1.0 - s))


_G0, _G1 = 0.7978845608028654, 0.044715


def _gelu(x):
    return 0.5 * x * (1.0 + jnp.tanh(_G0 * (x + _G1 * x * x * x)))


def _dgelu(x):
    t = jnp.tanh(_G0 * (x + _G1 * x * x * x))
    return 0.5 * (1.0 + t) + 0.5 * x * (1.0 - t * t) * (_G0 * (1.0 + 3.0 * _G1 * x * x))


def _rstd(x):
    return lax.rsqrt(jnp.mean(x * x, axis=-1, keepdims=True) + EPS)


def _rms_bwd(xh, r, t):
    return r * (t - xh * jnp.mean(t * xh, axis=-1, keepdims=True))


def _ln_stats(x):
    mu = jnp.mean(x, axis=-1, keepdims=True)
    xc = x - mu
    rstd = lax.rsqrt(jnp.mean(xc * xc, axis=-1, keepdims=True) + EPS)
    return xc * rstd, rstd


def _ln_bwd(xh, rstd, dxh):
    return rstd * (dxh - jnp.mean(dxh, axis=-1, keepdims=True) - xh * jnp.mean(dxh * xh, axis=-1, keepdims=True))


def _colsum(x):
    return jnp.sum(x, axis=0, keepdims=True)


def _dot(a, b, kind):
    dn = {"nn": (((1,), (0,)), ((), ())), "nt": (((1,), (1,)), ((), ())), "tn": (((0,), (0,)), ((), ()))}[kind]
    return lax.dot_general(a.astype(BF16), b.astype(BF16), dn, preferred_element_type=F32)


def _exchange(arrs, modes, name):
    n = len(arrs)
    outs = [SDS((N_DEV,) + a.shape, a.dtype) if m == "bcast" else SDS(a.shape, a.dtype) for a, m in zip(arrs, modes)]

    def body(*refs):
        ins, dst = refs[:n], refs[n:2 * n]
        send, recv, loc = refs[2 * n:]
        x, y, c = lax.axis_index("x"), lax.axis_index("y"), lax.axis_index("c")
        me = 4 * x + 2 * y + c

        def src(a, j):
            return ins[a] if modes[a] == "bcast" else ins[a].at[j]

        local = [pltpu.make_async_copy(src(a, me), dst[a].at[me], loc.at[a]) for a in range(n)]
        for cp in local:
            cp.start()
        sent, landed = [], []
        for k in (2, 4, 6, 3, 5, 7, 1):
            px = 1 - x if k & 4 else x
            py = 1 - y if k & 2 else y
            pc = 1 - c if k & 1 else c
            peer = 4 * px + 2 * py + pc
            for a in range(n):
                cp = pltpu.make_async_remote_copy(src_ref=src(a, peer), dst_ref=dst[a].at[me], send_sem=send.at[a, k - 1],
                                                  recv_sem=recv.at[a, k - 1], device_id=(px, py, pc), device_id_type=MESH)
                cp.start()
                sent.append(cp)
                landed.append(pltpu.make_async_remote_copy(src_ref=src(a, peer), dst_ref=dst[a].at[peer],
                                                           send_sem=send.at[a, k - 1], recv_sem=recv.at[a, k - 1],
                                                           device_id=(px, py, pc), device_id_type=MESH))
        for cp in landed:
            cp.wait_recv()
        for cp in sent:
            cp.wait_send()
        for cp in local:
            cp.wait()

    hbm = pl.BlockSpec(memory_space=pltpu.HBM)
    return pl.pallas_call(
        body, name=name, out_shape=outs, in_specs=[hbm] * n, out_specs=[hbm] * n,
        scratch_shapes=[pltpu.SemaphoreType.DMA((n, N_DEV - 1)), pltpu.SemaphoreType.DMA((n, N_DEV - 1)),
                        pltpu.SemaphoreType.DMA((n,))],
        compiler_params=pltpu.CompilerParams(has_side_effects=True),
    )(*arrs)


_PEER_ORDER = (2, 4, 6, 3, 5, 7, 1)
_HBM = pl.BlockSpec(memory_space=pltpu.HBM)
_SEM = pl.BlockSpec(memory_space=pltpu.SEMAPHORE)
_EFFECT = pltpu.SideEffectType.DATAFLOW_SIDE_EFFECTING


def _peer(k):
    x, y, c = lax.axis_index("x"), lax.axis_index("y"), lax.axis_index("c")
    px = 1 - x if k & 4 else x
    py = 1 - y if k & 2 else y
    pc = 1 - c if k & 1 else c
    return (px, py, pc), 4 * px + 2 * py + pc


def _my_id():
    return 4 * lax.axis_index("x") + 2 * lax.axis_index("y") + lax.axis_index("c")


def _split_copies(src_ref, land_ref, send, recv, loc, mode):
    me = _my_id()
    pick = (lambda j: src_ref) if mode == "bcast" else (lambda j: src_ref.at[j])
    local = pltpu.make_async_copy(pick(me), land_ref.at[me], loc)
    remote = []
    for k in _PEER_ORDER:
        dev, peer = _peer(k)
        out = pltpu.make_async_remote_copy(src_ref=pick(peer), dst_ref=land_ref.at[me], send_sem=send.at[k - 1],
                                           recv_sem=recv.at[k - 1], device_id=dev, device_id_type=MESH)
        arrive = pltpu.make_async_remote_copy(src_ref=pick(peer), dst_ref=land_ref.at[peer], send_sem=send.at[k - 1],
                                              recv_sem=recv.at[k - 1], device_id=dev, device_id_type=MESH)
        remote.append((out, arrive))
    return local, remote


class _Flight:
    def __init__(self, srcs, lands, sends, recvs, locs, modes, token):
        self.srcs, self.lands, self.sends, self.recvs, self.locs, self.modes, self.token = (
            list(srcs), list(lands), list(sends), list(recvs), list(locs), list(modes), token)


def _xchg_start(arrs, modes, name):
    n = len(arrs)
    lands = [lax.empty((N_DEV,) + a.shape if m == "bcast" else a.shape, a.dtype) for a, m in zip(arrs, modes)]

    def body(*refs):
        srcs, lnds = refs[:n], refs[n:2 * n]
        outs = refs[2 * n:]
        sends, recvs, locs, token = outs[:n], outs[n:2 * n], outs[2 * n:3 * n], outs[5 * n]
        for a in range(n):
            local, remote = _split_copies(srcs[a], lnds[a], sends[a], recvs[a], locs[a], modes[a])
            local.start()
            for out, _ in remote:
                out.start()
        token[...] = jnp.zeros_like(token)

    sem7 = pltpu.SemaphoreType.DMA((N_DEV - 1,))
    res = pl.pallas_call(
        body, name=name,
        out_shape=[sem7] * (2 * n) + [pltpu.SemaphoreType.DMA(())] * n + [pltpu.HBM(a.shape, a.dtype) for a in arrs]
        + [pltpu.HBM(b.shape, b.dtype) for b in lands] + [SDS((8, 128), F32)],
        in_specs=[_HBM] * (2 * n), out_specs=[_SEM] * (3 * n) + [_HBM] * (2 * n) + [pl.BlockSpec(memory_space=pltpu.VMEM)],
        input_output_aliases={i: 3 * n + i for i in range(2 * n)},
        compiler_params=pltpu.CompilerParams(has_side_effects=_EFFECT),
    )(*[pltpu.with_memory_space_constraint(a, pltpu.HBM) for a in arrs],
      *[pltpu.with_memory_space_constraint(b, pltpu.HBM) for b in lands])
    return _Flight(res[3 * n:4 * n], res[4 * n:5 * n], res[:n], res[n:2 * n], res[2 * n:3 * n], modes, res[5 * n])


def _xchg_wait(flight, idx, after, name):
    n = len(idx)
    modes = [flight.modes[i] for i in idx]

    def body(*refs):
        srcs, lnds = refs[:n], refs[n:2 * n]
        sends, recvs, locs = refs[2 * n:3 * n], refs[3 * n:4 * n], refs[4 * n:5 * n]
        for a in range(n):
            local, remote = _split_copies(srcs[a], lnds[a], sends[a], recvs[a], locs[a], modes[a])
            local.wait()
            for _, arrive in remote:
                arrive.wait_send()
                arrive.wait_recv()

    ops = ([flight.srcs[i] for i in idx] + [flight.lands[i] for i in idx] + [flight.sends[i] for i in idx]
           + [flight.recvs[i] for i in idx] + [flight.locs[i] for i in idx])
    res = pl.pallas_call(
        body, name=name, out_shape=[pltpu.HBM(o.shape, o.dtype) for o in ops[:2 * n]],
        in_specs=[_HBM] * (2 * n) + [_SEM] * (3 * n) + [pl.BlockSpec(memory_space=pl.ANY)], out_specs=[_HBM] * (2 * n),
        input_output_aliases={i: i for i in range(2 * n)},
        compiler_params=pltpu.CompilerParams(has_side_effects=_EFFECT),
    )(*ops, after)
    return res[n:]


class _Lazy:
    def __init__(self, fn):
        self.fn, self.val = fn, None

    def get(self, after):
        if self.val is None:
            self.val = self.fn(after)
        return self.val


def _matmul(a, b, kind, out_shape, out_dtype, grid, a_spec, b_spec, o_spec, acc_shape, name):
    nk = grid[2]

    def body(a_ref, b_ref, o_ref, *scratch):
        prod = _dot(a_ref[...], b_ref[...], kind)
        if nk == 1:
            o_ref[...] = prod.astype(out_dtype)
        else:
            acc = scratch[0]
            k = pl.program_id(2)

            @pl.when(k == 0)
            def _():
                acc[...] = prod

            @pl.when(k > 0)
            def _():
                acc[...] += prod

            @pl.when(k == nk - 1)
            def _():
                o_ref[...] = acc[...].astype(out_dtype)

    return pl.pallas_call(
        body, name=name, grid=grid, in_specs=[a_spec, b_spec], out_specs=o_spec, out_shape=SDS(out_shape, out_dtype),
        scratch_shapes=[] if nk == 1 else [pltpu.VMEM(acc_shape, F32)],
        compiler_params=_cp(("parallel", "parallel", "arbitrary")))(a, b)


def _bs(shape, fn):
    return pl.BlockSpec(shape, fn)


def _mm_rows(a, w, kind, n_out, out_dtype, name):
    s, k = a.shape
    tm = min(TM, s)
    return _matmul(a, w, kind, (s, n_out), out_dtype, (s // tm, 1, 1),
                   _bs((tm, k), lambda i, j, kk: (i, 0)), _bs(w.shape, lambda i, j, kk: (0, 0)),
                   _bs((tm, n_out), lambda i, j, kk: (i, 0)), None, name)


def _mm_wgrad(a, dy, out_dtype, name):
    s, k = a.shape
    n = dy.shape[1]
    tko = min(WGRAD_ROWS, k)
    return _matmul(a, dy, "tn", (k, n), out_dtype, (k // tko, 1, 1),
                   _bs((s, tko), lambda i, j, kk: (0, i)), _bs((s, n), lambda i, j, kk: (0, 0)),
                   _bs((tko, n), lambda i, j, kk: (i, 0)), None, name)


def _ada_fwd(c_all, w_ada):
    ncol = w_ada.shape[2]

    def body(c_ref, w_ref, o_ref):
        ca = _silu(c_ref[...])
        ca = jnp.concatenate([ca, jnp.zeros_like(ca)], axis=0)
        o_ref[...] = _dot(ca, w_ref[...], "nn")[:N_DEV, :]

    return pl.pallas_call(
        body, name="ada_fwd", grid=(N_LAYER,),
        in_specs=[pl.BlockSpec((N_DEV, D), lambda l: (0, 0)), pl.BlockSpec((None, D, ncol), lambda l: (l, 0, 0))],
        out_specs=pl.BlockSpec((None, N_DEV, ncol), lambda l: (l, 0, 0)),
        out_shape=SDS((N_LAYER, N_DEV, ncol), F32), compiler_params=_cp(("parallel",)))(c_all, w_ada)


def _ada_finish(m_mine, b_ada):
    def body(m_ref, b_ref, o_ref, t_ref):
        o_ref[...] = m_ref[...] + b_ref[...]
        t_ref[...] = jnp.zeros_like(t_ref)

    return pl.pallas_call(body, name="ada_finish", out_shape=[SDS(b_ada.shape, F32), SDS((8, 128), F32)])(m_mine, b_ada)


def _ada_bwd(c_all, dmod_cols):
    ncol = dmod_cols.shape[2]

    def body(c_ref, d_ref, o_ref):
        ca = _silu(c_ref[...])
        ca = jnp.concatenate([ca, jnp.zeros_like(ca)], axis=0)
        dm = d_ref[...]
        dm = jnp.concatenate([dm, jnp.zeros_like(dm)], axis=0)
        o_ref[...] = _dot(ca, dm, "tn")

    return pl.pallas_call(
        body, name="ada_bwd", grid=(N_LAYER,),
        in_specs=[pl.BlockSpec((N_DEV, D), lambda l: (0, 0)), pl.BlockSpec((None, N_DEV, ncol), lambda l: (l, 0, 0))],
        out_specs=pl.BlockSpec((None, D, ncol), lambda l: (l, 0, 0)),
        out_shape=SDS((N_LAYER, D, ncol), F32), compiler_params=_cp(("parallel",)))(c_all, dmod_cols)


def _rows(s):
    tm = min(TM, s)
    return tm, pl.BlockSpec((tm, D), lambda i: (i, 0))


def _rms_mod(x, g, sc, sh, name):
    s = x.shape[0]
    tm, row = _rows(s)

    def body(x_ref, g_ref, sc_ref, sh_ref, h_ref):
        xf = x_ref[...]
        h_ref[...] = (xf * _rstd(xf) * (g_ref[...] * (1.0 + sc_ref[...])) + sh_ref[...]).astype(BF16)

    return pl.pallas_call(
        body, name=name, grid=(s // tm,), in_specs=[row, _vec(*g, 1), _vec(*sc, 1), _vec(*sh, 1)], out_specs=row,
        out_shape=SDS((s, D), BF16), compiler_params=_cp(("parallel",)))(x, g[0], sc[0], sh[0])


def _post(xres, o, ga, gpost, gn, scn, shn, name):
    s = xres.shape[0]
    tm, row = _rows(s)

    def body(x_ref, o_ref, ga_ref, gp_ref, gn_ref, sc_ref, sh_ref, xn_ref, h_ref):
        of = o_ref[...]
        xn = x_ref[...] + ga_ref[...] * (of * _rstd(of) * gp_ref[...])
        xn_ref[...] = xn
        h_ref[...] = (xn * _rstd(xn) * (gn_ref[...] * (1.0 + sc_ref[...])) + sh_ref[...]).astype(BF16)

    return pl.pallas_call(
        body, name=name, grid=(s // tm,),
        in_specs=[row, row, _vec(*ga, 1), _vec(*gpost, 1), _vec(*gn, 1), _vec(*scn, 1), _vec(*shn, 1)],
        out_specs=[row, row], out_shape=[SDS((s, D), F32), SDS((s, D), BF16)],
        compiler_params=_cp(("parallel",)))(xres, o, ga[0], gpost[0], gn[0], scn[0], shn[0])


def _post_loss(xres, o, ga, gpost, target, name):
    s = xres.shape[0]
    tm, row = _rows(s)

    def body(x_ref, o_ref, ga_ref, gp_ref, t_ref, dy_ref, loss_ref):
        of = o_ref[...]
        err = x_ref[...] + ga_ref[...] * (of * _rstd(of) * gp_ref[...]) - t_ref[...]
        dy_ref[...] = err * (1.0 / D)

        @pl.when(pl.program_id(0) == 0)
        def _():
            loss_ref[...] = jnp.zeros_like(loss_ref)

        loss_ref[...] += jnp.sum(jnp.mean(err * err, axis=-1, keepdims=True), axis=0, keepdims=True) * 0.5

    return pl.pallas_call(
        body, name=name, grid=(s // tm,), in_specs=[row, row, _vec(*ga, 1), _vec(*gpost, 1), row],
        out_specs=[row, pl.BlockSpec((8, 128), lambda i: (0, 0))], out_shape=[SDS((s, D), F32), SDS((8, 128), F32)],
        compiler_params=_cp(("arbitrary",)))(xres, o, ga[0], gpost[0], target)


def _acc(ref, val, first):
    @pl.when(first)
    def _():
        ref[...] = val

    @pl.when(jnp.logical_not(first))
    def _():
        ref[...] += val


def _post_bwd(dxn, o, ga, gpost, name):
    s = dxn.shape[0]
    tm, row = _rows(s)
    vec = pl.BlockSpec((1, D), lambda i: (0, 0))

    def body(d_ref, o_ref, ga_ref, gp_ref, do_ref, dga_ref, dgp_ref):
        of, dx = o_ref[...], d_ref[...]
        r = _rstd(of)
        oh = of * r
        do_ref[...] = _rms_bwd(oh, r, dx * (ga_ref[...] * gp_ref[...])).astype(BF16)
        cs = _colsum(dx * oh)
        first = pl.program_id(0) == 0
        _acc(dga_ref, cs * gp_ref[...], first)
        _acc(dgp_ref, cs * ga_ref[...], first)

    return pl.pallas_call(
        body, name=name, grid=(s // tm,), in_specs=[row, row, _vec(*ga, 1), _vec(*gpost, 1)], out_specs=[row, vec, vec],
        out_shape=[SDS((s, D), BF16), SDS((1, D), F32), SDS((1, D), F32)],
        compiler_params=_cp(("arbitrary",)))(dxn, o, ga[0], gpost[0])


def _pre_bwd(dh, x, dres, g, sc, name):
    s = x.shape[0]
    tm, row = _rows(s)
    vec = pl.BlockSpec((1, D), lambda i: (0, 0))

    def body(dh_ref, x_ref, dr_ref, g_ref, sc_ref, dx_ref, dsh_ref, dsc_ref, dg_ref):
        xf, d = x_ref[...], dh_ref[...]
        r = _rstd(xf)
        xh = xf * r
        dx_ref[...] = dr_ref[...] + _rms_bwd(xh, r, d * (g_ref[...] * (1.0 + sc_ref[...])))
        cs = _colsum(d * xh)
        first = pl.program_id(0) == 0
        _acc(dsh_ref, _colsum(d), first)
        _acc(dsc_ref, cs * g_ref[...], first)
        _acc(dg_ref, cs * (1.0 + sc_ref[...]), first)

    return pl.pallas_call(
        body, name=name, grid=(s // tm,), in_specs=[row, row, row, _vec(*g, 1), _vec(*sc, 1)],
        out_specs=[row, vec, vec, vec],
        out_shape=[SDS((s, D), F32), SDS((1, D), F32), SDS((1, D), F32), SDS((1, D), F32)],
        compiler_params=_cp(("arbitrary",)))(dh, x, dres, g[0], sc[0])


def _gnorm(ys, gg, name):
    s = ys[0].shape[0]
    tm = min(TM, s)
    yb = pl.BlockSpec((tm, GW), lambda i: (i, 0))

    def body(y0, y1, y2, y3, g_ref, o_ref):
        for i, yr in enumerate((y0, y1, y2, y3)):
            y = yr[...]
            o_ref[:, GW * i:GW * (i + 1)] = (y * _rstd(y) * g_ref[:, GW * i:GW * (i + 1)]).astype(BF16)

    return pl.pallas_call(
        body, name=name, grid=(s // tm,), in_specs=[yb] * 4 + [_vec(*gg, 1)], out_specs=pl.BlockSpec((tm, D), lambda i: (i, 0)),
        out_shape=SDS((s, D), BF16), compiler_params=_cp(("parallel",)))(*ys, gg[0])


def _gnorm_bwd(dyn, ys, gg, name):
    s = ys[0].shape[0]
    tm = min(TM, s)
    yb = pl.BlockSpec((tm, GW), lambda i: (i, 0))

    def body(d_ref, y0, y1, y2, y3, g_ref, o0, o1, o2, o3, dg_ref):
        first = pl.program_id(0) == 0
        for i, (yr, orf) in enumerate(zip((y0, y1, y2, y3), (o0, o1, o2, o3))):
            y = yr[...]
            d = d_ref[:, GW * i:GW * (i + 1)]
            r = _rstd(y)
            yh = y * r
            orf[...] = _rms_bwd(yh, r, d * g_ref[:, GW * i:GW * (i + 1)])
            cs = _colsum(d * yh)

            @pl.when(first)
            def _():
                dg_ref[:, GW * i:GW * (i + 1)] = cs

            @pl.when(jnp.logical_not(first))
            def _():
                dg_ref[:, GW * i:GW * (i + 1)] += cs

    return pl.pallas_call(
        body, name=name, grid=(s // tm,), in_specs=[pl.BlockSpec((tm, D), lambda i: (i, 0))] + [yb] * 4 + [_vec(*gg, 1)],
        out_specs=[yb] * 4 + [pl.BlockSpec((1, D), lambda i: (0, 0))],
        out_shape=[SDS((s, GW), F32)] * 4 + [SDS((1, D), F32)], compiler_params=_cp(("arbitrary",)))(dyn, *ys, gg[0])


def _lane_put(acc, col, h):
    lane = lax.broadcasted_iota(jnp.int32, acc.shape, 1)
    return jnp.where(lane == h, col, acc)


def _fgate(z, bf, name):
    s = z.shape[0]

    def body(z_ref, b_ref, fc_ref, fr_ref):
        xg = z_ref[...] + b_ref[...]
        lf = jnp.minimum(xg, 0.0) - jnp.log(1.0 + jnp.exp(-jnp.abs(xg)))
        lane = lax.broadcasted_iota(jnp.int32, lf.shape, 1)
        row = lax.broadcasted_iota(jnp.int32, lf.shape, 0)
        f = jnp.where(lane < 4, lf, 0.0)
        sh = 1
        while sh < s:
            f = f + jnp.where(row >= sh, pltpu.roll(f, sh, 0), 0.0)
            sh *= 2
        fc_ref[...] = f
        fr_ref[...] = f.T[:8, :]

    return pl.pallas_call(
        body, name=name, grid=(1,),
        in_specs=[pl.BlockSpec((s, 128), lambda i: (0, Z_FG)), pl.BlockSpec((1, 128), lambda i: (0, 0))],
        out_specs=[pl.BlockSpec((s, 128), lambda i: (0, 0)), pl.BlockSpec((8, s), lambda i: (0, 0))],
        out_shape=[SDS((s, 128), F32), SDS((8, s), F32)], compiler_params=_cp(("arbitrary",)))(z, bf)


def _fgate_bwd(z, bf, dfrow, dfcol, name):
    s = z.shape[0]

    def body(z_ref, b_ref, d_ref, dc_ref, dz_ref, db_ref):
        d = jnp.concatenate([d_ref[...], jnp.zeros((120, s), F32)], axis=0).T + dc_ref[...]
        row = lax.broadcasted_iota(jnp.int32, d.shape, 0)
        lane = lax.broadcasted_iota(jnp.int32, d.shape, 1)
        sh = 1
        while sh < s:
            d = d + jnp.where(row < s - sh, pltpu.roll(d, s - sh, 0), 0.0)
            sh *= 2
        xg = z_ref[...] + b_ref[...]
        dz = jnp.where(lane < 4, d * _sigmoid(-xg), 0.0)
        dz_ref[...] = dz.astype(BF16)
        db_ref[...] = _colsum(dz)

    return pl.pallas_call(
        body, name=name, grid=(1,),
        in_specs=[pl.BlockSpec((s, 128), lambda i: (0, Z_FG)), pl.BlockSpec((1, 128), lambda i: (0, 0)),
                  pl.BlockSpec((8, s), lambda i: (0, 0)), pl.BlockSpec((s, 128), lambda i: (0, 0))],
        out_specs=[pl.BlockSpec((s, 128), lambda i: (0, 0)), pl.BlockSpec((1, 128), lambda i: (0, 0))],
        out_shape=[SDS((s, 128), BF16), SDS((1, 128), F32)], compiler_params=_cp(("arbitrary",)))(z, bf, dfrow, dfcol)


def _fox_scores(q_ref, k_ref, fc_ref, fr_ref, h, n, tq):
    kw = (n + 1) * tq
    q = q_ref[:, HD * h:HD * (h + 1)] * SCALE
    sc = _dot(q, k_ref[0:kw, HD * h:HD * (h + 1)], "nt") + fc_ref[:, h:h + 1] - fr_ref[h:h + 1, 0:kw]
    qpos = n * tq + lax.broadcasted_iota(jnp.int32, (tq, kw), 0)
    kpos = lax.broadcasted_iota(jnp.int32, (tq, kw), 1)
    return q, jnp.where(kpos <= qpos, sc, NEG)


def _fox_fwd(z, fcol, frow, name):
    s = z.shape[0]
    tq = min(TQ, s)
    nc = s // tq

    def body(q_ref, k_ref, v_ref, fc_ref, fr_ref, y_ref, l_ref):
        for n in range(nc):
            @pl.when(pl.program_id(0) == n)
            def _():
                kw = (n + 1) * tq
                lse = jnp.zeros((tq, 128), F32)
                for h in range(4):
                    _, sc = _fox_scores(q_ref, k_ref, fc_ref, fr_ref, h, n, tq)
                    m = jnp.max(sc, axis=-1, keepdims=True)
                    p = jnp.exp(sc - m)
                    l = jnp.sum(p, axis=-1, keepdims=True)
                    y_ref[:, HD * h:HD * (h + 1)] = _dot(p, v_ref[0:kw, HD * h:HD * (h + 1)], "nn") / l
                    lse = _lane_put(lse, m + jnp.log(l), h)
                l_ref[...] = lse

    return pl.pallas_call(
        body, name=name, grid=(nc,),
        in_specs=[pl.BlockSpec((tq, GW), lambda i: (i, Z_FQ)), pl.BlockSpec((s, GW), lambda i: (0, Z_FK)),
                  pl.BlockSpec((s, GW), lambda i: (0, Z_FV)), pl.BlockSpec((tq, 128), lambda i: (i, 0)),
                  pl.BlockSpec((8, s), lambda i: (0, 0))],
        out_specs=[pl.BlockSpec((tq, GW), lambda i: (i, 0)), pl.BlockSpec((tq, 128), lambda i: (i, 0))],
        out_shape=[SDS((s, GW), F32), SDS((s, 128), F32)], compiler_params=_cp(("parallel",)))(z, z, z, fcol, frow)


def _fox_bwd(z, fcol, frow, lse, y, dy, name):
    s = z.shape[0]
    tq = min(TQ, s)
    nc = s // tq

    def body(q_ref, k_ref, v_ref, fc_ref, fr_ref, l_ref, y_ref, dy_ref, dq_ref, dk_ref, dv_ref, df_ref, dfq_ref):
        @pl.when(pl.program_id(0) == 0)
        def _():
            dk_ref[...] = jnp.zeros_like(dk_ref)
            dv_ref[...] = jnp.zeros_like(dv_ref)
            df_ref[...] = jnp.zeros_like(df_ref)

        for n in range(nc):
            @pl.when(pl.program_id(0) == n)
            def _():
                kw = (n + 1) * tq
                dfq = jnp.zeros((tq, 128), F32)
                for h in range(4):
                    hs = slice(HD * h, HD * (h + 1))
                    q, sc = _fox_scores(q_ref, k_ref, fc_ref, fr_ref, h, n, tq)
                    p = jnp.exp(sc - l_ref[:, h:h + 1])
                    dyh = dy_ref[:, hs]
                    dd = jnp.sum(dyh * y_ref[:, hs], axis=-1, keepdims=True)
                    ds = p * (_dot(dyh, v_ref[0:kw, hs], "nt") - dd)
                    dq_ref[:, hs] = _dot(ds, k_ref[0:kw, hs], "nn") * SCALE
                    dk_ref[0:kw, hs] += _dot(ds, q, "tn")
                    dv_ref[0:kw, hs] += _dot(p, dyh, "tn")
                    df_ref[h:h + 1, 0:kw] -= _colsum(ds)
                    dfq = _lane_put(dfq, jnp.sum(ds, axis=-1, keepdims=True), h)
                dfq_ref[...] = dfq

    tile = lambda w: pl.BlockSpec((tq, w), lambda i: (i, 0))
    full = pl.BlockSpec((s, GW), lambda i: (0, 0))
    rows8 = pl.BlockSpec((8, s), lambda i: (0, 0))
    return pl.pallas_call(
        body, name=name, grid=(nc,),
        in_specs=[pl.BlockSpec((tq, GW), lambda i: (i, Z_FQ)), pl.BlockSpec((s, GW), lambda i: (0, Z_FK)),
                  pl.BlockSpec((s, GW), lambda i: (0, Z_FV)), tile(128), rows8, tile(128), tile(GW), tile(GW)],
        out_specs=[tile(GW), full, full, rows8, tile(128)],
        out_shape=[SDS((s, GW), F32), SDS((s, GW), F32), SDS((s, GW), F32), SDS((8, s), F32), SDS((s, 128), F32)],
        compiler_params=_cp(("arbitrary",)))(z, z, z, fcol, frow, lse, y, dy)


def _swa_block(q_ref, k_ref, v_ref, n):
    qs = pl.multiple_of(n * WIN, WIN)
    ks = pl.multiple_of(jnp.maximum(n - 1, 0) * WIN, WIN)
    qb = q_ref[pl.ds(qs, WIN), :]
    kb = k_ref[pl.ds(ks, 2 * WIN), :]
    vb = v_ref[pl.ds(ks, 2 * WIN), :]
    dist = (qs + lax.broadcasted_iota(jnp.int32, (WIN, 2 * WIN), 0)) - (ks + lax.broadcasted_iota(jnp.int32, (WIN, 2 * WIN), 1))
    return qs, ks, qb, kb, vb, (dist >= 0) & (dist < WIN)


def _swa_fwd(z, sinks, name):
    s = z.shape[0]

    def body(sink_ref, q_ref, k_ref, v_ref, y_ref, l_ref):
        def step(n, carry):
            qs, ks, qb, kb, vb, valid = _swa_block(q_ref, k_ref, v_ref, n)
            lse = jnp.zeros((WIN, 128), F32)
            for h in range(4):
                kv = slice(HD * (h // 2), HD * (h // 2 + 1))
                sc = jnp.where(valid, _dot(qb[:, HD * h:HD * (h + 1)] * SCALE, kb[:, kv], "nt"), NEG)
                sink = sink_ref[h]
                m = jnp.maximum(jnp.max(sc, axis=-1, keepdims=True), sink)
                p = jnp.exp(sc - m)
                den = jnp.sum(p, axis=-1, keepdims=True) + jnp.exp(sink - m)
                y_ref[pl.ds(qs, WIN), HD * h:HD * (h + 1)] = _dot(p, vb[:, kv], "nn") / den
                lse = _lane_put(lse, m + jnp.log(den), h)
            l_ref[pl.ds(qs, WIN), :] = lse
            return carry

        lax.fori_loop(0, s // WIN, step, 0)

    return pl.pallas_call(
        body, name=name, grid=(1,),
        in_specs=[pl.BlockSpec(memory_space=pltpu.SMEM), pl.BlockSpec((s, GW), lambda i: (0, Z_SQ)),
                  pl.BlockSpec((s, 128), lambda i: (0, Z_SK)), pl.BlockSpec((s, 128), lambda i: (0, Z_SV))],
        out_specs=[pl.BlockSpec((s, GW), lambda i: (0, 0)), pl.BlockSpec((s, 128), lambda i: (0, 0))],
        out_shape=[SDS((s, GW), F32), SDS((s, 128), F32)], compiler_params=_cp(("arbitrary",)))(sinks, z, z, z)


def _swa_bwd(z, sinks, lse, y, dy, name):
    s = z.shape[0]

    def body(sink_ref, q_ref, k_ref, v_ref, l_ref, y_ref, dy_ref, dq_ref, dk_ref, dv_ref, dsink_ref):
        dk_ref[...] = jnp.zeros_like(dk_ref)
        dv_ref[...] = jnp.zeros_like(dv_ref)
        dsink_ref[...] = jnp.zeros_like(dsink_ref)

        def step(n, carry):
            qs, ks, qb, kb, vb, valid = _swa_block(q_ref, k_ref, v_ref, n)
            lse_b = l_ref[pl.ds(qs, WIN), :]
            yb = y_ref[pl.ds(qs, WIN), :]
            dyb = dy_ref[pl.ds(qs, WIN), :]
            dsink = jnp.zeros((1, 128), F32)
            for h in range(4):
                hs = slice(HD * h, HD * (h + 1))
                kv = slice(HD * (h // 2), HD * (h // 2 + 1))
                q = qb[:, hs] * SCALE
                sc = jnp.where(valid, _dot(q, kb[:, kv], "nt"), NEG)
                lh = lse_b[:, h:h + 1]
                p = jnp.exp(sc - lh)
                dd = jnp.sum(dyb[:, hs] * yb[:, hs], axis=-1, keepdims=True)
                ds = p * (_dot(dyb[:, hs], vb[:, kv], "nt") - dd)
                dq_ref[pl.ds(qs, WIN), hs] = _dot(ds, kb[:, kv], "nn") * SCALE
                dk_ref[pl.ds(ks, 2 * WIN), kv] += _dot(ds, q, "tn")
                dv_ref[pl.ds(ks, 2 * WIN), kv] += _dot(p, dyb[:, hs], "tn")
                dsink = _lane_put(dsink, dsink[:, h:h + 1] - jnp.sum(jnp.exp(sink_ref[h] - lh) * dd, axis=0, keepdims=True), h)
            dsink_ref[...] += dsink
            return carry

        lax.fori_loop(0, s // WIN, step, 0)

    full = lambda w: pl.BlockSpec((s, w), lambda i: (0, 0))
    return pl.pallas_call(
        body, name=name, grid=(1,),
        in_specs=[pl.BlockSpec(memory_space=pltpu.SMEM), pl.BlockSpec((s, GW), lambda i: (0, Z_SQ)),
                  pl.BlockSpec((s, 128), lambda i: (0, Z_SK)), pl.BlockSpec((s, 128), lambda i: (0, Z_SV)),
                  full(128), full(GW), full(GW)],
        out_specs=[full(GW), full(128), full(128), pl.BlockSpec((1, 128), lambda i: (0, 0))],
        out_shape=[SDS((s, GW), F32), SDS((s, 128), F32), SDS((s, 128), F32), SDS((1, 128), F32)],
        compiler_params=_cp(("arbitrary",)))(sinks, z, z, z, lse, y, dy)


def _delayed(win, shift, halo):
    return win[halo:, :] if shift == 0 else pltpu.roll(win, shift, 0)[halo:, :]


def _prev_halo(width, halo, tm, col):
    return pl.BlockSpec((halo, width), lambda i: (jnp.maximum(i * (tm // halo) - 1, 0), col))


def _glu_window(a_ref, g_ref, ah_ref, gh_ref):
    keep = (pl.program_id(0) > 0).astype(F32)
    a = jnp.concatenate([ah_ref[...] * keep, a_ref[...]], axis=0)
    g = jnp.concatenate([gh_ref[...], g_ref[...]], axis=0)
    return a * _sigmoid(g)


def _conv_fwd(z, cw, cb, lg, lb, pw, pb, name):
    s = z.shape[0]
    tm = min(TM, s)

    def body(a_ref, g_ref, ah_ref, gh_ref, w_ref, b_ref, lg_ref, lb_ref, pw_ref, pb_ref, y_ref, hc_ref):
        hg = _glu_window(a_ref, g_ref, ah_ref, gh_ref)
        hc = jnp.zeros((tm, GW), F32) + b_ref[...]
        for k in range(CONV_K):
            hc = hc + w_ref[k:k + 1, :] * _delayed(hg, CONV_K - 1 - k, CONV_HALO)
        hc_ref[...] = hc
        xh, _ = _ln_stats(hc)
        y_ref[...] = _dot(_silu(xh * lg_ref[...] + lb_ref[...]), pw_ref[...], "nn") + pb_ref[...]

    tile = lambda col: pl.BlockSpec((tm, GW), lambda i: (i, col))
    whole = lambda a: pl.BlockSpec(a.shape, lambda i: (0, 0))
    return pl.pallas_call(
        body, name=name, grid=(s // tm,),
        in_specs=[tile(Z_CA), tile(Z_CG), _prev_halo(GW, CONV_HALO, tm, Z_CA), _prev_halo(GW, CONV_HALO, tm, Z_CG),
                  whole(cw), whole(cb), whole(lg), whole(lb), whole(pw), whole(pb)],
        out_specs=[tile(0), tile(0)], out_shape=[SDS((s, GW), F32), SDS((s, GW), F32)],
        compiler_params=_cp(("parallel",)))(z, z, z, z, cw, cb, lg, lb, pw, pb)


def _conv_bwd_a(z, hc, dy, cw, lg, lb, pw, name):
    s = z.shape[0]
    tm = min(TM, s)

    def body(a_ref, g_ref, ah_ref, gh_ref, hc_ref, dy_ref, lg_ref, lb_ref, pw_ref,
             dhc_ref, dpw_ref, dpb_ref, dlg_ref, dlb_ref, dcw_ref, dcb_ref):
        first = pl.program_id(0) == 0
        dy = dy_ref[...]
        xh, rstd = _ln_stats(hc_ref[...])
        hn = xh * lg_ref[...] + lb_ref[...]
        dhn = _dot(dy, pw_ref[...], "nt") * _dsilu(hn)
        dhc = _ln_bwd(xh, rstd, dhn * lg_ref[...])
        dhc_ref[...] = dhc
        _acc(dpw_ref, _dot(_silu(hn), dy, "tn"), first)
        _acc(dpb_ref, _colsum(dy), first)
        _acc(dlg_ref, _colsum(dhn * xh), first)
        _acc(dlb_ref, _colsum(dhn), first)
        _acc(dcb_ref, _colsum(dhc), first)
        hg = _glu_window(a_ref, g_ref, ah_ref, gh_ref)

        @pl.when(first)
        def _():
            dcw_ref[...] = jnp.zeros_like(dcw_ref)

        for k in range(CONV_K):
            dcw_ref[k:k + 1, :] += _colsum(dhc * _delayed(hg, CONV_K - 1 - k, CONV_HALO))

    tile = lambda col: pl.BlockSpec((tm, GW), lambda i: (i, col))
    whole = lambda shape: pl.BlockSpec(shape, lambda i: (0, 0))
    return pl.pallas_call(
        body, name=name, grid=(s // tm,),
        in_specs=[tile(Z_CA), tile(Z_CG), _prev_halo(GW, CONV_HALO, tm, Z_CA), _prev_halo(GW, CONV_HALO, tm, Z_CG),
                  tile(0), tile(0), whole(lg.shape), whole(lb.shape), whole(pw.shape)],
        out_specs=[tile(0), whole((GW, GW)), whole((1, GW)), whole((1, GW)), whole((1, GW)), whole((32, GW)), whole((1, GW))],
        out_shape=[SDS((s, GW), F32), SDS((GW, GW), F32), SDS((1, GW), F32), SDS((1, GW), F32), SDS((1, GW), F32),
                   SDS((32, GW), F32), SDS((1, GW), F32)],
        compiler_params=_cp(("arbitrary",)))(z, z, z, z, hc, dy, lg, lb, pw)


def _conv_bwd_b(z, dhc, cw, name):
    s = z.shape[0]
    tm = min(TM, s)
    nt = s // tm

    def body(a_ref, g_ref, d_ref, dn_ref, w_ref, da_ref, dg_ref):
        keep = (pl.program_id(0) < nt - 1).astype(F32)
        win = jnp.concatenate([d_ref[...], dn_ref[...] * keep], axis=0)
        dhg = jnp.zeros((tm, GW), F32)
        for k in range(CONV_K):
            sh = CONV_K - 1 - k
            dhg = dhg + w_ref[k:k + 1, :] * (win[:tm, :] if sh == 0 else pltpu.roll(win, tm + CONV_HALO - sh, 0)[:tm, :])
        sg = _sigmoid(g_ref[...])
        da_ref[...] = (dhg * sg).astype(BF16)
        dg_ref[...] = (dhg * a_ref[...] * sg * (1.0 - sg)).astype(BF16)

    tile = lambda col: pl.BlockSpec((tm, GW), lambda i: (i, col))
    nxt = pl.BlockSpec((CONV_HALO, GW), lambda i: (jnp.minimum((i + 1) * (tm // CONV_HALO), s // CONV_HALO - 1), 0))
    return pl.pallas_call(
        body, name=name, grid=(nt,),
        in_specs=[tile(Z_CA), tile(Z_CG), tile(0), nxt, pl.BlockSpec(cw.shape, lambda i: (0, 0))],
        out_specs=[tile(0), tile(0)], out_shape=[SDS((s, GW), BF16), SDS((s, GW), BF16)],
        compiler_params=_cp(("parallel",)))(z, z, dhc, dhc, cw)


def _sgu_chunk(zu, zv, lg, lb, wcat, bfull):
    u, v = _gelu(zu), _gelu(zv)
    xh, rstd = _ln_stats(v)
    vn = xh * lg + lb
    lane = lax.shift_right_logical(lax.broadcasted_iota(jnp.int32, (WIN, GW), 1), 6)
    r = jnp.concatenate([jnp.where(lane == g, vn, 0.0) for g in range(4)], axis=0)
    mix = _dot(wcat, r, "nn") + bfull
    return u, xh, rstd, r, mix, lane


def _tril4(w):
    t = lax.broadcasted_iota(jnp.int32, w.shape, 0)
    sidx = lax.broadcasted_iota(jnp.int32, w.shape, 1) & (WIN - 1)
    return jnp.where(sidx <= t, w, 0.0)


def _sgu_fwd(z, lg, lb, wcat, bfull, name):
    s = z.shape[0]
    tm = min(TM, s)

    def body(u_ref, v_ref, lg_ref, lb_ref, w_ref, b_ref, y_ref):
        w = _tril4(w_ref[...])
        for n in range(tm // WIN):
            rows = slice(WIN * n, WIN * (n + 1))
            u, _, _, _, mix, _ = _sgu_chunk(u_ref[rows, :], v_ref[rows, :], lg_ref[...], lb_ref[...], w, b_ref[...])
            y_ref[rows, :] = u * mix

    tile = lambda col: pl.BlockSpec((tm, GW), lambda i: (i, col))
    whole = lambda a: pl.BlockSpec(a.shape, lambda i: (0, 0))
    return pl.pallas_call(
        body, name=name, grid=(s // tm,), in_specs=[tile(Z_GU), tile(Z_GV), whole(lg), whole(lb), whole(wcat), whole(bfull)],
        out_specs=tile(0), out_shape=SDS((s, GW), F32), compiler_params=_cp(("parallel",)))(z, z, lg, lb, wcat, bfull)


def _sgu_bwd(z, dy, lg, lb, wcat, bfull, name):
    s = z.shape[0]
    tm = min(TM, s)

    def body(u_ref, v_ref, dy_ref, lg_ref, lb_ref, w_ref, b_ref, du_ref, dv_ref, dw_ref, db_ref, dlg_ref, dlb_ref):
        first = pl.program_id(0) == 0
        w = _tril4(w_ref[...])
        wt = w.T
        dw = jnp.zeros((WIN, 4 * WIN), F32)
        db = jnp.zeros((WIN, 128), F32)
        dlg = jnp.zeros((1, GW), F32)
        dlb = jnp.zeros((1, GW), F32)
        for n in range(tm // WIN):
            rows = slice(WIN * n, WIN * (n + 1))
            zu, zv, dout = u_ref[rows, :], v_ref[rows, :], dy_ref[rows, :]
            u, xh, rstd, r, mix, lane = _sgu_chunk(zu, zv, lg_ref[...], lb_ref[...], w, b_ref[...])
            dmix = dout * u
            du_ref[rows, :] = (dout * mix * _dgelu(zu)).astype(BF16)
            dw = dw + _dot(dmix, r, "nt")
            for g in range(4):
                db = _lane_put(db, db[:, g:g + 1] + jnp.sum(dmix[:, HD * g:HD * (g + 1)], axis=1, keepdims=True), g)
            dr = _dot(wt, dmix, "nn")
            dvn = jnp.zeros((WIN, GW), F32)
            for g in range(4):
                dvn = dvn + jnp.where(lane == g, dr[WIN * g:WIN * (g + 1), :], 0.0)
            dlg = dlg + _colsum(dvn * xh)
            dlb = dlb + _colsum(dvn)
            dv_ref[rows, :] = (_ln_bwd(xh, rstd, dvn * lg_ref[...]) * _dgelu(zv)).astype(BF16)
        _acc(dw_ref, _tril4(dw), first)
        _acc(db_ref, db, first)
        _acc(dlg_ref, dlg, first)
        _acc(dlb_ref, dlb, first)

    tile = lambda col: pl.BlockSpec((tm, GW), lambda i: (i, col))
    whole = lambda shape: pl.BlockSpec(shape, lambda i: (0, 0))
    return pl.pallas_call(
        body, name=name, grid=(s // tm,),
        in_specs=[tile(Z_GU), tile(Z_GV), tile(0), whole(lg.shape), whole(lb.shape), whole(wcat.shape), whole(bfull.shape)],
        out_specs=[tile(0), tile(0), whole((WIN, 4 * WIN)), whole((WIN, 128)), whole((1, GW)), whole((1, GW))],
        out_shape=[SDS((s, GW), BF16), SDS((s, GW), BF16), SDS((WIN, 4 * WIN), F32), SDS((WIN, 128), F32),
                   SDS((1, GW), F32), SDS((1, GW), F32)],
        compiler_params=_cp(("arbitrary",)))(z, z, dy, lg, lb, wcat, bfull)


def _conv3(win, w, b):
    return (w[2:3, :] * win[FFN_HALO:, :] + w[1:2, :] * pltpu.roll(win, 1, 0)[FFN_HALO:, :]
            + w[0:1, :] * pltpu.roll(win, 2, 0)[FFN_HALO:, :] + b)


def _ffn_specs(s, tm):
    main = pl.BlockSpec((2, None, tm, FF_BLK), lambda j, i: (0, j, i, 0))
    prev = pl.BlockSpec((2, None, FFN_HALO, FF_BLK), lambda j, i: (0, j, jnp.maximum(i * (tm // FFN_HALO) - 1, 0), 0))
    nxt = pl.BlockSpec((2, None, FFN_HALO, FF_BLK),
                       lambda j, i: (0, j, jnp.minimum((i + 1) * (tm // FFN_HALO), s // FFN_HALO - 1), 0))
    wsp = pl.BlockSpec((2, None, 3, FF_BLK), lambda j, i: (0, j, 0, 0))
    bsp = pl.BlockSpec((2, None, 1, FF_BLK), lambda j, i: (0, j, 0, 0))
    return main, prev, nxt, wsp, bsp


def _ffn_act(u4, w4, b4, name):
    s = u4.shape[2]
    tm = min(TM, s)
    main, prev, _, wsp, bsp = _ffn_specs(s, tm)

    def body(u_ref, uh_ref, w_ref, b_ref, o_ref):
        keep = (pl.program_id(1) > 0).astype(F32)
        gw, vw = [jnp.concatenate([uh_ref[p].astype(F32) * keep, u_ref[p].astype(F32)], axis=0) for p in range(2)]
        o_ref[...] = (_silu(_conv3(gw, w_ref[0], b_ref[0])) * _conv3(vw, w_ref[1], b_ref[1])).astype(BF16)

    return pl.pallas_call(
        body, name=name, grid=(FF_NBLK, s // tm), in_specs=[main, prev, wsp, bsp],
        out_specs=pl.BlockSpec((None, tm, FF_BLK), lambda j, i: (j, i, 0)), out_shape=SDS((FF_NBLK, s, FF_BLK), BF16),
        compiler_params=_cp(("parallel", "parallel")))(u4, u4, w4, b4)


def _ffn_bwd(u4, dact, w4, b4, name):
    s = u4.shape[2]
    tm = min(TM, s)
    nt = s // tm
    main, prev, nxt, wsp, bsp = _ffn_specs(s, tm)
    dmain = pl.BlockSpec((None, tm, FF_BLK), lambda j, i: (j, i, 0))
    dnext = pl.BlockSpec((None, FFN_HALO, FF_BLK), lambda j, i: (j, jnp.minimum((i + 1) * (tm // FFN_HALO), s // FFN_HALO - 1), 0))
    ext = tm + FFN_HALO

    def body(u_ref, up_ref, un_ref, d_ref, dn_ref, w_ref, b_ref, du_ref, dw_ref, db_ref):
        i = pl.program_id(1)
        first = i == 0
        keep_prev = (i > 0).astype(F32)
        keep_next = (i < nt - 1).astype(F32)
        wins = [jnp.concatenate([up_ref[p].astype(F32) * keep_prev, u_ref[p].astype(F32), un_ref[p].astype(F32)], axis=0)
                for p in range(2)]
        gc = _conv3(wins[0], w_ref[0], b_ref[0])
        vc = _conv3(wins[1], w_ref[1], b_ref[1])
        d = jnp.concatenate([d_ref[...].astype(F32), dn_ref[...].astype(F32) * keep_next], axis=0)
        sg = _sigmoid(gc)
        duc = (d * vc * (sg * (1.0 + gc * (1.0 - sg))), d * (gc * sg))
        for p in range(2):
            w = w_ref[p]
            du_ref[p] = (w[2:3, :] * duc[p][:tm, :] + w[1:2, :] * pltpu.roll(duc[p], ext - 1, 0)[:tm, :]
                         + w[0:1, :] * pltpu.roll(duc[p], ext - 2, 0)[:tm, :]).astype(BF16)
            own = duc[p][:tm, :]
            taps = [_colsum(own * (wins[p] if k == 2 else pltpu.roll(wins[p], 2 - k, 0))[FFN_HALO:FFN_HALO + tm, :])
                    for k in range(3)]

            @pl.when(first)
            def _():
                db_ref[p] = _colsum(own)
                for k in range(3):
                    dw_ref[p, k:k + 1, :] = taps[k]

            @pl.when(jnp.logical_not(first))
            def _():
                db_ref[p] += _colsum(own)
                for k in range(3):
                    dw_ref[p, k:k + 1, :] += taps[k]

    return pl.pallas_call(
        body, name=name, grid=(FF_NBLK, nt), in_specs=[main, prev, nxt, dmain, dnext, wsp, bsp], out_specs=[main, wsp, bsp],
        out_shape=[SDS(u4.shape, BF16), SDS((2, FF_NBLK, 3, FF_BLK), F32), SDS((2, FF_NBLK, 1, FF_BLK), F32)],
        compiler_params=_cp(("parallel", "arbitrary")))(u4, u4, u4, dact, dact, w4, b4)


def _sum8(parts, name):
    _, r, c = parts.shape
    tr = r
    for cand in (512, 256, 128, 64, 32, 16):
        if r % cand == 0 and r > cand:
            tr = cand
            break

    def body(p_ref, o_ref):
        acc = p_ref[0].astype(F32)
        for j in range(1, N_DEV):
            acc = acc + p_ref[j].astype(F32)
        o_ref[...] = acc

    return pl.pallas_call(
        body, name=name, grid=(r // tr,), in_specs=[pl.BlockSpec((N_DEV, tr, c), lambda i: (0, i, 0))],
        out_specs=pl.BlockSpec((tr, c), lambda i: (i, 0)), out_shape=SDS((r, c), F32),
        compiler_params=_cp(("parallel",)))(parts)


def _sum8_small(parts, name):
    n = len(parts)

    def body(*refs):
        for p_ref, o_ref in zip(refs[:n], refs[n:]):
            acc = p_ref[0]
            for j in range(1, N_DEV):
                acc = acc + p_ref[j]
            o_ref[...] = acc

    return pl.pallas_call(body, name=name, out_shape=[SDS(p.shape[1:], F32) for p in parts], compiler_params=_cp())(*parts)


def _adamw_math(w, g, m, v):
    m = ADAM_B1 * m + (1.0 - ADAM_B1) * g
    v = ADAM_B2 * v + (1.0 - ADAM_B2) * (g * g)
    m_hat = m / (1.0 - ADAM_B1 ** ADAM_STEP)
    v_hat = v / (1.0 - ADAM_B2 ** ADAM_STEP)
    return -ADAM_LR * (m_hat / (jnp.sqrt(v_hat) + ADAM_EPS) + ADAM_WD * w), m, v


def _adamw(w, g, m, v, name):
    r, c = w.shape
    tr = r
    for cand in (256, 128, 64):
        if r % cand == 0 and r > cand:
            tr = cand
            break

    def body(w_ref, g_ref, m_ref, v_ref, d_ref, mo_ref, vo_ref):
        d_ref[...], mo_ref[...], vo_ref[...] = _adamw_math(w_ref[...], g_ref[...], m_ref[...], v_ref[...])

    blk = pl.BlockSpec((tr, c), lambda i: (i, 0))
    return pl.pallas_call(body, name=name, grid=(r // tr,), in_specs=[blk] * 4, out_specs=[blk] * 3,
                          out_shape=[SDS((r, c), F32)] * 3, compiler_params=_cp(("parallel",)))(w, g, m, v)


def _adamw_small(ws, gs, ms, vs, name):
    n = len(ws)

    def body(*refs):
        ins, outs = refs[:4 * n], refs[4 * n:]
        for i in range(n):
            d, m, v = _adamw_math(ins[i][...], ins[n + i][...], ins[2 * n + i][...], ins[3 * n + i][...])
            outs[i][...], outs[n + i][...], outs[2 * n + i][...] = d, m, v

    shapes = [SDS(w.shape, F32) for w in ws]
    res = pl.pallas_call(body, name=name, out_shape=shapes * 3, compiler_params=_cp())(*ws, *gs, *ms, *vs)
    return res[:n], res[n:2 * n], res[2 * n:]


def _perm_in(w):
    pad = jnp.zeros(w.shape[:-1] + (ZW - 2308,), w.dtype)
    return jnp.concatenate([w[..., :768], w[..., 772:], w[..., 768:772], pad], axis=-1)


def _unperm_in(g):
    return jnp.concatenate([g[..., :768], g[..., 2304:2308], g[..., 768:2304]], axis=-1)


def _wcat(sgu_w):
    return sgu_w.transpose(1, 0, 2).reshape(WIN, 4 * WIN)


def _layer_fwd(l, x, h1, mod, p, wg, last, target, nxt):
    s = x.shape[0]
    tag = f"_l{l}"
    mrow = lambda k: (mod, 6 * l + k)
    z = _mm_rows(h1, wg["w_in"].get(h1), "nn", ZW, F32, "mm_z" + tag)
    fcol, frow = _fgate(z, p["bf"], "fgate" + tag)
    y_fox, lse_fox = _fox_fwd(z, fcol, frow, "fox_fwd" + tag)
    y_conv, hc = _conv_fwd(z, wg["conv_w"], p["conv_b"], p["conv_ln_g"], p["conv_ln_b"], wg["conv_pw_w"], p["conv_pw_b"],
                           "conv_fwd" + tag)
    y_swa, lse_swa = _swa_fwd(z, p["sinks"], "swa_fwd" + tag)
    y_sgu = _sgu_fwd(z, p["sgu_ln_g"], p["sgu_ln_b"], p["wcat"], p["bfull"], "sgu_fwd" + tag)
    ys = (y_fox, y_conv, y_swa, y_sgu)
    yn = _gnorm(ys, (p["g_group"], l), "gnorm" + tag)
    o = _mm_rows(yn, wg["w_out"].get(yn), "nn", D, F32, "mm_o" + tag)
    x1, h2 = _post(x, o, mrow(2), (p["g_post_mix"], l), (p["g_pre_ffn"], l), mrow(4), mrow(3), "post_mix" + tag)
    tm = min(TM, s)
    u = _matmul(h2, wg["w_up"].get(h2), "nn", (N_DEV, s, FF_BLK), BF16, (N_DEV, 1, 1),
                _bs((s, D), lambda j, i, k: (0, 0)), _bs((None, D, FF_BLK), lambda j, i, k: (j, 0, 0)),
                _bs((None, s, FF_BLK), lambda j, i, k: (j, 0, 0)), None, "mm_u" + tag)
    u4 = u.reshape(2, FF_NBLK, s, FF_BLK)
    act = _ffn_act(u4, wg["ffn_conv_w"], p["ffn_conv_b"], "ffn_act" + tag)
    f = _matmul(act, wg["w_down"].get(act), "nn", (s, D), F32, (1, 1, FF_NBLK),
                _bs((None, s, FF_BLK), lambda i, j, k: (k, 0, 0)), _bs((FF_BLK, D), lambda i, j, k: (k, 0)),
                _bs((s, D), lambda i, j, k: (0, 0)), (s, D), "mm_f" + tag)
    if last:
        out = _post_loss(x1, f, mrow(5), (p["g_post_ffn"], l), target, "post_loss")
    else:
        out = _post(x1, f, mrow(5), (p["g_post_ffn"], l), *nxt, "post_ffn" + tag)
    saved = dict(x=x, h1=h1, z=z, fcol=fcol, frow=frow, lse_fox=lse_fox, hc=hc, lse_swa=lse_swa, ys=ys, yn=yn, o=o, x1=x1,
                 h2=h2, u4=u4, act=act, f=f)
    return out, saved


def _tie(a, token):
    return a if token is None else a + token[0, 0]


def _layer_bwd(l, dx2, sv, mod, p, wg, emit):
    s = dx2.shape[0]
    tm = min(TM, s)
    tag = f"_l{l}"
    mrow = lambda k: (mod, 6 * l + k)
    g = {}
    df, g["ga2"], g["g_post_ffn"] = _post_bwd(dx2, sv["f"], mrow(5), (p["g_post_ffn"], l), "post_ffn_bwd" + tag)
    dact = _matmul(df, wg["w_down"].get(None), "nt", (FF_NBLK, s, FF_BLK), BF16, (FF_NBLK, 1, 1),
                   _bs((s, D), lambda j, i, k: (0, 0)), _bs((FF_BLK, D), lambda j, i, k: (j, 0)),
                   _bs((None, s, FF_BLK), lambda j, i, k: (j, 0, 0)), None, "mm_dact" + tag)
    tok = emit("w_down", _matmul(sv["act"], df, "tn", (FF_NBLK * FF_BLK, D), BF16, (FF_NBLK, 1, 1),
                                 _bs((None, s, FF_BLK), lambda j, i, k: (j, 0, 0)), _bs((s, D), lambda j, i, k: (0, 0)),
                                 _bs((FF_BLK, D), lambda j, i, k: (j, 0)), None, "mm_dwdown" + tag))
    du, g["ffn_conv_w"], g["ffn_conv_b"] = _ffn_bwd(sv["u4"], dact, wg["ffn_conv_w"], _tie(p["ffn_conv_b"], tok),
                                                    "ffn_bwd" + tag)
    du = du.reshape(N_DEV, s, FF_BLK)
    dh2 = _matmul(du, wg["w_up"].get(None), "nt", (s, D), F32, (1, 1, N_DEV),
                  _bs((None, s, FF_BLK), lambda i, j, k: (k, 0, 0)), _bs((None, D, FF_BLK), lambda i, j, k: (k, 0, 0)),
                  _bs((s, D), lambda i, j, k: (0, 0)), (s, D), "mm_dh2" + tag)
    tok = emit("w_up", _matmul(du, sv["h2"], "tn", (N_DEV, FF_BLK, D), BF16, (N_DEV, 1, 1),
                               _bs((None, s, FF_BLK), lambda j, i, k: (j, 0, 0)), _bs((s, D), lambda j, i, k: (0, 0)),
                               _bs((None, FF_BLK, D), lambda j, i, k: (j, 0, 0)), None, "mm_dwup" + tag))
    dx1, g["sh2"], g["sc2"], g["g_pre_ffn"] = _pre_bwd(dh2, sv["x1"], dx2, (_tie(p["g_pre_ffn"], tok), l), mrow(4),
                                                       "pre_ffn_bwd" + tag)
    do, g["ga1"], g["g_post_mix"] = _post_bwd(dx1, sv["o"], mrow(2), (p["g_post_mix"], l), "post_mix_bwd" + tag)
    dyn = _mm_rows(do, wg["w_out"].get(None), "nt", D, F32, "mm_dyn" + tag)
    tok = emit("w_out", _mm_wgrad(sv["yn"], do, BF16, "mm_dwout" + tag))
    dy_fox, dy_conv, dy_swa, dy_sgu, g["g_group"] = _gnorm_bwd(dyn, sv["ys"], (_tie(p["g_group"], tok), l), "gnorm_bwd" + tag)
    z = sv["z"]
    dq_f, dk_f, dv_f, dfrow, dfcol = _fox_bwd(z, sv["fcol"], sv["frow"], sv["lse_fox"], sv["ys"][0], dy_fox, "fox_bwd" + tag)
    dgate, g["bf"] = _fgate_bwd(z, p["bf"], dfrow, dfcol, "fgate_bwd" + tag)
    dhc, g["conv_pw_w"], g["conv_pw_b"], g["conv_ln_g"], g["conv_ln_b"], g["conv_w"], g["conv_b"] = _conv_bwd_a(
        z, sv["hc"], dy_conv, wg["conv_w"], p["conv_ln_g"], p["conv_ln_b"], wg["conv_pw_w"], "conv_bwd_a" + tag)
    da_c, dg_c = _conv_bwd_b(z, dhc, wg["conv_w"], "conv_bwd_b" + tag)
    dq_s, dk_s, dv_s, g["sinks"] = _swa_bwd(z, p["sinks"], sv["lse_swa"], sv["ys"][2], dy_swa, "swa_bwd" + tag)
    du_g, dv_g, g["wcat"], g["sgu_bcol"], g["sgu_ln_g"], g["sgu_ln_b"] = _sgu_bwd(
        z, dy_sgu, p["sgu_ln_g"], p["sgu_ln_b"], p["wcat"], p["bfull"], "sgu_bwd" + tag)
    dz = jnp.concatenate([dq_f.astype(BF16), dk_f.astype(BF16), dv_f.astype(BF16), da_c, dg_c, dq_s.astype(BF16), dk_s.astype(BF16),
                          dv_s.astype(BF16), du_g, dv_g, dgate], axis=1)
    dh1 = _mm_rows(dz, wg["w_in"].get(None), "nt", D, F32, "mm_dh1" + tag)
    tok = emit("w_in", _mm_wgrad(sv["h1"], dz, BF16, "mm_dwin" + tag))
    dx, g["sh1"], g["sc1"], g["g_pre_mix"] = _pre_bwd(dh1, sv["x"], dx1, (_tie(p["g_pre_mix"], tok), l), mrow(1),
                                                      "pre_mix_bwd" + tag)
    return dx, g


def _layer_params(l, small, conv_w_full, conv_pw_full, ffn_conv_w_full):
    bf = jnp.pad(small["b_fgate"][l][None, :], ((0, 0), (0, 124)))
    p = dict(
        bf=bf, conv_b=small["conv_b"][l][None], conv_ln_g=small["conv_ln_g"][l][None], conv_ln_b=small["conv_ln_b"][l][None],
        conv_pw_b=small["conv_pw_b"][l][None], sinks=small["swa_sinks"][l], sgu_ln_g=small["sgu_ln_g"][l][None],
        sgu_ln_b=small["sgu_ln_b"][l][None], wcat=_wcat(small["sgu_w"][l]),
        bfull=jnp.repeat(small["sgu_b"][l].T, HD, axis=1),
        ffn_conv_b=small["ffn_conv_b"][l].reshape(2, FF_NBLK, 1, FF_BLK),
        g_group=small["g_group"].reshape(N_LAYER, 1, D), g_post_mix=small["g_post_mix"].reshape(N_LAYER, 1, D),
        g_pre_ffn=small["g_pre_ffn"].reshape(N_LAYER, 1, D), g_post_ffn=small["g_post_ffn"].reshape(N_LAYER, 1, D),
        g_pre_mix=small["g_pre_mix"].reshape(N_LAYER, 1, D))
    wsmall = dict(conv_w=conv_w_full[l], conv_pw_w=conv_pw_full[l].astype(BF16),
                  ffn_conv_w=ffn_conv_w_full[l].reshape(3, 2, FF_NBLK, FF_BLK).transpose(1, 2, 0, 3))
    return p, wsmall


def _local_step(x, target, mod, small, wbig, conv_w_full, conv_pw_full, ffn_conv_w_full, emit):
    ps, wgs = [], []
    for l in range(N_LAYER):
        p, wsmall = _layer_params(l, small, conv_w_full, conv_pw_full, ffn_conv_w_full)
        ps.append(p)
        wgs.append({**wbig[l], **wsmall})
    h = _rms_mod(x, (ps[0]["g_pre_mix"], 0), (mod, 1), (mod, 0), "rms_mod_l0")
    saved = []
    for l in range(N_LAYER):
        last = l == N_LAYER - 1
        nxt = None if last else ((ps[l]["g_pre_mix"], l + 1), (mod, 6 * (l + 1) + 1), (mod, 6 * (l + 1)))
        out, sv = _layer_fwd(l, x, h, mod, ps[l], wgs[l], last, target, nxt)
        saved.append(sv)
        if not last:
            x, h = out
    dx, loss = out
    grads = [None] * N_LAYER
    for l in reversed(range(N_LAYER)):
        dx, grads[l] = _layer_bwd(l, dx, saved[l], mod, ps[l], wgs[l], functools.partial(emit, l))
    return loss, dx, grads


_SMALL = ("b_ada", "g_pre_mix", "g_post_mix", "g_pre_ffn", "g_post_ffn", "b_fgate", "conv_b", "conv_ln_g", "conv_ln_b",
          "conv_pw_b", "swa_sinks", "sgu_ln_g", "sgu_ln_b", "sgu_w", "sgu_b", "g_group", "ffn_conv_b")
_WEIGHTS = ("w_ada", "b_ada", "g_pre_mix", "g_post_mix", "g_pre_ffn", "g_post_ffn", "w_in", "b_fgate", "conv_w", "conv_b",
            "conv_ln_g", "conv_ln_b", "conv_pw_w", "conv_pw_b", "swa_sinks", "sgu_ln_g", "sgu_ln_b", "sgu_w", "sgu_b",
            "g_group", "w_out", "ffn_w_up", "ffn_conv_w", "ffn_conv_b", "ffn_w_down")


def _pad_rows(a, mult):
    r = (-a.shape[0]) % mult
    return a if r == 0 else jnp.concatenate([a, jnp.zeros((r,) + a.shape[1:], a.dtype)], axis=0)


def _view2d(a):
    if a.ndim == 2:
        return a
    return a.reshape(-1, a.shape[-1])


def kernel(x, c, w_ada, b_ada, g_pre_mix, g_post_mix, g_pre_ffn, g_post_ffn, w_in, b_fgate, conv_w, conv_b, conv_ln_g, conv_ln_b, conv_pw_w, conv_pw_b, swa_sinks, sgu_ln_g, sgu_ln_b, sgu_w, sgu_b, g_group, w_out, ffn_w_up, ffn_conv_w, ffn_conv_b, ffn_w_down, loss_target, m_w_ada, m_b_ada, m_g_pre_mix, m_g_post_mix, m_g_pre_ffn, m_g_post_ffn, m_w_in, m_b_fgate, m_conv_w, m_conv_b, m_conv_ln_g, m_conv_ln_b, m_conv_pw_w, m_conv_pw_b, m_swa_sinks, m_sgu_ln_g, m_sgu_ln_b, m_sgu_w, m_sgu_b, m_g_group, m_w_out, m_ffn_w_up, m_ffn_conv_w, m_ffn_conv_b, m_ffn_w_down, v_w_ada, v_b_ada, v_g_pre_mix, v_g_post_mix, v_g_pre_ffn, v_g_post_ffn, v_w_in, v_b_fgate, v_conv_w, v_conv_b, v_conv_ln_g, v_conv_ln_b, v_conv_pw_w, v_conv_pw_b, v_swa_sinks, v_sgu_ln_g, v_sgu_ln_b, v_sgu_w, v_sgu_b, v_g_group, v_w_out, v_ffn_w_up, v_ffn_conv_w, v_ffn_conv_b, v_ffn_w_down):
    env = dict(locals())
    w = {n: env[n] for n in _WEIGHTS}
    mom = {n: env["m_" + n] for n in _WEIGHTS}
    var = {n: env["v_" + n] for n in _WEIGHTS}
    me = 4 * lax.axis_index("x") + 2 * lax.axis_index("y") + lax.axis_index("c")
    x2, target = x[0], loss_target[0]

    (c_all,) = _exchange([c], ["bcast"], "gather_c")
    c_all = c_all.reshape(N_DEV, D)
    (m_all,) = _exchange([_ada_fwd(c_all, w_ada)], ["bcast"], "gather_mod")
    m_mine = lax.dynamic_index_in_dim(m_all, me, axis=2, keepdims=False)
    mod, mod_token = _ada_finish(m_mine.transpose(1, 0, 2).reshape(N_LAYER, 6 * D), b_ada)
    mod = mod.reshape(6 * N_LAYER, 1, D)

    shards = [_tie(conv_w, mod_token), conv_pw_w, ffn_conv_w]
    for l in range(N_LAYER):
        shards += [_perm_in(w_in[l]).astype(BF16), w_out[l].astype(BF16), ffn_w_up[l].astype(BF16), ffn_w_down[l].astype(BF16)]
    flight = _xchg_start(shards, ["bcast"] * len(shards), "gather_weights_start")
    mod = _tie(mod, flight.token)
    g_cw, g_pw, g_fcw = _xchg_wait(flight, [0, 1, 2], mod, "gather_small_wait")
    conv_w_full = g_cw.transpose(1, 2, 0, 3).reshape(N_LAYER, CONV_K, GW)
    conv_pw_full = g_pw.transpose(1, 0, 2, 3).reshape(N_LAYER, GW, GW)
    ffn_conv_w_full = g_fcw.transpose(1, 2, 0, 3).reshape(N_LAYER, 3, N_DEV * FF_BLK)

    def lazy(i, shape, name):
        return _Lazy(lambda after: _xchg_wait(flight, [i], after, name)[0].reshape(shape))

    wbig = [dict(w_in=lazy(3 + 4 * l, (D, ZW), f"wait_w_in_l{l}"), w_out=lazy(4 + 4 * l, (D, D), f"wait_w_out_l{l}"),
                 w_up=lazy(5 + 4 * l, (N_DEV, D, FF_BLK), f"wait_w_up_l{l}"),
                 w_down=lazy(6 + 4 * l, (FF_NBLK * FF_BLK, D), f"wait_w_down_l{l}")) for l in range(N_LAYER)]

    grad_flights = []

    def emit(l, key, arr):
        fl = _xchg_start([arr.reshape(N_DEV, -1, arr.shape[-1])], ["a2a"], f"grad_start_{key}_l{l}")
        grad_flights.append(((l, key), fl))
        return fl.token

    small = {n: w[n] for n in _SMALL}
    loss8, dx, grads = _local_step(x2, target, mod, small, wbig, conv_w_full, conv_pw_full, ffn_conv_w_full, emit)
    loss = lax.psum(loss8[0, 0], ("x", "y", "c"))
    grad_x = dx[None]


    st = lambda key: jnp.stack([grads[l][key] for l in range(N_LAYER)])
    d_conv_w = st("conv_w")[:, :CONV_K, :].reshape(N_LAYER, CONV_K, N_DEV, GW // N_DEV).transpose(2, 0, 1, 3)
    d_pw_w = st("conv_pw_w").reshape(N_LAYER, N_DEV, GW // N_DEV, GW).transpose(1, 0, 2, 3)
    d_fcw = st("ffn_conv_w").reshape(N_LAYER, N_DEV, 3, FF_BLK).transpose(1, 0, 2, 3)
    rows_d = _pad_rows(jnp.concatenate(
        [grads[l][k] for l in range(N_LAYER) for k in ("sh1", "sc1", "ga1", "sh2", "sc2", "ga2")]
        + [grads[l][k] for k in ("g_pre_mix", "g_post_mix", "g_pre_ffn", "g_post_ffn", "g_group") for l in range(N_LAYER)],
        axis=0), 8)
    rows_gw = _pad_rows(jnp.concatenate(
        [grads[l][k] for k in ("conv_b", "conv_ln_g", "conv_ln_b", "conv_pw_b", "sgu_ln_g", "sgu_ln_b") for l in range(N_LAYER)],
        axis=0), 8)
    rows_128 = jnp.concatenate([_pad_rows(jnp.concatenate([grads[l]["bf"] for l in range(N_LAYER)]
                                                          + [grads[l]["sinks"] for l in range(N_LAYER)], axis=0), 8)]
                               + [grads[l]["sgu_bcol"] for l in range(N_LAYER)], axis=0)
    rows_w = jnp.concatenate([grads[l]["wcat"] for l in range(N_LAYER)], axis=0)
    rows_fb = st("ffn_conv_b").reshape(N_LAYER * N_DEV, FF_BLK)
    small_flight = _xchg_start([d_conv_w, d_pw_w, d_fcw, rows_d, rows_gw, rows_128, rows_w, rows_fb],
                               ["a2a"] * 3 + ["bcast"] * 5, "small_grads_start")

    big_sum = {}
    for (l, key), fl in grad_flights:
        (parts,) = _xchg_wait(fl, [0], small_flight.token, f"grad_wait_{key}_l{l}")
        big_sum[(l, key)] = _sum8(parts, f"sum_{key}_l{l}")
    swap = lambda a: a.transpose(0, 2, 1)
    gr = {}
    gr["w_in"] = swap(_unperm_in(jnp.stack([big_sum[(l, "w_in")] for l in range(N_LAYER)])))
    gr["w_out"] = jnp.stack([big_sum[(l, "w_out")] for l in range(N_LAYER)])
    gr["ffn_w_up"] = jnp.stack([big_sum[(l, "w_up")] for l in range(N_LAYER)])
    gr["ffn_w_down"] = jnp.stack([big_sum[(l, "w_down")] for l in range(N_LAYER)])
    delta, new_m, new_v = {}, {}, {}

    def adamw_big(n):
        view = swap if n in ("w_in", "ffn_w_up") else (lambda a: a)
        shape = view(w[n]).shape
        d, m2, v2 = _adamw(_view2d(view(w[n])), _view2d(gr[n]), _view2d(view(mom[n])), _view2d(view(var[n])), "adamw_" + n)
        delta[n], new_m[n], new_v[n] = view(d.reshape(shape)), view(m2.reshape(shape)), view(v2.reshape(shape))
        gr[n] = view(gr[n].reshape(shape))

    for n in ("ffn_w_up", "ffn_w_down", "w_out", "w_in"):
        adamw_big(n)

    small_parts = _xchg_wait(small_flight, list(range(8)), new_v["w_in"], "small_grads_wait")
    s_conv_w, s_pw_w, s_fcw, s_d, s_gw, s_128, s_w, s_fb = _sum8_small(
        [p.reshape(N_DEV, -1, p.shape[-1]) for p in small_parts], "sum_small_grads")
    gr["conv_w"] = s_conv_w.reshape(N_LAYER, CONV_K, GW // N_DEV)
    gr["conv_pw_w"] = s_pw_w.reshape(N_LAYER, GW // N_DEV, GW)
    gr["ffn_conv_w"] = s_fcw.reshape(N_LAYER, 3, FF_BLK)
    gr["b_ada"] = s_d[:6 * N_LAYER].reshape(N_LAYER, 6 * D)
    for i, k in enumerate(("g_pre_mix", "g_post_mix", "g_pre_ffn", "g_post_ffn", "g_group")):
        gr[k] = s_d[6 * N_LAYER + 2 * i:6 * N_LAYER + 2 * i + 2]
    for i, k in enumerate(("conv_b", "conv_ln_g", "conv_ln_b", "conv_pw_b", "sgu_ln_g", "sgu_ln_b")):
        gr[k] = s_gw[2 * i:2 * i + 2]
    gr["b_fgate"] = s_128[0:2, :4]
    gr["swa_sinks"] = s_128[2:4, :4]
    gr["sgu_b"] = s_128[8:].reshape(N_LAYER, WIN, 128)[:, :, :4].transpose(0, 2, 1)
    gr["sgu_w"] = s_w.reshape(N_LAYER, WIN, 4, WIN).transpose(0, 2, 1, 3)
    gr["ffn_conv_b"] = s_fb.reshape(N_LAYER, N_DEV * FF_BLK)
    dmod_all = small_parts[3][:, :6 * N_LAYER, :].reshape(N_DEV, N_LAYER, 6 * D)
    ncol = 6 * D // N_DEV
    dmod_cols = lax.dynamic_slice_in_dim(dmod_all, me * ncol, ncol, axis=2).transpose(1, 0, 2)
    gr["w_ada"] = _ada_bwd(c_all, dmod_cols)

    adamw_big("w_ada")
    smalls = [n for n in _WEIGHTS if n not in ("w_ada", "w_in", "w_out", "ffn_w_up", "ffn_w_down")]
    ds, ms, vs = _adamw_small([_view2d(w[n]) for n in smalls], [_view2d(gr[n]) for n in smalls],
                              [_view2d(mom[n]) for n in smalls], [_view2d(var[n]) for n in smalls], "adamw_small")
    for i, n in enumerate(smalls):
        delta[n], new_m[n], new_v[n] = ds[i].reshape(w[n].shape), ms[i].reshape(w[n].shape), vs[i].reshape(w[n].shape)

    return (loss, grad_x, *[gr[n].reshape(w[n].shape) for n in _WEIGHTS], *[delta[n] for n in _WEIGHTS],
            *[new_m[n] for n in _WEIGHTS], *[new_v[n] for n in _WEIGHTS])
```

```python
import functools

import jax
import jax.numpy as jnp
from jax import lax
from jax.experimental import pallas as pl
from jax.experimental.pallas import tpu as pltpu

F32, BF16 = jnp.float32, jnp.bfloat16
SDS = jax.ShapeDtypeStruct
MESH = pl.DeviceIdType.MESH

N_DEV = 8
D = 1024
GW = 256
HD = 64
N_LAYER = 2
ZW = 2432
FF_BLK = 704
FF_NBLK = 4
CONV_K = 31
CONV_HALO = 32
FFN_HALO = 16
EPS = 1e-6
NEG = -1e30
SCALE = HD ** -0.5
VMEM_LIMIT_V7X = 56 * 1024 * 1024
TM = 512
WGRAD_ROWS = 256
TQ = 256
WIN = 128

ADAM_LR, ADAM_B1, ADAM_B2, ADAM_EPS, ADAM_WD, ADAM_STEP = 0.001, 0.9, 0.999, 1e-08, 0.01, 10

Z_FQ, Z_FK, Z_FV, Z_CA, Z_CG, Z_SQ = 0, 1, 2, 3, 4, 5
Z_SK, Z_SV = 12, 13
Z_GU, Z_GV = 7, 8
Z_FG = 18


def _cp(sem=None):
    return pltpu.CompilerParams(dimension_semantics=sem, vmem_limit_bytes=VMEM_LIMIT_V7X)


def _vec(arr3, idx, ngrid):
    w = arr3.shape[-1]
    if ngrid == 1:
        return pl.BlockSpec((None, 1, w), lambda i: (idx, 0, 0))
    return pl.BlockSpec((None, 1, w), lambda i, j: (idx, 0, 0))


def _sigmoid(x):
    return jax.nn.sigmoid(x)


def _silu(x):
    return x * _sigmoid(x)


def _dsilu(x):
    s = _sigmoid(x)
    return s * (1.0 + x * (1.0 - s))


_G0, _G1 = 0.7978845608028654, 0.044715


def _gelu(x):
    return 0.5 * x * (1.0 + jnp.tanh(_G0 * (x + _G1 * x * x * x)))


def _dgelu(x):
    t = jnp.tanh(_G0 * (x + _G1 * x * x * x))
    return 0.5 * (1.0 + t) + 0.5 * x * (1.0 - t * t) * (_G0 * (1.0 + 3.0 * _G1 * x * x))


def _rstd(x):
    return lax.rsqrt(jnp.mean(x * x, axis=-1, keepdims=True) + EPS)


def _rms_bwd(xh, r, t):
    return r * (t - xh * jnp.mean(t * xh, axis=-1, keepdims=True))


def _ln_stats(x):
    mu = jnp.mean(x, axis=-1, keepdims=True)
    xc = x - mu
    rstd = lax.rsqrt(jnp.mean(xc * xc, axis=-1, keepdims=True) + EPS)
    return xc * rstd, rstd


def _ln_bwd(xh, rstd, dxh):
    return rstd * (dxh - jnp.mean(dxh, axis=-1, keepdims=True) - xh * jnp.mean(dxh * xh, axis=-1, keepdims=True))


def _colsum(x):
    return jnp.sum(x, axis=0, keepdims=True)


def _dot(a, b, kind):
    dn = {"nn": (((1,), (0,)), ((), ())), "nt": (((1,), (1,)), ((), ())), "tn": (((0,), (0,)), ((), ()))}[kind]
    return lax.dot_general(a.astype(BF16), b.astype(BF16), dn, preferred_element_type=F32)


def _exchange(arrs, modes, name):
    n = len(arrs)
    outs = [SDS((N_DEV,) + a.shape, a.dtype) if m == "bcast" else SDS(a.shape, a.dtype) for a, m in zip(arrs, modes)]

    def body(*refs):
        ins, dst = refs[:n], refs[n:2 * n]
        send, recv, loc = refs[2 * n:]
        x, y, c = lax.axis_index("x"), lax.axis_index("y"), lax.axis_index("c")
        me = 4 * x + 2 * y + c

        def src(a, j):
            return ins[a] if modes[a] == "bcast" else ins[a].at[j]

        local = [pltpu.make_async_copy(src(a, me), dst[a].at[me], loc.at[a]) for a in range(n)]
        for cp in local:
            cp.start()
        sent, landed = [], []
        for k in (2, 4, 6, 3, 5, 7, 1):
            px = 1 - x if k & 4 else x
            py = 1 - y if k & 2 else y
            pc = 1 - c if k & 1 else c
            peer = 4 * px + 2 * py + pc
            for a in range(n):
                cp = pltpu.make_async_remote_copy(src_ref=src(a, peer), dst_ref=dst[a].at[me], send_sem=send.at[a, k - 1],
                                                  recv_sem=recv.at[a, k - 1], device_id=(px, py, pc), device_id_type=MESH)
                cp.start()
                sent.append(cp)
                landed.append(pltpu.make_async_remote_copy(src_ref=src(a, peer), dst_ref=dst[a].at[peer],
                                                           send_sem=send.at[a, k - 1], recv_sem=recv.at[a, k - 1],
                                                           device_id=(px, py, pc), device_id_type=MESH))
        for cp in landed:
            cp.wait_recv()
        for cp in sent:
            cp.wait_send()
        for cp in local:
            cp.wait()

    hbm = pl.BlockSpec(memory_space=pltpu.HBM)
    return pl.pallas_call(
        body, name=name, out_shape=outs, in_specs=[hbm] * n, out_specs=[hbm] * n,
        scratch_shapes=[pltpu.SemaphoreType.DMA((n, N_DEV - 1)), pltpu.SemaphoreType.DMA((n, N_DEV - 1)),
                        pltpu.SemaphoreType.DMA((n,))],
        compiler_params=pltpu.CompilerParams(has_side_effects=True),
    )(*arrs)


_PEER_ORDER = (2, 4, 6, 3, 5, 7, 1)
_HBM = pl.BlockSpec(memory_space=pltpu.HBM)
_SEM = pl.BlockSpec(memory_space=pltpu.SEMAPHORE)
_EFFECT = pltpu.SideEffectType.DATAFLOW_SIDE_EFFECTING


def _peer(k):
    x, y, c = lax.axis_index("x"), lax.axis_index("y"), lax.axis_index("c")
    px = 1 - x if k & 4 else x
    py = 1 - y if k & 2 else y
    pc = 1 - c if k & 1 else c
    return (px, py, pc), 4 * px + 2 * py + pc


def _my_id():
    return 4 * lax.axis_index("x") + 2 * lax.axis_index("y") + lax.axis_index("c")


def _split_copies(src_ref, land_ref, send, recv, loc, mode):
    me = _my_id()
    pick = (lambda j: src_ref) if mode == "bcast" else (lambda j: src_ref.at[j])
    local = pltpu.make_async_copy(pick(me), land_ref.at[me], loc)
    remote = []
    for k in _PEER_ORDER:
        dev, peer = _peer(k)
        out = pltpu.make_async_remote_copy(src_ref=pick(peer), dst_ref=land_ref.at[me], send_sem=send.at[k - 1],
                                           recv_sem=recv.at[k - 1], device_id=dev, device_id_type=MESH)
        arrive = pltpu.make_async_remote_copy(src_ref=pick(peer), dst_ref=land_ref.at[peer], send_sem=send.at[k - 1],
                                              recv_sem=recv.at[k - 1], device_id=dev, device_id_type=MESH)
        remote.append((out, arrive))
    return local, remote


class _Flight:
    def __init__(self, srcs, lands, sends, recvs, locs, modes, token):
        self.srcs, self.lands, self.sends, self.recvs, self.locs, self.modes, self.token = (
            list(srcs), list(lands), list(sends), list(recvs), list(locs), list(modes), token)


def _xchg_start(arrs, modes, name):
    n = len(arrs)
    lands = [lax.empty((N_DEV,) + a.shape if m == "bcast" else a.shape, a.dtype) for a, m in zip(arrs, modes)]

    def body(*refs):
        srcs, lnds = refs[:n], refs[n:2 * n]
        outs = refs[2 * n:]
        sends, recvs, locs, token = outs[:n], outs[n:2 * n], outs[2 * n:3 * n], outs[5 * n]
        for a in range(n):
            local, remote = _split_copies(srcs[a], lnds[a], sends[a], recvs[a], locs[a], modes[a])
            local.start()
            for out, _ in remote:
                out.start()
        token[...] = jnp.zeros_like(token)

    sem7 = pltpu.SemaphoreType.DMA((N_DEV - 1,))
    res = pl.pallas_call(
        body, name=name,
        out_shape=[sem7] * (2 * n) + [pltpu.SemaphoreType.DMA(())] * n + [pltpu.HBM(a.shape, a.dtype) for a in arrs]
        + [pltpu.HBM(b.shape, b.dtype) for b in lands] + [SDS((8, 128), F32)],
        in_specs=[_HBM] * (2 * n), out_specs=[_SEM] * (3 * n) + [_HBM] * (2 * n) + [pl.BlockSpec(memory_space=pltpu.VMEM)],
        input_output_aliases={i: 3 * n + i for i in range(2 * n)},
        compiler_params=pltpu.CompilerParams(has_side_effects=_EFFECT),
    )(*[pltpu.with_memory_space_constraint(a, pltpu.HBM) for a in arrs],
      *[pltpu.with_memory_space_constraint(b, pltpu.HBM) for b in lands])
    return _Flight(res[3 * n:4 * n], res[4 * n:5 * n], res[:n], res[n:2 * n], res[2 * n:3 * n], modes, res[5 * n])


def _xchg_wait(flight, idx, after, name):
    n = len(idx)
    modes = [flight.modes[i] for i in idx]

    def body(*refs):
        srcs, lnds = refs[:n], refs[n:2 * n]
        sends, recvs, locs = refs[2 * n:3 * n], refs[3 * n:4 * n], refs[4 * n:5 * n]
        for a in range(n):
            local, remote = _split_copies(srcs[a], lnds[a], sends[a], recvs[a], locs[a], modes[a])
            local.wait()
            for _, arrive in remote:
                arrive.wait_send()
                arrive.wait_recv()

    ops = ([flight.srcs[i] for i in idx] + [flight.lands[i] for i in idx] + [flight.sends[i] for i in idx]
           + [flight.recvs[i] for i in idx] + [flight.locs[i] for i in idx])
    res = pl.pallas_call(
        body, name=name, out_shape=[pltpu.HBM(o.shape, o.dtype) for o in ops[:2 * n]],
        in_specs=[_HBM] * (2 * n) + [_SEM] * (3 * n) + [pl.BlockSpec(memory_space=pl.ANY)], out_specs=[_HBM] * (2 * n),
        input_output_aliases={i: i for i in range(2 * n)},
        compiler_params=pltpu.CompilerParams(has_side_effects=_EFFECT),
    )(*ops, after)
    return res[n:]


class _Lazy:
    def __init__(self, fn):
        self.fn, self.val = fn, None

    def get(self, after):
        if self.val is None:
            self.val = self.fn(after)
        return self.val


def _matmul(a, b, kind, out_shape, out_dtype, grid, a_spec, b_spec, o_spec, acc_shape, name):
    nk = grid[2]

    def body(a_ref, b_ref, o_ref, *scratch):
        prod = _dot(a_ref[...], b_ref[...], kind)
        if nk == 1:
            o_ref[...] = prod.astype(out_dtype)
        else:
            acc = scratch[0]
            k = pl.program_id(2)

            @pl.when(k == 0)
            def _():
                acc[...] = prod

            @pl.when(k > 0)
            def _():
                acc[...] += prod

            @pl.when(k == nk - 1)
            def _():
                o_ref[...] = acc[...].astype(out_dtype)

    return pl.pallas_call(
        body, name=name, grid=grid, in_specs=[a_spec, b_spec], out_specs=o_spec, out_shape=SDS(out_shape, out_dtype),
        scratch_shapes=[] if nk == 1 else [pltpu.VMEM(acc_shape, F32)],
        compiler_params=_cp(("parallel", "parallel", "arbitrary")))(a, b)


def _bs(shape, fn):
    return pl.BlockSpec(shape, fn)


def _mm_rows(a, w, kind, n_out, out_dtype, name):
    s, k = a.shape
    tm = min(TM, s)
    return _matmul(a, w, kind, (s, n_out), out_dtype, (s // tm, 1, 1),
                   _bs((tm, k), lambda i, j, kk: (i, 0)), _bs(w.shape, lambda i, j, kk: (0, 0)),
                   _bs((tm, n_out), lambda i, j, kk: (i, 0)), None, name)


def _mm_wgrad(a, dy, out_dtype, name):
    s, k = a.shape
    n = dy.shape[1]
    tko = min(WGRAD_ROWS, k)
    return _matmul(a, dy, "tn", (k, n), out_dtype, (k // tko, 1, 1),
                   _bs((s, tko), lambda i, j, kk: (0, i)), _bs((s, n), lambda i, j, kk: (0, 0)),
                   _bs((tko, n), lambda i, j, kk: (i, 0)), None, name)


def _ada_fwd(c_all, w_ada):
    ncol = w_ada.shape[2]

    def body(c_ref, w_ref, o_ref):
        ca = _silu(c_ref[...])
        ca = jnp.concatenate([ca, jnp.zeros_like(ca)], axis=0)
        o_ref[...] = _dot(ca, w_ref[...], "nn")[:N_DEV, :]

    return pl.pallas_call(
        body, name="ada_fwd", grid=(N_LAYER,),
        in_specs=[pl.BlockSpec((N_DEV, D), lambda l: (0, 0)), pl.BlockSpec((None, D, ncol), lambda l: (l, 0, 0))],
        out_specs=pl.BlockSpec((None, N_DEV, ncol), lambda l: (l, 0, 0)),
        out_shape=SDS((N_LAYER, N_DEV, ncol), F32), compiler_params=_cp(("parallel",)))(c_all, w_ada)


def _ada_finish(m_mine, b_ada):
    def body(m_ref, b_ref, o_ref, t_ref):
        o_ref[...] = m_ref[...] + b_ref[...]
        t_ref[...] = jnp.zeros_like(t_ref)

    return pl.pallas_call(body, name="ada_finish", out_shape=[SDS(b_ada.shape, F32), SDS((8, 128), F32)])(m_mine, b_ada)


def _ada_bwd(c_all, dmod_cols):
    ncol = dmod_cols.shape[2]

    def body(c_ref, d_ref, o_ref):
        ca = _silu(c_ref[...])
        ca = jnp.concatenate([ca, jnp.zeros_like(ca)], axis=0)
        dm = d_ref[...]
        dm = jnp.concatenate([dm, jnp.zeros_like(dm)], axis=0)
        o_ref[...] = _dot(ca, dm, "tn")

    return pl.pallas_call(
        body, name="ada_bwd", grid=(N_LAYER,),
        in_specs=[pl.BlockSpec((N_DEV, D), lambda l: (0, 0)), pl.BlockSpec((None, N_DEV, ncol), lambda l: (l, 0, 0))],
        out_specs=pl.BlockSpec((None, D, ncol), lambda l: (l, 0, 0)),
        out_shape=SDS((N_LAYER, D, ncol), F32), compiler_params=_cp(("parallel",)))(c_all, dmod_cols)


def _rows(s):
    tm = min(TM, s)
    return tm, pl.BlockSpec((tm, D), lambda i: (i, 0))


def _rms_mod(x, g, sc, sh, name):
    s = x.shape[0]
    tm, row = _rows(s)

    def body(x_ref, g_ref, sc_ref, sh_ref, h_ref):
        xf = x_ref[...]
        h_ref[...] = (xf * _rstd(xf) * (g_ref[...] * (1.0 + sc_ref[...])) + sh_ref[...]).astype(BF16)

    return pl.pallas_call(
        body, name=name, grid=(s // tm,), in_specs=[row, _vec(*g, 1), _vec(*sc, 1), _vec(*sh, 1)], out_specs=row,
        out_shape=SDS((s, D), BF16), compiler_params=_cp(("parallel",)))(x, g[0], sc[0], sh[0])


def _post(xres, o, ga, gpost, gn, scn, shn, name):
    s = xres.shape[0]
    tm, row = _rows(s)

    def body(x_ref, o_ref, ga_ref, gp_ref, gn_ref, sc_ref, sh_ref, xn_ref, h_ref):
        of = o_ref[...]
        xn = x_ref[...] + ga_ref[...] * (of * _rstd(of) * gp_ref[...])
        xn_ref[...] = xn
        h_ref[...] = (xn * _rstd(xn) * (gn_ref[...] * (1.0 + sc_ref[...])) + sh_ref[...]).astype(BF16)

    return pl.pallas_call(
        body, name=name, grid=(s // tm,),
        in_specs=[row, row, _vec(*ga, 1), _vec(*gpost, 1), _vec(*gn, 1), _vec(*scn, 1), _vec(*shn, 1)],
        out_specs=[row, row], out_shape=[SDS((s, D), F32), SDS((s, D), BF16)],
        compiler_params=_cp(("parallel",)))(xres, o, ga[0], gpost[0], gn[0], scn[0], shn[0])


def _post_loss(xres, o, ga, gpost, target, name):
    s = xres.shape[0]
    tm, row = _rows(s)

    def body(x_ref, o_ref, ga_ref, gp_ref, t_ref, dy_ref, loss_ref):
        of = o_ref[...]
        err = x_ref[...] + ga_ref[...] * (of * _rstd(of) * gp_ref[...]) - t_ref[...]
        dy_ref[...] = err * (1.0 / D)

        @pl.when(pl.program_id(0) == 0)
        def _():
            loss_ref[...] = jnp.zeros_like(loss_ref)

        loss_ref[...] += jnp.sum(jnp.mean(err * err, axis=-1, keepdims=True), axis=0, keepdims=True) * 0.5

    return pl.pallas_call(
        body, name=name, grid=(s // tm,), in_specs=[row, row, _vec(*ga, 1), _vec(*gpost, 1), row],
        out_specs=[row, pl.BlockSpec((8, 128), lambda i: (0, 0))], out_shape=[SDS((s, D), F32), SDS((8, 128), F32)],
        compiler_params=_cp(("arbitrary",)))(xres, o, ga[0], gpost[0], target)


def _acc(ref, val, first):
    @pl.when(first)
    def _():
        ref[...] = val

    @pl.when(jnp.logical_not(first))
    def _():
        ref[...] += val


def _post_bwd(dxn, o, ga, gpost, name):
    s = dxn.shape[0]
    tm, row = _rows(s)
    vec = pl.BlockSpec((1, D), lambda i: (0, 0))

    def body(d_ref, o_ref, ga_ref, gp_ref, do_ref, dga_ref, dgp_ref):
        of, dx = o_ref[...], d_ref[...]
        r = _rstd(of)
        oh = of * r
        do_ref[...] = _rms_bwd(oh, r, dx * (ga_ref[...] * gp_ref[...])).astype(BF16)
        cs = _colsum(dx * oh)
        first = pl.program_id(0) == 0
        _acc(dga_ref, cs * gp_ref[...], first)
        _acc(dgp_ref, cs * ga_ref[...], first)

    return pl.pallas_call(
        body, name=name, grid=(s // tm,), in_specs=[row, row, _vec(*ga, 1), _vec(*gpost, 1)], out_specs=[row, vec, vec],
        out_shape=[SDS((s, D), BF16), SDS((1, D), F32), SDS((1, D), F32)],
        compiler_params=_cp(("arbitrary",)))(dxn, o, ga[0], gpost[0])


def _pre_bwd(dh, x, dres, g, sc, name):
    s = x.shape[0]
    tm, row = _rows(s)
    vec = pl.BlockSpec((1, D), lambda i: (0, 0))

    def body(dh_ref, x_ref, dr_ref, g_ref, sc_ref, dx_ref, dsh_ref, dsc_ref, dg_ref):
        xf, d = x_ref[...], dh_ref[...]
        r = _rstd(xf)
        xh = xf * r
        dx_ref[...] = dr_ref[...] + _rms_bwd(xh, r, d * (g_ref[...] * (1.0 + sc_ref[...])))
        cs = _colsum(d * xh)
        first = pl.program_id(0) == 0
        _acc(dsh_ref, _colsum(d), first)
        _acc(dsc_ref, cs * g_ref[...], first)
        _acc(dg_ref, cs * (1.0 + sc_ref[...]), first)

    return pl.pallas_call(
        body, name=name, grid=(s // tm,), in_specs=[row, row, row, _vec(*g, 1), _vec(*sc, 1)],
        out_specs=[row, vec, vec, vec],
        out_shape=[SDS((s, D), F32), SDS((1, D), F32), SDS((1, D), F32), SDS((1, D), F32)],
        compiler_params=_cp(("arbitrary",)))(dh, x, dres, g[0], sc[0])


def _gnorm(ys, gg, name):
    s = ys[0].shape[0]
    tm = min(TM, s)
    yb = pl.BlockSpec((tm, GW), lambda i: (i, 0))

    def body(y0, y1, y2, y3, g_ref, o_ref):
        for i, yr in enumerate((y0, y1, y2, y3)):
            y = yr[...]
            o_ref[:, GW * i:GW * (i + 1)] = (y * _rstd(y) * g_ref[:, GW * i:GW * (i + 1)]).astype(BF16)

    return pl.pallas_call(
        body, name=name, grid=(s // tm,), in_specs=[yb] * 4 + [_vec(*gg, 1)], out_specs=pl.BlockSpec((tm, D), lambda i: (i, 0)),
        out_shape=SDS((s, D), BF16), compiler_params=_cp(("parallel",)))(*ys, gg[0])


def _gnorm_bwd(dyn, ys, gg, name):
    s = ys[0].shape[0]
    tm = min(TM, s)
    yb = pl.BlockSpec((tm, GW), lambda i: (i, 0))

    def body(d_ref, y0, y1, y2, y3, g_ref, o0, o1, o2, o3, dg_ref):
        first = pl.program_id(0) == 0
        for i, (yr, orf) in enumerate(zip((y0, y1, y2, y3), (o0, o1, o2, o3))):
            y = yr[...]
            d = d_ref[:, GW * i:GW * (i + 1)]
            r = _rstd(y)
            yh = y * r
            orf[...] = _rms_bwd(yh, r, d * g_ref[:, GW * i:GW * (i + 1)])
            cs = _colsum(d * yh)

            @pl.when(first)
            def _():
                dg_ref[:, GW * i:GW * (i + 1)] = cs

            @pl.when(jnp.logical_not(first))
            def _():
                dg_ref[:, GW * i:GW * (i + 1)] += cs

    return pl.pallas_call(
        body, name=name, grid=(s // tm,), in_specs=[pl.BlockSpec((tm, D), lambda i: (i, 0))] + [yb] * 4 + [_vec(*gg, 1)],
        out_specs=[yb] * 4 + [pl.BlockSpec((1, D), lambda i: (0, 0))],
        out_shape=[SDS((s, GW), F32)] * 4 + [SDS((1, D), F32)], compiler_params=_cp(("arbitrary",)))(dyn, *ys, gg[0])


def _lane_put(acc, col, h):
    lane = lax.broadcasted_iota(jnp.int32, acc.shape, 1)
    return jnp.where(lane == h, col, acc)


def _fgate(z, bf, name):
    s = z.shape[0]

    def body(z_ref, b_ref, fc_ref, fr_ref):
        xg = z_ref[...] + b_ref[...]
        lf = jnp.minimum(xg, 0.0) - jnp.log(1.0 + jnp.exp(-jnp.abs(xg)))
        lane = lax.broadcasted_iota(jnp.int32, lf.shape, 1)
        row = lax.broadcasted_iota(jnp.int32, lf.shape, 0)
        f = jnp.where(lane < 4, lf, 0.0)
        sh = 1
        while sh < s:
            f = f + jnp.where(row >= sh, pltpu.roll(f, sh, 0), 0.0)
            sh *= 2
        fc_ref[...] = f
        fr_ref[...] = f.T[:8, :]

    return pl.pallas_call(
        body, name=name, grid=(1,),
        in_specs=[pl.BlockSpec((s, 128), lambda i: (0, Z_FG)), pl.BlockSpec((1, 128), lambda i: (0, 0))],
        out_specs=[pl.BlockSpec((s, 128), lambda i: (0, 0)), pl.BlockSpec((8, s), lambda i: (0, 0))],
        out_shape=[SDS((s, 128), F32), SDS((8, s), F32)], compiler_params=_cp(("arbitrary",)))(z, bf)


def _fgate_bwd(z, bf, dfrow, dfcol, name):
    s = z.shape[0]

    def body(z_ref, b_ref, d_ref, dc_ref, dz_ref, db_ref):
        d = jnp.concatenate([d_ref[...], jnp.zeros((120, s), F32)], axis=0).T + dc_ref[...]
        row = lax.broadcasted_iota(jnp.int32, d.shape, 0)
        lane = lax.broadcasted_iota(jnp.int32, d.shape, 1)
        sh = 1
        while sh < s:
            d = d + jnp.where(row < s - sh, pltpu.roll(d, s - sh, 0), 0.0)
            sh *= 2
        xg = z_ref[...] + b_ref[...]
        dz = jnp.where(lane < 4, d * _sigmoid(-xg), 0.0)
        dz_ref[...] = dz.astype(BF16)
        db_ref[...] = _colsum(dz)

    return pl.pallas_call(
        body, name=name, grid=(1,),
        in_specs=[pl.BlockSpec((s, 128), lambda i: (0, Z_FG)), pl.BlockSpec((1, 128), lambda i: (0, 0)),
                  pl.BlockSpec((8, s), lambda i: (0, 0)), pl.BlockSpec((s, 128), lambda i: (0, 0))],
        out_specs=[pl.BlockSpec((s, 128), lambda i: (0, 0)), pl.BlockSpec((1, 128), lambda i: (0, 0))],
        out_shape=[SDS((s, 128), BF16), SDS((1, 128), F32)], compiler_params=_cp(("arbitrary",)))(z, bf, dfrow, dfcol)


def _fox_scores(q_ref, k_ref, fc_ref, fr_ref, h, i, nk, tq):
    kw = nk * tq
    q = q_ref[:, HD * h:HD * (h + 1)] * SCALE
    sc = _dot(q, k_ref[0:kw, HD * h:HD * (h + 1)], "nt") + fc_ref[:, h:h + 1] - fr_ref[h:h + 1, 0:kw]
    qpos = i * tq + lax.broadcasted_iota(jnp.int32, (tq, kw), 0)
    kpos = lax.broadcasted_iota(jnp.int32, (tq, kw), 1)
    return q, jnp.where(kpos <= qpos, sc, NEG)


def _fox_fwd(z, fcol, frow, name):
    s = z.shape[0]
    tq = min(TQ, s)
    nc = s // tq

    def body(q_ref, k_ref, v_ref, fc_ref, fr_ref, y_ref, l_ref):
        for n in range(nc):
            @pl.when(pl.program_id(0) == n)
            def _():
                kw = (n + 1) * tq
                lse = jnp.zeros((tq, 128), F32)
                for h in range(4):
                    _, sc = _fox_scores(q_ref, k_ref, fc_ref, fr_ref, h, n, n + 1, tq)
                    m = jnp.max(sc, axis=-1, keepdims=True)
                    p = jnp.exp(sc - m)
                    l = jnp.sum(p, axis=-1, keepdims=True)
                    y_ref[:, HD * h:HD * (h + 1)] = _dot(p, v_ref[0:kw, HD * h:HD * (h + 1)], "nn") / l
                    lse = _lane_put(lse, m + jnp.log(l), h)
                l_ref[...] = lse

    return pl.pallas_call(
        body, name=name, grid=(nc,),
        in_specs=[pl.BlockSpec((tq, GW), lambda i: (i, Z_FQ)), pl.BlockSpec((s, GW), lambda i: (0, Z_FK)),
                  pl.BlockSpec((s, GW), lambda i: (0, Z_FV)), pl.BlockSpec((tq, 128), lambda i: (i, 0)),
                  pl.BlockSpec((8, s), lambda i: (0, 0))],
        out_specs=[pl.BlockSpec((tq, GW), lambda i: (i, 0)), pl.BlockSpec((tq, 128), lambda i: (i, 0))],
        out_shape=[SDS((s, GW), F32), SDS((s, 128), F32)], compiler_params=_cp(("parallel",)))(z, z, z, fcol, frow)


def _fox_bwd(z, fcol, frow, lse, y, dy, name):
    s = z.shape[0]
    tq = min(TQ, s)
    nc = s // tq
    half = max(nc // 2, 1)

    def body(q_ref, k_ref, v_ref, fc_ref, fr_ref, l_ref, y_ref, dy_ref, dq_ref, dk_ref, dv_ref, df_ref, dfq_ref):
        @pl.when(pl.program_id(0) == 0)
        def _():
            dk_ref[...] = jnp.zeros_like(dk_ref)
            dv_ref[...] = jnp.zeros_like(dv_ref)
            df_ref[...] = jnp.zeros_like(df_ref)

        i = pl.program_id(0)
        for cond, nk in ((i < half, half), (i >= half, nc)):
            @pl.when(cond)
            def _():
                kw = nk * tq
                dfq = jnp.zeros((tq, 128), F32)
                for h in range(4):
                    hs = slice(HD * h, HD * (h + 1))
                    q, sc = _fox_scores(q_ref, k_ref, fc_ref, fr_ref, h, i, nk, tq)
                    p = jnp.exp(sc - l_ref[:, h:h + 1])
                    dyh = dy_ref[:, hs]
                    dd = jnp.sum(dyh * y_ref[:, hs], axis=-1, keepdims=True)
                    ds = p * (_dot(dyh, v_ref[0:kw, hs], "nt") - dd)
                    dq_ref[:, hs] = _dot(ds, k_ref[0:kw, hs], "nn") * SCALE
                    dk_ref[0:kw, hs] += _dot(ds, q, "tn")
                    dv_ref[0:kw, hs] += _dot(p, dyh, "tn")
                    df_ref[h:h + 1, 0:kw] -= _colsum(ds)
                    dfq = _lane_put(dfq, jnp.sum(ds, axis=-1, keepdims=True), h)
                dfq_ref[...] = dfq

    tile = lambda w: pl.BlockSpec((tq, w), lambda i: (i, 0))
    full = pl.BlockSpec((s, GW), lambda i: (0, 0))
    rows8 = pl.BlockSpec((8, s), lambda i: (0, 0))
    return pl.pallas_call(
        body, name=name, grid=(nc,),
        in_specs=[pl.BlockSpec((tq, GW), lambda i: (i, Z_FQ)), pl.BlockSpec((s, GW), lambda i: (0, Z_FK)),
                  pl.BlockSpec((s, GW), lambda i: (0, Z_FV)), tile(128), rows8, tile(128), tile(GW), tile(GW)],
        out_specs=[tile(GW), full, full, rows8, tile(128)],
        out_shape=[SDS((s, GW), F32), SDS((s, GW), F32), SDS((s, GW), F32), SDS((8, s), F32), SDS((s, 128), F32)],
        compiler_params=_cp(("arbitrary",)))(z, z, z, fcol, frow, lse, y, dy)


def _swa_block(q_ref, k_ref, v_ref, n):
    qs = pl.multiple_of(n * WIN, WIN)
    ks = pl.multiple_of(jnp.maximum(n - 1, 0) * WIN, WIN)
    qb = q_ref[pl.ds(qs, WIN), :]
    kb = k_ref[pl.ds(ks, 2 * WIN), :]
    vb = v_ref[pl.ds(ks, 2 * WIN), :]
    dist = (qs + lax.broadcasted_iota(jnp.int32, (WIN, 2 * WIN), 0)) - (ks + lax.broadcasted_iota(jnp.int32, (WIN, 2 * WIN), 1))
    return qs, ks, qb, kb, vb, (dist >= 0) & (dist < WIN)


def _swa_fwd(z, sinks, name):
    s = z.shape[0]

    def body(sink_ref, q_ref, k_ref, v_ref, y_ref, l_ref):
        def step(n, carry):
            qs, ks, qb, kb, vb, valid = _swa_block(q_ref, k_ref, v_ref, n)
            lse = jnp.zeros((WIN, 128), F32)
            for h in range(4):
                kv = slice(HD * (h // 2), HD * (h // 2 + 1))
                sc = jnp.where(valid, _dot(qb[:, HD * h:HD * (h + 1)] * SCALE, kb[:, kv], "nt"), NEG)
                sink = sink_ref[h]
                m = jnp.maximum(jnp.max(sc, axis=-1, keepdims=True), sink)
                p = jnp.exp(sc - m)
                den = jnp.sum(p, axis=-1, keepdims=True) + jnp.exp(sink - m)
                y_ref[pl.ds(qs, WIN), HD * h:HD * (h + 1)] = _dot(p, vb[:, kv], "nn") / den
                lse = _lane_put(lse, m + jnp.log(den), h)
            l_ref[pl.ds(qs, WIN), :] = lse
            return carry

        lax.fori_loop(0, s // WIN, step, 0)

    return pl.pallas_call(
        body, name=name, grid=(1,),
        in_specs=[pl.BlockSpec(memory_space=pltpu.SMEM), pl.BlockSpec((s, GW), lambda i: (0, Z_SQ)),
                  pl.BlockSpec((s, 128), lambda i: (0, Z_SK)), pl.BlockSpec((s, 128), lambda i: (0, Z_SV))],
        out_specs=[pl.BlockSpec((s, GW), lambda i: (0, 0)), pl.BlockSpec((s, 128), lambda i: (0, 0))],
        out_shape=[SDS((s, GW), F32), SDS((s, 128), F32)], compiler_params=_cp(("arbitrary",)))(sinks, z, z, z)


def _swa_bwd(z, sinks, lse, y, dy, name):
    s = z.shape[0]

    def body(sink_ref, q_ref, k_ref, v_ref, l_ref, y_ref, dy_ref, dq_ref, dk_ref, dv_ref, dsink_ref):
        dk_ref[...] = jnp.zeros_like(dk_ref)
        dv_ref[...] = jnp.zeros_like(dv_ref)
        dsink_ref[...] = jnp.zeros_like(dsink_ref)

        def step(n, carry):
            qs, ks, qb, kb, vb, valid = _swa_block(q_ref, k_ref, v_ref, n)
            lse_b = l_ref[pl.ds(qs, WIN), :]
            yb = y_ref[pl.ds(qs, WIN), :]
            dyb = dy_ref[pl.ds(qs, WIN), :]
            dsink = jnp.zeros((1, 128), F32)
            for h in range(4):
                hs = slice(HD * h, HD * (h + 1))
                kv = slice(HD * (h // 2), HD * (h // 2 + 1))
                q = qb[:, hs] * SCALE
                sc = jnp.where(valid, _dot(q, kb[:, kv], "nt"), NEG)
                lh = lse_b[:, h:h + 1]
                p = jnp.exp(sc - lh)
                dd = jnp.sum(dyb[:, hs] * yb[:, hs], axis=-1, keepdims=True)
                ds = p * (_dot(dyb[:, hs], vb[:, kv], "nt") - dd)
                dq_ref[pl.ds(qs, WIN), hs] = _dot(ds, kb[:, kv], "nn") * SCALE
                dk_ref[pl.ds(ks, 2 * WIN), kv] += _dot(ds, q, "tn")
                dv_ref[pl.ds(ks, 2 * WIN), kv] += _dot(p, dyb[:, hs], "tn")
                dsink = _lane_put(dsink, dsink[:, h:h + 1] - jnp.sum(jnp.exp(sink_ref[h] - lh) * dd, axis=0, keepdims=True), h)
            dsink_ref[...] += dsink
            return carry

        lax.fori_loop(0, s // WIN, step, 0)

    full = lambda w: pl.BlockSpec((s, w), lambda i: (0, 0))
    return pl.pallas_call(
        body, name=name, grid=(1,),
        in_specs=[pl.BlockSpec(memory_space=pltpu.SMEM), pl.BlockSpec((s, GW), lambda i: (0, Z_SQ)),
                  pl.BlockSpec((s, 128), lambda i: (0, Z_SK)), pl.BlockSpec((s, 128), lambda i: (0, Z_SV)),
                  full(128), full(GW), full(GW)],
        out_specs=[full(GW), full(128), full(128), pl.BlockSpec((1, 128), lambda i: (0, 0))],
        out_shape=[SDS((s, GW), F32), SDS((s, 128), F32), SDS((s, 128), F32), SDS((1, 128), F32)],
        compiler_params=_cp(("arbitrary",)))(sinks, z, z, z, lse, y, dy)


def _delayed(win, shift, halo):
    return win[halo:, :] if shift == 0 else pltpu.roll(win, shift, 0)[halo:, :]


def _prev_halo(width, halo, tm, col):
    return pl.BlockSpec((halo, width), lambda i: (jnp.maximum(i * (tm // halo) - 1, 0), col))


def _glu_window(a_ref, g_ref, ah_ref, gh_ref):
    keep = (pl.program_id(0) > 0).astype(F32)
    a = jnp.concatenate([ah_ref[...] * keep, a_ref[...]], axis=0)
    g = jnp.concatenate([gh_ref[...], g_ref[...]], axis=0)
    return a * _sigmoid(g)


def _conv_fwd(z, cw, cb, lg, lb, pw, pb, name):
    s = z.shape[0]
    tm = min(TM, s)

    def body(a_ref, g_ref, ah_ref, gh_ref, w_ref, b_ref, lg_ref, lb_ref, pw_ref, pb_ref, y_ref, hc_ref):
        hg = _glu_window(a_ref, g_ref, ah_ref, gh_ref)
        hc = jnp.zeros((tm, GW), F32) + b_ref[...]
        for k in range(CONV_K):
            hc = hc + w_ref[k:k + 1, :] * _delayed(hg, CONV_K - 1 - k, CONV_HALO)
        hc_ref[...] = hc
        xh, _ = _ln_stats(hc)
        y_ref[...] = _dot(_silu(xh * lg_ref[...] + lb_ref[...]), pw_ref[...], "nn") + pb_ref[...]

    tile = lambda col: pl.BlockSpec((tm, GW), lambda i: (i, col))
    whole = lambda a: pl.BlockSpec(a.shape, lambda i: (0, 0))
    return pl.pallas_call(
        body, name=name, grid=(s // tm,),
        in_specs=[tile(Z_CA), tile(Z_CG), _prev_halo(GW, CONV_HALO, tm, Z_CA), _prev_halo(GW, CONV_HALO, tm, Z_CG),
                  whole(cw), whole(cb), whole(lg), whole(lb), whole(pw), whole(pb)],
        out_specs=[tile(0), tile(0)], out_shape=[SDS((s, GW), F32), SDS((s, GW), F32)],
        compiler_params=_cp(("parallel",)))(z, z, z, z, cw, cb, lg, lb, pw, pb)


def _conv_bwd_a(z, hc, dy, cw, lg, lb, pw, name):
    s = z.shape[0]
    tm = min(TM, s)

    def body(a_ref, g_ref, ah_ref, gh_ref, hc_ref, dy_ref, lg_ref, lb_ref, pw_ref,
             dhc_ref, dpw_ref, dpb_ref, dlg_ref, dlb_ref, dcw_ref, dcb_ref):
        first = pl.program_id(0) == 0
        dy = dy_ref[...]
        xh, rstd = _ln_stats(hc_ref[...])
        hn = xh * lg_ref[...] + lb_ref[...]
        dhn = _dot(dy, pw_ref[...], "nt") * _dsilu(hn)
        dhc = _ln_bwd(xh, rstd, dhn * lg_ref[...])
        dhc_ref[...] = dhc
        _acc(dpw_ref, _dot(_silu(hn), dy, "tn"), first)
        _acc(dpb_ref, _colsum(dy), first)
        _acc(dlg_ref, _colsum(dhn * xh), first)
        _acc(dlb_ref, _colsum(dhn), first)
        _acc(dcb_ref, _colsum(dhc), first)
        hg = _glu_window(a_ref, g_ref, ah_ref, gh_ref)

        @pl.when(first)
        def _():
            dcw_ref[...] = jnp.zeros_like(dcw_ref)

        for k in range(CONV_K):
            dcw_ref[k:k + 1, :] += _colsum(dhc * _delayed(hg, CONV_K - 1 - k, CONV_HALO))

    tile = lambda col: pl.BlockSpec((tm, GW), lambda i: (i, col))
    whole = lambda shape: pl.BlockSpec(shape, lambda i: (0, 0))
    return pl.pallas_call(
        body, name=name, grid=(s // tm,),
        in_specs=[tile(Z_CA), tile(Z_CG), _prev_halo(GW, CONV_HALO, tm, Z_CA), _prev_halo(GW, CONV_HALO, tm, Z_CG),
                  tile(0), tile(0), whole(lg.shape), whole(lb.shape), whole(pw.shape)],
        out_specs=[tile(0), whole((GW, GW)), whole((1, GW)), whole((1, GW)), whole((1, GW)), whole((32, GW)), whole((1, GW))],
        out_shape=[SDS((s, GW), F32), SDS((GW, GW), F32), SDS((1, GW), F32), SDS((1, GW), F32), SDS((1, GW), F32),
                   SDS((32, GW), F32), SDS((1, GW), F32)],
        compiler_params=_cp(("arbitrary",)))(z, z, z, z, hc, dy, lg, lb, pw)


def _conv_bwd_b(z, dhc, cw, name):
    s = z.shape[0]
    tm = min(TM, s)
    nt = s // tm

    def body(a_ref, g_ref, d_ref, dn_ref, w_ref, da_ref, dg_ref):
        keep = (pl.program_id(0) < nt - 1).astype(F32)
        win = jnp.concatenate([d_ref[...], dn_ref[...] * keep], axis=0)
        dhg = jnp.zeros((tm, GW), F32)
        for k in range(CONV_K):
            sh = CONV_K - 1 - k
            dhg = dhg + w_ref[k:k + 1, :] * (win[:tm, :] if sh == 0 else pltpu.roll(win, tm + CONV_HALO - sh, 0)[:tm, :])
        sg = _sigmoid(g_ref[...])
        da_ref[...] = (dhg * sg).astype(BF16)
        dg_ref[...] = (dhg * a_ref[...] * sg * (1.0 - sg)).astype(BF16)

    tile = lambda col: pl.BlockSpec((tm, GW), lambda i: (i, col))
    nxt = pl.BlockSpec((CONV_HALO, GW), lambda i: (jnp.minimum((i + 1) * (tm // CONV_HALO), s // CONV_HALO - 1), 0))
    return pl.pallas_call(
        body, name=name, grid=(nt,),
        in_specs=[tile(Z_CA), tile(Z_CG), tile(0), nxt, pl.BlockSpec(cw.shape, lambda i: (0, 0))],
        out_specs=[tile(0), tile(0)], out_shape=[SDS((s, GW), BF16), SDS((s, GW), BF16)],
        compiler_params=_cp(("parallel",)))(z, z, dhc, dhc, cw)


def _sgu_chunk(zu, zv, lg, lb, wcat, bfull):
    u, v = _gelu(zu), _gelu(zv)
    xh, rstd = _ln_stats(v)
    vn = xh * lg + lb
    lane = lax.shift_right_logical(lax.broadcasted_iota(jnp.int32, (WIN, GW), 1), 6)
    r = jnp.concatenate([jnp.where(lane == g, vn, 0.0) for g in range(4)], axis=0)
    mix = _dot(wcat, r, "nn") + bfull
    return u, xh, rstd, r, mix, lane


def _tril4(w):
    t = lax.broadcasted_iota(jnp.int32, w.shape, 0)
    sidx = lax.broadcasted_iota(jnp.int32, w.shape, 1) & (WIN - 1)
    return jnp.where(sidx <= t, w, 0.0)


def _sgu_fwd(z, lg, lb, wcat, bfull, name):
    s = z.shape[0]
    tm = min(TM, s)

    def body(u_ref, v_ref, lg_ref, lb_ref, w_ref, b_ref, y_ref):
        w = _tril4(w_ref[...])
        for n in range(tm // WIN):
            rows = slice(WIN * n, WIN * (n + 1))
            u, _, _, _, mix, _ = _sgu_chunk(u_ref[rows, :], v_ref[rows, :], lg_ref[...], lb_ref[...], w, b_ref[...])
            y_ref[rows, :] = u * mix

    tile = lambda col: pl.BlockSpec((tm, GW), lambda i: (i, col))
    whole = lambda a: pl.BlockSpec(a.shape, lambda i: (0, 0))
    return pl.pallas_call(
        body, name=name, grid=(s // tm,), in_specs=[tile(Z_GU), tile(Z_GV), whole(lg), whole(lb), whole(wcat), whole(bfull)],
        out_specs=tile(0), out_shape=SDS((s, GW), F32), compiler_params=_cp(("parallel",)))(z, z, lg, lb, wcat, bfull)


def _sgu_bwd(z, dy, lg, lb, wcat, bfull, name):
    s = z.shape[0]
    tm = min(TM, s)

    def body(u_ref, v_ref, dy_ref, lg_ref, lb_ref, w_ref, b_ref, du_ref, dv_ref, dw_ref, db_ref, dlg_ref, dlb_ref):
        first = pl.program_id(0) == 0
        w = _tril4(w_ref[...])
        wt = w.T
        dw = jnp.zeros((WIN, 4 * WIN), F32)
        db = jnp.zeros((WIN, 128), F32)
        dlg = jnp.zeros((1, GW), F32)
        dlb = jnp.zeros((1, GW), F32)
        for n in range(tm // WIN):
            rows = slice(WIN * n, WIN * (n + 1))
            zu, zv, dout = u_ref[rows, :], v_ref[rows, :], dy_ref[rows, :]
            u, xh, rstd, r, mix, lane = _sgu_chunk(zu, zv, lg_ref[...], lb_ref[...], w, b_ref[...])
            dmix = dout * u
            du_ref[rows, :] = (dout * mix * _dgelu(zu)).astype(BF16)
            dw = dw + _dot(dmix, r, "nt")
            for g in range(4):
                db = _lane_put(db, db[:, g:g + 1] + jnp.sum(dmix[:, HD * g:HD * (g + 1)], axis=1, keepdims=True), g)
            dr = _dot(wt, dmix, "nn")
            dvn = jnp.zeros((WIN, GW), F32)
            for g in range(4):
                dvn = dvn + jnp.where(lane == g, dr[WIN * g:WIN * (g + 1), :], 0.0)
            dlg = dlg + _colsum(dvn * xh)
            dlb = dlb + _colsum(dvn)
            dv_ref[rows, :] = (_ln_bwd(xh, rstd, dvn * lg_ref[...]) * _dgelu(zv)).astype(BF16)
        _acc(dw_ref, _tril4(dw), first)
        _acc(db_ref, db, first)
        _acc(dlg_ref, dlg, first)
        _acc(dlb_ref, dlb, first)

    tile = lambda col: pl.BlockSpec((tm, GW), lambda i: (i, col))
    whole = lambda shape: pl.BlockSpec(shape, lambda i: (0, 0))
    return pl.pallas_call(
        body, name=name, grid=(s // tm,),
        in_specs=[tile(Z_GU), tile(Z_GV), tile(0), whole(lg.shape), whole(lb.shape), whole(wcat.shape), whole(bfull.shape)],
        out_specs=[tile(0), tile(0), whole((WIN, 4 * WIN)), whole((WIN, 128)), whole((1, GW)), whole((1, GW))],
        out_shape=[SDS((s, GW), BF16), SDS((s, GW), BF16), SDS((WIN, 4 * WIN), F32), SDS((WIN, 128), F32),
                   SDS((1, GW), F32), SDS((1, GW), F32)],
        compiler_params=_cp(("arbitrary",)))(z, z, dy, lg, lb, wcat, bfull)


def _conv3(win, w, b):
    return (w[2:3, :] * win[FFN_HALO:, :] + w[1:2, :] * pltpu.roll(win, 1, 0)[FFN_HALO:, :]
            + w[0:1, :] * pltpu.roll(win, 2, 0)[FFN_HALO:, :] + b)


def _ffn_specs(s, tm):
    main = pl.BlockSpec((2, None, tm, FF_BLK), lambda j, i: (0, j, i, 0))
    prev = pl.BlockSpec((2, None, FFN_HALO, FF_BLK), lambda j, i: (0, j, jnp.maximum(i * (tm // FFN_HALO) - 1, 0), 0))
    nxt = pl.BlockSpec((2, None, FFN_HALO, FF_BLK),
                       lambda j, i: (0, j, jnp.minimum((i + 1) * (tm // FFN_HALO), s // FFN_HALO - 1), 0))
    wsp = pl.BlockSpec((2, None, 3, FF_BLK), lambda j, i: (0, j, 0, 0))
    bsp = pl.BlockSpec((2, None, 1, FF_BLK), lambda j, i: (0, j, 0, 0))
    return main, prev, nxt, wsp, bsp


def _ffn_act(u4, w4, b4, name):
    s = u4.shape[2]
    tm = min(TM, s)
    main, prev, _, wsp, bsp = _ffn_specs(s, tm)

    def body(u_ref, uh_ref, w_ref, b_ref, o_ref):
        keep = (pl.program_id(1) > 0).astype(F32)
        gw, vw = [jnp.concatenate([uh_ref[p].astype(F32) * keep, u_ref[p].astype(F32)], axis=0) for p in range(2)]
        o_ref[...] = (_silu(_conv3(gw, w_ref[0], b_ref[0])) * _conv3(vw, w_ref[1], b_ref[1])).astype(BF16)

    return pl.pallas_call(
        body, name=name, grid=(FF_NBLK, s // tm), in_specs=[main, prev, wsp, bsp],
        out_specs=pl.BlockSpec((None, tm, FF_BLK), lambda j, i: (j, i, 0)), out_shape=SDS((FF_NBLK, s, FF_BLK), BF16),
        compiler_params=_cp(("parallel", "parallel")))(u4, u4, w4, b4)


def _ffn_bwd(u4, dact, w4, b4, name):
    s = u4.shape[2]
    tm = min(TM, s)
    nt = s // tm
    main, prev, nxt, wsp, bsp = _ffn_specs(s, tm)
    dmain = pl.BlockSpec((None, tm, FF_BLK), lambda j, i: (j, i, 0))
    dnext = pl.BlockSpec((None, FFN_HALO, FF_BLK), lambda j, i: (j, jnp.minimum((i + 1) * (tm // FFN_HALO), s // FFN_HALO - 1), 0))
    ext = tm + FFN_HALO

    def body(u_ref, up_ref, un_ref, d_ref, dn_ref, w_ref, b_ref, du_ref, dw_ref, db_ref):
        i = pl.program_id(1)
        first = i == 0
        keep_prev = (i > 0).astype(F32)
        keep_next = (i < nt - 1).astype(F32)
        wins = [jnp.concatenate([up_ref[p].astype(F32) * keep_prev, u_ref[p].astype(F32), un_ref[p].astype(F32)], axis=0)
                for p in range(2)]
        gc = _conv3(wins[0], w_ref[0], b_ref[0])
        vc = _conv3(wins[1], w_ref[1], b_ref[1])
        d = jnp.concatenate([d_ref[...].astype(F32), dn_ref[...].astype(F32) * keep_next], axis=0)
        sg = _sigmoid(gc)
        duc = (d * vc * (sg * (1.0 + gc * (1.0 - sg))), d * (gc * sg))
        for p in range(2):
            w = w_ref[p]
            du_ref[p] = (w[2:3, :] * duc[p][:tm, :] + w[1:2, :] * pltpu.roll(duc[p], ext - 1, 0)[:tm, :]
                         + w[0:1, :] * pltpu.roll(duc[p], ext - 2, 0)[:tm, :]).astype(BF16)
            own = duc[p][:tm, :]
            taps = [_colsum(own * (wins[p] if k == 2 else pltpu.roll(wins[p], 2 - k, 0))[FFN_HALO:FFN_HALO + tm, :])
                    for k in range(3)]

            @pl.when(first)
            def _():
                db_ref[p] = _colsum(own)
                for k in range(3):
                    dw_ref[p, k:k + 1, :] = taps[k]

            @pl.when(jnp.logical_not(first))
            def _():
                db_ref[p] += _colsum(own)
                for k in range(3):
                    dw_ref[p, k:k + 1, :] += taps[k]

    return pl.pallas_call(
        body, name=name, grid=(FF_NBLK, nt), in_specs=[main, prev, nxt, dmain, dnext, wsp, bsp], out_specs=[main, wsp, bsp],
        out_shape=[SDS(u4.shape, BF16), SDS((2, FF_NBLK, 3, FF_BLK), F32), SDS((2, FF_NBLK, 1, FF_BLK), F32)],
        compiler_params=_cp(("parallel", "arbitrary")))(u4, u4, u4, dact, dact, w4, b4)


def _sum8(parts, name):
    _, r, c = parts.shape
    tr = r
    for cand in (512, 256, 128, 64, 32, 16):
        if r % cand == 0 and r > cand:
            tr = cand
            break

    def body(p_ref, o_ref):
        acc = p_ref[0].astype(F32)
        for j in range(1, N_DEV):
            acc = acc + p_ref[j].astype(F32)
        o_ref[...] = acc

    return pl.pallas_call(
        body, name=name, grid=(r // tr,), in_specs=[pl.BlockSpec((N_DEV, tr, c), lambda i: (0, i, 0))],
        out_specs=pl.BlockSpec((tr, c), lambda i: (i, 0)), out_shape=SDS((r, c), F32),
        compiler_params=_cp(("parallel",)))(parts)


def _sum8_small(parts, name):
    n = len(parts)

    def body(*refs):
        for p_ref, o_ref in zip(refs[:n], refs[n:]):
            acc = p_ref[0]
            for j in range(1, N_DEV):
                acc = acc + p_ref[j]
            o_ref[...] = acc

    return pl.pallas_call(body, name=name, out_shape=[SDS(p.shape[1:], F32) for p in parts], compiler_params=_cp())(*parts)


def _adamw_math(w, g, m, v):
    m = ADAM_B1 * m + (1.0 - ADAM_B1) * g
    v = ADAM_B2 * v + (1.0 - ADAM_B2) * (g * g)
    m_hat = m / (1.0 - ADAM_B1 ** ADAM_STEP)
    v_hat = v / (1.0 - ADAM_B2 ** ADAM_STEP)
    return -ADAM_LR * (m_hat / (jnp.sqrt(v_hat) + ADAM_EPS) + ADAM_WD * w), m, v


def _adamw(w, g, m, v, name):
    r, c = w.shape
    tr = r
    for cand in (256, 128, 64):
        if r % cand == 0 and r > cand:
            tr = cand
            break

    def body(w_ref, g_ref, m_ref, v_ref, d_ref, mo_ref, vo_ref):
        d_ref[...], mo_ref[...], vo_ref[...] = _adamw_math(w_ref[...], g_ref[...], m_ref[...], v_ref[...])

    blk = pl.BlockSpec((tr, c), lambda i: (i, 0))
    return pl.pallas_call(body, name=name, grid=(r // tr,), in_specs=[blk] * 4, out_specs=[blk] * 3,
                          out_shape=[SDS((r, c), F32)] * 3, compiler_params=_cp(("parallel",)))(w, g, m, v)


def _adamw_small(ws, gs, ms, vs, name):
    n = len(ws)

    def body(*refs):
        ins, outs = refs[:4 * n], refs[4 * n:]
        for i in range(n):
            d, m, v = _adamw_math(ins[i][...], ins[n + i][...], ins[2 * n + i][...], ins[3 * n + i][...])
            outs[i][...], outs[n + i][...], outs[2 * n + i][...] = d, m, v

    shapes = [SDS(w.shape, F32) for w in ws]
    res = pl.pallas_call(body, name=name, out_shape=shapes * 3, compiler_params=_cp())(*ws, *gs, *ms, *vs)
    return res[:n], res[n:2 * n], res[2 * n:]


def _perm_in(w):
    pad = jnp.zeros(w.shape[:-1] + (ZW - 2308,), w.dtype)
    return jnp.concatenate([w[..., :768], w[..., 772:], w[..., 768:772], pad], axis=-1)


def _unperm_in(g):
    return jnp.concatenate([g[..., :768], g[..., 2304:2308], g[..., 768:2304]], axis=-1)


def _wcat(sgu_w):
    return sgu_w.transpose(1, 0, 2).reshape(WIN, 4 * WIN)


def _layer_fwd(l, x, h1, mod, p, wg, last, target, nxt):
    s = x.shape[0]
    tag = f"_l{l}"
    mrow = lambda k: (mod, 6 * l + k)
    z = _mm_rows(h1, wg["w_in"].get(h1), "nn", ZW, F32, "mm_z" + tag)
    fcol, frow = _fgate(z, p["bf"], "fgate" + tag)
    y_fox, lse_fox = _fox_fwd(z, fcol, frow, "fox_fwd" + tag)
    y_conv, hc = _conv_fwd(z, wg["conv_w"], p["conv_b"], p["conv_ln_g"], p["conv_ln_b"], wg["conv_pw_w"], p["conv_pw_b"],
                           "conv_fwd" + tag)
    y_swa, lse_swa = _swa_fwd(z, p["sinks"], "swa_fwd" + tag)
    y_sgu = _sgu_fwd(z, p["sgu_ln_g"], p["sgu_ln_b"], p["wcat"], p["bfull"], "sgu_fwd" + tag)
    ys = (y_fox, y_conv, y_swa, y_sgu)
    yn = _gnorm(ys, (p["g_group"], l), "gnorm" + tag)
    o = _mm_rows(yn, wg["w_out"].get(yn), "nn", D, F32, "mm_o" + tag)
    x1, h2 = _post(x, o, mrow(2), (p["g_post_mix"], l), (p["g_pre_ffn"], l), mrow(4), mrow(3), "post_mix" + tag)
    tm = min(TM, s)
    u = _matmul(h2, wg["w_up"].get(h2), "nn", (N_DEV, s, FF_BLK), BF16, (N_DEV, 1, 1),
                _bs((s, D), lambda j, i, k: (0, 0)), _bs((None, D, FF_BLK), lambda j, i, k: (j, 0, 0)),
                _bs((None, s, FF_BLK), lambda j, i, k: (j, 0, 0)), None, "mm_u" + tag)
    u4 = u.reshape(2, FF_NBLK, s, FF_BLK)
    act = _ffn_act(u4, wg["ffn_conv_w"], p["ffn_conv_b"], "ffn_act" + tag)
    f = _matmul(act, wg["w_down"].get(act), "nn", (s, D), F32, (1, 1, FF_NBLK),
                _bs((None, s, FF_BLK), lambda i, j, k: (k, 0, 0)), _bs((FF_BLK, D), lambda i, j, k: (k, 0)),
                _bs((s, D), lambda i, j, k: (0, 0)), (s, D), "mm_f" + tag)
    if last:
        out = _post_loss(x1, f, mrow(5), (p["g_post_ffn"], l), target, "post_loss")
    else:
        out = _post(x1, f, mrow(5), (p["g_post_ffn"], l), *nxt, "post_ffn" + tag)
    saved = dict(x=x, h1=h1, z=z, fcol=fcol, frow=frow, lse_fox=lse_fox, hc=hc, lse_swa=lse_swa, ys=ys, yn=yn, o=o, x1=x1,
                 h2=h2, u4=u4, act=act, f=f)
    return out, saved


def _tie(a, token):
    return a if token is None else a + token[0, 0]


def _layer_bwd(l, dx2, sv, mod, p, wg, emit):
    s = dx2.shape[0]
    tm = min(TM, s)
    tag = f"_l{l}"
    mrow = lambda k: (mod, 6 * l + k)
    g = {}
    df, g["ga2"], g["g_post_ffn"] = _post_bwd(dx2, sv["f"], mrow(5), (p["g_post_ffn"], l), "post_ffn_bwd" + tag)
    dact = _matmul(df, wg["w_down"].get(None), "nt", (FF_NBLK, s, FF_BLK), BF16, (FF_NBLK, 1, 1),
                   _bs((s, D), lambda j, i, k: (0, 0)), _bs((FF_BLK, D), lambda j, i, k: (j, 0)),
                   _bs((None, s, FF_BLK), lambda j, i, k: (j, 0, 0)), None, "mm_dact" + tag)
    tok = emit("w_down", _matmul(sv["act"], df, "tn", (FF_NBLK * FF_BLK, D), BF16, (FF_NBLK, 1, 1),
                                 _bs((None, s, FF_BLK), lambda j, i, k: (j, 0, 0)), _bs((s, D), lambda j, i, k: (0, 0)),
                                 _bs((FF_BLK, D), lambda j, i, k: (j, 0)), None, "mm_dwdown" + tag))
    du, g["ffn_conv_w"], g["ffn_conv_b"] = _ffn_bwd(sv["u4"], dact, wg["ffn_conv_w"], _tie(p["ffn_conv_b"], tok),
                                                    "ffn_bwd" + tag)
    du = du.reshape(N_DEV, s, FF_BLK)
    dh2 = _matmul(du, wg["w_up"].get(None), "nt", (s, D), F32, (1, 1, N_DEV),
                  _bs((None, s, FF_BLK), lambda i, j, k: (k, 0, 0)), _bs((None, D, FF_BLK), lambda i, j, k: (k, 0, 0)),
                  _bs((s, D), lambda i, j, k: (0, 0)), (s, D), "mm_dh2" + tag)
    tok = emit("w_up", _matmul(du, sv["h2"], "tn", (N_DEV, FF_BLK, D), BF16, (N_DEV, 1, 1),
                               _bs((None, s, FF_BLK), lambda j, i, k: (j, 0, 0)), _bs((s, D), lambda j, i, k: (0, 0)),
                               _bs((None, FF_BLK, D), lambda j, i, k: (j, 0, 0)), None, "mm_dwup" + tag))
    dx1, g["sh2"], g["sc2"], g["g_pre_ffn"] = _pre_bwd(dh2, sv["x1"], dx2, (_tie(p["g_pre_ffn"], tok), l), mrow(4),
                                                       "pre_ffn_bwd" + tag)
    do, g["ga1"], g["g_post_mix"] = _post_bwd(dx1, sv["o"], mrow(2), (p["g_post_mix"], l), "post_mix_bwd" + tag)
    dyn = _mm_rows(do, wg["w_out"].get(None), "nt", D, F32, "mm_dyn" + tag)
    tok = emit("w_out", _mm_wgrad(sv["yn"], do, BF16, "mm_dwout" + tag))
    dy_fox, dy_conv, dy_swa, dy_sgu, g["g_group"] = _gnorm_bwd(dyn, sv["ys"], (_tie(p["g_group"], tok), l), "gnorm_bwd" + tag)
    z = sv["z"]
    dq_f, dk_f, dv_f, dfrow, dfcol = _fox_bwd(z, sv["fcol"], sv["frow"], sv["lse_fox"], sv["ys"][0], dy_fox, "fox_bwd" + tag)
    dgate, g["bf"] = _fgate_bwd(z, p["bf"], dfrow, dfcol, "fgate_bwd" + tag)
    dhc, g["conv_pw_w"], g["conv_pw_b"], g["conv_ln_g"], g["conv_ln_b"], g["conv_w"], g["conv_b"] = _conv_bwd_a(
        z, sv["hc"], dy_conv, wg["conv_w"], p["conv_ln_g"], p["conv_ln_b"], wg["conv_pw_w"], "conv_bwd_a" + tag)
    da_c, dg_c = _conv_bwd_b(z, dhc, wg["conv_w"], "conv_bwd_b" + tag)
    dq_s, dk_s, dv_s, g["sinks"] = _swa_bwd(z, p["sinks"], sv["lse_swa"], sv["ys"][2], dy_swa, "swa_bwd" + tag)
    du_g, dv_g, g["wcat"], g["sgu_bcol"], g["sgu_ln_g"], g["sgu_ln_b"] = _sgu_bwd(
        z, dy_sgu, p["sgu_ln_g"], p["sgu_ln_b"], p["wcat"], p["bfull"], "sgu_bwd" + tag)
    dz = jnp.concatenate([dq_f.astype(BF16), dk_f.astype(BF16), dv_f.astype(BF16), da_c, dg_c, dq_s.astype(BF16), dk_s.astype(BF16),
                          dv_s.astype(BF16), du_g, dv_g, dgate], axis=1)
    dh1 = _mm_rows(dz, wg["w_in"].get(None), "nt", D, F32, "mm_dh1" + tag)
    tok = emit("w_in", _mm_wgrad(sv["h1"], dz, BF16, "mm_dwin" + tag))
    dx, g["sh1"], g["sc1"], g["g_pre_mix"] = _pre_bwd(dh1, sv["x"], dx1, (_tie(p["g_pre_mix"], tok), l), mrow(1),
                                                      "pre_mix_bwd" + tag)
    return dx, g


def _layer_params(l, small, conv_w_full, conv_pw_full, ffn_conv_w_full):
    bf = jnp.pad(small["b_fgate"][l][None, :], ((0, 0), (0, 124)))
    p = dict(
        bf=bf, conv_b=small["conv_b"][l][None], conv_ln_g=small["conv_ln_g"][l][None], conv_ln_b=small["conv_ln_b"][l][None],
        conv_pw_b=small["conv_pw_b"][l][None], sinks=small["swa_sinks"][l], sgu_ln_g=small["sgu_ln_g"][l][None],
        sgu_ln_b=small["sgu_ln_b"][l][None], wcat=_wcat(small["sgu_w"][l]),
        bfull=jnp.repeat(small["sgu_b"][l].T, HD, axis=1),
        ffn_conv_b=small["ffn_conv_b"][l].reshape(2, FF_NBLK, 1, FF_BLK),
        g_group=small["g_group"].reshape(N_LAYER, 1, D), g_post_mix=small["g_post_mix"].reshape(N_LAYER, 1, D),
        g_pre_ffn=small["g_pre_ffn"].reshape(N_LAYER, 1, D), g_post_ffn=small["g_post_ffn"].reshape(N_LAYER, 1, D),
        g_pre_mix=small["g_pre_mix"].reshape(N_LAYER, 1, D))
    wsmall = dict(conv_w=conv_w_full[l], conv_pw_w=conv_pw_full[l].astype(BF16),
                  ffn_conv_w=ffn_conv_w_full[l].reshape(3, 2, FF_NBLK, FF_BLK).transpose(1, 2, 0, 3))
    return p, wsmall


def _local_step(x, target, mod, small, wbig, conv_w_full, conv_pw_full, ffn_conv_w_full, emit):
    ps, wgs = [], []
    for l in range(N_LAYER):
        p, wsmall = _layer_params(l, small, conv_w_full, conv_pw_full, ffn_conv_w_full)
        ps.append(p)
        wgs.append({**wbig[l], **wsmall})
    h = _rms_mod(x, (ps[0]["g_pre_mix"], 0), (mod, 1), (mod, 0), "rms_mod_l0")
    saved = []
    for l in range(N_LAYER):
        last = l == N_LAYER - 1
        nxt = None if last else ((ps[l]["g_pre_mix"], l + 1), (mod, 6 * (l + 1) + 1), (mod, 6 * (l + 1)))
        out, sv = _layer_fwd(l, x, h, mod, ps[l], wgs[l], last, target, nxt)
        saved.append(sv)
        if not last:
            x, h = out
    dx, loss = out
    grads = [None] * N_LAYER
    for l in reversed(range(N_LAYER)):
        dx, grads[l] = _layer_bwd(l, dx, saved[l], mod, ps[l], wgs[l], functools.partial(emit, l))
    return loss, dx, grads


_SMALL = ("b_ada", "g_pre_mix", "g_post_mix", "g_pre_ffn", "g_post_ffn", "b_fgate", "conv_b", "conv_ln_g", "conv_ln_b",
          "conv_pw_b", "swa_sinks", "sgu_ln_g", "sgu_ln_b", "sgu_w", "sgu_b", "g_group", "ffn_conv_b")
_WEIGHTS = ("w_ada", "b_ada", "g_pre_mix", "g_post_mix", "g_pre_ffn", "g_post_ffn", "w_in", "b_fgate", "conv_w", "conv_b",
            "conv_ln_g", "conv_ln_b", "conv_pw_w", "conv_pw_b", "swa_sinks", "sgu_ln_g", "sgu_ln_b", "sgu_w", "sgu_b",
            "g_group", "w_out", "ffn_w_up", "ffn_conv_w", "ffn_conv_b", "ffn_w_down")


def _pad_rows(a, mult):
    r = (-a.shape[0]) % mult
    return a if r == 0 else jnp.concatenate([a, jnp.zeros((r,) + a.shape[1:], a.dtype)], axis=0)


def _view2d(a):
    if a.ndim == 2:
        return a
    return a.reshape(-1, a.shape[-1])


def kernel(x, c, w_ada, b_ada, g_pre_mix, g_post_mix, g_pre_ffn, g_post_ffn, w_in, b_fgate, conv_w, conv_b, conv_ln_g, conv_ln_b, conv_pw_w, conv_pw_b, swa_sinks, sgu_ln_g, sgu_ln_b, sgu_w, sgu_b, g_group, w_out, ffn_w_up, ffn_conv_w, ffn_conv_b, ffn_w_down, loss_target, m_w_ada, m_b_ada, m_g_pre_mix, m_g_post_mix, m_g_pre_ffn, m_g_post_ffn, m_w_in, m_b_fgate, m_conv_w, m_conv_b, m_conv_ln_g, m_conv_ln_b, m_conv_pw_w, m_conv_pw_b, m_swa_sinks, m_sgu_ln_g, m_sgu_ln_b, m_sgu_w, m_sgu_b, m_g_group, m_w_out, m_ffn_w_up, m_ffn_conv_w, m_ffn_conv_b, m_ffn_w_down, v_w_ada, v_b_ada, v_g_pre_mix, v_g_post_mix, v_g_pre_ffn, v_g_post_ffn, v_w_in, v_b_fgate, v_conv_w, v_conv_b, v_conv_ln_g, v_conv_ln_b, v_conv_pw_w, v_conv_pw_b, v_swa_sinks, v_sgu_ln_g, v_sgu_ln_b, v_sgu_w, v_sgu_b, v_g_group, v_w_out, v_ffn_w_up, v_ffn_conv_w, v_ffn_conv_b, v_ffn_w_down):
    env = dict(locals())
    w = {n: env[n] for n in _WEIGHTS}
    mom = {n: env["m_" + n] for n in _WEIGHTS}
    var = {n: env["v_" + n] for n in _WEIGHTS}
    me = 4 * lax.axis_index("x") + 2 * lax.axis_index("y") + lax.axis_index("c")
    x2, target = x[0], loss_target[0]

    (c_all,) = _exchange([c], ["bcast"], "gather_c")
    c_all = c_all.reshape(N_DEV, D)
    (m_all,) = _exchange([_ada_fwd(c_all, w_ada)], ["bcast"], "gather_mod")
    m_mine = lax.dynamic_index_in_dim(m_all, me, axis=2, keepdims=False)
    mod, mod_token = _ada_finish(m_mine.transpose(1, 0, 2).reshape(N_LAYER, 6 * D), b_ada)
    mod = mod.reshape(6 * N_LAYER, 1, D)

    shards = [_tie(conv_w, mod_token), conv_pw_w, ffn_conv_w]
    for l in range(N_LAYER):
        shards += [_perm_in(w_in[l]).astype(BF16), w_out[l].astype(BF16), ffn_w_up[l].astype(BF16), ffn_w_down[l].astype(BF16)]
    flight = _xchg_start(shards, ["bcast"] * len(shards), "gather_weights_start")
    mod = _tie(mod, flight.token)
    g_cw, g_pw, g_fcw = _xchg_wait(flight, [0, 1, 2], mod, "gather_small_wait")
    conv_w_full = g_cw.transpose(1, 2, 0, 3).reshape(N_LAYER, CONV_K, GW)
    conv_pw_full = g_pw.transpose(1, 0, 2, 3).reshape(N_LAYER, GW, GW)
    ffn_conv_w_full = g_fcw.transpose(1, 2, 0, 3).reshape(N_LAYER, 3, N_DEV * FF_BLK)

    def lazy(i, shape, name):
        return _Lazy(lambda after: _xchg_wait(flight, [i], after, name)[0].reshape(shape))

    wbig = [dict(w_in=lazy(3 + 4 * l, (D, ZW), f"wait_w_in_l{l}"), w_out=lazy(4 + 4 * l, (D, D), f"wait_w_out_l{l}"),
                 w_up=lazy(5 + 4 * l, (N_DEV, D, FF_BLK), f"wait_w_up_l{l}"),
                 w_down=lazy(6 + 4 * l, (FF_NBLK * FF_BLK, D), f"wait_w_down_l{l}")) for l in range(N_LAYER)]

    grad_flights = []

    def emit(l, key, arr):
        fl = _xchg_start([arr.reshape(N_DEV, -1, arr.shape[-1])], ["a2a"], f"grad_start_{key}_l{l}")
        grad_flights.append(((l, key), fl))
        return fl.token

    small = {n: w[n] for n in _SMALL}
    loss8, dx, grads = _local_step(x2, target, mod, small, wbig, conv_w_full, conv_pw_full, ffn_conv_w_full, emit)
    loss = lax.psum(loss8[0, 0], ("x", "y", "c"))
    grad_x = dx[None]


    st = lambda key: jnp.stack([grads[l][key] for l in range(N_LAYER)])
    d_conv_w = st("conv_w")[:, :CONV_K, :].reshape(N_LAYER, CONV_K, N_DEV, GW // N_DEV).transpose(2, 0, 1, 3)
    d_pw_w = st("conv_pw_w").reshape(N_LAYER, N_DEV, GW // N_DEV, GW).transpose(1, 0, 2, 3)
    d_fcw = st("ffn_conv_w").reshape(N_LAYER, N_DEV, 3, FF_BLK).transpose(1, 0, 2, 3)
    rows_d = _pad_rows(jnp.concatenate(
        [grads[l][k] for l in range(N_LAYER) for k in ("sh1", "sc1", "ga1", "sh2", "sc2", "ga2")]
        + [grads[l][k] for k in ("g_pre_mix", "g_post_mix", "g_pre_ffn", "g_post_ffn", "g_group") for l in range(N_LAYER)],
        axis=0), 8)
    rows_gw = _pad_rows(jnp.concatenate(
        [grads[l][k] for k in ("conv_b", "conv_ln_g", "conv_ln_b", "conv_pw_b", "sgu_ln_g", "sgu_ln_b") for l in range(N_LAYER)],
        axis=0), 8)
    rows_128 = jnp.concatenate([_pad_rows(jnp.concatenate([grads[l]["bf"] for l in range(N_LAYER)]
                                                          + [grads[l]["sinks"] for l in range(N_LAYER)], axis=0), 8)]
                               + [grads[l]["sgu_bcol"] for l in range(N_LAYER)], axis=0)
    rows_w = jnp.concatenate([grads[l]["wcat"] for l in range(N_LAYER)], axis=0)
    rows_fb = st("ffn_conv_b").reshape(N_LAYER * N_DEV, FF_BLK)
    small_flight = _xchg_start([d_conv_w, d_pw_w, d_fcw, rows_d, rows_gw, rows_128, rows_w, rows_fb],
                               ["a2a"] * 3 + ["bcast"] * 5, "small_grads_start")

    big_sum = {}
    for (l, key), fl in grad_flights:
        (parts,) = _xchg_wait(fl, [0], small_flight.token, f"grad_wait_{key}_l{l}")
        big_sum[(l, key)] = _sum8(parts, f"sum_{key}_l{l}")
    to_mem = {"w_in": lambda a: a.transpose(2, 0, 1), "ffn_w_up": lambda a: a.transpose(0, 2, 1)}
    from_mem = {"w_in": lambda a: a.transpose(1, 2, 0), "ffn_w_up": lambda a: a.transpose(0, 2, 1)}
    gr = {}
    gr["w_in"] = to_mem["w_in"](_unperm_in(jnp.stack([big_sum[(l, "w_in")] for l in range(N_LAYER)])))
    gr["w_out"] = jnp.stack([big_sum[(l, "w_out")] for l in range(N_LAYER)])
    gr["ffn_w_up"] = jnp.stack([big_sum[(l, "w_up")] for l in range(N_LAYER)])
    gr["ffn_w_down"] = jnp.stack([big_sum[(l, "w_down")] for l in range(N_LAYER)])
    delta, new_m, new_v = {}, {}, {}

    def adamw_big(n):
        view, back = to_mem.get(n, lambda a: a), from_mem.get(n, lambda a: a)
        shape = view(w[n]).shape
        d, m2, v2 = _adamw(_view2d(view(w[n])), _view2d(gr[n]), _view2d(view(mom[n])), _view2d(view(var[n])), "adamw_" + n)
        delta[n], new_m[n], new_v[n] = back(d.reshape(shape)), back(m2.reshape(shape)), back(v2.reshape(shape))
        gr[n] = back(gr[n].reshape(shape))

    for n in ("ffn_w_up", "ffn_w_down", "w_out", "w_in"):
        adamw_big(n)

    small_parts = _xchg_wait(small_flight, list(range(8)), new_v["w_in"], "small_grads_wait")
    s_conv_w, s_pw_w, s_fcw, s_d, s_gw, s_128, s_w, s_fb = _sum8_small(
        [p.reshape(N_DEV, -1, p.shape[-1]) for p in small_parts], "sum_small_grads")
    gr["conv_w"] = s_conv_w.reshape(N_LAYER, CONV_K, GW // N_DEV)
    gr["conv_pw_w"] = s_pw_w.reshape(N_LAYER, GW // N_DEV, GW)
    gr["ffn_conv_w"] = s_fcw.reshape(N_LAYER, 3, FF_BLK)
    gr["b_ada"] = s_d[:6 * N_LAYER].reshape(N_LAYER, 6 * D)
    for i, k in enumerate(("g_pre_mix", "g_post_mix", "g_pre_ffn", "g_post_ffn", "g_group")):
        gr[k] = s_d[6 * N_LAYER + 2 * i:6 * N_LAYER + 2 * i + 2]
    for i, k in enumerate(("conv_b", "conv_ln_g", "conv_ln_b", "conv_pw_b", "sgu_ln_g", "sgu_ln_b")):
        gr[k] = s_gw[2 * i:2 * i + 2]
    gr["b_fgate"] = s_128[0:2, :4]
    gr["swa_sinks"] = s_128[2:4, :4]
    gr["sgu_b"] = s_128[8:].reshape(N_LAYER, WIN, 128)[:, :, :4].transpose(0, 2, 1)
    gr["sgu_w"] = s_w.reshape(N_LAYER, WIN, 4, WIN).transpose(0, 2, 1, 3)
    gr["ffn_conv_b"] = s_fb.reshape(N_LAYER, N_DEV * FF_BLK)
    dmod_all = small_parts[3][:, :6 * N_LAYER, :].reshape(N_DEV, N_LAYER, 6 * D)
    ncol = 6 * D // N_DEV
    dmod_cols = lax.dynamic_slice_in_dim(dmod_all, me * ncol, ncol, axis=2).transpose(1, 0, 2)
    gr["w_ada"] = _ada_bwd(c_all, dmod_cols)

    adamw_big("w_ada")
    smalls = [n for n in _WEIGHTS if n not in ("w_ada", "w_in", "w_out", "ffn_w_up", "ffn_w_down")]
    ds, ms, vs = _adamw_small([_view2d(w[n]) for n in smalls], [_view2d(gr[n]) for n in smalls],
                              [_view2d(mom[n]) for n in smalls], [_view2d(var[n]) for n in smalls], "adamw_small")
    for i, n in enumerate(smalls):
        delta[n], new_m[n], new_v[n] = ds[i].reshape(w[n].shape), ms[i].reshape(w[n].shape), vs[i].reshape(w[n].shape)

    return (loss, grad_x, *[gr[n].reshape(w[n].shape) for n in _WEIGHTS], *[delta[n] for n in _WEIGHTS],
            *[new_m[n] for n in _WEIGHTS], *[new_v[n] for n in _WEIGHTS])
```

```python
import functools

import jax
import jax.numpy as jnp
from jax import lax
from jax.experimental import pallas as pl
from jax.experimental.pallas import tpu as pltpu

F32, BF16 = jnp.float32, jnp.bfloat16
SDS = jax.ShapeDtypeStruct
MESH = pl.DeviceIdType.MESH

N_DEV = 8
D = 1024
GW = 256
HD = 64
N_LAYER = 2
ZW = 2432
FF_BLK = 704
FF_NBLK = 4
CONV_K = 31
CONV_HALO = 32
FFN_HALO = 16
EPS = 1e-6
NEG = -1e30
SCALE = HD ** -0.5
VMEM_LIMIT_V7X = 56 * 1024 * 1024
TM = 512
WGRAD_ROWS = 256
TQ = 256
WIN = 128

ADAM_LR, ADAM_B1, ADAM_B2, ADAM_EPS, ADAM_WD, ADAM_STEP = 0.001, 0.9, 0.999, 1e-08, 0.01, 10

Z_FQ, Z_FK, Z_FV, Z_CA, Z_CG, Z_SQ = 0, 1, 2, 3, 4, 5
Z_SK, Z_SV = 12, 13
Z_GU, Z_GV = 7, 8
Z_FG = 18


def _cp(sem=None):
    return pltpu.CompilerParams(dimension_semantics=sem, vmem_limit_bytes=VMEM_LIMIT_V7X)


def _vec(arr3, idx, ngrid):
    w = arr3.shape[-1]
    if ngrid == 1:
        return pl.BlockSpec((None, 1, w), lambda i: (idx, 0, 0))
    return pl.BlockSpec((None, 1, w), lambda i, j: (idx, 0, 0))


def _sigmoid(x):
    return jax.nn.sigmoid(x)


def _silu(x):
    return x * _sigmoid(x)


def _dsilu(x):
    s = _sigmoid(x)
    return s * (1.0 + x * (1.0 - s))


_G0, _G1 = 0.7978845608028654, 0.044715


def _gelu(x):
    return 0.5 * x * (1.0 + jnp.tanh(_G0 * (x + _G1 * x * x * x)))


def _dgelu(x):
    t = jnp.tanh(_G0 * (x + _G1 * x * x * x))
    return 0.5 * (1.0 + t) + 0.5 * x * (1.0 - t * t) * (_G0 * (1.0 + 3.0 * _G1 * x * x))


def _rstd(x):
    return lax.rsqrt(jnp.mean(x * x, axis=-1, keepdims=True) + EPS)


def _rms_bwd(xh, r, t):
    return r * (t - xh * jnp.mean(t * xh, axis=-1, keepdims=True))


def _ln_stats(x):
    mu = jnp.mean(x, axis=-1, keepdims=True)
    xc = x - mu
    rstd = lax.rsqrt(jnp.mean(xc * xc, axis=-1, keepdims=True) + EPS)
    return xc * rstd, rstd


def _ln_bwd(xh, rstd, dxh):
    return rstd * (dxh - jnp.mean(dxh, axis=-1, keepdims=True) - xh * jnp.mean(dxh * xh, axis=-1, keepdims=True))


def _colsum(x):
    return jnp.sum(x, axis=0, keepdims=True)


def _dot(a, b, kind):
    dn = {"nn": (((1,), (0,)), ((), ())), "nt": (((1,), (1,)), ((), ())), "tn": (((0,), (0,)), ((), ()))}[kind]
    return lax.dot_general(a.astype(BF16), b.astype(BF16), dn, preferred_element_type=F32)


def _exchange(arrs, modes, name):
    n = len(arrs)
    outs = [SDS((N_DEV,) + a.shape, a.dtype) if m == "bcast" else SDS(a.shape, a.dtype) for a, m in zip(arrs, modes)]

    def body(*refs):
        ins, dst = refs[:n], refs[n:2 * n]
        send, recv, loc = refs[2 * n:]
        x, y, c = lax.axis_index("x"), lax.axis_index("y"), lax.axis_index("c")
        me = 4 * x + 2 * y + c

        def src(a, j):
            return ins[a] if modes[a] == "bcast" else ins[a].at[j]

        local = [pltpu.make_async_copy(src(a, me), dst[a].at[me], loc.at[a]) for a in range(n)]
        for cp in local:
            cp.start()
        sent, landed = [], []
        for k in (2, 4, 6, 3, 5, 7, 1):
            px = 1 - x if k & 4 else x
            py = 1 - y if k & 2 else y
            pc = 1 - c if k & 1 else c
            peer = 4 * px + 2 * py + pc
            for a in range(n):
                cp = pltpu.make_async_remote_copy(src_ref=src(a, peer), dst_ref=dst[a].at[me], send_sem=send.at[a, k - 1],
                                                  recv_sem=recv.at[a, k - 1], device_id=(px, py, pc), device_id_type=MESH)
                cp.start()
                sent.append(cp)
                landed.append(pltpu.make_async_remote_copy(src_ref=src(a, peer), dst_ref=dst[a].at[peer],
                                                           send_sem=send.at[a, k - 1], recv_sem=recv.at[a, k - 1],
                                                           device_id=(px, py, pc), device_id_type=MESH))
        for cp in landed:
            cp.wait_recv()
        for cp in sent:
            cp.wait_send()
        for cp in local:
            cp.wait()

    hbm = pl.BlockSpec(memory_space=pltpu.HBM)
    return pl.pallas_call(
        body, name=name, out_shape=outs, in_specs=[hbm] * n, out_specs=[hbm] * n,
        scratch_shapes=[pltpu.SemaphoreType.DMA((n, N_DEV - 1)), pltpu.SemaphoreType.DMA((n, N_DEV - 1)),
                        pltpu.SemaphoreType.DMA((n,))],
        compiler_params=pltpu.CompilerParams(has_side_effects=True),
    )(*arrs)


_PEER_ORDER = (2, 4, 6, 3, 5, 7, 1)
_HBM = pl.BlockSpec(memory_space=pltpu.HBM)
_SEM = pl.BlockSpec(memory_space=pltpu.SEMAPHORE)
_EFFECT = pltpu.SideEffectType.DATAFLOW_SIDE_EFFECTING


def _peer(k):
    x, y, c = lax.axis_index("x"), lax.axis_index("y"), lax.axis_index("c")
    px = 1 - x if k & 4 else x
    py = 1 - y if k & 2 else y
    pc = 1 - c if k & 1 else c
    return (px, py, pc), 4 * px + 2 * py + pc


def _my_id():
    return 4 * lax.axis_index("x") + 2 * lax.axis_index("y") + lax.axis_index("c")


def _split_copies(src_ref, land_ref, send, recv, loc, mode):
    me = _my_id()
    pick = (lambda j: src_ref) if mode == "bcast" else (lambda j: src_ref.at[j])
    local = pltpu.make_async_copy(pick(me), land_ref.at[me], loc)
    remote = []
    for k in _PEER_ORDER:
        dev, peer = _peer(k)
        out = pltpu.make_async_remote_copy(src_ref=pick(peer), dst_ref=land_ref.at[me], send_sem=send.at[k - 1],
                                           recv_sem=recv.at[k - 1], device_id=dev, device_id_type=MESH)
        arrive = pltpu.make_async_remote_copy(src_ref=pick(peer), dst_ref=land_ref.at[peer], send_sem=send.at[k - 1],
                                              recv_sem=recv.at[k - 1], device_id=dev, device_id_type=MESH)
        remote.append((out, arrive))
    return local, remote


class _Flight:
    def __init__(self, srcs, lands, sends, recvs, locs, modes, token):
        self.srcs, self.lands, self.sends, self.recvs, self.locs, self.modes, self.token = (
            list(srcs), list(lands), list(sends), list(recvs), list(locs), list(modes), token)


def _xchg_start(arrs, modes, name):
    n = len(arrs)
    lands = [lax.empty((N_DEV,) + a.shape if m == "bcast" else a.shape, a.dtype) for a, m in zip(arrs, modes)]

    def body(*refs):
        srcs, lnds = refs[:n], refs[n:2 * n]
        outs = refs[2 * n:]
        sends, recvs, locs, token = outs[:n], outs[n:2 * n], outs[2 * n:3 * n], outs[5 * n]
        for a in range(n):
            local, remote = _split_copies(srcs[a], lnds[a], sends[a], recvs[a], locs[a], modes[a])
            local.start()
            for out, _ in remote:
                out.start()
        token[...] = jnp.zeros_like(token)

    sem7 = pltpu.SemaphoreType.DMA((N_DEV - 1,))
    res = pl.pallas_call(
        body, name=name,
        out_shape=[sem7] * (2 * n) + [pltpu.SemaphoreType.DMA(())] * n + [pltpu.HBM(a.shape, a.dtype) for a in arrs]
        + [pltpu.HBM(b.shape, b.dtype) for b in lands] + [SDS((8, 128), F32)],
        in_specs=[_HBM] * (2 * n), out_specs=[_SEM] * (3 * n) + [_HBM] * (2 * n) + [pl.BlockSpec(memory_space=pltpu.VMEM)],
        input_output_aliases={i: 3 * n + i for i in range(2 * n)},
        compiler_params=pltpu.CompilerParams(has_side_effects=_EFFECT),
    )(*[pltpu.with_memory_space_constraint(a, pltpu.HBM) for a in arrs],
      *[pltpu.with_memory_space_constraint(b, pltpu.HBM) for b in lands])
    return _Flight(res[3 * n:4 * n], res[4 * n:5 * n], res[:n], res[n:2 * n], res[2 * n:3 * n], modes, res[5 * n])


def _xchg_wait(flight, idx, after, name):
    n = len(idx)
    modes = [flight.modes[i] for i in idx]

    def body(*refs):
        srcs, lnds = refs[:n], refs[n:2 * n]
        sends, recvs, locs = refs[2 * n:3 * n], refs[3 * n:4 * n], refs[4 * n:5 * n]
        for a in range(n):
            local, remote = _split_copies(srcs[a], lnds[a], sends[a], recvs[a], locs[a], modes[a])
            local.wait()
            for _, arrive in remote:
                arrive.wait_send()
                arrive.wait_recv()

    ops = ([flight.srcs[i] for i in idx] + [flight.lands[i] for i in idx] + [flight.sends[i] for i in idx]
           + [flight.recvs[i] for i in idx] + [flight.locs[i] for i in idx])
    res = pl.pallas_call(
        body, name=name, out_shape=[pltpu.HBM(o.shape, o.dtype) for o in ops[:2 * n]],
        in_specs=[_HBM] * (2 * n) + [_SEM] * (3 * n) + [pl.BlockSpec(memory_space=pl.ANY)], out_specs=[_HBM] * (2 * n),
        input_output_aliases={i: i for i in range(2 * n)},
        compiler_params=pltpu.CompilerParams(has_side_effects=_EFFECT),
    )(*ops, after)
    return res[n:]


class _Lazy:
    def __init__(self, fn, pre=None):
        self.fn, self.pre, self.val, self.started = fn, pre, None, False

    def prefetch(self, after):
        token = self.pre(after) if self.pre is not None and not self.started else None
        self.started = True
        return token

    def get(self, after):
        self.prefetch(after)
        if self.val is None:
            self.val = self.fn(after)
        return self.val


_CHIP_PEERS = (2, 4, 6)


def _g2_copies_a(src_ref, land_ref, send, recv, loc):
    me = _my_id()
    local = pltpu.make_async_copy(src_ref, land_ref.at[me], loc)
    remote = []
    for j, k in enumerate(_CHIP_PEERS + (1,)):
        dev, peer = _peer(k)
        out = pltpu.make_async_remote_copy(src_ref=src_ref, dst_ref=land_ref.at[me], send_sem=send.at[j], recv_sem=recv.at[j],
                                           device_id=dev, device_id_type=MESH)
        arrive = pltpu.make_async_remote_copy(src_ref=src_ref, dst_ref=land_ref.at[peer], send_sem=send.at[j],
                                              recv_sem=recv.at[j], device_id=dev, device_id_type=MESH)
        remote.append((out, arrive))
    return local, remote


def _g2_copies_b(land_ref, send, recv):
    sib, _ = _peer(1)
    pairs = []
    for j, k in enumerate(_CHIP_PEERS):
        _, same_core = _peer(k)
        _, other_core = _peer(k | 1)
        out = pltpu.make_async_remote_copy(src_ref=land_ref.at[same_core], dst_ref=land_ref.at[same_core], send_sem=send.at[j],
                                           recv_sem=recv.at[j], device_id=sib, device_id_type=MESH)
        arrive = pltpu.make_async_remote_copy(src_ref=land_ref.at[same_core], dst_ref=land_ref.at[other_core],
                                              send_sem=send.at[j], recv_sem=recv.at[j], device_id=sib, device_id_type=MESH)
        pairs.append((out, arrive))
    return pairs


class _Gather2:
    def __init__(self, srcs, lands, sends, recvs, locs, token):
        self.srcs, self.lands, self.sends, self.recvs, self.locs, self.token = (
            list(srcs), list(lands), list(sends), list(recvs), list(locs), token)
        self.sends_b, self.recvs_b = [None] * len(self.srcs), [None] * len(self.srcs)


def _g2_start(arrs, name):
    n = len(arrs)
    lands = [lax.empty((N_DEV,) + a.shape, a.dtype) for a in arrs]

    def body(*refs):
        srcs, lnds = refs[:n], refs[n:2 * n]
        outs = refs[2 * n:]
        sends, recvs, locs, token = outs[:n], outs[n:2 * n], outs[2 * n:3 * n], outs[5 * n]
        for a in range(n):
            local, remote = _g2_copies_a(srcs[a], lnds[a], sends[a], recvs[a], locs[a])
            local.start()
            for out, _ in remote:
                out.start()
        token[...] = jnp.zeros_like(token)

    sem4 = pltpu.SemaphoreType.DMA((4,))
    res = pl.pallas_call(
        body, name=name,
        out_shape=[sem4] * (2 * n) + [pltpu.SemaphoreType.DMA(())] * n + [pltpu.HBM(a.shape, a.dtype) for a in arrs]
        + [pltpu.HBM(b.shape, b.dtype) for b in lands] + [SDS((8, 128), F32)],
        in_specs=[_HBM] * (2 * n), out_specs=[_SEM] * (3 * n) + [_HBM] * (2 * n) + [pl.BlockSpec(memory_space=pltpu.VMEM)],
        input_output_aliases={i: 3 * n + i for i in range(2 * n)},
        compiler_params=pltpu.CompilerParams(has_side_effects=_EFFECT),
    )(*[pltpu.with_memory_space_constraint(a, pltpu.HBM) for a in arrs],
      *[pltpu.with_memory_space_constraint(b, pltpu.HBM) for b in lands])
    return _Gather2(res[3 * n:4 * n], res[4 * n:5 * n], res[:n], res[n:2 * n], res[2 * n:3 * n], res[5 * n])


def _g2_relay(g, idx, after, name):
    n = len(idx)

    def body(*refs):
        srcs, lnds = refs[:n], refs[n:2 * n]
        sends, recvs, locs = refs[2 * n:3 * n], refs[3 * n:4 * n], refs[4 * n:5 * n]
        outs = refs[5 * n + 1:]
        sends_b, recvs_b = outs[2 * n:3 * n], outs[3 * n:4 * n]
        for a in range(n):
            local, remote = _g2_copies_a(srcs[a], lnds[a], sends[a], recvs[a], locs[a])
            local.wait()
            for _, arrive in remote:
                arrive.wait_send()
                arrive.wait_recv()
        for a in range(n):
            for out, _ in _g2_copies_b(lnds[a], sends_b[a], recvs_b[a]):
                out.start()
        outs[4 * n][...] = jnp.zeros_like(outs[4 * n])

    ops = ([g.srcs[i] for i in idx] + [g.lands[i] for i in idx] + [g.sends[i] for i in idx] + [g.recvs[i] for i in idx]
           + [g.locs[i] for i in idx])
    sem3 = pltpu.SemaphoreType.DMA((3,))
    res = pl.pallas_call(
        body, name=name, out_shape=[pltpu.HBM(o.shape, o.dtype) for o in ops[:2 * n]] + [sem3] * (2 * n) + [SDS((8, 128), F32)],
        in_specs=[_HBM] * (2 * n) + [_SEM] * (3 * n) + [pl.BlockSpec(memory_space=pl.ANY)],
        out_specs=[_HBM] * (2 * n) + [_SEM] * (2 * n) + [pl.BlockSpec(memory_space=pltpu.VMEM)],
        input_output_aliases={i: i for i in range(2 * n)},
        compiler_params=pltpu.CompilerParams(has_side_effects=_EFFECT),
    )(*ops, after)
    for a, i in enumerate(idx):
        g.srcs[i], g.lands[i] = res[a], res[n + a]
        g.sends_b[i], g.recvs_b[i] = res[2 * n + a], res[3 * n + a]
    return res[4 * n]


def _g2_wait(g, idx, after, name):
    n = len(idx)

    def body(*refs):
        lnds, sends_b, recvs_b = refs[:n], refs[n:2 * n], refs[2 * n:3 * n]
        for a in range(n):
            for _, arrive in _g2_copies_b(lnds[a], sends_b[a], recvs_b[a]):
                arrive.wait_send()
                arrive.wait_recv()

    ops = [g.lands[i] for i in idx] + [g.sends_b[i] for i in idx] + [g.recvs_b[i] for i in idx]
    res = pl.pallas_call(
        body, name=name, out_shape=[pltpu.HBM(o.shape, o.dtype) for o in ops[:n]],
        in_specs=[_HBM] * n + [_SEM] * (2 * n) + [pl.BlockSpec(memory_space=pl.ANY)], out_specs=[_HBM] * n,
        input_output_aliases={i: i for i in range(n)},
        compiler_params=pltpu.CompilerParams(has_side_effects=_EFFECT),
    )(*ops, after)
    return list(res)


def _matmul(a, b, kind, out_shape, out_dtype, grid, a_spec, b_spec, o_spec, acc_shape, name):
    nk = grid[2]

    def body(a_ref, b_ref, o_ref, *scratch):
        prod = _dot(a_ref[...], b_ref[...], kind)
        if nk == 1:
            o_ref[...] = prod.astype(out_dtype)
        else:
            acc = scratch[0]
            k = pl.program_id(2)

            @pl.when(k == 0)
            def _():
                acc[...] = prod

            @pl.when(k > 0)
            def _():
                acc[...] += prod

            @pl.when(k == nk - 1)
            def _():
                o_ref[...] = acc[...].astype(out_dtype)

    return pl.pallas_call(
        body, name=name, grid=grid, in_specs=[a_spec, b_spec], out_specs=o_spec, out_shape=SDS(out_shape, out_dtype),
        scratch_shapes=[] if nk == 1 else [pltpu.VMEM(acc_shape, F32)],
        compiler_params=_cp(("parallel", "parallel", "arbitrary")))(a, b)


def _bs(shape, fn):
    return pl.BlockSpec(shape, fn)


def _mm_rows(a, w, kind, n_out, out_dtype, name):
    s, k = a.shape
    tm = min(TM, s)
    return _matmul(a, w, kind, (s, n_out), out_dtype, (s // tm, 1, 1),
                   _bs((tm, k), lambda i, j, kk: (i, 0)), _bs(w.shape, lambda i, j, kk: (0, 0)),
                   _bs((tm, n_out), lambda i, j, kk: (i, 0)), None, name)


def _mm_wgrad(a, dy, out_dtype, name):
    s, k = a.shape
    n = dy.shape[1]
    tko = min(WGRAD_ROWS, k)
    return _matmul(a, dy, "tn", (k, n), out_dtype, (k // tko, 1, 1),
                   _bs((s, tko), lambda i, j, kk: (0, i)), _bs((s, n), lambda i, j, kk: (0, 0)),
                   _bs((tko, n), lambda i, j, kk: (i, 0)), None, name)


def _ada_fwd(c_all, w_ada):
    ncol = w_ada.shape[2]

    def body(c_ref, w_ref, o_ref):
        ca = _silu(c_ref[...])
        ca = jnp.concatenate([ca, jnp.zeros_like(ca)], axis=0)
        o_ref[...] = _dot(ca, w_ref[...], "nn")[:N_DEV, :]

    return pl.pallas_call(
        body, name="ada_fwd", grid=(N_LAYER,),
        in_specs=[pl.BlockSpec((N_DEV, D), lambda l: (0, 0)), pl.BlockSpec((None, D, ncol), lambda l: (l, 0, 0))],
        out_specs=pl.BlockSpec((None, N_DEV, ncol), lambda l: (l, 0, 0)),
        out_shape=SDS((N_LAYER, N_DEV, ncol), F32), compiler_params=_cp(("parallel",)))(c_all, w_ada)


def _ada_finish(m_mine, b_ada):
    def body(m_ref, b_ref, o_ref, t_ref):
        o_ref[...] = m_ref[...] + b_ref[...]
        t_ref[...] = jnp.zeros_like(t_ref)

    return pl.pallas_call(body, name="ada_finish", out_shape=[SDS(b_ada.shape, F32), SDS((8, 128), F32)])(m_mine, b_ada)


def _ada_bwd(c_all, dmod_cols):
    ncol = dmod_cols.shape[2]

    def body(c_ref, d_ref, o_ref):
        ca = _silu(c_ref[...])
        ca = jnp.concatenate([ca, jnp.zeros_like(ca)], axis=0)
        dm = d_ref[...]
        dm = jnp.concatenate([dm, jnp.zeros_like(dm)], axis=0)
        o_ref[...] = _dot(ca, dm, "tn")

    return pl.pallas_call(
        body, name="ada_bwd", grid=(N_LAYER,),
        in_specs=[pl.BlockSpec((N_DEV, D), lambda l: (0, 0)), pl.BlockSpec((None, N_DEV, ncol), lambda l: (l, 0, 0))],
        out_specs=pl.BlockSpec((None, D, ncol), lambda l: (l, 0, 0)),
        out_shape=SDS((N_LAYER, D, ncol), F32), compiler_params=_cp(("parallel",)))(c_all, dmod_cols)


def _rows(s):
    tm = min(TM, s)
    return tm, pl.BlockSpec((tm, D), lambda i: (i, 0))


def _rms_mod(x, g, sc, sh, name):
    s = x.shape[0]
    tm, row = _rows(s)

    def body(x_ref, g_ref, sc_ref, sh_ref, h_ref):
        xf = x_ref[...]
        h_ref[...] = (xf * _rstd(xf) * (g_ref[...] * (1.0 + sc_ref[...])) + sh_ref[...]).astype(BF16)

    return pl.pallas_call(
        body, name=name, grid=(s // tm,), in_specs=[row, _vec(*g, 1), _vec(*sc, 1), _vec(*sh, 1)], out_specs=row,
        out_shape=SDS((s, D), BF16), compiler_params=_cp(("parallel",)))(x, g[0], sc[0], sh[0])


def _post(xres, o, ga, gpost, gn, scn, shn, name):
    s = xres.shape[0]
    tm, row = _rows(s)

    def body(x_ref, o_ref, ga_ref, gp_ref, gn_ref, sc_ref, sh_ref, xn_ref, h_ref):
        of = o_ref[...]
        xn = x_ref[...] + ga_ref[...] * (of * _rstd(of) * gp_ref[...])
        xn_ref[...] = xn
        h_ref[...] = (xn * _rstd(xn) * (gn_ref[...] * (1.0 + sc_ref[...])) + sh_ref[...]).astype(BF16)

    return pl.pallas_call(
        body, name=name, grid=(s // tm,),
        in_specs=[row, row, _vec(*ga, 1), _vec(*gpost, 1), _vec(*gn, 1), _vec(*scn, 1), _vec(*shn, 1)],
        out_specs=[row, row], out_shape=[SDS((s, D), F32), SDS((s, D), BF16)],
        compiler_params=_cp(("parallel",)))(xres, o, ga[0], gpost[0], gn[0], scn[0], shn[0])


def _post_loss(xres, o, ga, gpost, target, name):
    s = xres.shape[0]
    tm, row = _rows(s)

    def body(x_ref, o_ref, ga_ref, gp_ref, t_ref, dy_ref, loss_ref):
        of = o_ref[...]
        err = x_ref[...] + ga_ref[...] * (of * _rstd(of) * gp_ref[...]) - t_ref[...]
        dy_ref[...] = err * (1.0 / D)

        @pl.when(pl.program_id(0) == 0)
        def _():
            loss_ref[...] = jnp.zeros_like(loss_ref)

        loss_ref[...] += jnp.sum(jnp.mean(err * err, axis=-1, keepdims=True), axis=0, keepdims=True) * 0.5

    return pl.pallas_call(
        body, name=name, grid=(s // tm,), in_specs=[row, row, _vec(*ga, 1), _vec(*gpost, 1), row],
        out_specs=[row, pl.BlockSpec((8, 128), lambda i: (0, 0))], out_shape=[SDS((s, D), F32), SDS((8, 128), F32)],
        compiler_params=_cp(("arbitrary",)))(xres, o, ga[0], gpost[0], target)


def _acc(ref, val, first):
    @pl.when(first)
    def _():
        ref[...] = val

    @pl.when(jnp.logical_not(first))
    def _():
        ref[...] += val


def _post_bwd(dxn, o, ga, gpost, name):
    s = dxn.shape[0]
    tm, row = _rows(s)
    vec = pl.BlockSpec((1, D), lambda i: (0, 0))

    def body(d_ref, o_ref, ga_ref, gp_ref, do_ref, dga_ref, dgp_ref):
        of, dx = o_ref[...], d_ref[...]
        r = _rstd(of)
        oh = of * r
        do_ref[...] = _rms_bwd(oh, r, dx * (ga_ref[...] * gp_ref[...])).astype(BF16)
        cs = _colsum(dx * oh)
        first = pl.program_id(0) == 0
        _acc(dga_ref, cs * gp_ref[...], first)
        _acc(dgp_ref, cs * ga_ref[...], first)

    return pl.pallas_call(
        body, name=name, grid=(s // tm,), in_specs=[row, row, _vec(*ga, 1), _vec(*gpost, 1)], out_specs=[row, vec, vec],
        out_shape=[SDS((s, D), BF16), SDS((1, D), F32), SDS((1, D), F32)],
        compiler_params=_cp(("arbitrary",)))(dxn, o, ga[0], gpost[0])


def _pre_bwd(dh, x, dres, g, sc, name):
    s = x.shape[0]
    tm, row = _rows(s)
    vec = pl.BlockSpec((1, D), lambda i: (0, 0))

    def body(dh_ref, x_ref, dr_ref, g_ref, sc_ref, dx_ref, dsh_ref, dsc_ref, dg_ref):
        xf, d = x_ref[...], dh_ref[...]
        r = _rstd(xf)
        xh = xf * r
        dx_ref[...] = dr_ref[...] + _rms_bwd(xh, r, d * (g_ref[...] * (1.0 + sc_ref[...])))
        cs = _colsum(d * xh)
        first = pl.program_id(0) == 0
        _acc(dsh_ref, _colsum(d), first)
        _acc(dsc_ref, cs * g_ref[...], first)
        _acc(dg_ref, cs * (1.0 + sc_ref[...]), first)

    return pl.pallas_call(
        body, name=name, grid=(s // tm,), in_specs=[row, row, row, _vec(*g, 1), _vec(*sc, 1)],
        out_specs=[row, vec, vec, vec],
        out_shape=[SDS((s, D), F32), SDS((1, D), F32), SDS((1, D), F32), SDS((1, D), F32)],
        compiler_params=_cp(("arbitrary",)))(dh, x, dres, g[0], sc[0])


def _gnorm(ys, gg, name):
    s = ys[0].shape[0]
    tm = min(TM, s)
    yb = pl.BlockSpec((tm, GW), lambda i: (i, 0))

    def body(y0, y1, y2, y3, g_ref, o_ref):
        for i, yr in enumerate((y0, y1, y2, y3)):
            y = yr[...]
            o_ref[:, GW * i:GW * (i + 1)] = (y * _rstd(y) * g_ref[:, GW * i:GW * (i + 1)]).astype(BF16)

    return pl.pallas_call(
        body, name=name, grid=(s // tm,), in_specs=[yb] * 4 + [_vec(*gg, 1)], out_specs=pl.BlockSpec((tm, D), lambda i: (i, 0)),
        out_shape=SDS((s, D), BF16), compiler_params=_cp(("parallel",)))(*ys, gg[0])


def _gnorm_bwd(dyn, ys, gg, name):
    s = ys[0].shape[0]
    tm = min(TM, s)
    yb = pl.BlockSpec((tm, GW), lambda i: (i, 0))

    def body(d_ref, y0, y1, y2, y3, g_ref, o0, o1, o2, o3, dg_ref):
        first = pl.program_id(0) == 0
        for i, (yr, orf) in enumerate(zip((y0, y1, y2, y3), (o0, o1, o2, o3))):
            y = yr[...]
            d = d_ref[:, GW * i:GW * (i + 1)]
            r = _rstd(y)
            yh = y * r
            orf[...] = _rms_bwd(yh, r, d * g_ref[:, GW * i:GW * (i + 1)])
            cs = _colsum(d * yh)

            @pl.when(first)
            def _():
                dg_ref[:, GW * i:GW * (i + 1)] = cs

            @pl.when(jnp.logical_not(first))
            def _():
                dg_ref[:, GW * i:GW * (i + 1)] += cs

    return pl.pallas_call(
        body, name=name, grid=(s // tm,), in_specs=[pl.BlockSpec((tm, D), lambda i: (i, 0))] + [yb] * 4 + [_vec(*gg, 1)],
        out_specs=[yb] * 4 + [pl.BlockSpec((1, D), lambda i: (0, 0))],
        out_shape=[SDS((s, GW), F32)] * 4 + [SDS((1, D), F32)], compiler_params=_cp(("arbitrary",)))(dyn, *ys, gg[0])


def _lane_put(acc, col, h):
    lane = lax.broadcasted_iota(jnp.int32, acc.shape, 1)
    return jnp.where(lane == h, col, acc)


def _fgate(z, bf, name):
    s = z.shape[0]

    def body(z_ref, b_ref, fc_ref, fr_ref):
        xg = z_ref[...] + b_ref[...]
        lf = jnp.minimum(xg, 0.0) - jnp.log(1.0 + jnp.exp(-jnp.abs(xg)))
        lane = lax.broadcasted_iota(jnp.int32, lf.shape, 1)
        row = lax.broadcasted_iota(jnp.int32, lf.shape, 0)
        f = jnp.where(lane < 4, lf, 0.0)
        sh = 1
        while sh < s:
            f = f + jnp.where(row >= sh, pltpu.roll(f, sh, 0), 0.0)
            sh *= 2
        fc_ref[...] = f
        fr_ref[...] = f.T[:8, :]

    return pl.pallas_call(
        body, name=name, grid=(1,),
        in_specs=[pl.BlockSpec((s, 128), lambda i: (0, Z_FG)), pl.BlockSpec((1, 128), lambda i: (0, 0))],
        out_specs=[pl.BlockSpec((s, 128), lambda i: (0, 0)), pl.BlockSpec((8, s), lambda i: (0, 0))],
        out_shape=[SDS((s, 128), F32), SDS((8, s), F32)], compiler_params=_cp(("arbitrary",)))(z, bf)


def _fgate_bwd(z, bf, dfrow, dfcol, name):
    s = z.shape[0]

    def body(z_ref, b_ref, d_ref, dc_ref, dz_ref, db_ref):
        d = jnp.concatenate([d_ref[...], jnp.zeros((120, s), F32)], axis=0).T + dc_ref[...]
        row = lax.broadcasted_iota(jnp.int32, d.shape, 0)
        lane = lax.broadcasted_iota(jnp.int32, d.shape, 1)
        sh = 1
        while sh < s:
            d = d + jnp.where(row < s - sh, pltpu.roll(d, s - sh, 0), 0.0)
            sh *= 2
        xg = z_ref[...] + b_ref[...]
        dz = jnp.where(lane < 4, d * _sigmoid(-xg), 0.0)
        dz_ref[...] = dz.astype(BF16)
        db_ref[...] = _colsum(dz)

    return pl.pallas_call(
        body, name=name, grid=(1,),
        in_specs=[pl.BlockSpec((s, 128), lambda i: (0, Z_FG)), pl.BlockSpec((1, 128), lambda i: (0, 0)),
                  pl.BlockSpec((8, s), lambda i: (0, 0)), pl.BlockSpec((s, 128), lambda i: (0, 0))],
        out_specs=[pl.BlockSpec((s, 128), lambda i: (0, 0)), pl.BlockSpec((1, 128), lambda i: (0, 0))],
        out_shape=[SDS((s, 128), BF16), SDS((1, 128), F32)], compiler_params=_cp(("arbitrary",)))(z, bf, dfrow, dfcol)


def _fox_scores(q_ref, k_ref, fc_ref, fr_ref, h, i, nk, tq):
    kw = nk * tq
    q = q_ref[:, HD * h:HD * (h + 1)] * SCALE
    sc = _dot(q, k_ref[0:kw, HD * h:HD * (h + 1)], "nt") + fc_ref[:, h:h + 1] - fr_ref[h:h + 1, 0:kw]
    qpos = i * tq + lax.broadcasted_iota(jnp.int32, (tq, kw), 0)
    kpos = lax.broadcasted_iota(jnp.int32, (tq, kw), 1)
    return q, jnp.where(kpos <= qpos, sc, NEG)


def _fox_fwd(z, fcol, frow, name):
    s = z.shape[0]
    tq = min(TQ, s)
    nc = s // tq

    def body(q_ref, k_ref, v_ref, fc_ref, fr_ref, y_ref, l_ref):
        for n in range(nc):
            @pl.when(pl.program_id(0) == n)
            def _():
                kw = (n + 1) * tq
                lse = jnp.zeros((tq, 128), F32)
                for h in range(4):
                    _, sc = _fox_scores(q_ref, k_ref, fc_ref, fr_ref, h, n, n + 1, tq)
                    m = jnp.max(sc, axis=-1, keepdims=True)
                    p = jnp.exp(sc - m)
                    l = jnp.sum(p, axis=-1, keepdims=True)
                    y_ref[:, HD * h:HD * (h + 1)] = _dot(p, v_ref[0:kw, HD * h:HD * (h + 1)], "nn") / l
                    lse = _lane_put(lse, m + jnp.log(l), h)
                l_ref[...] = lse

    return pl.pallas_call(
        body, name=name, grid=(nc,),
        in_specs=[pl.BlockSpec((tq, GW), lambda i: (i, Z_FQ)), pl.BlockSpec((s, GW), lambda i: (0, Z_FK)),
                  pl.BlockSpec((s, GW), lambda i: (0, Z_FV)), pl.BlockSpec((tq, 128), lambda i: (i, 0)),
                  pl.BlockSpec((8, s), lambda i: (0, 0))],
        out_specs=[pl.BlockSpec((tq, GW), lambda i: (i, 0)), pl.BlockSpec((tq, 128), lambda i: (i, 0))],
        out_shape=[SDS((s, GW), F32), SDS((s, 128), F32)], compiler_params=_cp(("parallel",)))(z, z, z, fcol, frow)


def _fox_bwd(z, fcol, frow, lse, y, dy, name):
    s = z.shape[0]
    tq = min(TQ, s)
    nc = s // tq
    half = max(nc // 2, 1)

    def body(q_ref, k_ref, v_ref, fc_ref, fr_ref, l_ref, y_ref, dy_ref, dq_ref, dk_ref, dv_ref, df_ref, dfq_ref):
        @pl.when(pl.program_id(0) == 0)
        def _():
            dk_ref[...] = jnp.zeros_like(dk_ref)
            dv_ref[...] = jnp.zeros_like(dv_ref)
            df_ref[...] = jnp.zeros_like(df_ref)

        i = pl.program_id(0)
        for cond, nk in ((i < half, half), (i >= half, nc)):
            @pl.when(cond)
            def _():
                kw = nk * tq
                dfq = jnp.zeros((tq, 128), F32)
                for h in range(4):
                    hs = slice(HD * h, HD * (h + 1))
                    q, sc = _fox_scores(q_ref, k_ref, fc_ref, fr_ref, h, i, nk, tq)
                    p = jnp.exp(sc - l_ref[:, h:h + 1])
                    dyh = dy_ref[:, hs]
                    dd = jnp.sum(dyh * y_ref[:, hs], axis=-1, keepdims=True)
                    ds = p * (_dot(dyh, v_ref[0:kw, hs], "nt") - dd)
                    dq_ref[:, hs] = _dot(ds, k_ref[0:kw, hs], "nn") * SCALE
                    dk_ref[0:kw, hs] += _dot(ds, q, "tn")
                    dv_ref[0:kw, hs] += _dot(p, dyh, "tn")
                    df_ref[h:h + 1, 0:kw] -= _colsum(ds)
                    dfq = _lane_put(dfq, jnp.sum(ds, axis=-1, keepdims=True), h)
                dfq_ref[...] = dfq

    tile = lambda w: pl.BlockSpec((tq, w), lambda i: (i, 0))
    full = pl.BlockSpec((s, GW), lambda i: (0, 0))
    rows8 = pl.BlockSpec((8, s), lambda i: (0, 0))
    return pl.pallas_call(
        body, name=name, grid=(nc,),
        in_specs=[pl.BlockSpec((tq, GW), lambda i: (i, Z_FQ)), pl.BlockSpec((s, GW), lambda i: (0, Z_FK)),
                  pl.BlockSpec((s, GW), lambda i: (0, Z_FV)), tile(128), rows8, tile(128), tile(GW), tile(GW)],
        out_specs=[tile(GW), full, full, rows8, tile(128)],
        out_shape=[SDS((s, GW), F32), SDS((s, GW), F32), SDS((s, GW), F32), SDS((8, s), F32), SDS((s, 128), F32)],
        compiler_params=_cp(("arbitrary",)))(z, z, z, fcol, frow, lse, y, dy)


def _swa_block(q_ref, k_ref, v_ref, n):
    qs = pl.multiple_of(n * WIN, WIN)
    ks = pl.multiple_of(jnp.maximum(n - 1, 0) * WIN, WIN)
    qb = q_ref[pl.ds(qs, WIN), :]
    kb = k_ref[pl.ds(ks, 2 * WIN), :]
    vb = v_ref[pl.ds(ks, 2 * WIN), :]
    rows = lax.broadcasted_iota(jnp.int32, (2 * WIN, 2 * WIN), 0) & (WIN - 1)
    dist = (qs + rows) - (ks + lax.broadcasted_iota(jnp.int32, (2 * WIN, 2 * WIN), 1))
    return qs, ks, qb, kb, vb, (dist >= 0) & (dist < WIN)


def _stack2(x, kvh):
    return jnp.concatenate([x[:, HD * (2 * kvh):HD * (2 * kvh + 1)], x[:, HD * (2 * kvh + 1):HD * (2 * kvh + 2)]], axis=0)


def _sink2(sink_ref, kvh):
    top = lax.broadcasted_iota(jnp.int32, (2 * WIN, 1), 0) < WIN
    return jnp.where(top, sink_ref[2 * kvh], sink_ref[2 * kvh + 1])


def _swa_fwd(z, sinks, name):
    s = z.shape[0]

    def body(sink_ref, q_ref, k_ref, v_ref, y_ref, l_ref):
        def step(n, carry):
            qs, ks, qb, kb, vb, valid = _swa_block(q_ref, k_ref, v_ref, n)
            lse = jnp.zeros((WIN, 128), F32)
            for kvh in range(2):
                kv = slice(HD * kvh, HD * (kvh + 1))
                sc = jnp.where(valid, _dot(_stack2(qb, kvh) * SCALE, kb[:, kv], "nt"), NEG)
                sink = _sink2(sink_ref, kvh)
                m = jnp.maximum(jnp.max(sc, axis=-1, keepdims=True), sink)
                p = jnp.exp(sc - m)
                den = jnp.sum(p, axis=-1, keepdims=True) + jnp.exp(sink - m)
                o = _dot(p, vb[:, kv], "nn") / den
                lrow = m + jnp.log(den)
                for j in range(2):
                    h = 2 * kvh + j
                    y_ref[pl.ds(qs, WIN), HD * h:HD * (h + 1)] = o[WIN * j:WIN * (j + 1), :]
                    lse = _lane_put(lse, lrow[WIN * j:WIN * (j + 1), :], h)
            l_ref[pl.ds(qs, WIN), :] = lse
            return carry

        lax.fori_loop(0, s // WIN, step, 0)

    return pl.pallas_call(
        body, name=name, grid=(1,),
        in_specs=[pl.BlockSpec(memory_space=pltpu.SMEM), pl.BlockSpec((s, GW), lambda i: (0, Z_SQ)),
                  pl.BlockSpec((s, 128), lambda i: (0, Z_SK)), pl.BlockSpec((s, 128), lambda i: (0, Z_SV))],
        out_specs=[pl.BlockSpec((s, GW), lambda i: (0, 0)), pl.BlockSpec((s, 128), lambda i: (0, 0))],
        out_shape=[SDS((s, GW), F32), SDS((s, 128), F32)], compiler_params=_cp(("arbitrary",)))(sinks, z, z, z)


def _swa_bwd(z, sinks, lse, y, dy, name):
    s = z.shape[0]

    def body(sink_ref, q_ref, k_ref, v_ref, l_ref, y_ref, dy_ref, dq_ref, dk_ref, dv_ref, dsink_ref):
        dk_ref[...] = jnp.zeros_like(dk_ref)
        dv_ref[...] = jnp.zeros_like(dv_ref)
        dsink_ref[...] = jnp.zeros_like(dsink_ref)

        def step(n, carry):
            qs, ks, qb, kb, vb, valid = _swa_block(q_ref, k_ref, v_ref, n)
            lse_b = l_ref[pl.ds(qs, WIN), :]
            yb = y_ref[pl.ds(qs, WIN), :]
            dyb = dy_ref[pl.ds(qs, WIN), :]
            dsink = jnp.zeros((1, 128), F32)
            for kvh in range(2):
                kv = slice(HD * kvh, HD * (kvh + 1))
                q = _stack2(qb, kvh) * SCALE
                dy2 = _stack2(dyb, kvh)
                sc = jnp.where(valid, _dot(q, kb[:, kv], "nt"), NEG)
                lh = jnp.concatenate([lse_b[:, 2 * kvh:2 * kvh + 1], lse_b[:, 2 * kvh + 1:2 * kvh + 2]], axis=0)
                p = jnp.exp(sc - lh)
                dd = jnp.sum(dy2 * _stack2(yb, kvh), axis=-1, keepdims=True)
                ds = p * (_dot(dy2, vb[:, kv], "nt") - dd)
                dq = _dot(ds, kb[:, kv], "nn") * SCALE
                dk_ref[pl.ds(ks, 2 * WIN), kv] += _dot(ds, q, "tn")
                dv_ref[pl.ds(ks, 2 * WIN), kv] += _dot(p, dy2, "tn")
                dsk = jnp.exp(_sink2(sink_ref, kvh) - lh) * dd
                for j in range(2):
                    h = 2 * kvh + j
                    dq_ref[pl.ds(qs, WIN), HD * h:HD * (h + 1)] = dq[WIN * j:WIN * (j + 1), :]
                    dsink = _lane_put(dsink, dsink[:, h:h + 1] - jnp.sum(dsk[WIN * j:WIN * (j + 1), :], axis=0, keepdims=True), h)
            dsink_ref[...] += dsink
            return carry

        lax.fori_loop(0, s // WIN, step, 0)

    full = lambda w: pl.BlockSpec((s, w), lambda i: (0, 0))
    return pl.pallas_call(
        body, name=name, grid=(1,),
        in_specs=[pl.BlockSpec(memory_space=pltpu.SMEM), pl.BlockSpec((s, GW), lambda i: (0, Z_SQ)),
                  pl.BlockSpec((s, 128), lambda i: (0, Z_SK)), pl.BlockSpec((s, 128), lambda i: (0, Z_SV)),
                  full(128), full(GW), full(GW)],
        out_specs=[full(GW), full(128), full(128), pl.BlockSpec((1, 128), lambda i: (0, 0))],
        out_shape=[SDS((s, GW), F32), SDS((s, 128), F32), SDS((s, 128), F32), SDS((1, 128), F32)],
        compiler_params=_cp(("arbitrary",)))(sinks, z, z, z, lse, y, dy)


def _delayed(win, shift, halo):
    return win[halo:, :] if shift == 0 else pltpu.roll(win, shift, 0)[halo:, :]


def _prev_halo(width, halo, tm, col):
    return pl.BlockSpec((halo, width), lambda i: (jnp.maximum(i * (tm // halo) - 1, 0), col))


def _glu_window(a_ref, g_ref, ah_ref, gh_ref):
    keep = (pl.program_id(0) > 0).astype(F32)
    a = jnp.concatenate([ah_ref[...] * keep, a_ref[...]], axis=0)
    g = jnp.concatenate([gh_ref[...], g_ref[...]], axis=0)
    return a * _sigmoid(g)


def _conv_fwd(z, cw, cb, lg, lb, pw, pb, name):
    s = z.shape[0]
    tm = min(TM, s)

    def body(a_ref, g_ref, ah_ref, gh_ref, w_ref, b_ref, lg_ref, lb_ref, pw_ref, pb_ref, y_ref, hc_ref):
        hg = _glu_window(a_ref, g_ref, ah_ref, gh_ref)
        hc = jnp.zeros((tm, GW), F32) + b_ref[...]
        for k in range(CONV_K):
            hc = hc + w_ref[k:k + 1, :] * _delayed(hg, CONV_K - 1 - k, CONV_HALO)
        hc_ref[...] = hc
        xh, _ = _ln_stats(hc)
        y_ref[...] = _dot(_silu(xh * lg_ref[...] + lb_ref[...]), pw_ref[...], "nn") + pb_ref[...]

    tile = lambda col: pl.BlockSpec((tm, GW), lambda i: (i, col))
    whole = lambda a: pl.BlockSpec(a.shape, lambda i: (0, 0))
    return pl.pallas_call(
        body, name=name, grid=(s // tm,),
        in_specs=[tile(Z_CA), tile(Z_CG), _prev_halo(GW, CONV_HALO, tm, Z_CA), _prev_halo(GW, CONV_HALO, tm, Z_CG),
                  whole(cw), whole(cb), whole(lg), whole(lb), whole(pw), whole(pb)],
        out_specs=[tile(0), tile(0)], out_shape=[SDS((s, GW), F32), SDS((s, GW), F32)],
        compiler_params=_cp(("parallel",)))(z, z, z, z, cw, cb, lg, lb, pw, pb)


def _conv_bwd_a(z, hc, dy, cw, lg, lb, pw, name):
    s = z.shape[0]
    tm = min(TM, s)

    def body(a_ref, g_ref, ah_ref, gh_ref, hc_ref, dy_ref, lg_ref, lb_ref, pw_ref,
             dhc_ref, dpw_ref, dpb_ref, dlg_ref, dlb_ref, dcw_ref, dcb_ref):
        first = pl.program_id(0) == 0
        dy = dy_ref[...]
        xh, rstd = _ln_stats(hc_ref[...])
        hn = xh * lg_ref[...] + lb_ref[...]
        dhn = _dot(dy, pw_ref[...], "nt") * _dsilu(hn)
        dhc = _ln_bwd(xh, rstd, dhn * lg_ref[...])
        dhc_ref[...] = dhc
        _acc(dpw_ref, _dot(_silu(hn), dy, "tn"), first)
        _acc(dpb_ref, _colsum(dy), first)
        _acc(dlg_ref, _colsum(dhn * xh), first)
        _acc(dlb_ref, _colsum(dhn), first)
        _acc(dcb_ref, _colsum(dhc), first)
        hg = _glu_window(a_ref, g_ref, ah_ref, gh_ref)

        @pl.when(first)
        def _():
            dcw_ref[...] = jnp.zeros_like(dcw_ref)

        for k in range(CONV_K):
            dcw_ref[k:k + 1, :] += _colsum(dhc * _delayed(hg, CONV_K - 1 - k, CONV_HALO))

    tile = lambda col: pl.BlockSpec((tm, GW), lambda i: (i, col))
    whole = lambda shape: pl.BlockSpec(shape, lambda i: (0, 0))
    return pl.pallas_call(
        body, name=name, grid=(s // tm,),
        in_specs=[tile(Z_CA), tile(Z_CG), _prev_halo(GW, CONV_HALO, tm, Z_CA), _prev_halo(GW, CONV_HALO, tm, Z_CG),
                  tile(0), tile(0), whole(lg.shape), whole(lb.shape), whole(pw.shape)],
        out_specs=[tile(0), whole((GW, GW)), whole((1, GW)), whole((1, GW)), whole((1, GW)), whole((32, GW)), whole((1, GW))],
        out_shape=[SDS((s, GW), F32), SDS((GW, GW), F32), SDS((1, GW), F32), SDS((1, GW), F32), SDS((1, GW), F32),
                   SDS((32, GW), F32), SDS((1, GW), F32)],
        compiler_params=_cp(("arbitrary",)))(z, z, z, z, hc, dy, lg, lb, pw)


def _conv_bwd_b(z, dhc, cw, name):
    s = z.shape[0]
    tm = min(TM, s)
    nt = s // tm

    def body(a_ref, g_ref, d_ref, dn_ref, w_ref, da_ref, dg_ref):
        keep = (pl.program_id(0) < nt - 1).astype(F32)
        win = jnp.concatenate([d_ref[...], dn_ref[...] * keep], axis=0)
        dhg = jnp.zeros((tm, GW), F32)
        for k in range(CONV_K):
            sh = CONV_K - 1 - k
            dhg = dhg + w_ref[k:k + 1, :] * (win[:tm, :] if sh == 0 else pltpu.roll(win, tm + CONV_HALO - sh, 0)[:tm, :])
        sg = _sigmoid(g_ref[...])
        da_ref[...] = (dhg * sg).astype(BF16)
        dg_ref[...] = (dhg * a_ref[...] * sg * (1.0 - sg)).astype(BF16)

    tile = lambda col: pl.BlockSpec((tm, GW), lambda i: (i, col))
    nxt = pl.BlockSpec((CONV_HALO, GW), lambda i: (jnp.minimum((i + 1) * (tm // CONV_HALO), s // CONV_HALO - 1), 0))
    return pl.pallas_call(
        body, name=name, grid=(nt,),
        in_specs=[tile(Z_CA), tile(Z_CG), tile(0), nxt, pl.BlockSpec(cw.shape, lambda i: (0, 0))],
        out_specs=[tile(0), tile(0)], out_shape=[SDS((s, GW), BF16), SDS((s, GW), BF16)],
        compiler_params=_cp(("parallel",)))(z, z, dhc, dhc, cw)


def _sgu_chunk(zu, zv, lg, lb, wcat, bfull):
    u, v = _gelu(zu), _gelu(zv)
    xh, rstd = _ln_stats(v)
    vn = xh * lg + lb
    lane = lax.shift_right_logical(lax.broadcasted_iota(jnp.int32, (WIN, GW), 1), 6)
    r = jnp.concatenate([jnp.where(lane == g, vn, 0.0) for g in range(4)], axis=0)
    mix = _dot(wcat, r, "nn") + bfull
    return u, xh, rstd, r, mix, lane


def _tril4(w):
    t = lax.broadcasted_iota(jnp.int32, w.shape, 0)
    sidx = lax.broadcasted_iota(jnp.int32, w.shape, 1) & (WIN - 1)
    return jnp.where(sidx <= t, w, 0.0)


def _sgu_fwd(z, lg, lb, wcat, bfull, name):
    s = z.shape[0]
    tm = min(TM, s)

    def body(u_ref, v_ref, lg_ref, lb_ref, w_ref, b_ref, y_ref):
        w = _tril4(w_ref[...])
        for n in range(tm // WIN):
            rows = slice(WIN * n, WIN * (n + 1))
            u, _, _, _, mix, _ = _sgu_chunk(u_ref[rows, :], v_ref[rows, :], lg_ref[...], lb_ref[...], w, b_ref[...])
            y_ref[rows, :] = u * mix

    tile = lambda col: pl.BlockSpec((tm, GW), lambda i: (i, col))
    whole = lambda a: pl.BlockSpec(a.shape, lambda i: (0, 0))
    return pl.pallas_call(
        body, name=name, grid=(s // tm,), in_specs=[tile(Z_GU), tile(Z_GV), whole(lg), whole(lb), whole(wcat), whole(bfull)],
        out_specs=tile(0), out_shape=SDS((s, GW), F32), compiler_params=_cp(("parallel",)))(z, z, lg, lb, wcat, bfull)


def _sgu_bwd(z, dy, lg, lb, wcat, bfull, name):
    s = z.shape[0]
    tm = min(TM, s)

    def body(u_ref, v_ref, dy_ref, lg_ref, lb_ref, w_ref, b_ref, du_ref, dv_ref, dw_ref, db_ref, dlg_ref, dlb_ref):
        first = pl.program_id(0) == 0
        w = _tril4(w_ref[...])
        wt = w.T
        dw = jnp.zeros((WIN, 4 * WIN), F32)
        db = jnp.zeros((WIN, 128), F32)
        dlg = jnp.zeros((1, GW), F32)
        dlb = jnp.zeros((1, GW), F32)
        for n in range(tm // WIN):
            rows = slice(WIN * n, WIN * (n + 1))
            zu, zv, dout = u_ref[rows, :], v_ref[rows, :], dy_ref[rows, :]
            u, xh, rstd, r, mix, lane = _sgu_chunk(zu, zv, lg_ref[...], lb_ref[...], w, b_ref[...])
            dmix = dout * u
            du_ref[rows, :] = (dout * mix * _dgelu(zu)).astype(BF16)
            dw = dw + _dot(dmix, r, "nt")
            for g in range(4):
                db = _lane_put(db, db[:, g:g + 1] + jnp.sum(dmix[:, HD * g:HD * (g + 1)], axis=1, keepdims=True), g)
            dr = _dot(wt, dmix, "nn")
            dvn = jnp.zeros((WIN, GW), F32)
            for g in range(4):
                dvn = dvn + jnp.where(lane == g, dr[WIN * g:WIN * (g + 1), :], 0.0)
            dlg = dlg + _colsum(dvn * xh)
            dlb = dlb + _colsum(dvn)
            dv_ref[rows, :] = (_ln_bwd(xh, rstd, dvn * lg_ref[...]) * _dgelu(zv)).astype(BF16)
        _acc(dw_ref, _tril4(dw), first)
        _acc(db_ref, db, first)
        _acc(dlg_ref, dlg, first)
        _acc(dlb_ref, dlb, first)

    tile = lambda col: pl.BlockSpec((tm, GW), lambda i: (i, col))
    whole = lambda shape: pl.BlockSpec(shape, lambda i: (0, 0))
    return pl.pallas_call(
        body, name=name, grid=(s // tm,),
        in_specs=[tile(Z_GU), tile(Z_GV), tile(0), whole(lg.shape), whole(lb.shape), whole(wcat.shape), whole(bfull.shape)],
        out_specs=[tile(0), tile(0), whole((WIN, 4 * WIN)), whole((WIN, 128)), whole((1, GW)), whole((1, GW))],
        out_shape=[SDS((s, GW), BF16), SDS((s, GW), BF16), SDS((WIN, 4 * WIN), F32), SDS((WIN, 128), F32),
                   SDS((1, GW), F32), SDS((1, GW), F32)],
        compiler_params=_cp(("arbitrary",)))(z, z, dy, lg, lb, wcat, bfull)


def _conv3(win, w, b):
    return (w[2:3, :] * win[FFN_HALO:, :] + w[1:2, :] * pltpu.roll(win, 1, 0)[FFN_HALO:, :]
            + w[0:1, :] * pltpu.roll(win, 2, 0)[FFN_HALO:, :] + b)


def _ffn_specs(s, tm):
    main = pl.BlockSpec((2, None, tm, FF_BLK), lambda j, i: (0, j, i, 0))
    prev = pl.BlockSpec((2, None, FFN_HALO, FF_BLK), lambda j, i: (0, j, jnp.maximum(i * (tm // FFN_HALO) - 1, 0), 0))
    nxt = pl.BlockSpec((2, None, FFN_HALO, FF_BLK),
                       lambda j, i: (0, j, jnp.minimum((i + 1) * (tm // FFN_HALO), s // FFN_HALO - 1), 0))
    wsp = pl.BlockSpec((2, None, 3, FF_BLK), lambda j, i: (0, j, 0, 0))
    bsp = pl.BlockSpec((2, None, 1, FF_BLK), lambda j, i: (0, j, 0, 0))
    return main, prev, nxt, wsp, bsp


def _ffn_act(u4, w4, b4, name):
    s = u4.shape[2]
    tm = min(TM, s)
    main, prev, _, wsp, bsp = _ffn_specs(s, tm)

    def body(u_ref, uh_ref, w_ref, b_ref, o_ref):
        keep = (pl.program_id(1) > 0).astype(F32)
        gw, vw = [jnp.concatenate([uh_ref[p].astype(F32) * keep, u_ref[p].astype(F32)], axis=0) for p in range(2)]
        o_ref[...] = (_silu(_conv3(gw, w_ref[0], b_ref[0])) * _conv3(vw, w_ref[1], b_ref[1])).astype(BF16)

    return pl.pallas_call(
        body, name=name, grid=(FF_NBLK, s // tm), in_specs=[main, prev, wsp, bsp],
        out_specs=pl.BlockSpec((None, tm, FF_BLK), lambda j, i: (j, i, 0)), out_shape=SDS((FF_NBLK, s, FF_BLK), BF16),
        compiler_params=_cp(("parallel", "parallel")))(u4, u4, w4, b4)


def _ffn_bwd(u4, dact, w4, b4, name):
    s = u4.shape[2]
    tm = min(TM, s)
    nt = s // tm
    main, prev, nxt, wsp, bsp = _ffn_specs(s, tm)
    dmain = pl.BlockSpec((None, tm, FF_BLK), lambda j, i: (j, i, 0))
    dnext = pl.BlockSpec((None, FFN_HALO, FF_BLK), lambda j, i: (j, jnp.minimum((i + 1) * (tm // FFN_HALO), s // FFN_HALO - 1), 0))
    ext = tm + FFN_HALO

    def body(u_ref, up_ref, un_ref, d_ref, dn_ref, w_ref, b_ref, du_ref, dw_ref, db_ref):
        i = pl.program_id(1)
        first = i == 0
        keep_prev = (i > 0).astype(F32)
        keep_next = (i < nt - 1).astype(F32)
        wins = [jnp.concatenate([up_ref[p].astype(F32) * keep_prev, u_ref[p].astype(F32), un_ref[p].astype(F32)], axis=0)
                for p in range(2)]
        gc = _conv3(wins[0], w_ref[0], b_ref[0])
        vc = _conv3(wins[1], w_ref[1], b_ref[1])
        d = jnp.concatenate([d_ref[...].astype(F32), dn_ref[...].astype(F32) * keep_next], axis=0)
        sg = _sigmoid(gc)
        duc = (d * vc * (sg * (1.0 + gc * (1.0 - sg))), d * (gc * sg))
        for p in range(2):
            w = w_ref[p]
            du_ref[p] = (w[2:3, :] * duc[p][:tm, :] + w[1:2, :] * pltpu.roll(duc[p], ext - 1, 0)[:tm, :]
                         + w[0:1, :] * pltpu.roll(duc[p], ext - 2, 0)[:tm, :]).astype(BF16)
            own = duc[p][:tm, :]
            taps = [_colsum(own * (wins[p] if k == 2 else pltpu.roll(wins[p], 2 - k, 0))[FFN_HALO:FFN_HALO + tm, :])
                    for k in range(3)]

            @pl.when(first)
            def _():
                db_ref[p] = _colsum(own)
                for k in range(3):
                    dw_ref[p, k:k + 1, :] = taps[k]

            @pl.when(jnp.logical_not(first))
            def _():
                db_ref[p] += _colsum(own)
                for k in range(3):
                    dw_ref[p, k:k + 1, :] += taps[k]

    return pl.pallas_call(
        body, name=name, grid=(FF_NBLK, nt), in_specs=[main, prev, nxt, dmain, dnext, wsp, bsp], out_specs=[main, wsp, bsp],
        out_shape=[SDS(u4.shape, BF16), SDS((2, FF_NBLK, 3, FF_BLK), F32), SDS((2, FF_NBLK, 1, FF_BLK), F32)],
        compiler_params=_cp(("parallel", "arbitrary")))(u4, u4, u4, dact, dact, w4, b4)


def _sum8(parts, name):
    _, r, c = parts.shape
    tr = r
    for cand in (512, 256, 128, 64, 32, 16):
        if r % cand == 0 and r > cand:
            tr = cand
            break

    def body(p_ref, o_ref):
        acc = p_ref[0].astype(F32)
        for j in range(1, N_DEV):
            acc = acc + p_ref[j].astype(F32)
        o_ref[...] = acc

    return pl.pallas_call(
        body, name=name, grid=(r // tr,), in_specs=[pl.BlockSpec((N_DEV, tr, c), lambda i: (0, i, 0))],
        out_specs=pl.BlockSpec((tr, c), lambda i: (i, 0)), out_shape=SDS((r, c), F32),
        compiler_params=_cp(("parallel",)))(parts)


def _sum8_small(parts, name):
    n = len(parts)

    def body(*refs):
        for p_ref, o_ref in zip(refs[:n], refs[n:]):
            acc = p_ref[0]
            for j in range(1, N_DEV):
                acc = acc + p_ref[j]
            o_ref[...] = acc

    return pl.pallas_call(body, name=name, out_shape=[SDS(p.shape[1:], F32) for p in parts], compiler_params=_cp())(*parts)


def _adamw_math(w, g, m, v):
    m = ADAM_B1 * m + (1.0 - ADAM_B1) * g
    v = ADAM_B2 * v + (1.0 - ADAM_B2) * (g * g)
    m_hat = m / (1.0 - ADAM_B1 ** ADAM_STEP)
    v_hat = v / (1.0 - ADAM_B2 ** ADAM_STEP)
    return -ADAM_LR * (m_hat / (jnp.sqrt(v_hat) + ADAM_EPS) + ADAM_WD * w), m, v


def _adamw(w, g, m, v, name):
    r, c = w.shape
    tr = r
    for cand in (256, 128, 64):
        if r % cand == 0 and r > cand:
            tr = cand
            break

    def body(w_ref, g_ref, m_ref, v_ref, d_ref, mo_ref, vo_ref):
        d_ref[...], mo_ref[...], vo_ref[...] = _adamw_math(w_ref[...], g_ref[...], m_ref[...], v_ref[...])

    blk = pl.BlockSpec((tr, c), lambda i: (i, 0))
    return pl.pallas_call(body, name=name, grid=(r // tr,), in_specs=[blk] * 4, out_specs=[blk] * 3,
                          out_shape=[SDS((r, c), F32)] * 3, compiler_params=_cp(("parallel",)))(w, g, m, v)


def _adamw_small(ws, gs, ms, vs, name):
    n = len(ws)

    def body(*refs):
        ins, outs = refs[:4 * n], refs[4 * n:]
        for i in range(n):
            d, m, v = _adamw_math(ins[i][...], ins[n + i][...], ins[2 * n + i][...], ins[3 * n + i][...])
            outs[i][...], outs[n + i][...], outs[2 * n + i][...] = d, m, v

    shapes = [SDS(w.shape, F32) for w in ws]
    res = pl.pallas_call(body, name=name, out_shape=shapes * 3, compiler_params=_cp())(*ws, *gs, *ms, *vs)
    return res[:n], res[n:2 * n], res[2 * n:]


def _perm_in(w):
    pad = jnp.zeros(w.shape[:-1] + (ZW - 2308,), w.dtype)
    return jnp.concatenate([w[..., :768], w[..., 772:], w[..., 768:772], pad], axis=-1)


def _unperm_in(g):
    return jnp.concatenate([g[..., :768], g[..., 2304:2308], g[..., 768:2304]], axis=-1)


def _wcat(sgu_w):
    return sgu_w.transpose(1, 0, 2).reshape(WIN, 4 * WIN)


def _layer_fwd(l, x, h1, mod, p, wg, last, target, nxt, w_in_next):
    s = x.shape[0]
    tag = f"_l{l}"
    mrow = lambda k: (mod, 6 * l + k)
    z = _mm_rows(h1, wg["w_in"].get(h1), "nn", ZW, F32, "mm_z" + tag)
    fcol, frow = _fgate(z, _tie(p["bf"], wg["w_out"].prefetch(z)), "fgate" + tag)
    y_fox, lse_fox = _fox_fwd(z, fcol, frow, "fox_fwd" + tag)
    y_conv, hc = _conv_fwd(z, wg["conv_w"], p["conv_b"], p["conv_ln_g"], p["conv_ln_b"], wg["conv_pw_w"], p["conv_pw_b"],
                           "conv_fwd" + tag)
    y_swa, lse_swa = _swa_fwd(z, p["sinks"], "swa_fwd" + tag)
    y_sgu = _sgu_fwd(z, p["sgu_ln_g"], p["sgu_ln_b"], p["wcat"], p["bfull"], "sgu_fwd" + tag)
    ys = (y_fox, y_conv, y_swa, y_sgu)
    yn = _gnorm(ys, (p["g_group"], l), "gnorm" + tag)
    tok = wg["w_up"].prefetch(yn)
    o = _mm_rows(yn, wg["w_out"].get(yn), "nn", D, F32, "mm_o" + tag)
    x1, h2 = _post(x, o, mrow(2), (_tie(p["g_post_mix"], tok), l), (p["g_pre_ffn"], l), mrow(4), mrow(3), "post_mix" + tag)
    tok = wg["w_down"].prefetch(h2)
    u = _matmul(h2, wg["w_up"].get(h2), "nn", (N_DEV, s, FF_BLK), BF16, (N_DEV, 1, 1),
                _bs((s, D), lambda j, i, k: (0, 0)), _bs((None, D, FF_BLK), lambda j, i, k: (j, 0, 0)),
                _bs((None, s, FF_BLK), lambda j, i, k: (j, 0, 0)), None, "mm_u" + tag)
    u4 = u.reshape(2, FF_NBLK, s, FF_BLK)
    act = _ffn_act(u4, wg["ffn_conv_w"], _tie(p["ffn_conv_b"], tok), "ffn_act" + tag)
    g_post = (_tie(p["g_post_ffn"], None if w_in_next is None else w_in_next.prefetch(act)), l)
    f = _matmul(act, wg["w_down"].get(act), "nn", (s, D), F32, (1, 1, FF_NBLK),
                _bs((None, s, FF_BLK), lambda i, j, k: (k, 0, 0)), _bs((FF_BLK, D), lambda i, j, k: (k, 0)),
                _bs((s, D), lambda i, j, k: (0, 0)), (s, D), "mm_f" + tag)
    if last:
        out = _post_loss(x1, f, mrow(5), g_post, target, "post_loss")
    else:
        out = _post(x1, f, mrow(5), g_post, *nxt, "post_ffn" + tag)
    saved = dict(x=x, h1=h1, z=z, fcol=fcol, frow=frow, lse_fox=lse_fox, hc=hc, lse_swa=lse_swa, ys=ys, yn=yn, o=o, x1=x1,
                 h2=h2, u4=u4, act=act, f=f)
    return out, saved


def _tie(a, token):
    return a if token is None else a + token[0, 0]


def _layer_bwd(l, dx2, sv, mod, p, wg, emit):
    s = dx2.shape[0]
    tm = min(TM, s)
    tag = f"_l{l}"
    mrow = lambda k: (mod, 6 * l + k)
    g = {}
    df, g["ga2"], g["g_post_ffn"] = _post_bwd(dx2, sv["f"], mrow(5), (p["g_post_ffn"], l), "post_ffn_bwd" + tag)
    dact = _matmul(df, wg["w_down"].get(None), "nt", (FF_NBLK, s, FF_BLK), BF16, (FF_NBLK, 1, 1),
                   _bs((s, D), lambda j, i, k: (0, 0)), _bs((FF_BLK, D), lambda j, i, k: (j, 0)),
                   _bs((None, s, FF_BLK), lambda j, i, k: (j, 0, 0)), None, "mm_dact" + tag)
    tok = emit("w_down", _matmul(sv["act"], df, "tn", (FF_NBLK * FF_BLK, D), BF16, (FF_NBLK, 1, 1),
                                 _bs((None, s, FF_BLK), lambda j, i, k: (j, 0, 0)), _bs((s, D), lambda j, i, k: (0, 0)),
                                 _bs((FF_BLK, D), lambda j, i, k: (j, 0)), None, "mm_dwdown" + tag))
    du, g["ffn_conv_w"], g["ffn_conv_b"] = _ffn_bwd(sv["u4"], dact, wg["ffn_conv_w"], _tie(p["ffn_conv_b"], tok),
                                                    "ffn_bwd" + tag)
    du = du.reshape(N_DEV, s, FF_BLK)
    dh2 = _matmul(du, wg["w_up"].get(None), "nt", (s, D), F32, (1, 1, N_DEV),
                  _bs((None, s, FF_BLK), lambda i, j, k: (k, 0, 0)), _bs((None, D, FF_BLK), lambda i, j, k: (k, 0, 0)),
                  _bs((s, D), lambda i, j, k: (0, 0)), (s, D), "mm_dh2" + tag)
    tok = emit("w_up", _matmul(du, sv["h2"], "tn", (N_DEV, FF_BLK, D), BF16, (N_DEV, 1, 1),
                               _bs((None, s, FF_BLK), lambda j, i, k: (j, 0, 0)), _bs((s, D), lambda j, i, k: (0, 0)),
                               _bs((None, FF_BLK, D), lambda j, i, k: (j, 0, 0)), None, "mm_dwup" + tag))
    dx1, g["sh2"], g["sc2"], g["g_pre_ffn"] = _pre_bwd(dh2, sv["x1"], dx2, (_tie(p["g_pre_ffn"], tok), l), mrow(4),
                                                       "pre_ffn_bwd" + tag)
    do, g["ga1"], g["g_post_mix"] = _post_bwd(dx1, sv["o"], mrow(2), (p["g_post_mix"], l), "post_mix_bwd" + tag)
    dyn = _mm_rows(do, wg["w_out"].get(None), "nt", D, F32, "mm_dyn" + tag)
    tok = emit("w_out", _mm_wgrad(sv["yn"], do, BF16, "mm_dwout" + tag))
    dy_fox, dy_conv, dy_swa, dy_sgu, g["g_group"] = _gnorm_bwd(dyn, sv["ys"], (_tie(p["g_group"], tok), l), "gnorm_bwd" + tag)
    z = sv["z"]
    dq_f, dk_f, dv_f, dfrow, dfcol = _fox_bwd(z, sv["fcol"], sv["frow"], sv["lse_fox"], sv["ys"][0], dy_fox, "fox_bwd" + tag)
    dgate, g["bf"] = _fgate_bwd(z, p["bf"], dfrow, dfcol, "fgate_bwd" + tag)
    dhc, g["conv_pw_w"], g["conv_pw_b"], g["conv_ln_g"], g["conv_ln_b"], g["conv_w"], g["conv_b"] = _conv_bwd_a(
        z, sv["hc"], dy_conv, wg["conv_w"], p["conv_ln_g"], p["conv_ln_b"], wg["conv_pw_w"], "conv_bwd_a" + tag)
    da_c, dg_c = _conv_bwd_b(z, dhc, wg["conv_w"], "conv_bwd_b" + tag)
    dq_s, dk_s, dv_s, g["sinks"] = _swa_bwd(z, p["sinks"], sv["lse_swa"], sv["ys"][2], dy_swa, "swa_bwd" + tag)
    du_g, dv_g, g["wcat"], g["sgu_bcol"], g["sgu_ln_g"], g["sgu_ln_b"] = _sgu_bwd(
        z, dy_sgu, p["sgu_ln_g"], p["sgu_ln_b"], p["wcat"], p["bfull"], "sgu_bwd" + tag)
    dz = jnp.concatenate([dq_f.astype(BF16), dk_f.astype(BF16), dv_f.astype(BF16), da_c, dg_c, dq_s.astype(BF16), dk_s.astype(BF16),
                          dv_s.astype(BF16), du_g, dv_g, dgate], axis=1)
    dh1 = _mm_rows(dz, wg["w_in"].get(None), "nt", D, F32, "mm_dh1" + tag)
    tok = emit("w_in", _mm_wgrad(sv["h1"], dz, BF16, "mm_dwin" + tag))
    dx, g["sh1"], g["sc1"], g["g_pre_mix"] = _pre_bwd(dh1, sv["x"], dx1, (_tie(p["g_pre_mix"], tok), l), mrow(1),
                                                      "pre_mix_bwd" + tag)
    return dx, g


def _layer_params(l, small, conv_w_full, conv_pw_full, ffn_conv_w_full):
    bf = jnp.pad(small["b_fgate"][l][None, :], ((0, 0), (0, 124)))
    p = dict(
        bf=bf, conv_b=small["conv_b"][l][None], conv_ln_g=small["conv_ln_g"][l][None], conv_ln_b=small["conv_ln_b"][l][None],
        conv_pw_b=small["conv_pw_b"][l][None], sinks=small["swa_sinks"][l], sgu_ln_g=small["sgu_ln_g"][l][None],
        sgu_ln_b=small["sgu_ln_b"][l][None], wcat=_wcat(small["sgu_w"][l]),
        bfull=jnp.repeat(small["sgu_b"][l].T, HD, axis=1),
        ffn_conv_b=small["ffn_conv_b"][l].reshape(2, FF_NBLK, 1, FF_BLK),
        g_group=small["g_group"].reshape(N_LAYER, 1, D), g_post_mix=small["g_post_mix"].reshape(N_LAYER, 1, D),
        g_pre_ffn=small["g_pre_ffn"].reshape(N_LAYER, 1, D), g_post_ffn=small["g_post_ffn"].reshape(N_LAYER, 1, D),
        g_pre_mix=small["g_pre_mix"].reshape(N_LAYER, 1, D))
    wsmall = dict(conv_w=conv_w_full[l], conv_pw_w=conv_pw_full[l].astype(BF16),
                  ffn_conv_w=ffn_conv_w_full[l].reshape(3, 2, FF_NBLK, FF_BLK).transpose(1, 2, 0, 3))
    return p, wsmall


def _local_step(x, target, mod, small, wbig, conv_w_full, conv_pw_full, ffn_conv_w_full, emit):
    ps, wgs = [], []
    for l in range(N_LAYER):
        p, wsmall = _layer_params(l, small, conv_w_full, conv_pw_full, ffn_conv_w_full)
        ps.append(p)
        wgs.append({**wbig[l], **wsmall})
    h = _rms_mod(x, (ps[0]["g_pre_mix"], 0), (mod, 1), (mod, 0), "rms_mod_l0")
    saved = []
    for l in range(N_LAYER):
        last = l == N_LAYER - 1
        nxt = None if last else ((ps[l]["g_pre_mix"], l + 1), (mod, 6 * (l + 1) + 1), (mod, 6 * (l + 1)))
        out, sv = _layer_fwd(l, x, h, mod, ps[l], wgs[l], last, target, nxt, None if last else wgs[l + 1]["w_in"])
        saved.append(sv)
        if not last:
            x, h = out
    dx, loss = out
    grads = [None] * N_LAYER
    for l in reversed(range(N_LAYER)):
        dx, grads[l] = _layer_bwd(l, dx, saved[l], mod, ps[l], wgs[l], functools.partial(emit, l))
    return loss, dx, grads


_SMALL = ("b_ada", "g_pre_mix", "g_post_mix", "g_pre_ffn", "g_post_ffn", "b_fgate", "conv_b", "conv_ln_g", "conv_ln_b",
          "conv_pw_b", "swa_sinks", "sgu_ln_g", "sgu_ln_b", "sgu_w", "sgu_b", "g_group", "ffn_conv_b")
_WEIGHTS = ("w_ada", "b_ada", "g_pre_mix", "g_post_mix", "g_pre_ffn", "g_post_ffn", "w_in", "b_fgate", "conv_w", "conv_b",
            "conv_ln_g", "conv_ln_b", "conv_pw_w", "conv_pw_b", "swa_sinks", "sgu_ln_g", "sgu_ln_b", "sgu_w", "sgu_b",
            "g_group", "w_out", "ffn_w_up", "ffn_conv_w", "ffn_conv_b", "ffn_w_down")


def _pad_rows(a, mult):
    r = (-a.shape[0]) % mult
    return a if r == 0 else jnp.concatenate([a, jnp.zeros((r,) + a.shape[1:], a.dtype)], axis=0)


def _view2d(a):
    if a.ndim == 2:
        return a
    return a.reshape(-1, a.shape[-1])


def kernel(x, c, w_ada, b_ada, g_pre_mix, g_post_mix, g_pre_ffn, g_post_ffn, w_in, b_fgate, conv_w, conv_b, conv_ln_g, conv_ln_b, conv_pw_w, conv_pw_b, swa_sinks, sgu_ln_g, sgu_ln_b, sgu_w, sgu_b, g_group, w_out, ffn_w_up, ffn_conv_w, ffn_conv_b, ffn_w_down, loss_target, m_w_ada, m_b_ada, m_g_pre_mix, m_g_post_mix, m_g_pre_ffn, m_g_post_ffn, m_w_in, m_b_fgate, m_conv_w, m_conv_b, m_conv_ln_g, m_conv_ln_b, m_conv_pw_w, m_conv_pw_b, m_swa_sinks, m_sgu_ln_g, m_sgu_ln_b, m_sgu_w, m_sgu_b, m_g_group, m_w_out, m_ffn_w_up, m_ffn_conv_w, m_ffn_conv_b, m_ffn_w_down, v_w_ada, v_b_ada, v_g_pre_mix, v_g_post_mix, v_g_pre_ffn, v_g_post_ffn, v_w_in, v_b_fgate, v_conv_w, v_conv_b, v_conv_ln_g, v_conv_ln_b, v_conv_pw_w, v_conv_pw_b, v_swa_sinks, v_sgu_ln_g, v_sgu_ln_b, v_sgu_w, v_sgu_b, v_g_group, v_w_out, v_ffn_w_up, v_ffn_conv_w, v_ffn_conv_b, v_ffn_w_down):
    env = dict(locals())
    w = {n: env[n] for n in _WEIGHTS}
    mom = {n: env["m_" + n] for n in _WEIGHTS}
    var = {n: env["v_" + n] for n in _WEIGHTS}
    me = 4 * lax.axis_index("x") + 2 * lax.axis_index("y") + lax.axis_index("c")
    x2, target = x[0], loss_target[0]

    (c_all,) = _exchange([c], ["bcast"], "gather_c")
    c_all = c_all.reshape(N_DEV, D)
    (m_all,) = _exchange([_ada_fwd(c_all, w_ada)], ["bcast"], "gather_mod")
    m_mine = lax.dynamic_index_in_dim(m_all, me, axis=2, keepdims=False)
    mod, mod_token = _ada_finish(m_mine.transpose(1, 0, 2).reshape(N_LAYER, 6 * D), b_ada)
    mod = mod.reshape(6 * N_LAYER, 1, D)

    shards = [_tie(conv_w, mod_token), conv_pw_w, ffn_conv_w]
    for l in range(N_LAYER):
        shards += [_perm_in(w_in[l]).astype(BF16), w_out[l].astype(BF16), ffn_w_up[l].astype(BF16), ffn_w_down[l].astype(BF16)]
    gather = _g2_start(shards, "gather_weights_start")
    mod = _tie(mod, gather.token)
    _g2_relay(gather, [0, 1, 2, 3], mod, "gather_relay_first")
    g_cw, g_pw, g_fcw = _g2_wait(gather, [0, 1, 2], mod, "gather_small_wait")
    conv_w_full = g_cw.transpose(1, 2, 0, 3).reshape(N_LAYER, CONV_K, GW)
    conv_pw_full = g_pw.transpose(1, 0, 2, 3).reshape(N_LAYER, GW, GW)
    ffn_conv_w_full = g_fcw.transpose(1, 2, 0, 3).reshape(N_LAYER, 3, N_DEV * FF_BLK)

    def lazy(i, shape, key):
        pre = None if i == 3 else (lambda after: _g2_relay(gather, [i], after, "relay_" + key))
        return _Lazy(lambda after: _g2_wait(gather, [i], after, "wait_" + key)[0].reshape(shape), pre)

    wbig = [dict(w_in=lazy(3 + 4 * l, (D, ZW), f"w_in_l{l}"), w_out=lazy(4 + 4 * l, (D, D), f"w_out_l{l}"),
                 w_up=lazy(5 + 4 * l, (N_DEV, D, FF_BLK), f"w_up_l{l}"),
                 w_down=lazy(6 + 4 * l, (FF_NBLK * FF_BLK, D), f"w_down_l{l}")) for l in range(N_LAYER)]

    grad_flights = []

    def emit(l, key, arr):
        fl = _xchg_start([arr.reshape(N_DEV, -1, arr.shape[-1])], ["a2a"], f"grad_start_{key}_l{l}")
        grad_flights.append(((l, key), fl))
        return fl.token

    small = {n: w[n] for n in _SMALL}
    loss8, dx, grads = _local_step(x2, target, mod, small, wbig, conv_w_full, conv_pw_full, ffn_conv_w_full, emit)
    loss = lax.psum(loss8[0, 0], ("x", "y", "c"))
    grad_x = dx[None]


    st = lambda key: jnp.stack([grads[l][key] for l in range(N_LAYER)])
    d_conv_w = st("conv_w")[:, :CONV_K, :].reshape(N_LAYER, CONV_K, N_DEV, GW // N_DEV).transpose(2, 0, 1, 3)
    d_pw_w = st("conv_pw_w").reshape(N_LAYER, N_DEV, GW // N_DEV, GW).transpose(1, 0, 2, 3)
    d_fcw = st("ffn_conv_w").reshape(N_LAYER, N_DEV, 3, FF_BLK).transpose(1, 0, 2, 3)
    rows_d = _pad_rows(jnp.concatenate(
        [grads[l][k] for l in range(N_LAYER) for k in ("sh1", "sc1", "ga1", "sh2", "sc2", "ga2")]
        + [grads[l][k] for k in ("g_pre_mix", "g_post_mix", "g_pre_ffn", "g_post_ffn", "g_group") for l in range(N_LAYER)],
        axis=0), 8)
    rows_gw = _pad_rows(jnp.concatenate(
        [grads[l][k] for k in ("conv_b", "conv_ln_g", "conv_ln_b", "conv_pw_b", "sgu_ln_g", "sgu_ln_b") for l in range(N_LAYER)],
        axis=0), 8)
    rows_128 = jnp.concatenate([_pad_rows(jnp.concatenate([grads[l]["bf"] for l in range(N_LAYER)]
                                                          + [grads[l]["sinks"] for l in range(N_LAYER)], axis=0), 8)]
                               + [grads[l]["sgu_bcol"] for l in range(N_LAYER)], axis=0)
    rows_w = jnp.concatenate([grads[l]["wcat"] for l in range(N_LAYER)], axis=0)
    rows_fb = st("ffn_conv_b").reshape(N_LAYER * N_DEV, FF_BLK)
    small_flight = _xchg_start([d_conv_w, d_pw_w, d_fcw, rows_d, rows_gw, rows_128, rows_w, rows_fb],
                               ["a2a"] * 3 + ["bcast"] * 5, "small_grads_start")

    to_mem = {"w_in": lambda a: a.transpose(2, 0, 1), "ffn_w_up": lambda a: a.transpose(0, 2, 1)}
    from_mem = {"w_in": lambda a: a.transpose(1, 2, 0), "ffn_w_up": lambda a: a.transpose(0, 2, 1)}
    flights = dict(grad_flights)
    gr, delta, new_m, new_v = {}, {}, {}, {}

    def adamw_big(n):
        view, back = to_mem.get(n, lambda a: a), from_mem.get(n, lambda a: a)
        shape = view(w[n]).shape
        d, m2, v2 = _adamw(_view2d(view(w[n])), _view2d(gr[n]), _view2d(view(mom[n])), _view2d(view(var[n])), "adamw_" + n)
        delta[n], new_m[n], new_v[n] = back(d.reshape(shape)), back(m2.reshape(shape)), back(v2.reshape(shape))
        gr[n] = back(gr[n].reshape(shape))

    after = small_flight.token
    for n, key in (("ffn_w_down", "w_down"), ("ffn_w_up", "w_up"), ("w_out", "w_out"), ("w_in", "w_in")):
        sums = []
        for l in reversed(range(N_LAYER)):
            (parts,) = _xchg_wait(flights[(l, key)], [0], after, f"grad_wait_{key}_l{l}")
            sums.append(_sum8(parts, f"sum_{key}_l{l}"))
        g = jnp.stack(sums[::-1])
        gr[n] = to_mem["w_in"](_unperm_in(g)) if n == "w_in" else g
        adamw_big(n)
        after = new_v[n]

    small_parts = _xchg_wait(small_flight, list(range(8)), after, "small_grads_wait")
    s_conv_w, s_pw_w, s_fcw, s_d, s_gw, s_128, s_w, s_fb = _sum8_small(
        [p.reshape(N_DEV, -1, p.shape[-1]) for p in small_parts], "sum_small_grads")
    gr["conv_w"] = s_conv_w.reshape(N_LAYER, CONV_K, GW // N_DEV)
    gr["conv_pw_w"] = s_pw_w.reshape(N_LAYER, GW // N_DEV, GW)
    gr["ffn_conv_w"] = s_fcw.reshape(N_LAYER, 3, FF_BLK)
    gr["b_ada"] = s_d[:6 * N_LAYER].reshape(N_LAYER, 6 * D)
    for i, k in enumerate(("g_pre_mix", "g_post_mix", "g_pre_ffn", "g_post_ffn", "g_group")):
        gr[k] = s_d[6 * N_LAYER + 2 * i:6 * N_LAYER + 2 * i + 2]
    for i, k in enumerate(("conv_b", "conv_ln_g", "conv_ln_b", "conv_pw_b", "sgu_ln_g", "sgu_ln_b")):
        gr[k] = s_gw[2 * i:2 * i + 2]
    gr["b_fgate"] = s_128[0:2, :4]
    gr["swa_sinks"] = s_128[2:4, :4]
    gr["sgu_b"] = s_128[8:].reshape(N_LAYER, WIN, 128)[:, :, :4].transpose(0, 2, 1)
    gr["sgu_w"] = s_w.reshape(N_LAYER, WIN, 4, WIN).transpose(0, 2, 1, 3)
    gr["ffn_conv_b"] = s_fb.reshape(N_LAYER, N_DEV * FF_BLK)
    dmod_all = small_parts[3][:, :6 * N_LAYER, :].reshape(N_DEV, N_LAYER, 6 * D)
    ncol = 6 * D // N_DEV
    dmod_cols = lax.dynamic_slice_in_dim(dmod_all, me * ncol, ncol, axis=2).transpose(1, 0, 2)
    gr["w_ada"] = _ada_bwd(c_all, dmod_cols)

    adamw_big("w_ada")
    smalls = [n for n in _WEIGHTS if n not in ("w_ada", "w_in", "w_out", "ffn_w_up", "ffn_w_down")]
    ds, ms, vs = _adamw_small([_view2d(w[n]) for n in smalls], [_view2d(gr[n]) for n in smalls],
                              [_view2d(mom[n]) for n in smalls], [_view2d(var[n]) for n in smalls], "adamw_small")
    for i, n in enumerate(smalls):
        delta[n], new_m[n], new_v[n] = ds[i].reshape(w[n].shape), ms[i].reshape(w[n].shape), vs[i].reshape(w[n].shape)

    return (loss, grad_x, *[gr[n].reshape(w[n].shape) for n in _WEIGHTS], *[delta[n] for n in _WEIGHTS],
            *[new_m[n] for n in _WEIGHTS], *[new_v[n] for n in _WEIGHTS])
```

```python
import functools

import jax
import jax.numpy as jnp
from jax import lax
from jax.experimental import pallas as pl
from jax.experimental.pallas import tpu as pltpu

F32, BF16 = jnp.float32, jnp.bfloat16
SDS = jax.ShapeDtypeStruct
MESH = pl.DeviceIdType.MESH

N_DEV = 8
D = 1024
GW = 256
HD = 64
N_LAYER = 2
ZW = 2432
FF_BLK = 704
FF_NBLK = 4
CONV_K = 31
CONV_HALO = 32
FFN_HALO = 16
EPS = 1e-6
NEG = -1e30
SCALE = HD ** -0.5
VMEM_LIMIT_V7X = 56 * 1024 * 1024
TM = 512
WGRAD_ROWS = 256
TQ = 256
WIN = 128

ADAM_LR, ADAM_B1, ADAM_B2, ADAM_EPS, ADAM_WD, ADAM_STEP = 0.001, 0.9, 0.999, 1e-08, 0.01, 10

Z_FQ, Z_FK, Z_FV, Z_CA, Z_CG, Z_SQ = 0, 1, 2, 3, 4, 5
Z_SK, Z_SV = 12, 13
Z_GU, Z_GV = 7, 8
Z_FG = 18


def _cp(sem=None):
    return pltpu.CompilerParams(dimension_semantics=sem, vmem_limit_bytes=VMEM_LIMIT_V7X)


def _vec(arr3, idx, ngrid):
    w = arr3.shape[-1]
    if ngrid == 1:
        return pl.BlockSpec((None, 1, w), lambda i: (idx, 0, 0))
    return pl.BlockSpec((None, 1, w), lambda i, j: (idx, 0, 0))


def _sigmoid(x):
    return jax.nn.sigmoid(x)


def _silu(x):
    return x * _sigmoid(x)


def _dsilu(x):
    s = _sigmoid(x)
    return s * (1.0 + x * (1.0 - s))


_G0, _G1 = 0.7978845608028654, 0.044715


def _gelu(x):
    return 0.5 * x * (1.0 + jnp.tanh(_G0 * (x + _G1 * x * x * x)))


def _dgelu(x):
    t = jnp.tanh(_G0 * (x + _G1 * x * x * x))
    return 0.5 * (1.0 + t) + 0.5 * x * (1.0 - t * t) * (_G0 * (1.0 + 3.0 * _G1 * x * x))


def _rstd(x):
    return lax.rsqrt(jnp.mean(x * x, axis=-1, keepdims=True) + EPS)


def _rms_bwd(xh, r, t):
    return r * (t - xh * jnp.mean(t * xh, axis=-1, keepdims=True))


def _ln_stats(x):
    mu = jnp.mean(x, axis=-1, keepdims=True)
    xc = x - mu
    rstd = lax.rsqrt(jnp.mean(xc * xc, axis=-1, keepdims=True) + EPS)
    return xc * rstd, rstd


def _ln_bwd(xh, rstd, dxh):
    return rstd * (dxh - jnp.mean(dxh, axis=-1, keepdims=True) - xh * jnp.mean(dxh * xh, axis=-1, keepdims=True))


def _colsum(x):
    return jnp.sum(x, axis=0, keepdims=True)


def _dot(a, b, kind):
    dn = {"nn": (((1,), (0,)), ((), ())), "nt": (((1,), (1,)), ((), ())), "tn": (((0,), (0,)), ((), ()))}[kind]
    return lax.dot_general(a.astype(BF16), b.astype(BF16), dn, preferred_element_type=F32)


def _exchange(arrs, modes, name):
    n = len(arrs)
    outs = [SDS((N_DEV,) + a.shape, a.dtype) if m == "bcast" else SDS(a.shape, a.dtype) for a, m in zip(arrs, modes)]

    def body(*refs):
        ins, dst = refs[:n], refs[n:2 * n]
        send, recv, loc = refs[2 * n:]
        x, y, c = lax.axis_index("x"), lax.axis_index("y"), lax.axis_index("c")
        me = 4 * x + 2 * y + c

        def src(a, j):
            return ins[a] if modes[a] == "bcast" else ins[a].at[j]

        local = [pltpu.make_async_copy(src(a, me), dst[a].at[me], loc.at[a]) for a in range(n)]
        for cp in local:
            cp.start()
        sent, landed = [], []
        for k in (2, 4, 6, 3, 5, 7, 1):
            px = 1 - x if k & 4 else x
            py = 1 - y if k & 2 else y
            pc = 1 - c if k & 1 else c
            peer = 4 * px + 2 * py + pc
            for a in range(n):
                cp = pltpu.make_async_remote_copy(src_ref=src(a, peer), dst_ref=dst[a].at[me], send_sem=send.at[a, k - 1],
                                                  recv_sem=recv.at[a, k - 1], device_id=(px, py, pc), device_id_type=MESH)
                cp.start()
                sent.append(cp)
                landed.append(pltpu.make_async_remote_copy(src_ref=src(a, peer), dst_ref=dst[a].at[peer],
                                                           send_sem=send.at[a, k - 1], recv_sem=recv.at[a, k - 1],
                                                           device_id=(px, py, pc), device_id_type=MESH))
        for cp in landed:
            cp.wait_recv()
        for cp in sent:
            cp.wait_send()
        for cp in local:
            cp.wait()

    hbm = pl.BlockSpec(memory_space=pltpu.HBM)
    return pl.pallas_call(
        body, name=name, out_shape=outs, in_specs=[hbm] * n, out_specs=[hbm] * n,
        scratch_shapes=[pltpu.SemaphoreType.DMA((n, N_DEV - 1)), pltpu.SemaphoreType.DMA((n, N_DEV - 1)),
                        pltpu.SemaphoreType.DMA((n,))],
        compiler_params=pltpu.CompilerParams(has_side_effects=True),
    )(*arrs)


_PEER_ORDER = (2, 4, 6, 3, 5, 7, 1)
_HBM = pl.BlockSpec(memory_space=pltpu.HBM)
_SEM = pl.BlockSpec(memory_space=pltpu.SEMAPHORE)
_EFFECT = pltpu.SideEffectType.DATAFLOW_SIDE_EFFECTING


def _peer(k):
    x, y, c = lax.axis_index("x"), lax.axis_index("y"), lax.axis_index("c")
    px = 1 - x if k & 4 else x
    py = 1 - y if k & 2 else y
    pc = 1 - c if k & 1 else c
    return (px, py, pc), 4 * px + 2 * py + pc


def _my_id():
    return 4 * lax.axis_index("x") + 2 * lax.axis_index("y") + lax.axis_index("c")


def _split_copies(src_ref, land_ref, send, recv, loc, mode):
    me = _my_id()
    pick = (lambda j: src_ref) if mode == "bcast" else (lambda j: src_ref.at[j])
    local = pltpu.make_async_copy(pick(me), land_ref.at[me], loc)
    remote = []
    for k in _PEER_ORDER:
        dev, peer = _peer(k)
        out = pltpu.make_async_remote_copy(src_ref=pick(peer), dst_ref=land_ref.at[me], send_sem=send.at[k - 1],
                                           recv_sem=recv.at[k - 1], device_id=dev, device_id_type=MESH)
        arrive = pltpu.make_async_remote_copy(src_ref=pick(peer), dst_ref=land_ref.at[peer], send_sem=send.at[k - 1],
                                              recv_sem=recv.at[k - 1], device_id=dev, device_id_type=MESH)
        remote.append((out, arrive))
    return local, remote


class _Flight:
    def __init__(self, srcs, lands, sends, recvs, locs, modes, token):
        self.srcs, self.lands, self.sends, self.recvs, self.locs, self.modes, self.token = (
            list(srcs), list(lands), list(sends), list(recvs), list(locs), list(modes), token)


def _xchg_start(arrs, modes, name):
    n = len(arrs)
    lands = [lax.empty((N_DEV,) + a.shape if m == "bcast" else a.shape, a.dtype) for a, m in zip(arrs, modes)]

    def body(*refs):
        srcs, lnds = refs[:n], refs[n:2 * n]
        outs = refs[2 * n:]
        sends, recvs, locs, token = outs[:n], outs[n:2 * n], outs[2 * n:3 * n], outs[5 * n]
        for a in range(n):
            local, remote = _split_copies(srcs[a], lnds[a], sends[a], recvs[a], locs[a], modes[a])
            local.start()
            for out, _ in remote:
                out.start()
        token[...] = jnp.zeros_like(token)

    sem7 = pltpu.SemaphoreType.DMA((N_DEV - 1,))
    res = pl.pallas_call(
        body, name=name,
        out_shape=[sem7] * (2 * n) + [pltpu.SemaphoreType.DMA(())] * n + [pltpu.HBM(a.shape, a.dtype) for a in arrs]
        + [pltpu.HBM(b.shape, b.dtype) for b in lands] + [SDS((8, 128), F32)],
        in_specs=[_HBM] * (2 * n), out_specs=[_SEM] * (3 * n) + [_HBM] * (2 * n) + [pl.BlockSpec(memory_space=pltpu.VMEM)],
        input_output_aliases={i: 3 * n + i for i in range(2 * n)},
        compiler_params=pltpu.CompilerParams(has_side_effects=_EFFECT),
    )(*[pltpu.with_memory_space_constraint(a, pltpu.HBM) for a in arrs],
      *[pltpu.with_memory_space_constraint(b, pltpu.HBM) for b in lands])
    return _Flight(res[3 * n:4 * n], res[4 * n:5 * n], res[:n], res[n:2 * n], res[2 * n:3 * n], modes, res[5 * n])


def _xchg_wait(flight, idx, after, name):
    n = len(idx)
    modes = [flight.modes[i] for i in idx]

    def body(*refs):
        srcs, lnds = refs[:n], refs[n:2 * n]
        sends, recvs, locs = refs[2 * n:3 * n], refs[3 * n:4 * n], refs[4 * n:5 * n]
        for a in range(n):
            local, remote = _split_copies(srcs[a], lnds[a], sends[a], recvs[a], locs[a], modes[a])
            local.wait()
            for _, arrive in remote:
                arrive.wait_send()
                arrive.wait_recv()

    ops = ([flight.srcs[i] for i in idx] + [flight.lands[i] for i in idx] + [flight.sends[i] for i in idx]
           + [flight.recvs[i] for i in idx] + [flight.locs[i] for i in idx])
    res = pl.pallas_call(
        body, name=name, out_shape=[pltpu.HBM(o.shape, o.dtype) for o in ops[:2 * n]],
        in_specs=[_HBM] * (2 * n) + [_SEM] * (3 * n) + [pl.BlockSpec(memory_space=pl.ANY)], out_specs=[_HBM] * (2 * n),
        input_output_aliases={i: i for i in range(2 * n)},
        compiler_params=pltpu.CompilerParams(has_side_effects=_EFFECT),
    )(*ops, after)
    return res[n:]


class _Lazy:
    def __init__(self, fn, pre=None):
        self.fn, self.pre, self.val, self.started = fn, pre, None, False

    def prefetch(self, after):
        token = self.pre(after) if self.pre is not None and not self.started else None
        self.started = True
        return token

    def get(self, after):
        self.prefetch(after)
        if self.val is None:
            self.val = self.fn(after)
        return self.val


_CHIP_PEERS = (2, 4, 6)


def _g2_copies_a(src_ref, land_ref, send, recv, loc):
    me = _my_id()
    local = pltpu.make_async_copy(src_ref, land_ref.at[me], loc)
    remote = []
    for j, k in enumerate(_CHIP_PEERS + (1,)):
        dev, peer = _peer(k)
        out = pltpu.make_async_remote_copy(src_ref=src_ref, dst_ref=land_ref.at[me], send_sem=send.at[j], recv_sem=recv.at[j],
                                           device_id=dev, device_id_type=MESH)
        arrive = pltpu.make_async_remote_copy(src_ref=src_ref, dst_ref=land_ref.at[peer], send_sem=send.at[j],
                                              recv_sem=recv.at[j], device_id=dev, device_id_type=MESH)
        remote.append((out, arrive))
    return local, remote


def _g2_copies_b(land_ref, send, recv):
    sib, _ = _peer(1)
    pairs = []
    for j, k in enumerate(_CHIP_PEERS):
        _, same_core = _peer(k)
        _, other_core = _peer(k | 1)
        out = pltpu.make_async_remote_copy(src_ref=land_ref.at[same_core], dst_ref=land_ref.at[same_core], send_sem=send.at[j],
                                           recv_sem=recv.at[j], device_id=sib, device_id_type=MESH)
        arrive = pltpu.make_async_remote_copy(src_ref=land_ref.at[same_core], dst_ref=land_ref.at[other_core],
                                              send_sem=send.at[j], recv_sem=recv.at[j], device_id=sib, device_id_type=MESH)
        pairs.append((out, arrive))
    return pairs


class _Gather2:
    def __init__(self, srcs, lands, sends, recvs, locs, token):
        self.srcs, self.lands, self.sends, self.recvs, self.locs, self.token = (
            list(srcs), list(lands), list(sends), list(recvs), list(locs), token)
        self.sends_b, self.recvs_b = [None] * len(self.srcs), [None] * len(self.srcs)


def _g2_start(arrs, name):
    n = len(arrs)
    lands = [lax.empty((N_DEV,) + a.shape, a.dtype) for a in arrs]

    def body(*refs):
        srcs, lnds = refs[:n], refs[n:2 * n]
        outs = refs[2 * n:]
        sends, recvs, locs, token = outs[:n], outs[n:2 * n], outs[2 * n:3 * n], outs[5 * n]
        for a in range(n):
            local, remote = _g2_copies_a(srcs[a], lnds[a], sends[a], recvs[a], locs[a])
            local.start()
            for out, _ in remote:
                out.start()
        token[...] = jnp.zeros_like(token)

    sem4 = pltpu.SemaphoreType.DMA((4,))
    res = pl.pallas_call(
        body, name=name,
        out_shape=[sem4] * (2 * n) + [pltpu.SemaphoreType.DMA(())] * n + [pltpu.HBM(a.shape, a.dtype) for a in arrs]
        + [pltpu.HBM(b.shape, b.dtype) for b in lands] + [SDS((8, 128), F32)],
        in_specs=[_HBM] * (2 * n), out_specs=[_SEM] * (3 * n) + [_HBM] * (2 * n) + [pl.BlockSpec(memory_space=pltpu.VMEM)],
        input_output_aliases={i: 3 * n + i for i in range(2 * n)},
        compiler_params=pltpu.CompilerParams(has_side_effects=_EFFECT),
    )(*[pltpu.with_memory_space_constraint(a, pltpu.HBM) for a in arrs],
      *[pltpu.with_memory_space_constraint(b, pltpu.HBM) for b in lands])
    return _Gather2(res[3 * n:4 * n], res[4 * n:5 * n], res[:n], res[n:2 * n], res[2 * n:3 * n], res[5 * n])


def _g2_relay(g, idx, after, name):
    n = len(idx)

    def body(*refs):
        srcs, lnds = refs[:n], refs[n:2 * n]
        sends, recvs, locs = refs[2 * n:3 * n], refs[3 * n:4 * n], refs[4 * n:5 * n]
        outs = refs[5 * n + 1:]
        sends_b, recvs_b = outs[2 * n:3 * n], outs[3 * n:4 * n]
        for a in range(n):
            local, remote = _g2_copies_a(srcs[a], lnds[a], sends[a], recvs[a], locs[a])
            local.wait()
            for _, arrive in remote:
                arrive.wait_send()
                arrive.wait_recv()
        for a in range(n):
            for out, _ in _g2_copies_b(lnds[a], sends_b[a], recvs_b[a]):
                out.start()
        outs[4 * n][...] = jnp.zeros_like(outs[4 * n])

    ops = ([g.srcs[i] for i in idx] + [g.lands[i] for i in idx] + [g.sends[i] for i in idx] + [g.recvs[i] for i in idx]
           + [g.locs[i] for i in idx])
    sem3 = pltpu.SemaphoreType.DMA((3,))
    res = pl.pallas_call(
        body, name=name, out_shape=[pltpu.HBM(o.shape, o.dtype) for o in ops[:2 * n]] + [sem3] * (2 * n) + [SDS((8, 128), F32)],
        in_specs=[_HBM] * (2 * n) + [_SEM] * (3 * n) + [pl.BlockSpec(memory_space=pl.ANY)],
        out_specs=[_HBM] * (2 * n) + [_SEM] * (2 * n) + [pl.BlockSpec(memory_space=pltpu.VMEM)],
        input_output_aliases={i: i for i in range(2 * n)},
        compiler_params=pltpu.CompilerParams(has_side_effects=_EFFECT),
    )(*ops, after)
    for a, i in enumerate(idx):
        g.srcs[i], g.lands[i] = res[a], res[n + a]
        g.sends_b[i], g.recvs_b[i] = res[2 * n + a], res[3 * n + a]
    return res[4 * n]


def _g2_wait(g, idx, after, name):
    n = len(idx)

    def body(*refs):
        lnds, sends_b, recvs_b = refs[:n], refs[n:2 * n], refs[2 * n:3 * n]
        for a in range(n):
            for _, arrive in _g2_copies_b(lnds[a], sends_b[a], recvs_b[a]):
                arrive.wait_send()
                arrive.wait_recv()

    ops = [g.lands[i] for i in idx] + [g.sends_b[i] for i in idx] + [g.recvs_b[i] for i in idx]
    res = pl.pallas_call(
        body, name=name, out_shape=[pltpu.HBM(o.shape, o.dtype) for o in ops[:n]],
        in_specs=[_HBM] * n + [_SEM] * (2 * n) + [pl.BlockSpec(memory_space=pl.ANY)], out_specs=[_HBM] * n,
        input_output_aliases={i: i for i in range(n)},
        compiler_params=pltpu.CompilerParams(has_side_effects=_EFFECT),
    )(*ops, after)
    return list(res)


def _matmul(a, b, kind, out_shape, out_dtype, grid, a_spec, b_spec, o_spec, acc_shape, name):
    nk = grid[2]

    def body(a_ref, b_ref, o_ref, *scratch):
        prod = _dot(a_ref[...], b_ref[...], kind)
        if nk == 1:
            o_ref[...] = prod.astype(out_dtype)
        else:
            acc = scratch[0]
            k = pl.program_id(2)

            @pl.when(k == 0)
            def _():
                acc[...] = prod

            @pl.when(k > 0)
            def _():
                acc[...] += prod

            @pl.when(k == nk - 1)
            def _():
                o_ref[...] = acc[...].astype(out_dtype)

    return pl.pallas_call(
        body, name=name, grid=grid, in_specs=[a_spec, b_spec], out_specs=o_spec, out_shape=SDS(out_shape, out_dtype),
        scratch_shapes=[] if nk == 1 else [pltpu.VMEM(acc_shape, F32)],
        compiler_params=_cp(("parallel", "parallel", "arbitrary")))(a, b)


def _bs(shape, fn):
    return pl.BlockSpec(shape, fn)


def _mm_rows(a, w, kind, n_out, out_dtype, name):
    s, k = a.shape
    tm = min(TM, s)
    return _matmul(a, w, kind, (s, n_out), out_dtype, (s // tm, 1, 1),
                   _bs((tm, k), lambda i, j, kk: (i, 0)), _bs(w.shape, lambda i, j, kk: (0, 0)),
                   _bs((tm, n_out), lambda i, j, kk: (i, 0)), None, name)


def _mm_wgrad(a, dy, out_dtype, name):
    s, k = a.shape
    n = dy.shape[1]
    tko = min(WGRAD_ROWS, k)
    return _matmul(a, dy, "tn", (k, n), out_dtype, (k // tko, 1, 1),
                   _bs((s, tko), lambda i, j, kk: (0, i)), _bs((s, n), lambda i, j, kk: (0, 0)),
                   _bs((tko, n), lambda i, j, kk: (i, 0)), None, name)


def _ada_fwd(c_all, w_ada):
    ncol = w_ada.shape[2]

    def body(c_ref, w_ref, o_ref):
        ca = _silu(c_ref[...])
        ca = jnp.concatenate([ca, jnp.zeros_like(ca)], axis=0)
        o_ref[...] = _dot(ca, w_ref[...], "nn")[:N_DEV, :]

    return pl.pallas_call(
        body, name="ada_fwd", grid=(N_LAYER,),
        in_specs=[pl.BlockSpec((N_DEV, D), lambda l: (0, 0)), pl.BlockSpec((None, D, ncol), lambda l: (l, 0, 0))],
        out_specs=pl.BlockSpec((None, N_DEV, ncol), lambda l: (l, 0, 0)),
        out_shape=SDS((N_LAYER, N_DEV, ncol), F32), compiler_params=_cp(("parallel",)))(c_all, w_ada)


def _ada_finish(m_mine, b_ada):
    def body(m_ref, b_ref, o_ref, t_ref):
        o_ref[...] = m_ref[...] + b_ref[...]
        t_ref[...] = jnp.zeros_like(t_ref)

    return pl.pallas_call(body, name="ada_finish", out_shape=[SDS(b_ada.shape, F32), SDS((8, 128), F32)])(m_mine, b_ada)


def _ada_bwd(c_all, dmod_cols):
    ncol = dmod_cols.shape[2]

    def body(c_ref, d_ref, o_ref):
        ca = _silu(c_ref[...])
        ca = jnp.concatenate([ca, jnp.zeros_like(ca)], axis=0)
        dm = d_ref[...]
        dm = jnp.concatenate([dm, jnp.zeros_like(dm)], axis=0)
        o_ref[...] = _dot(ca, dm, "tn")

    return pl.pallas_call(
        body, name="ada_bwd", grid=(N_LAYER,),
        in_specs=[pl.BlockSpec((N_DEV, D), lambda l: (0, 0)), pl.BlockSpec((None, N_DEV, ncol), lambda l: (l, 0, 0))],
        out_specs=pl.BlockSpec((None, D, ncol), lambda l: (l, 0, 0)),
        out_shape=SDS((N_LAYER, D, ncol), F32), compiler_params=_cp(("parallel",)))(c_all, dmod_cols)


def _rows(s):
    tm = min(TM, s)
    return tm, pl.BlockSpec((tm, D), lambda i: (i, 0))


def _pcall(body, operands, dep, **kw):
    if dep is None:
        return pl.pallas_call(body, **kw)(*operands)
    n = len(operands)

    def body_dep(*refs):
        body(*refs[:n], *refs[n + 1:])

    kw["in_specs"] = list(kw["in_specs"]) + [pl.BlockSpec(memory_space=pl.ANY)]
    return pl.pallas_call(body_dep, **kw)(*operands, dep)


def _rms_mod(x, g, sc, sh, name):
    s = x.shape[0]
    tm, row = _rows(s)

    def body(x_ref, g_ref, sc_ref, sh_ref, h_ref):
        xf = x_ref[...]
        h_ref[...] = (xf * _rstd(xf) * (g_ref[...] * (1.0 + sc_ref[...])) + sh_ref[...]).astype(BF16)

    return pl.pallas_call(
        body, name=name, grid=(s // tm,), in_specs=[row, _vec(*g, 1), _vec(*sc, 1), _vec(*sh, 1)], out_specs=row,
        out_shape=SDS((s, D), BF16), compiler_params=_cp(("parallel",)))(x, g[0], sc[0], sh[0])


def _post(xres, o, ga, gpost, gn, scn, shn, name, dep=None):
    s = xres.shape[0]
    tm, row = _rows(s)

    def body(x_ref, o_ref, ga_ref, gp_ref, gn_ref, sc_ref, sh_ref, xn_ref, h_ref):
        of = o_ref[...].astype(F32)
        xn = x_ref[...] + ga_ref[...] * (of * _rstd(of) * gp_ref[...])
        xn_ref[...] = xn
        h_ref[...] = (xn * _rstd(xn) * (gn_ref[...] * (1.0 + sc_ref[...])) + sh_ref[...]).astype(BF16)

    return _pcall(
        body, (xres, o, ga[0], gpost[0], gn[0], scn[0], shn[0]), dep, name=name, grid=(s // tm,),
        in_specs=[row, row, _vec(*ga, 1), _vec(*gpost, 1), _vec(*gn, 1), _vec(*scn, 1), _vec(*shn, 1)],
        out_specs=[row, row], out_shape=[SDS((s, D), F32), SDS((s, D), BF16)], compiler_params=_cp(("parallel",)))


def _post_loss(xres, o, ga, gpost, target, name, dep=None):
    s = xres.shape[0]
    tm, row = _rows(s)

    def body(x_ref, o_ref, ga_ref, gp_ref, t_ref, dy_ref, loss_ref):
        of = o_ref[...].astype(F32)
        err = x_ref[...] + ga_ref[...] * (of * _rstd(of) * gp_ref[...]) - t_ref[...]
        dy_ref[...] = err * (1.0 / D)

        @pl.when(pl.program_id(0) == 0)
        def _():
            loss_ref[...] = jnp.zeros_like(loss_ref)

        loss_ref[...] += jnp.sum(jnp.mean(err * err, axis=-1, keepdims=True), axis=0, keepdims=True) * 0.5

    return _pcall(
        body, (xres, o, ga[0], gpost[0], target), dep, name=name, grid=(s // tm,),
        in_specs=[row, row, _vec(*ga, 1), _vec(*gpost, 1), row],
        out_specs=[row, pl.BlockSpec((8, 128), lambda i: (0, 0))], out_shape=[SDS((s, D), F32), SDS((8, 128), F32)],
        compiler_params=_cp(("arbitrary",)))


def _acc(ref, val, first):
    @pl.when(first)
    def _():
        ref[...] = val

    @pl.when(jnp.logical_not(first))
    def _():
        ref[...] += val


def _post_bwd(dxn, o, ga, gpost, name):
    s = dxn.shape[0]
    tm, row = _rows(s)
    vec = pl.BlockSpec((1, D), lambda i: (0, 0))

    def body(d_ref, o_ref, ga_ref, gp_ref, do_ref, dga_ref, dgp_ref):
        of, dx = o_ref[...].astype(F32), d_ref[...]
        r = _rstd(of)
        oh = of * r
        do_ref[...] = _rms_bwd(oh, r, dx * (ga_ref[...] * gp_ref[...])).astype(BF16)
        cs = _colsum(dx * oh)
        first = pl.program_id(0) == 0
        _acc(dga_ref, cs * gp_ref[...], first)
        _acc(dgp_ref, cs * ga_ref[...], first)

    return pl.pallas_call(
        body, name=name, grid=(s // tm,), in_specs=[row, row, _vec(*ga, 1), _vec(*gpost, 1)], out_specs=[row, vec, vec],
        out_shape=[SDS((s, D), BF16), SDS((1, D), F32), SDS((1, D), F32)],
        compiler_params=_cp(("arbitrary",)))(dxn, o, ga[0], gpost[0])


def _pre_bwd(dh, x, dres, g, sc, name, dep=None):
    s = x.shape[0]
    tm, row = _rows(s)
    vec = pl.BlockSpec((1, D), lambda i: (0, 0))

    def body(dh_ref, x_ref, dr_ref, g_ref, sc_ref, dx_ref, dsh_ref, dsc_ref, dg_ref):
        xf, d = x_ref[...], dh_ref[...].astype(F32)
        r = _rstd(xf)
        xh = xf * r
        dx_ref[...] = dr_ref[...] + _rms_bwd(xh, r, d * (g_ref[...] * (1.0 + sc_ref[...])))
        cs = _colsum(d * xh)
        first = pl.program_id(0) == 0
        _acc(dsh_ref, _colsum(d), first)
        _acc(dsc_ref, cs * g_ref[...], first)
        _acc(dg_ref, cs * (1.0 + sc_ref[...]), first)

    return _pcall(
        body, (dh, x, dres, g[0], sc[0]), dep, name=name, grid=(s // tm,),
        in_specs=[row, row, row, _vec(*g, 1), _vec(*sc, 1)], out_specs=[row, vec, vec, vec],
        out_shape=[SDS((s, D), F32), SDS((1, D), F32), SDS((1, D), F32), SDS((1, D), F32)],
        compiler_params=_cp(("arbitrary",)))


def _gnorm(ys, gg, name):
    s = ys[0].shape[0]
    tm = min(TM, s)
    yb = pl.BlockSpec((tm, GW), lambda i: (i, 0))

    def body(y0, y1, y2, y3, g_ref, o_ref):
        for i, yr in enumerate((y0, y1, y2, y3)):
            y = yr[...]
            o_ref[:, GW * i:GW * (i + 1)] = (y * _rstd(y) * g_ref[:, GW * i:GW * (i + 1)]).astype(BF16)

    return pl.pallas_call(
        body, name=name, grid=(s // tm,), in_specs=[yb] * 4 + [_vec(*gg, 1)], out_specs=pl.BlockSpec((tm, D), lambda i: (i, 0)),
        out_shape=SDS((s, D), BF16), compiler_params=_cp(("parallel",)))(*ys, gg[0])


def _gnorm_bwd(dyn, ys, gg, name, dep=None):
    s = ys[0].shape[0]
    tm = min(TM, s)
    yb = pl.BlockSpec((tm, GW), lambda i: (i, 0))

    def body(d_ref, y0, y1, y2, y3, g_ref, o0, o1, o2, o3, dg_ref):
        first = pl.program_id(0) == 0
        for i, (yr, orf) in enumerate(zip((y0, y1, y2, y3), (o0, o1, o2, o3))):
            y = yr[...]
            d = d_ref[:, GW * i:GW * (i + 1)].astype(F32)
            r = _rstd(y)
            yh = y * r
            orf[...] = _rms_bwd(yh, r, d * g_ref[:, GW * i:GW * (i + 1)]).astype(BF16)
            cs = _colsum(d * yh)

            @pl.when(first)
            def _():
                dg_ref[:, GW * i:GW * (i + 1)] = cs

            @pl.when(jnp.logical_not(first))
            def _():
                dg_ref[:, GW * i:GW * (i + 1)] += cs

    return _pcall(
        body, (dyn, *ys, gg[0]), dep, name=name, grid=(s // tm,),
        in_specs=[pl.BlockSpec((tm, D), lambda i: (i, 0))] + [yb] * 4 + [_vec(*gg, 1)],
        out_specs=[yb] * 4 + [pl.BlockSpec((1, D), lambda i: (0, 0))],
        out_shape=[SDS((s, GW), BF16)] * 4 + [SDS((1, D), F32)], compiler_params=_cp(("arbitrary",)))


def _lane_put(acc, col, h):
    lane = lax.broadcasted_iota(jnp.int32, acc.shape, 1)
    return jnp.where(lane == h, col, acc)


def _fgate(z, bf, name, dep=None):
    s = z.shape[0]

    def body(z_ref, b_ref, fc_ref, fr_ref):
        xg = z_ref[...].astype(F32) + b_ref[...]
        lf = jnp.minimum(xg, 0.0) - jnp.log(1.0 + jnp.exp(-jnp.abs(xg)))
        lane = lax.broadcasted_iota(jnp.int32, lf.shape, 1)
        row = lax.broadcasted_iota(jnp.int32, lf.shape, 0)
        f = jnp.where(lane < 4, lf, 0.0)
        sh = 1
        while sh < s:
            f = f + jnp.where(row >= sh, pltpu.roll(f, sh, 0), 0.0)
            sh *= 2
        fc_ref[...] = f
        fr_ref[...] = f.T[:8, :]

    return _pcall(
        body, (z, bf), dep, name=name, grid=(1,),
        in_specs=[pl.BlockSpec((s, 128), lambda i: (0, Z_FG)), pl.BlockSpec((1, 128), lambda i: (0, 0))],
        out_specs=[pl.BlockSpec((s, 128), lambda i: (0, 0)), pl.BlockSpec((8, s), lambda i: (0, 0))],
        out_shape=[SDS((s, 128), F32), SDS((8, s), F32)], compiler_params=_cp(("arbitrary",)))


def _fgate_bwd(z, bf, dfrow, dfcol, name):
    s = z.shape[0]

    def body(z_ref, b_ref, d_ref, dc_ref, dz_ref, db_ref):
        d = jnp.concatenate([d_ref[...], jnp.zeros((120, s), F32)], axis=0).T + dc_ref[...]
        row = lax.broadcasted_iota(jnp.int32, d.shape, 0)
        lane = lax.broadcasted_iota(jnp.int32, d.shape, 1)
        sh = 1
        while sh < s:
            d = d + jnp.where(row < s - sh, pltpu.roll(d, s - sh, 0), 0.0)
            sh *= 2
        xg = z_ref[...].astype(F32) + b_ref[...]
        dz = jnp.where(lane < 4, d * _sigmoid(-xg), 0.0)
        dz_ref[...] = dz.astype(BF16)
        db_ref[...] = _colsum(dz)

    return pl.pallas_call(
        body, name=name, grid=(1,),
        in_specs=[pl.BlockSpec((s, 128), lambda i: (0, Z_FG)), pl.BlockSpec((1, 128), lambda i: (0, 0)),
                  pl.BlockSpec((8, s), lambda i: (0, 0)), pl.BlockSpec((s, 128), lambda i: (0, 0))],
        out_specs=[pl.BlockSpec((s, 128), lambda i: (0, 0)), pl.BlockSpec((1, 128), lambda i: (0, 0))],
        out_shape=[SDS((s, 128), BF16), SDS((1, 128), F32)], compiler_params=_cp(("arbitrary",)))(z, bf, dfrow, dfcol)


def _fox_scores(q_ref, k_ref, fc_ref, fr_ref, h, i, nk, tq):
    kw = nk * tq
    q = q_ref[:, HD * h:HD * (h + 1)] * SCALE
    sc = _dot(q, k_ref[0:kw, HD * h:HD * (h + 1)], "nt") + fc_ref[:, h:h + 1] - fr_ref[h:h + 1, 0:kw]
    qpos = i * tq + lax.broadcasted_iota(jnp.int32, (tq, kw), 0)
    kpos = lax.broadcasted_iota(jnp.int32, (tq, kw), 1)
    return q, jnp.where(kpos <= qpos, sc, NEG)


def _fox_fwd(z, fcol, frow, name):
    s = z.shape[0]
    tq = min(TQ, s)
    nc = s // tq

    def body(q_ref, k_ref, v_ref, fc_ref, fr_ref, y_ref, l_ref):
        for n in range(nc):
            @pl.when(pl.program_id(0) == n)
            def _():
                kw = (n + 1) * tq
                lse = jnp.zeros((tq, 128), F32)
                for h in range(4):
                    _, sc = _fox_scores(q_ref, k_ref, fc_ref, fr_ref, h, n, n + 1, tq)
                    m = jnp.max(sc, axis=-1, keepdims=True)
                    p = jnp.exp(sc - m)
                    l = jnp.sum(p, axis=-1, keepdims=True)
                    y_ref[:, HD * h:HD * (h + 1)] = _dot(p, v_ref[0:kw, HD * h:HD * (h + 1)], "nn") / l
                    lse = _lane_put(lse, m + jnp.log(l), h)
                l_ref[...] = lse

    return pl.pallas_call(
        body, name=name, grid=(nc,),
        in_specs=[pl.BlockSpec((tq, GW), lambda i: (i, Z_FQ)), pl.BlockSpec((s, GW), lambda i: (0, Z_FK)),
                  pl.BlockSpec((s, GW), lambda i: (0, Z_FV)), pl.BlockSpec((tq, 128), lambda i: (i, 0)),
                  pl.BlockSpec((8, s), lambda i: (0, 0))],
        out_specs=[pl.BlockSpec((tq, GW), lambda i: (i, 0)), pl.BlockSpec((tq, 128), lambda i: (i, 0))],
        out_shape=[SDS((s, GW), F32), SDS((s, 128), F32)], compiler_params=_cp(("parallel",)))(z, z, z, fcol, frow)


def _fox_bwd(z, fcol, frow, lse, y, dy, name):
    s = z.shape[0]
    tq = min(TQ, s)
    nc = s // tq
    half = max(nc // 2, 1)

    def body(q_ref, k_ref, v_ref, fc_ref, fr_ref, l_ref, y_ref, dy_ref, dq_ref, dk_ref, dv_ref, df_ref, dfq_ref):
        @pl.when(pl.program_id(0) == 0)
        def _():
            dk_ref[...] = jnp.zeros_like(dk_ref)
            dv_ref[...] = jnp.zeros_like(dv_ref)
            df_ref[...] = jnp.zeros_like(df_ref)

        i = pl.program_id(0)
        for cond, nk in ((i < half, half), (i >= half, nc)):
            @pl.when(cond)
            def _():
                kw = nk * tq
                dfq = jnp.zeros((tq, 128), F32)
                for h in range(4):
                    hs = slice(HD * h, HD * (h + 1))
                    q, sc = _fox_scores(q_ref, k_ref, fc_ref, fr_ref, h, i, nk, tq)
                    p = jnp.exp(sc - l_ref[:, h:h + 1])
                    dyh = dy_ref[...].astype(F32)[:, hs]
                    dd = jnp.sum(dyh * y_ref[:, hs], axis=-1, keepdims=True)
                    ds = p * (_dot(dyh, v_ref[0:kw, hs], "nt") - dd)
                    dq_ref[:, hs] = _dot(ds, k_ref[0:kw, hs], "nn") * SCALE
                    dk_ref[0:kw, hs] += _dot(ds, q, "tn")
                    dv_ref[0:kw, hs] += _dot(p, dyh, "tn")
                    df_ref[h:h + 1, 0:kw] -= _colsum(ds)
                    dfq = _lane_put(dfq, jnp.sum(ds, axis=-1, keepdims=True), h)
                dfq_ref[...] = dfq

    tile = lambda w: pl.BlockSpec((tq, w), lambda i: (i, 0))
    full = pl.BlockSpec((s, GW), lambda i: (0, 0))
    rows8 = pl.BlockSpec((8, s), lambda i: (0, 0))
    return pl.pallas_call(
        body, name=name, grid=(nc,),
        in_specs=[pl.BlockSpec((tq, GW), lambda i: (i, Z_FQ)), pl.BlockSpec((s, GW), lambda i: (0, Z_FK)),
                  pl.BlockSpec((s, GW), lambda i: (0, Z_FV)), tile(128), rows8, tile(128), tile(GW), tile(GW)],
        out_specs=[tile(GW), full, full, rows8, tile(128)],
        out_shape=[SDS((s, GW), F32), SDS((s, GW), F32), SDS((s, GW), F32), SDS((8, s), F32), SDS((s, 128), F32)],
        compiler_params=_cp(("arbitrary",)))(z, z, z, fcol, frow, lse, y, dy)


def _swa_block(q_ref, k_ref, v_ref, n):
    qs = pl.multiple_of(n * WIN, WIN)
    ks = pl.multiple_of(jnp.maximum(n - 1, 0) * WIN, WIN)
    qb = q_ref[pl.ds(qs, WIN), :]
    kb = k_ref[pl.ds(ks, 2 * WIN), :]
    vb = v_ref[pl.ds(ks, 2 * WIN), :]
    rows = lax.broadcasted_iota(jnp.int32, (2 * WIN, 2 * WIN), 0) & (WIN - 1)
    dist = (qs + rows) - (ks + lax.broadcasted_iota(jnp.int32, (2 * WIN, 2 * WIN), 1))
    return qs, ks, qb, kb, vb, (dist >= 0) & (dist < WIN)


def _stack2(x, kvh):
    return jnp.concatenate([x[:, HD * (2 * kvh):HD * (2 * kvh + 1)], x[:, HD * (2 * kvh + 1):HD * (2 * kvh + 2)]], axis=0)


def _sink2(sink_ref, kvh):
    top = lax.broadcasted_iota(jnp.int32, (2 * WIN, 1), 0) < WIN
    return jnp.where(top, sink_ref[2 * kvh], sink_ref[2 * kvh + 1])


def _swa_fwd(z, sinks, name):
    s = z.shape[0]

    def body(sink_ref, q_ref, k_ref, v_ref, y_ref, l_ref):
        def step(n, carry):
            qs, ks, qb, kb, vb, valid = _swa_block(q_ref, k_ref, v_ref, n)
            lse = jnp.zeros((WIN, 128), F32)
            for kvh in range(2):
                kv = slice(HD * kvh, HD * (kvh + 1))
                sc = jnp.where(valid, _dot(_stack2(qb, kvh) * SCALE, kb[:, kv], "nt"), NEG)
                sink = _sink2(sink_ref, kvh)
                m = jnp.maximum(jnp.max(sc, axis=-1, keepdims=True), sink)
                p = jnp.exp(sc - m)
                den = jnp.sum(p, axis=-1, keepdims=True) + jnp.exp(sink - m)
                o = _dot(p, vb[:, kv], "nn") / den
                lrow = m + jnp.log(den)
                for j in range(2):
                    h = 2 * kvh + j
                    y_ref[pl.ds(qs, WIN), HD * h:HD * (h + 1)] = o[WIN * j:WIN * (j + 1), :]
                    lse = _lane_put(lse, lrow[WIN * j:WIN * (j + 1), :], h)
            l_ref[pl.ds(qs, WIN), :] = lse
            return carry

        lax.fori_loop(0, s // WIN, step, 0)

    return pl.pallas_call(
        body, name=name, grid=(1,),
        in_specs=[pl.BlockSpec(memory_space=pltpu.SMEM), pl.BlockSpec((s, GW), lambda i: (0, Z_SQ)),
                  pl.BlockSpec((s, 128), lambda i: (0, Z_SK)), pl.BlockSpec((s, 128), lambda i: (0, Z_SV))],
        out_specs=[pl.BlockSpec((s, GW), lambda i: (0, 0)), pl.BlockSpec((s, 128), lambda i: (0, 0))],
        out_shape=[SDS((s, GW), F32), SDS((s, 128), F32)], compiler_params=_cp(("arbitrary",)))(sinks, z, z, z)


def _swa_bwd(z, sinks, lse, y, dy, name):
    s = z.shape[0]

    def body(sink_ref, q_ref, k_ref, v_ref, l_ref, y_ref, dy_ref, dq_ref, dk_ref, dv_ref, dsink_ref):
        dk_ref[...] = jnp.zeros_like(dk_ref)
        dv_ref[...] = jnp.zeros_like(dv_ref)
        dsink_ref[...] = jnp.zeros_like(dsink_ref)

        def step(n, carry):
            qs, ks, qb, kb, vb, valid = _swa_block(q_ref, k_ref, v_ref, n)
            lse_b = l_ref[pl.ds(qs, WIN), :]
            yb = y_ref[pl.ds(qs, WIN), :]
            dyb = dy_ref[pl.ds(qs, WIN), :].astype(F32)
            dsink = jnp.zeros((1, 128), F32)
            for kvh in range(2):
                kv = slice(HD * kvh, HD * (kvh + 1))
                q = _stack2(qb, kvh) * SCALE
                dy2 = _stack2(dyb, kvh)
                sc = jnp.where(valid, _dot(q, kb[:, kv], "nt"), NEG)
                lh = jnp.concatenate([lse_b[:, 2 * kvh:2 * kvh + 1], lse_b[:, 2 * kvh + 1:2 * kvh + 2]], axis=0)
                p = jnp.exp(sc - lh)
                dd = jnp.sum(dy2 * _stack2(yb, kvh), axis=-1, keepdims=True)
                ds = p * (_dot(dy2, vb[:, kv], "nt") - dd)
                dq = _dot(ds, kb[:, kv], "nn") * SCALE
                dk_ref[pl.ds(ks, 2 * WIN), kv] += _dot(ds, q, "tn")
                dv_ref[pl.ds(ks, 2 * WIN), kv] += _dot(p, dy2, "tn")
                dsk = jnp.exp(_sink2(sink_ref, kvh) - lh) * dd
                for j in range(2):
                    h = 2 * kvh + j
                    dq_ref[pl.ds(qs, WIN), HD * h:HD * (h + 1)] = dq[WIN * j:WIN * (j + 1), :]
                    dsink = _lane_put(dsink, dsink[:, h:h + 1] - jnp.sum(dsk[WIN * j:WIN * (j + 1), :], axis=0, keepdims=True), h)
            dsink_ref[...] += dsink
            return carry

        lax.fori_loop(0, s // WIN, step, 0)

    full = lambda w: pl.BlockSpec((s, w), lambda i: (0, 0))
    return pl.pallas_call(
        body, name=name, grid=(1,),
        in_specs=[pl.BlockSpec(memory_space=pltpu.SMEM), pl.BlockSpec((s, GW), lambda i: (0, Z_SQ)),
                  pl.BlockSpec((s, 128), lambda i: (0, Z_SK)), pl.BlockSpec((s, 128), lambda i: (0, Z_SV)),
                  full(128), full(GW), full(GW)],
        out_specs=[full(GW), full(128), full(128), pl.BlockSpec((1, 128), lambda i: (0, 0))],
        out_shape=[SDS((s, GW), F32), SDS((s, 128), F32), SDS((s, 128), F32), SDS((1, 128), F32)],
        compiler_params=_cp(("arbitrary",)))(sinks, z, z, z, lse, y, dy)


def _delayed(win, shift, halo):
    return win[halo:, :] if shift == 0 else pltpu.roll(win, shift, 0)[halo:, :]


def _prev_halo(width, halo, tm, col):
    return pl.BlockSpec((halo, width), lambda i: (jnp.maximum(i * (tm // halo) - 1, 0), col))


def _glu_window(a_ref, g_ref, ah_ref, gh_ref):
    keep = (pl.program_id(0) > 0).astype(F32)
    a = jnp.concatenate([ah_ref[...].astype(F32) * keep, a_ref[...].astype(F32)], axis=0)
    g = jnp.concatenate([gh_ref[...].astype(F32), g_ref[...].astype(F32)], axis=0)
    return a * _sigmoid(g)


def _conv_fwd(z, cw, cb, lg, lb, pw, pb, name):
    s = z.shape[0]
    tm = min(TM, s)

    def body(a_ref, g_ref, ah_ref, gh_ref, w_ref, b_ref, lg_ref, lb_ref, pw_ref, pb_ref, y_ref, hc_ref):
        hg = _glu_window(a_ref, g_ref, ah_ref, gh_ref)
        hc = jnp.zeros((tm, GW), F32) + b_ref[...]
        for k in range(CONV_K):
            hc = hc + w_ref[k:k + 1, :] * _delayed(hg, CONV_K - 1 - k, CONV_HALO)
        hc_ref[...] = hc
        xh, _ = _ln_stats(hc)
        y_ref[...] = _dot(_silu(xh * lg_ref[...] + lb_ref[...]), pw_ref[...], "nn") + pb_ref[...]

    tile = lambda col: pl.BlockSpec((tm, GW), lambda i: (i, col))
    whole = lambda a: pl.BlockSpec(a.shape, lambda i: (0, 0))
    return pl.pallas_call(
        body, name=name, grid=(s // tm,),
        in_specs=[tile(Z_CA), tile(Z_CG), _prev_halo(GW, CONV_HALO, tm, Z_CA), _prev_halo(GW, CONV_HALO, tm, Z_CG),
                  whole(cw), whole(cb), whole(lg), whole(lb), whole(pw), whole(pb)],
        out_specs=[tile(0), tile(0)], out_shape=[SDS((s, GW), F32), SDS((s, GW), F32)],
        compiler_params=_cp(("parallel",)))(z, z, z, z, cw, cb, lg, lb, pw, pb)


def _conv_bwd_a(z, hc, dy, cw, lg, lb, pw, name):
    s = z.shape[0]
    tm = min(TM, s)

    def body(a_ref, g_ref, ah_ref, gh_ref, hc_ref, dy_ref, lg_ref, lb_ref, pw_ref,
             dhc_ref, dpw_ref, dpb_ref, dlg_ref, dlb_ref, dcw_ref, dcb_ref):
        first = pl.program_id(0) == 0
        dy = dy_ref[...].astype(F32)
        xh, rstd = _ln_stats(hc_ref[...])
        hn = xh * lg_ref[...] + lb_ref[...]
        dhn = _dot(dy, pw_ref[...], "nt") * _dsilu(hn)
        dhc = _ln_bwd(xh, rstd, dhn * lg_ref[...])
        dhc_ref[...] = dhc
        _acc(dpw_ref, _dot(_silu(hn), dy, "tn"), first)
        _acc(dpb_ref, _colsum(dy), first)
        _acc(dlg_ref, _colsum(dhn * xh), first)
        _acc(dlb_ref, _colsum(dhn), first)
        _acc(dcb_ref, _colsum(dhc), first)
        hg = _glu_window(a_ref, g_ref, ah_ref, gh_ref)

        @pl.when(first)
        def _():
            dcw_ref[...] = jnp.zeros_like(dcw_ref)

        for k in range(CONV_K):
            dcw_ref[k:k + 1, :] += _colsum(dhc * _delayed(hg, CONV_K - 1 - k, CONV_HALO))

    tile = lambda col: pl.BlockSpec((tm, GW), lambda i: (i, col))
    whole = lambda shape: pl.BlockSpec(shape, lambda i: (0, 0))
    return pl.pallas_call(
        body, name=name, grid=(s // tm,),
        in_specs=[tile(Z_CA), tile(Z_CG), _prev_halo(GW, CONV_HALO, tm, Z_CA), _prev_halo(GW, CONV_HALO, tm, Z_CG),
                  tile(0), tile(0), whole(lg.shape), whole(lb.shape), whole(pw.shape)],
        out_specs=[tile(0), whole((GW, GW)), whole((1, GW)), whole((1, GW)), whole((1, GW)), whole((32, GW)), whole((1, GW))],
        out_shape=[SDS((s, GW), F32), SDS((GW, GW), F32), SDS((1, GW), F32), SDS((1, GW), F32), SDS((1, GW), F32),
                   SDS((32, GW), F32), SDS((1, GW), F32)],
        compiler_params=_cp(("arbitrary",)))(z, z, z, z, hc, dy, lg, lb, pw)


def _conv_bwd_b(z, dhc, cw, name):
    s = z.shape[0]
    tm = min(TM, s)
    nt = s // tm

    def body(a_ref, g_ref, d_ref, dn_ref, w_ref, da_ref, dg_ref):
        keep = (pl.program_id(0) < nt - 1).astype(F32)
        win = jnp.concatenate([d_ref[...], dn_ref[...] * keep], axis=0)
        dhg = jnp.zeros((tm, GW), F32)
        for k in range(CONV_K):
            sh = CONV_K - 1 - k
            dhg = dhg + w_ref[k:k + 1, :] * (win[:tm, :] if sh == 0 else pltpu.roll(win, tm + CONV_HALO - sh, 0)[:tm, :])
        sg = _sigmoid(g_ref[...].astype(F32))
        da_ref[...] = (dhg * sg).astype(BF16)
        dg_ref[...] = (dhg * a_ref[...].astype(F32) * sg * (1.0 - sg)).astype(BF16)

    tile = lambda col: pl.BlockSpec((tm, GW), lambda i: (i, col))
    nxt = pl.BlockSpec((CONV_HALO, GW), lambda i: (jnp.minimum((i + 1) * (tm // CONV_HALO), s // CONV_HALO - 1), 0))
    return pl.pallas_call(
        body, name=name, grid=(nt,),
        in_specs=[tile(Z_CA), tile(Z_CG), tile(0), nxt, pl.BlockSpec(cw.shape, lambda i: (0, 0))],
        out_specs=[tile(0), tile(0)], out_shape=[SDS((s, GW), BF16), SDS((s, GW), BF16)],
        compiler_params=_cp(("parallel",)))(z, z, dhc, dhc, cw)


def _sgu_chunk(zu, zv, lg, lb, wcat, bfull):
    u, v = _gelu(zu), _gelu(zv)
    xh, rstd = _ln_stats(v)
    vn = xh * lg + lb
    lane = lax.shift_right_logical(lax.broadcasted_iota(jnp.int32, (WIN, GW), 1), 6)
    r = jnp.concatenate([jnp.where(lane == g, vn, 0.0) for g in range(4)], axis=0)
    mix = _dot(wcat, r, "nn") + bfull
    return u, xh, rstd, r, mix, lane


def _tril4(w):
    t = lax.broadcasted_iota(jnp.int32, w.shape, 0)
    sidx = lax.broadcasted_iota(jnp.int32, w.shape, 1) & (WIN - 1)
    return jnp.where(sidx <= t, w, 0.0)


def _sgu_fwd(z, lg, lb, wcat, bfull, name):
    s = z.shape[0]
    tm = min(TM, s)

    def body(u_ref, v_ref, lg_ref, lb_ref, w_ref, b_ref, y_ref):
        w = _tril4(w_ref[...])
        for n in range(tm // WIN):
            rows = slice(WIN * n, WIN * (n + 1))
            u, _, _, _, mix, _ = _sgu_chunk(u_ref[rows, :].astype(F32), v_ref[rows, :].astype(F32), lg_ref[...], lb_ref[...], w, b_ref[...])
            y_ref[rows, :] = u * mix

    tile = lambda col: pl.BlockSpec((tm, GW), lambda i: (i, col))
    whole = lambda a: pl.BlockSpec(a.shape, lambda i: (0, 0))
    return pl.pallas_call(
        body, name=name, grid=(s // tm,), in_specs=[tile(Z_GU), tile(Z_GV), whole(lg), whole(lb), whole(wcat), whole(bfull)],
        out_specs=tile(0), out_shape=SDS((s, GW), F32), compiler_params=_cp(("parallel",)))(z, z, lg, lb, wcat, bfull)


def _sgu_bwd(z, dy, lg, lb, wcat, bfull, name):
    s = z.shape[0]
    tm = min(TM, s)

    def body(u_ref, v_ref, dy_ref, lg_ref, lb_ref, w_ref, b_ref, du_ref, dv_ref, dw_ref, db_ref, dlg_ref, dlb_ref):
        first = pl.program_id(0) == 0
        w = _tril4(w_ref[...])
        wt = w.T
        dw = jnp.zeros((WIN, 4 * WIN), F32)
        db = jnp.zeros((WIN, 128), F32)
        dlg = jnp.zeros((1, GW), F32)
        dlb = jnp.zeros((1, GW), F32)
        for n in range(tm // WIN):
            rows = slice(WIN * n, WIN * (n + 1))
            zu, zv, dout = u_ref[rows, :].astype(F32), v_ref[rows, :].astype(F32), dy_ref[rows, :].astype(F32)
            u, xh, rstd, r, mix, lane = _sgu_chunk(zu, zv, lg_ref[...], lb_ref[...], w, b_ref[...])
            dmix = dout * u
            du_ref[rows, :] = (dout * mix * _dgelu(zu)).astype(BF16)
            dw = dw + _dot(dmix, r, "nt")
            for g in range(4):
                db = _lane_put(db, db[:, g:g + 1] + jnp.sum(dmix[:, HD * g:HD * (g + 1)], axis=1, keepdims=True), g)
            dr = _dot(wt, dmix, "nn")
            dvn = jnp.zeros((WIN, GW), F32)
            for g in range(4):
                dvn = dvn + jnp.where(lane == g, dr[WIN * g:WIN * (g + 1), :], 0.0)
            dlg = dlg + _colsum(dvn * xh)
            dlb = dlb + _colsum(dvn)
            dv_ref[rows, :] = (_ln_bwd(xh, rstd, dvn * lg_ref[...]) * _dgelu(zv)).astype(BF16)
        _acc(dw_ref, _tril4(dw), first)
        _acc(db_ref, db, first)
        _acc(dlg_ref, dlg, first)
        _acc(dlb_ref, dlb, first)

    tile = lambda col: pl.BlockSpec((tm, GW), lambda i: (i, col))
    whole = lambda shape: pl.BlockSpec(shape, lambda i: (0, 0))
    return pl.pallas_call(
        body, name=name, grid=(s // tm,),
        in_specs=[tile(Z_GU), tile(Z_GV), tile(0), whole(lg.shape), whole(lb.shape), whole(wcat.shape), whole(bfull.shape)],
        out_specs=[tile(0), tile(0), whole((WIN, 4 * WIN)), whole((WIN, 128)), whole((1, GW)), whole((1, GW))],
        out_shape=[SDS((s, GW), BF16), SDS((s, GW), BF16), SDS((WIN, 4 * WIN), F32), SDS((WIN, 128), F32),
                   SDS((1, GW), F32), SDS((1, GW), F32)],
        compiler_params=_cp(("arbitrary",)))(z, z, dy, lg, lb, wcat, bfull)


def _conv3(win, w, b):
    return (w[2:3, :] * win[FFN_HALO:, :] + w[1:2, :] * pltpu.roll(win, 1, 0)[FFN_HALO:, :]
            + w[0:1, :] * pltpu.roll(win, 2, 0)[FFN_HALO:, :] + b)


def _ffn_specs(s, tm):
    main = pl.BlockSpec((2, None, tm, FF_BLK), lambda j, i: (0, j, i, 0))
    prev = pl.BlockSpec((2, None, FFN_HALO, FF_BLK), lambda j, i: (0, j, jnp.maximum(i * (tm // FFN_HALO) - 1, 0), 0))
    nxt = pl.BlockSpec((2, None, FFN_HALO, FF_BLK),
                       lambda j, i: (0, j, jnp.minimum((i + 1) * (tm // FFN_HALO), s // FFN_HALO - 1), 0))
    wsp = pl.BlockSpec((2, None, 3, FF_BLK), lambda j, i: (0, j, 0, 0))
    bsp = pl.BlockSpec((2, None, 1, FF_BLK), lambda j, i: (0, j, 0, 0))
    return main, prev, nxt, wsp, bsp


def _ffn_act(u4, w4, b4, name, dep=None):
    s = u4.shape[2]
    tm = min(TM, s)
    main, prev, _, wsp, bsp = _ffn_specs(s, tm)

    def body(u_ref, uh_ref, w_ref, b_ref, o_ref):
        keep = (pl.program_id(1) > 0).astype(F32)
        gw, vw = [jnp.concatenate([uh_ref[p].astype(F32) * keep, u_ref[p].astype(F32)], axis=0) for p in range(2)]
        o_ref[...] = (_silu(_conv3(gw, w_ref[0], b_ref[0])) * _conv3(vw, w_ref[1], b_ref[1])).astype(BF16)

    return _pcall(
        body, (u4, u4, w4, b4), dep, name=name, grid=(FF_NBLK, s // tm), in_specs=[main, prev, wsp, bsp],
        out_specs=pl.BlockSpec((None, tm, FF_BLK), lambda j, i: (j, i, 0)), out_shape=SDS((FF_NBLK, s, FF_BLK), BF16),
        compiler_params=_cp(("parallel", "parallel")))


def _ffn_bwd(u4, dact, w4, b4, name, dep=None):
    s = u4.shape[2]
    tm = min(TM, s)
    nt = s // tm
    main, prev, nxt, wsp, bsp = _ffn_specs(s, tm)
    dmain = pl.BlockSpec((None, tm, FF_BLK), lambda j, i: (j, i, 0))
    dnext = pl.BlockSpec((None, FFN_HALO, FF_BLK), lambda j, i: (j, jnp.minimum((i + 1) * (tm // FFN_HALO), s // FFN_HALO - 1), 0))
    ext = tm + FFN_HALO

    def body(u_ref, up_ref, un_ref, d_ref, dn_ref, w_ref, b_ref, du_ref, dw_ref, db_ref):
        i = pl.program_id(1)
        first = i == 0
        keep_prev = (i > 0).astype(F32)
        keep_next = (i < nt - 1).astype(F32)
        wins = [jnp.concatenate([up_ref[p].astype(F32) * keep_prev, u_ref[p].astype(F32), un_ref[p].astype(F32)], axis=0)
                for p in range(2)]
        gc = _conv3(wins[0], w_ref[0], b_ref[0])
        vc = _conv3(wins[1], w_ref[1], b_ref[1])
        d = jnp.concatenate([d_ref[...].astype(F32), dn_ref[...].astype(F32) * keep_next], axis=0)
        sg = _sigmoid(gc)
        duc = (d * vc * (sg * (1.0 + gc * (1.0 - sg))), d * (gc * sg))
        for p in range(2):
            w = w_ref[p]
            du_ref[p] = (w[2:3, :] * duc[p][:tm, :] + w[1:2, :] * pltpu.roll(duc[p], ext - 1, 0)[:tm, :]
                         + w[0:1, :] * pltpu.roll(duc[p], ext - 2, 0)[:tm, :]).astype(BF16)
            own = duc[p][:tm, :]
            taps = [_colsum(own * (wins[p] if k == 2 else pltpu.roll(wins[p], 2 - k, 0))[FFN_HALO:FFN_HALO + tm, :])
                    for k in range(3)]

            @pl.when(first)
            def _():
                db_ref[p] = _colsum(own)
                for k in range(3):
                    dw_ref[p, k:k + 1, :] = taps[k]

            @pl.when(jnp.logical_not(first))
            def _():
                db_ref[p] += _colsum(own)
                for k in range(3):
                    dw_ref[p, k:k + 1, :] += taps[k]

    return _pcall(
        body, (u4, u4, u4, dact, dact, w4, b4), dep, name=name, grid=(FF_NBLK, nt),
        in_specs=[main, prev, nxt, dmain, dnext, wsp, bsp], out_specs=[main, wsp, bsp],
        out_shape=[SDS(u4.shape, BF16), SDS((2, FF_NBLK, 3, FF_BLK), F32), SDS((2, FF_NBLK, 1, FF_BLK), F32)],
        compiler_params=_cp(("parallel", "arbitrary")))


def _sum8(parts, name):
    _, r, c = parts[0].shape
    tr = r
    for cand in (512, 256, 128, 64, 32, 16):
        if r % cand == 0 and r > cand:
            tr = cand
            break
    nb = r // tr

    def body(*refs):
        o_ref = refs[-1]
        for l, p_ref in enumerate(refs[:-1]):
            @pl.when(pl.program_id(0) == l)
            def _():
                acc = p_ref[0].astype(F32)
                for j in range(1, N_DEV):
                    acc = acc + p_ref[j].astype(F32)
                o_ref[...] = acc

    def spec(l):
        return pl.BlockSpec((N_DEV, tr, c), lambda ll, i: (0, jnp.where(ll == l, i, jnp.where(ll < l, 0, nb - 1)), 0))

    return pl.pallas_call(
        body, name=name, grid=(len(parts), nb), in_specs=[spec(l) for l in range(len(parts))],
        out_specs=pl.BlockSpec((None, tr, c), lambda ll, i: (ll, i, 0)), out_shape=SDS((len(parts), r, c), F32),
        compiler_params=_cp(("arbitrary", "arbitrary")))(*parts)


def _sum8_small(parts, name):
    n = len(parts)

    def body(*refs):
        for p_ref, o_ref in zip(refs[:n], refs[n:]):
            acc = p_ref[0]
            for j in range(1, N_DEV):
                acc = acc + p_ref[j]
            o_ref[...] = acc

    return pl.pallas_call(body, name=name, out_shape=[SDS(p.shape[1:], F32) for p in parts], compiler_params=_cp())(*parts)


def _adamw_math(w, g, m, v):
    m = ADAM_B1 * m + (1.0 - ADAM_B1) * g
    v = ADAM_B2 * v + (1.0 - ADAM_B2) * (g * g)
    m_hat = m / (1.0 - ADAM_B1 ** ADAM_STEP)
    v_hat = v / (1.0 - ADAM_B2 ** ADAM_STEP)
    return -ADAM_LR * (m_hat / (jnp.sqrt(v_hat) + ADAM_EPS) + ADAM_WD * w), m, v


def _adamw(w, g, m, v, name):
    r, c = w.shape
    tr = r
    for cand in (256, 128, 64):
        if r % cand == 0 and r > cand:
            tr = cand
            break

    def body(w_ref, g_ref, m_ref, v_ref, d_ref, mo_ref, vo_ref):
        d_ref[...], mo_ref[...], vo_ref[...] = _adamw_math(w_ref[...], g_ref[...], m_ref[...], v_ref[...])

    blk = pl.BlockSpec((tr, c), lambda i: (i, 0))
    return pl.pallas_call(body, name=name, grid=(r // tr,), in_specs=[blk] * 4, out_specs=[blk] * 3,
                          out_shape=[SDS((r, c), F32)] * 3, compiler_params=_cp(("parallel",)))(w, g, m, v)


def _adamw_small(ws, gs, ms, vs, name):
    n = len(ws)

    def body(*refs):
        ins, outs = refs[:4 * n], refs[4 * n:]
        for i in range(n):
            d, m, v = _adamw_math(ins[i][...], ins[n + i][...], ins[2 * n + i][...], ins[3 * n + i][...])
            outs[i][...], outs[n + i][...], outs[2 * n + i][...] = d, m, v

    shapes = [SDS(w.shape, F32) for w in ws]
    res = pl.pallas_call(body, name=name, out_shape=shapes * 3, compiler_params=_cp())(*ws, *gs, *ms, *vs)
    return res[:n], res[n:2 * n], res[2 * n:]


def _perm_in(w):
    pad = jnp.zeros(w.shape[:-1] + (ZW - 2308,), w.dtype)
    return jnp.concatenate([w[..., :768], w[..., 772:], w[..., 768:772], pad], axis=-1)


def _unperm_in(g):
    return jnp.concatenate([g[..., :768], g[..., 2304:2308], g[..., 768:2304]], axis=-1)


def _wcat(sgu_w):
    return sgu_w.transpose(1, 0, 2).reshape(WIN, 4 * WIN)


def _layer_fwd(l, x, h1, mod, p, wg, last, target, nxt, w_in_next):
    s = x.shape[0]
    tag = f"_l{l}"
    mrow = lambda k: (mod, 6 * l + k)
    z = _mm_rows(h1, wg["w_in"].get(h1), "nn", ZW, BF16, "mm_z" + tag)
    fcol, frow = _fgate(z, p["bf"], "fgate" + tag, dep=wg["w_out"].prefetch(z))
    y_fox, lse_fox = _fox_fwd(z, fcol, frow, "fox_fwd" + tag)
    y_conv, hc = _conv_fwd(z, wg["conv_w"], p["conv_b"], p["conv_ln_g"], p["conv_ln_b"], wg["conv_pw_w"], p["conv_pw_b"],
                           "conv_fwd" + tag)
    y_swa, lse_swa = _swa_fwd(z, p["sinks"], "swa_fwd" + tag)
    y_sgu = _sgu_fwd(z, p["sgu_ln_g"], p["sgu_ln_b"], p["wcat"], p["bfull"], "sgu_fwd" + tag)
    ys = (y_fox, y_conv, y_swa, y_sgu)
    yn = _gnorm(ys, (p["g_group"], l), "gnorm" + tag)
    tok = wg["w_up"].prefetch(yn)
    o = _mm_rows(yn, wg["w_out"].get(yn), "nn", D, BF16, "mm_o" + tag)
    x1, h2 = _post(x, o, mrow(2), (p["g_post_mix"], l), (p["g_pre_ffn"], l), mrow(4), mrow(3), "post_mix" + tag, dep=tok)
    tok = wg["w_down"].prefetch(h2)
    u = _matmul(h2, wg["w_up"].get(h2), "nn", (N_DEV, s, FF_BLK), BF16, (N_DEV, 1, 1),
                _bs((s, D), lambda j, i, k: (0, 0)), _bs((None, D, FF_BLK), lambda j, i, k: (j, 0, 0)),
                _bs((None, s, FF_BLK), lambda j, i, k: (j, 0, 0)), None, "mm_u" + tag)
    u4 = u.reshape(2, FF_NBLK, s, FF_BLK)
    act = _ffn_act(u4, wg["ffn_conv_w"], p["ffn_conv_b"], "ffn_act" + tag, dep=tok)
    tok = None if w_in_next is None else w_in_next.prefetch(act)
    f = _matmul(act, wg["w_down"].get(act), "nn", (s, D), BF16, (1, 1, FF_NBLK),
                _bs((None, s, FF_BLK), lambda i, j, k: (k, 0, 0)), _bs((FF_BLK, D), lambda i, j, k: (k, 0)),
                _bs((s, D), lambda i, j, k: (0, 0)), (s, D), "mm_f" + tag)
    if last:
        out = _post_loss(x1, f, mrow(5), (p["g_post_ffn"], l), target, "post_loss")
    else:
        out = _post(x1, f, mrow(5), (p["g_post_ffn"], l), *nxt, "post_ffn" + tag, dep=tok)
    saved = dict(x=x, h1=h1, z=z, fcol=fcol, frow=frow, lse_fox=lse_fox, hc=hc, lse_swa=lse_swa, ys=ys, yn=yn, o=o, x1=x1,
                 h2=h2, u4=u4, act=act, f=f)
    return out, saved


def _tie(a, token):
    return a if token is None else a + token[0, 0]


def _layer_bwd(l, dx2, sv, mod, p, wg, emit):
    s = dx2.shape[0]
    tm = min(TM, s)
    tag = f"_l{l}"
    mrow = lambda k: (mod, 6 * l + k)
    g = {}
    df, g["ga2"], g["g_post_ffn"] = _post_bwd(dx2, sv["f"], mrow(5), (p["g_post_ffn"], l), "post_ffn_bwd" + tag)
    dact = _matmul(df, wg["w_down"].get(None), "nt", (FF_NBLK, s, FF_BLK), BF16, (FF_NBLK, 1, 1),
                   _bs((s, D), lambda j, i, k: (0, 0)), _bs((FF_BLK, D), lambda j, i, k: (j, 0)),
                   _bs((None, s, FF_BLK), lambda j, i, k: (j, 0, 0)), None, "mm_dact" + tag)
    tok = emit("w_down", _matmul(sv["act"], df, "tn", (FF_NBLK * FF_BLK, D), BF16, (FF_NBLK, 1, 1),
                                 _bs((None, s, FF_BLK), lambda j, i, k: (j, 0, 0)), _bs((s, D), lambda j, i, k: (0, 0)),
                                 _bs((FF_BLK, D), lambda j, i, k: (j, 0)), None, "mm_dwdown" + tag))
    du, g["ffn_conv_w"], g["ffn_conv_b"] = _ffn_bwd(sv["u4"], dact, wg["ffn_conv_w"], p["ffn_conv_b"], "ffn_bwd" + tag, dep=tok)
    du = du.reshape(N_DEV, s, FF_BLK)
    dh2 = _matmul(du, wg["w_up"].get(None), "nt", (s, D), BF16, (1, 1, N_DEV),
                  _bs((None, s, FF_BLK), lambda i, j, k: (k, 0, 0)), _bs((None, D, FF_BLK), lambda i, j, k: (k, 0, 0)),
                  _bs((s, D), lambda i, j, k: (0, 0)), (s, D), "mm_dh2" + tag)
    tok = emit("w_up", _matmul(du, sv["h2"], "tn", (N_DEV, FF_BLK, D), BF16, (N_DEV, 1, 1),
                               _bs((None, s, FF_BLK), lambda j, i, k: (j, 0, 0)), _bs((s, D), lambda j, i, k: (0, 0)),
                               _bs((None, FF_BLK, D), lambda j, i, k: (j, 0, 0)), None, "mm_dwup" + tag))
    dx1, g["sh2"], g["sc2"], g["g_pre_ffn"] = _pre_bwd(dh2, sv["x1"], dx2, (p["g_pre_ffn"], l), mrow(4), "pre_ffn_bwd" + tag,
                                                       dep=tok)
    do, g["ga1"], g["g_post_mix"] = _post_bwd(dx1, sv["o"], mrow(2), (p["g_post_mix"], l), "post_mix_bwd" + tag)
    dyn = _mm_rows(do, wg["w_out"].get(None), "nt", D, BF16, "mm_dyn" + tag)
    tok = emit("w_out", _mm_wgrad(sv["yn"], do, BF16, "mm_dwout" + tag))
    dy_fox, dy_conv, dy_swa, dy_sgu, g["g_group"] = _gnorm_bwd(dyn, sv["ys"], (p["g_group"], l), "gnorm_bwd" + tag, dep=tok)
    z = sv["z"]
    dq_f, dk_f, dv_f, dfrow, dfcol = _fox_bwd(z, sv["fcol"], sv["frow"], sv["lse_fox"], sv["ys"][0], dy_fox, "fox_bwd" + tag)
    dgate, g["bf"] = _fgate_bwd(z, p["bf"], dfrow, dfcol, "fgate_bwd" + tag)
    dhc, g["conv_pw_w"], g["conv_pw_b"], g["conv_ln_g"], g["conv_ln_b"], g["conv_w"], g["conv_b"] = _conv_bwd_a(
        z, sv["hc"], dy_conv, wg["conv_w"], p["conv_ln_g"], p["conv_ln_b"], wg["conv_pw_w"], "conv_bwd_a" + tag)
    da_c, dg_c = _conv_bwd_b(z, dhc, wg["conv_w"], "conv_bwd_b" + tag)
    dq_s, dk_s, dv_s, g["sinks"] = _swa_bwd(z, p["sinks"], sv["lse_swa"], sv["ys"][2], dy_swa, "swa_bwd" + tag)
    du_g, dv_g, g["wcat"], g["sgu_bcol"], g["sgu_ln_g"], g["sgu_ln_b"] = _sgu_bwd(
        z, dy_sgu, p["sgu_ln_g"], p["sgu_ln_b"], p["wcat"], p["bfull"], "sgu_bwd" + tag)
    dz = jnp.concatenate([dq_f.astype(BF16), dk_f.astype(BF16), dv_f.astype(BF16), da_c, dg_c, dq_s.astype(BF16), dk_s.astype(BF16),
                          dv_s.astype(BF16), du_g, dv_g, dgate], axis=1)
    dh1 = _mm_rows(dz, wg["w_in"].get(None), "nt", D, BF16, "mm_dh1" + tag)
    tok = emit("w_in", _mm_wgrad(sv["h1"], dz, BF16, "mm_dwin" + tag))
    dx, g["sh1"], g["sc1"], g["g_pre_mix"] = _pre_bwd(dh1, sv["x"], dx1, (p["g_pre_mix"], l), mrow(1), "pre_mix_bwd" + tag,
                                                      dep=tok)
    return dx, g


def _layer_params(l, small, conv_w_full, conv_pw_full, ffn_conv_w_full):
    bf = jnp.pad(small["b_fgate"][l][None, :], ((0, 0), (0, 124)))
    p = dict(
        bf=bf, conv_b=small["conv_b"][l][None], conv_ln_g=small["conv_ln_g"][l][None], conv_ln_b=small["conv_ln_b"][l][None],
        conv_pw_b=small["conv_pw_b"][l][None], sinks=small["swa_sinks"][l], sgu_ln_g=small["sgu_ln_g"][l][None],
        sgu_ln_b=small["sgu_ln_b"][l][None], wcat=_wcat(small["sgu_w"][l]),
        bfull=jnp.repeat(small["sgu_b"][l].T, HD, axis=1),
        ffn_conv_b=small["ffn_conv_b"][l].reshape(2, FF_NBLK, 1, FF_BLK),
        g_group=small["g_group"].reshape(N_LAYER, 1, D), g_post_mix=small["g_post_mix"].reshape(N_LAYER, 1, D),
        g_pre_ffn=small["g_pre_ffn"].reshape(N_LAYER, 1, D), g_post_ffn=small["g_post_ffn"].reshape(N_LAYER, 1, D),
        g_pre_mix=small["g_pre_mix"].reshape(N_LAYER, 1, D))
    wsmall = dict(conv_w=conv_w_full[l], conv_pw_w=conv_pw_full[l].astype(BF16),
                  ffn_conv_w=ffn_conv_w_full[l].reshape(3, 2, FF_NBLK, FF_BLK).transpose(1, 2, 0, 3))
    return p, wsmall


def _local_step(x, target, mod, small, wbig, conv_w_full, conv_pw_full, ffn_conv_w_full, emit):
    ps, wgs = [], []
    for l in range(N_LAYER):
        p, wsmall = _layer_params(l, small, conv_w_full, conv_pw_full, ffn_conv_w_full)
        ps.append(p)
        wgs.append({**wbig[l], **wsmall})
    h = _rms_mod(x, (ps[0]["g_pre_mix"], 0), (mod, 1), (mod, 0), "rms_mod_l0")
    saved = []
    for l in range(N_LAYER):
        last = l == N_LAYER - 1
        nxt = None if last else ((ps[l]["g_pre_mix"], l + 1), (mod, 6 * (l + 1) + 1), (mod, 6 * (l + 1)))
        out, sv = _layer_fwd(l, x, h, mod, ps[l], wgs[l], last, target, nxt, None if last else wgs[l + 1]["w_in"])
        saved.append(sv)
        if not last:
            x, h = out
    dx, loss = out
    grads = [None] * N_LAYER
    for l in reversed(range(N_LAYER)):
        dx, grads[l] = _layer_bwd(l, dx, saved[l], mod, ps[l], wgs[l], functools.partial(emit, l))
    return loss, dx, grads


_SMALL = ("b_ada", "g_pre_mix", "g_post_mix", "g_pre_ffn", "g_post_ffn", "b_fgate", "conv_b", "conv_ln_g", "conv_ln_b",
          "conv_pw_b", "swa_sinks", "sgu_ln_g", "sgu_ln_b", "sgu_w", "sgu_b", "g_group", "ffn_conv_b")
_WEIGHTS = ("w_ada", "b_ada", "g_pre_mix", "g_post_mix", "g_pre_ffn", "g_post_ffn", "w_in", "b_fgate", "conv_w", "conv_b",
            "conv_ln_g", "conv_ln_b", "conv_pw_w", "conv_pw_b", "swa_sinks", "sgu_ln_g", "sgu_ln_b", "sgu_w", "sgu_b",
            "g_group", "w_out", "ffn_w_up", "ffn_conv_w", "ffn_conv_b", "ffn_w_down")


def _pad_rows(a, mult):
    r = (-a.shape[0]) % mult
    return a if r == 0 else jnp.concatenate([a, jnp.zeros((r,) + a.shape[1:], a.dtype)], axis=0)


def _view2d(a):
    if a.ndim == 2:
        return a
    return a.reshape(-1, a.shape[-1])


def kernel(x, c, w_ada, b_ada, g_pre_mix, g_post_mix, g_pre_ffn, g_post_ffn, w_in, b_fgate, conv_w, conv_b, conv_ln_g, conv_ln_b, conv_pw_w, conv_pw_b, swa_sinks, sgu_ln_g, sgu_ln_b, sgu_w, sgu_b, g_group, w_out, ffn_w_up, ffn_conv_w, ffn_conv_b, ffn_w_down, loss_target, m_w_ada, m_b_ada, m_g_pre_mix, m_g_post_mix, m_g_pre_ffn, m_g_post_ffn, m_w_in, m_b_fgate, m_conv_w, m_conv_b, m_conv_ln_g, m_conv_ln_b, m_conv_pw_w, m_conv_pw_b, m_swa_sinks, m_sgu_ln_g, m_sgu_ln_b, m_sgu_w, m_sgu_b, m_g_group, m_w_out, m_ffn_w_up, m_ffn_conv_w, m_ffn_conv_b, m_ffn_w_down, v_w_ada, v_b_ada, v_g_pre_mix, v_g_post_mix, v_g_pre_ffn, v_g_post_ffn, v_w_in, v_b_fgate, v_conv_w, v_conv_b, v_conv_ln_g, v_conv_ln_b, v_conv_pw_w, v_conv_pw_b, v_swa_sinks, v_sgu_ln_g, v_sgu_ln_b, v_sgu_w, v_sgu_b, v_g_group, v_w_out, v_ffn_w_up, v_ffn_conv_w, v_ffn_conv_b, v_ffn_w_down):
    env = dict(locals())
    w = {n: env[n] for n in _WEIGHTS}
    mom = {n: env["m_" + n] for n in _WEIGHTS}
    var = {n: env["v_" + n] for n in _WEIGHTS}
    me = 4 * lax.axis_index("x") + 2 * lax.axis_index("y") + lax.axis_index("c")
    x2, target = x[0], loss_target[0]

    (c_all,) = _exchange([c], ["bcast"], "gather_c")
    c_all = c_all.reshape(N_DEV, D)
    (m_all,) = _exchange([_ada_fwd(c_all, w_ada)], ["bcast"], "gather_mod")
    m_mine = lax.dynamic_index_in_dim(m_all, me, axis=2, keepdims=False)
    mod, mod_token = _ada_finish(m_mine.transpose(1, 0, 2).reshape(N_LAYER, 6 * D), b_ada)
    mod = mod.reshape(6 * N_LAYER, 1, D)

    shards = [_tie(conv_w, mod_token), conv_pw_w, ffn_conv_w]
    for l in range(N_LAYER):
        shards += [_perm_in(w_in[l]).astype(BF16), w_out[l].astype(BF16), ffn_w_up[l].astype(BF16), ffn_w_down[l].astype(BF16)]
    gather = _g2_start(shards, "gather_weights_start")
    mod = _tie(mod, gather.token)
    _g2_relay(gather, [0, 1, 2, 3], mod, "gather_relay_first")
    g_cw, g_pw, g_fcw = _g2_wait(gather, [0, 1, 2], mod, "gather_small_wait")
    conv_w_full = g_cw.transpose(1, 2, 0, 3).reshape(N_LAYER, CONV_K, GW)
    conv_pw_full = g_pw.transpose(1, 0, 2, 3).reshape(N_LAYER, GW, GW)
    ffn_conv_w_full = g_fcw.transpose(1, 2, 0, 3).reshape(N_LAYER, 3, N_DEV * FF_BLK)

    def lazy(i, shape, key):
        pre = None if i == 3 else (lambda after: _g2_relay(gather, [i], after, "relay_" + key))
        return _Lazy(lambda after: _g2_wait(gather, [i], after, "wait_" + key)[0].reshape(shape), pre)

    wbig = [dict(w_in=lazy(3 + 4 * l, (D, ZW), f"w_in_l{l}"), w_out=lazy(4 + 4 * l, (D, D), f"w_out_l{l}"),
                 w_up=lazy(5 + 4 * l, (N_DEV, D, FF_BLK), f"w_up_l{l}"),
                 w_down=lazy(6 + 4 * l, (FF_NBLK * FF_BLK, D), f"w_down_l{l}")) for l in range(N_LAYER)]

    grad_flights = []

    def emit(l, key, arr):
        fl = _xchg_start([arr.reshape(N_DEV, -1, arr.shape[-1])], ["a2a"], f"grad_start_{key}_l{l}")
        grad_flights.append(((l, key), fl))
        return fl.token

    small = {n: w[n] for n in _SMALL}
    loss8, dx, grads = _local_step(x2, target, mod, small, wbig, conv_w_full, conv_pw_full, ffn_conv_w_full, emit)
    loss = lax.psum(loss8[0, 0], ("x", "y", "c"))
    grad_x = dx[None]


    st = lambda key: jnp.stack([grads[l][key] for l in range(N_LAYER)])
    d_conv_w = st("conv_w")[:, :CONV_K, :].reshape(N_LAYER, CONV_K, N_DEV, GW // N_DEV).transpose(2, 0, 1, 3)
    d_pw_w = st("conv_pw_w").reshape(N_LAYER, N_DEV, GW // N_DEV, GW).transpose(1, 0, 2, 3)
    d_fcw = st("ffn_conv_w").reshape(N_LAYER, N_DEV, 3, FF_BLK).transpose(1, 0, 2, 3)
    rows_d = _pad_rows(jnp.concatenate(
        [grads[l][k] for l in range(N_LAYER) for k in ("sh1", "sc1", "ga1", "sh2", "sc2", "ga2")]
        + [grads[l][k] for k in ("g_pre_mix", "g_post_mix", "g_pre_ffn", "g_post_ffn", "g_group") for l in range(N_LAYER)],
        axis=0), 8)
    rows_gw = _pad_rows(jnp.concatenate(
        [grads[l][k] for k in ("conv_b", "conv_ln_g", "conv_ln_b", "conv_pw_b", "sgu_ln_g", "sgu_ln_b") for l in range(N_LAYER)],
        axis=0), 8)
    rows_128 = jnp.concatenate([_pad_rows(jnp.concatenate([grads[l]["bf"] for l in range(N_LAYER)]
                                                          + [grads[l]["sinks"] for l in range(N_LAYER)], axis=0), 8)]
                               + [grads[l]["sgu_bcol"] for l in range(N_LAYER)], axis=0)
    rows_w = jnp.concatenate([grads[l]["wcat"] for l in range(N_LAYER)], axis=0)
    rows_fb = st("ffn_conv_b").reshape(N_LAYER * N_DEV, FF_BLK)
    small_flight = _xchg_start([d_conv_w, d_pw_w, d_fcw, rows_d, rows_gw, rows_128, rows_w, rows_fb],
                               ["a2a"] * 3 + ["bcast"] * 5, "small_grads_start")

    to_mem = {"w_in": lambda a: a.transpose(2, 0, 1), "ffn_w_up": lambda a: a.transpose(0, 2, 1)}
    from_mem = {"w_in": lambda a: a.transpose(1, 2, 0), "ffn_w_up": lambda a: a.transpose(0, 2, 1)}
    flights = dict(grad_flights)
    gr, delta, new_m, new_v = {}, {}, {}, {}

    def adamw_big(n):
        view, back = to_mem.get(n, lambda a: a), from_mem.get(n, lambda a: a)
        shape = view(w[n]).shape
        d, m2, v2 = _adamw(_view2d(view(w[n])), _view2d(gr[n]), _view2d(view(mom[n])), _view2d(view(var[n])), "adamw_" + n)
        delta[n], new_m[n], new_v[n] = back(d.reshape(shape)), back(m2.reshape(shape)), back(v2.reshape(shape))
        gr[n] = back(gr[n].reshape(shape))

    after = small_flight.token
    for n, key in (("ffn_w_down", "w_down"), ("ffn_w_up", "w_up"), ("w_out", "w_out"), ("w_in", "w_in")):
        parts = [_xchg_wait(flights[(l, key)], [0], after, f"grad_wait_{key}_l{l}")[0] for l in reversed(range(N_LAYER))]
        g = _sum8(parts[::-1], "sum_" + key)
        gr[n] = to_mem["w_in"](_unperm_in(g)) if n == "w_in" else g
        adamw_big(n)
        after = new_v[n]

    small_parts = _xchg_wait(small_flight, list(range(8)), after, "small_grads_wait")
    s_conv_w, s_pw_w, s_fcw, s_d, s_gw, s_128, s_w, s_fb = _sum8_small(
        [p.reshape(N_DEV, -1, p.shape[-1]) for p in small_parts], "sum_small_grads")
    gr["conv_w"] = s_conv_w.reshape(N_LAYER, CONV_K, GW // N_DEV)
    gr["conv_pw_w"] = s_pw_w.reshape(N_LAYER, GW // N_DEV, GW)
    gr["ffn_conv_w"] = s_fcw.reshape(N_LAYER, 3, FF_BLK)
    gr["b_ada"] = s_d[:6 * N_LAYER].reshape(N_LAYER, 6 * D)
    for i, k in enumerate(("g_pre_mix", "g_post_mix", "g_pre_ffn", "g_post_ffn", "g_group")):
        gr[k] = s_d[6 * N_LAYER + 2 * i:6 * N_LAYER + 2 * i + 2]
    for i, k in enumerate(("conv_b", "conv_ln_g", "conv_ln_b", "conv_pw_b", "sgu_ln_g", "sgu_ln_b")):
        gr[k] = s_gw[2 * i:2 * i + 2]
    gr["b_fgate"] = s_128[0:2, :4]
    gr["swa_sinks"] = s_128[2:4, :4]
    gr["sgu_b"] = s_128[8:].reshape(N_LAYER, WIN, 128)[:, :, :4].transpose(0, 2, 1)
    gr["sgu_w"] = s_w.reshape(N_LAYER, WIN, 4, WIN).transpose(0, 2, 1, 3)
    gr["ffn_conv_b"] = s_fb.reshape(N_LAYER, N_DEV * FF_BLK)
    dmod_all = small_parts[3][:, :6 * N_LAYER, :].reshape(N_DEV, N_LAYER, 6 * D)
    ncol = 6 * D // N_DEV
    dmod_cols = lax.dynamic_slice_in_dim(dmod_all, me * ncol, ncol, axis=2).transpose(1, 0, 2)
    gr["w_ada"] = _ada_bwd(c_all, dmod_cols)

    adamw_big("w_ada")
    smalls = [n for n in _WEIGHTS if n not in ("w_ada", "w_in", "w_out", "ffn_w_up", "ffn_w_down")]
    ds, ms, vs = _adamw_small([_view2d(w[n]) for n in smalls], [_view2d(gr[n]) for n in smalls],
                              [_view2d(mom[n]) for n in smalls], [_view2d(var[n]) for n in smalls], "adamw_small")
    for i, n in enumerate(smalls):
        delta[n], new_m[n], new_v[n] = ds[i].reshape(w[n].shape), ms[i].reshape(w[n].shape), vs[i].reshape(w[n].shape)

    return (loss, grad_x, *[gr[n].reshape(w[n].shape) for n in _WEIGHTS], *[delta[n] for n in _WEIGHTS],
            *[new_m[n] for n in _WEIGHTS], *[new_v[n] for n in _WEIGHTS])
```

```python
import functools

import jax
import jax.numpy as jnp
from jax import lax
from jax.experimental import pallas as pl
from jax.experimental.pallas import tpu as pltpu

F32, BF16 = jnp.float32, jnp.bfloat16
SDS = jax.ShapeDtypeStruct
MESH = pl.DeviceIdType.MESH

N_DEV = 8
D = 1024
GW = 256
HD = 64
N_LAYER = 2
ZW = 2432
FF_BLK = 704
FF_NBLK = 4
CONV_K = 31
CONV_HALO = 32
FFN_HALO = 16
EPS = 1e-6
NEG = -1e30
SCALE = HD ** -0.5
VMEM_LIMIT_V7X = 56 * 1024 * 1024
TM = 512
WGRAD_ROWS = 256
TQ = 256
WIN = 128

ADAM_LR, ADAM_B1, ADAM_B2, ADAM_EPS, ADAM_WD, ADAM_STEP = 0.001, 0.9, 0.999, 1e-08, 0.01, 10

Z_FQ, Z_FK, Z_FV, Z_CA, Z_CG, Z_SQ = 0, 1, 2, 3, 4, 5
Z_SK, Z_SV = 12, 13
Z_GU, Z_GV = 7, 8
Z_FG = 18


def _cp(sem=None):
    return pltpu.CompilerParams(dimension_semantics=sem, vmem_limit_bytes=VMEM_LIMIT_V7X)


def _vec(arr3, idx, ngrid):
    w = arr3.shape[-1]
    if ngrid == 1:
        return pl.BlockSpec((None, 1, w), lambda i: (idx, 0, 0))
    return pl.BlockSpec((None, 1, w), lambda i, j: (idx, 0, 0))


def _sigmoid(x):
    return jax.nn.sigmoid(x)


def _silu(x):
    return x * _sigmoid(x)


def _dsilu(x):
    s = _sigmoid(x)
    return s * (1.0 + x * (1.0 - s))


_G0, _G1 = 0.7978845608028654, 0.044715


def _gelu(x):
    return 0.5 * x * (1.0 + jnp.tanh(_G0 * (x + _G1 * x * x * x)))


def _dgelu(x):
    t = jnp.tanh(_G0 * (x + _G1 * x * x * x))
    return 0.5 * (1.0 + t) + 0.5 * x * (1.0 - t * t) * (_G0 * (1.0 + 3.0 * _G1 * x * x))


def _rstd(x):
    return lax.rsqrt(jnp.mean(x * x, axis=-1, keepdims=True) + EPS)


def _rms_bwd(xh, r, t):
    return r * (t - xh * jnp.mean(t * xh, axis=-1, keepdims=True))


def _ln_stats(x):
    mu = jnp.mean(x, axis=-1, keepdims=True)
    xc = x - mu
    rstd = lax.rsqrt(jnp.mean(xc * xc, axis=-1, keepdims=True) + EPS)
    return xc * rstd, rstd


def _ln_bwd(xh, rstd, dxh):
    return rstd * (dxh - jnp.mean(dxh, axis=-1, keepdims=True) - xh * jnp.mean(dxh * xh, axis=-1, keepdims=True))


def _colsum(x):
    return jnp.sum(x, axis=0, keepdims=True)


def _dot(a, b, kind):
    dn = {"nn": (((1,), (0,)), ((), ())), "nt": (((1,), (1,)), ((), ())), "tn": (((0,), (0,)), ((), ()))}[kind]
    return lax.dot_general(a.astype(BF16), b.astype(BF16), dn, preferred_element_type=F32)


def _exchange(arrs, modes, name):
    n = len(arrs)
    outs = [SDS((N_DEV,) + a.shape, a.dtype) if m == "bcast" else SDS(a.shape, a.dtype) for a, m in zip(arrs, modes)]

    def body(*refs):
        ins, dst = refs[:n], refs[n:2 * n]
        send, recv, loc = refs[2 * n:]
        x, y, c = lax.axis_index("x"), lax.axis_index("y"), lax.axis_index("c")
        me = 4 * x + 2 * y + c

        def src(a, j):
            return ins[a] if modes[a] == "bcast" else ins[a].at[j]

        local = [pltpu.make_async_copy(src(a, me), dst[a].at[me], loc.at[a]) for a in range(n)]
        for cp in local:
            cp.start()
        sent, landed = [], []
        for k in (2, 4, 6, 3, 5, 7, 1):
            px = 1 - x if k & 4 else x
            py = 1 - y if k & 2 else y
            pc = 1 - c if k & 1 else c
            peer = 4 * px + 2 * py + pc
            for a in range(n):
                cp = pltpu.make_async_remote_copy(src_ref=src(a, peer), dst_ref=dst[a].at[me], send_sem=send.at[a, k - 1],
                                                  recv_sem=recv.at[a, k - 1], device_id=(px, py, pc), device_id_type=MESH)
                cp.start()
                sent.append(cp)
                landed.append(pltpu.make_async_remote_copy(src_ref=src(a, peer), dst_ref=dst[a].at[peer],
                                                           send_sem=send.at[a, k - 1], recv_sem=recv.at[a, k - 1],
                                                           device_id=(px, py, pc), device_id_type=MESH))
        for cp in landed:
            cp.wait_recv()
        for cp in sent:
            cp.wait_send()
        for cp in local:
            cp.wait()

    hbm = pl.BlockSpec(memory_space=pltpu.HBM)
    return pl.pallas_call(
        body, name=name, out_shape=outs, in_specs=[hbm] * n, out_specs=[hbm] * n,
        scratch_shapes=[pltpu.SemaphoreType.DMA((n, N_DEV - 1)), pltpu.SemaphoreType.DMA((n, N_DEV - 1)),
                        pltpu.SemaphoreType.DMA((n,))],
        compiler_params=pltpu.CompilerParams(has_side_effects=True),
    )(*arrs)


_PEER_ORDER = (2, 4, 6, 3, 5, 7, 1)
_HBM = pl.BlockSpec(memory_space=pltpu.HBM)
_SEM = pl.BlockSpec(memory_space=pltpu.SEMAPHORE)
_EFFECT = pltpu.SideEffectType.DATAFLOW_SIDE_EFFECTING


def _peer(k):
    x, y, c = lax.axis_index("x"), lax.axis_index("y"), lax.axis_index("c")
    px = 1 - x if k & 4 else x
    py = 1 - y if k & 2 else y
    pc = 1 - c if k & 1 else c
    return (px, py, pc), 4 * px + 2 * py + pc


def _my_id():
    return 4 * lax.axis_index("x") + 2 * lax.axis_index("y") + lax.axis_index("c")


def _split_copies(src_ref, land_ref, send, recv, loc, mode):
    me = _my_id()
    pick = (lambda j: src_ref) if mode == "bcast" else (lambda j: src_ref.at[j])
    local = pltpu.make_async_copy(pick(me), land_ref.at[me], loc)
    remote = []
    for k in _PEER_ORDER:
        dev, peer = _peer(k)
        out = pltpu.make_async_remote_copy(src_ref=pick(peer), dst_ref=land_ref.at[me], send_sem=send.at[k - 1],
                                           recv_sem=recv.at[k - 1], device_id=dev, device_id_type=MESH)
        arrive = pltpu.make_async_remote_copy(src_ref=pick(peer), dst_ref=land_ref.at[peer], send_sem=send.at[k - 1],
                                              recv_sem=recv.at[k - 1], device_id=dev, device_id_type=MESH)
        remote.append((out, arrive))
    return local, remote


class _Flight:
    def __init__(self, srcs, lands, sends, recvs, locs, modes, token):
        self.srcs, self.lands, self.sends, self.recvs, self.locs, self.modes, self.token = (
            list(srcs), list(lands), list(sends), list(recvs), list(locs), list(modes), token)


def _xchg_start(arrs, modes, name):
    n = len(arrs)
    lands = [lax.empty((N_DEV,) + a.shape if m == "bcast" else a.shape, a.dtype) for a, m in zip(arrs, modes)]

    def body(*refs):
        srcs, lnds = refs[:n], refs[n:2 * n]
        outs = refs[2 * n:]
        sends, recvs, locs, token = outs[:n], outs[n:2 * n], outs[2 * n:3 * n], outs[5 * n]
        for a in range(n):
            local, remote = _split_copies(srcs[a], lnds[a], sends[a], recvs[a], locs[a], modes[a])
            local.start()
            for out, _ in remote:
                out.start()
        token[...] = jnp.zeros_like(token)

    sem7 = pltpu.SemaphoreType.DMA((N_DEV - 1,))
    res = pl.pallas_call(
        body, name=name,
        out_shape=[sem7] * (2 * n) + [pltpu.SemaphoreType.DMA(())] * n + [pltpu.HBM(a.shape, a.dtype) for a in arrs]
        + [pltpu.HBM(b.shape, b.dtype) for b in lands] + [SDS((8, 128), F32)],
        in_specs=[_HBM] * (2 * n), out_specs=[_SEM] * (3 * n) + [_HBM] * (2 * n) + [pl.BlockSpec(memory_space=pltpu.VMEM)],
        input_output_aliases={i: 3 * n + i for i in range(2 * n)},
        compiler_params=pltpu.CompilerParams(has_side_effects=_EFFECT),
    )(*[pltpu.with_memory_space_constraint(a, pltpu.HBM) for a in arrs],
      *[pltpu.with_memory_space_constraint(b, pltpu.HBM) for b in lands])
    return _Flight(res[3 * n:4 * n], res[4 * n:5 * n], res[:n], res[n:2 * n], res[2 * n:3 * n], modes, res[5 * n])


def _xchg_wait(flight, idx, after, name):
    n = len(idx)
    modes = [flight.modes[i] for i in idx]

    def body(*refs):
        srcs, lnds = refs[:n], refs[n:2 * n]
        sends, recvs, locs = refs[2 * n:3 * n], refs[3 * n:4 * n], refs[4 * n:5 * n]
        for a in range(n):
            local, remote = _split_copies(srcs[a], lnds[a], sends[a], recvs[a], locs[a], modes[a])
            local.wait()
            for _, arrive in remote:
                arrive.wait_send()
                arrive.wait_recv()

    ops = ([flight.srcs[i] for i in idx] + [flight.lands[i] for i in idx] + [flight.sends[i] for i in idx]
           + [flight.recvs[i] for i in idx] + [flight.locs[i] for i in idx])
    res = pl.pallas_call(
        body, name=name, out_shape=[pltpu.HBM(o.shape, o.dtype) for o in ops[:2 * n]],
        in_specs=[_HBM] * (2 * n) + [_SEM] * (3 * n) + [pl.BlockSpec(memory_space=pl.ANY)], out_specs=[_HBM] * (2 * n),
        input_output_aliases={i: i for i in range(2 * n)},
        compiler_params=pltpu.CompilerParams(has_side_effects=_EFFECT),
    )(*ops, after)
    return res[n:]


class _Lazy:
    def __init__(self, fn, pre=None):
        self.fn, self.pre, self.val, self.started = fn, pre, None, False

    def prefetch(self, after):
        token = self.pre(after) if self.pre is not None and not self.started else None
        self.started = True
        return token

    def get(self, after):
        self.prefetch(after)
        if self.val is None:
            self.val = self.fn(after)
        return self.val


_CHIP_PEERS = (2, 4, 6)


def _g2_copies_a(src_ref, land_ref, send, recv, loc):
    me = _my_id()
    local = pltpu.make_async_copy(src_ref, land_ref.at[me], loc)
    remote = []
    for j, k in enumerate(_CHIP_PEERS + (1,)):
        dev, peer = _peer(k)
        out = pltpu.make_async_remote_copy(src_ref=src_ref, dst_ref=land_ref.at[me], send_sem=send.at[j], recv_sem=recv.at[j],
                                           device_id=dev, device_id_type=MESH)
        arrive = pltpu.make_async_remote_copy(src_ref=src_ref, dst_ref=land_ref.at[peer], send_sem=send.at[j],
                                              recv_sem=recv.at[j], device_id=dev, device_id_type=MESH)
        remote.append((out, arrive))
    return local, remote


def _g2_copies_b(land_ref, send, recv):
    sib, _ = _peer(1)
    pairs = []
    for j, k in enumerate(_CHIP_PEERS):
        _, same_core = _peer(k)
        _, other_core = _peer(k | 1)
        out = pltpu.make_async_remote_copy(src_ref=land_ref.at[same_core], dst_ref=land_ref.at[same_core], send_sem=send.at[j],
                                           recv_sem=recv.at[j], device_id=sib, device_id_type=MESH)
        arrive = pltpu.make_async_remote_copy(src_ref=land_ref.at[same_core], dst_ref=land_ref.at[other_core],
                                              send_sem=send.at[j], recv_sem=recv.at[j], device_id=sib, device_id_type=MESH)
        pairs.append((out, arrive))
    return pairs


class _Gather2:
    def __init__(self, srcs, lands, sends, recvs, locs, token):
        self.srcs, self.lands, self.sends, self.recvs, self.locs, self.token = (
            list(srcs), list(lands), list(sends), list(recvs), list(locs), token)
        self.sends_b, self.recvs_b = [None] * len(self.srcs), [None] * len(self.srcs)


def _g2_start(arrs, name):
    n = len(arrs)
    lands = [lax.empty((N_DEV,) + a.shape, a.dtype) for a in arrs]

    def body(*refs):
        srcs, lnds = refs[:n], refs[n:2 * n]
        outs = refs[2 * n:]
        sends, recvs, locs, token = outs[:n], outs[n:2 * n], outs[2 * n:3 * n], outs[5 * n]
        for a in range(n):
            local, remote = _g2_copies_a(srcs[a], lnds[a], sends[a], recvs[a], locs[a])
            local.start()
            for out, _ in remote:
                out.start()
        token[...] = jnp.zeros_like(token)

    sem4 = pltpu.SemaphoreType.DMA((4,))
    res = pl.pallas_call(
        body, name=name,
        out_shape=[sem4] * (2 * n) + [pltpu.SemaphoreType.DMA(())] * n + [pltpu.HBM(a.shape, a.dtype) for a in arrs]
        + [pltpu.HBM(b.shape, b.dtype) for b in lands] + [SDS((8, 128), F32)],
        in_specs=[_HBM] * (2 * n), out_specs=[_SEM] * (3 * n) + [_HBM] * (2 * n) + [pl.BlockSpec(memory_space=pltpu.VMEM)],
        input_output_aliases={i: 3 * n + i for i in range(2 * n)},
        compiler_params=pltpu.CompilerParams(has_side_effects=_EFFECT),
    )(*[pltpu.with_memory_space_constraint(a, pltpu.HBM) for a in arrs],
      *[pltpu.with_memory_space_constraint(b, pltpu.HBM) for b in lands])
    return _Gather2(res[3 * n:4 * n], res[4 * n:5 * n], res[:n], res[n:2 * n], res[2 * n:3 * n], res[5 * n])


def _g2_relay(g, idx, after, name):
    n = len(idx)

    def body(*refs):
        srcs, lnds = refs[:n], refs[n:2 * n]
        sends, recvs, locs = refs[2 * n:3 * n], refs[3 * n:4 * n], refs[4 * n:5 * n]
        outs = refs[5 * n + 1:]
        sends_b, recvs_b = outs[2 * n:3 * n], outs[3 * n:4 * n]
        for a in range(n):
            local, remote = _g2_copies_a(srcs[a], lnds[a], sends[a], recvs[a], locs[a])
            local.wait()
            for _, arrive in remote:
                arrive.wait_send()
                arrive.wait_recv()
        for a in range(n):
            for out, _ in _g2_copies_b(lnds[a], sends_b[a], recvs_b[a]):
                out.start()
        outs[4 * n][...] = jnp.zeros_like(outs[4 * n])

    ops = ([g.srcs[i] for i in idx] + [g.lands[i] for i in idx] + [g.sends[i] for i in idx] + [g.recvs[i] for i in idx]
           + [g.locs[i] for i in idx])
    sem3 = pltpu.SemaphoreType.DMA((3,))
    res = pl.pallas_call(
        body, name=name, out_shape=[pltpu.HBM(o.shape, o.dtype) for o in ops[:2 * n]] + [sem3] * (2 * n) + [SDS((8, 128), F32)],
        in_specs=[_HBM] * (2 * n) + [_SEM] * (3 * n) + [pl.BlockSpec(memory_space=pl.ANY)],
        out_specs=[_HBM] * (2 * n) + [_SEM] * (2 * n) + [pl.BlockSpec(memory_space=pltpu.VMEM)],
        input_output_aliases={i: i for i in range(2 * n)},
        compiler_params=pltpu.CompilerParams(has_side_effects=_EFFECT),
    )(*ops, after)
    for a, i in enumerate(idx):
        g.srcs[i], g.lands[i] = res[a], res[n + a]
        g.sends_b[i], g.recvs_b[i] = res[2 * n + a], res[3 * n + a]
    return res[4 * n]


def _g2_wait(g, idx, after, name):
    n = len(idx)

    def body(*refs):
        lnds, sends_b, recvs_b = refs[:n], refs[n:2 * n], refs[2 * n:3 * n]
        for a in range(n):
            for _, arrive in _g2_copies_b(lnds[a], sends_b[a], recvs_b[a]):
                arrive.wait_send()
                arrive.wait_recv()

    ops = [g.lands[i] for i in idx] + [g.sends_b[i] for i in idx] + [g.recvs_b[i] for i in idx]
    res = pl.pallas_call(
        body, name=name, out_shape=[pltpu.HBM(o.shape, o.dtype) for o in ops[:n]],
        in_specs=[_HBM] * n + [_SEM] * (2 * n) + [pl.BlockSpec(memory_space=pl.ANY)], out_specs=[_HBM] * n,
        input_output_aliases={i: i for i in range(n)},
        compiler_params=pltpu.CompilerParams(has_side_effects=_EFFECT),
    )(*ops, after)
    return list(res)


def _matmul(a, b, kind, out_shape, out_dtype, grid, a_spec, b_spec, o_spec, acc_shape, name):
    nk = grid[2]

    def body(a_ref, b_ref, o_ref, *scratch):
        prod = _dot(a_ref[...], b_ref[...], kind)
        if nk == 1:
            o_ref[...] = prod.astype(out_dtype)
        else:
            acc = scratch[0]
            k = pl.program_id(2)

            @pl.when(k == 0)
            def _():
                acc[...] = prod

            @pl.when(k > 0)
            def _():
                acc[...] += prod

            @pl.when(k == nk - 1)
            def _():
                o_ref[...] = acc[...].astype(out_dtype)

    return pl.pallas_call(
        body, name=name, grid=grid, in_specs=[a_spec, b_spec], out_specs=o_spec, out_shape=SDS(out_shape, out_dtype),
        scratch_shapes=[] if nk == 1 else [pltpu.VMEM(acc_shape, F32)],
        compiler_params=_cp(("parallel", "parallel", "arbitrary")))(a, b)


def _bs(shape, fn):
    return pl.BlockSpec(shape, fn)


def _mm_rows(a, w, kind, n_out, out_dtype, name):
    s, k = a.shape
    tm = min(TM, s)
    return _matmul(a, w, kind, (s, n_out), out_dtype, (s // tm, 1, 1),
                   _bs((tm, k), lambda i, j, kk: (i, 0)), _bs(w.shape, lambda i, j, kk: (0, 0)),
                   _bs((tm, n_out), lambda i, j, kk: (i, 0)), None, name)


def _mm_wgrad(a, dy, out_dtype, name):
    s, k = a.shape
    n = dy.shape[1]
    tko = min(WGRAD_ROWS, k)
    return _matmul(a, dy, "tn", (k, n), out_dtype, (k // tko, 1, 1),
                   _bs((s, tko), lambda i, j, kk: (0, i)), _bs((s, n), lambda i, j, kk: (0, 0)),
                   _bs((tko, n), lambda i, j, kk: (i, 0)), None, name)


def _ada_fwd(c_all, w_ada):
    ncol = w_ada.shape[2]

    def body(c_ref, w_ref, o_ref):
        ca = _silu(c_ref[...])
        ca = jnp.concatenate([ca, jnp.zeros_like(ca)], axis=0)
        o_ref[...] = _dot(ca, w_ref[...], "nn")[:N_DEV, :]

    return pl.pallas_call(
        body, name="ada_fwd", grid=(N_LAYER,),
        in_specs=[pl.BlockSpec((N_DEV, D), lambda l: (0, 0)), pl.BlockSpec((None, D, ncol), lambda l: (l, 0, 0))],
        out_specs=pl.BlockSpec((None, N_DEV, ncol), lambda l: (l, 0, 0)),
        out_shape=SDS((N_LAYER, N_DEV, ncol), F32), compiler_params=_cp(("parallel",)))(c_all, w_ada)


def _ada_finish(m_mine, b_ada):
    def body(m_ref, b_ref, o_ref, t_ref):
        o_ref[...] = m_ref[...] + b_ref[...]
        t_ref[...] = jnp.zeros_like(t_ref)

    return pl.pallas_call(body, name="ada_finish", out_shape=[SDS(b_ada.shape, F32), SDS((8, 128), F32)])(m_mine, b_ada)


def _ada_bwd(c_all, dmod_cols):
    ncol = dmod_cols.shape[2]

    def body(c_ref, d_ref, o_ref):
        ca = _silu(c_ref[...])
        ca = jnp.concatenate([ca, jnp.zeros_like(ca)], axis=0)
        dm = d_ref[...]
        dm = jnp.concatenate([dm, jnp.zeros_like(dm)], axis=0)
        o_ref[...] = _dot(ca, dm, "tn")

    return pl.pallas_call(
        body, name="ada_bwd", grid=(N_LAYER,),
        in_specs=[pl.BlockSpec((N_DEV, D), lambda l: (0, 0)), pl.BlockSpec((None, N_DEV, ncol), lambda l: (l, 0, 0))],
        out_specs=pl.BlockSpec((None, D, ncol), lambda l: (l, 0, 0)),
        out_shape=SDS((N_LAYER, D, ncol), F32), compiler_params=_cp(("parallel",)))(c_all, dmod_cols)


def _rows(s):
    tm = min(TM, s)
    return tm, pl.BlockSpec((tm, D), lambda i: (i, 0))


def _pcall(body, operands, dep, **kw):
    if dep is None:
        return pl.pallas_call(body, **kw)(*operands)
    n = len(operands)

    def body_dep(*refs):
        body(*refs[:n], *refs[n + 1:])

    kw["in_specs"] = list(kw["in_specs"]) + [pl.BlockSpec(memory_space=pl.ANY)]
    return pl.pallas_call(body_dep, **kw)(*operands, dep)


def _rms_mod(x, g, sc, sh, name):
    s = x.shape[0]
    tm, row = _rows(s)

    def body(x_ref, g_ref, sc_ref, sh_ref, h_ref):
        xf = x_ref[...]
        h_ref[...] = (xf * _rstd(xf) * (g_ref[...] * (1.0 + sc_ref[...])) + sh_ref[...]).astype(BF16)

    return pl.pallas_call(
        body, name=name, grid=(s // tm,), in_specs=[row, _vec(*g, 1), _vec(*sc, 1), _vec(*sh, 1)], out_specs=row,
        out_shape=SDS((s, D), BF16), compiler_params=_cp(("parallel",)))(x, g[0], sc[0], sh[0])


def _post(xres, o, ga, gpost, gn, scn, shn, name, dep=None):
    s = xres.shape[0]
    tm, row = _rows(s)

    def body(x_ref, o_ref, ga_ref, gp_ref, gn_ref, sc_ref, sh_ref, xn_ref, h_ref):
        of = o_ref[...].astype(F32)
        xn = x_ref[...] + ga_ref[...] * (of * _rstd(of) * gp_ref[...])
        xn_ref[...] = xn
        h_ref[...] = (xn * _rstd(xn) * (gn_ref[...] * (1.0 + sc_ref[...])) + sh_ref[...]).astype(BF16)

    return _pcall(
        body, (xres, o, ga[0], gpost[0], gn[0], scn[0], shn[0]), dep, name=name, grid=(s // tm,),
        in_specs=[row, row, _vec(*ga, 1), _vec(*gpost, 1), _vec(*gn, 1), _vec(*scn, 1), _vec(*shn, 1)],
        out_specs=[row, row], out_shape=[SDS((s, D), F32), SDS((s, D), BF16)], compiler_params=_cp(("parallel",)))


def _post_loss(xres, o, ga, gpost, target, name, dep=None):
    s = xres.shape[0]
    tm, row = _rows(s)

    def body(x_ref, o_ref, ga_ref, gp_ref, t_ref, dy_ref, loss_ref):
        of = o_ref[...].astype(F32)
        err = x_ref[...] + ga_ref[...] * (of * _rstd(of) * gp_ref[...]) - t_ref[...]
        dy_ref[...] = err * (1.0 / D)

        @pl.when(pl.program_id(0) == 0)
        def _():
            loss_ref[...] = jnp.zeros_like(loss_ref)

        loss_ref[...] += jnp.sum(jnp.mean(err * err, axis=-1, keepdims=True), axis=0, keepdims=True) * 0.5

    return _pcall(
        body, (xres, o, ga[0], gpost[0], target), dep, name=name, grid=(s // tm,),
        in_specs=[row, row, _vec(*ga, 1), _vec(*gpost, 1), row],
        out_specs=[row, pl.BlockSpec((8, 128), lambda i: (0, 0))], out_shape=[SDS((s, D), F32), SDS((8, 128), F32)],
        compiler_params=_cp(("arbitrary",)))


def _acc(ref, val, first):
    @pl.when(first)
    def _():
        ref[...] = val

    @pl.when(jnp.logical_not(first))
    def _():
        ref[...] += val


def _post_bwd(dxn, o, ga, gpost, name, dep=None):
    s = dxn.shape[0]
    tm, row = _rows(s)
    vec = pl.BlockSpec((1, D), lambda i: (0, 0))

    def body(d_ref, o_ref, ga_ref, gp_ref, do_ref, dga_ref, dgp_ref):
        of, dx = o_ref[...].astype(F32), d_ref[...]
        r = _rstd(of)
        oh = of * r
        do_ref[...] = _rms_bwd(oh, r, dx * (ga_ref[...] * gp_ref[...])).astype(BF16)
        cs = _colsum(dx * oh)
        first = pl.program_id(0) == 0
        _acc(dga_ref, cs * gp_ref[...], first)
        _acc(dgp_ref, cs * ga_ref[...], first)

    return _pcall(
        body, (dxn, o, ga[0], gpost[0]), dep, name=name, grid=(s // tm,),
        in_specs=[row, row, _vec(*ga, 1), _vec(*gpost, 1)], out_specs=[row, vec, vec],
        out_shape=[SDS((s, D), BF16), SDS((1, D), F32), SDS((1, D), F32)], compiler_params=_cp(("arbitrary",)))


def _pre_bwd(dh, x, dres, g, sc, name, dep=None):
    s = x.shape[0]
    tm, row = _rows(s)
    vec = pl.BlockSpec((1, D), lambda i: (0, 0))

    def body(dh_ref, x_ref, dr_ref, g_ref, sc_ref, dx_ref, dsh_ref, dsc_ref, dg_ref):
        xf, d = x_ref[...], dh_ref[...].astype(F32)
        r = _rstd(xf)
        xh = xf * r
        dx_ref[...] = dr_ref[...] + _rms_bwd(xh, r, d * (g_ref[...] * (1.0 + sc_ref[...])))
        cs = _colsum(d * xh)
        first = pl.program_id(0) == 0
        _acc(dsh_ref, _colsum(d), first)
        _acc(dsc_ref, cs * g_ref[...], first)
        _acc(dg_ref, cs * (1.0 + sc_ref[...]), first)

    return _pcall(
        body, (dh, x, dres, g[0], sc[0]), dep, name=name, grid=(s // tm,),
        in_specs=[row, row, row, _vec(*g, 1), _vec(*sc, 1)], out_specs=[row, vec, vec, vec],
        out_shape=[SDS((s, D), F32), SDS((1, D), F32), SDS((1, D), F32), SDS((1, D), F32)],
        compiler_params=_cp(("arbitrary",)))


def _gnorm(ys, gg, name):
    s = ys[0].shape[0]
    tm = min(TM, s)
    yb = pl.BlockSpec((tm, GW), lambda i: (i, 0))

    def body(y0, y1, y2, y3, g_ref, o_ref):
        for i, yr in enumerate((y0, y1, y2, y3)):
            y = yr[...]
            o_ref[:, GW * i:GW * (i + 1)] = (y * _rstd(y) * g_ref[:, GW * i:GW * (i + 1)]).astype(BF16)

    return pl.pallas_call(
        body, name=name, grid=(s // tm,), in_specs=[yb] * 4 + [_vec(*gg, 1)], out_specs=pl.BlockSpec((tm, D), lambda i: (i, 0)),
        out_shape=SDS((s, D), BF16), compiler_params=_cp(("parallel",)))(*ys, gg[0])


def _gnorm_bwd(dyn, ys, gg, name, dep=None):
    s = ys[0].shape[0]
    tm = min(TM, s)
    yb = pl.BlockSpec((tm, GW), lambda i: (i, 0))

    def body(d_ref, y0, y1, y2, y3, g_ref, o0, o1, o2, o3, dg_ref):
        first = pl.program_id(0) == 0
        for i, (yr, orf) in enumerate(zip((y0, y1, y2, y3), (o0, o1, o2, o3))):
            y = yr[...]
            d = d_ref[:, GW * i:GW * (i + 1)].astype(F32)
            r = _rstd(y)
            yh = y * r
            orf[...] = _rms_bwd(yh, r, d * g_ref[:, GW * i:GW * (i + 1)]).astype(BF16)
            cs = _colsum(d * yh)

            @pl.when(first)
            def _():
                dg_ref[:, GW * i:GW * (i + 1)] = cs

            @pl.when(jnp.logical_not(first))
            def _():
                dg_ref[:, GW * i:GW * (i + 1)] += cs

    return _pcall(
        body, (dyn, *ys, gg[0]), dep, name=name, grid=(s // tm,),
        in_specs=[pl.BlockSpec((tm, D), lambda i: (i, 0))] + [yb] * 4 + [_vec(*gg, 1)],
        out_specs=[yb] * 4 + [pl.BlockSpec((1, D), lambda i: (0, 0))],
        out_shape=[SDS((s, GW), BF16)] * 4 + [SDS((1, D), F32)], compiler_params=_cp(("arbitrary",)))


def _lane_put(acc, col, h):
    lane = lax.broadcasted_iota(jnp.int32, acc.shape, 1)
    return jnp.where(lane == h, col, acc)


def _fgate(z, bf, name, dep=None):
    s = z.shape[0]

    def body(z_ref, b_ref, fc_ref, fr_ref):
        xg = z_ref[...].astype(F32) + b_ref[...]
        lf = jnp.minimum(xg, 0.0) - jnp.log(1.0 + jnp.exp(-jnp.abs(xg)))
        lane = lax.broadcasted_iota(jnp.int32, lf.shape, 1)
        row = lax.broadcasted_iota(jnp.int32, lf.shape, 0)
        f = jnp.where(lane < 4, lf, 0.0)
        sh = 1
        while sh < s:
            f = f + jnp.where(row >= sh, pltpu.roll(f, sh, 0), 0.0)
            sh *= 2
        fc_ref[...] = f
        fr_ref[...] = f.T[:8, :]

    return _pcall(
        body, (z, bf), dep, name=name, grid=(1,),
        in_specs=[pl.BlockSpec((s, 128), lambda i: (0, Z_FG)), pl.BlockSpec((1, 128), lambda i: (0, 0))],
        out_specs=[pl.BlockSpec((s, 128), lambda i: (0, 0)), pl.BlockSpec((8, s), lambda i: (0, 0))],
        out_shape=[SDS((s, 128), F32), SDS((8, s), F32)], compiler_params=_cp(("arbitrary",)))


def _fgate_bwd(z, bf, dfrow, dfcol, name):
    s = z.shape[0]

    def body(z_ref, b_ref, d_ref, dc_ref, dz_ref, db_ref):
        d = jnp.concatenate([d_ref[...], jnp.zeros((120, s), F32)], axis=0).T + dc_ref[...]
        row = lax.broadcasted_iota(jnp.int32, d.shape, 0)
        lane = lax.broadcasted_iota(jnp.int32, d.shape, 1)
        sh = 1
        while sh < s:
            d = d + jnp.where(row < s - sh, pltpu.roll(d, s - sh, 0), 0.0)
            sh *= 2
        xg = z_ref[...].astype(F32) + b_ref[...]
        dz = jnp.where(lane < 4, d * _sigmoid(-xg), 0.0)
        dz_ref[...] = dz.astype(BF16)
        db_ref[...] = _colsum(dz)

    return pl.pallas_call(
        body, name=name, grid=(1,),
        in_specs=[pl.BlockSpec((s, 128), lambda i: (0, Z_FG)), pl.BlockSpec((1, 128), lambda i: (0, 0)),
                  pl.BlockSpec((8, s), lambda i: (0, 0)), pl.BlockSpec((s, 128), lambda i: (0, 0))],
        out_specs=[pl.BlockSpec((s, 128), lambda i: (0, 0)), pl.BlockSpec((1, 128), lambda i: (0, 0))],
        out_shape=[SDS((s, 128), BF16), SDS((1, 128), F32)], compiler_params=_cp(("arbitrary",)))(z, bf, dfrow, dfcol)


def _fox_scores(q_ref, k_ref, fc_ref, fr_ref, h, i, nk, tq):
    kw = nk * tq
    q = q_ref[:, HD * h:HD * (h + 1)] * SCALE
    sc = _dot(q, k_ref[0:kw, HD * h:HD * (h + 1)], "nt") + fc_ref[:, h:h + 1] - fr_ref[h:h + 1, 0:kw]
    qpos = i * tq + lax.broadcasted_iota(jnp.int32, (tq, kw), 0)
    kpos = lax.broadcasted_iota(jnp.int32, (tq, kw), 1)
    return q, jnp.where(kpos <= qpos, sc, NEG)


def _fox_fwd(z, fcol, frow, name):
    s = z.shape[0]
    tq = min(TQ, s)
    nc = s // tq

    def body(q_ref, k_ref, v_ref, fc_ref, fr_ref, y_ref, l_ref):
        for n in range(nc):
            @pl.when(pl.program_id(0) == n)
            def _():
                kw = (n + 1) * tq
                lse = jnp.zeros((tq, 128), F32)
                for h in range(4):
                    _, sc = _fox_scores(q_ref, k_ref, fc_ref, fr_ref, h, n, n + 1, tq)
                    m = jnp.max(sc, axis=-1, keepdims=True)
                    p = jnp.exp(sc - m)
                    l = jnp.sum(p, axis=-1, keepdims=True)
                    y_ref[:, HD * h:HD * (h + 1)] = _dot(p, v_ref[0:kw, HD * h:HD * (h + 1)], "nn") / l
                    lse = _lane_put(lse, m + jnp.log(l), h)
                l_ref[...] = lse

    return pl.pallas_call(
        body, name=name, grid=(nc,),
        in_specs=[pl.BlockSpec((tq, GW), lambda i: (i, Z_FQ)), pl.BlockSpec((s, GW), lambda i: (0, Z_FK)),
                  pl.BlockSpec((s, GW), lambda i: (0, Z_FV)), pl.BlockSpec((tq, 128), lambda i: (i, 0)),
                  pl.BlockSpec((8, s), lambda i: (0, 0))],
        out_specs=[pl.BlockSpec((tq, GW), lambda i: (i, 0)), pl.BlockSpec((tq, 128), lambda i: (i, 0))],
        out_shape=[SDS((s, GW), F32), SDS((s, 128), F32)], compiler_params=_cp(("parallel",)))(z, z, z, fcol, frow)


def _fox_bwd(z, fcol, frow, lse, y, dy, name):
    s = z.shape[0]
    tq = min(TQ, s)
    nc = s // tq
    half = max(nc // 2, 1)

    def body(q_ref, k_ref, v_ref, fc_ref, fr_ref, l_ref, y_ref, dy_ref, dq_ref, dk_ref, dv_ref, df_ref, dfq_ref):
        @pl.when(pl.program_id(0) == 0)
        def _():
            dk_ref[...] = jnp.zeros_like(dk_ref)
            dv_ref[...] = jnp.zeros_like(dv_ref)
            df_ref[...] = jnp.zeros_like(df_ref)

        i = pl.program_id(0)
        for cond, nk in ((i < half, half), (i >= half, nc)):
            @pl.when(cond)
            def _():
                kw = nk * tq
                dfq = jnp.zeros((tq, 128), F32)
                for h in range(4):
                    hs = slice(HD * h, HD * (h + 1))
                    q, sc = _fox_scores(q_ref, k_ref, fc_ref, fr_ref, h, i, nk, tq)
                    p = jnp.exp(sc - l_ref[:, h:h + 1])
                    dyh = dy_ref[...].astype(F32)[:, hs]
                    dd = jnp.sum(dyh * y_ref[:, hs], axis=-1, keepdims=True)
                    ds = p * (_dot(dyh, v_ref[0:kw, hs], "nt") - dd)
                    dq_ref[:, hs] = _dot(ds, k_ref[0:kw, hs], "nn") * SCALE
                    dk_ref[0:kw, hs] += _dot(ds, q, "tn")
                    dv_ref[0:kw, hs] += _dot(p, dyh, "tn")
                    df_ref[h:h + 1, 0:kw] -= _colsum(ds)
                    dfq = _lane_put(dfq, jnp.sum(ds, axis=-1, keepdims=True), h)
                dfq_ref[...] = dfq

    tile = lambda w: pl.BlockSpec((tq, w), lambda i: (i, 0))
    full = pl.BlockSpec((s, GW), lambda i: (0, 0))
    rows8 = pl.BlockSpec((8, s), lambda i: (0, 0))
    return pl.pallas_call(
        body, name=name, grid=(nc,),
        in_specs=[pl.BlockSpec((tq, GW), lambda i: (i, Z_FQ)), pl.BlockSpec((s, GW), lambda i: (0, Z_FK)),
                  pl.BlockSpec((s, GW), lambda i: (0, Z_FV)), tile(128), rows8, tile(128), tile(GW), tile(GW)],
        out_specs=[tile(GW), full, full, rows8, tile(128)],
        out_shape=[SDS((s, GW), F32), SDS((s, GW), F32), SDS((s, GW), F32), SDS((8, s), F32), SDS((s, 128), F32)],
        compiler_params=_cp(("arbitrary",)))(z, z, z, fcol, frow, lse, y, dy)


def _swa_block(q_ref, k_ref, v_ref, n):
    qs = pl.multiple_of(n * WIN, WIN)
    ks = pl.multiple_of(jnp.maximum(n - 1, 0) * WIN, WIN)
    qb = q_ref[pl.ds(qs, WIN), :]
    kb = k_ref[pl.ds(ks, 2 * WIN), :]
    vb = v_ref[pl.ds(ks, 2 * WIN), :]
    rows = lax.broadcasted_iota(jnp.int32, (2 * WIN, 2 * WIN), 0) & (WIN - 1)
    dist = (qs + rows) - (ks + lax.broadcasted_iota(jnp.int32, (2 * WIN, 2 * WIN), 1))
    return qs, ks, qb, kb, vb, (dist >= 0) & (dist < WIN)


def _stack2(x, kvh):
    return jnp.concatenate([x[:, HD * (2 * kvh):HD * (2 * kvh + 1)], x[:, HD * (2 * kvh + 1):HD * (2 * kvh + 2)]], axis=0)


def _sink2(sink_ref, kvh):
    top = lax.broadcasted_iota(jnp.int32, (2 * WIN, 1), 0) < WIN
    return jnp.where(top, sink_ref[2 * kvh], sink_ref[2 * kvh + 1])


def _swa_fwd(z, sinks, name):
    s = z.shape[0]

    def body(sink_ref, q_ref, k_ref, v_ref, y_ref, l_ref):
        def step(n, carry):
            qs, ks, qb, kb, vb, valid = _swa_block(q_ref, k_ref, v_ref, n)
            lse = jnp.zeros((WIN, 128), F32)
            for kvh in range(2):
                kv = slice(HD * kvh, HD * (kvh + 1))
                sc = jnp.where(valid, _dot(_stack2(qb, kvh) * SCALE, kb[:, kv], "nt"), NEG)
                sink = _sink2(sink_ref, kvh)
                m = jnp.maximum(jnp.max(sc, axis=-1, keepdims=True), sink)
                p = jnp.exp(sc - m)
                den = jnp.sum(p, axis=-1, keepdims=True) + jnp.exp(sink - m)
                o = _dot(p, vb[:, kv], "nn") / den
                lrow = m + jnp.log(den)
                for j in range(2):
                    h = 2 * kvh + j
                    y_ref[pl.ds(qs, WIN), HD * h:HD * (h + 1)] = o[WIN * j:WIN * (j + 1), :]
                    lse = _lane_put(lse, lrow[WIN * j:WIN * (j + 1), :], h)
            l_ref[pl.ds(qs, WIN), :] = lse
            return carry

        lax.fori_loop(0, s // WIN, step, 0)

    return pl.pallas_call(
        body, name=name, grid=(1,),
        in_specs=[pl.BlockSpec(memory_space=pltpu.SMEM), pl.BlockSpec((s, GW), lambda i: (0, Z_SQ)),
                  pl.BlockSpec((s, 128), lambda i: (0, Z_SK)), pl.BlockSpec((s, 128), lambda i: (0, Z_SV))],
        out_specs=[pl.BlockSpec((s, GW), lambda i: (0, 0)), pl.BlockSpec((s, 128), lambda i: (0, 0))],
        out_shape=[SDS((s, GW), F32), SDS((s, 128), F32)], compiler_params=_cp(("arbitrary",)))(sinks, z, z, z)


def _swa_bwd(z, sinks, lse, y, dy, name):
    s = z.shape[0]

    def body(sink_ref, q_ref, k_ref, v_ref, l_ref, y_ref, dy_ref, dq_ref, dk_ref, dv_ref, dsink_ref):
        dk_ref[...] = jnp.zeros_like(dk_ref)
        dv_ref[...] = jnp.zeros_like(dv_ref)
        dsink_ref[...] = jnp.zeros_like(dsink_ref)

        def step(n, carry):
            qs, ks, qb, kb, vb, valid = _swa_block(q_ref, k_ref, v_ref, n)
            lse_b = l_ref[pl.ds(qs, WIN), :]
            yb = y_ref[pl.ds(qs, WIN), :]
            dyb = dy_ref[pl.ds(qs, WIN), :].astype(F32)
            dsink = jnp.zeros((1, 128), F32)
            for kvh in range(2):
                kv = slice(HD * kvh, HD * (kvh + 1))
                q = _stack2(qb, kvh) * SCALE
                dy2 = _stack2(dyb, kvh)
                sc = jnp.where(valid, _dot(q, kb[:, kv], "nt"), NEG)
                lh = jnp.concatenate([lse_b[:, 2 * kvh:2 * kvh + 1], lse_b[:, 2 * kvh + 1:2 * kvh + 2]], axis=0)
                p = jnp.exp(sc - lh)
                dd = jnp.sum(dy2 * _stack2(yb, kvh), axis=-1, keepdims=True)
                ds = p * (_dot(dy2, vb[:, kv], "nt") - dd)
                dq = _dot(ds, kb[:, kv], "nn") * SCALE
                dk_ref[pl.ds(ks, 2 * WIN), kv] += _dot(ds, q, "tn")
                dv_ref[pl.ds(ks, 2 * WIN), kv] += _dot(p, dy2, "tn")
                dsk = jnp.exp(_sink2(sink_ref, kvh) - lh) * dd
                for j in range(2):
                    h = 2 * kvh + j
                    dq_ref[pl.ds(qs, WIN), HD * h:HD * (h + 1)] = dq[WIN * j:WIN * (j + 1), :]
                    dsink = _lane_put(dsink, dsink[:, h:h + 1] - jnp.sum(dsk[WIN * j:WIN * (j + 1), :], axis=0, keepdims=True), h)
            dsink_ref[...] += dsink
            return carry

        lax.fori_loop(0, s // WIN, step, 0)

    full = lambda w: pl.BlockSpec((s, w), lambda i: (0, 0))
    return pl.pallas_call(
        body, name=name, grid=(1,),
        in_specs=[pl.BlockSpec(memory_space=pltpu.SMEM), pl.BlockSpec((s, GW), lambda i: (0, Z_SQ)),
                  pl.BlockSpec((s, 128), lambda i: (0, Z_SK)), pl.BlockSpec((s, 128), lambda i: (0, Z_SV)),
                  full(128), full(GW), full(GW)],
        out_specs=[full(GW), full(128), full(128), pl.BlockSpec((1, 128), lambda i: (0, 0))],
        out_shape=[SDS((s, GW), F32), SDS((s, 128), F32), SDS((s, 128), F32), SDS((1, 128), F32)],
        compiler_params=_cp(("arbitrary",)))(sinks, z, z, z, lse, y, dy)


def _delayed(win, shift, halo):
    return win[halo:, :] if shift == 0 else pltpu.roll(win, shift, 0)[halo:, :]


def _prev_halo(width, halo, tm, col):
    return pl.BlockSpec((halo, width), lambda i: (jnp.maximum(i * (tm // halo) - 1, 0), col))


def _glu_window(a_ref, g_ref, ah_ref, gh_ref):
    keep = (pl.program_id(0) > 0).astype(F32)
    a = jnp.concatenate([ah_ref[...].astype(F32) * keep, a_ref[...].astype(F32)], axis=0)
    g = jnp.concatenate([gh_ref[...].astype(F32), g_ref[...].astype(F32)], axis=0)
    return a * _sigmoid(g)


def _conv_fwd(z, cw, cb, lg, lb, pw, pb, name):
    s = z.shape[0]
    tm = min(TM, s)

    def body(a_ref, g_ref, ah_ref, gh_ref, w_ref, b_ref, lg_ref, lb_ref, pw_ref, pb_ref, y_ref, hc_ref):
        hg = _glu_window(a_ref, g_ref, ah_ref, gh_ref)
        hc = jnp.zeros((tm, GW), F32) + b_ref[...]
        for k in range(CONV_K):
            hc = hc + w_ref[k:k + 1, :] * _delayed(hg, CONV_K - 1 - k, CONV_HALO)
        hc_ref[...] = hc
        xh, _ = _ln_stats(hc)
        y_ref[...] = _dot(_silu(xh * lg_ref[...] + lb_ref[...]), pw_ref[...], "nn") + pb_ref[...]

    tile = lambda col: pl.BlockSpec((tm, GW), lambda i: (i, col))
    whole = lambda a: pl.BlockSpec(a.shape, lambda i: (0, 0))
    return pl.pallas_call(
        body, name=name, grid=(s // tm,),
        in_specs=[tile(Z_CA), tile(Z_CG), _prev_halo(GW, CONV_HALO, tm, Z_CA), _prev_halo(GW, CONV_HALO, tm, Z_CG),
                  whole(cw), whole(cb), whole(lg), whole(lb), whole(pw), whole(pb)],
        out_specs=[tile(0), tile(0)], out_shape=[SDS((s, GW), F32), SDS((s, GW), F32)],
        compiler_params=_cp(("parallel",)))(z, z, z, z, cw, cb, lg, lb, pw, pb)


def _conv_bwd_a(z, hc, dy, cw, lg, lb, pw, name):
    s = z.shape[0]
    tm = min(TM, s)

    def body(a_ref, g_ref, ah_ref, gh_ref, hc_ref, dy_ref, lg_ref, lb_ref, pw_ref,
             dhc_ref, dpw_ref, dpb_ref, dlg_ref, dlb_ref, dcw_ref, dcb_ref):
        first = pl.program_id(0) == 0
        dy = dy_ref[...].astype(F32)
        xh, rstd = _ln_stats(hc_ref[...])
        hn = xh * lg_ref[...] + lb_ref[...]
        dhn = _dot(dy, pw_ref[...], "nt") * _dsilu(hn)
        dhc = _ln_bwd(xh, rstd, dhn * lg_ref[...])
        dhc_ref[...] = dhc
        _acc(dpw_ref, _dot(_silu(hn), dy, "tn"), first)
        _acc(dpb_ref, _colsum(dy), first)
        _acc(dlg_ref, _colsum(dhn * xh), first)
        _acc(dlb_ref, _colsum(dhn), first)
        _acc(dcb_ref, _colsum(dhc), first)
        hg = _glu_window(a_ref, g_ref, ah_ref, gh_ref)

        @pl.when(first)
        def _():
            dcw_ref[...] = jnp.zeros_like(dcw_ref)

        for k in range(CONV_K):
            dcw_ref[k:k + 1, :] += _colsum(dhc * _delayed(hg, CONV_K - 1 - k, CONV_HALO))

    tile = lambda col: pl.BlockSpec((tm, GW), lambda i: (i, col))
    whole = lambda shape: pl.BlockSpec(shape, lambda i: (0, 0))
    return pl.pallas_call(
        body, name=name, grid=(s // tm,),
        in_specs=[tile(Z_CA), tile(Z_CG), _prev_halo(GW, CONV_HALO, tm, Z_CA), _prev_halo(GW, CONV_HALO, tm, Z_CG),
                  tile(0), tile(0), whole(lg.shape), whole(lb.shape), whole(pw.shape)],
        out_specs=[tile(0), whole((GW, GW)), whole((1, GW)), whole((1, GW)), whole((1, GW)), whole((32, GW)), whole((1, GW))],
        out_shape=[SDS((s, GW), F32), SDS((GW, GW), F32), SDS((1, GW), F32), SDS((1, GW), F32), SDS((1, GW), F32),
                   SDS((32, GW), F32), SDS((1, GW), F32)],
        compiler_params=_cp(("arbitrary",)))(z, z, z, z, hc, dy, lg, lb, pw)


def _conv_bwd_b(z, dhc, cw, name):
    s = z.shape[0]
    tm = min(TM, s)
    nt = s // tm

    def body(a_ref, g_ref, d_ref, dn_ref, w_ref, da_ref, dg_ref):
        keep = (pl.program_id(0) < nt - 1).astype(F32)
        win = jnp.concatenate([d_ref[...], dn_ref[...] * keep], axis=0)
        dhg = jnp.zeros((tm, GW), F32)
        for k in range(CONV_K):
            sh = CONV_K - 1 - k
            dhg = dhg + w_ref[k:k + 1, :] * (win[:tm, :] if sh == 0 else pltpu.roll(win, tm + CONV_HALO - sh, 0)[:tm, :])
        sg = _sigmoid(g_ref[...].astype(F32))
        da_ref[...] = (dhg * sg).astype(BF16)
        dg_ref[...] = (dhg * a_ref[...].astype(F32) * sg * (1.0 - sg)).astype(BF16)

    tile = lambda col: pl.BlockSpec((tm, GW), lambda i: (i, col))
    nxt = pl.BlockSpec((CONV_HALO, GW), lambda i: (jnp.minimum((i + 1) * (tm // CONV_HALO), s // CONV_HALO - 1), 0))
    return pl.pallas_call(
        body, name=name, grid=(nt,),
        in_specs=[tile(Z_CA), tile(Z_CG), tile(0), nxt, pl.BlockSpec(cw.shape, lambda i: (0, 0))],
        out_specs=[tile(0), tile(0)], out_shape=[SDS((s, GW), BF16), SDS((s, GW), BF16)],
        compiler_params=_cp(("parallel",)))(z, z, dhc, dhc, cw)


def _sgu_chunk(zu, zv, lg, lb, wcat, bfull):
    u, v = _gelu(zu), _gelu(zv)
    xh, rstd = _ln_stats(v)
    vn = xh * lg + lb
    lane = lax.shift_right_logical(lax.broadcasted_iota(jnp.int32, (WIN, GW), 1), 6)
    r = jnp.concatenate([jnp.where(lane == g, vn, 0.0) for g in range(4)], axis=0)
    mix = _dot(wcat, r, "nn") + bfull
    return u, xh, rstd, r, mix, lane


def _tril4(w):
    t = lax.broadcasted_iota(jnp.int32, w.shape, 0)
    sidx = lax.broadcasted_iota(jnp.int32, w.shape, 1) & (WIN - 1)
    return jnp.where(sidx <= t, w, 0.0)


def _sgu_fwd(z, lg, lb, wcat, bfull, name):
    s = z.shape[0]
    tm = min(TM, s)

    def body(u_ref, v_ref, lg_ref, lb_ref, w_ref, b_ref, y_ref):
        w = _tril4(w_ref[...])
        for n in range(tm // WIN):
            rows = slice(WIN * n, WIN * (n + 1))
            u, _, _, _, mix, _ = _sgu_chunk(u_ref[rows, :].astype(F32), v_ref[rows, :].astype(F32), lg_ref[...], lb_ref[...], w, b_ref[...])
            y_ref[rows, :] = u * mix

    tile = lambda col: pl.BlockSpec((tm, GW), lambda i: (i, col))
    whole = lambda a: pl.BlockSpec(a.shape, lambda i: (0, 0))
    return pl.pallas_call(
        body, name=name, grid=(s // tm,), in_specs=[tile(Z_GU), tile(Z_GV), whole(lg), whole(lb), whole(wcat), whole(bfull)],
        out_specs=tile(0), out_shape=SDS((s, GW), F32), compiler_params=_cp(("parallel",)))(z, z, lg, lb, wcat, bfull)


def _sgu_bwd(z, dy, lg, lb, wcat, bfull, name):
    s = z.shape[0]
    tm = min(TM, s)

    def body(u_ref, v_ref, dy_ref, lg_ref, lb_ref, w_ref, b_ref, du_ref, dv_ref, dw_ref, db_ref, dlg_ref, dlb_ref):
        first = pl.program_id(0) == 0
        w = _tril4(w_ref[...])
        wt = w.T
        dw = jnp.zeros((WIN, 4 * WIN), F32)
        db = jnp.zeros((WIN, 128), F32)
        dlg = jnp.zeros((1, GW), F32)
        dlb = jnp.zeros((1, GW), F32)
        for n in range(tm // WIN):
            rows = slice(WIN * n, WIN * (n + 1))
            zu, zv, dout = u_ref[rows, :].astype(F32), v_ref[rows, :].astype(F32), dy_ref[rows, :].astype(F32)
            u, xh, rstd, r, mix, lane = _sgu_chunk(zu, zv, lg_ref[...], lb_ref[...], w, b_ref[...])
            dmix = dout * u
            du_ref[rows, :] = (dout * mix * _dgelu(zu)).astype(BF16)
            dw = dw + _dot(dmix, r, "nt")
            for g in range(4):
                db = _lane_put(db, db[:, g:g + 1] + jnp.sum(dmix[:, HD * g:HD * (g + 1)], axis=1, keepdims=True), g)
            dr = _dot(wt, dmix, "nn")
            dvn = jnp.zeros((WIN, GW), F32)
            for g in range(4):
                dvn = dvn + jnp.where(lane == g, dr[WIN * g:WIN * (g + 1), :], 0.0)
            dlg = dlg + _colsum(dvn * xh)
            dlb = dlb + _colsum(dvn)
            dv_ref[rows, :] = (_ln_bwd(xh, rstd, dvn * lg_ref[...]) * _dgelu(zv)).astype(BF16)
        _acc(dw_ref, _tril4(dw), first)
        _acc(db_ref, db, first)
        _acc(dlg_ref, dlg, first)
        _acc(dlb_ref, dlb, first)

    tile = lambda col: pl.BlockSpec((tm, GW), lambda i: (i, col))
    whole = lambda shape: pl.BlockSpec(shape, lambda i: (0, 0))
    return pl.pallas_call(
        body, name=name, grid=(s // tm,),
        in_specs=[tile(Z_GU), tile(Z_GV), tile(0), whole(lg.shape), whole(lb.shape), whole(wcat.shape), whole(bfull.shape)],
        out_specs=[tile(0), tile(0), whole((WIN, 4 * WIN)), whole((WIN, 128)), whole((1, GW)), whole((1, GW))],
        out_shape=[SDS((s, GW), BF16), SDS((s, GW), BF16), SDS((WIN, 4 * WIN), F32), SDS((WIN, 128), F32),
                   SDS((1, GW), F32), SDS((1, GW), F32)],
        compiler_params=_cp(("arbitrary",)))(z, z, dy, lg, lb, wcat, bfull)


def _conv3(win, w, b):
    return (w[2:3, :] * win[FFN_HALO:, :] + w[1:2, :] * pltpu.roll(win, 1, 0)[FFN_HALO:, :]
            + w[0:1, :] * pltpu.roll(win, 2, 0)[FFN_HALO:, :] + b)


def _ffn_specs(s, tm):
    main = pl.BlockSpec((2, None, tm, FF_BLK), lambda j, i: (0, j, i, 0))
    prev = pl.BlockSpec((2, None, FFN_HALO, FF_BLK), lambda j, i: (0, j, jnp.maximum(i * (tm // FFN_HALO) - 1, 0), 0))
    nxt = pl.BlockSpec((2, None, FFN_HALO, FF_BLK),
                       lambda j, i: (0, j, jnp.minimum((i + 1) * (tm // FFN_HALO), s // FFN_HALO - 1), 0))
    wsp = pl.BlockSpec((2, None, 3, FF_BLK), lambda j, i: (0, j, 0, 0))
    bsp = pl.BlockSpec((2, None, 1, FF_BLK), lambda j, i: (0, j, 0, 0))
    return main, prev, nxt, wsp, bsp


def _ffn_act(u4, w4, b4, name, dep=None):
    s = u4.shape[2]
    tm = min(TM, s)
    main, prev, _, wsp, bsp = _ffn_specs(s, tm)

    def body(u_ref, uh_ref, w_ref, b_ref, o_ref):
        keep = (pl.program_id(1) > 0).astype(F32)
        gw, vw = [jnp.concatenate([uh_ref[p].astype(F32) * keep, u_ref[p].astype(F32)], axis=0) for p in range(2)]
        o_ref[...] = (_silu(_conv3(gw, w_ref[0], b_ref[0])) * _conv3(vw, w_ref[1], b_ref[1])).astype(BF16)

    return _pcall(
        body, (u4, u4, w4, b4), dep, name=name, grid=(FF_NBLK, s // tm), in_specs=[main, prev, wsp, bsp],
        out_specs=pl.BlockSpec((None, tm, FF_BLK), lambda j, i: (j, i, 0)), out_shape=SDS((FF_NBLK, s, FF_BLK), BF16),
        compiler_params=_cp(("parallel", "parallel")))


def _ffn_bwd(u4, dact, w4, b4, name, dep=None):
    s = u4.shape[2]
    tm = min(TM, s)
    nt = s // tm
    main, prev, nxt, wsp, bsp = _ffn_specs(s, tm)
    dmain = pl.BlockSpec((None, tm, FF_BLK), lambda j, i: (j, i, 0))
    dnext = pl.BlockSpec((None, FFN_HALO, FF_BLK), lambda j, i: (j, jnp.minimum((i + 1) * (tm // FFN_HALO), s // FFN_HALO - 1), 0))
    ext = tm + FFN_HALO

    def body(u_ref, up_ref, un_ref, d_ref, dn_ref, w_ref, b_ref, du_ref, dw_ref, db_ref):
        i = pl.program_id(1)
        first = i == 0
        keep_prev = (i > 0).astype(F32)
        keep_next = (i < nt - 1).astype(F32)
        wins = [jnp.concatenate([up_ref[p].astype(F32) * keep_prev, u_ref[p].astype(F32), un_ref[p].astype(F32)], axis=0)
                for p in range(2)]
        gc = _conv3(wins[0], w_ref[0], b_ref[0])
        vc = _conv3(wins[1], w_ref[1], b_ref[1])
        d = jnp.concatenate([d_ref[...].astype(F32), dn_ref[...].astype(F32) * keep_next], axis=0)
        sg = _sigmoid(gc)
        duc = (d * vc * (sg * (1.0 + gc * (1.0 - sg))), d * (gc * sg))
        for p in range(2):
            w = w_ref[p]
            du_ref[p] = (w[2:3, :] * duc[p][:tm, :] + w[1:2, :] * pltpu.roll(duc[p], ext - 1, 0)[:tm, :]
                         + w[0:1, :] * pltpu.roll(duc[p], ext - 2, 0)[:tm, :]).astype(BF16)
            own = duc[p][:tm, :]
            taps = [_colsum(own * (wins[p] if k == 2 else pltpu.roll(wins[p], 2 - k, 0))[FFN_HALO:FFN_HALO + tm, :])
                    for k in range(3)]

            @pl.when(first)
            def _():
                db_ref[p] = _colsum(own)
                for k in range(3):
                    dw_ref[p, k:k + 1, :] = taps[k]

            @pl.when(jnp.logical_not(first))
            def _():
                db_ref[p] += _colsum(own)
                for k in range(3):
                    dw_ref[p, k:k + 1, :] += taps[k]

    return _pcall(
        body, (u4, u4, u4, dact, dact, w4, b4), dep, name=name, grid=(FF_NBLK, nt),
        in_specs=[main, prev, nxt, dmain, dnext, wsp, bsp], out_specs=[main, wsp, bsp],
        out_shape=[SDS(u4.shape, BF16), SDS((2, FF_NBLK, 3, FF_BLK), F32), SDS((2, FF_NBLK, 1, FF_BLK), F32)],
        compiler_params=_cp(("parallel", "arbitrary")))


def _sum8(parts, name):
    _, r, c = parts[0].shape
    tr = r
    for cand in (512, 256, 128, 64, 32, 16):
        if r % cand == 0 and r > cand:
            tr = cand
            break
    nb = r // tr

    def body(*refs):
        o_ref = refs[-1]
        for l, p_ref in enumerate(refs[:-1]):
            @pl.when(pl.program_id(0) == l)
            def _():
                acc = p_ref[0].astype(F32)
                for j in range(1, N_DEV):
                    acc = acc + p_ref[j].astype(F32)
                o_ref[...] = acc

    def spec(l):
        return pl.BlockSpec((N_DEV, tr, c), lambda ll, i: (0, jnp.where(ll == l, i, jnp.where(ll < l, 0, nb - 1)), 0))

    return pl.pallas_call(
        body, name=name, grid=(len(parts), nb), in_specs=[spec(l) for l in range(len(parts))],
        out_specs=pl.BlockSpec((None, tr, c), lambda ll, i: (ll, i, 0)), out_shape=SDS((len(parts), r, c), F32),
        compiler_params=_cp(("arbitrary", "arbitrary")))(*parts)


def _sum8_small(parts, name):
    n = len(parts)

    def body(*refs):
        for p_ref, o_ref in zip(refs[:n], refs[n:]):
            acc = p_ref[0]
            for j in range(1, N_DEV):
                acc = acc + p_ref[j]
            o_ref[...] = acc

    return pl.pallas_call(body, name=name, out_shape=[SDS(p.shape[1:], F32) for p in parts], compiler_params=_cp())(*parts)


def _adamw_math(w, g, m, v):
    m = ADAM_B1 * m + (1.0 - ADAM_B1) * g
    v = ADAM_B2 * v + (1.0 - ADAM_B2) * (g * g)
    m_hat = m / (1.0 - ADAM_B1 ** ADAM_STEP)
    v_hat = v / (1.0 - ADAM_B2 ** ADAM_STEP)
    return -ADAM_LR * (m_hat / (jnp.sqrt(v_hat) + ADAM_EPS) + ADAM_WD * w), m, v


def _adamw(w, g, m, v, name):
    r, c = w.shape
    tr = r
    for cand in (256, 128, 64):
        if r % cand == 0 and r > cand:
            tr = cand
            break

    def body(w_ref, g_ref, m_ref, v_ref, d_ref, mo_ref, vo_ref):
        d_ref[...], mo_ref[...], vo_ref[...] = _adamw_math(w_ref[...], g_ref[...], m_ref[...], v_ref[...])

    blk = pl.BlockSpec((tr, c), lambda i: (i, 0))
    return pl.pallas_call(body, name=name, grid=(r // tr,), in_specs=[blk] * 4, out_specs=[blk] * 3,
                          out_shape=[SDS((r, c), F32)] * 3, compiler_params=_cp(("parallel",)))(w, g, m, v)


def _adamw_small(ws, gs, ms, vs, name):
    n = len(ws)

    def body(*refs):
        ins, outs = refs[:4 * n], refs[4 * n:]
        for i in range(n):
            d, m, v = _adamw_math(ins[i][...], ins[n + i][...], ins[2 * n + i][...], ins[3 * n + i][...])
            outs[i][...], outs[n + i][...], outs[2 * n + i][...] = d, m, v

    shapes = [SDS(w.shape, F32) for w in ws]
    res = pl.pallas_call(body, name=name, out_shape=shapes * 3, compiler_params=_cp())(*ws, *gs, *ms, *vs)
    return res[:n], res[n:2 * n], res[2 * n:]


def _perm_in(w):
    pad = jnp.zeros(w.shape[:-1] + (ZW - 2308,), w.dtype)
    return jnp.concatenate([w[..., :768], w[..., 772:], w[..., 768:772], pad], axis=-1)


def _unperm_in(g):
    return jnp.concatenate([g[..., :768], g[..., 2304:2308], g[..., 768:2304]], axis=-1)


def _wcat(sgu_w):
    return sgu_w.transpose(1, 0, 2).reshape(WIN, 4 * WIN)


def _layer_fwd(l, x, h1, mod, p, wg, last, target, nxt, w_in_next):
    s = x.shape[0]
    tag = f"_l{l}"
    mrow = lambda k: (mod, 6 * l + k)
    z = _mm_rows(h1, wg["w_in"].get(h1), "nn", ZW, BF16, "mm_z" + tag)
    fcol, frow = _fgate(z, p["bf"], "fgate" + tag, dep=wg["w_out"].prefetch(z))
    y_fox, lse_fox = _fox_fwd(z, fcol, frow, "fox_fwd" + tag)
    y_conv, hc = _conv_fwd(z, wg["conv_w"], p["conv_b"], p["conv_ln_g"], p["conv_ln_b"], wg["conv_pw_w"], p["conv_pw_b"],
                           "conv_fwd" + tag)
    y_swa, lse_swa = _swa_fwd(z, p["sinks"], "swa_fwd" + tag)
    y_sgu = _sgu_fwd(z, p["sgu_ln_g"], p["sgu_ln_b"], p["wcat"], p["bfull"], "sgu_fwd" + tag)
    ys = (y_fox, y_conv, y_swa, y_sgu)
    yn = _gnorm(ys, (p["g_group"], l), "gnorm" + tag)
    tok = wg["w_up"].prefetch(yn)
    o = _mm_rows(yn, wg["w_out"].get(yn), "nn", D, BF16, "mm_o" + tag)
    x1, h2 = _post(x, o, mrow(2), (p["g_post_mix"], l), (p["g_pre_ffn"], l), mrow(4), mrow(3), "post_mix" + tag, dep=tok)
    tok = wg["w_down"].prefetch(h2)
    u = _matmul(h2, wg["w_up"].get(h2), "nn", (N_DEV, s, FF_BLK), BF16, (N_DEV, 1, 1),
                _bs((s, D), lambda j, i, k: (0, 0)), _bs((None, D, FF_BLK), lambda j, i, k: (j, 0, 0)),
                _bs((None, s, FF_BLK), lambda j, i, k: (j, 0, 0)), None, "mm_u" + tag)
    u4 = u.reshape(2, FF_NBLK, s, FF_BLK)
    act = _ffn_act(u4, wg["ffn_conv_w"], p["ffn_conv_b"], "ffn_act" + tag, dep=tok)
    tok = None if w_in_next is None else w_in_next.prefetch(act)
    f = _matmul(act, wg["w_down"].get(act), "nn", (s, D), BF16, (1, 1, FF_NBLK),
                _bs((None, s, FF_BLK), lambda i, j, k: (k, 0, 0)), _bs((FF_BLK, D), lambda i, j, k: (k, 0)),
                _bs((s, D), lambda i, j, k: (0, 0)), (s, D), "mm_f" + tag)
    if last:
        out = _post_loss(x1, f, mrow(5), (p["g_post_ffn"], l), target, "post_loss")
    else:
        out = _post(x1, f, mrow(5), (p["g_post_ffn"], l), *nxt, "post_ffn" + tag, dep=tok)
    saved = dict(x=x, h1=h1, z=z, fcol=fcol, frow=frow, lse_fox=lse_fox, hc=hc, lse_swa=lse_swa, ys=ys, yn=yn, o=o, x1=x1,
                 h2=h2, u4=u4, act=act, f=f)
    return out, saved


def _tie(a, token):
    return a if token is None else a + token[0, 0]


def _layer_bwd(l, dx2, sv, mod, p, wg, emit, dep=None):
    s = dx2.shape[0]
    tm = min(TM, s)
    tag = f"_l{l}"
    mrow = lambda k: (mod, 6 * l + k)
    g = {}
    df, g["ga2"], g["g_post_ffn"] = _post_bwd(dx2, sv["f"], mrow(5), (p["g_post_ffn"], l), "post_ffn_bwd" + tag, dep=dep)
    dact = _matmul(df, wg["w_down"].get(None), "nt", (FF_NBLK, s, FF_BLK), BF16, (FF_NBLK, 1, 1),
                   _bs((s, D), lambda j, i, k: (0, 0)), _bs((FF_BLK, D), lambda j, i, k: (j, 0)),
                   _bs((None, s, FF_BLK), lambda j, i, k: (j, 0, 0)), None, "mm_dact" + tag)
    tok = emit("w_down", _matmul(sv["act"], df, "tn", (FF_NBLK * FF_BLK, D), BF16, (FF_NBLK, 1, 1),
                                 _bs((None, s, FF_BLK), lambda j, i, k: (j, 0, 0)), _bs((s, D), lambda j, i, k: (0, 0)),
                                 _bs((FF_BLK, D), lambda j, i, k: (j, 0)), None, "mm_dwdown" + tag))
    du, g["ffn_conv_w"], g["ffn_conv_b"] = _ffn_bwd(sv["u4"], dact, wg["ffn_conv_w"], p["ffn_conv_b"], "ffn_bwd" + tag, dep=tok)
    du = du.reshape(N_DEV, s, FF_BLK)
    dh2 = _matmul(du, wg["w_up"].get(None), "nt", (s, D), BF16, (1, 1, N_DEV),
                  _bs((None, s, FF_BLK), lambda i, j, k: (k, 0, 0)), _bs((None, D, FF_BLK), lambda i, j, k: (k, 0, 0)),
                  _bs((s, D), lambda i, j, k: (0, 0)), (s, D), "mm_dh2" + tag)
    tok = emit("w_up", _matmul(du, sv["h2"], "tn", (N_DEV, FF_BLK, D), BF16, (N_DEV, 1, 1),
                               _bs((None, s, FF_BLK), lambda j, i, k: (j, 0, 0)), _bs((s, D), lambda j, i, k: (0, 0)),
                               _bs((None, FF_BLK, D), lambda j, i, k: (j, 0, 0)), None, "mm_dwup" + tag))
    dx1, g["sh2"], g["sc2"], g["g_pre_ffn"] = _pre_bwd(dh2, sv["x1"], dx2, (p["g_pre_ffn"], l), mrow(4), "pre_ffn_bwd" + tag,
                                                       dep=tok)
    do, g["ga1"], g["g_post_mix"] = _post_bwd(dx1, sv["o"], mrow(2), (p["g_post_mix"], l), "post_mix_bwd" + tag)
    dyn = _mm_rows(do, wg["w_out"].get(None), "nt", D, BF16, "mm_dyn" + tag)
    tok = emit("w_out", _mm_wgrad(sv["yn"], do, BF16, "mm_dwout" + tag))
    dy_fox, dy_conv, dy_swa, dy_sgu, g["g_group"] = _gnorm_bwd(dyn, sv["ys"], (p["g_group"], l), "gnorm_bwd" + tag, dep=tok)
    z = sv["z"]
    dq_f, dk_f, dv_f, dfrow, dfcol = _fox_bwd(z, sv["fcol"], sv["frow"], sv["lse_fox"], sv["ys"][0], dy_fox, "fox_bwd" + tag)
    dgate, g["bf"] = _fgate_bwd(z, p["bf"], dfrow, dfcol, "fgate_bwd" + tag)
    dhc, g["conv_pw_w"], g["conv_pw_b"], g["conv_ln_g"], g["conv_ln_b"], g["conv_w"], g["conv_b"] = _conv_bwd_a(
        z, sv["hc"], dy_conv, wg["conv_w"], p["conv_ln_g"], p["conv_ln_b"], wg["conv_pw_w"], "conv_bwd_a" + tag)
    da_c, dg_c = _conv_bwd_b(z, dhc, wg["conv_w"], "conv_bwd_b" + tag)
    dq_s, dk_s, dv_s, g["sinks"] = _swa_bwd(z, p["sinks"], sv["lse_swa"], sv["ys"][2], dy_swa, "swa_bwd" + tag)
    du_g, dv_g, g["wcat"], g["sgu_bcol"], g["sgu_ln_g"], g["sgu_ln_b"] = _sgu_bwd(
        z, dy_sgu, p["sgu_ln_g"], p["sgu_ln_b"], p["wcat"], p["bfull"], "sgu_bwd" + tag)
    dz = jnp.concatenate([dq_f.astype(BF16), dk_f.astype(BF16), dv_f.astype(BF16), da_c, dg_c, dq_s.astype(BF16), dk_s.astype(BF16),
                          dv_s.astype(BF16), du_g, dv_g, dgate], axis=1)
    tok = emit("w_in", _mm_wgrad(sv["h1"], dz, BF16, "mm_dwin" + tag))
    dh1 = _mm_rows(dz, wg["w_in"].get(None), "nt", D, BF16, "mm_dh1" + tag)
    dx, g["sh1"], g["sc1"], g["g_pre_mix"] = _pre_bwd(dh1, sv["x"], dx1, (p["g_pre_mix"], l), mrow(1), "pre_mix_bwd" + tag,
                                                      dep=tok)
    return dx, g


def _layer_params(l, small, conv_w_full, conv_pw_full, ffn_conv_w_full):
    bf = jnp.pad(small["b_fgate"][l][None, :], ((0, 0), (0, 124)))
    p = dict(
        bf=bf, conv_b=small["conv_b"][l][None], conv_ln_g=small["conv_ln_g"][l][None], conv_ln_b=small["conv_ln_b"][l][None],
        conv_pw_b=small["conv_pw_b"][l][None], sinks=small["swa_sinks"][l], sgu_ln_g=small["sgu_ln_g"][l][None],
        sgu_ln_b=small["sgu_ln_b"][l][None], wcat=_wcat(small["sgu_w"][l]),
        bfull=jnp.repeat(small["sgu_b"][l].T, HD, axis=1),
        ffn_conv_b=small["ffn_conv_b"][l].reshape(2, FF_NBLK, 1, FF_BLK),
        g_group=small["g_group"].reshape(N_LAYER, 1, D), g_post_mix=small["g_post_mix"].reshape(N_LAYER, 1, D),
        g_pre_ffn=small["g_pre_ffn"].reshape(N_LAYER, 1, D), g_post_ffn=small["g_post_ffn"].reshape(N_LAYER, 1, D),
        g_pre_mix=small["g_pre_mix"].reshape(N_LAYER, 1, D))
    wsmall = dict(conv_w=conv_w_full[l], conv_pw_w=conv_pw_full[l].astype(BF16),
                  ffn_conv_w=ffn_conv_w_full[l].reshape(3, 2, FF_NBLK, FF_BLK).transpose(1, 2, 0, 3))
    return p, wsmall


def _local_step(x, target, mod, small, wbig, conv_w_full, conv_pw_full, ffn_conv_w_full, emit, on_loss=None):
    ps, wgs = [], []
    for l in range(N_LAYER):
        p, wsmall = _layer_params(l, small, conv_w_full, conv_pw_full, ffn_conv_w_full)
        ps.append(p)
        wgs.append({**wbig[l], **wsmall})
    h = _rms_mod(x, (ps[0]["g_pre_mix"], 0), (mod, 1), (mod, 0), "rms_mod_l0")
    saved = []
    for l in range(N_LAYER):
        last = l == N_LAYER - 1
        nxt = None if last else ((ps[l]["g_pre_mix"], l + 1), (mod, 6 * (l + 1) + 1), (mod, 6 * (l + 1)))
        out, sv = _layer_fwd(l, x, h, mod, ps[l], wgs[l], last, target, nxt, None if last else wgs[l + 1]["w_in"])
        saved.append(sv)
        if not last:
            x, h = out
    dx, loss = out
    dep = None if on_loss is None else on_loss(loss)
    grads = [None] * N_LAYER
    for l in reversed(range(N_LAYER)):
        dx, grads[l] = _layer_bwd(l, dx, saved[l], mod, ps[l], wgs[l], functools.partial(emit, l), dep)
        dep = None
    return loss, dx, grads


_SMALL = ("b_ada", "g_pre_mix", "g_post_mix", "g_pre_ffn", "g_post_ffn", "b_fgate", "conv_b", "conv_ln_g", "conv_ln_b",
          "conv_pw_b", "swa_sinks", "sgu_ln_g", "sgu_ln_b", "sgu_w", "sgu_b", "g_group", "ffn_conv_b")
_WEIGHTS = ("w_ada", "b_ada", "g_pre_mix", "g_post_mix", "g_pre_ffn", "g_post_ffn", "w_in", "b_fgate", "conv_w", "conv_b",
            "conv_ln_g", "conv_ln_b", "conv_pw_w", "conv_pw_b", "swa_sinks", "sgu_ln_g", "sgu_ln_b", "sgu_w", "sgu_b",
            "g_group", "w_out", "ffn_w_up", "ffn_conv_w", "ffn_conv_b", "ffn_w_down")


def _pad_rows(a, mult):
    r = (-a.shape[0]) % mult
    return a if r == 0 else jnp.concatenate([a, jnp.zeros((r,) + a.shape[1:], a.dtype)], axis=0)


def _view2d(a):
    if a.ndim == 2:
        return a
    return a.reshape(-1, a.shape[-1])


def kernel(x, c, w_ada, b_ada, g_pre_mix, g_post_mix, g_pre_ffn, g_post_ffn, w_in, b_fgate, conv_w, conv_b, conv_ln_g, conv_ln_b, conv_pw_w, conv_pw_b, swa_sinks, sgu_ln_g, sgu_ln_b, sgu_w, sgu_b, g_group, w_out, ffn_w_up, ffn_conv_w, ffn_conv_b, ffn_w_down, loss_target, m_w_ada, m_b_ada, m_g_pre_mix, m_g_post_mix, m_g_pre_ffn, m_g_post_ffn, m_w_in, m_b_fgate, m_conv_w, m_conv_b, m_conv_ln_g, m_conv_ln_b, m_conv_pw_w, m_conv_pw_b, m_swa_sinks, m_sgu_ln_g, m_sgu_ln_b, m_sgu_w, m_sgu_b, m_g_group, m_w_out, m_ffn_w_up, m_ffn_conv_w, m_ffn_conv_b, m_ffn_w_down, v_w_ada, v_b_ada, v_g_pre_mix, v_g_post_mix, v_g_pre_ffn, v_g_post_ffn, v_w_in, v_b_fgate, v_conv_w, v_conv_b, v_conv_ln_g, v_conv_ln_b, v_conv_pw_w, v_conv_pw_b, v_swa_sinks, v_sgu_ln_g, v_sgu_ln_b, v_sgu_w, v_sgu_b, v_g_group, v_w_out, v_ffn_w_up, v_ffn_conv_w, v_ffn_conv_b, v_ffn_w_down):
    env = dict(locals())
    w = {n: env[n] for n in _WEIGHTS}
    mom = {n: env["m_" + n] for n in _WEIGHTS}
    var = {n: env["v_" + n] for n in _WEIGHTS}
    me = 4 * lax.axis_index("x") + 2 * lax.axis_index("y") + lax.axis_index("c")
    x2, target = x[0], loss_target[0]

    (c_all,) = _exchange([c], ["bcast"], "gather_c")
    c_all = c_all.reshape(N_DEV, D)
    (m_all,) = _exchange([_ada_fwd(c_all, w_ada)], ["bcast"], "gather_mod")
    m_mine = lax.dynamic_index_in_dim(m_all, me, axis=2, keepdims=False)
    mod, mod_token = _ada_finish(m_mine.transpose(1, 0, 2).reshape(N_LAYER, 6 * D), b_ada)
    mod = mod.reshape(6 * N_LAYER, 1, D)

    shards = [_tie(conv_w, mod_token), conv_pw_w, ffn_conv_w]
    for l in range(N_LAYER):
        shards += [_perm_in(w_in[l]).astype(BF16), w_out[l].astype(BF16), ffn_w_up[l].astype(BF16), ffn_w_down[l].astype(BF16)]
    gather = _g2_start(shards, "gather_weights_start")
    mod = _tie(mod, gather.token)
    _g2_relay(gather, [0, 1, 2, 3], mod, "gather_relay_first")
    g_cw, g_pw, g_fcw = _g2_wait(gather, [0, 1, 2], mod, "gather_small_wait")
    conv_w_full = g_cw.transpose(1, 2, 0, 3).reshape(N_LAYER, CONV_K, GW)
    conv_pw_full = g_pw.transpose(1, 0, 2, 3).reshape(N_LAYER, GW, GW)
    ffn_conv_w_full = g_fcw.transpose(1, 2, 0, 3).reshape(N_LAYER, 3, N_DEV * FF_BLK)

    def lazy(i, shape, key):
        pre = None if i == 3 else (lambda after: _g2_relay(gather, [i], after, "relay_" + key))
        return _Lazy(lambda after: _g2_wait(gather, [i], after, "wait_" + key)[0].reshape(shape), pre)

    wbig = [dict(w_in=lazy(3 + 4 * l, (D, ZW), f"w_in_l{l}"), w_out=lazy(4 + 4 * l, (D, D), f"w_out_l{l}"),
                 w_up=lazy(5 + 4 * l, (N_DEV, D, FF_BLK), f"w_up_l{l}"),
                 w_down=lazy(6 + 4 * l, (FF_NBLK * FF_BLK, D), f"w_down_l{l}")) for l in range(N_LAYER)]

    grad_flights = []

    def emit(l, key, arr):
        fl = _xchg_start([arr.reshape(N_DEV, -1, arr.shape[-1])], ["a2a"], f"grad_start_{key}_l{l}")
        grad_flights.append(((l, key), fl))
        return fl.token

    small = {n: w[n] for n in _SMALL}
    total = []

    def on_loss(loss8):
        total.append(lax.psum(loss8[0, 0], ("x", "y", "c")))
        return total[0].reshape(1, 1)

    _, dx, grads = _local_step(x2, target, mod, small, wbig, conv_w_full, conv_pw_full, ffn_conv_w_full, emit, on_loss)
    loss = total[0]
    grad_x = dx[None]


    st = lambda key: jnp.stack([grads[l][key] for l in range(N_LAYER)])
    d_conv_w = st("conv_w")[:, :CONV_K, :].reshape(N_LAYER, CONV_K, N_DEV, GW // N_DEV).transpose(2, 0, 1, 3)
    d_pw_w = st("conv_pw_w").reshape(N_LAYER, N_DEV, GW // N_DEV, GW).transpose(1, 0, 2, 3)
    d_fcw = st("ffn_conv_w").reshape(N_LAYER, N_DEV, 3, FF_BLK).transpose(1, 0, 2, 3)
    rows_d = _pad_rows(jnp.concatenate(
        [grads[l][k] for l in range(N_LAYER) for k in ("sh1", "sc1", "ga1", "sh2", "sc2", "ga2")]
        + [grads[l][k] for k in ("g_pre_mix", "g_post_mix", "g_pre_ffn", "g_post_ffn", "g_group") for l in range(N_LAYER)],
        axis=0), 8)
    rows_gw = _pad_rows(jnp.concatenate(
        [grads[l][k] for k in ("conv_b", "conv_ln_g", "conv_ln_b", "conv_pw_b", "sgu_ln_g", "sgu_ln_b") for l in range(N_LAYER)],
        axis=0), 8)
    rows_128 = jnp.concatenate([_pad_rows(jnp.concatenate([grads[l]["bf"] for l in range(N_LAYER)]
                                                          + [grads[l]["sinks"] for l in range(N_LAYER)], axis=0), 8)]
                               + [grads[l]["sgu_bcol"] for l in range(N_LAYER)], axis=0)
    rows_w = jnp.concatenate([grads[l]["wcat"] for l in range(N_LAYER)], axis=0)
    rows_fb = st("ffn_conv_b").reshape(N_LAYER * N_DEV, FF_BLK)
    small_flight = _xchg_start([d_conv_w, d_pw_w, d_fcw, rows_d, rows_gw, rows_128, rows_w, rows_fb],
                               ["a2a"] * 3 + ["bcast"] * 5, "small_grads_start")

    to_mem = {"w_in": lambda a: a.transpose(2, 0, 1), "ffn_w_up": lambda a: a.transpose(0, 2, 1)}
    from_mem = {"w_in": lambda a: a.transpose(1, 2, 0), "ffn_w_up": lambda a: a.transpose(0, 2, 1)}
    flights = dict(grad_flights)
    gr, delta, new_m, new_v = {}, {}, {}, {}

    def adamw_big(n):
        view, back = to_mem.get(n, lambda a: a), from_mem.get(n, lambda a: a)
        shape = view(w[n]).shape
        d, m2, v2 = _adamw(_view2d(view(w[n])), _view2d(gr[n]), _view2d(view(mom[n])), _view2d(view(var[n])), "adamw_" + n)
        delta[n], new_m[n], new_v[n] = back(d.reshape(shape)), back(m2.reshape(shape)), back(v2.reshape(shape))
        gr[n] = back(gr[n].reshape(shape))

    after = small_flight.token
    for n, key in (("ffn_w_down", "w_down"), ("ffn_w_up", "w_up"), ("w_out", "w_out"), ("w_in", "w_in")):
        parts = [_xchg_wait(flights[(l, key)], [0], after, f"grad_wait_{key}_l{l}")[0] for l in reversed(range(N_LAYER))]
        g = _sum8(parts[::-1], "sum_" + key)
        gr[n] = to_mem["w_in"](_unperm_in(g)) if n == "w_in" else g
        adamw_big(n)
        after = new_v[n]

    small_parts = _xchg_wait(small_flight, list(range(8)), after, "small_grads_wait")
    s_conv_w, s_pw_w, s_fcw, s_d, s_gw, s_128, s_w, s_fb = _sum8_small(
        [p.reshape(N_DEV, -1, p.shape[-1]) for p in small_parts], "sum_small_grads")
    gr["conv_w"] = s_conv_w.reshape(N_LAYER, CONV_K, GW // N_DEV)
    gr["conv_pw_w"] = s_pw_w.reshape(N_LAYER, GW // N_DEV, GW)
    gr["ffn_conv_w"] = s_fcw.reshape(N_LAYER, 3, FF_BLK)
    gr["b_ada"] = s_d[:6 * N_LAYER].reshape(N_LAYER, 6 * D)
    for i, k in enumerate(("g_pre_mix", "g_post_mix", "g_pre_ffn", "g_post_ffn", "g_group")):
        gr[k] = s_d[6 * N_LAYER + 2 * i:6 * N_LAYER + 2 * i + 2]
    for i, k in enumerate(("conv_b", "conv_ln_g", "conv_ln_b", "conv_pw_b", "sgu_ln_g", "sgu_ln_b")):
        gr[k] = s_gw[2 * i:2 * i + 2]
    gr["b_fgate"] = s_128[0:2, :4]
    gr["swa_sinks"] = s_128[2:4, :4]
    gr["sgu_b"] = s_128[8:].reshape(N_LAYER, WIN, 128)[:, :, :4].transpose(0, 2, 1)
    gr["sgu_w"] = s_w.reshape(N_LAYER, WIN, 4, WIN).transpose(0, 2, 1, 3)
    gr["ffn_conv_b"] = s_fb.reshape(N_LAYER, N_DEV * FF_BLK)
    dmod_all = small_parts[3][:, :6 * N_LAYER, :].reshape(N_DEV, N_LAYER, 6 * D)
    ncol = 6 * D // N_DEV
    dmod_cols = lax.dynamic_slice_in_dim(dmod_all, me * ncol, ncol, axis=2).transpose(1, 0, 2)
    gr["w_ada"] = _ada_bwd(c_all, dmod_cols)

    adamw_big("w_ada")
    smalls = [n for n in _WEIGHTS if n not in ("w_ada", "w_in", "w_out", "ffn_w_up", "ffn_w_down")]
    ds, ms, vs = _adamw_small([_view2d(w[n]) for n in smalls], [_view2d(gr[n]) for n in smalls],
                              [_view2d(mom[n]) for n in smalls], [_view2d(var[n]) for n in smalls], "adamw_small")
    for i, n in enumerate(smalls):
        delta[n], new_m[n], new_v[n] = ds[i].reshape(w[n].shape), ms[i].reshape(w[n].shape), vs[i].reshape(w[n].shape)

    return (loss, grad_x, *[gr[n].reshape(w[n].shape) for n in _WEIGHTS], *[delta[n] for n in _WEIGHTS],
            *[new_m[n] for n in _WEIGHTS], *[new_v[n] for n in _WEIGHTS])
```

```python
import functools

import jax
import jax.numpy as jnp
from jax import lax
from jax.experimental import pallas as pl
from jax.experimental.pallas import tpu as pltpu

F32, BF16 = jnp.float32, jnp.bfloat16
SDS = jax.ShapeDtypeStruct
MESH = pl.DeviceIdType.MESH

N_DEV = 8
D = 1024
GW = 256
HD = 64
N_LAYER = 2
ZW = 2432
FF_BLK = 704
FF_NBLK = 4
CONV_K = 31
CONV_HALO = 32
FFN_HALO = 16
EPS = 1e-6
NEG = -1e30
SCALE = HD ** -0.5
VMEM_LIMIT_V7X = 56 * 1024 * 1024
TM = 512
WGRAD_ROWS = 256
TQ = 256
WIN = 128

ADAM_LR, ADAM_B1, ADAM_B2, ADAM_EPS, ADAM_WD, ADAM_STEP = 0.001, 0.9, 0.999, 1e-08, 0.01, 10

Z_FQ, Z_FK, Z_FV, Z_CA, Z_CG, Z_SQ = 0, 1, 2, 3, 4, 5
Z_SK, Z_SV = 12, 13
Z_GU, Z_GV = 7, 8
Z_FG = 18


def _cp(sem=None):
    return pltpu.CompilerParams(dimension_semantics=sem, vmem_limit_bytes=VMEM_LIMIT_V7X)


def _vec(arr3, idx, ngrid):
    w = arr3.shape[-1]
    if ngrid == 1:
        return pl.BlockSpec((None, 1, w), lambda i: (idx, 0, 0))
    return pl.BlockSpec((None, 1, w), lambda i, j: (idx, 0, 0))


def _sigmoid(x):
    return jax.nn.sigmoid(x)


def _silu(x):
    return x * _sigmoid(x)


def _dsilu(x):
    s = _sigmoid(x)
    return s * (1.0 + x * (1.0 - s))


_G0, _G1 = 0.7978845608028654, 0.044715


def _gelu(x):
    return 0.5 * x * (1.0 + jnp.tanh(_G0 * (x + _G1 * x * x * x)))


def _dgelu(x):
    t = jnp.tanh(_G0 * (x + _G1 * x * x * x))
    return 0.5 * (1.0 + t) + 0.5 * x * (1.0 - t * t) * (_G0 * (1.0 + 3.0 * _G1 * x * x))


def _rstd(x):
    return lax.rsqrt(jnp.mean(x * x, axis=-1, keepdims=True) + EPS)


def _rms_bwd(xh, r, t):
    return r * (t - xh * jnp.mean(t * xh, axis=-1, keepdims=True))


def _ln_stats(x):
    mu = jnp.mean(x, axis=-1, keepdims=True)
    xc = x - mu
    rstd = lax.rsqrt(jnp.mean(xc * xc, axis=-1, keepdims=True) + EPS)
    return xc * rstd, rstd


def _ln_bwd(xh, rstd, dxh):
    return rstd * (dxh - jnp.mean(dxh, axis=-1, keepdims=True) - xh * jnp.mean(dxh * xh, axis=-1, keepdims=True))


def _colsum(x):
    return jnp.sum(x, axis=0, keepdims=True)


def _dot(a, b, kind):
    dn = {"nn": (((1,), (0,)), ((), ())), "nt": (((1,), (1,)), ((), ())), "tn": (((0,), (0,)), ((), ()))}[kind]
    return lax.dot_general(a.astype(BF16), b.astype(BF16), dn, preferred_element_type=F32)


def _exchange(arrs, modes, name):
    n = len(arrs)
    outs = [SDS((N_DEV,) + a.shape, a.dtype) if m == "bcast" else SDS(a.shape, a.dtype) for a, m in zip(arrs, modes)]

    def body(*refs):
        ins, dst = refs[:n], refs[n:2 * n]
        send, recv, loc = refs[2 * n:]
        x, y, c = lax.axis_index("x"), lax.axis_index("y"), lax.axis_index("c")
        me = 4 * x + 2 * y + c

        def src(a, j):
            return ins[a] if modes[a] == "bcast" else ins[a].at[j]

        local = [pltpu.make_async_copy(src(a, me), dst[a].at[me], loc.at[a]) for a in range(n)]
        for cp in local:
            cp.start()
        sent, landed = [], []
        for k in (2, 4, 6, 3, 5, 7, 1):
            px = 1 - x if k & 4 else x
            py = 1 - y if k & 2 else y
            pc = 1 - c if k & 1 else c
            peer = 4 * px + 2 * py + pc
            for a in range(n):
                cp = pltpu.make_async_remote_copy(src_ref=src(a, peer), dst_ref=dst[a].at[me], send_sem=send.at[a, k - 1],
                                                  recv_sem=recv.at[a, k - 1], device_id=(px, py, pc), device_id_type=MESH)
                cp.start()
                sent.append(cp)
                landed.append(pltpu.make_async_remote_copy(src_ref=src(a, peer), dst_ref=dst[a].at[peer],
                                                           send_sem=send.at[a, k - 1], recv_sem=recv.at[a, k - 1],
                                                           device_id=(px, py, pc), device_id_type=MESH))
        for cp in landed:
            cp.wait_recv()
        for cp in sent:
            cp.wait_send()
        for cp in local:
            cp.wait()

    hbm = pl.BlockSpec(memory_space=pltpu.HBM)
    return pl.pallas_call(
        body, name=name, out_shape=outs, in_specs=[hbm] * n, out_specs=[hbm] * n,
        scratch_shapes=[pltpu.SemaphoreType.DMA((n, N_DEV - 1)), pltpu.SemaphoreType.DMA((n, N_DEV - 1)),
                        pltpu.SemaphoreType.DMA((n,))],
        compiler_params=pltpu.CompilerParams(has_side_effects=True),
    )(*arrs)


_PEER_ORDER = (2, 4, 6, 3, 5, 7, 1)
_HBM = pl.BlockSpec(memory_space=pltpu.HBM)
_SEM = pl.BlockSpec(memory_space=pltpu.SEMAPHORE)
_EFFECT = pltpu.SideEffectType.DATAFLOW_SIDE_EFFECTING


def _peer(k):
    x, y, c = lax.axis_index("x"), lax.axis_index("y"), lax.axis_index("c")
    px = 1 - x if k & 4 else x
    py = 1 - y if k & 2 else y
    pc = 1 - c if k & 1 else c
    return (px, py, pc), 4 * px + 2 * py + pc


def _my_id():
    return 4 * lax.axis_index("x") + 2 * lax.axis_index("y") + lax.axis_index("c")


def _split_copies(src_ref, land_ref, send, recv, loc, mode):
    me = _my_id()
    pick = (lambda j: src_ref) if mode == "bcast" else (lambda j: src_ref.at[j])
    local = pltpu.make_async_copy(pick(me), land_ref.at[me], loc)
    remote = []
    for k in _PEER_ORDER:
        dev, peer = _peer(k)
        out = pltpu.make_async_remote_copy(src_ref=pick(peer), dst_ref=land_ref.at[me], send_sem=send.at[k - 1],
                                           recv_sem=recv.at[k - 1], device_id=dev, device_id_type=MESH)
        arrive = pltpu.make_async_remote_copy(src_ref=pick(peer), dst_ref=land_ref.at[peer], send_sem=send.at[k - 1],
                                              recv_sem=recv.at[k - 1], device_id=dev, device_id_type=MESH)
        remote.append((out, arrive))
    return local, remote


class _Flight:
    def __init__(self, srcs, lands, sends, recvs, locs, modes, token):
        self.srcs, self.lands, self.sends, self.recvs, self.locs, self.modes, self.token = (
            list(srcs), list(lands), list(sends), list(recvs), list(locs), list(modes), token)


def _xchg_start(arrs, modes, name):
    n = len(arrs)
    lands = [lax.empty((N_DEV,) + a.shape if m == "bcast" else a.shape, a.dtype) for a, m in zip(arrs, modes)]

    def body(*refs):
        srcs, lnds = refs[:n], refs[n:2 * n]
        outs = refs[2 * n:]
        sends, recvs, locs, token = outs[:n], outs[n:2 * n], outs[2 * n:3 * n], outs[5 * n]
        for a in range(n):
            local, remote = _split_copies(srcs[a], lnds[a], sends[a], recvs[a], locs[a], modes[a])
            local.start()
            for out, _ in remote:
                out.start()
        token[...] = jnp.zeros_like(token)

    sem7 = pltpu.SemaphoreType.DMA((N_DEV - 1,))
    res = pl.pallas_call(
        body, name=name,
        out_shape=[sem7] * (2 * n) + [pltpu.SemaphoreType.DMA(())] * n + [pltpu.HBM(a.shape, a.dtype) for a in arrs]
        + [pltpu.HBM(b.shape, b.dtype) for b in lands] + [SDS((8, 128), F32)],
        in_specs=[_HBM] * (2 * n), out_specs=[_SEM] * (3 * n) + [_HBM] * (2 * n) + [pl.BlockSpec(memory_space=pltpu.VMEM)],
        input_output_aliases={i: 3 * n + i for i in range(2 * n)},
        compiler_params=pltpu.CompilerParams(has_side_effects=_EFFECT),
    )(*[pltpu.with_memory_space_constraint(a, pltpu.HBM) for a in arrs],
      *[pltpu.with_memory_space_constraint(b, pltpu.HBM) for b in lands])
    return _Flight(res[3 * n:4 * n], res[4 * n:5 * n], res[:n], res[n:2 * n], res[2 * n:3 * n], modes, res[5 * n])


def _xchg_wait(flight, idx, after, name):
    n = len(idx)
    modes = [flight.modes[i] for i in idx]

    def body(*refs):
        srcs, lnds = refs[:n], refs[n:2 * n]
        sends, recvs, locs = refs[2 * n:3 * n], refs[3 * n:4 * n], refs[4 * n:5 * n]
        for a in range(n):
            local, remote = _split_copies(srcs[a], lnds[a], sends[a], recvs[a], locs[a], modes[a])
            local.wait()
            for _, arrive in remote:
                arrive.wait_send()
                arrive.wait_recv()

    ops = ([flight.srcs[i] for i in idx] + [flight.lands[i] for i in idx] + [flight.sends[i] for i in idx]
           + [flight.recvs[i] for i in idx] + [flight.locs[i] for i in idx])
    res = pl.pallas_call(
        body, name=name, out_shape=[pltpu.HBM(o.shape, o.dtype) for o in ops[:2 * n]],
        in_specs=[_HBM] * (2 * n) + [_SEM] * (3 * n) + [pl.BlockSpec(memory_space=pl.ANY)], out_specs=[_HBM] * (2 * n),
        input_output_aliases={i: i for i in range(2 * n)},
        compiler_params=pltpu.CompilerParams(has_side_effects=_EFFECT),
    )(*ops, after)
    return res[n:]


class _Lazy:
    def __init__(self, fn, pre=None):
        self.fn, self.pre, self.val, self.started = fn, pre, None, False

    def prefetch(self, after):
        token = self.pre(after) if self.pre is not None and not self.started else None
        self.started = True
        return token

    def get(self, after):
        self.prefetch(after)
        if self.val is None:
            self.val = self.fn(after)
        return self.val


_CHIP_PEERS = (2, 4, 6)


def _g2_copies_a(src_ref, land_ref, send, recv, loc):
    me = _my_id()
    local = pltpu.make_async_copy(src_ref, land_ref.at[me], loc)
    remote = []
    for j, k in enumerate(_CHIP_PEERS + (1,)):
        dev, peer = _peer(k)
        out = pltpu.make_async_remote_copy(src_ref=src_ref, dst_ref=land_ref.at[me], send_sem=send.at[j], recv_sem=recv.at[j],
                                           device_id=dev, device_id_type=MESH)
        arrive = pltpu.make_async_remote_copy(src_ref=src_ref, dst_ref=land_ref.at[peer], send_sem=send.at[j],
                                              recv_sem=recv.at[j], device_id=dev, device_id_type=MESH)
        remote.append((out, arrive))
    return local, remote


def _g2_copies_b(land_ref, send, recv):
    sib, _ = _peer(1)
    pairs = []
    for j, k in enumerate(_CHIP_PEERS):
        _, same_core = _peer(k)
        _, other_core = _peer(k | 1)
        out = pltpu.make_async_remote_copy(src_ref=land_ref.at[same_core], dst_ref=land_ref.at[same_core], send_sem=send.at[j],
                                           recv_sem=recv.at[j], device_id=sib, device_id_type=MESH)
        arrive = pltpu.make_async_remote_copy(src_ref=land_ref.at[same_core], dst_ref=land_ref.at[other_core],
                                              send_sem=send.at[j], recv_sem=recv.at[j], device_id=sib, device_id_type=MESH)
        pairs.append((out, arrive))
    return pairs


class _Gather2:
    def __init__(self, srcs, lands, sends, recvs, locs, token):
        self.srcs, self.lands, self.sends, self.recvs, self.locs, self.token = (
            list(srcs), list(lands), list(sends), list(recvs), list(locs), token)
        self.sends_b, self.recvs_b = [None] * len(self.srcs), [None] * len(self.srcs)


def _g2_start(arrs, name):
    n = len(arrs)
    lands = [lax.empty((N_DEV,) + a.shape, a.dtype) for a in arrs]

    def body(*refs):
        srcs, lnds = refs[:n], refs[n:2 * n]
        outs = refs[2 * n:]
        sends, recvs, locs, token = outs[:n], outs[n:2 * n], outs[2 * n:3 * n], outs[5 * n]
        for a in range(n):
            local, remote = _g2_copies_a(srcs[a], lnds[a], sends[a], recvs[a], locs[a])
            local.start()
            for out, _ in remote:
                out.start()
        token[...] = jnp.zeros_like(token)

    sem4 = pltpu.SemaphoreType.DMA((4,))
    res = pl.pallas_call(
        body, name=name,
        out_shape=[sem4] * (2 * n) + [pltpu.SemaphoreType.DMA(())] * n + [pltpu.HBM(a.shape, a.dtype) for a in arrs]
        + [pltpu.HBM(b.shape, b.dtype) for b in lands] + [SDS((8, 128), F32)],
        in_specs=[_HBM] * (2 * n), out_specs=[_SEM] * (3 * n) + [_HBM] * (2 * n) + [pl.BlockSpec(memory_space=pltpu.VMEM)],
        input_output_aliases={i: 3 * n + i for i in range(2 * n)},
        compiler_params=pltpu.CompilerParams(has_side_effects=_EFFECT),
    )(*[pltpu.with_memory_space_constraint(a, pltpu.HBM) for a in arrs],
      *[pltpu.with_memory_space_constraint(b, pltpu.HBM) for b in lands])
    return _Gather2(res[3 * n:4 * n], res[4 * n:5 * n], res[:n], res[n:2 * n], res[2 * n:3 * n], res[5 * n])


def _g2_relay(g, idx, after, name):
    n = len(idx)

    def body(*refs):
        srcs, lnds = refs[:n], refs[n:2 * n]
        sends, recvs, locs = refs[2 * n:3 * n], refs[3 * n:4 * n], refs[4 * n:5 * n]
        outs = refs[5 * n + 1:]
        sends_b, recvs_b = outs[2 * n:3 * n], outs[3 * n:4 * n]
        for a in range(n):
            local, remote = _g2_copies_a(srcs[a], lnds[a], sends[a], recvs[a], locs[a])
            local.wait()
            for _, arrive in remote:
                arrive.wait_send()
                arrive.wait_recv()
        for a in range(n):
            for out, _ in _g2_copies_b(lnds[a], sends_b[a], recvs_b[a]):
                out.start()
        outs[4 * n][...] = jnp.zeros_like(outs[4 * n])

    ops = ([g.srcs[i] for i in idx] + [g.lands[i] for i in idx] + [g.sends[i] for i in idx] + [g.recvs[i] for i in idx]
           + [g.locs[i] for i in idx])
    sem3 = pltpu.SemaphoreType.DMA((3,))
    res = pl.pallas_call(
        body, name=name, out_shape=[pltpu.HBM(o.shape, o.dtype) for o in ops[:2 * n]] + [sem3] * (2 * n) + [SDS((8, 128), F32)],
        in_specs=[_HBM] * (2 * n) + [_SEM] * (3 * n) + [pl.BlockSpec(memory_space=pl.ANY)],
        out_specs=[_HBM] * (2 * n) + [_SEM] * (2 * n) + [pl.BlockSpec(memory_space=pltpu.VMEM)],
        input_output_aliases={i: i for i in range(2 * n)},
        compiler_params=pltpu.CompilerParams(has_side_effects=_EFFECT),
    )(*ops, after)
    for a, i in enumerate(idx):
        g.srcs[i], g.lands[i] = res[a], res[n + a]
        g.sends_b[i], g.recvs_b[i] = res[2 * n + a], res[3 * n + a]
    return res[4 * n]


def _g2_wait(g, idx, after, name):
    n = len(idx)

    def body(*refs):
        lnds, sends_b, recvs_b = refs[:n], refs[n:2 * n], refs[2 * n:3 * n]
        for a in range(n):
            for _, arrive in _g2_copies_b(lnds[a], sends_b[a], recvs_b[a]):
                arrive.wait_send()
                arrive.wait_recv()

    ops = [g.lands[i] for i in idx] + [g.sends_b[i] for i in idx] + [g.recvs_b[i] for i in idx]
    res = pl.pallas_call(
        body, name=name, out_shape=[pltpu.HBM(o.shape, o.dtype) for o in ops[:n]],
        in_specs=[_HBM] * n + [_SEM] * (2 * n) + [pl.BlockSpec(memory_space=pl.ANY)], out_specs=[_HBM] * n,
        input_output_aliases={i: i for i in range(n)},
        compiler_params=pltpu.CompilerParams(has_side_effects=_EFFECT),
    )(*ops, after)
    return list(res)


def _matmul(a, b, kind, out_shape, out_dtype, grid, a_spec, b_spec, o_spec, acc_shape, name):
    nk = grid[2]

    def body(a_ref, b_ref, o_ref, *scratch):
        prod = _dot(a_ref[...], b_ref[...], kind)
        if nk == 1:
            o_ref[...] = prod.astype(out_dtype)
        else:
            acc = scratch[0]
            k = pl.program_id(2)

            @pl.when(k == 0)
            def _():
                acc[...] = prod

            @pl.when(k > 0)
            def _():
                acc[...] += prod

            @pl.when(k == nk - 1)
            def _():
                o_ref[...] = acc[...].astype(out_dtype)

    return pl.pallas_call(
        body, name=name, grid=grid, in_specs=[a_spec, b_spec], out_specs=o_spec, out_shape=SDS(out_shape, out_dtype),
        scratch_shapes=[] if nk == 1 else [pltpu.VMEM(acc_shape, F32)],
        compiler_params=_cp(("parallel", "parallel", "arbitrary")))(a, b)


def _bs(shape, fn):
    return pl.BlockSpec(shape, fn)


def _mm_rows(a, w, kind, n_out, out_dtype, name):
    s, k = a.shape
    tm = min(TM, s)
    return _matmul(a, w, kind, (s, n_out), out_dtype, (s // tm, 1, 1),
                   _bs((tm, k), lambda i, j, kk: (i, 0)), _bs(w.shape, lambda i, j, kk: (0, 0)),
                   _bs((tm, n_out), lambda i, j, kk: (i, 0)), None, name)


def _mm_wgrad(a, dy, out_dtype, name):
    s, k = a.shape
    n = dy.shape[1]
    tko = min(WGRAD_ROWS, k)
    return _matmul(a, dy, "tn", (k, n), out_dtype, (k // tko, 1, 1),
                   _bs((s, tko), lambda i, j, kk: (0, i)), _bs((s, n), lambda i, j, kk: (0, 0)),
                   _bs((tko, n), lambda i, j, kk: (i, 0)), None, name)


def _ada_fwd(c_all, w_ada):
    ncol = w_ada.shape[2]

    def body(c_ref, w_ref, o_ref):
        ca = _silu(c_ref[...])
        ca = jnp.concatenate([ca, jnp.zeros_like(ca)], axis=0)
        o_ref[...] = _dot(ca, w_ref[...], "nn")[:N_DEV, :]

    return pl.pallas_call(
        body, name="ada_fwd", grid=(N_LAYER,),
        in_specs=[pl.BlockSpec((N_DEV, D), lambda l: (0, 0)), pl.BlockSpec((None, D, ncol), lambda l: (l, 0, 0))],
        out_specs=pl.BlockSpec((None, N_DEV, ncol), lambda l: (l, 0, 0)),
        out_shape=SDS((N_LAYER, N_DEV, ncol), F32), compiler_params=_cp(("parallel",)))(c_all, w_ada)


def _ada_finish(m_mine, b_ada):
    def body(m_ref, b_ref, o_ref, t_ref):
        o_ref[...] = m_ref[...] + b_ref[...]
        t_ref[...] = jnp.zeros_like(t_ref)

    return pl.pallas_call(body, name="ada_finish", out_shape=[SDS(b_ada.shape, F32), SDS((8, 128), F32)])(m_mine, b_ada)


def _ada_bwd(c_all, dmod_cols):
    ncol = dmod_cols.shape[2]

    def body(c_ref, d_ref, o_ref):
        ca = _silu(c_ref[...])
        ca = jnp.concatenate([ca, jnp.zeros_like(ca)], axis=0)
        dm = d_ref[...]
        dm = jnp.concatenate([dm, jnp.zeros_like(dm)], axis=0)
        o_ref[...] = _dot(ca, dm, "tn")

    return pl.pallas_call(
        body, name="ada_bwd", grid=(N_LAYER,),
        in_specs=[pl.BlockSpec((N_DEV, D), lambda l: (0, 0)), pl.BlockSpec((None, N_DEV, ncol), lambda l: (l, 0, 0))],
        out_specs=pl.BlockSpec((None, D, ncol), lambda l: (l, 0, 0)),
        out_shape=SDS((N_LAYER, D, ncol), F32), compiler_params=_cp(("parallel",)))(c_all, dmod_cols)


def _rows(s):
    tm = min(TM, s)
    return tm, pl.BlockSpec((tm, D), lambda i: (i, 0))


def _pcall(body, operands, dep, **kw):
    if dep is None:
        return pl.pallas_call(body, **kw)(*operands)
    n = len(operands)

    def body_dep(*refs):
        body(*refs[:n], *refs[n + 1:])

    kw["in_specs"] = list(kw["in_specs"]) + [pl.BlockSpec(memory_space=pl.ANY)]
    return pl.pallas_call(body_dep, **kw)(*operands, dep)


def _rms_mod(x, g, sc, sh, name):
    s = x.shape[0]
    tm, row = _rows(s)

    def body(x_ref, g_ref, sc_ref, sh_ref, h_ref):
        xf = x_ref[...]
        h_ref[...] = (xf * _rstd(xf) * (g_ref[...] * (1.0 + sc_ref[...])) + sh_ref[...]).astype(BF16)

    return pl.pallas_call(
        body, name=name, grid=(s // tm,), in_specs=[row, _vec(*g, 1), _vec(*sc, 1), _vec(*sh, 1)], out_specs=row,
        out_shape=SDS((s, D), BF16), compiler_params=_cp(("parallel",)))(x, g[0], sc[0], sh[0])


def _post(xres, o, ga, gpost, gn, scn, shn, name, dep=None):
    s = xres.shape[0]
    tm, row = _rows(s)

    def body(x_ref, o_ref, ga_ref, gp_ref, gn_ref, sc_ref, sh_ref, xn_ref, h_ref):
        of = o_ref[...].astype(F32)
        xn = x_ref[...] + ga_ref[...] * (of * _rstd(of) * gp_ref[...])
        xn_ref[...] = xn
        h_ref[...] = (xn * _rstd(xn) * (gn_ref[...] * (1.0 + sc_ref[...])) + sh_ref[...]).astype(BF16)

    return _pcall(
        body, (xres, o, ga[0], gpost[0], gn[0], scn[0], shn[0]), dep, name=name, grid=(s // tm,),
        in_specs=[row, row, _vec(*ga, 1), _vec(*gpost, 1), _vec(*gn, 1), _vec(*scn, 1), _vec(*shn, 1)],
        out_specs=[row, row], out_shape=[SDS((s, D), F32), SDS((s, D), BF16)], compiler_params=_cp(("parallel",)))


def _post_loss(xres, o, ga, gpost, target, name, dep=None):
    s = xres.shape[0]
    tm, row = _rows(s)

    def body(x_ref, o_ref, ga_ref, gp_ref, t_ref, dy_ref, loss_ref):
        of = o_ref[...].astype(F32)
        err = x_ref[...] + ga_ref[...] * (of * _rstd(of) * gp_ref[...]) - t_ref[...]
        dy_ref[...] = err * (1.0 / D)

        @pl.when(pl.program_id(0) == 0)
        def _():
            loss_ref[...] = jnp.zeros_like(loss_ref)

        loss_ref[...] += jnp.sum(jnp.mean(err * err, axis=-1, keepdims=True), axis=0, keepdims=True) * 0.5

    return _pcall(
        body, (xres, o, ga[0], gpost[0], target), dep, name=name, grid=(s // tm,),
        in_specs=[row, row, _vec(*ga, 1), _vec(*gpost, 1), row],
        out_specs=[row, pl.BlockSpec((8, 128), lambda i: (0, 0))], out_shape=[SDS((s, D), F32), SDS((8, 128), F32)],
        compiler_params=_cp(("arbitrary",)))


def _acc(ref, val, first):
    @pl.when(first)
    def _():
        ref[...] = val

    @pl.when(jnp.logical_not(first))
    def _():
        ref[...] += val


def _post_bwd(dxn, o, ga, gpost, name, dep=None):
    s = dxn.shape[0]
    tm, row = _rows(s)
    vec = pl.BlockSpec((1, D), lambda i: (0, 0))

    def body(d_ref, o_ref, ga_ref, gp_ref, do_ref, dga_ref, dgp_ref):
        of, dx = o_ref[...].astype(F32), d_ref[...]
        r = _rstd(of)
        oh = of * r
        do_ref[...] = _rms_bwd(oh, r, dx * (ga_ref[...] * gp_ref[...])).astype(BF16)
        cs = _colsum(dx * oh)
        first = pl.program_id(0) == 0
        _acc(dga_ref, cs * gp_ref[...], first)
        _acc(dgp_ref, cs * ga_ref[...], first)

    return _pcall(
        body, (dxn, o, ga[0], gpost[0]), dep, name=name, grid=(s // tm,),
        in_specs=[row, row, _vec(*ga, 1), _vec(*gpost, 1)], out_specs=[row, vec, vec],
        out_shape=[SDS((s, D), BF16), SDS((1, D), F32), SDS((1, D), F32)], compiler_params=_cp(("arbitrary",)))


def _pre_bwd(dh, x, dres, g, sc, name, dep=None):
    s = x.shape[0]
    tm, row = _rows(s)
    vec = pl.BlockSpec((1, D), lambda i: (0, 0))

    def body(dh_ref, x_ref, dr_ref, g_ref, sc_ref, dx_ref, dsh_ref, dsc_ref, dg_ref):
        xf, d = x_ref[...], dh_ref[...].astype(F32)
        r = _rstd(xf)
        xh = xf * r
        dx_ref[...] = dr_ref[...] + _rms_bwd(xh, r, d * (g_ref[...] * (1.0 + sc_ref[...])))
        cs = _colsum(d * xh)
        first = pl.program_id(0) == 0
        _acc(dsh_ref, _colsum(d), first)
        _acc(dsc_ref, cs * g_ref[...], first)
        _acc(dg_ref, cs * (1.0 + sc_ref[...]), first)

    return _pcall(
        body, (dh, x, dres, g[0], sc[0]), dep, name=name, grid=(s // tm,),
        in_specs=[row, row, row, _vec(*g, 1), _vec(*sc, 1)], out_specs=[row, vec, vec, vec],
        out_shape=[SDS((s, D), F32), SDS((1, D), F32), SDS((1, D), F32), SDS((1, D), F32)],
        compiler_params=_cp(("arbitrary",)))


def _gnorm(ys, gg, name):
    s = ys[0].shape[0]
    tm = min(TM, s)
    yb = pl.BlockSpec((tm, GW), lambda i: (i, 0))

    def body(y0, y1, y2, y3, g_ref, o_ref):
        for i, yr in enumerate((y0, y1, y2, y3)):
            y = yr[...]
            o_ref[:, GW * i:GW * (i + 1)] = (y * _rstd(y) * g_ref[:, GW * i:GW * (i + 1)]).astype(BF16)

    return pl.pallas_call(
        body, name=name, grid=(s // tm,), in_specs=[yb] * 4 + [_vec(*gg, 1)], out_specs=pl.BlockSpec((tm, D), lambda i: (i, 0)),
        out_shape=SDS((s, D), BF16), compiler_params=_cp(("parallel",)))(*ys, gg[0])


def _gnorm_bwd(dyn, ys, gg, name, dep=None):
    s = ys[0].shape[0]
    tm = min(TM, s)
    yb = pl.BlockSpec((tm, GW), lambda i: (i, 0))

    def body(d_ref, y0, y1, y2, y3, g_ref, o0, o1, o2, o3, dg_ref):
        first = pl.program_id(0) == 0
        for i, (yr, orf) in enumerate(zip((y0, y1, y2, y3), (o0, o1, o2, o3))):
            y = yr[...]
            d = d_ref[:, GW * i:GW * (i + 1)].astype(F32)
            r = _rstd(y)
            yh = y * r
            orf[...] = _rms_bwd(yh, r, d * g_ref[:, GW * i:GW * (i + 1)]).astype(BF16)
            cs = _colsum(d * yh)

            @pl.when(first)
            def _():
                dg_ref[:, GW * i:GW * (i + 1)] = cs

            @pl.when(jnp.logical_not(first))
            def _():
                dg_ref[:, GW * i:GW * (i + 1)] += cs

    return _pcall(
        body, (dyn, *ys, gg[0]), dep, name=name, grid=(s // tm,),
        in_specs=[pl.BlockSpec((tm, D), lambda i: (i, 0))] + [yb] * 4 + [_vec(*gg, 1)],
        out_specs=[yb] * 4 + [pl.BlockSpec((1, D), lambda i: (0, 0))],
        out_shape=[SDS((s, GW), BF16)] * 4 + [SDS((1, D), F32)], compiler_params=_cp(("arbitrary",)))


def _lane_put(acc, col, h):
    lane = lax.broadcasted_iota(jnp.int32, acc.shape, 1)
    return jnp.where(lane == h, col, acc)


def _fgate(z, bf, name, dep=None):
    s = z.shape[0]

    def body(z_ref, b_ref, fc_ref, fr_ref):
        xg = z_ref[...].astype(F32) + b_ref[...]
        lf = jnp.minimum(xg, 0.0) - jnp.log(1.0 + jnp.exp(-jnp.abs(xg)))
        lane = lax.broadcasted_iota(jnp.int32, lf.shape, 1)
        row = lax.broadcasted_iota(jnp.int32, lf.shape, 0)
        f = jnp.where(lane < 4, lf, 0.0)
        sh = 1
        while sh < s:
            f = f + jnp.where(row >= sh, pltpu.roll(f, sh, 0), 0.0)
            sh *= 2
        fc_ref[...] = f
        fr_ref[...] = f.T[:8, :]

    return _pcall(
        body, (z, bf), dep, name=name, grid=(1,),
        in_specs=[pl.BlockSpec((s, 128), lambda i: (0, Z_FG)), pl.BlockSpec((1, 128), lambda i: (0, 0))],
        out_specs=[pl.BlockSpec((s, 128), lambda i: (0, 0)), pl.BlockSpec((8, s), lambda i: (0, 0))],
        out_shape=[SDS((s, 128), F32), SDS((8, s), F32)], compiler_params=_cp(("arbitrary",)))


def _fgate_bwd(z, bf, dfrow, dfcol, name):
    s = z.shape[0]

    def body(z_ref, b_ref, d_ref, dc_ref, dz_ref, db_ref):
        d = jnp.concatenate([d_ref[...], jnp.zeros((120, s), F32)], axis=0).T + dc_ref[...]
        row = lax.broadcasted_iota(jnp.int32, d.shape, 0)
        lane = lax.broadcasted_iota(jnp.int32, d.shape, 1)
        sh = 1
        while sh < s:
            d = d + jnp.where(row < s - sh, pltpu.roll(d, s - sh, 0), 0.0)
            sh *= 2
        xg = z_ref[...].astype(F32) + b_ref[...]
        dz = jnp.where(lane < 4, d * _sigmoid(-xg), 0.0)
        dz_ref[...] = dz.astype(BF16)
        db_ref[...] = _colsum(dz)

    return pl.pallas_call(
        body, name=name, grid=(1,),
        in_specs=[pl.BlockSpec((s, 128), lambda i: (0, Z_FG)), pl.BlockSpec((1, 128), lambda i: (0, 0)),
                  pl.BlockSpec((8, s), lambda i: (0, 0)), pl.BlockSpec((s, 128), lambda i: (0, 0))],
        out_specs=[pl.BlockSpec((s, 128), lambda i: (0, 0)), pl.BlockSpec((1, 128), lambda i: (0, 0))],
        out_shape=[SDS((s, 128), BF16), SDS((1, 128), F32)], compiler_params=_cp(("arbitrary",)))(z, bf, dfrow, dfcol)


def _fox_scores(q_ref, k_ref, fc_ref, fr_ref, h, i, nk, tq):
    kw = nk * tq
    q = q_ref[:, HD * h:HD * (h + 1)] * SCALE
    sc = _dot(q, k_ref[0:kw, HD * h:HD * (h + 1)], "nt") + fc_ref[:, h:h + 1] - fr_ref[h:h + 1, 0:kw]
    qpos = i * tq + lax.broadcasted_iota(jnp.int32, (tq, kw), 0)
    kpos = lax.broadcasted_iota(jnp.int32, (tq, kw), 1)
    return q, jnp.where(kpos <= qpos, sc, NEG)


def _fox_fwd(z, fcol, frow, name):
    s = z.shape[0]
    tq = min(TQ, s)
    nc = s // tq

    def body(q_ref, k_ref, v_ref, fc_ref, fr_ref, y_ref, l_ref):
        for n in range(nc):
            @pl.when(pl.program_id(0) == n)
            def _():
                kw = (n + 1) * tq
                lse = jnp.zeros((tq, 128), F32)
                for h in range(4):
                    _, sc = _fox_scores(q_ref, k_ref, fc_ref, fr_ref, h, n, n + 1, tq)
                    m = jnp.max(sc, axis=-1, keepdims=True)
                    p = jnp.exp(sc - m)
                    l = jnp.sum(p, axis=-1, keepdims=True)
                    y_ref[:, HD * h:HD * (h + 1)] = _dot(p, v_ref[0:kw, HD * h:HD * (h + 1)], "nn") / l
                    lse = _lane_put(lse, m + jnp.log(l), h)
                l_ref[...] = lse

    return pl.pallas_call(
        body, name=name, grid=(nc,),
        in_specs=[pl.BlockSpec((tq, GW), lambda i: (i, Z_FQ)), pl.BlockSpec((s, GW), lambda i: (0, Z_FK)),
                  pl.BlockSpec((s, GW), lambda i: (0, Z_FV)), pl.BlockSpec((tq, 128), lambda i: (i, 0)),
                  pl.BlockSpec((8, s), lambda i: (0, 0))],
        out_specs=[pl.BlockSpec((tq, GW), lambda i: (i, 0)), pl.BlockSpec((tq, 128), lambda i: (i, 0))],
        out_shape=[SDS((s, GW), F32), SDS((s, 128), F32)], compiler_params=_cp(("parallel",)))(z, z, z, fcol, frow)


def _fox_bwd(z, fcol, frow, lse, y, dy, name):
    s = z.shape[0]
    tq = min(TQ, s)
    nc = s // tq
    half = max(nc // 2, 1)

    def body(q_ref, k_ref, v_ref, fc_ref, fr_ref, l_ref, y_ref, dy_ref, dq_ref, dk_ref, dv_ref, df_ref, dfq_ref):
        @pl.when(pl.program_id(0) == 0)
        def _():
            dk_ref[...] = jnp.zeros_like(dk_ref)
            dv_ref[...] = jnp.zeros_like(dv_ref)
            df_ref[...] = jnp.zeros_like(df_ref)

        i = pl.program_id(0)
        for cond, nk in ((i < half, half), (i >= half, nc)):
            @pl.when(cond)
            def _():
                kw = nk * tq
                dfq = jnp.zeros((tq, 128), F32)
                dyf = dy_ref[...].astype(F32)
                for h in range(4):
                    hs = slice(HD * h, HD * (h + 1))
                    q, sc = _fox_scores(q_ref, k_ref, fc_ref, fr_ref, h, i, nk, tq)
                    p = jnp.exp(sc - l_ref[:, h:h + 1])
                    dyh = dyf[:, hs]
                    dd = jnp.sum(dyh * y_ref[:, hs], axis=-1, keepdims=True)
                    ds = p * (_dot(dyh, v_ref[0:kw, hs], "nt") - dd)
                    dq_ref[:, hs] = _dot(ds, k_ref[0:kw, hs], "nn") * SCALE
                    dk_ref[0:kw, hs] += _dot(ds, q, "tn")
                    dv_ref[0:kw, hs] += _dot(p, dyh, "tn")
                    df_ref[h:h + 1, 0:kw] -= _colsum(ds)
                    dfq = _lane_put(dfq, jnp.sum(ds, axis=-1, keepdims=True), h)
                dfq_ref[...] = dfq

    tile = lambda w: pl.BlockSpec((tq, w), lambda i: (i, 0))
    full = pl.BlockSpec((s, GW), lambda i: (0, 0))
    rows8 = pl.BlockSpec((8, s), lambda i: (0, 0))
    return pl.pallas_call(
        body, name=name, grid=(nc,),
        in_specs=[pl.BlockSpec((tq, GW), lambda i: (i, Z_FQ)), pl.BlockSpec((s, GW), lambda i: (0, Z_FK)),
                  pl.BlockSpec((s, GW), lambda i: (0, Z_FV)), tile(128), rows8, tile(128), tile(GW), tile(GW)],
        out_specs=[tile(GW), full, full, rows8, tile(128)],
        out_shape=[SDS((s, GW), F32), SDS((s, GW), F32), SDS((s, GW), F32), SDS((8, s), F32), SDS((s, 128), F32)],
        compiler_params=_cp(("arbitrary",)))(z, z, z, fcol, frow, lse, y, dy)


def _swa_block(q_ref, k_ref, v_ref, n):
    qs = pl.multiple_of(n * WIN, WIN)
    ks = pl.multiple_of(jnp.maximum(n - 1, 0) * WIN, WIN)
    qb = q_ref[pl.ds(qs, WIN), :]
    kb = k_ref[pl.ds(ks, 2 * WIN), :]
    vb = v_ref[pl.ds(ks, 2 * WIN), :]
    rows = lax.broadcasted_iota(jnp.int32, (2 * WIN, 2 * WIN), 0) & (WIN - 1)
    dist = (qs + rows) - (ks + lax.broadcasted_iota(jnp.int32, (2 * WIN, 2 * WIN), 1))
    return qs, ks, qb, kb, vb, (dist >= 0) & (dist < WIN)


def _stack2(x, kvh):
    return jnp.concatenate([x[:, HD * (2 * kvh):HD * (2 * kvh + 1)], x[:, HD * (2 * kvh + 1):HD * (2 * kvh + 2)]], axis=0)


def _sink2(sink_ref, kvh):
    top = lax.broadcasted_iota(jnp.int32, (2 * WIN, 1), 0) < WIN
    return jnp.where(top, sink_ref[2 * kvh], sink_ref[2 * kvh + 1])


def _swa_fwd(z, sinks, name):
    s = z.shape[0]

    def body(sink_ref, q_ref, k_ref, v_ref, y_ref, l_ref):
        def step(n, carry):
            qs, ks, qb, kb, vb, valid = _swa_block(q_ref, k_ref, v_ref, n)
            lse = jnp.zeros((WIN, 128), F32)
            for kvh in range(2):
                kv = slice(HD * kvh, HD * (kvh + 1))
                sc = jnp.where(valid, _dot(_stack2(qb, kvh) * SCALE, kb[:, kv], "nt"), NEG)
                sink = _sink2(sink_ref, kvh)
                m = jnp.maximum(jnp.max(sc, axis=-1, keepdims=True), sink)
                p = jnp.exp(sc - m)
                den = jnp.sum(p, axis=-1, keepdims=True) + jnp.exp(sink - m)
                o = _dot(p, vb[:, kv], "nn") / den
                lrow = m + jnp.log(den)
                for j in range(2):
                    h = 2 * kvh + j
                    y_ref[pl.ds(qs, WIN), HD * h:HD * (h + 1)] = o[WIN * j:WIN * (j + 1), :]
                    lse = _lane_put(lse, lrow[WIN * j:WIN * (j + 1), :], h)
            l_ref[pl.ds(qs, WIN), :] = lse
            return carry

        lax.fori_loop(0, s // WIN, step, 0, unroll=2)

    return pl.pallas_call(
        body, name=name, grid=(1,),
        in_specs=[pl.BlockSpec(memory_space=pltpu.SMEM), pl.BlockSpec((s, GW), lambda i: (0, Z_SQ)),
                  pl.BlockSpec((s, 128), lambda i: (0, Z_SK)), pl.BlockSpec((s, 128), lambda i: (0, Z_SV))],
        out_specs=[pl.BlockSpec((s, GW), lambda i: (0, 0)), pl.BlockSpec((s, 128), lambda i: (0, 0))],
        out_shape=[SDS((s, GW), F32), SDS((s, 128), F32)], compiler_params=_cp(("arbitrary",)))(sinks, z, z, z)


def _swa_bwd(z, sinks, lse, y, dy, name):
    s = z.shape[0]

    def body(sink_ref, q_ref, k_ref, v_ref, l_ref, y_ref, dy_ref, dq_ref, dk_ref, dv_ref, dsink_ref):
        dk_ref[...] = jnp.zeros_like(dk_ref)
        dv_ref[...] = jnp.zeros_like(dv_ref)
        dsink_ref[...] = jnp.zeros_like(dsink_ref)

        def step(n, carry):
            qs, ks, qb, kb, vb, valid = _swa_block(q_ref, k_ref, v_ref, n)
            lse_b = l_ref[pl.ds(qs, WIN), :]
            yb = y_ref[pl.ds(qs, WIN), :]
            dyb = dy_ref[pl.ds(qs, WIN), :].astype(F32)
            dsink = jnp.zeros((1, 128), F32)
            for kvh in range(2):
                kv = slice(HD * kvh, HD * (kvh + 1))
                q = _stack2(qb, kvh) * SCALE
                dy2 = _stack2(dyb, kvh)
                sc = jnp.where(valid, _dot(q, kb[:, kv], "nt"), NEG)
                lh = jnp.concatenate([lse_b[:, 2 * kvh:2 * kvh + 1], lse_b[:, 2 * kvh + 1:2 * kvh + 2]], axis=0)
                p = jnp.exp(sc - lh)
                dd = jnp.sum(dy2 * _stack2(yb, kvh), axis=-1, keepdims=True)
                ds = p * (_dot(dy2, vb[:, kv], "nt") - dd)
                dq = _dot(ds, kb[:, kv], "nn") * SCALE
                dk_ref[pl.ds(ks, 2 * WIN), kv] += _dot(ds, q, "tn")
                dv_ref[pl.ds(ks, 2 * WIN), kv] += _dot(p, dy2, "tn")
                dsk = jnp.exp(_sink2(sink_ref, kvh) - lh) * dd
                for j in range(2):
                    h = 2 * kvh + j
                    dq_ref[pl.ds(qs, WIN), HD * h:HD * (h + 1)] = dq[WIN * j:WIN * (j + 1), :]
                    dsink = _lane_put(dsink, dsink[:, h:h + 1] - jnp.sum(dsk[WIN * j:WIN * (j + 1), :], axis=0, keepdims=True), h)
            dsink_ref[...] += dsink
            return carry

        lax.fori_loop(0, s // WIN, step, 0, unroll=2)

    full = lambda w: pl.BlockSpec((s, w), lambda i: (0, 0))
    return pl.pallas_call(
        body, name=name, grid=(1,),
        in_specs=[pl.BlockSpec(memory_space=pltpu.SMEM), pl.BlockSpec((s, GW), lambda i: (0, Z_SQ)),
                  pl.BlockSpec((s, 128), lambda i: (0, Z_SK)), pl.BlockSpec((s, 128), lambda i: (0, Z_SV)),
                  full(128), full(GW), full(GW)],
        out_specs=[full(GW), full(128), full(128), pl.BlockSpec((1, 128), lambda i: (0, 0))],
        out_shape=[SDS((s, GW), F32), SDS((s, 128), F32), SDS((s, 128), F32), SDS((1, 128), F32)],
        compiler_params=_cp(("arbitrary",)))(sinks, z, z, z, lse, y, dy)


_SUBLANES = 8


def _rotations(win, advance=False):
    n = win.shape[0]
    return [win] + [pltpu.roll(win, (n - b) if advance else b, 0) for b in range(1, _SUBLANES)]


def _delayed(rots, shift, halo, tm):
    a, b = divmod(shift, _SUBLANES)
    return rots[b][halo - _SUBLANES * a:halo - _SUBLANES * a + tm, :]


def _advanced(rots, shift, tm):
    a, b = divmod(shift, _SUBLANES)
    return rots[b][_SUBLANES * a:_SUBLANES * a + tm, :]


def _prev_halo(width, halo, tm, col):
    return pl.BlockSpec((halo, width), lambda i: (jnp.maximum(i * (tm // halo) - 1, 0), col))


def _glu_window(a_ref, g_ref, ah_ref, gh_ref):
    keep = (pl.program_id(0) > 0).astype(F32)
    a = jnp.concatenate([ah_ref[...].astype(F32) * keep, a_ref[...].astype(F32)], axis=0)
    g = jnp.concatenate([gh_ref[...].astype(F32), g_ref[...].astype(F32)], axis=0)
    return a * _sigmoid(g)


def _conv_fwd(z, cw, cb, lg, lb, pw, pb, name):
    s = z.shape[0]
    tm = min(TM, s)

    def body(a_ref, g_ref, ah_ref, gh_ref, w_ref, b_ref, lg_ref, lb_ref, pw_ref, pb_ref, y_ref, hc_ref):
        rots = _rotations(_glu_window(a_ref, g_ref, ah_ref, gh_ref))
        hc = jnp.zeros((tm, GW), F32) + b_ref[...]
        for k in range(CONV_K):
            hc = hc + w_ref[k:k + 1, :] * _delayed(rots, CONV_K - 1 - k, CONV_HALO, tm)
        hc_ref[...] = hc
        xh, _ = _ln_stats(hc)
        y_ref[...] = _dot(_silu(xh * lg_ref[...] + lb_ref[...]), pw_ref[...], "nn") + pb_ref[...]

    tile = lambda col: pl.BlockSpec((tm, GW), lambda i: (i, col))
    whole = lambda a: pl.BlockSpec(a.shape, lambda i: (0, 0))
    return pl.pallas_call(
        body, name=name, grid=(s // tm,),
        in_specs=[tile(Z_CA), tile(Z_CG), _prev_halo(GW, CONV_HALO, tm, Z_CA), _prev_halo(GW, CONV_HALO, tm, Z_CG),
                  whole(cw), whole(cb), whole(lg), whole(lb), whole(pw), whole(pb)],
        out_specs=[tile(0), tile(0)], out_shape=[SDS((s, GW), F32), SDS((s, GW), F32)],
        compiler_params=_cp(("parallel",)))(z, z, z, z, cw, cb, lg, lb, pw, pb)


def _conv_bwd_a(z, hc, dy, cw, lg, lb, pw, name):
    s = z.shape[0]
    tm = min(TM, s)

    def body(a_ref, g_ref, ah_ref, gh_ref, hc_ref, dy_ref, lg_ref, lb_ref, pw_ref,
             dhc_ref, dpw_ref, dpb_ref, dlg_ref, dlb_ref, dcw_ref, dcb_ref):
        first = pl.program_id(0) == 0
        dy = dy_ref[...].astype(F32)
        xh, rstd = _ln_stats(hc_ref[...])
        hn = xh * lg_ref[...] + lb_ref[...]
        dhn = _dot(dy, pw_ref[...], "nt") * _dsilu(hn)
        dhc = _ln_bwd(xh, rstd, dhn * lg_ref[...])
        dhc_ref[...] = dhc
        _acc(dpw_ref, _dot(_silu(hn), dy, "tn"), first)
        _acc(dpb_ref, _colsum(dy), first)
        _acc(dlg_ref, _colsum(dhn * xh), first)
        _acc(dlb_ref, _colsum(dhn), first)
        _acc(dcb_ref, _colsum(dhc), first)
        rots = _rotations(_glu_window(a_ref, g_ref, ah_ref, gh_ref))

        @pl.when(first)
        def _():
            dcw_ref[...] = jnp.zeros_like(dcw_ref)

        for k in range(CONV_K):
            dcw_ref[k:k + 1, :] += _colsum(dhc * _delayed(rots, CONV_K - 1 - k, CONV_HALO, tm))

    tile = lambda col: pl.BlockSpec((tm, GW), lambda i: (i, col))
    whole = lambda shape: pl.BlockSpec(shape, lambda i: (0, 0))
    return pl.pallas_call(
        body, name=name, grid=(s // tm,),
        in_specs=[tile(Z_CA), tile(Z_CG), _prev_halo(GW, CONV_HALO, tm, Z_CA), _prev_halo(GW, CONV_HALO, tm, Z_CG),
                  tile(0), tile(0), whole(lg.shape), whole(lb.shape), whole(pw.shape)],
        out_specs=[tile(0), whole((GW, GW)), whole((1, GW)), whole((1, GW)), whole((1, GW)), whole((32, GW)), whole((1, GW))],
        out_shape=[SDS((s, GW), F32), SDS((GW, GW), F32), SDS((1, GW), F32), SDS((1, GW), F32), SDS((1, GW), F32),
                   SDS((32, GW), F32), SDS((1, GW), F32)],
        compiler_params=_cp(("arbitrary",)))(z, z, z, z, hc, dy, lg, lb, pw)


def _conv_bwd_b(z, dhc, cw, name):
    s = z.shape[0]
    tm = min(TM, s)
    nt = s // tm

    def body(a_ref, g_ref, d_ref, dn_ref, w_ref, da_ref, dg_ref):
        keep = (pl.program_id(0) < nt - 1).astype(F32)
        rots = _rotations(jnp.concatenate([d_ref[...], dn_ref[...] * keep], axis=0), advance=True)
        dhg = jnp.zeros((tm, GW), F32)
        for k in range(CONV_K):
            dhg = dhg + w_ref[k:k + 1, :] * _advanced(rots, CONV_K - 1 - k, tm)
        sg = _sigmoid(g_ref[...].astype(F32))
        da_ref[...] = (dhg * sg).astype(BF16)
        dg_ref[...] = (dhg * a_ref[...].astype(F32) * sg * (1.0 - sg)).astype(BF16)

    tile = lambda col: pl.BlockSpec((tm, GW), lambda i: (i, col))
    nxt = pl.BlockSpec((CONV_HALO, GW), lambda i: (jnp.minimum((i + 1) * (tm // CONV_HALO), s // CONV_HALO - 1), 0))
    return pl.pallas_call(
        body, name=name, grid=(nt,),
        in_specs=[tile(Z_CA), tile(Z_CG), tile(0), nxt, pl.BlockSpec(cw.shape, lambda i: (0, 0))],
        out_specs=[tile(0), tile(0)], out_shape=[SDS((s, GW), BF16), SDS((s, GW), BF16)],
        compiler_params=_cp(("parallel",)))(z, z, dhc, dhc, cw)


def _sgu_chunk(zu, zv, lg, lb, wcat, bfull):
    u, v = _gelu(zu), _gelu(zv)
    xh, rstd = _ln_stats(v)
    vn = xh * lg + lb
    lane = lax.shift_right_logical(lax.broadcasted_iota(jnp.int32, (WIN, GW), 1), 6)
    r = jnp.concatenate([jnp.where(lane == g, vn, 0.0) for g in range(4)], axis=0)
    mix = _dot(wcat, r, "nn") + bfull
    return u, xh, rstd, r, mix, lane


def _tril4(w):
    t = lax.broadcasted_iota(jnp.int32, w.shape, 0)
    sidx = lax.broadcasted_iota(jnp.int32, w.shape, 1) & (WIN - 1)
    return jnp.where(sidx <= t, w, 0.0)


def _sgu_fwd(z, lg, lb, wcat, bfull, name):
    s = z.shape[0]
    tm = min(TM, s)

    def body(u_ref, v_ref, lg_ref, lb_ref, w_ref, b_ref, y_ref):
        w = _tril4(w_ref[...])
        for n in range(tm // WIN):
            rows = slice(WIN * n, WIN * (n + 1))
            u, _, _, _, mix, _ = _sgu_chunk(u_ref[rows, :].astype(F32), v_ref[rows, :].astype(F32), lg_ref[...], lb_ref[...], w, b_ref[...])
            y_ref[rows, :] = u * mix

    tile = lambda col: pl.BlockSpec((tm, GW), lambda i: (i, col))
    whole = lambda a: pl.BlockSpec(a.shape, lambda i: (0, 0))
    return pl.pallas_call(
        body, name=name, grid=(s // tm,), in_specs=[tile(Z_GU), tile(Z_GV), whole(lg), whole(lb), whole(wcat), whole(bfull)],
        out_specs=tile(0), out_shape=SDS((s, GW), F32), compiler_params=_cp(("parallel",)))(z, z, lg, lb, wcat, bfull)


def _sgu_bwd(z, dy, lg, lb, wcat, bfull, name):
    s = z.shape[0]
    tm = min(TM, s)

    def body(u_ref, v_ref, dy_ref, lg_ref, lb_ref, w_ref, b_ref, du_ref, dv_ref, dw_ref, db_ref, dlg_ref, dlb_ref):
        first = pl.program_id(0) == 0
        w = _tril4(w_ref[...])
        wt = w.T
        dw = jnp.zeros((WIN, 4 * WIN), F32)
        db = jnp.zeros((WIN, 128), F32)
        dlg = jnp.zeros((1, GW), F32)
        dlb = jnp.zeros((1, GW), F32)
        for n in range(tm // WIN):
            rows = slice(WIN * n, WIN * (n + 1))
            zu, zv, dout = u_ref[rows, :].astype(F32), v_ref[rows, :].astype(F32), dy_ref[rows, :].astype(F32)
            u, xh, rstd, r, mix, lane = _sgu_chunk(zu, zv, lg_ref[...], lb_ref[...], w, b_ref[...])
            dmix = dout * u
            du_ref[rows, :] = (dout * mix * _dgelu(zu)).astype(BF16)
            dw = dw + _dot(dmix, r, "nt")
            for g in range(4):
                db = _lane_put(db, db[:, g:g + 1] + jnp.sum(dmix[:, HD * g:HD * (g + 1)], axis=1, keepdims=True), g)
            dr = _dot(wt, dmix, "nn")
            dvn = jnp.zeros((WIN, GW), F32)
            for g in range(4):
                dvn = dvn + jnp.where(lane == g, dr[WIN * g:WIN * (g + 1), :], 0.0)
            dlg = dlg + _colsum(dvn * xh)
            dlb = dlb + _colsum(dvn)
            dv_ref[rows, :] = (_ln_bwd(xh, rstd, dvn * lg_ref[...]) * _dgelu(zv)).astype(BF16)
        _acc(dw_ref, _tril4(dw), first)
        _acc(db_ref, db, first)
        _acc(dlg_ref, dlg, first)
        _acc(dlb_ref, dlb, first)

    tile = lambda col: pl.BlockSpec((tm, GW), lambda i: (i, col))
    whole = lambda shape: pl.BlockSpec(shape, lambda i: (0, 0))
    return pl.pallas_call(
        body, name=name, grid=(s // tm,),
        in_specs=[tile(Z_GU), tile(Z_GV), tile(0), whole(lg.shape), whole(lb.shape), whole(wcat.shape), whole(bfull.shape)],
        out_specs=[tile(0), tile(0), whole((WIN, 4 * WIN)), whole((WIN, 128)), whole((1, GW)), whole((1, GW))],
        out_shape=[SDS((s, GW), BF16), SDS((s, GW), BF16), SDS((WIN, 4 * WIN), F32), SDS((WIN, 128), F32),
                   SDS((1, GW), F32), SDS((1, GW), F32)],
        compiler_params=_cp(("arbitrary",)))(z, z, dy, lg, lb, wcat, bfull)


def _conv3(win, w, b):
    return (w[2:3, :] * win[FFN_HALO:, :] + w[1:2, :] * pltpu.roll(win, 1, 0)[FFN_HALO:, :]
            + w[0:1, :] * pltpu.roll(win, 2, 0)[FFN_HALO:, :] + b)


def _ffn_specs(s, tm):
    main = pl.BlockSpec((2, None, tm, FF_BLK), lambda j, i: (0, j, i, 0))
    prev = pl.BlockSpec((2, None, FFN_HALO, FF_BLK), lambda j, i: (0, j, jnp.maximum(i * (tm // FFN_HALO) - 1, 0), 0))
    nxt = pl.BlockSpec((2, None, FFN_HALO, FF_BLK),
                       lambda j, i: (0, j, jnp.minimum((i + 1) * (tm // FFN_HALO), s // FFN_HALO - 1), 0))
    wsp = pl.BlockSpec((2, None, 3, FF_BLK), lambda j, i: (0, j, 0, 0))
    bsp = pl.BlockSpec((2, None, 1, FF_BLK), lambda j, i: (0, j, 0, 0))
    return main, prev, nxt, wsp, bsp


def _ffn_act(u4, w4, b4, name, dep=None):
    s = u4.shape[2]
    tm = min(TM, s)
    main, prev, _, wsp, bsp = _ffn_specs(s, tm)

    def body(u_ref, uh_ref, w_ref, b_ref, o_ref):
        keep = (pl.program_id(1) > 0).astype(F32)
        gw, vw = [jnp.concatenate([uh_ref[p].astype(F32) * keep, u_ref[p].astype(F32)], axis=0) for p in range(2)]
        o_ref[...] = (_silu(_conv3(gw, w_ref[0], b_ref[0])) * _conv3(vw, w_ref[1], b_ref[1])).astype(BF16)

    return _pcall(
        body, (u4, u4, w4, b4), dep, name=name, grid=(FF_NBLK, s // tm), in_specs=[main, prev, wsp, bsp],
        out_specs=pl.BlockSpec((None, tm, FF_BLK), lambda j, i: (j, i, 0)), out_shape=SDS((FF_NBLK, s, FF_BLK), BF16),
        compiler_params=_cp(("parallel", "parallel")))


def _ffn_bwd(u4, dact, w4, b4, name, dep=None):
    s = u4.shape[2]
    tm = min(TM, s)
    nt = s // tm
    main, prev, nxt, wsp, bsp = _ffn_specs(s, tm)
    dmain = pl.BlockSpec((None, tm, FF_BLK), lambda j, i: (j, i, 0))
    dnext = pl.BlockSpec((None, FFN_HALO, FF_BLK), lambda j, i: (j, jnp.minimum((i + 1) * (tm // FFN_HALO), s // FFN_HALO - 1), 0))
    ext = tm + FFN_HALO

    def body(u_ref, up_ref, un_ref, d_ref, dn_ref, w_ref, b_ref, du_ref, dw_ref, db_ref):
        i = pl.program_id(1)
        first = i == 0
        keep_prev = (i > 0).astype(F32)
        keep_next = (i < nt - 1).astype(F32)
        wins = [jnp.concatenate([up_ref[p].astype(F32) * keep_prev, u_ref[p].astype(F32), un_ref[p].astype(F32)], axis=0)
                for p in range(2)]
        gc = _conv3(wins[0], w_ref[0], b_ref[0])
        vc = _conv3(wins[1], w_ref[1], b_ref[1])
        d = jnp.concatenate([d_ref[...].astype(F32), dn_ref[...].astype(F32) * keep_next], axis=0)
        sg = _sigmoid(gc)
        duc = (d * vc * (sg * (1.0 + gc * (1.0 - sg))), d * (gc * sg))
        for p in range(2):
            w = w_ref[p]
            du_ref[p] = (w[2:3, :] * duc[p][:tm, :] + w[1:2, :] * pltpu.roll(duc[p], ext - 1, 0)[:tm, :]
                         + w[0:1, :] * pltpu.roll(duc[p], ext - 2, 0)[:tm, :]).astype(BF16)
            own = duc[p][:tm, :]
            taps = [_colsum(own * (wins[p] if k == 2 else pltpu.roll(wins[p], 2 - k, 0))[FFN_HALO:FFN_HALO + tm, :])
                    for k in range(3)]

            @pl.when(first)
            def _():
                db_ref[p] = _colsum(own)
                for k in range(3):
                    dw_ref[p, k:k + 1, :] = taps[k]

            @pl.when(jnp.logical_not(first))
            def _():
                db_ref[p] += _colsum(own)
                for k in range(3):
                    dw_ref[p, k:k + 1, :] += taps[k]

    return _pcall(
        body, (u4, u4, u4, dact, dact, w4, b4), dep, name=name, grid=(FF_NBLK, nt),
        in_specs=[main, prev, nxt, dmain, dnext, wsp, bsp], out_specs=[main, wsp, bsp],
        out_shape=[SDS(u4.shape, BF16), SDS((2, FF_NBLK, 3, FF_BLK), F32), SDS((2, FF_NBLK, 1, FF_BLK), F32)],
        compiler_params=_cp(("parallel", "arbitrary")))


def _sum8(parts, name):
    _, r, c = parts[0].shape
    tr = r
    for cand in (512, 256, 128, 64, 32, 16):
        if r % cand == 0 and r > cand:
            tr = cand
            break
    nb = r // tr

    def body(*refs):
        o_ref = refs[-1]
        for l, p_ref in enumerate(refs[:-1]):
            @pl.when(pl.program_id(0) == l)
            def _():
                acc = p_ref[0].astype(F32)
                for j in range(1, N_DEV):
                    acc = acc + p_ref[j].astype(F32)
                o_ref[...] = acc

    def spec(l):
        return pl.BlockSpec((N_DEV, tr, c), lambda ll, i: (0, jnp.where(ll == l, i, jnp.where(ll < l, 0, nb - 1)), 0))

    return pl.pallas_call(
        body, name=name, grid=(len(parts), nb), in_specs=[spec(l) for l in range(len(parts))],
        out_specs=pl.BlockSpec((None, tr, c), lambda ll, i: (ll, i, 0)), out_shape=SDS((len(parts), r, c), F32),
        compiler_params=_cp(("arbitrary", "arbitrary")))(*parts)


def _sum8_small(parts, name):
    n = len(parts)

    def body(*refs):
        for p_ref, o_ref in zip(refs[:n], refs[n:]):
            acc = p_ref[0]
            for j in range(1, N_DEV):
                acc = acc + p_ref[j]
            o_ref[...] = acc

    return pl.pallas_call(body, name=name, out_shape=[SDS(p.shape[1:], F32) for p in parts], compiler_params=_cp())(*parts)


def _adamw_math(w, g, m, v):
    m = ADAM_B1 * m + (1.0 - ADAM_B1) * g
    v = ADAM_B2 * v + (1.0 - ADAM_B2) * (g * g)
    m_hat = m / (1.0 - ADAM_B1 ** ADAM_STEP)
    v_hat = v / (1.0 - ADAM_B2 ** ADAM_STEP)
    return -ADAM_LR * (m_hat / (jnp.sqrt(v_hat) + ADAM_EPS) + ADAM_WD * w), m, v


def _adamw(w, g, m, v, name):
    r, c = w.shape
    tr = r
    for cand in (256, 128, 64):
        if r % cand == 0 and r > cand:
            tr = cand
            break

    def body(w_ref, g_ref, m_ref, v_ref, d_ref, mo_ref, vo_ref):
        d_ref[...], mo_ref[...], vo_ref[...] = _adamw_math(w_ref[...], g_ref[...], m_ref[...], v_ref[...])

    blk = pl.BlockSpec((tr, c), lambda i: (i, 0))
    return pl.pallas_call(body, name=name, grid=(r // tr,), in_specs=[blk] * 4, out_specs=[blk] * 3,
                          out_shape=[SDS((r, c), F32)] * 3, compiler_params=_cp(("parallel",)))(w, g, m, v)


def _adamw_small(ws, gs, ms, vs, name):
    n = len(ws)

    def body(*refs):
        ins, outs = refs[:4 * n], refs[4 * n:]
        for i in range(n):
            d, m, v = _adamw_math(ins[i][...], ins[n + i][...], ins[2 * n + i][...], ins[3 * n + i][...])
            outs[i][...], outs[n + i][...], outs[2 * n + i][...] = d, m, v

    shapes = [SDS(w.shape, F32) for w in ws]
    res = pl.pallas_call(body, name=name, out_shape=shapes * 3, compiler_params=_cp())(*ws, *gs, *ms, *vs)
    return res[:n], res[n:2 * n], res[2 * n:]


def _perm_in(w):
    pad = jnp.zeros(w.shape[:-1] + (ZW - 2308,), w.dtype)
    return jnp.concatenate([w[..., :768], w[..., 772:], w[..., 768:772], pad], axis=-1)


def _unperm_in(g):
    return jnp.concatenate([g[..., :768], g[..., 2304:2308], g[..., 768:2304]], axis=-1)


def _wcat(sgu_w):
    return sgu_w.transpose(1, 0, 2).reshape(WIN, 4 * WIN)


def _layer_fwd(l, x, h1, mod, p, wg, last, target, nxt, w_in_next):
    s = x.shape[0]
    tag = f"_l{l}"
    mrow = lambda k: (mod, 6 * l + k)
    z = _mm_rows(h1, wg["w_in"].get(h1), "nn", ZW, BF16, "mm_z" + tag)
    fcol, frow = _fgate(z, p["bf"], "fgate" + tag, dep=wg["w_out"].prefetch(z))
    y_fox, lse_fox = _fox_fwd(z, fcol, frow, "fox_fwd" + tag)
    y_conv, hc = _conv_fwd(z, wg["conv_w"], p["conv_b"], p["conv_ln_g"], p["conv_ln_b"], wg["conv_pw_w"], p["conv_pw_b"],
                           "conv_fwd" + tag)
    y_swa, lse_swa = _swa_fwd(z, p["sinks"], "swa_fwd" + tag)
    y_sgu = _sgu_fwd(z, p["sgu_ln_g"], p["sgu_ln_b"], p["wcat"], p["bfull"], "sgu_fwd" + tag)
    ys = (y_fox, y_conv, y_swa, y_sgu)
    yn = _gnorm(ys, (p["g_group"], l), "gnorm" + tag)
    tok = wg["w_up"].prefetch(yn)
    o = _mm_rows(yn, wg["w_out"].get(yn), "nn", D, BF16, "mm_o" + tag)
    x1, h2 = _post(x, o, mrow(2), (p["g_post_mix"], l), (p["g_pre_ffn"], l), mrow(4), mrow(3), "post_mix" + tag, dep=tok)
    tok = wg["w_down"].prefetch(h2)
    u = _matmul(h2, wg["w_up"].get(h2), "nn", (N_DEV, s, FF_BLK), BF16, (N_DEV, 1, 1),
                _bs((s, D), lambda j, i, k: (0, 0)), _bs((None, D, FF_BLK), lambda j, i, k: (j, 0, 0)),
                _bs((None, s, FF_BLK), lambda j, i, k: (j, 0, 0)), None, "mm_u" + tag)
    u4 = u.reshape(2, FF_NBLK, s, FF_BLK)
    act = _ffn_act(u4, wg["ffn_conv_w"], p["ffn_conv_b"], "ffn_act" + tag, dep=tok)
    tok = None if w_in_next is None else w_in_next.prefetch(act)
    f = _matmul(act, wg["w_down"].get(act), "nn", (s, D), BF16, (1, 1, FF_NBLK),
                _bs((None, s, FF_BLK), lambda i, j, k: (k, 0, 0)), _bs((FF_BLK, D), lambda i, j, k: (k, 0)),
                _bs((s, D), lambda i, j, k: (0, 0)), (s, D), "mm_f" + tag)
    if last:
        out = _post_loss(x1, f, mrow(5), (p["g_post_ffn"], l), target, "post_loss")
    else:
        out = _post(x1, f, mrow(5), (p["g_post_ffn"], l), *nxt, "post_ffn" + tag, dep=tok)
    saved = dict(x=x, h1=h1, z=z, fcol=fcol, frow=frow, lse_fox=lse_fox, hc=hc, lse_swa=lse_swa, ys=ys, yn=yn, o=o, x1=x1,
                 h2=h2, u4=u4, act=act, f=f)
    return out, saved


def _tie(a, token):
    return a if token is None else a + token[0, 0]


def _layer_bwd(l, dx2, sv, mod, p, wg, emit, dep=None):
    s = dx2.shape[0]
    tm = min(TM, s)
    tag = f"_l{l}"
    mrow = lambda k: (mod, 6 * l + k)
    g = {}
    df, g["ga2"], g["g_post_ffn"] = _post_bwd(dx2, sv["f"], mrow(5), (p["g_post_ffn"], l), "post_ffn_bwd" + tag, dep=dep)
    dact = _matmul(df, wg["w_down"].get(None), "nt", (FF_NBLK, s, FF_BLK), BF16, (FF_NBLK, 1, 1),
                   _bs((s, D), lambda j, i, k: (0, 0)), _bs((FF_BLK, D), lambda j, i, k: (j, 0)),
                   _bs((None, s, FF_BLK), lambda j, i, k: (j, 0, 0)), None, "mm_dact" + tag)
    tok = emit("w_down", _matmul(sv["act"], df, "tn", (FF_NBLK * FF_BLK, D), BF16, (FF_NBLK, 1, 1),
                                 _bs((None, s, FF_BLK), lambda j, i, k: (j, 0, 0)), _bs((s, D), lambda j, i, k: (0, 0)),
                                 _bs((FF_BLK, D), lambda j, i, k: (j, 0)), None, "mm_dwdown" + tag))
    du, g["ffn_conv_w"], g["ffn_conv_b"] = _ffn_bwd(sv["u4"], dact, wg["ffn_conv_w"], p["ffn_conv_b"], "ffn_bwd" + tag, dep=tok)
    du = du.reshape(N_DEV, s, FF_BLK)
    dh2 = _matmul(du, wg["w_up"].get(None), "nt", (s, D), BF16, (1, 1, N_DEV),
                  _bs((None, s, FF_BLK), lambda i, j, k: (k, 0, 0)), _bs((None, D, FF_BLK), lambda i, j, k: (k, 0, 0)),
                  _bs((s, D), lambda i, j, k: (0, 0)), (s, D), "mm_dh2" + tag)
    tok = emit("w_up", _matmul(du, sv["h2"], "tn", (N_DEV, FF_BLK, D), BF16, (N_DEV, 1, 1),
                               _bs((None, s, FF_BLK), lambda j, i, k: (j, 0, 0)), _bs((s, D), lambda j, i, k: (0, 0)),
                               _bs((None, FF_BLK, D), lambda j, i, k: (j, 0, 0)), None, "mm_dwup" + tag))
    dx1, g["sh2"], g["sc2"], g["g_pre_ffn"] = _pre_bwd(dh2, sv["x1"], dx2, (p["g_pre_ffn"], l), mrow(4), "pre_ffn_bwd" + tag,
                                                       dep=tok)
    do, g["ga1"], g["g_post_mix"] = _post_bwd(dx1, sv["o"], mrow(2), (p["g_post_mix"], l), "post_mix_bwd" + tag)
    dyn = _mm_rows(do, wg["w_out"].get(None), "nt", D, BF16, "mm_dyn" + tag)
    tok = emit("w_out", _mm_wgrad(sv["yn"], do, BF16, "mm_dwout" + tag))
    dy_fox, dy_conv, dy_swa, dy_sgu, g["g_group"] = _gnorm_bwd(dyn, sv["ys"], (p["g_group"], l), "gnorm_bwd" + tag, dep=tok)
    z = sv["z"]
    dq_f, dk_f, dv_f, dfrow, dfcol = _fox_bwd(z, sv["fcol"], sv["frow"], sv["lse_fox"], sv["ys"][0], dy_fox, "fox_bwd" + tag)
    dgate, g["bf"] = _fgate_bwd(z, p["bf"], dfrow, dfcol, "fgate_bwd" + tag)
    dhc, g["conv_pw_w"], g["conv_pw_b"], g["conv_ln_g"], g["conv_ln_b"], g["conv_w"], g["conv_b"] = _conv_bwd_a(
        z, sv["hc"], dy_conv, wg["conv_w"], p["conv_ln_g"], p["conv_ln_b"], wg["conv_pw_w"], "conv_bwd_a" + tag)
    da_c, dg_c = _conv_bwd_b(z, dhc, wg["conv_w"], "conv_bwd_b" + tag)
    dq_s, dk_s, dv_s, g["sinks"] = _swa_bwd(z, p["sinks"], sv["lse_swa"], sv["ys"][2], dy_swa, "swa_bwd" + tag)
    du_g, dv_g, g["wcat"], g["sgu_bcol"], g["sgu_ln_g"], g["sgu_ln_b"] = _sgu_bwd(
        z, dy_sgu, p["sgu_ln_g"], p["sgu_ln_b"], p["wcat"], p["bfull"], "sgu_bwd" + tag)
    dz = jnp.concatenate([dq_f.astype(BF16), dk_f.astype(BF16), dv_f.astype(BF16), da_c, dg_c, dq_s.astype(BF16), dk_s.astype(BF16),
                          dv_s.astype(BF16), du_g, dv_g, dgate], axis=1)
    tok = emit("w_in", _mm_wgrad(sv["h1"], dz, BF16, "mm_dwin" + tag))
    dh1 = _mm_rows(dz, wg["w_in"].get(None), "nt", D, BF16, "mm_dh1" + tag)
    dx, g["sh1"], g["sc1"], g["g_pre_mix"] = _pre_bwd(dh1, sv["x"], dx1, (p["g_pre_mix"], l), mrow(1), "pre_mix_bwd" + tag,
                                                      dep=tok)
    return dx, g


def _layer_params(l, small, conv_w_full, conv_pw_full, ffn_conv_w_full):
    bf = jnp.pad(small["b_fgate"][l][None, :], ((0, 0), (0, 124)))
    p = dict(
        bf=bf, conv_b=small["conv_b"][l][None], conv_ln_g=small["conv_ln_g"][l][None], conv_ln_b=small["conv_ln_b"][l][None],
        conv_pw_b=small["conv_pw_b"][l][None], sinks=small["swa_sinks"][l], sgu_ln_g=small["sgu_ln_g"][l][None],
        sgu_ln_b=small["sgu_ln_b"][l][None], wcat=_wcat(small["sgu_w"][l]),
        bfull=jnp.repeat(small["sgu_b"][l].T, HD, axis=1),
        ffn_conv_b=small["ffn_conv_b"][l].reshape(2, FF_NBLK, 1, FF_BLK),
        g_group=small["g_group"].reshape(N_LAYER, 1, D), g_post_mix=small["g_post_mix"].reshape(N_LAYER, 1, D),
        g_pre_ffn=small["g_pre_ffn"].reshape(N_LAYER, 1, D), g_post_ffn=small["g_post_ffn"].reshape(N_LAYER, 1, D),
        g_pre_mix=small["g_pre_mix"].reshape(N_LAYER, 1, D))
    wsmall = dict(conv_w=conv_w_full[l], conv_pw_w=conv_pw_full[l].astype(BF16),
                  ffn_conv_w=ffn_conv_w_full[l].reshape(3, 2, FF_NBLK, FF_BLK).transpose(1, 2, 0, 3))
    return p, wsmall


def _local_step(x, target, mod, small, wbig, conv_w_full, conv_pw_full, ffn_conv_w_full, emit, on_loss=None):
    ps, wgs = [], []
    for l in range(N_LAYER):
        p, wsmall = _layer_params(l, small, conv_w_full, conv_pw_full, ffn_conv_w_full)
        ps.append(p)
        wgs.append({**wbig[l], **wsmall})
    h = _rms_mod(x, (ps[0]["g_pre_mix"], 0), (mod, 1), (mod, 0), "rms_mod_l0")
    saved = []
    for l in range(N_LAYER):
        last = l == N_LAYER - 1
        nxt = None if last else ((ps[l]["g_pre_mix"], l + 1), (mod, 6 * (l + 1) + 1), (mod, 6 * (l + 1)))
        out, sv = _layer_fwd(l, x, h, mod, ps[l], wgs[l], last, target, nxt, None if last else wgs[l + 1]["w_in"])
        saved.append(sv)
        if not last:
            x, h = out
    dx, loss = out
    dep = None if on_loss is None else on_loss(loss)
    grads = [None] * N_LAYER
    for l in reversed(range(N_LAYER)):
        dx, grads[l] = _layer_bwd(l, dx, saved[l], mod, ps[l], wgs[l], functools.partial(emit, l), dep)
        dep = None
    return loss, dx, grads


_SMALL = ("b_ada", "g_pre_mix", "g_post_mix", "g_pre_ffn", "g_post_ffn", "b_fgate", "conv_b", "conv_ln_g", "conv_ln_b",
          "conv_pw_b", "swa_sinks", "sgu_ln_g", "sgu_ln_b", "sgu_w", "sgu_b", "g_group", "ffn_conv_b")
_WEIGHTS = ("w_ada", "b_ada", "g_pre_mix", "g_post_mix", "g_pre_ffn", "g_post_ffn", "w_in", "b_fgate", "conv_w", "conv_b",
            "conv_ln_g", "conv_ln_b", "conv_pw_w", "conv_pw_b", "swa_sinks", "sgu_ln_g", "sgu_ln_b", "sgu_w", "sgu_b",
            "g_group", "w_out", "ffn_w_up", "ffn_conv_w", "ffn_conv_b", "ffn_w_down")


def _pad_rows(a, mult):
    r = (-a.shape[0]) % mult
    return a if r == 0 else jnp.concatenate([a, jnp.zeros((r,) + a.shape[1:], a.dtype)], axis=0)


def _view2d(a):
    if a.ndim == 2:
        return a
    return a.reshape(-1, a.shape[-1])


def kernel(x, c, w_ada, b_ada, g_pre_mix, g_post_mix, g_pre_ffn, g_post_ffn, w_in, b_fgate, conv_w, conv_b, conv_ln_g, conv_ln_b, conv_pw_w, conv_pw_b, swa_sinks, sgu_ln_g, sgu_ln_b, sgu_w, sgu_b, g_group, w_out, ffn_w_up, ffn_conv_w, ffn_conv_b, ffn_w_down, loss_target, m_w_ada, m_b_ada, m_g_pre_mix, m_g_post_mix, m_g_pre_ffn, m_g_post_ffn, m_w_in, m_b_fgate, m_conv_w, m_conv_b, m_conv_ln_g, m_conv_ln_b, m_conv_pw_w, m_conv_pw_b, m_swa_sinks, m_sgu_ln_g, m_sgu_ln_b, m_sgu_w, m_sgu_b, m_g_group, m_w_out, m_ffn_w_up, m_ffn_conv_w, m_ffn_conv_b, m_ffn_w_down, v_w_ada, v_b_ada, v_g_pre_mix, v_g_post_mix, v_g_pre_ffn, v_g_post_ffn, v_w_in, v_b_fgate, v_conv_w, v_conv_b, v_conv_ln_g, v_conv_ln_b, v_conv_pw_w, v_conv_pw_b, v_swa_sinks, v_sgu_ln_g, v_sgu_ln_b, v_sgu_w, v_sgu_b, v_g_group, v_w_out, v_ffn_w_up, v_ffn_conv_w, v_ffn_conv_b, v_ffn_w_down):
    env = dict(locals())
    w = {n: env[n] for n in _WEIGHTS}
    mom = {n: env["m_" + n] for n in _WEIGHTS}
    var = {n: env["v_" + n] for n in _WEIGHTS}
    me = 4 * lax.axis_index("x") + 2 * lax.axis_index("y") + lax.axis_index("c")
    x2, target = x[0], loss_target[0]

    (c_all,) = _exchange([c], ["bcast"], "gather_c")
    c_all = c_all.reshape(N_DEV, D)
    (m_all,) = _exchange([_ada_fwd(c_all, w_ada)], ["bcast"], "gather_mod")
    m_mine = lax.dynamic_index_in_dim(m_all, me, axis=2, keepdims=False)
    mod, mod_token = _ada_finish(m_mine.transpose(1, 0, 2).reshape(N_LAYER, 6 * D), b_ada)
    mod = mod.reshape(6 * N_LAYER, 1, D)

    shards = [_tie(conv_w, mod_token), conv_pw_w, ffn_conv_w]
    for l in range(N_LAYER):
        shards += [_perm_in(w_in[l]).astype(BF16), w_out[l].astype(BF16), ffn_w_up[l].astype(BF16), ffn_w_down[l].astype(BF16)]
    gather = _g2_start(shards, "gather_weights_start")
    mod = _tie(mod, gather.token)
    _g2_relay(gather, [0, 1, 2, 3], mod, "gather_relay_first")
    g_cw, g_pw, g_fcw = _g2_wait(gather, [0, 1, 2], mod, "gather_small_wait")
    conv_w_full = g_cw.transpose(1, 2, 0, 3).reshape(N_LAYER, CONV_K, GW)
    conv_pw_full = g_pw.transpose(1, 0, 2, 3).reshape(N_LAYER, GW, GW)
    ffn_conv_w_full = g_fcw.transpose(1, 2, 0, 3).reshape(N_LAYER, 3, N_DEV * FF_BLK)

    def lazy(i, shape, key):
        pre = None if i == 3 else (lambda after: _g2_relay(gather, [i], after, "relay_" + key))
        return _Lazy(lambda after: _g2_wait(gather, [i], after, "wait_" + key)[0].reshape(shape), pre)

    wbig = [dict(w_in=lazy(3 + 4 * l, (D, ZW), f"w_in_l{l}"), w_out=lazy(4 + 4 * l, (D, D), f"w_out_l{l}"),
                 w_up=lazy(5 + 4 * l, (N_DEV, D, FF_BLK), f"w_up_l{l}"),
                 w_down=lazy(6 + 4 * l, (FF_NBLK * FF_BLK, D), f"w_down_l{l}")) for l in range(N_LAYER)]

    grad_flights = []

    def emit(l, key, arr):
        fl = _xchg_start([arr.reshape(N_DEV, -1, arr.shape[-1])], ["a2a"], f"grad_start_{key}_l{l}")
        grad_flights.append(((l, key), fl))
        return fl.token

    small = {n: w[n] for n in _SMALL}
    total = []

    def on_loss(loss8):
        total.append(lax.psum(loss8[0, 0], ("x", "y", "c")))
        return total[0].reshape(1, 1)

    _, dx, grads = _local_step(x2, target, mod, small, wbig, conv_w_full, conv_pw_full, ffn_conv_w_full, emit, on_loss)
    loss = total[0]
    grad_x = dx[None]


    st = lambda key: jnp.stack([grads[l][key] for l in range(N_LAYER)])
    d_conv_w = st("conv_w")[:, :CONV_K, :].reshape(N_LAYER, CONV_K, N_DEV, GW // N_DEV).transpose(2, 0, 1, 3)
    d_pw_w = st("conv_pw_w").reshape(N_LAYER, N_DEV, GW // N_DEV, GW).transpose(1, 0, 2, 3)
    d_fcw = st("ffn_conv_w").reshape(N_LAYER, N_DEV, 3, FF_BLK).transpose(1, 0, 2, 3)
    rows_d = _pad_rows(jnp.concatenate(
        [grads[l][k] for l in range(N_LAYER) for k in ("sh1", "sc1", "ga1", "sh2", "sc2", "ga2")]
        + [grads[l][k] for k in ("g_pre_mix", "g_post_mix", "g_pre_ffn", "g_post_ffn", "g_group") for l in range(N_LAYER)],
        axis=0), 8)
    rows_gw = _pad_rows(jnp.concatenate(
        [grads[l][k] for k in ("conv_b", "conv_ln_g", "conv_ln_b", "conv_pw_b", "sgu_ln_g", "sgu_ln_b") for l in range(N_LAYER)],
        axis=0), 8)
    rows_128 = jnp.concatenate([_pad_rows(jnp.concatenate([grads[l]["bf"] for l in range(N_LAYER)]
                                                          + [grads[l]["sinks"] for l in range(N_LAYER)], axis=0), 8)]
                               + [grads[l]["sgu_bcol"] for l in range(N_LAYER)], axis=0)
    rows_w = jnp.concatenate([grads[l]["wcat"] for l in range(N_LAYER)], axis=0)
    rows_fb = st("ffn_conv_b").reshape(N_LAYER * N_DEV, FF_BLK)
    small_flight = _xchg_start([d_conv_w, d_pw_w, d_fcw, rows_d, rows_gw, rows_128, rows_w, rows_fb],
                               ["a2a"] * 3 + ["bcast"] * 5, "small_grads_start")

    to_mem = {"w_in": lambda a: a.transpose(2, 0, 1), "ffn_w_up": lambda a: a.transpose(0, 2, 1)}
    from_mem = {"w_in": lambda a: a.transpose(1, 2, 0), "ffn_w_up": lambda a: a.transpose(0, 2, 1)}
    flights = dict(grad_flights)
    gr, delta, new_m, new_v = {}, {}, {}, {}

    def adamw_big(n):
        view, back = to_mem.get(n, lambda a: a), from_mem.get(n, lambda a: a)
        shape = view(w[n]).shape
        d, m2, v2 = _adamw(_view2d(view(w[n])), _view2d(gr[n]), _view2d(view(mom[n])), _view2d(view(var[n])), "adamw_" + n)
        delta[n], new_m[n], new_v[n] = back(d.reshape(shape)), back(m2.reshape(shape)), back(v2.reshape(shape))
        gr[n] = back(gr[n].reshape(shape))

    after = small_flight.token
    for n, key in (("ffn_w_down", "w_down"), ("ffn_w_up", "w_up"), ("w_out", "w_out"), ("w_in", "w_in")):
        parts = [_xchg_wait(flights[(l, key)], [0], after, f"grad_wait_{key}_l{l}")[0] for l in reversed(range(N_LAYER))]
        g = _sum8(parts[::-1], "sum_" + key)
        gr[n] = to_mem["w_in"](_unperm_in(g)) if n == "w_in" else g
        adamw_big(n)
        after = new_v[n]

    small_parts = _xchg_wait(small_flight, list(range(8)), after, "small_grads_wait")
    s_conv_w, s_pw_w, s_fcw, s_d, s_gw, s_128, s_w, s_fb = _sum8_small(
        [p.reshape(N_DEV, -1, p.shape[-1]) for p in small_parts], "sum_small_grads")
    gr["conv_w"] = s_conv_w.reshape(N_LAYER, CONV_K, GW // N_DEV)
    gr["conv_pw_w"] = s_pw_w.reshape(N_LAYER, GW // N_DEV, GW)
    gr["ffn_conv_w"] = s_fcw.reshape(N_LAYER, 3, FF_BLK)
    gr["b_ada"] = s_d[:6 * N_LAYER].reshape(N_LAYER, 6 * D)
    for i, k in enumerate(("g_pre_mix", "g_post_mix", "g_pre_ffn", "g_post_ffn", "g_group")):
        gr[k] = s_d[6 * N_LAYER + 2 * i:6 * N_LAYER + 2 * i + 2]
    for i, k in enumerate(("conv_b", "conv_ln_g", "conv_ln_b", "conv_pw_b", "sgu_ln_g", "sgu_ln_b")):
        gr[k] = s_gw[2 * i:2 * i + 2]
    gr["b_fgate"] = s_128[0:2, :4]
    gr["swa_sinks"] = s_128[2:4, :4]
    gr["sgu_b"] = s_128[8:].reshape(N_LAYER, WIN, 128)[:, :, :4].transpose(0, 2, 1)
    gr["sgu_w"] = s_w.reshape(N_LAYER, WIN, 4, WIN).transpose(0, 2, 1, 3)
    gr["ffn_conv_b"] = s_fb.reshape(N_LAYER, N_DEV * FF_BLK)
    dmod_all = small_parts[3][:, :6 * N_LAYER, :].reshape(N_DEV, N_LAYER, 6 * D)
    ncol = 6 * D // N_DEV
    dmod_cols = lax.dynamic_slice_in_dim(dmod_all, me * ncol, ncol, axis=2).transpose(1, 0, 2)
    gr["w_ada"] = _ada_bwd(c_all, dmod_cols)

    adamw_big("w_ada")
    smalls = [n for n in _WEIGHTS if n not in ("w_ada", "w_in", "w_out", "ffn_w_up", "ffn_w_down")]
    ds, ms, vs = _adamw_small([_view2d(w[n]) for n in smalls], [_view2d(gr[n]) for n in smalls],
                              [_view2d(mom[n]) for n in smalls], [_view2d(var[n]) for n in smalls], "adamw_small")
    for i, n in enumerate(smalls):
        delta[n], new_m[n], new_v[n] = ds[i].reshape(w[n].shape), ms[i].reshape(w[n].shape), vs[i].reshape(w[n].shape)

    return (loss, grad_x, *[gr[n].reshape(w[n].shape) for n in _WEIGHTS], *[delta[n] for n in _WEIGHTS],
            *[new_m[n] for n in _WEIGHTS], *[new_v[n] for n in _WEIGHTS])
```

```python
import functools

import jax
import jax.numpy as jnp
from jax import lax
from jax.experimental import pallas as pl
from jax.experimental.pallas import tpu as pltpu

F32, BF16 = jnp.float32, jnp.bfloat16
SDS = jax.ShapeDtypeStruct
MESH = pl.DeviceIdType.MESH

N_DEV = 8
D = 1024
GW = 256
HD = 64
N_LAYER = 2
ZW = 2432
FF_BLK = 704
FF_NBLK = 4
CONV_K = 31
CONV_HALO = 32
FFN_HALO = 16
EPS = 1e-6
NEG = -1e30
SCALE = HD ** -0.5
VMEM_LIMIT_V7X = 56 * 1024 * 1024
TM = 512
WGRAD_ROWS = 256
TQ = 256
WIN = 128

ADAM_LR, ADAM_B1, ADAM_B2, ADAM_EPS, ADAM_WD, ADAM_STEP = 0.001, 0.9, 0.999, 1e-08, 0.01, 10

Z_FQ, Z_FK, Z_FV, Z_CA, Z_CG, Z_SQ = 0, 1, 2, 3, 4, 5
Z_SK, Z_SV = 12, 13
Z_GU, Z_GV = 7, 8
Z_FG = 18


def _cp(sem=None):
    return pltpu.CompilerParams(dimension_semantics=sem, vmem_limit_bytes=VMEM_LIMIT_V7X)


def _vec(arr3, idx, ngrid):
    w = arr3.shape[-1]
    if ngrid == 1:
        return pl.BlockSpec((None, 1, w), lambda i: (idx, 0, 0))
    return pl.BlockSpec((None, 1, w), lambda i, j: (idx, 0, 0))


def _sigmoid(x):
    return jax.nn.sigmoid(x)


def _silu(x):
    return x * _sigmoid(x)


def _dsilu(x):
    s = _sigmoid(x)
    return s * (1.0 + x * (1.0 - s))


_G0, _G1 = 0.7978845608028654, 0.044715


def _gelu(x):
    return 0.5 * x * (1.0 + jnp.tanh(_G0 * (x + _G1 * x * x * x)))


def _dgelu(x):
    t = jnp.tanh(_G0 * (x + _G1 * x * x * x))
    return 0.5 * (1.0 + t) + 0.5 * x * (1.0 - t * t) * (_G0 * (1.0 + 3.0 * _G1 * x * x))


def _rstd(x):
    return lax.rsqrt(jnp.mean(x * x, axis=-1, keepdims=True) + EPS)


def _rms_bwd(xh, r, t):
    return r * (t - xh * jnp.mean(t * xh, axis=-1, keepdims=True))


def _ln_stats(x):
    mu = jnp.mean(x, axis=-1, keepdims=True)
    xc = x - mu
    rstd = lax.rsqrt(jnp.mean(xc * xc, axis=-1, keepdims=True) + EPS)
    return xc * rstd, rstd


def _ln_bwd(xh, rstd, dxh):
    return rstd * (dxh - jnp.mean(dxh, axis=-1, keepdims=True) - xh * jnp.mean(dxh * xh, axis=-1, keepdims=True))


def _colsum(x):
    return jnp.sum(x, axis=0, keepdims=True)


def _dot(a, b, kind):
    dn = {"nn": (((1,), (0,)), ((), ())), "nt": (((1,), (1,)), ((), ())), "tn": (((0,), (0,)), ((), ()))}[kind]
    return lax.dot_general(a.astype(BF16), b.astype(BF16), dn, preferred_element_type=F32)


def _exchange(arrs, modes, name):
    n = len(arrs)
    outs = [SDS((N_DEV,) + a.shape, a.dtype) if m == "bcast" else SDS(a.shape, a.dtype) for a, m in zip(arrs, modes)]

    def body(*refs):
        ins, dst = refs[:n], refs[n:2 * n]
        send, recv, loc = refs[2 * n:]
        x, y, c = lax.axis_index("x"), lax.axis_index("y"), lax.axis_index("c")
        me = 4 * x + 2 * y + c

        def src(a, j):
            return ins[a] if modes[a] == "bcast" else ins[a].at[j]

        local = [pltpu.make_async_copy(src(a, me), dst[a].at[me], loc.at[a]) for a in range(n)]
        for cp in local:
            cp.start()
        sent, landed = [], []
        for k in (2, 4, 6, 3, 5, 7, 1):
            px = 1 - x if k & 4 else x
            py = 1 - y if k & 2 else y
            pc = 1 - c if k & 1 else c
            peer = 4 * px + 2 * py + pc
            for a in range(n):
                cp = pltpu.make_async_remote_copy(src_ref=src(a, peer), dst_ref=dst[a].at[me], send_sem=send.at[a, k - 1],
                                                  recv_sem=recv.at[a, k - 1], device_id=(px, py, pc), device_id_type=MESH)
                cp.start()
                sent.append(cp)
                landed.append(pltpu.make_async_remote_copy(src_ref=src(a, peer), dst_ref=dst[a].at[peer],
                                                           send_sem=send.at[a, k - 1], recv_sem=recv.at[a, k - 1],
                                                           device_id=(px, py, pc), device_id_type=MESH))
        for cp in landed:
            cp.wait_recv()
        for cp in sent:
            cp.wait_send()
        for cp in local:
            cp.wait()

    hbm = pl.BlockSpec(memory_space=pltpu.HBM)
    return pl.pallas_call(
        body, name=name, out_shape=outs, in_specs=[hbm] * n, out_specs=[hbm] * n,
        scratch_shapes=[pltpu.SemaphoreType.DMA((n, N_DEV - 1)), pltpu.SemaphoreType.DMA((n, N_DEV - 1)),
                        pltpu.SemaphoreType.DMA((n,))],
        compiler_params=pltpu.CompilerParams(has_side_effects=True),
    )(*arrs)


_PEER_ORDER = (2, 4, 6, 3, 5, 7, 1)
_HBM = pl.BlockSpec(memory_space=pltpu.HBM)
_SEM = pl.BlockSpec(memory_space=pltpu.SEMAPHORE)
_EFFECT = pltpu.SideEffectType.DATAFLOW_SIDE_EFFECTING


def _peer(k):
    x, y, c = lax.axis_index("x"), lax.axis_index("y"), lax.axis_index("c")
    px = 1 - x if k & 4 else x
    py = 1 - y if k & 2 else y
    pc = 1 - c if k & 1 else c
    return (px, py, pc), 4 * px + 2 * py + pc


def _my_id():
    return 4 * lax.axis_index("x") + 2 * lax.axis_index("y") + lax.axis_index("c")


def _split_copies(src_ref, land_ref, send, recv, loc, mode):
    me = _my_id()
    pick = (lambda j: src_ref) if mode == "bcast" else (lambda j: src_ref.at[j])
    local = pltpu.make_async_copy(pick(me), land_ref.at[me], loc)
    remote = []
    for k in _PEER_ORDER:
        dev, peer = _peer(k)
        out = pltpu.make_async_remote_copy(src_ref=pick(peer), dst_ref=land_ref.at[me], send_sem=send.at[k - 1],
                                           recv_sem=recv.at[k - 1], device_id=dev, device_id_type=MESH)
        arrive = pltpu.make_async_remote_copy(src_ref=pick(peer), dst_ref=land_ref.at[peer], send_sem=send.at[k - 1],
                                              recv_sem=recv.at[k - 1], device_id=dev, device_id_type=MESH)
        remote.append((out, arrive))
    return local, remote


class _Flight:
    def __init__(self, srcs, lands, sends, recvs, locs, modes, token):
        self.srcs, self.lands, self.sends, self.recvs, self.locs, self.modes, self.token = (
            list(srcs), list(lands), list(sends), list(recvs), list(locs), list(modes), token)


def _xchg_start(arrs, modes, name):
    n = len(arrs)
    lands = [lax.empty((N_DEV,) + a.shape if m == "bcast" else a.shape, a.dtype) for a, m in zip(arrs, modes)]

    def body(*refs):
        srcs, lnds = refs[:n], refs[n:2 * n]
        outs = refs[2 * n:]
        sends, recvs, locs, token = outs[:n], outs[n:2 * n], outs[2 * n:3 * n], outs[5 * n]
        for a in range(n):
            local, remote = _split_copies(srcs[a], lnds[a], sends[a], recvs[a], locs[a], modes[a])
            local.start()
            for out, _ in remote:
                out.start()
        token[...] = jnp.zeros_like(token)

    sem7 = pltpu.SemaphoreType.DMA((N_DEV - 1,))
    res = pl.pallas_call(
        body, name=name,
        out_shape=[sem7] * (2 * n) + [pltpu.SemaphoreType.DMA(())] * n + [pltpu.HBM(a.shape, a.dtype) for a in arrs]
        + [pltpu.HBM(b.shape, b.dtype) for b in lands] + [SDS((8, 128), F32)],
        in_specs=[_HBM] * (2 * n), out_specs=[_SEM] * (3 * n) + [_HBM] * (2 * n) + [pl.BlockSpec(memory_space=pltpu.VMEM)],
        input_output_aliases={i: 3 * n + i for i in range(2 * n)},
        compiler_params=pltpu.CompilerParams(has_side_effects=_EFFECT),
    )(*[pltpu.with_memory_space_constraint(a, pltpu.HBM) for a in arrs],
      *[pltpu.with_memory_space_constraint(b, pltpu.HBM) for b in lands])
    return _Flight(res[3 * n:4 * n], res[4 * n:5 * n], res[:n], res[n:2 * n], res[2 * n:3 * n], modes, res[5 * n])


def _xchg_wait(flight, idx, after, name):
    n = len(idx)
    modes = [flight.modes[i] for i in idx]

    def body(*refs):
        srcs, lnds = refs[:n], refs[n:2 * n]
        sends, recvs, locs = refs[2 * n:3 * n], refs[3 * n:4 * n], refs[4 * n:5 * n]
        for a in range(n):
            local, remote = _split_copies(srcs[a], lnds[a], sends[a], recvs[a], locs[a], modes[a])
            local.wait()
            for _, arrive in remote:
                arrive.wait_send()
                arrive.wait_recv()

    ops = ([flight.srcs[i] for i in idx] + [flight.lands[i] for i in idx] + [flight.sends[i] for i in idx]
           + [flight.recvs[i] for i in idx] + [flight.locs[i] for i in idx])
    res = pl.pallas_call(
        body, name=name, out_shape=[pltpu.HBM(o.shape, o.dtype) for o in ops[:2 * n]],
        in_specs=[_HBM] * (2 * n) + [_SEM] * (3 * n) + [pl.BlockSpec(memory_space=pl.ANY)], out_specs=[_HBM] * (2 * n),
        input_output_aliases={i: i for i in range(2 * n)},
        compiler_params=pltpu.CompilerParams(has_side_effects=_EFFECT),
    )(*ops, after)
    return res[n:]


class _Lazy:
    def __init__(self, fn, pre=None):
        self.fn, self.pre, self.val, self.started = fn, pre, None, False

    def prefetch(self, after):
        token = self.pre(after) if self.pre is not None and not self.started else None
        self.started = True
        return token

    def get(self, after):
        self.prefetch(after)
        if self.val is None:
            self.val = self.fn(after)
        return self.val


_CHIP_PEERS = (2, 4, 6)


def _g2_copies_a(src_ref, land_ref, send, recv, loc):
    me = _my_id()
    local = pltpu.make_async_copy(src_ref, land_ref.at[me], loc)
    remote = []
    for j, k in enumerate(_CHIP_PEERS + (1,)):
        dev, peer = _peer(k)
        out = pltpu.make_async_remote_copy(src_ref=src_ref, dst_ref=land_ref.at[me], send_sem=send.at[j], recv_sem=recv.at[j],
                                           device_id=dev, device_id_type=MESH)
        arrive = pltpu.make_async_remote_copy(src_ref=src_ref, dst_ref=land_ref.at[peer], send_sem=send.at[j],
                                              recv_sem=recv.at[j], device_id=dev, device_id_type=MESH)
        remote.append((out, arrive))
    return local, remote


def _g2_copies_b(land_ref, send, recv):
    sib, _ = _peer(1)
    pairs = []
    for j, k in enumerate(_CHIP_PEERS):
        _, same_core = _peer(k)
        _, other_core = _peer(k | 1)
        out = pltpu.make_async_remote_copy(src_ref=land_ref.at[same_core], dst_ref=land_ref.at[same_core], send_sem=send.at[j],
                                           recv_sem=recv.at[j], device_id=sib, device_id_type=MESH)
        arrive = pltpu.make_async_remote_copy(src_ref=land_ref.at[same_core], dst_ref=land_ref.at[other_core],
                                              send_sem=send.at[j], recv_sem=recv.at[j], device_id=sib, device_id_type=MESH)
        pairs.append((out, arrive))
    return pairs


class _Gather2:
    def __init__(self, srcs, lands, sends, recvs, locs, token):
        self.srcs, self.lands, self.sends, self.recvs, self.locs, self.token = (
            list(srcs), list(lands), list(sends), list(recvs), list(locs), token)
        self.sends_b, self.recvs_b = [None] * len(self.srcs), [None] * len(self.srcs)


def _g2_start(arrs, name):
    n = len(arrs)
    lands = [lax.empty((N_DEV,) + a.shape, a.dtype) for a in arrs]

    def body(*refs):
        srcs, lnds = refs[:n], refs[n:2 * n]
        outs = refs[2 * n:]
        sends, recvs, locs, token = outs[:n], outs[n:2 * n], outs[2 * n:3 * n], outs[5 * n]
        for a in range(n):
            local, remote = _g2_copies_a(srcs[a], lnds[a], sends[a], recvs[a], locs[a])
            local.start()
            for out, _ in remote:
                out.start()
        token[...] = jnp.zeros_like(token)

    sem4 = pltpu.SemaphoreType.DMA((4,))
    res = pl.pallas_call(
        body, name=name,
        out_shape=[sem4] * (2 * n) + [pltpu.SemaphoreType.DMA(())] * n + [pltpu.HBM(a.shape, a.dtype) for a in arrs]
        + [pltpu.HBM(b.shape, b.dtype) for b in lands] + [SDS((8, 128), F32)],
        in_specs=[_HBM] * (2 * n), out_specs=[_SEM] * (3 * n) + [_HBM] * (2 * n) + [pl.BlockSpec(memory_space=pltpu.VMEM)],
        input_output_aliases={i: 3 * n + i for i in range(2 * n)},
        compiler_params=pltpu.CompilerParams(has_side_effects=_EFFECT),
    )(*[pltpu.with_memory_space_constraint(a, pltpu.HBM) for a in arrs],
      *[pltpu.with_memory_space_constraint(b, pltpu.HBM) for b in lands])
    return _Gather2(res[3 * n:4 * n], res[4 * n:5 * n], res[:n], res[n:2 * n], res[2 * n:3 * n], res[5 * n])


def _g2_relay(g, idx, after, name):
    n = len(idx)

    def body(*refs):
        srcs, lnds = refs[:n], refs[n:2 * n]
        sends, recvs, locs = refs[2 * n:3 * n], refs[3 * n:4 * n], refs[4 * n:5 * n]
        outs = refs[5 * n + 1:]
        sends_b, recvs_b = outs[2 * n:3 * n], outs[3 * n:4 * n]
        for a in range(n):
            local, remote = _g2_copies_a(srcs[a], lnds[a], sends[a], recvs[a], locs[a])
            local.wait()
            for _, arrive in remote:
                arrive.wait_send()
                arrive.wait_recv()
        for a in range(n):
            for out, _ in _g2_copies_b(lnds[a], sends_b[a], recvs_b[a]):
                out.start()
        outs[4 * n][...] = jnp.zeros_like(outs[4 * n])

    ops = ([g.srcs[i] for i in idx] + [g.lands[i] for i in idx] + [g.sends[i] for i in idx] + [g.recvs[i] for i in idx]
           + [g.locs[i] for i in idx])
    sem3 = pltpu.SemaphoreType.DMA((3,))
    res = pl.pallas_call(
        body, name=name, out_shape=[pltpu.HBM(o.shape, o.dtype) for o in ops[:2 * n]] + [sem3] * (2 * n) + [SDS((8, 128), F32)],
        in_specs=[_HBM] * (2 * n) + [_SEM] * (3 * n) + [pl.BlockSpec(memory_space=pl.ANY)],
        out_specs=[_HBM] * (2 * n) + [_SEM] * (2 * n) + [pl.BlockSpec(memory_space=pltpu.VMEM)],
        input_output_aliases={i: i for i in range(2 * n)},
        compiler_params=pltpu.CompilerParams(has_side_effects=_EFFECT),
    )(*ops, after)
    for a, i in enumerate(idx):
        g.srcs[i], g.lands[i] = res[a], res[n + a]
        g.sends_b[i], g.recvs_b[i] = res[2 * n + a], res[3 * n + a]
    return res[4 * n]


def _g2_wait(g, idx, after, name):
    n = len(idx)

    def body(*refs):
        lnds, sends_b, recvs_b = refs[:n], refs[n:2 * n], refs[2 * n:3 * n]
        for a in range(n):
            for _, arrive in _g2_copies_b(lnds[a], sends_b[a], recvs_b[a]):
                arrive.wait_send()
                arrive.wait_recv()

    ops = [g.lands[i] for i in idx] + [g.sends_b[i] for i in idx] + [g.recvs_b[i] for i in idx]
    res = pl.pallas_call(
        body, name=name, out_shape=[pltpu.HBM(o.shape, o.dtype) for o in ops[:n]],
        in_specs=[_HBM] * n + [_SEM] * (2 * n) + [pl.BlockSpec(memory_space=pl.ANY)], out_specs=[_HBM] * n,
        input_output_aliases={i: i for i in range(n)},
        compiler_params=pltpu.CompilerParams(has_side_effects=_EFFECT),
    )(*ops, after)
    return list(res)


def _matmul(a, b, kind, out_shape, out_dtype, grid, a_spec, b_spec, o_spec, acc_shape, name):
    nk = grid[2]

    def body(a_ref, b_ref, o_ref, *scratch):
        prod = _dot(a_ref[...], b_ref[...], kind)
        if nk == 1:
            o_ref[...] = prod.astype(out_dtype)
        else:
            acc = scratch[0]
            k = pl.program_id(2)

            @pl.when(k == 0)
            def _():
                acc[...] = prod

            @pl.when(k > 0)
            def _():
                acc[...] += prod

            @pl.when(k == nk - 1)
            def _():
                o_ref[...] = acc[...].astype(out_dtype)

    return pl.pallas_call(
        body, name=name, grid=grid, in_specs=[a_spec, b_spec], out_specs=o_spec, out_shape=SDS(out_shape, out_dtype),
        scratch_shapes=[] if nk == 1 else [pltpu.VMEM(acc_shape, F32)],
        compiler_params=_cp(("parallel", "parallel", "arbitrary")))(a, b)


def _bs(shape, fn):
    return pl.BlockSpec(shape, fn)


def _mm_rows(a, w, kind, n_out, out_dtype, name):
    s, k = a.shape
    tm = min(TM, s)
    return _matmul(a, w, kind, (s, n_out), out_dtype, (s // tm, 1, 1),
                   _bs((tm, k), lambda i, j, kk: (i, 0)), _bs(w.shape, lambda i, j, kk: (0, 0)),
                   _bs((tm, n_out), lambda i, j, kk: (i, 0)), None, name)


def _mm_wgrad(a, dy, out_dtype, name):
    s, k = a.shape
    n = dy.shape[1]
    tko = min(WGRAD_ROWS, k)
    return _matmul(a, dy, "tn", (k, n), out_dtype, (k // tko, 1, 1),
                   _bs((s, tko), lambda i, j, kk: (0, i)), _bs((s, n), lambda i, j, kk: (0, 0)),
                   _bs((tko, n), lambda i, j, kk: (i, 0)), None, name)


def _ada_fwd(c_all, w_ada):
    ncol = w_ada.shape[2]

    def body(c_ref, w_ref, o_ref):
        ca = _silu(c_ref[...])
        ca = jnp.concatenate([ca, jnp.zeros_like(ca)], axis=0)
        o_ref[...] = _dot(ca, w_ref[...], "nn")[:N_DEV, :]

    return pl.pallas_call(
        body, name="ada_fwd", grid=(N_LAYER,),
        in_specs=[pl.BlockSpec((N_DEV, D), lambda l: (0, 0)), pl.BlockSpec((None, D, ncol), lambda l: (l, 0, 0))],
        out_specs=pl.BlockSpec((None, N_DEV, ncol), lambda l: (l, 0, 0)),
        out_shape=SDS((N_LAYER, N_DEV, ncol), F32), compiler_params=_cp(("parallel",)))(c_all, w_ada)


def _ada_finish(m_mine, b_ada):
    def body(m_ref, b_ref, o_ref, t_ref):
        o_ref[...] = m_ref[...] + b_ref[...]
        t_ref[...] = jnp.zeros_like(t_ref)

    return pl.pallas_call(body, name="ada_finish", out_shape=[SDS(b_ada.shape, F32), SDS((8, 128), F32)])(m_mine, b_ada)


def _ada_bwd(c_all, dmod_cols):
    ncol = dmod_cols.shape[2]

    def body(c_ref, d_ref, o_ref):
        ca = _silu(c_ref[...])
        ca = jnp.concatenate([ca, jnp.zeros_like(ca)], axis=0)
        dm = d_ref[...]
        dm = jnp.concatenate([dm, jnp.zeros_like(dm)], axis=0)
        o_ref[...] = _dot(ca, dm, "tn")

    return pl.pallas_call(
        body, name="ada_bwd", grid=(N_LAYER,),
        in_specs=[pl.BlockSpec((N_DEV, D), lambda l: (0, 0)), pl.BlockSpec((None, N_DEV, ncol), lambda l: (l, 0, 0))],
        out_specs=pl.BlockSpec((None, D, ncol), lambda l: (l, 0, 0)),
        out_shape=SDS((N_LAYER, D, ncol), F32), compiler_params=_cp(("parallel",)))(c_all, dmod_cols)


def _rows(s):
    tm = min(TM, s)
    return tm, pl.BlockSpec((tm, D), lambda i: (i, 0))


def _pcall(body, operands, dep, **kw):
    if dep is None:
        return pl.pallas_call(body, **kw)(*operands)
    n = len(operands)

    def body_dep(*refs):
        body(*refs[:n], *refs[n + 1:])

    kw["in_specs"] = list(kw["in_specs"]) + [pl.BlockSpec(memory_space=pl.ANY)]
    return pl.pallas_call(body_dep, **kw)(*operands, dep)


def _rms_mod(x, g, sc, sh, name):
    s = x.shape[0]
    tm, row = _rows(s)

    def body(x_ref, g_ref, sc_ref, sh_ref, h_ref):
        xf = x_ref[...]
        h_ref[...] = (xf * _rstd(xf) * (g_ref[...] * (1.0 + sc_ref[...])) + sh_ref[...]).astype(BF16)

    return pl.pallas_call(
        body, name=name, grid=(s // tm,), in_specs=[row, _vec(*g, 1), _vec(*sc, 1), _vec(*sh, 1)], out_specs=row,
        out_shape=SDS((s, D), BF16), compiler_params=_cp(("parallel",)))(x, g[0], sc[0], sh[0])


def _post(xres, o, ga, gpost, gn, scn, shn, name, dep=None):
    s = xres.shape[0]
    tm, row = _rows(s)

    def body(x_ref, o_ref, ga_ref, gp_ref, gn_ref, sc_ref, sh_ref, xn_ref, h_ref):
        of = o_ref[...].astype(F32)
        xn = x_ref[...] + ga_ref[...] * (of * _rstd(of) * gp_ref[...])
        xn_ref[...] = xn
        h_ref[...] = (xn * _rstd(xn) * (gn_ref[...] * (1.0 + sc_ref[...])) + sh_ref[...]).astype(BF16)

    return _pcall(
        body, (xres, o, ga[0], gpost[0], gn[0], scn[0], shn[0]), dep, name=name, grid=(s // tm,),
        in_specs=[row, row, _vec(*ga, 1), _vec(*gpost, 1), _vec(*gn, 1), _vec(*scn, 1), _vec(*shn, 1)],
        out_specs=[row, row], out_shape=[SDS((s, D), F32), SDS((s, D), BF16)], compiler_params=_cp(("parallel",)))


def _post_loss(xres, o, ga, gpost, target, name, dep=None):
    s = xres.shape[0]
    tm, row = _rows(s)

    def body(x_ref, o_ref, ga_ref, gp_ref, t_ref, dy_ref, loss_ref):
        of = o_ref[...].astype(F32)
        err = x_ref[...] + ga_ref[...] * (of * _rstd(of) * gp_ref[...]) - t_ref[...]
        dy_ref[...] = err * (1.0 / D)

        @pl.when(pl.program_id(0) == 0)
        def _():
            loss_ref[...] = jnp.zeros_like(loss_ref)

        loss_ref[...] += jnp.sum(jnp.mean(err * err, axis=-1, keepdims=True), axis=0, keepdims=True) * 0.5

    return _pcall(
        body, (xres, o, ga[0], gpost[0], target), dep, name=name, grid=(s // tm,),
        in_specs=[row, row, _vec(*ga, 1), _vec(*gpost, 1), row],
        out_specs=[row, pl.BlockSpec((8, 128), lambda i: (0, 0))], out_shape=[SDS((s, D), F32), SDS((8, 128), F32)],
        compiler_params=_cp(("arbitrary",)))


def _acc(ref, val, first):
    @pl.when(first)
    def _():
        ref[...] = val

    @pl.when(jnp.logical_not(first))
    def _():
        ref[...] += val


def _post_bwd(dxn, o, ga, gpost, name, dep=None):
    s = dxn.shape[0]
    tm, row = _rows(s)
    vec = pl.BlockSpec((1, D), lambda i: (0, 0))

    def body(d_ref, o_ref, ga_ref, gp_ref, do_ref, dga_ref, dgp_ref):
        of, dx = o_ref[...].astype(F32), d_ref[...]
        r = _rstd(of)
        oh = of * r
        do_ref[...] = _rms_bwd(oh, r, dx * (ga_ref[...] * gp_ref[...])).astype(BF16)
        cs = _colsum(dx * oh)
        first = pl.program_id(0) == 0
        _acc(dga_ref, cs * gp_ref[...], first)
        _acc(dgp_ref, cs * ga_ref[...], first)

    return _pcall(
        body, (dxn, o, ga[0], gpost[0]), dep, name=name, grid=(s // tm,),
        in_specs=[row, row, _vec(*ga, 1), _vec(*gpost, 1)], out_specs=[row, vec, vec],
        out_shape=[SDS((s, D), BF16), SDS((1, D), F32), SDS((1, D), F32)], compiler_params=_cp(("arbitrary",)))


def _pre_bwd(dh, x, dres, g, sc, name, dep=None):
    s = x.shape[0]
    tm, row = _rows(s)
    vec = pl.BlockSpec((1, D), lambda i: (0, 0))

    def body(dh_ref, x_ref, dr_ref, g_ref, sc_ref, dx_ref, dsh_ref, dsc_ref, dg_ref):
        xf, d = x_ref[...], dh_ref[...].astype(F32)
        r = _rstd(xf)
        xh = xf * r
        dx_ref[...] = dr_ref[...] + _rms_bwd(xh, r, d * (g_ref[...] * (1.0 + sc_ref[...])))
        cs = _colsum(d * xh)
        first = pl.program_id(0) == 0
        _acc(dsh_ref, _colsum(d), first)
        _acc(dsc_ref, cs * g_ref[...], first)
        _acc(dg_ref, cs * (1.0 + sc_ref[...]), first)

    return _pcall(
        body, (dh, x, dres, g[0], sc[0]), dep, name=name, grid=(s // tm,),
        in_specs=[row, row, row, _vec(*g, 1), _vec(*sc, 1)], out_specs=[row, vec, vec, vec],
        out_shape=[SDS((s, D), F32), SDS((1, D), F32), SDS((1, D), F32), SDS((1, D), F32)],
        compiler_params=_cp(("arbitrary",)))


def _gnorm(ys, gg, name):
    s = ys[0].shape[0]
    tm = min(TM, s)
    yb = pl.BlockSpec((tm, GW), lambda i: (i, 0))

    def body(y0, y1, y2, y3, g_ref, o_ref):
        for i, yr in enumerate((y0, y1, y2, y3)):
            y = yr[...]
            o_ref[:, GW * i:GW * (i + 1)] = (y * _rstd(y) * g_ref[:, GW * i:GW * (i + 1)]).astype(BF16)

    return pl.pallas_call(
        body, name=name, grid=(s // tm,), in_specs=[yb] * 4 + [_vec(*gg, 1)], out_specs=pl.BlockSpec((tm, D), lambda i: (i, 0)),
        out_shape=SDS((s, D), BF16), compiler_params=_cp(("parallel",)))(*ys, gg[0])


def _gnorm_bwd(dyn, ys, gg, name, dep=None):
    s = ys[0].shape[0]
    tm = min(TM, s)
    yb = pl.BlockSpec((tm, GW), lambda i: (i, 0))

    def body(d_ref, y0, y1, y2, y3, g_ref, o0, o1, o2, o3, dg_ref):
        first = pl.program_id(0) == 0
        for i, (yr, orf) in enumerate(zip((y0, y1, y2, y3), (o0, o1, o2, o3))):
            y = yr[...]
            d = d_ref[:, GW * i:GW * (i + 1)].astype(F32)
            r = _rstd(y)
            yh = y * r
            orf[...] = _rms_bwd(yh, r, d * g_ref[:, GW * i:GW * (i + 1)]).astype(BF16)
            cs = _colsum(d * yh)

            @pl.when(first)
            def _():
                dg_ref[:, GW * i:GW * (i + 1)] = cs

            @pl.when(jnp.logical_not(first))
            def _():
                dg_ref[:, GW * i:GW * (i + 1)] += cs

    return _pcall(
        body, (dyn, *ys, gg[0]), dep, name=name, grid=(s // tm,),
        in_specs=[pl.BlockSpec((tm, D), lambda i: (i, 0))] + [yb] * 4 + [_vec(*gg, 1)],
        out_specs=[yb] * 4 + [pl.BlockSpec((1, D), lambda i: (0, 0))],
        out_shape=[SDS((s, GW), BF16)] * 4 + [SDS((1, D), F32)], compiler_params=_cp(("arbitrary",)))


def _lane_put(acc, col, h):
    lane = lax.broadcasted_iota(jnp.int32, acc.shape, 1)
    return jnp.where(lane == h, col, acc)


def _fgate(z, bf, name, dep=None):
    s = z.shape[0]

    def body(z_ref, b_ref, fc_ref, fr_ref):
        xg = z_ref[...].astype(F32) + b_ref[...]
        lf = jnp.minimum(xg, 0.0) - jnp.log(1.0 + jnp.exp(-jnp.abs(xg)))
        lane = lax.broadcasted_iota(jnp.int32, lf.shape, 1)
        row = lax.broadcasted_iota(jnp.int32, lf.shape, 0)
        f = jnp.where(lane < 4, lf, 0.0)
        sh = 1
        while sh < s:
            f = f + jnp.where(row >= sh, pltpu.roll(f, sh, 0), 0.0)
            sh *= 2
        fc_ref[...] = f
        fr_ref[...] = f.T[:8, :]

    return _pcall(
        body, (z, bf), dep, name=name, grid=(1,),
        in_specs=[pl.BlockSpec((s, 128), lambda i: (0, Z_FG)), pl.BlockSpec((1, 128), lambda i: (0, 0))],
        out_specs=[pl.BlockSpec((s, 128), lambda i: (0, 0)), pl.BlockSpec((8, s), lambda i: (0, 0))],
        out_shape=[SDS((s, 128), F32), SDS((8, s), F32)], compiler_params=_cp(("arbitrary",)))


def _fgate_bwd(z, bf, dfrow, dfcol, name):
    s = z.shape[0]

    def body(z_ref, b_ref, d_ref, dc_ref, dz_ref, db_ref):
        d = jnp.concatenate([d_ref[...], jnp.zeros((120, s), F32)], axis=0).T + dc_ref[...]
        row = lax.broadcasted_iota(jnp.int32, d.shape, 0)
        lane = lax.broadcasted_iota(jnp.int32, d.shape, 1)
        sh = 1
        while sh < s:
            d = d + jnp.where(row < s - sh, pltpu.roll(d, s - sh, 0), 0.0)
            sh *= 2
        xg = z_ref[...].astype(F32) + b_ref[...]
        dz = jnp.where(lane < 4, d * _sigmoid(-xg), 0.0)
        dz_ref[...] = dz.astype(BF16)
        db_ref[...] = _colsum(dz)

    return pl.pallas_call(
        body, name=name, grid=(1,),
        in_specs=[pl.BlockSpec((s, 128), lambda i: (0, Z_FG)), pl.BlockSpec((1, 128), lambda i: (0, 0)),
                  pl.BlockSpec((8, s), lambda i: (0, 0)), pl.BlockSpec((s, 128), lambda i: (0, 0))],
        out_specs=[pl.BlockSpec((s, 128), lambda i: (0, 0)), pl.BlockSpec((1, 128), lambda i: (0, 0))],
        out_shape=[SDS((s, 128), BF16), SDS((1, 128), F32)], compiler_params=_cp(("arbitrary",)))(z, bf, dfrow, dfcol)


def _fox_scores(q_ref, k_ref, fc_ref, fr_ref, h, i, nk, tq):
    kw = nk * tq
    q = q_ref[:, HD * h:HD * (h + 1)] * SCALE
    sc = _dot(q, k_ref[0:kw, HD * h:HD * (h + 1)], "nt") + fc_ref[:, h:h + 1] - fr_ref[h:h + 1, 0:kw]
    qpos = i * tq + lax.broadcasted_iota(jnp.int32, (tq, kw), 0)
    kpos = lax.broadcasted_iota(jnp.int32, (tq, kw), 1)
    return q, jnp.where(kpos <= qpos, sc, NEG)


def _fox_fwd(z, fcol, frow, name):
    s = z.shape[0]
    tq = min(TQ, s)
    nc = s // tq

    def body(q_ref, k_ref, v_ref, fc_ref, fr_ref, y_ref, l_ref):
        for n in range(nc):
            @pl.when(pl.program_id(0) == n)
            def _():
                kw = (n + 1) * tq
                lse = jnp.zeros((tq, 128), F32)
                for h in range(4):
                    _, sc = _fox_scores(q_ref, k_ref, fc_ref, fr_ref, h, n, n + 1, tq)
                    m = jnp.max(sc, axis=-1, keepdims=True)
                    p = jnp.exp(sc - m)
                    l = jnp.sum(p, axis=-1, keepdims=True)
                    y_ref[:, HD * h:HD * (h + 1)] = _dot(p, v_ref[0:kw, HD * h:HD * (h + 1)], "nn") / l
                    lse = _lane_put(lse, m + jnp.log(l), h)
                l_ref[...] = lse

    return pl.pallas_call(
        body, name=name, grid=(nc,),
        in_specs=[pl.BlockSpec((tq, GW), lambda i: (i, Z_FQ)), pl.BlockSpec((s, GW), lambda i: (0, Z_FK)),
                  pl.BlockSpec((s, GW), lambda i: (0, Z_FV)), pl.BlockSpec((tq, 128), lambda i: (i, 0)),
                  pl.BlockSpec((8, s), lambda i: (0, 0))],
        out_specs=[pl.BlockSpec((tq, GW), lambda i: (i, 0)), pl.BlockSpec((tq, 128), lambda i: (i, 0))],
        out_shape=[SDS((s, GW), F32), SDS((s, 128), F32)], compiler_params=_cp(("parallel",)))(z, z, z, fcol, frow)


def _fox_bwd(z, fcol, frow, lse, y, dy, name):
    s = z.shape[0]
    tq = min(TQ, s)
    nc = s // tq
    half = max(nc // 2, 1)

    def body(q_ref, k_ref, v_ref, fc_ref, fr_ref, l_ref, y_ref, dy_ref, dq_ref, dk_ref, dv_ref, df_ref, dfq_ref):
        @pl.when(pl.program_id(0) == 0)
        def _():
            dk_ref[...] = jnp.zeros_like(dk_ref)
            dv_ref[...] = jnp.zeros_like(dv_ref)
            df_ref[...] = jnp.zeros_like(df_ref)

        i = pl.program_id(0)
        for cond, nk in ((i < half, half), (i >= half, nc)):
            @pl.when(cond)
            def _():
                kw = nk * tq
                dfq = jnp.zeros((tq, 128), F32)
                dyf = dy_ref[...].astype(F32)
                for h in range(4):
                    hs = slice(HD * h, HD * (h + 1))
                    q, sc = _fox_scores(q_ref, k_ref, fc_ref, fr_ref, h, i, nk, tq)
                    p = jnp.exp(sc - l_ref[:, h:h + 1])
                    dyh = dyf[:, hs]
                    dd = jnp.sum(dyh * y_ref[:, hs], axis=-1, keepdims=True)
                    ds = p * (_dot(dyh, v_ref[0:kw, hs], "nt") - dd)
                    dq_ref[:, hs] = _dot(ds, k_ref[0:kw, hs], "nn") * SCALE
                    dk_ref[0:kw, hs] += _dot(ds, q, "tn")
                    dv_ref[0:kw, hs] += _dot(p, dyh, "tn")
                    df_ref[h:h + 1, 0:kw] -= _colsum(ds)
                    dfq = _lane_put(dfq, jnp.sum(ds, axis=-1, keepdims=True), h)
                dfq_ref[...] = dfq

    tile = lambda w: pl.BlockSpec((tq, w), lambda i: (i, 0))
    full = pl.BlockSpec((s, GW), lambda i: (0, 0))
    rows8 = pl.BlockSpec((8, s), lambda i: (0, 0))
    return pl.pallas_call(
        body, name=name, grid=(nc,),
        in_specs=[pl.BlockSpec((tq, GW), lambda i: (i, Z_FQ)), pl.BlockSpec((s, GW), lambda i: (0, Z_FK)),
                  pl.BlockSpec((s, GW), lambda i: (0, Z_FV)), tile(128), rows8, tile(128), tile(GW), tile(GW)],
        out_specs=[tile(GW), full, full, rows8, tile(128)],
        out_shape=[SDS((s, GW), F32), SDS((s, GW), F32), SDS((s, GW), F32), SDS((8, s), F32), SDS((s, 128), F32)],
        compiler_params=_cp(("arbitrary",)))(z, z, z, fcol, frow, lse, y, dy)


def _swa_block(q_ref, k_ref, v_ref, n):
    qs = pl.multiple_of(n * WIN, WIN)
    ks = pl.multiple_of(jnp.maximum(n - 1, 0) * WIN, WIN)
    qb = q_ref[pl.ds(qs, WIN), :]
    kb = k_ref[pl.ds(ks, 2 * WIN), :]
    vb = v_ref[pl.ds(ks, 2 * WIN), :]
    rows = lax.broadcasted_iota(jnp.int32, (2 * WIN, 2 * WIN), 0) & (WIN - 1)
    dist = (qs + rows) - (ks + lax.broadcasted_iota(jnp.int32, (2 * WIN, 2 * WIN), 1))
    return qs, ks, qb, kb, vb, (dist >= 0) & (dist < WIN)


def _stack2(x, kvh):
    return jnp.concatenate([x[:, HD * (2 * kvh):HD * (2 * kvh + 1)], x[:, HD * (2 * kvh + 1):HD * (2 * kvh + 2)]], axis=0)


def _sink2(sink_ref, kvh):
    top = lax.broadcasted_iota(jnp.int32, (2 * WIN, 1), 0) < WIN
    return jnp.where(top, sink_ref[2 * kvh], sink_ref[2 * kvh + 1])


def _swa_fwd(z, sinks, name):
    s = z.shape[0]

    def body(sink_ref, q_ref, k_ref, v_ref, y_ref, l_ref):
        def step(n, carry):
            qs, ks, qb, kb, vb, valid = _swa_block(q_ref, k_ref, v_ref, n)
            lse = jnp.zeros((WIN, 128), F32)
            for kvh in range(2):
                kv = slice(HD * kvh, HD * (kvh + 1))
                sc = jnp.where(valid, _dot(_stack2(qb, kvh) * SCALE, kb[:, kv], "nt"), NEG)
                sink = _sink2(sink_ref, kvh)
                m = jnp.maximum(jnp.max(sc, axis=-1, keepdims=True), sink)
                p = jnp.exp(sc - m)
                den = jnp.sum(p, axis=-1, keepdims=True) + jnp.exp(sink - m)
                o = _dot(p, vb[:, kv], "nn") / den
                lrow = m + jnp.log(den)
                for j in range(2):
                    h = 2 * kvh + j
                    y_ref[pl.ds(qs, WIN), HD * h:HD * (h + 1)] = o[WIN * j:WIN * (j + 1), :]
                    lse = _lane_put(lse, lrow[WIN * j:WIN * (j + 1), :], h)
            l_ref[pl.ds(qs, WIN), :] = lse
            return carry

        lax.fori_loop(0, s // WIN, step, 0, unroll=2)

    return pl.pallas_call(
        body, name=name, grid=(1,),
        in_specs=[pl.BlockSpec(memory_space=pltpu.SMEM), pl.BlockSpec((s, GW), lambda i: (0, Z_SQ)),
                  pl.BlockSpec((s, 128), lambda i: (0, Z_SK)), pl.BlockSpec((s, 128), lambda i: (0, Z_SV))],
        out_specs=[pl.BlockSpec((s, GW), lambda i: (0, 0)), pl.BlockSpec((s, 128), lambda i: (0, 0))],
        out_shape=[SDS((s, GW), F32), SDS((s, 128), F32)], compiler_params=_cp(("arbitrary",)))(sinks, z, z, z)


def _swa_bwd(z, sinks, lse, y, dy, name):
    s = z.shape[0]

    def body(sink_ref, q_ref, k_ref, v_ref, l_ref, y_ref, dy_ref, dq_ref, dk_ref, dv_ref, dsink_ref):
        dk_ref[...] = jnp.zeros_like(dk_ref)
        dv_ref[...] = jnp.zeros_like(dv_ref)
        dsink_ref[...] = jnp.zeros_like(dsink_ref)

        def step(n, carry):
            qs, ks, qb, kb, vb, valid = _swa_block(q_ref, k_ref, v_ref, n)
            lse_b = l_ref[pl.ds(qs, WIN), :]
            yb = y_ref[pl.ds(qs, WIN), :]
            dyb = dy_ref[pl.ds(qs, WIN), :].astype(F32)
            dsink = jnp.zeros((1, 128), F32)
            for kvh in range(2):
                kv = slice(HD * kvh, HD * (kvh + 1))
                q = _stack2(qb, kvh) * SCALE
                dy2 = _stack2(dyb, kvh)
                sc = jnp.where(valid, _dot(q, kb[:, kv], "nt"), NEG)
                lh = jnp.concatenate([lse_b[:, 2 * kvh:2 * kvh + 1], lse_b[:, 2 * kvh + 1:2 * kvh + 2]], axis=0)
                p = jnp.exp(sc - lh)
                dd = jnp.sum(dy2 * _stack2(yb, kvh), axis=-1, keepdims=True)
                ds = p * (_dot(dy2, vb[:, kv], "nt") - dd)
                dq = _dot(ds, kb[:, kv], "nn") * SCALE
                dk_ref[pl.ds(ks, 2 * WIN), kv] += _dot(ds, q, "tn")
                dv_ref[pl.ds(ks, 2 * WIN), kv] += _dot(p, dy2, "tn")
                dsk = jnp.exp(_sink2(sink_ref, kvh) - lh) * dd
                for j in range(2):
                    h = 2 * kvh + j
                    dq_ref[pl.ds(qs, WIN), HD * h:HD * (h + 1)] = dq[WIN * j:WIN * (j + 1), :]
                    dsink = _lane_put(dsink, dsink[:, h:h + 1] - jnp.sum(dsk[WIN * j:WIN * (j + 1), :], axis=0, keepdims=True), h)
            dsink_ref[...] += dsink
            return carry

        lax.fori_loop(0, s // WIN, step, 0, unroll=2)

    full = lambda w: pl.BlockSpec((s, w), lambda i: (0, 0))
    return pl.pallas_call(
        body, name=name, grid=(1,),
        in_specs=[pl.BlockSpec(memory_space=pltpu.SMEM), pl.BlockSpec((s, GW), lambda i: (0, Z_SQ)),
                  pl.BlockSpec((s, 128), lambda i: (0, Z_SK)), pl.BlockSpec((s, 128), lambda i: (0, Z_SV)),
                  full(128), full(GW), full(GW)],
        out_specs=[full(GW), full(128), full(128), pl.BlockSpec((1, 128), lambda i: (0, 0))],
        out_shape=[SDS((s, GW), F32), SDS((s, 128), F32), SDS((s, 128), F32), SDS((1, 128), F32)],
        compiler_params=_cp(("arbitrary",)))(sinks, z, z, z, lse, y, dy)


_SUBLANES = 8


def _rotations(win, advance=False):
    n = win.shape[0]
    return [win] + [pltpu.roll(win, (n - b) if advance else b, 0) for b in range(1, _SUBLANES)]


def _delayed(rots, shift, halo, tm):
    a, b = divmod(shift, _SUBLANES)
    return rots[b][halo - _SUBLANES * a:halo - _SUBLANES * a + tm, :]


def _advanced(rots, shift, tm):
    a, b = divmod(shift, _SUBLANES)
    return rots[b][_SUBLANES * a:_SUBLANES * a + tm, :]


def _prev_halo(width, halo, tm, col):
    return pl.BlockSpec((halo, width), lambda i: (jnp.maximum(i * (tm // halo) - 1, 0), col))


def _glu_window(a_ref, g_ref, ah_ref, gh_ref):
    keep = (pl.program_id(0) > 0).astype(F32)
    a = jnp.concatenate([ah_ref[...].astype(F32) * keep, a_ref[...].astype(F32)], axis=0)
    g = jnp.concatenate([gh_ref[...].astype(F32), g_ref[...].astype(F32)], axis=0)
    return a * _sigmoid(g)


def _conv_fwd(z, cw, cb, lg, lb, pw, pb, name):
    s = z.shape[0]
    tm = min(TM, s)

    def body(a_ref, g_ref, ah_ref, gh_ref, w_ref, b_ref, lg_ref, lb_ref, pw_ref, pb_ref, y_ref, hc_ref):
        rots = _rotations(_glu_window(a_ref, g_ref, ah_ref, gh_ref))
        hc = jnp.zeros((tm, GW), F32) + b_ref[...]
        for k in range(CONV_K):
            hc = hc + w_ref[k:k + 1, :] * _delayed(rots, CONV_K - 1 - k, CONV_HALO, tm)
        hc_ref[...] = hc
        xh, _ = _ln_stats(hc)
        y_ref[...] = _dot(_silu(xh * lg_ref[...] + lb_ref[...]), pw_ref[...], "nn") + pb_ref[...]

    tile = lambda col: pl.BlockSpec((tm, GW), lambda i: (i, col))
    whole = lambda a: pl.BlockSpec(a.shape, lambda i: (0, 0))
    return pl.pallas_call(
        body, name=name, grid=(s // tm,),
        in_specs=[tile(Z_CA), tile(Z_CG), _prev_halo(GW, CONV_HALO, tm, Z_CA), _prev_halo(GW, CONV_HALO, tm, Z_CG),
                  whole(cw), whole(cb), whole(lg), whole(lb), whole(pw), whole(pb)],
        out_specs=[tile(0), tile(0)], out_shape=[SDS((s, GW), F32), SDS((s, GW), F32)],
        compiler_params=_cp(("parallel",)))(z, z, z, z, cw, cb, lg, lb, pw, pb)


def _conv_bwd_a(z, hc, dy, cw, lg, lb, pw, name):
    s = z.shape[0]
    tm = min(TM, s)

    def body(a_ref, g_ref, ah_ref, gh_ref, hc_ref, dy_ref, lg_ref, lb_ref, pw_ref,
             dhc_ref, dpw_ref, dpb_ref, dlg_ref, dlb_ref, dcw_ref, dcb_ref):
        first = pl.program_id(0) == 0
        dy = dy_ref[...].astype(F32)
        xh, rstd = _ln_stats(hc_ref[...])
        hn = xh * lg_ref[...] + lb_ref[...]
        dhn = _dot(dy, pw_ref[...], "nt") * _dsilu(hn)
        dhc = _ln_bwd(xh, rstd, dhn * lg_ref[...])
        dhc_ref[...] = dhc
        _acc(dpw_ref, _dot(_silu(hn), dy, "tn"), first)
        _acc(dpb_ref, _colsum(dy), first)
        _acc(dlg_ref, _colsum(dhn * xh), first)
        _acc(dlb_ref, _colsum(dhn), first)
        _acc(dcb_ref, _colsum(dhc), first)
        rots = _rotations(_glu_window(a_ref, g_ref, ah_ref, gh_ref))

        @pl.when(first)
        def _():
            dcw_ref[...] = jnp.zeros_like(dcw_ref)

        for k in range(CONV_K):
            dcw_ref[k:k + 1, :] += _colsum(dhc * _delayed(rots, CONV_K - 1 - k, CONV_HALO, tm))

    tile = lambda col: pl.BlockSpec((tm, GW), lambda i: (i, col))
    whole = lambda shape: pl.BlockSpec(shape, lambda i: (0, 0))
    return pl.pallas_call(
        body, name=name, grid=(s // tm,),
        in_specs=[tile(Z_CA), tile(Z_CG), _prev_halo(GW, CONV_HALO, tm, Z_CA), _prev_halo(GW, CONV_HALO, tm, Z_CG),
                  tile(0), tile(0), whole(lg.shape), whole(lb.shape), whole(pw.shape)],
        out_specs=[tile(0), whole((GW, GW)), whole((1, GW)), whole((1, GW)), whole((1, GW)), whole((32, GW)), whole((1, GW))],
        out_shape=[SDS((s, GW), F32), SDS((GW, GW), F32), SDS((1, GW), F32), SDS((1, GW), F32), SDS((1, GW), F32),
                   SDS((32, GW), F32), SDS((1, GW), F32)],
        compiler_params=_cp(("arbitrary",)))(z, z, z, z, hc, dy, lg, lb, pw)


def _conv_bwd_b(z, dhc, cw, name):
    s = z.shape[0]
    tm = min(TM, s)
    nt = s // tm

    def body(a_ref, g_ref, d_ref, dn_ref, w_ref, da_ref, dg_ref):
        keep = (pl.program_id(0) < nt - 1).astype(F32)
        rots = _rotations(jnp.concatenate([d_ref[...], dn_ref[...] * keep], axis=0), advance=True)
        dhg = jnp.zeros((tm, GW), F32)
        for k in range(CONV_K):
            dhg = dhg + w_ref[k:k + 1, :] * _advanced(rots, CONV_K - 1 - k, tm)
        sg = _sigmoid(g_ref[...].astype(F32))
        da_ref[...] = (dhg * sg).astype(BF16)
        dg_ref[...] = (dhg * a_ref[...].astype(F32) * sg * (1.0 - sg)).astype(BF16)

    tile = lambda col: pl.BlockSpec((tm, GW), lambda i: (i, col))
    nxt = pl.BlockSpec((CONV_HALO, GW), lambda i: (jnp.minimum((i + 1) * (tm // CONV_HALO), s // CONV_HALO - 1), 0))
    return pl.pallas_call(
        body, name=name, grid=(nt,),
        in_specs=[tile(Z_CA), tile(Z_CG), tile(0), nxt, pl.BlockSpec(cw.shape, lambda i: (0, 0))],
        out_specs=[tile(0), tile(0)], out_shape=[SDS((s, GW), BF16), SDS((s, GW), BF16)],
        compiler_params=_cp(("parallel",)))(z, z, dhc, dhc, cw)


def _sgu_chunk(zu, zv, lg, lb, wcat, bfull):
    u, v = _gelu(zu), _gelu(zv)
    xh, rstd = _ln_stats(v)
    vn = xh * lg + lb
    lane = lax.shift_right_logical(lax.broadcasted_iota(jnp.int32, (WIN, GW), 1), 6)
    r = jnp.concatenate([jnp.where(lane == g, vn, 0.0) for g in range(4)], axis=0)
    mix = _dot(wcat, r, "nn") + bfull
    return u, xh, rstd, r, mix, lane


def _tril4(w):
    t = lax.broadcasted_iota(jnp.int32, w.shape, 0)
    sidx = lax.broadcasted_iota(jnp.int32, w.shape, 1) & (WIN - 1)
    return jnp.where(sidx <= t, w, 0.0)


def _sgu_fwd(z, lg, lb, wcat, bfull, name):
    s = z.shape[0]
    tm = min(TM, s)

    def body(u_ref, v_ref, lg_ref, lb_ref, w_ref, b_ref, y_ref):
        w = _tril4(w_ref[...])
        for n in range(tm // WIN):
            rows = slice(WIN * n, WIN * (n + 1))
            u, _, _, _, mix, _ = _sgu_chunk(u_ref[rows, :].astype(F32), v_ref[rows, :].astype(F32), lg_ref[...], lb_ref[...], w, b_ref[...])
            y_ref[rows, :] = u * mix

    tile = lambda col: pl.BlockSpec((tm, GW), lambda i: (i, col))
    whole = lambda a: pl.BlockSpec(a.shape, lambda i: (0, 0))
    return pl.pallas_call(
        body, name=name, grid=(s // tm,), in_specs=[tile(Z_GU), tile(Z_GV), whole(lg), whole(lb), whole(wcat), whole(bfull)],
        out_specs=tile(0), out_shape=SDS((s, GW), F32), compiler_params=_cp(("parallel",)))(z, z, lg, lb, wcat, bfull)


def _sgu_bwd(z, dy, lg, lb, wcat, bfull, name):
    s = z.shape[0]
    tm = min(TM, s)

    def body(u_ref, v_ref, dy_ref, lg_ref, lb_ref, w_ref, b_ref, du_ref, dv_ref, dw_ref, db_ref, dlg_ref, dlb_ref):
        first = pl.program_id(0) == 0
        w = _tril4(w_ref[...])
        wt = w.T
        dw = jnp.zeros((WIN, 4 * WIN), F32)
        db = jnp.zeros((WIN, 128), F32)
        dlg = jnp.zeros((1, GW), F32)
        dlb = jnp.zeros((1, GW), F32)
        for n in range(tm // WIN):
            rows = slice(WIN * n, WIN * (n + 1))
            zu, zv, dout = u_ref[rows, :].astype(F32), v_ref[rows, :].astype(F32), dy_ref[rows, :].astype(F32)
            u, xh, rstd, r, mix, lane = _sgu_chunk(zu, zv, lg_ref[...], lb_ref[...], w, b_ref[...])
            dmix = dout * u
            du_ref[rows, :] = (dout * mix * _dgelu(zu)).astype(BF16)
            dw = dw + _dot(dmix, r, "nt")
            for g in range(4):
                db = _lane_put(db, db[:, g:g + 1] + jnp.sum(dmix[:, HD * g:HD * (g + 1)], axis=1, keepdims=True), g)
            dr = _dot(wt, dmix, "nn")
            dvn = jnp.zeros((WIN, GW), F32)
            for g in range(4):
                dvn = dvn + jnp.where(lane == g, dr[WIN * g:WIN * (g + 1), :], 0.0)
            dlg = dlg + _colsum(dvn * xh)
            dlb = dlb + _colsum(dvn)
            dv_ref[rows, :] = (_ln_bwd(xh, rstd, dvn * lg_ref[...]) * _dgelu(zv)).astype(BF16)
        _acc(dw_ref, _tril4(dw), first)
        _acc(db_ref, db, first)
        _acc(dlg_ref, dlg, first)
        _acc(dlb_ref, dlb, first)

    tile = lambda col: pl.BlockSpec((tm, GW), lambda i: (i, col))
    whole = lambda shape: pl.BlockSpec(shape, lambda i: (0, 0))
    return pl.pallas_call(
        body, name=name, grid=(s // tm,),
        in_specs=[tile(Z_GU), tile(Z_GV), tile(0), whole(lg.shape), whole(lb.shape), whole(wcat.shape), whole(bfull.shape)],
        out_specs=[tile(0), tile(0), whole((WIN, 4 * WIN)), whole((WIN, 128)), whole((1, GW)), whole((1, GW))],
        out_shape=[SDS((s, GW), BF16), SDS((s, GW), BF16), SDS((WIN, 4 * WIN), F32), SDS((WIN, 128), F32),
                   SDS((1, GW), F32), SDS((1, GW), F32)],
        compiler_params=_cp(("arbitrary",)))(z, z, dy, lg, lb, wcat, bfull)


def _conv3(win, w, b):
    return (w[2:3, :] * win[FFN_HALO:, :] + w[1:2, :] * pltpu.roll(win, 1, 0)[FFN_HALO:, :]
            + w[0:1, :] * pltpu.roll(win, 2, 0)[FFN_HALO:, :] + b)


def _ffn_specs(s, tm):
    main = pl.BlockSpec((2, None, tm, FF_BLK), lambda j, i: (0, j, i, 0))
    prev = pl.BlockSpec((2, None, FFN_HALO, FF_BLK), lambda j, i: (0, j, jnp.maximum(i * (tm // FFN_HALO) - 1, 0), 0))
    nxt = pl.BlockSpec((2, None, FFN_HALO, FF_BLK),
                       lambda j, i: (0, j, jnp.minimum((i + 1) * (tm // FFN_HALO), s // FFN_HALO - 1), 0))
    wsp = pl.BlockSpec((2, None, 3, FF_BLK), lambda j, i: (0, j, 0, 0))
    bsp = pl.BlockSpec((2, None, 1, FF_BLK), lambda j, i: (0, j, 0, 0))
    return main, prev, nxt, wsp, bsp


def _ffn_act(u4, w4, b4, name, dep=None):
    s = u4.shape[2]
    tm = min(TM, s)
    main, prev, _, wsp, bsp = _ffn_specs(s, tm)

    def body(u_ref, uh_ref, w_ref, b_ref, o_ref, c_ref):
        keep = (pl.program_id(1) > 0).astype(F32)
        gw, vw = [jnp.concatenate([uh_ref[p].astype(F32) * keep, u_ref[p].astype(F32)], axis=0) for p in range(2)]
        gc, vc = _conv3(gw, w_ref[0], b_ref[0]), _conv3(vw, w_ref[1], b_ref[1])
        o_ref[...] = (_silu(gc) * vc).astype(BF16)
        c_ref[0] = gc.astype(BF16)
        c_ref[1] = vc.astype(BF16)

    return _pcall(
        body, (u4, u4, w4, b4), dep, name=name, grid=(FF_NBLK, s // tm), in_specs=[main, prev, wsp, bsp],
        out_specs=[pl.BlockSpec((None, tm, FF_BLK), lambda j, i: (j, i, 0)), main],
        out_shape=[SDS((FF_NBLK, s, FF_BLK), BF16), SDS(u4.shape, BF16)], compiler_params=_cp(("parallel", "parallel")))


def _ffn_bwd(u4, cv4, dact, w4, name, dep=None):
    s = u4.shape[2]
    tm = min(TM, s)
    nt = s // tm
    main, _, nxt, wsp, bsp = _ffn_specs(s, tm)
    dmain = pl.BlockSpec((None, tm, FF_BLK), lambda j, i: (j, i, 0))
    dnext = pl.BlockSpec((None, FFN_HALO, FF_BLK), lambda j, i: (j, jnp.minimum((i + 1) * (tm // FFN_HALO), s // FFN_HALO - 1), 0))
    ext = tm + FFN_HALO

    def body(u_ref, c_ref, cn_ref, d_ref, dn_ref, w_ref, du_ref, dw_ref, db_ref):
        i = pl.program_id(1)
        first = i == 0
        keep_next = (i < nt - 1).astype(F32)
        gc, vc = [jnp.concatenate([c_ref[p].astype(F32), cn_ref[p].astype(F32)], axis=0) for p in range(2)]
        d = jnp.concatenate([d_ref[...].astype(F32), dn_ref[...].astype(F32) * keep_next], axis=0)
        sg = _sigmoid(gc)
        duc = (d * vc * (sg * (1.0 + gc * (1.0 - sg))), d * (gc * sg))
        for p in range(2):
            w = w_ref[p]
            own = duc[p][:tm, :]
            adv = (pltpu.roll(duc[p], ext - 2, 0)[:tm, :], pltpu.roll(duc[p], ext - 1, 0)[:tm, :], own)
            du_ref[p] = (w[2:3, :] * adv[2] + w[1:2, :] * adv[1] + w[0:1, :] * adv[0]).astype(BF16)
            ut = u_ref[p].astype(F32)
            taps = [_colsum(adv[k] * ut) for k in range(3)]

            @pl.when(first)
            def _():
                db_ref[p] = _colsum(own)
                for k in range(3):
                    dw_ref[p, k:k + 1, :] = taps[k]

            @pl.when(jnp.logical_not(first))
            def _():
                db_ref[p] += _colsum(own)
                for k in range(3):
                    dw_ref[p, k:k + 1, :] += taps[k]

    return _pcall(
        body, (u4, cv4, cv4, dact, dact, w4), dep, name=name, grid=(FF_NBLK, nt),
        in_specs=[main, main, nxt, dmain, dnext, wsp], out_specs=[main, wsp, bsp],
        out_shape=[SDS(u4.shape, BF16), SDS((2, FF_NBLK, 3, FF_BLK), F32), SDS((2, FF_NBLK, 1, FF_BLK), F32)],
        compiler_params=_cp(("parallel", "arbitrary")))


def _sum8(parts, name):
    _, r, c = parts[0].shape
    tr = r
    for cand in (512, 256, 128, 64, 32, 16):
        if r % cand == 0 and r > cand:
            tr = cand
            break
    nb = r // tr

    def body(*refs):
        o_ref = refs[-1]
        for l, p_ref in enumerate(refs[:-1]):
            @pl.when(pl.program_id(0) == l)
            def _():
                acc = p_ref[0].astype(F32)
                for j in range(1, N_DEV):
                    acc = acc + p_ref[j].astype(F32)
                o_ref[...] = acc

    def spec(l):
        return pl.BlockSpec((N_DEV, tr, c), lambda ll, i: (0, jnp.where(ll == l, i, jnp.where(ll < l, 0, nb - 1)), 0))

    return pl.pallas_call(
        body, name=name, grid=(len(parts), nb), in_specs=[spec(l) for l in range(len(parts))],
        out_specs=pl.BlockSpec((None, tr, c), lambda ll, i: (ll, i, 0)), out_shape=SDS((len(parts), r, c), F32),
        compiler_params=_cp(("arbitrary", "arbitrary")))(*parts)


def _sum8_small(parts, name):
    n = len(parts)

    def body(*refs):
        for p_ref, o_ref in zip(refs[:n], refs[n:]):
            acc = p_ref[0]
            for j in range(1, N_DEV):
                acc = acc + p_ref[j]
            o_ref[...] = acc

    return pl.pallas_call(body, name=name, out_shape=[SDS(p.shape[1:], F32) for p in parts], compiler_params=_cp())(*parts)


def _adamw_math(w, g, m, v):
    m = ADAM_B1 * m + (1.0 - ADAM_B1) * g
    v = ADAM_B2 * v + (1.0 - ADAM_B2) * (g * g)
    m_hat = m / (1.0 - ADAM_B1 ** ADAM_STEP)
    v_hat = v / (1.0 - ADAM_B2 ** ADAM_STEP)
    return -ADAM_LR * (m_hat / (jnp.sqrt(v_hat) + ADAM_EPS) + ADAM_WD * w), m, v


def _adamw(w, g, m, v, name):
    r, c = w.shape
    tr = r
    for cand in (256, 128, 64):
        if r % cand == 0 and r > cand:
            tr = cand
            break

    def body(w_ref, g_ref, m_ref, v_ref, d_ref, mo_ref, vo_ref):
        d_ref[...], mo_ref[...], vo_ref[...] = _adamw_math(w_ref[...], g_ref[...], m_ref[...], v_ref[...])

    blk = pl.BlockSpec((tr, c), lambda i: (i, 0))
    return pl.pallas_call(body, name=name, grid=(r // tr,), in_specs=[blk] * 4, out_specs=[blk] * 3,
                          out_shape=[SDS((r, c), F32)] * 3, compiler_params=_cp(("parallel",)))(w, g, m, v)


def _adamw_small(ws, gs, ms, vs, name):
    n = len(ws)

    def body(*refs):
        ins, outs = refs[:4 * n], refs[4 * n:]
        for i in range(n):
            d, m, v = _adamw_math(ins[i][...], ins[n + i][...], ins[2 * n + i][...], ins[3 * n + i][...])
            outs[i][...], outs[n + i][...], outs[2 * n + i][...] = d, m, v

    shapes = [SDS(w.shape, F32) for w in ws]
    res = pl.pallas_call(body, name=name, out_shape=shapes * 3, compiler_params=_cp())(*ws, *gs, *ms, *vs)
    return res[:n], res[n:2 * n], res[2 * n:]


def _perm_in(w):
    pad = jnp.zeros(w.shape[:-1] + (ZW - 2308,), w.dtype)
    return jnp.concatenate([w[..., :768], w[..., 772:], w[..., 768:772], pad], axis=-1)


def _unperm_in(g):
    return jnp.concatenate([g[..., :768], g[..., 2304:2308], g[..., 768:2304]], axis=-1)


def _wcat(sgu_w):
    return sgu_w.transpose(1, 0, 2).reshape(WIN, 4 * WIN)


def _layer_fwd(l, x, h1, mod, p, wg, last, target, nxt, w_in_next):
    s = x.shape[0]
    tag = f"_l{l}"
    mrow = lambda k: (mod, 6 * l + k)
    z = _mm_rows(h1, wg["w_in"].get(h1), "nn", ZW, BF16, "mm_z" + tag)
    fcol, frow = _fgate(z, p["bf"], "fgate" + tag, dep=wg["w_out"].prefetch(z))
    y_fox, lse_fox = _fox_fwd(z, fcol, frow, "fox_fwd" + tag)
    y_conv, hc = _conv_fwd(z, wg["conv_w"], p["conv_b"], p["conv_ln_g"], p["conv_ln_b"], wg["conv_pw_w"], p["conv_pw_b"],
                           "conv_fwd" + tag)
    y_swa, lse_swa = _swa_fwd(z, p["sinks"], "swa_fwd" + tag)
    y_sgu = _sgu_fwd(z, p["sgu_ln_g"], p["sgu_ln_b"], p["wcat"], p["bfull"], "sgu_fwd" + tag)
    ys = (y_fox, y_conv, y_swa, y_sgu)
    yn = _gnorm(ys, (p["g_group"], l), "gnorm" + tag)
    tok = wg["w_up"].prefetch(yn)
    o = _mm_rows(yn, wg["w_out"].get(yn), "nn", D, BF16, "mm_o" + tag)
    x1, h2 = _post(x, o, mrow(2), (p["g_post_mix"], l), (p["g_pre_ffn"], l), mrow(4), mrow(3), "post_mix" + tag, dep=tok)
    tok = wg["w_down"].prefetch(h2)
    u = _matmul(h2, wg["w_up"].get(h2), "nn", (N_DEV, s, FF_BLK), BF16, (N_DEV, 1, 1),
                _bs((s, D), lambda j, i, k: (0, 0)), _bs((None, D, FF_BLK), lambda j, i, k: (j, 0, 0)),
                _bs((None, s, FF_BLK), lambda j, i, k: (j, 0, 0)), None, "mm_u" + tag)
    u4 = u.reshape(2, FF_NBLK, s, FF_BLK)
    act, cv4 = _ffn_act(u4, wg["ffn_conv_w"], p["ffn_conv_b"], "ffn_act" + tag, dep=tok)
    tok = None if w_in_next is None else w_in_next.prefetch(act)
    f = _matmul(act, wg["w_down"].get(act), "nn", (s, D), BF16, (1, 1, FF_NBLK),
                _bs((None, s, FF_BLK), lambda i, j, k: (k, 0, 0)), _bs((FF_BLK, D), lambda i, j, k: (k, 0)),
                _bs((s, D), lambda i, j, k: (0, 0)), (s, D), "mm_f" + tag)
    if last:
        out = _post_loss(x1, f, mrow(5), (p["g_post_ffn"], l), target, "post_loss")
    else:
        out = _post(x1, f, mrow(5), (p["g_post_ffn"], l), *nxt, "post_ffn" + tag, dep=tok)
    saved = dict(x=x, h1=h1, z=z, fcol=fcol, frow=frow, lse_fox=lse_fox, hc=hc, lse_swa=lse_swa, ys=ys, yn=yn, o=o, x1=x1,
                 h2=h2, u4=u4, cv4=cv4, act=act, f=f)
    return out, saved


def _tie(a, token):
    return a if token is None else a + token[0, 0]


def _layer_bwd(l, dx2, sv, mod, p, wg, emit, dep=None):
    s = dx2.shape[0]
    tm = min(TM, s)
    tag = f"_l{l}"
    mrow = lambda k: (mod, 6 * l + k)
    g = {}
    df, g["ga2"], g["g_post_ffn"] = _post_bwd(dx2, sv["f"], mrow(5), (p["g_post_ffn"], l), "post_ffn_bwd" + tag, dep=dep)
    dact = _matmul(df, wg["w_down"].get(None), "nt", (FF_NBLK, s, FF_BLK), BF16, (FF_NBLK, 1, 1),
                   _bs((s, D), lambda j, i, k: (0, 0)), _bs((FF_BLK, D), lambda j, i, k: (j, 0)),
                   _bs((None, s, FF_BLK), lambda j, i, k: (j, 0, 0)), None, "mm_dact" + tag)
    tok = emit("w_down", _matmul(sv["act"], df, "tn", (FF_NBLK * FF_BLK, D), BF16, (FF_NBLK, 1, 1),
                                 _bs((None, s, FF_BLK), lambda j, i, k: (j, 0, 0)), _bs((s, D), lambda j, i, k: (0, 0)),
                                 _bs((FF_BLK, D), lambda j, i, k: (j, 0)), None, "mm_dwdown" + tag))
    du, g["ffn_conv_w"], g["ffn_conv_b"] = _ffn_bwd(sv["u4"], sv["cv4"], dact, wg["ffn_conv_w"], "ffn_bwd" + tag, dep=tok)
    du = du.reshape(N_DEV, s, FF_BLK)
    dh2 = _matmul(du, wg["w_up"].get(None), "nt", (s, D), BF16, (1, 1, N_DEV),
                  _bs((None, s, FF_BLK), lambda i, j, k: (k, 0, 0)), _bs((None, D, FF_BLK), lambda i, j, k: (k, 0, 0)),
                  _bs((s, D), lambda i, j, k: (0, 0)), (s, D), "mm_dh2" + tag)
    tok = emit("w_up", _matmul(du, sv["h2"], "tn", (N_DEV, FF_BLK, D), BF16, (N_DEV, 1, 1),
                               _bs((None, s, FF_BLK), lambda j, i, k: (j, 0, 0)), _bs((s, D), lambda j, i, k: (0, 0)),
                               _bs((None, FF_BLK, D), lambda j, i, k: (j, 0, 0)), None, "mm_dwup" + tag))
    dx1, g["sh2"], g["sc2"], g["g_pre_ffn"] = _pre_bwd(dh2, sv["x1"], dx2, (p["g_pre_ffn"], l), mrow(4), "pre_ffn_bwd" + tag,
                                                       dep=tok)
    do, g["ga1"], g["g_post_mix"] = _post_bwd(dx1, sv["o"], mrow(2), (p["g_post_mix"], l), "post_mix_bwd" + tag)
    dyn = _mm_rows(do, wg["w_out"].get(None), "nt", D, BF16, "mm_dyn" + tag)
    tok = emit("w_out", _mm_wgrad(sv["yn"], do, BF16, "mm_dwout" + tag))
    dy_fox, dy_conv, dy_swa, dy_sgu, g["g_group"] = _gnorm_bwd(dyn, sv["ys"], (p["g_group"], l), "gnorm_bwd" + tag, dep=tok)
    z = sv["z"]
    dq_f, dk_f, dv_f, dfrow, dfcol = _fox_bwd(z, sv["fcol"], sv["frow"], sv["lse_fox"], sv["ys"][0], dy_fox, "fox_bwd" + tag)
    dgate, g["bf"] = _fgate_bwd(z, p["bf"], dfrow, dfcol, "fgate_bwd" + tag)
    dhc, g["conv_pw_w"], g["conv_pw_b"], g["conv_ln_g"], g["conv_ln_b"], g["conv_w"], g["conv_b"] = _conv_bwd_a(
        z, sv["hc"], dy_conv, wg["conv_w"], p["conv_ln_g"], p["conv_ln_b"], wg["conv_pw_w"], "conv_bwd_a" + tag)
    da_c, dg_c = _conv_bwd_b(z, dhc, wg["conv_w"], "conv_bwd_b" + tag)
    dq_s, dk_s, dv_s, g["sinks"] = _swa_bwd(z, p["sinks"], sv["lse_swa"], sv["ys"][2], dy_swa, "swa_bwd" + tag)
    du_g, dv_g, g["wcat"], g["sgu_bcol"], g["sgu_ln_g"], g["sgu_ln_b"] = _sgu_bwd(
        z, dy_sgu, p["sgu_ln_g"], p["sgu_ln_b"], p["wcat"], p["bfull"], "sgu_bwd" + tag)
    dz = jnp.concatenate([dq_f.astype(BF16), dk_f.astype(BF16), dv_f.astype(BF16), da_c, dg_c, dq_s.astype(BF16), dk_s.astype(BF16),
                          dv_s.astype(BF16), du_g, dv_g, dgate], axis=1)
    tok = emit("w_in", _mm_wgrad(sv["h1"], dz, BF16, "mm_dwin" + tag))
    dh1 = _mm_rows(dz, wg["w_in"].get(None), "nt", D, BF16, "mm_dh1" + tag)
    dx, g["sh1"], g["sc1"], g["g_pre_mix"] = _pre_bwd(dh1, sv["x"], dx1, (p["g_pre_mix"], l), mrow(1), "pre_mix_bwd" + tag,
                                                      dep=tok)
    return dx, g


def _layer_params(l, small, conv_w_full, conv_pw_full, ffn_conv_w_full):
    bf = jnp.pad(small["b_fgate"][l][None, :], ((0, 0), (0, 124)))
    p = dict(
        bf=bf, conv_b=small["conv_b"][l][None], conv_ln_g=small["conv_ln_g"][l][None], conv_ln_b=small["conv_ln_b"][l][None],
        conv_pw_b=small["conv_pw_b"][l][None], sinks=small["swa_sinks"][l], sgu_ln_g=small["sgu_ln_g"][l][None],
        sgu_ln_b=small["sgu_ln_b"][l][None], wcat=_wcat(small["sgu_w"][l]),
        bfull=jnp.repeat(small["sgu_b"][l].T, HD, axis=1),
        ffn_conv_b=small["ffn_conv_b"][l].reshape(2, FF_NBLK, 1, FF_BLK),
        g_group=small["g_group"].reshape(N_LAYER, 1, D), g_post_mix=small["g_post_mix"].reshape(N_LAYER, 1, D),
        g_pre_ffn=small["g_pre_ffn"].reshape(N_LAYER, 1, D), g_post_ffn=small["g_post_ffn"].reshape(N_LAYER, 1, D),
        g_pre_mix=small["g_pre_mix"].reshape(N_LAYER, 1, D))
    wsmall = dict(conv_w=conv_w_full[l], conv_pw_w=conv_pw_full[l].astype(BF16),
                  ffn_conv_w=ffn_conv_w_full[l].reshape(3, 2, FF_NBLK, FF_BLK).transpose(1, 2, 0, 3))
    return p, wsmall


def _local_step(x, target, mod, small, wbig, conv_w_full, conv_pw_full, ffn_conv_w_full, emit, on_loss=None):
    ps, wgs = [], []
    for l in range(N_LAYER):
        p, wsmall = _layer_params(l, small, conv_w_full, conv_pw_full, ffn_conv_w_full)
        ps.append(p)
        wgs.append({**wbig[l], **wsmall})
    h = _rms_mod(x, (ps[0]["g_pre_mix"], 0), (mod, 1), (mod, 0), "rms_mod_l0")
    saved = []
    for l in range(N_LAYER):
        last = l == N_LAYER - 1
        nxt = None if last else ((ps[l]["g_pre_mix"], l + 1), (mod, 6 * (l + 1) + 1), (mod, 6 * (l + 1)))
        out, sv = _layer_fwd(l, x, h, mod, ps[l], wgs[l], last, target, nxt, None if last else wgs[l + 1]["w_in"])
        saved.append(sv)
        if not last:
            x, h = out
    dx, loss = out
    dep = None if on_loss is None else on_loss(loss)
    grads = [None] * N_LAYER
    for l in reversed(range(N_LAYER)):
        dx, grads[l] = _layer_bwd(l, dx, saved[l], mod, ps[l], wgs[l], functools.partial(emit, l), dep)
        dep = None
    return loss, dx, grads


_SMALL = ("b_ada", "g_pre_mix", "g_post_mix", "g_pre_ffn", "g_post_ffn", "b_fgate", "conv_b", "conv_ln_g", "conv_ln_b",
          "conv_pw_b", "swa_sinks", "sgu_ln_g", "sgu_ln_b", "sgu_w", "sgu_b", "g_group", "ffn_conv_b")
_WEIGHTS = ("w_ada", "b_ada", "g_pre_mix", "g_post_mix", "g_pre_ffn", "g_post_ffn", "w_in", "b_fgate", "conv_w", "conv_b",
            "conv_ln_g", "conv_ln_b", "conv_pw_w", "conv_pw_b", "swa_sinks", "sgu_ln_g", "sgu_ln_b", "sgu_w", "sgu_b",
            "g_group", "w_out", "ffn_w_up", "ffn_conv_w", "ffn_conv_b", "ffn_w_down")


def _pad_rows(a, mult):
    r = (-a.shape[0]) % mult
    return a if r == 0 else jnp.concatenate([a, jnp.zeros((r,) + a.shape[1:], a.dtype)], axis=0)


def _view2d(a):
    if a.ndim == 2:
        return a
    return a.reshape(-1, a.shape[-1])


def kernel(x, c, w_ada, b_ada, g_pre_mix, g_post_mix, g_pre_ffn, g_post_ffn, w_in, b_fgate, conv_w, conv_b, conv_ln_g, conv_ln_b, conv_pw_w, conv_pw_b, swa_sinks, sgu_ln_g, sgu_ln_b, sgu_w, sgu_b, g_group, w_out, ffn_w_up, ffn_conv_w, ffn_conv_b, ffn_w_down, loss_target, m_w_ada, m_b_ada, m_g_pre_mix, m_g_post_mix, m_g_pre_ffn, m_g_post_ffn, m_w_in, m_b_fgate, m_conv_w, m_conv_b, m_conv_ln_g, m_conv_ln_b, m_conv_pw_w, m_conv_pw_b, m_swa_sinks, m_sgu_ln_g, m_sgu_ln_b, m_sgu_w, m_sgu_b, m_g_group, m_w_out, m_ffn_w_up, m_ffn_conv_w, m_ffn_conv_b, m_ffn_w_down, v_w_ada, v_b_ada, v_g_pre_mix, v_g_post_mix, v_g_pre_ffn, v_g_post_ffn, v_w_in, v_b_fgate, v_conv_w, v_conv_b, v_conv_ln_g, v_conv_ln_b, v_conv_pw_w, v_conv_pw_b, v_swa_sinks, v_sgu_ln_g, v_sgu_ln_b, v_sgu_w, v_sgu_b, v_g_group, v_w_out, v_ffn_w_up, v_ffn_conv_w, v_ffn_conv_b, v_ffn_w_down):
    env = dict(locals())
    w = {n: env[n] for n in _WEIGHTS}
    mom = {n: env["m_" + n] for n in _WEIGHTS}
    var = {n: env["v_" + n] for n in _WEIGHTS}
    me = 4 * lax.axis_index("x") + 2 * lax.axis_index("y") + lax.axis_index("c")
    x2, target = x[0], loss_target[0]

    (c_all,) = _exchange([c], ["bcast"], "gather_c")
    c_all = c_all.reshape(N_DEV, D)
    (m_all,) = _exchange([_ada_fwd(c_all, w_ada)], ["bcast"], "gather_mod")
    m_mine = lax.dynamic_index_in_dim(m_all, me, axis=2, keepdims=False)
    mod, mod_token = _ada_finish(m_mine.transpose(1, 0, 2).reshape(N_LAYER, 6 * D), b_ada)
    mod = mod.reshape(6 * N_LAYER, 1, D)

    shards = [_tie(conv_w, mod_token), conv_pw_w, ffn_conv_w]
    for l in range(N_LAYER):
        shards += [_perm_in(w_in[l]).astype(BF16), w_out[l].astype(BF16), ffn_w_up[l].astype(BF16), ffn_w_down[l].astype(BF16)]
    gather = _g2_start(shards, "gather_weights_start")
    mod = _tie(mod, gather.token)
    _g2_relay(gather, [0, 1, 2, 3], mod, "gather_relay_first")
    g_cw, g_pw, g_fcw = _g2_wait(gather, [0, 1, 2], mod, "gather_small_wait")
    conv_w_full = g_cw.transpose(1, 2, 0, 3).reshape(N_LAYER, CONV_K, GW)
    conv_pw_full = g_pw.transpose(1, 0, 2, 3).reshape(N_LAYER, GW, GW)
    ffn_conv_w_full = g_fcw.transpose(1, 2, 0, 3).reshape(N_LAYER, 3, N_DEV * FF_BLK)

    def lazy(i, shape, key):
        pre = None if i == 3 else (lambda after: _g2_relay(gather, [i], after, "relay_" + key))
        return _Lazy(lambda after: _g2_wait(gather, [i], after, "wait_" + key)[0].reshape(shape), pre)

    wbig = [dict(w_in=lazy(3 + 4 * l, (D, ZW), f"w_in_l{l}"), w_out=lazy(4 + 4 * l, (D, D), f"w_out_l{l}"),
                 w_up=lazy(5 + 4 * l, (N_DEV, D, FF_BLK), f"w_up_l{l}"),
                 w_down=lazy(6 + 4 * l, (FF_NBLK * FF_BLK, D), f"w_down_l{l}")) for l in range(N_LAYER)]

    grad_flights = []

    def emit(l, key, arr):
        fl = _xchg_start([arr.reshape(N_DEV, -1, arr.shape[-1])], ["a2a"], f"grad_start_{key}_l{l}")
        grad_flights.append(((l, key), fl))
        return fl.token

    small = {n: w[n] for n in _SMALL}
    total = []

    def on_loss(loss8):
        total.append(lax.psum(loss8[0, 0], ("x", "y", "c")))
        return total[0].reshape(1, 1)

    _, dx, grads = _local_step(x2, target, mod, small, wbig, conv_w_full, conv_pw_full, ffn_conv_w_full, emit, on_loss)
    loss = total[0]
    grad_x = dx[None]


    st = lambda key: jnp.stack([grads[l][key] for l in range(N_LAYER)])
    d_conv_w = st("conv_w")[:, :CONV_K, :].reshape(N_LAYER, CONV_K, N_DEV, GW // N_DEV).transpose(2, 0, 1, 3)
    d_pw_w = st("conv_pw_w").reshape(N_LAYER, N_DEV, GW // N_DEV, GW).transpose(1, 0, 2, 3)
    d_fcw = st("ffn_conv_w").reshape(N_LAYER, N_DEV, 3, FF_BLK).transpose(1, 0, 2, 3)
    rows_d = _pad_rows(jnp.concatenate(
        [grads[l][k] for l in range(N_LAYER) for k in ("sh1", "sc1", "ga1", "sh2", "sc2", "ga2")]
        + [grads[l][k] for k in ("g_pre_mix", "g_post_mix", "g_pre_ffn", "g_post_ffn", "g_group") for l in range(N_LAYER)],
        axis=0), 8)
    rows_gw = _pad_rows(jnp.concatenate(
        [grads[l][k] for k in ("conv_b", "conv_ln_g", "conv_ln_b", "conv_pw_b", "sgu_ln_g", "sgu_ln_b") for l in range(N_LAYER)],
        axis=0), 8)
    rows_128 = jnp.concatenate([_pad_rows(jnp.concatenate([grads[l]["bf"] for l in range(N_LAYER)]
                                                          + [grads[l]["sinks"] for l in range(N_LAYER)], axis=0), 8)]
                               + [grads[l]["sgu_bcol"] for l in range(N_LAYER)], axis=0)
    rows_w = jnp.concatenate([grads[l]["wcat"] for l in range(N_LAYER)], axis=0)
    rows_fb = st("ffn_conv_b").reshape(N_LAYER * N_DEV, FF_BLK)
    small_flight = _xchg_start([d_conv_w, d_pw_w, d_fcw, rows_d, rows_gw, rows_128, rows_w, rows_fb],
                               ["a2a"] * 3 + ["bcast"] * 5, "small_grads_start")

    to_mem = {"w_in": lambda a: a.transpose(2, 0, 1), "ffn_w_up": lambda a: a.transpose(0, 2, 1)}
    from_mem = {"w_in": lambda a: a.transpose(1, 2, 0), "ffn_w_up": lambda a: a.transpose(0, 2, 1)}
    flights = dict(grad_flights)
    gr, delta, new_m, new_v = {}, {}, {}, {}

    def adamw_big(n):
        view, back = to_mem.get(n, lambda a: a), from_mem.get(n, lambda a: a)
        shape = view(w[n]).shape
        d, m2, v2 = _adamw(_view2d(view(w[n])), _view2d(gr[n]), _view2d(view(mom[n])), _view2d(view(var[n])), "adamw_" + n)
        delta[n], new_m[n], new_v[n] = back(d.reshape(shape)), back(m2.reshape(shape)), back(v2.reshape(shape))
        gr[n] = back(gr[n].reshape(shape))

    after = small_flight.token
    for n, key in (("ffn_w_down", "w_down"), ("ffn_w_up", "w_up"), ("w_out", "w_out"), ("w_in", "w_in")):
        parts = [_xchg_wait(flights[(l, key)], [0], after, f"grad_wait_{key}_l{l}")[0] for l in reversed(range(N_LAYER))]
        g = _sum8(parts[::-1], "sum_" + key)
        gr[n] = to_mem["w_in"](_unperm_in(g)) if n == "w_in" else g
        adamw_big(n)
        after = new_v[n]

    small_parts = _xchg_wait(small_flight, list(range(8)), after, "small_grads_wait")
    s_conv_w, s_pw_w, s_fcw, s_d, s_gw, s_128, s_w, s_fb = _sum8_small(
        [p.reshape(N_DEV, -1, p.shape[-1]) for p in small_parts], "sum_small_grads")
    gr["conv_w"] = s_conv_w.reshape(N_LAYER, CONV_K, GW // N_DEV)
    gr["conv_pw_w"] = s_pw_w.reshape(N_LAYER, GW // N_DEV, GW)
    gr["ffn_conv_w"] = s_fcw.reshape(N_LAYER, 3, FF_BLK)
    gr["b_ada"] = s_d[:6 * N_LAYER].reshape(N_LAYER, 6 * D)
    for i, k in enumerate(("g_pre_mix", "g_post_mix", "g_pre_ffn", "g_post_ffn", "g_group")):
        gr[k] = s_d[6 * N_LAYER + 2 * i:6 * N_LAYER + 2 * i + 2]
    for i, k in enumerate(("conv_b", "conv_ln_g", "conv_ln_b", "conv_pw_b", "sgu_ln_g", "sgu_ln_b")):
        gr[k] = s_gw[2 * i:2 * i + 2]
    gr["b_fgate"] = s_128[0:2, :4]
    gr["swa_sinks"] = s_128[2:4, :4]
    gr["sgu_b"] = s_128[8:].reshape(N_LAYER, WIN, 128)[:, :, :4].transpose(0, 2, 1)
    gr["sgu_w"] = s_w.reshape(N_LAYER, WIN, 4, WIN).transpose(0, 2, 1, 3)
    gr["ffn_conv_b"] = s_fb.reshape(N_LAYER, N_DEV * FF_BLK)
    dmod_all = small_parts[3][:, :6 * N_LAYER, :].reshape(N_DEV, N_LAYER, 6 * D)
    ncol = 6 * D // N_DEV
    dmod_cols = lax.dynamic_slice_in_dim(dmod_all, me * ncol, ncol, axis=2).transpose(1, 0, 2)
    gr["w_ada"] = _ada_bwd(c_all, dmod_cols)

    adamw_big("w_ada")
    smalls = [n for n in _WEIGHTS if n not in ("w_ada", "w_in", "w_out", "ffn_w_up", "ffn_w_down")]
    ds, ms, vs = _adamw_small([_view2d(w[n]) for n in smalls], [_view2d(gr[n]) for n in smalls],
                              [_view2d(mom[n]) for n in smalls], [_view2d(var[n]) for n in smalls], "adamw_small")
    for i, n in enumerate(smalls):
        delta[n], new_m[n], new_v[n] = ds[i].reshape(w[n].shape), ms[i].reshape(w[n].shape), vs[i].reshape(w[n].shape)

    return (loss, grad_x, *[gr[n].reshape(w[n].shape) for n in _WEIGHTS], *[delta[n] for n in _WEIGHTS],
            *[new_m[n] for n in _WEIGHTS], *[new_v[n] for n in _WEIGHTS])
```

```python
import functools

import jax
import jax.numpy as jnp
from jax import lax
from jax.experimental import pallas as pl
from jax.experimental.pallas import tpu as pltpu

F32, BF16 = jnp.float32, jnp.bfloat16
SDS = jax.ShapeDtypeStruct
MESH = pl.DeviceIdType.MESH

N_DEV = 8
D = 1024
GW = 256
HD = 64
N_LAYER = 2
ZW = 2432
FF_BLK = 704
FF_NBLK = 4
CONV_K = 31
CONV_HALO = 32
FFN_HALO = 16
FFN_SUB = 128
EPS = 1e-6
NEG = -1e30
SCALE = HD ** -0.5
VMEM_LIMIT_V7X = 56 * 1024 * 1024
TM = 512
WGRAD_ROWS = 256
TQ = 256
WIN = 128

ADAM_LR, ADAM_B1, ADAM_B2, ADAM_EPS, ADAM_WD, ADAM_STEP = 0.001, 0.9, 0.999, 1e-08, 0.01, 10

Z_FQ, Z_FK, Z_FV, Z_CA, Z_CG, Z_SQ = 0, 1, 2, 3, 4, 5
Z_SK, Z_SV = 12, 13
Z_GU, Z_GV = 7, 8
Z_FG = 18


def _cp(sem=None):
    return pltpu.CompilerParams(dimension_semantics=sem, vmem_limit_bytes=VMEM_LIMIT_V7X)


def _vec(arr3, idx, ngrid):
    w = arr3.shape[-1]
    if ngrid == 1:
        return pl.BlockSpec((None, 1, w), lambda i: (idx, 0, 0))
    return pl.BlockSpec((None, 1, w), lambda i, j: (idx, 0, 0))


def _sigmoid(x):
    return jax.nn.sigmoid(x)


def _silu(x):
    return x * _sigmoid(x)


def _dsilu(x):
    s = _sigmoid(x)
    return s * (1.0 + x * (1.0 - s))


_G0, _G1 = 0.7978845608028654, 0.044715


def _gelu(x):
    return 0.5 * x * (1.0 + jnp.tanh(_G0 * (x + _G1 * x * x * x)))


def _dgelu(x):
    t = jnp.tanh(_G0 * (x + _G1 * x * x * x))
    return 0.5 * (1.0 + t) + 0.5 * x * (1.0 - t * t) * (_G0 * (1.0 + 3.0 * _G1 * x * x))


def _rstd(x):
    return lax.rsqrt(jnp.mean(x * x, axis=-1, keepdims=True) + EPS)


def _rms_bwd(xh, r, t):
    return r * (t - xh * jnp.mean(t * xh, axis=-1, keepdims=True))


def _ln_stats(x):
    mu = jnp.mean(x, axis=-1, keepdims=True)
    xc = x - mu
    rstd = lax.rsqrt(jnp.mean(xc * xc, axis=-1, keepdims=True) + EPS)
    return xc * rstd, rstd


def _ln_bwd(xh, rstd, dxh):
    return rstd * (dxh - jnp.mean(dxh, axis=-1, keepdims=True) - xh * jnp.mean(dxh * xh, axis=-1, keepdims=True))


def _colsum(x):
    return jnp.sum(x, axis=0, keepdims=True)


def _dot(a, b, kind):
    dn = {"nn": (((1,), (0,)), ((), ())), "nt": (((1,), (1,)), ((), ())), "tn": (((0,), (0,)), ((), ()))}[kind]
    return lax.dot_general(a.astype(BF16), b.astype(BF16), dn, preferred_element_type=F32)


def _exchange(arrs, modes, name):
    n = len(arrs)
    outs = [SDS((N_DEV,) + a.shape, a.dtype) if m == "bcast" else SDS(a.shape, a.dtype) for a, m in zip(arrs, modes)]

    def body(*refs):
        ins, dst = refs[:n], refs[n:2 * n]
        send, recv, loc = refs[2 * n:]
        x, y, c = lax.axis_index("x"), lax.axis_index("y"), lax.axis_index("c")
        me = 4 * x + 2 * y + c

        def src(a, j):
            return ins[a] if modes[a] == "bcast" else ins[a].at[j]

        local = [pltpu.make_async_copy(src(a, me), dst[a].at[me], loc.at[a]) for a in range(n)]
        for cp in local:
            cp.start()
        sent, landed = [], []
        for k in (2, 4, 6, 3, 5, 7, 1):
            px = 1 - x if k & 4 else x
            py = 1 - y if k & 2 else y
            pc = 1 - c if k & 1 else c
            peer = 4 * px + 2 * py + pc
            for a in range(n):
                cp = pltpu.make_async_remote_copy(src_ref=src(a, peer), dst_ref=dst[a].at[me], send_sem=send.at[a, k - 1],
                                                  recv_sem=recv.at[a, k - 1], device_id=(px, py, pc), device_id_type=MESH)
                cp.start()
                sent.append(cp)
                landed.append(pltpu.make_async_remote_copy(src_ref=src(a, peer), dst_ref=dst[a].at[peer],
                                                           send_sem=send.at[a, k - 1], recv_sem=recv.at[a, k - 1],
                                                           device_id=(px, py, pc), device_id_type=MESH))
        for cp in landed:
            cp.wait_recv()
        for cp in sent:
            cp.wait_send()
        for cp in local:
            cp.wait()

    hbm = pl.BlockSpec(memory_space=pltpu.HBM)
    return pl.pallas_call(
        body, name=name, out_shape=outs, in_specs=[hbm] * n, out_specs=[hbm] * n,
        scratch_shapes=[pltpu.SemaphoreType.DMA((n, N_DEV - 1)), pltpu.SemaphoreType.DMA((n, N_DEV - 1)),
                        pltpu.SemaphoreType.DMA((n,))],
        compiler_params=pltpu.CompilerParams(has_side_effects=True),
    )(*arrs)


_PEER_ORDER = (2, 4, 6, 3, 5, 7, 1)
_HBM = pl.BlockSpec(memory_space=pltpu.HBM)
_SEM = pl.BlockSpec(memory_space=pltpu.SEMAPHORE)
_EFFECT = pltpu.SideEffectType.DATAFLOW_SIDE_EFFECTING


def _peer(k):
    x, y, c = lax.axis_index("x"), lax.axis_index("y"), lax.axis_index("c")
    px = 1 - x if k & 4 else x
    py = 1 - y if k & 2 else y
    pc = 1 - c if k & 1 else c
    return (px, py, pc), 4 * px + 2 * py + pc


def _my_id():
    return 4 * lax.axis_index("x") + 2 * lax.axis_index("y") + lax.axis_index("c")


def _split_copies(src_ref, land_ref, send, recv, loc, mode):
    me = _my_id()
    pick = (lambda j: src_ref) if mode == "bcast" else (lambda j: src_ref.at[j])
    local = pltpu.make_async_copy(pick(me), land_ref.at[me], loc)
    remote = []
    for k in _PEER_ORDER:
        dev, peer = _peer(k)
        out = pltpu.make_async_remote_copy(src_ref=pick(peer), dst_ref=land_ref.at[me], send_sem=send.at[k - 1],
                                           recv_sem=recv.at[k - 1], device_id=dev, device_id_type=MESH)
        arrive = pltpu.make_async_remote_copy(src_ref=pick(peer), dst_ref=land_ref.at[peer], send_sem=send.at[k - 1],
                                              recv_sem=recv.at[k - 1], device_id=dev, device_id_type=MESH)
        remote.append((out, arrive))
    return local, remote


class _Flight:
    def __init__(self, srcs, lands, sends, recvs, locs, modes, token):
        self.srcs, self.lands, self.sends, self.recvs, self.locs, self.modes, self.token = (
            list(srcs), list(lands), list(sends), list(recvs), list(locs), list(modes), token)


def _xchg_start(arrs, modes, name):
    n = len(arrs)
    lands = [lax.empty((N_DEV,) + a.shape if m == "bcast" else a.shape, a.dtype) for a, m in zip(arrs, modes)]

    def body(*refs):
        srcs, lnds = refs[:n], refs[n:2 * n]
        outs = refs[2 * n:]
        sends, recvs, locs, token = outs[:n], outs[n:2 * n], outs[2 * n:3 * n], outs[5 * n]
        for a in range(n):
            local, remote = _split_copies(srcs[a], lnds[a], sends[a], recvs[a], locs[a], modes[a])
            local.start()
            for out, _ in remote:
                out.start()
        token[...] = jnp.zeros_like(token)

    sem7 = pltpu.SemaphoreType.DMA((N_DEV - 1,))
    res = pl.pallas_call(
        body, name=name,
        out_shape=[sem7] * (2 * n) + [pltpu.SemaphoreType.DMA(())] * n + [pltpu.HBM(a.shape, a.dtype) for a in arrs]
        + [pltpu.HBM(b.shape, b.dtype) for b in lands] + [SDS((8, 128), F32)],
        in_specs=[_HBM] * (2 * n), out_specs=[_SEM] * (3 * n) + [_HBM] * (2 * n) + [pl.BlockSpec(memory_space=pltpu.VMEM)],
        input_output_aliases={i: 3 * n + i for i in range(2 * n)},
        compiler_params=pltpu.CompilerParams(has_side_effects=_EFFECT),
    )(*[pltpu.with_memory_space_constraint(a, pltpu.HBM) for a in arrs],
      *[pltpu.with_memory_space_constraint(b, pltpu.HBM) for b in lands])
    return _Flight(res[3 * n:4 * n], res[4 * n:5 * n], res[:n], res[n:2 * n], res[2 * n:3 * n], modes, res[5 * n])


def _xchg_wait(flight, idx, after, name):
    n = len(idx)
    modes = [flight.modes[i] for i in idx]

    def body(*refs):
        srcs, lnds = refs[:n], refs[n:2 * n]
        sends, recvs, locs = refs[2 * n:3 * n], refs[3 * n:4 * n], refs[4 * n:5 * n]
        for a in range(n):
            local, remote = _split_copies(srcs[a], lnds[a], sends[a], recvs[a], locs[a], modes[a])
            local.wait()
            for _, arrive in remote:
                arrive.wait_send()
                arrive.wait_recv()

    ops = ([flight.srcs[i] for i in idx] + [flight.lands[i] for i in idx] + [flight.sends[i] for i in idx]
           + [flight.recvs[i] for i in idx] + [flight.locs[i] for i in idx])
    res = pl.pallas_call(
        body, name=name, out_shape=[pltpu.HBM(o.shape, o.dtype) for o in ops[:2 * n]],
        in_specs=[_HBM] * (2 * n) + [_SEM] * (3 * n) + [pl.BlockSpec(memory_space=pl.ANY)], out_specs=[_HBM] * (2 * n),
        input_output_aliases={i: i for i in range(2 * n)},
        compiler_params=pltpu.CompilerParams(has_side_effects=_EFFECT),
    )(*ops, after)
    return res[n:]


class _Lazy:
    def __init__(self, fn, pre=None):
        self.fn, self.pre, self.val, self.started = fn, pre, None, False

    def prefetch(self, after):
        token = self.pre(after) if self.pre is not None and not self.started else None
        self.started = True
        return token

    def get(self, after):
        self.prefetch(after)
        if self.val is None:
            self.val = self.fn(after)
        return self.val


_CHIP_PEERS = (2, 4, 6)


def _g2_copies_a(src_ref, land_ref, send, recv, loc):
    me = _my_id()
    local = pltpu.make_async_copy(src_ref, land_ref.at[me], loc)
    remote = []
    for j, k in enumerate(_CHIP_PEERS + (1,)):
        dev, peer = _peer(k)
        out = pltpu.make_async_remote_copy(src_ref=src_ref, dst_ref=land_ref.at[me], send_sem=send.at[j], recv_sem=recv.at[j],
                                           device_id=dev, device_id_type=MESH)
        arrive = pltpu.make_async_remote_copy(src_ref=src_ref, dst_ref=land_ref.at[peer], send_sem=send.at[j],
                                              recv_sem=recv.at[j], device_id=dev, device_id_type=MESH)
        remote.append((out, arrive))
    return local, remote


def _g2_copies_b(land_ref, send, recv):
    sib, _ = _peer(1)
    pairs = []
    for j, k in enumerate(_CHIP_PEERS):
        _, same_core = _peer(k)
        _, other_core = _peer(k | 1)
        out = pltpu.make_async_remote_copy(src_ref=land_ref.at[same_core], dst_ref=land_ref.at[same_core], send_sem=send.at[j],
                                           recv_sem=recv.at[j], device_id=sib, device_id_type=MESH)
        arrive = pltpu.make_async_remote_copy(src_ref=land_ref.at[same_core], dst_ref=land_ref.at[other_core],
                                              send_sem=send.at[j], recv_sem=recv.at[j], device_id=sib, device_id_type=MESH)
        pairs.append((out, arrive))
    return pairs


class _Gather2:
    def __init__(self, srcs, lands, sends, recvs, locs, token):
        self.srcs, self.lands, self.sends, self.recvs, self.locs, self.token = (
            list(srcs), list(lands), list(sends), list(recvs), list(locs), token)
        self.sends_b, self.recvs_b = [None] * len(self.srcs), [None] * len(self.srcs)


def _g2_start(arrs, name):
    n = len(arrs)
    lands = [lax.empty((N_DEV,) + a.shape, a.dtype) for a in arrs]

    def body(*refs):
        srcs, lnds = refs[:n], refs[n:2 * n]
        outs = refs[2 * n:]
        sends, recvs, locs, token = outs[:n], outs[n:2 * n], outs[2 * n:3 * n], outs[5 * n]
        for a in range(n):
            local, remote = _g2_copies_a(srcs[a], lnds[a], sends[a], recvs[a], locs[a])
            local.start()
            for out, _ in remote:
                out.start()
        token[...] = jnp.zeros_like(token)

    sem4 = pltpu.SemaphoreType.DMA((4,))
    res = pl.pallas_call(
        body, name=name,
        out_shape=[sem4] * (2 * n) + [pltpu.SemaphoreType.DMA(())] * n + [pltpu.HBM(a.shape, a.dtype) for a in arrs]
        + [pltpu.HBM(b.shape, b.dtype) for b in lands] + [SDS((8, 128), F32)],
        in_specs=[_HBM] * (2 * n), out_specs=[_SEM] * (3 * n) + [_HBM] * (2 * n) + [pl.BlockSpec(memory_space=pltpu.VMEM)],
        input_output_aliases={i: 3 * n + i for i in range(2 * n)},
        compiler_params=pltpu.CompilerParams(has_side_effects=_EFFECT),
    )(*[pltpu.with_memory_space_constraint(a, pltpu.HBM) for a in arrs],
      *[pltpu.with_memory_space_constraint(b, pltpu.HBM) for b in lands])
    return _Gather2(res[3 * n:4 * n], res[4 * n:5 * n], res[:n], res[n:2 * n], res[2 * n:3 * n], res[5 * n])


def _g2_relay(g, idx, after, name):
    n = len(idx)

    def body(*refs):
        srcs, lnds = refs[:n], refs[n:2 * n]
        sends, recvs, locs = refs[2 * n:3 * n], refs[3 * n:4 * n], refs[4 * n:5 * n]
        outs = refs[5 * n + 1:]
        sends_b, recvs_b = outs[2 * n:3 * n], outs[3 * n:4 * n]
        for a in range(n):
            local, remote = _g2_copies_a(srcs[a], lnds[a], sends[a], recvs[a], locs[a])
            local.wait()
            for _, arrive in remote:
                arrive.wait_send()
                arrive.wait_recv()
        for a in range(n):
            for out, _ in _g2_copies_b(lnds[a], sends_b[a], recvs_b[a]):
                out.start()
        outs[4 * n][...] = jnp.zeros_like(outs[4 * n])

    ops = ([g.srcs[i] for i in idx] + [g.lands[i] for i in idx] + [g.sends[i] for i in idx] + [g.recvs[i] for i in idx]
           + [g.locs[i] for i in idx])
    sem3 = pltpu.SemaphoreType.DMA((3,))
    res = pl.pallas_call(
        body, name=name, out_shape=[pltpu.HBM(o.shape, o.dtype) for o in ops[:2 * n]] + [sem3] * (2 * n) + [SDS((8, 128), F32)],
        in_specs=[_HBM] * (2 * n) + [_SEM] * (3 * n) + [pl.BlockSpec(memory_space=pl.ANY)],
        out_specs=[_HBM] * (2 * n) + [_SEM] * (2 * n) + [pl.BlockSpec(memory_space=pltpu.VMEM)],
        input_output_aliases={i: i for i in range(2 * n)},
        compiler_params=pltpu.CompilerParams(has_side_effects=_EFFECT),
    )(*ops, after)
    for a, i in enumerate(idx):
        g.srcs[i], g.lands[i] = res[a], res[n + a]
        g.sends_b[i], g.recvs_b[i] = res[2 * n + a], res[3 * n + a]
    return res[4 * n]


def _g2_wait(g, idx, after, name):
    n = len(idx)

    def body(*refs):
        lnds, sends_b, recvs_b = refs[:n], refs[n:2 * n], refs[2 * n:3 * n]
        for a in range(n):
            for _, arrive in _g2_copies_b(lnds[a], sends_b[a], recvs_b[a]):
                arrive.wait_send()
                arrive.wait_recv()

    ops = [g.lands[i] for i in idx] + [g.sends_b[i] for i in idx] + [g.recvs_b[i] for i in idx]
    res = pl.pallas_call(
        body, name=name, out_shape=[pltpu.HBM(o.shape, o.dtype) for o in ops[:n]],
        in_specs=[_HBM] * n + [_SEM] * (2 * n) + [pl.BlockSpec(memory_space=pl.ANY)], out_specs=[_HBM] * n,
        input_output_aliases={i: i for i in range(n)},
        compiler_params=pltpu.CompilerParams(has_side_effects=_EFFECT),
    )(*ops, after)
    return list(res)


def _matmul(a, b, kind, out_shape, out_dtype, grid, a_spec, b_spec, o_spec, acc_shape, name):
    nk = grid[2]

    def body(a_ref, b_ref, o_ref, *scratch):
        prod = _dot(a_ref[...], b_ref[...], kind)
        if nk == 1:
            o_ref[...] = prod.astype(out_dtype)
        else:
            acc = scratch[0]
            k = pl.program_id(2)

            @pl.when(k == 0)
            def _():
                acc[...] = prod

            @pl.when(k > 0)
            def _():
                acc[...] += prod

            @pl.when(k == nk - 1)
            def _():
                o_ref[...] = acc[...].astype(out_dtype)

    return pl.pallas_call(
        body, name=name, grid=grid, in_specs=[a_spec, b_spec], out_specs=o_spec, out_shape=SDS(out_shape, out_dtype),
        scratch_shapes=[] if nk == 1 else [pltpu.VMEM(acc_shape, F32)],
        compiler_params=_cp(("parallel", "parallel", "arbitrary")))(a, b)


def _bs(shape, fn):
    return pl.BlockSpec(shape, fn)


def _mm_rows(a, w, kind, n_out, out_dtype, name):
    s, k = a.shape
    tm = min(TM, s)
    return _matmul(a, w, kind, (s, n_out), out_dtype, (s // tm, 1, 1),
                   _bs((tm, k), lambda i, j, kk: (i, 0)), _bs(w.shape, lambda i, j, kk: (0, 0)),
                   _bs((tm, n_out), lambda i, j, kk: (i, 0)), None, name)


def _mm_wgrad(a, dy, out_dtype, name):
    s, k = a.shape
    n = dy.shape[1]
    tko = min(WGRAD_ROWS, k)
    return _matmul(a, dy, "tn", (k, n), out_dtype, (k // tko, 1, 1),
                   _bs((s, tko), lambda i, j, kk: (0, i)), _bs((s, n), lambda i, j, kk: (0, 0)),
                   _bs((tko, n), lambda i, j, kk: (i, 0)), None, name)


def _ada_fwd(c_all, w_ada):
    ncol = w_ada.shape[2]

    def body(c_ref, w_ref, o_ref):
        ca = _silu(c_ref[...])
        ca = jnp.concatenate([ca, jnp.zeros_like(ca)], axis=0)
        o_ref[...] = _dot(ca, w_ref[...], "nn")[:N_DEV, :]

    return pl.pallas_call(
        body, name="ada_fwd", grid=(N_LAYER,),
        in_specs=[pl.BlockSpec((N_DEV, D), lambda l: (0, 0)), pl.BlockSpec((None, D, ncol), lambda l: (l, 0, 0))],
        out_specs=pl.BlockSpec((None, N_DEV, ncol), lambda l: (l, 0, 0)),
        out_shape=SDS((N_LAYER, N_DEV, ncol), F32), compiler_params=_cp(("parallel",)))(c_all, w_ada)


def _ada_finish(m_mine, b_ada):
    def body(m_ref, b_ref, o_ref, t_ref):
        o_ref[...] = m_ref[...] + b_ref[...]
        t_ref[...] = jnp.zeros_like(t_ref)

    return pl.pallas_call(body, name="ada_finish", out_shape=[SDS(b_ada.shape, F32), SDS((8, 128), F32)])(m_mine, b_ada)


def _ada_bwd(c_all, dmod_cols):
    ncol = dmod_cols.shape[2]

    def body(c_ref, d_ref, o_ref):
        ca = _silu(c_ref[...])
        ca = jnp.concatenate([ca, jnp.zeros_like(ca)], axis=0)
        dm = d_ref[...]
        dm = jnp.concatenate([dm, jnp.zeros_like(dm)], axis=0)
        o_ref[...] = _dot(ca, dm, "tn")

    return pl.pallas_call(
        body, name="ada_bwd", grid=(N_LAYER,),
        in_specs=[pl.BlockSpec((N_DEV, D), lambda l: (0, 0)), pl.BlockSpec((None, N_DEV, ncol), lambda l: (l, 0, 0))],
        out_specs=pl.BlockSpec((None, D, ncol), lambda l: (l, 0, 0)),
        out_shape=SDS((N_LAYER, D, ncol), F32), compiler_params=_cp(("parallel",)))(c_all, dmod_cols)


def _rows(s):
    tm = min(TM, s)
    return tm, pl.BlockSpec((tm, D), lambda i: (i, 0))


def _pcall(body, operands, dep, **kw):
    if dep is None:
        return pl.pallas_call(body, **kw)(*operands)
    n = len(operands)

    def body_dep(*refs):
        body(*refs[:n], *refs[n + 1:])

    kw["in_specs"] = list(kw["in_specs"]) + [pl.BlockSpec(memory_space=pl.ANY)]
    return pl.pallas_call(body_dep, **kw)(*operands, dep)


def _rms_mod(x, g, sc, sh, name):
    s = x.shape[0]
    tm, row = _rows(s)

    def body(x_ref, g_ref, sc_ref, sh_ref, h_ref):
        xf = x_ref[...]
        h_ref[...] = (xf * _rstd(xf) * (g_ref[...] * (1.0 + sc_ref[...])) + sh_ref[...]).astype(BF16)

    return pl.pallas_call(
        body, name=name, grid=(s // tm,), in_specs=[row, _vec(*g, 1), _vec(*sc, 1), _vec(*sh, 1)], out_specs=row,
        out_shape=SDS((s, D), BF16), compiler_params=_cp(("parallel",)))(x, g[0], sc[0], sh[0])


def _post(xres, o, ga, gpost, gn, scn, shn, name, dep=None):
    s = xres.shape[0]
    tm, row = _rows(s)

    def body(x_ref, o_ref, ga_ref, gp_ref, gn_ref, sc_ref, sh_ref, xn_ref, h_ref):
        of = o_ref[...].astype(F32)
        xn = x_ref[...] + ga_ref[...] * (of * _rstd(of) * gp_ref[...])
        xn_ref[...] = xn
        h_ref[...] = (xn * _rstd(xn) * (gn_ref[...] * (1.0 + sc_ref[...])) + sh_ref[...]).astype(BF16)

    return _pcall(
        body, (xres, o, ga[0], gpost[0], gn[0], scn[0], shn[0]), dep, name=name, grid=(s // tm,),
        in_specs=[row, row, _vec(*ga, 1), _vec(*gpost, 1), _vec(*gn, 1), _vec(*scn, 1), _vec(*shn, 1)],
        out_specs=[row, row], out_shape=[SDS((s, D), F32), SDS((s, D), BF16)], compiler_params=_cp(("parallel",)))


def _post_loss(xres, o, ga, gpost, target, name, dep=None):
    s = xres.shape[0]
    tm, row = _rows(s)

    def body(x_ref, o_ref, ga_ref, gp_ref, t_ref, dy_ref, loss_ref):
        of = o_ref[...].astype(F32)
        err = x_ref[...] + ga_ref[...] * (of * _rstd(of) * gp_ref[...]) - t_ref[...]
        dy_ref[...] = err * (1.0 / D)

        @pl.when(pl.program_id(0) == 0)
        def _():
            loss_ref[...] = jnp.zeros_like(loss_ref)

        loss_ref[...] += jnp.sum(jnp.mean(err * err, axis=-1, keepdims=True), axis=0, keepdims=True) * 0.5

    return _pcall(
        body, (xres, o, ga[0], gpost[0], target), dep, name=name, grid=(s // tm,),
        in_specs=[row, row, _vec(*ga, 1), _vec(*gpost, 1), row],
        out_specs=[row, pl.BlockSpec((8, 128), lambda i: (0, 0))], out_shape=[SDS((s, D), F32), SDS((8, 128), F32)],
        compiler_params=_cp(("arbitrary",)))


def _acc(ref, val, first):
    @pl.when(first)
    def _():
        ref[...] = val

    @pl.when(jnp.logical_not(first))
    def _():
        ref[...] += val


def _post_bwd(dxn, o, ga, gpost, name, dep=None):
    s = dxn.shape[0]
    tm, row = _rows(s)
    vec = pl.BlockSpec((1, D), lambda i: (0, 0))

    def body(d_ref, o_ref, ga_ref, gp_ref, do_ref, dga_ref, dgp_ref):
        of, dx = o_ref[...].astype(F32), d_ref[...]
        r = _rstd(of)
        oh = of * r
        do_ref[...] = _rms_bwd(oh, r, dx * (ga_ref[...] * gp_ref[...])).astype(BF16)
        cs = _colsum(dx * oh)
        first = pl.program_id(0) == 0
        _acc(dga_ref, cs * gp_ref[...], first)
        _acc(dgp_ref, cs * ga_ref[...], first)

    return _pcall(
        body, (dxn, o, ga[0], gpost[0]), dep, name=name, grid=(s // tm,),
        in_specs=[row, row, _vec(*ga, 1), _vec(*gpost, 1)], out_specs=[row, vec, vec],
        out_shape=[SDS((s, D), BF16), SDS((1, D), F32), SDS((1, D), F32)], compiler_params=_cp(("arbitrary",)))


def _pre_bwd(dh, x, dres, g, sc, name, dep=None):
    s = x.shape[0]
    tm, row = _rows(s)
    vec = pl.BlockSpec((1, D), lambda i: (0, 0))

    def body(dh_ref, x_ref, dr_ref, g_ref, sc_ref, dx_ref, dsh_ref, dsc_ref, dg_ref):
        xf, d = x_ref[...], dh_ref[...].astype(F32)
        r = _rstd(xf)
        xh = xf * r
        dx_ref[...] = dr_ref[...] + _rms_bwd(xh, r, d * (g_ref[...] * (1.0 + sc_ref[...])))
        cs = _colsum(d * xh)
        first = pl.program_id(0) == 0
        _acc(dsh_ref, _colsum(d), first)
        _acc(dsc_ref, cs * g_ref[...], first)
        _acc(dg_ref, cs * (1.0 + sc_ref[...]), first)

    return _pcall(
        body, (dh, x, dres, g[0], sc[0]), dep, name=name, grid=(s // tm,),
        in_specs=[row, row, row, _vec(*g, 1), _vec(*sc, 1)], out_specs=[row, vec, vec, vec],
        out_shape=[SDS((s, D), F32), SDS((1, D), F32), SDS((1, D), F32), SDS((1, D), F32)],
        compiler_params=_cp(("arbitrary",)))


def _gnorm(ys, gg, name):
    s = ys[0].shape[0]
    tm = min(TM, s)
    yb = pl.BlockSpec((tm, GW), lambda i: (i, 0))

    def body(y0, y1, y2, y3, g_ref, o_ref):
        for i, yr in enumerate((y0, y1, y2, y3)):
            y = yr[...]
            o_ref[:, GW * i:GW * (i + 1)] = (y * _rstd(y) * g_ref[:, GW * i:GW * (i + 1)]).astype(BF16)

    return pl.pallas_call(
        body, name=name, grid=(s // tm,), in_specs=[yb] * 4 + [_vec(*gg, 1)], out_specs=pl.BlockSpec((tm, D), lambda i: (i, 0)),
        out_shape=SDS((s, D), BF16), compiler_params=_cp(("parallel",)))(*ys, gg[0])


def _gnorm_bwd(dyn, ys, gg, name, dep=None):
    s = ys[0].shape[0]
    tm = min(TM, s)
    yb = pl.BlockSpec((tm, GW), lambda i: (i, 0))

    def body(d_ref, y0, y1, y2, y3, g_ref, o0, o1, o2, o3, dg_ref):
        first = pl.program_id(0) == 0
        for i, (yr, orf) in enumerate(zip((y0, y1, y2, y3), (o0, o1, o2, o3))):
            y = yr[...]
            d = d_ref[:, GW * i:GW * (i + 1)].astype(F32)
            r = _rstd(y)
            yh = y * r
            orf[...] = _rms_bwd(yh, r, d * g_ref[:, GW * i:GW * (i + 1)]).astype(BF16)
            cs = _colsum(d * yh)

            @pl.when(first)
            def _():
                dg_ref[:, GW * i:GW * (i + 1)] = cs

            @pl.when(jnp.logical_not(first))
            def _():
                dg_ref[:, GW * i:GW * (i + 1)] += cs

    return _pcall(
        body, (dyn, *ys, gg[0]), dep, name=name, grid=(s // tm,),
        in_specs=[pl.BlockSpec((tm, D), lambda i: (i, 0))] + [yb] * 4 + [_vec(*gg, 1)],
        out_specs=[yb] * 4 + [pl.BlockSpec((1, D), lambda i: (0, 0))],
        out_shape=[SDS((s, GW), BF16)] * 4 + [SDS((1, D), F32)], compiler_params=_cp(("arbitrary",)))


def _lane_put(acc, col, h):
    lane = lax.broadcasted_iota(jnp.int32, acc.shape, 1)
    return jnp.where(lane == h, col, acc)


def _fgate(z, bf, name, dep=None):
    s = z.shape[0]

    def body(z_ref, b_ref, fc_ref, fr_ref):
        xg = z_ref[...].astype(F32) + b_ref[...]
        lf = jnp.minimum(xg, 0.0) - jnp.log(1.0 + jnp.exp(-jnp.abs(xg)))
        lane = lax.broadcasted_iota(jnp.int32, lf.shape, 1)
        row = lax.broadcasted_iota(jnp.int32, lf.shape, 0)
        f = jnp.where(lane < 4, lf, 0.0)
        sh = 1
        while sh < s:
            f = f + jnp.where(row >= sh, pltpu.roll(f, sh, 0), 0.0)
            sh *= 2
        fc_ref[...] = f
        fr_ref[...] = f.T[:8, :]

    return _pcall(
        body, (z, bf), dep, name=name, grid=(1,),
        in_specs=[pl.BlockSpec((s, 128), lambda i: (0, Z_FG)), pl.BlockSpec((1, 128), lambda i: (0, 0))],
        out_specs=[pl.BlockSpec((s, 128), lambda i: (0, 0)), pl.BlockSpec((8, s), lambda i: (0, 0))],
        out_shape=[SDS((s, 128), F32), SDS((8, s), F32)], compiler_params=_cp(("arbitrary",)))


def _fgate_bwd(z, bf, dfrow, dfcol, name):
    s = z.shape[0]

    def body(z_ref, b_ref, d_ref, dc_ref, dz_ref, db_ref):
        d = jnp.concatenate([d_ref[...], jnp.zeros((120, s), F32)], axis=0).T + dc_ref[...]
        row = lax.broadcasted_iota(jnp.int32, d.shape, 0)
        lane = lax.broadcasted_iota(jnp.int32, d.shape, 1)
        sh = 1
        while sh < s:
            d = d + jnp.where(row < s - sh, pltpu.roll(d, s - sh, 0), 0.0)
            sh *= 2
        xg = z_ref[...].astype(F32) + b_ref[...]
        dz = jnp.where(lane < 4, d * _sigmoid(-xg), 0.0)
        dz_ref[...] = dz.astype(BF16)
        db_ref[...] = _colsum(dz)

    return pl.pallas_call(
        body, name=name, grid=(1,),
        in_specs=[pl.BlockSpec((s, 128), lambda i: (0, Z_FG)), pl.BlockSpec((1, 128), lambda i: (0, 0)),
                  pl.BlockSpec((8, s), lambda i: (0, 0)), pl.BlockSpec((s, 128), lambda i: (0, 0))],
        out_specs=[pl.BlockSpec((s, 128), lambda i: (0, 0)), pl.BlockSpec((1, 128), lambda i: (0, 0))],
        out_shape=[SDS((s, 128), BF16), SDS((1, 128), F32)], compiler_params=_cp(("arbitrary",)))(z, bf, dfrow, dfcol)


def _fox_scores(q_ref, k_ref, fc_ref, fr_ref, h, i, nk, tq):
    kw = nk * tq
    q = q_ref[:, HD * h:HD * (h + 1)] * SCALE
    sc = _dot(q, k_ref[0:kw, HD * h:HD * (h + 1)], "nt") + fc_ref[:, h:h + 1] - fr_ref[h:h + 1, 0:kw]
    qpos = i * tq + lax.broadcasted_iota(jnp.int32, (tq, kw), 0)
    kpos = lax.broadcasted_iota(jnp.int32, (tq, kw), 1)
    return q, jnp.where(kpos <= qpos, sc, NEG)


def _fox_fwd(z, fcol, frow, name):
    s = z.shape[0]
    tq = min(TQ, s)
    nc = s // tq

    def body(q_ref, k_ref, v_ref, fc_ref, fr_ref, y_ref, l_ref):
        for n in range(nc):
            @pl.when(pl.program_id(0) == n)
            def _():
                kw = (n + 1) * tq
                lse = jnp.zeros((tq, 128), F32)
                for h in range(4):
                    _, sc = _fox_scores(q_ref, k_ref, fc_ref, fr_ref, h, n, n + 1, tq)
                    m = jnp.max(sc, axis=-1, keepdims=True)
                    p = jnp.exp(sc - m)
                    l = jnp.sum(p, axis=-1, keepdims=True)
                    y_ref[:, HD * h:HD * (h + 1)] = _dot(p, v_ref[0:kw, HD * h:HD * (h + 1)], "nn") / l
                    lse = _lane_put(lse, m + jnp.log(l), h)
                l_ref[...] = lse

    return pl.pallas_call(
        body, name=name, grid=(nc,),
        in_specs=[pl.BlockSpec((tq, GW), lambda i: (i, Z_FQ)), pl.BlockSpec((s, GW), lambda i: (0, Z_FK)),
                  pl.BlockSpec((s, GW), lambda i: (0, Z_FV)), pl.BlockSpec((tq, 128), lambda i: (i, 0)),
                  pl.BlockSpec((8, s), lambda i: (0, 0))],
        out_specs=[pl.BlockSpec((tq, GW), lambda i: (i, 0)), pl.BlockSpec((tq, 128), lambda i: (i, 0))],
        out_shape=[SDS((s, GW), F32), SDS((s, 128), F32)], compiler_params=_cp(("parallel",)))(z, z, z, fcol, frow)


def _fox_bwd(z, fcol, frow, lse, y, dy, name):
    s = z.shape[0]
    tq = min(TQ, s)
    nc = s // tq
    half = max(nc // 2, 1)

    def body(q_ref, k_ref, v_ref, fc_ref, fr_ref, l_ref, y_ref, dy_ref, dq_ref, dk_ref, dv_ref, df_ref, dfq_ref):
        @pl.when(pl.program_id(0) == 0)
        def _():
            dk_ref[...] = jnp.zeros_like(dk_ref)
            dv_ref[...] = jnp.zeros_like(dv_ref)
            df_ref[...] = jnp.zeros_like(df_ref)

        i = pl.program_id(0)
        for cond, nk in ((i < half, half), (i >= half, nc)):
            @pl.when(cond)
            def _():
                kw = nk * tq
                dfq = jnp.zeros((tq, 128), F32)
                dyf = dy_ref[...].astype(F32)
                for h in range(4):
                    hs = slice(HD * h, HD * (h + 1))
                    q, sc = _fox_scores(q_ref, k_ref, fc_ref, fr_ref, h, i, nk, tq)
                    p = jnp.exp(sc - l_ref[:, h:h + 1])
                    dyh = dyf[:, hs]
                    dd = jnp.sum(dyh * y_ref[:, hs], axis=-1, keepdims=True)
                    ds = p * (_dot(dyh, v_ref[0:kw, hs], "nt") - dd)
                    dq_ref[:, hs] = _dot(ds, k_ref[0:kw, hs], "nn") * SCALE
                    dk_ref[0:kw, hs] += _dot(ds, q, "tn")
                    dv_ref[0:kw, hs] += _dot(p, dyh, "tn")
                    df_ref[h:h + 1, 0:kw] -= _colsum(ds)
                    dfq = _lane_put(dfq, jnp.sum(ds, axis=-1, keepdims=True), h)
                dfq_ref[...] = dfq

    tile = lambda w: pl.BlockSpec((tq, w), lambda i: (i, 0))
    full = pl.BlockSpec((s, GW), lambda i: (0, 0))
    rows8 = pl.BlockSpec((8, s), lambda i: (0, 0))
    return pl.pallas_call(
        body, name=name, grid=(nc,),
        in_specs=[pl.BlockSpec((tq, GW), lambda i: (i, Z_FQ)), pl.BlockSpec((s, GW), lambda i: (0, Z_FK)),
                  pl.BlockSpec((s, GW), lambda i: (0, Z_FV)), tile(128), rows8, tile(128), tile(GW), tile(GW)],
        out_specs=[tile(GW), full, full, rows8, tile(128)],
        out_shape=[SDS((s, GW), F32), SDS((s, GW), F32), SDS((s, GW), F32), SDS((8, s), F32), SDS((s, 128), F32)],
        compiler_params=_cp(("arbitrary",)))(z, z, z, fcol, frow, lse, y, dy)


def _swa_block(q_ref, k_ref, v_ref, n):
    qs = pl.multiple_of(n * WIN, WIN)
    ks = pl.multiple_of(jnp.maximum(n - 1, 0) * WIN, WIN)
    qb = q_ref[pl.ds(qs, WIN), :]
    kb = k_ref[pl.ds(ks, 2 * WIN), :]
    vb = v_ref[pl.ds(ks, 2 * WIN), :]
    rows = lax.broadcasted_iota(jnp.int32, (2 * WIN, 2 * WIN), 0) & (WIN - 1)
    dist = (qs + rows) - (ks + lax.broadcasted_iota(jnp.int32, (2 * WIN, 2 * WIN), 1))
    return qs, ks, qb, kb, vb, (dist >= 0) & (dist < WIN)


def _stack2(x, kvh):
    return jnp.concatenate([x[:, HD * (2 * kvh):HD * (2 * kvh + 1)], x[:, HD * (2 * kvh + 1):HD * (2 * kvh + 2)]], axis=0)


def _sink2(sink_ref, kvh):
    top = lax.broadcasted_iota(jnp.int32, (2 * WIN, 1), 0) < WIN
    return jnp.where(top, sink_ref[2 * kvh], sink_ref[2 * kvh + 1])


def _swa_fwd(z, sinks, name):
    s = z.shape[0]

    def body(sink_ref, q_ref, k_ref, v_ref, y_ref, l_ref):
        def step(n, carry):
            qs, ks, qb, kb, vb, valid = _swa_block(q_ref, k_ref, v_ref, n)
            lse = jnp.zeros((WIN, 128), F32)
            for kvh in range(2):
                kv = slice(HD * kvh, HD * (kvh + 1))
                sc = jnp.where(valid, _dot(_stack2(qb, kvh) * SCALE, kb[:, kv], "nt"), NEG)
                sink = _sink2(sink_ref, kvh)
                m = jnp.maximum(jnp.max(sc, axis=-1, keepdims=True), sink)
                p = jnp.exp(sc - m)
                den = jnp.sum(p, axis=-1, keepdims=True) + jnp.exp(sink - m)
                o = _dot(p, vb[:, kv], "nn") / den
                lrow = m + jnp.log(den)
                for j in range(2):
                    h = 2 * kvh + j
                    y_ref[pl.ds(qs, WIN), HD * h:HD * (h + 1)] = o[WIN * j:WIN * (j + 1), :]
                    lse = _lane_put(lse, lrow[WIN * j:WIN * (j + 1), :], h)
            l_ref[pl.ds(qs, WIN), :] = lse
            return carry

        lax.fori_loop(0, s // WIN, step, 0, unroll=2)

    return pl.pallas_call(
        body, name=name, grid=(1,),
        in_specs=[pl.BlockSpec(memory_space=pltpu.SMEM), pl.BlockSpec((s, GW), lambda i: (0, Z_SQ)),
                  pl.BlockSpec((s, 128), lambda i: (0, Z_SK)), pl.BlockSpec((s, 128), lambda i: (0, Z_SV))],
        out_specs=[pl.BlockSpec((s, GW), lambda i: (0, 0)), pl.BlockSpec((s, 128), lambda i: (0, 0))],
        out_shape=[SDS((s, GW), F32), SDS((s, 128), F32)], compiler_params=_cp(("arbitrary",)))(sinks, z, z, z)


def _swa_bwd(z, sinks, lse, y, dy, name):
    s = z.shape[0]

    def body(sink_ref, q_ref, k_ref, v_ref, l_ref, y_ref, dy_ref, dq_ref, dk_ref, dv_ref, dsink_ref):
        dk_ref[...] = jnp.zeros_like(dk_ref)
        dv_ref[...] = jnp.zeros_like(dv_ref)
        dsink_ref[...] = jnp.zeros_like(dsink_ref)

        def step(n, carry):
            qs, ks, qb, kb, vb, valid = _swa_block(q_ref, k_ref, v_ref, n)
            lse_b = l_ref[pl.ds(qs, WIN), :]
            yb = y_ref[pl.ds(qs, WIN), :]
            dyb = dy_ref[pl.ds(qs, WIN), :].astype(F32)
            dsink = jnp.zeros((1, 128), F32)
            for kvh in range(2):
                kv = slice(HD * kvh, HD * (kvh + 1))
                q = _stack2(qb, kvh) * SCALE
                dy2 = _stack2(dyb, kvh)
                sc = jnp.where(valid, _dot(q, kb[:, kv], "nt"), NEG)
                lh = jnp.concatenate([lse_b[:, 2 * kvh:2 * kvh + 1], lse_b[:, 2 * kvh + 1:2 * kvh + 2]], axis=0)
                p = jnp.exp(sc - lh)
                dd = jnp.sum(dy2 * _stack2(yb, kvh), axis=-1, keepdims=True)
                ds = p * (_dot(dy2, vb[:, kv], "nt") - dd)
                dq = _dot(ds, kb[:, kv], "nn") * SCALE
                dk_ref[pl.ds(ks, 2 * WIN), kv] += _dot(ds, q, "tn")
                dv_ref[pl.ds(ks, 2 * WIN), kv] += _dot(p, dy2, "tn")
                dsk = jnp.exp(_sink2(sink_ref, kvh) - lh) * dd
                for j in range(2):
                    h = 2 * kvh + j
                    dq_ref[pl.ds(qs, WIN), HD * h:HD * (h + 1)] = dq[WIN * j:WIN * (j + 1), :]
                    dsink = _lane_put(dsink, dsink[:, h:h + 1] - jnp.sum(dsk[WIN * j:WIN * (j + 1), :], axis=0, keepdims=True), h)
            dsink_ref[...] += dsink
            return carry

        lax.fori_loop(0, s // WIN, step, 0, unroll=2)

    full = lambda w: pl.BlockSpec((s, w), lambda i: (0, 0))
    return pl.pallas_call(
        body, name=name, grid=(1,),
        in_specs=[pl.BlockSpec(memory_space=pltpu.SMEM), pl.BlockSpec((s, GW), lambda i: (0, Z_SQ)),
                  pl.BlockSpec((s, 128), lambda i: (0, Z_SK)), pl.BlockSpec((s, 128), lambda i: (0, Z_SV)),
                  full(128), full(GW), full(GW)],
        out_specs=[full(GW), full(128), full(128), pl.BlockSpec((1, 128), lambda i: (0, 0))],
        out_shape=[SDS((s, GW), F32), SDS((s, 128), F32), SDS((s, 128), F32), SDS((1, 128), F32)],
        compiler_params=_cp(("arbitrary",)))(sinks, z, z, z, lse, y, dy)


_SUBLANES = 8


def _rotations(win, advance=False):
    n = win.shape[0]
    return [win] + [pltpu.roll(win, (n - b) if advance else b, 0) for b in range(1, _SUBLANES)]


def _delayed(rots, shift, halo, tm):
    a, b = divmod(shift, _SUBLANES)
    return rots[b][halo - _SUBLANES * a:halo - _SUBLANES * a + tm, :]


def _advanced(rots, shift, tm):
    a, b = divmod(shift, _SUBLANES)
    return rots[b][_SUBLANES * a:_SUBLANES * a + tm, :]


def _prev_halo(width, halo, tm, col):
    return pl.BlockSpec((halo, width), lambda i: (jnp.maximum(i * (tm // halo) - 1, 0), col))


def _glu_window(a_ref, g_ref, ah_ref, gh_ref):
    keep = (pl.program_id(0) > 0).astype(F32)
    a = jnp.concatenate([ah_ref[...].astype(F32) * keep, a_ref[...].astype(F32)], axis=0)
    g = jnp.concatenate([gh_ref[...].astype(F32), g_ref[...].astype(F32)], axis=0)
    return a * _sigmoid(g)


def _conv_fwd(z, cw, cb, lg, lb, pw, pb, name):
    s = z.shape[0]
    tm = min(TM, s)

    def body(a_ref, g_ref, ah_ref, gh_ref, w_ref, b_ref, lg_ref, lb_ref, pw_ref, pb_ref, y_ref, hc_ref):
        rots = _rotations(_glu_window(a_ref, g_ref, ah_ref, gh_ref))
        hc = jnp.zeros((tm, GW), F32) + b_ref[...]
        for k in range(CONV_K):
            hc = hc + w_ref[k:k + 1, :] * _delayed(rots, CONV_K - 1 - k, CONV_HALO, tm)
        hc_ref[...] = hc
        xh, _ = _ln_stats(hc)
        y_ref[...] = _dot(_silu(xh * lg_ref[...] + lb_ref[...]), pw_ref[...], "nn") + pb_ref[...]

    tile = lambda col: pl.BlockSpec((tm, GW), lambda i: (i, col))
    whole = lambda a: pl.BlockSpec(a.shape, lambda i: (0, 0))
    return pl.pallas_call(
        body, name=name, grid=(s // tm,),
        in_specs=[tile(Z_CA), tile(Z_CG), _prev_halo(GW, CONV_HALO, tm, Z_CA), _prev_halo(GW, CONV_HALO, tm, Z_CG),
                  whole(cw), whole(cb), whole(lg), whole(lb), whole(pw), whole(pb)],
        out_specs=[tile(0), tile(0)], out_shape=[SDS((s, GW), F32), SDS((s, GW), F32)],
        compiler_params=_cp(("parallel",)))(z, z, z, z, cw, cb, lg, lb, pw, pb)


def _conv_bwd_a(z, hc, dy, cw, lg, lb, pw, name):
    s = z.shape[0]
    tm = min(TM, s)

    def body(a_ref, g_ref, ah_ref, gh_ref, hc_ref, dy_ref, lg_ref, lb_ref, pw_ref,
             dhc_ref, dpw_ref, dpb_ref, dlg_ref, dlb_ref, dcw_ref, dcb_ref):
        first = pl.program_id(0) == 0
        dy = dy_ref[...].astype(F32)
        xh, rstd = _ln_stats(hc_ref[...])
        hn = xh * lg_ref[...] + lb_ref[...]
        dhn = _dot(dy, pw_ref[...], "nt") * _dsilu(hn)
        dhc = _ln_bwd(xh, rstd, dhn * lg_ref[...])
        dhc_ref[...] = dhc
        _acc(dpw_ref, _dot(_silu(hn), dy, "tn"), first)
        _acc(dpb_ref, _colsum(dy), first)
        _acc(dlg_ref, _colsum(dhn * xh), first)
        _acc(dlb_ref, _colsum(dhn), first)
        _acc(dcb_ref, _colsum(dhc), first)
        rots = _rotations(_glu_window(a_ref, g_ref, ah_ref, gh_ref))

        @pl.when(first)
        def _():
            dcw_ref[...] = jnp.zeros_like(dcw_ref)

        for k in range(CONV_K):
            dcw_ref[k:k + 1, :] += _colsum(dhc * _delayed(rots, CONV_K - 1 - k, CONV_HALO, tm))

    tile = lambda col: pl.BlockSpec((tm, GW), lambda i: (i, col))
    whole = lambda shape: pl.BlockSpec(shape, lambda i: (0, 0))
    return pl.pallas_call(
        body, name=name, grid=(s // tm,),
        in_specs=[tile(Z_CA), tile(Z_CG), _prev_halo(GW, CONV_HALO, tm, Z_CA), _prev_halo(GW, CONV_HALO, tm, Z_CG),
                  tile(0), tile(0), whole(lg.shape), whole(lb.shape), whole(pw.shape)],
        out_specs=[tile(0), whole((GW, GW)), whole((1, GW)), whole((1, GW)), whole((1, GW)), whole((32, GW)), whole((1, GW))],
        out_shape=[SDS((s, GW), F32), SDS((GW, GW), F32), SDS((1, GW), F32), SDS((1, GW), F32), SDS((1, GW), F32),
                   SDS((32, GW), F32), SDS((1, GW), F32)],
        compiler_params=_cp(("arbitrary",)))(z, z, z, z, hc, dy, lg, lb, pw)


def _conv_bwd_b(z, dhc, cw, name):
    s = z.shape[0]
    tm = min(TM, s)
    nt = s // tm

    def body(a_ref, g_ref, d_ref, dn_ref, w_ref, da_ref, dg_ref):
        keep = (pl.program_id(0) < nt - 1).astype(F32)
        rots = _rotations(jnp.concatenate([d_ref[...], dn_ref[...] * keep], axis=0), advance=True)
        dhg = jnp.zeros((tm, GW), F32)
        for k in range(CONV_K):
            dhg = dhg + w_ref[k:k + 1, :] * _advanced(rots, CONV_K - 1 - k, tm)
        sg = _sigmoid(g_ref[...].astype(F32))
        da_ref[...] = (dhg * sg).astype(BF16)
        dg_ref[...] = (dhg * a_ref[...].astype(F32) * sg * (1.0 - sg)).astype(BF16)

    tile = lambda col: pl.BlockSpec((tm, GW), lambda i: (i, col))
    nxt = pl.BlockSpec((CONV_HALO, GW), lambda i: (jnp.minimum((i + 1) * (tm // CONV_HALO), s // CONV_HALO - 1), 0))
    return pl.pallas_call(
        body, name=name, grid=(nt,),
        in_specs=[tile(Z_CA), tile(Z_CG), tile(0), nxt, pl.BlockSpec(cw.shape, lambda i: (0, 0))],
        out_specs=[tile(0), tile(0)], out_shape=[SDS((s, GW), BF16), SDS((s, GW), BF16)],
        compiler_params=_cp(("parallel",)))(z, z, dhc, dhc, cw)


def _sgu_chunk(zu, zv, lg, lb, wcat, bfull):
    u, v = _gelu(zu), _gelu(zv)
    xh, rstd = _ln_stats(v)
    vn = xh * lg + lb
    lane = lax.shift_right_logical(lax.broadcasted_iota(jnp.int32, (WIN, GW), 1), 6)
    r = jnp.concatenate([jnp.where(lane == g, vn, 0.0) for g in range(4)], axis=0)
    mix = _dot(wcat, r, "nn") + bfull
    return u, xh, rstd, r, mix, lane


def _tril4(w):
    t = lax.broadcasted_iota(jnp.int32, w.shape, 0)
    sidx = lax.broadcasted_iota(jnp.int32, w.shape, 1) & (WIN - 1)
    return jnp.where(sidx <= t, w, 0.0)


def _sgu_fwd(z, lg, lb, wcat, bfull, name):
    s = z.shape[0]
    tm = min(TM, s)

    def body(u_ref, v_ref, lg_ref, lb_ref, w_ref, b_ref, y_ref):
        w = _tril4(w_ref[...])
        for n in range(tm // WIN):
            rows = slice(WIN * n, WIN * (n + 1))
            u, _, _, _, mix, _ = _sgu_chunk(u_ref[rows, :].astype(F32), v_ref[rows, :].astype(F32), lg_ref[...], lb_ref[...], w, b_ref[...])
            y_ref[rows, :] = u * mix

    tile = lambda col: pl.BlockSpec((tm, GW), lambda i: (i, col))
    whole = lambda a: pl.BlockSpec(a.shape, lambda i: (0, 0))
    return pl.pallas_call(
        body, name=name, grid=(s // tm,), in_specs=[tile(Z_GU), tile(Z_GV), whole(lg), whole(lb), whole(wcat), whole(bfull)],
        out_specs=tile(0), out_shape=SDS((s, GW), F32), compiler_params=_cp(("parallel",)))(z, z, lg, lb, wcat, bfull)


def _sgu_bwd(z, dy, lg, lb, wcat, bfull, name):
    s = z.shape[0]
    tm = min(TM, s)

    def body(u_ref, v_ref, dy_ref, lg_ref, lb_ref, w_ref, b_ref, du_ref, dv_ref, dw_ref, db_ref, dlg_ref, dlb_ref):
        first = pl.program_id(0) == 0
        w = _tril4(w_ref[...])
        wt = w.T
        dw = jnp.zeros((WIN, 4 * WIN), F32)
        db = jnp.zeros((WIN, 128), F32)
        dlg = jnp.zeros((1, GW), F32)
        dlb = jnp.zeros((1, GW), F32)
        for n in range(tm // WIN):
            rows = slice(WIN * n, WIN * (n + 1))
            zu, zv, dout = u_ref[rows, :].astype(F32), v_ref[rows, :].astype(F32), dy_ref[rows, :].astype(F32)
            u, xh, rstd, r, mix, lane = _sgu_chunk(zu, zv, lg_ref[...], lb_ref[...], w, b_ref[...])
            dmix = dout * u
            du_ref[rows, :] = (dout * mix * _dgelu(zu)).astype(BF16)
            dw = dw + _dot(dmix, r, "nt")
            for g in range(4):
                db = _lane_put(db, db[:, g:g + 1] + jnp.sum(dmix[:, HD * g:HD * (g + 1)], axis=1, keepdims=True), g)
            dr = _dot(wt, dmix, "nn")
            dvn = jnp.zeros((WIN, GW), F32)
            for g in range(4):
                dvn = dvn + jnp.where(lane == g, dr[WIN * g:WIN * (g + 1), :], 0.0)
            dlg = dlg + _colsum(dvn * xh)
            dlb = dlb + _colsum(dvn)
            dv_ref[rows, :] = (_ln_bwd(xh, rstd, dvn * lg_ref[...]) * _dgelu(zv)).astype(BF16)
        _acc(dw_ref, _tril4(dw), first)
        _acc(db_ref, db, first)
        _acc(dlg_ref, dlg, first)
        _acc(dlb_ref, dlb, first)

    tile = lambda col: pl.BlockSpec((tm, GW), lambda i: (i, col))
    whole = lambda shape: pl.BlockSpec(shape, lambda i: (0, 0))
    return pl.pallas_call(
        body, name=name, grid=(s // tm,),
        in_specs=[tile(Z_GU), tile(Z_GV), tile(0), whole(lg.shape), whole(lb.shape), whole(wcat.shape), whole(bfull.shape)],
        out_specs=[tile(0), tile(0), whole((WIN, 4 * WIN)), whole((WIN, 128)), whole((1, GW)), whole((1, GW))],
        out_shape=[SDS((s, GW), BF16), SDS((s, GW), BF16), SDS((WIN, 4 * WIN), F32), SDS((WIN, 128), F32),
                   SDS((1, GW), F32), SDS((1, GW), F32)],
        compiler_params=_cp(("arbitrary",)))(z, z, dy, lg, lb, wcat, bfull)


def _conv3(win, w, b):
    return (w[2:3, :] * win[FFN_HALO:, :] + w[1:2, :] * pltpu.roll(win, 1, 0)[FFN_HALO:, :]
            + w[0:1, :] * pltpu.roll(win, 2, 0)[FFN_HALO:, :] + b)


def _ffn_specs(s, tm):
    main = pl.BlockSpec((2, None, tm, FF_BLK), lambda j, i: (0, j, i, 0))
    prev = pl.BlockSpec((2, None, FFN_HALO, FF_BLK), lambda j, i: (0, j, jnp.maximum(i * (tm // FFN_HALO) - 1, 0), 0))
    nxt = pl.BlockSpec((2, None, FFN_HALO, FF_BLK),
                       lambda j, i: (0, j, jnp.minimum((i + 1) * (tm // FFN_HALO), s // FFN_HALO - 1), 0))
    wsp = pl.BlockSpec((2, None, 3, FF_BLK), lambda j, i: (0, j, 0, 0))
    bsp = pl.BlockSpec((2, None, 1, FF_BLK), lambda j, i: (0, j, 0, 0))
    return main, prev, nxt, wsp, bsp


def _ffn_act(u4, w4, b4, name, dep=None):
    s = u4.shape[2]
    tm = min(TM, s)
    main, prev, _, wsp, bsp = _ffn_specs(s, tm)

    def body(u_ref, uh_ref, w_ref, b_ref, o_ref, c_ref):
        keep = (pl.program_id(1) > 0).astype(F32)
        gw, vw = [jnp.concatenate([uh_ref[p].astype(F32) * keep, u_ref[p].astype(F32)], axis=0) for p in range(2)]
        gc, vc = _conv3(gw, w_ref[0], b_ref[0]), _conv3(vw, w_ref[1], b_ref[1])
        o_ref[...] = (_silu(gc) * vc).astype(BF16)
        c_ref[0] = gc.astype(BF16)
        c_ref[1] = vc.astype(BF16)

    return _pcall(
        body, (u4, u4, w4, b4), dep, name=name, grid=(FF_NBLK, s // tm), in_specs=[main, prev, wsp, bsp],
        out_specs=[pl.BlockSpec((None, tm, FF_BLK), lambda j, i: (j, i, 0)), main],
        out_shape=[SDS((FF_NBLK, s, FF_BLK), BF16), SDS(u4.shape, BF16)], compiler_params=_cp(("parallel", "parallel")))


def _ffn_bwd(u4, cv4, dact, w4, w_up, name, dep=None):
    s = u4.shape[2]
    tm = min(TM, s)
    nt = s // tm
    main = pl.BlockSpec((2, None, tm, FF_BLK), lambda i, j: (0, j, i, 0))
    nxt = pl.BlockSpec((2, None, FFN_HALO, FF_BLK),
                       lambda i, j: (0, j, jnp.minimum((i + 1) * (tm // FFN_HALO), s // FFN_HALO - 1), 0))
    dmain = pl.BlockSpec((None, tm, FF_BLK), lambda i, j: (j, i, 0))
    dnext = pl.BlockSpec((None, FFN_HALO, FF_BLK), lambda i, j: (j, jnp.minimum((i + 1) * (tm // FFN_HALO), s // FFN_HALO - 1), 0))
    wsp = pl.BlockSpec((2, None, 3, FF_BLK), lambda i, j: (0, j, 0, 0))
    wup = pl.BlockSpec((2, None, D, FF_BLK), lambda i, j: (0, j, 0, 0))
    all_w = pl.BlockSpec((2, FF_NBLK, 3, FF_BLK), lambda i, j: (0, 0, 0, 0))
    all_b = pl.BlockSpec((2, FF_NBLK, 1, FF_BLK), lambda i, j: (0, 0, 0, 0))

    def body(u_ref, c_ref, cn_ref, d_ref, dn_ref, w_ref, wup_ref, du_ref, dw_ref, db_ref, dh_ref, acc_ref):
        i, j = pl.program_id(0), pl.program_id(1)
        first = i == 0
        keep_next = (i < nt - 1).astype(F32)
        taps = [[jnp.zeros((1, FF_BLK), F32) for _ in range(3)] for _ in range(2)]
        bias = [jnp.zeros((1, FF_BLK), F32) for _ in range(2)]
        def matmul(rows, dus):
            prod = _dot(dus[0], wup_ref[0], "nt") + _dot(dus[1], wup_ref[1], "nt")
            acc_ref[rows, :] = jnp.where(j == 0, prod, acc_ref[rows, :] + prod)

        pending = []
        for r0 in range(0, tm, FFN_SUB):
            rows, wide = slice(r0, r0 + FFN_SUB), slice(r0, r0 + FFN_SUB + FFN_HALO)
            if r0 + FFN_SUB < tm:
                gc, vc = [c_ref[p, wide, :].astype(F32) for p in range(2)]
                d = d_ref[wide, :].astype(F32)
            else:
                gc, vc = [jnp.concatenate([c_ref[p, rows, :].astype(F32), cn_ref[p].astype(F32)], axis=0) for p in range(2)]
                d = jnp.concatenate([d_ref[rows, :].astype(F32), dn_ref[...].astype(F32) * keep_next], axis=0)
            sg = _sigmoid(gc)
            duc = (d * vc * (sg * (1.0 + gc * (1.0 - sg))), d * (gc * sg))
            dus = []
            for p in range(2):
                w = w_ref[p]
                own = duc[p][:FFN_SUB, :]
                adv = (pltpu.roll(duc[p], FFN_SUB + FFN_HALO - 2, 0)[:FFN_SUB, :],
                       pltpu.roll(duc[p], FFN_SUB + FFN_HALO - 1, 0)[:FFN_SUB, :], own)
                du = (w[2:3, :] * adv[2] + w[1:2, :] * adv[1] + w[0:1, :] * adv[0]).astype(BF16)
                du_ref[p, rows, :] = du
                dus.append(du)
                ut = u_ref[p, rows, :].astype(F32)
                for k in range(3):
                    taps[p][k] = taps[p][k] + _colsum(adv[k] * ut)
                bias[p] = bias[p] + _colsum(own)
            pending.append((rows, dus))
            if len(pending) > 1:
                matmul(*pending.pop(0))
        matmul(*pending.pop(0))

        for p in range(2):
            @pl.when(first)
            def _():
                db_ref[p, j] = bias[p]
                for k in range(3):
                    dw_ref[p, j, k:k + 1, :] = taps[p][k]

            @pl.when(jnp.logical_not(first))
            def _():
                db_ref[p, j] += bias[p]
                for k in range(3):
                    dw_ref[p, j, k:k + 1, :] += taps[p][k]

        @pl.when(j == FF_NBLK - 1)
        def _():
            dh_ref[...] = acc_ref[...].astype(BF16)

    return _pcall(
        body, (u4, cv4, cv4, dact, dact, w4, w_up.reshape(2, FF_NBLK, D, FF_BLK)), dep, name=name, grid=(nt, FF_NBLK),
        in_specs=[main, main, nxt, dmain, dnext, wsp, wup],
        out_specs=[main, all_w, all_b, pl.BlockSpec((tm, D), lambda i, j: (i, 0))],
        out_shape=[SDS(u4.shape, BF16), SDS((2, FF_NBLK, 3, FF_BLK), F32), SDS((2, FF_NBLK, 1, FF_BLK), F32), SDS((s, D), BF16)],
        scratch_shapes=[pltpu.VMEM((tm, D), F32)], compiler_params=_cp(("arbitrary", "arbitrary")))


def _sum8(parts, name):
    _, r, c = parts[0].shape
    tr = r
    for cand in (512, 256, 128, 64, 32, 16):
        if r % cand == 0 and r > cand:
            tr = cand
            break
    nb = r // tr

    def body(*refs):
        o_ref = refs[-1]
        for l, p_ref in enumerate(refs[:-1]):
            @pl.when(pl.program_id(0) == l)
            def _():
                acc = p_ref[0].astype(F32)
                for j in range(1, N_DEV):
                    acc = acc + p_ref[j].astype(F32)
                o_ref[...] = acc

    def spec(l):
        return pl.BlockSpec((N_DEV, tr, c), lambda ll, i: (0, jnp.where(ll == l, i, jnp.where(ll < l, 0, nb - 1)), 0))

    return pl.pallas_call(
        body, name=name, grid=(len(parts), nb), in_specs=[spec(l) for l in range(len(parts))],
        out_specs=pl.BlockSpec((None, tr, c), lambda ll, i: (ll, i, 0)), out_shape=SDS((len(parts), r, c), F32),
        compiler_params=_cp(("arbitrary", "arbitrary")))(*parts)


def _sum8_small(parts, name):
    n = len(parts)

    def body(*refs):
        for p_ref, o_ref in zip(refs[:n], refs[n:]):
            acc = p_ref[0]
            for j in range(1, N_DEV):
                acc = acc + p_ref[j]
            o_ref[...] = acc

    return pl.pallas_call(body, name=name, out_shape=[SDS(p.shape[1:], F32) for p in parts], compiler_params=_cp())(*parts)


def _adamw_math(w, g, m, v):
    m = ADAM_B1 * m + (1.0 - ADAM_B1) * g
    v = ADAM_B2 * v + (1.0 - ADAM_B2) * (g * g)
    m_hat = m / (1.0 - ADAM_B1 ** ADAM_STEP)
    v_hat = v / (1.0 - ADAM_B2 ** ADAM_STEP)
    return -ADAM_LR * (m_hat / (jnp.sqrt(v_hat) + ADAM_EPS) + ADAM_WD * w), m, v


def _adamw(w, g, m, v, name):
    r, c = w.shape
    tr = r
    for cand in (256, 128, 64):
        if r % cand == 0 and r > cand:
            tr = cand
            break

    def body(w_ref, g_ref, m_ref, v_ref, d_ref, mo_ref, vo_ref):
        d_ref[...], mo_ref[...], vo_ref[...] = _adamw_math(w_ref[...], g_ref[...], m_ref[...], v_ref[...])

    blk = pl.BlockSpec((tr, c), lambda i: (i, 0))
    return pl.pallas_call(body, name=name, grid=(r // tr,), in_specs=[blk] * 4, out_specs=[blk] * 3,
                          out_shape=[SDS((r, c), F32)] * 3, compiler_params=_cp(("parallel",)))(w, g, m, v)


def _adamw_small(ws, gs, ms, vs, name):
    n = len(ws)

    def body(*refs):
        ins, outs = refs[:4 * n], refs[4 * n:]
        for i in range(n):
            d, m, v = _adamw_math(ins[i][...], ins[n + i][...], ins[2 * n + i][...], ins[3 * n + i][...])
            outs[i][...], outs[n + i][...], outs[2 * n + i][...] = d, m, v

    shapes = [SDS(w.shape, F32) for w in ws]
    res = pl.pallas_call(body, name=name, out_shape=shapes * 3, compiler_params=_cp())(*ws, *gs, *ms, *vs)
    return res[:n], res[n:2 * n], res[2 * n:]


def _perm_in(w):
    pad = jnp.zeros(w.shape[:-1] + (ZW - 2308,), w.dtype)
    return jnp.concatenate([w[..., :768], w[..., 772:], w[..., 768:772], pad], axis=-1)


def _unperm_in(g):
    return jnp.concatenate([g[..., :768], g[..., 2304:2308], g[..., 768:2304]], axis=-1)


def _wcat(sgu_w):
    return sgu_w.transpose(1, 0, 2).reshape(WIN, 4 * WIN)


def _layer_fwd(l, x, h1, mod, p, wg, last, target, nxt, w_in_next):
    s = x.shape[0]
    tag = f"_l{l}"
    mrow = lambda k: (mod, 6 * l + k)
    z = _mm_rows(h1, wg["w_in"].get(h1), "nn", ZW, BF16, "mm_z" + tag)
    fcol, frow = _fgate(z, p["bf"], "fgate" + tag, dep=wg["w_out"].prefetch(z))
    y_fox, lse_fox = _fox_fwd(z, fcol, frow, "fox_fwd" + tag)
    y_conv, hc = _conv_fwd(z, wg["conv_w"], p["conv_b"], p["conv_ln_g"], p["conv_ln_b"], wg["conv_pw_w"], p["conv_pw_b"],
                           "conv_fwd" + tag)
    y_swa, lse_swa = _swa_fwd(z, p["sinks"], "swa_fwd" + tag)
    y_sgu = _sgu_fwd(z, p["sgu_ln_g"], p["sgu_ln_b"], p["wcat"], p["bfull"], "sgu_fwd" + tag)
    ys = (y_fox, y_conv, y_swa, y_sgu)
    yn = _gnorm(ys, (p["g_group"], l), "gnorm" + tag)
    tok = wg["w_up"].prefetch(yn)
    o = _mm_rows(yn, wg["w_out"].get(yn), "nn", D, BF16, "mm_o" + tag)
    x1, h2 = _post(x, o, mrow(2), (p["g_post_mix"], l), (p["g_pre_ffn"], l), mrow(4), mrow(3), "post_mix" + tag, dep=tok)
    tok = wg["w_down"].prefetch(h2)
    u = _matmul(h2, wg["w_up"].get(h2), "nn", (N_DEV, s, FF_BLK), BF16, (N_DEV, 1, 1),
                _bs((s, D), lambda j, i, k: (0, 0)), _bs((None, D, FF_BLK), lambda j, i, k: (j, 0, 0)),
                _bs((None, s, FF_BLK), lambda j, i, k: (j, 0, 0)), None, "mm_u" + tag)
    u4 = u.reshape(2, FF_NBLK, s, FF_BLK)
    act, cv4 = _ffn_act(u4, wg["ffn_conv_w"], p["ffn_conv_b"], "ffn_act" + tag, dep=tok)
    tok = None if w_in_next is None else w_in_next.prefetch(act)
    f = _matmul(act, wg["w_down"].get(act), "nn", (s, D), BF16, (1, 1, FF_NBLK),
                _bs((None, s, FF_BLK), lambda i, j, k: (k, 0, 0)), _bs((FF_BLK, D), lambda i, j, k: (k, 0)),
                _bs((s, D), lambda i, j, k: (0, 0)), (s, D), "mm_f" + tag)
    if last:
        out = _post_loss(x1, f, mrow(5), (p["g_post_ffn"], l), target, "post_loss")
    else:
        out = _post(x1, f, mrow(5), (p["g_post_ffn"], l), *nxt, "post_ffn" + tag, dep=tok)
    saved = dict(x=x, h1=h1, z=z, fcol=fcol, frow=frow, lse_fox=lse_fox, hc=hc, lse_swa=lse_swa, ys=ys, yn=yn, o=o, x1=x1,
                 h2=h2, u4=u4, cv4=cv4, act=act, f=f)
    return out, saved


def _tie(a, token):
    return a if token is None else a + token[0, 0]


def _layer_bwd(l, dx2, sv, mod, p, wg, emit, dep=None):
    s = dx2.shape[0]
    tm = min(TM, s)
    tag = f"_l{l}"
    mrow = lambda k: (mod, 6 * l + k)
    g = {}
    df, g["ga2"], g["g_post_ffn"] = _post_bwd(dx2, sv["f"], mrow(5), (p["g_post_ffn"], l), "post_ffn_bwd" + tag, dep=dep)
    dact = _matmul(df, wg["w_down"].get(None), "nt", (FF_NBLK, s, FF_BLK), BF16, (FF_NBLK, 1, 1),
                   _bs((s, D), lambda j, i, k: (0, 0)), _bs((FF_BLK, D), lambda j, i, k: (j, 0)),
                   _bs((None, s, FF_BLK), lambda j, i, k: (j, 0, 0)), None, "mm_dact" + tag)
    tok = emit("w_down", _matmul(sv["act"], df, "tn", (FF_NBLK * FF_BLK, D), BF16, (FF_NBLK, 1, 1),
                                 _bs((None, s, FF_BLK), lambda j, i, k: (j, 0, 0)), _bs((s, D), lambda j, i, k: (0, 0)),
                                 _bs((FF_BLK, D), lambda j, i, k: (j, 0)), None, "mm_dwdown" + tag))
    du, g["ffn_conv_w"], g["ffn_conv_b"], dh2 = _ffn_bwd(sv["u4"], sv["cv4"], dact, wg["ffn_conv_w"], wg["w_up"].get(None),
                                                         "ffn_bwd" + tag, dep=tok)
    du = du.reshape(N_DEV, s, FF_BLK)
    tok = emit("w_up", _matmul(du, sv["h2"], "tn", (N_DEV, FF_BLK, D), BF16, (N_DEV, 1, 1),
                               _bs((None, s, FF_BLK), lambda j, i, k: (j, 0, 0)), _bs((s, D), lambda j, i, k: (0, 0)),
                               _bs((None, FF_BLK, D), lambda j, i, k: (j, 0, 0)), None, "mm_dwup" + tag))
    dx1, g["sh2"], g["sc2"], g["g_pre_ffn"] = _pre_bwd(dh2, sv["x1"], dx2, (p["g_pre_ffn"], l), mrow(4), "pre_ffn_bwd" + tag,
                                                       dep=tok)
    do, g["ga1"], g["g_post_mix"] = _post_bwd(dx1, sv["o"], mrow(2), (p["g_post_mix"], l), "post_mix_bwd" + tag)
    dyn = _mm_rows(do, wg["w_out"].get(None), "nt", D, BF16, "mm_dyn" + tag)
    tok = emit("w_out", _mm_wgrad(sv["yn"], do, BF16, "mm_dwout" + tag))
    dy_fox, dy_conv, dy_swa, dy_sgu, g["g_group"] = _gnorm_bwd(dyn, sv["ys"], (p["g_group"], l), "gnorm_bwd" + tag, dep=tok)
    z = sv["z"]
    dq_f, dk_f, dv_f, dfrow, dfcol = _fox_bwd(z, sv["fcol"], sv["frow"], sv["lse_fox"], sv["ys"][0], dy_fox, "fox_bwd" + tag)
    dgate, g["bf"] = _fgate_bwd(z, p["bf"], dfrow, dfcol, "fgate_bwd" + tag)
    dhc, g["conv_pw_w"], g["conv_pw_b"], g["conv_ln_g"], g["conv_ln_b"], g["conv_w"], g["conv_b"] = _conv_bwd_a(
        z, sv["hc"], dy_conv, wg["conv_w"], p["conv_ln_g"], p["conv_ln_b"], wg["conv_pw_w"], "conv_bwd_a" + tag)
    da_c, dg_c = _conv_bwd_b(z, dhc, wg["conv_w"], "conv_bwd_b" + tag)
    dq_s, dk_s, dv_s, g["sinks"] = _swa_bwd(z, p["sinks"], sv["lse_swa"], sv["ys"][2], dy_swa, "swa_bwd" + tag)
    du_g, dv_g, g["wcat"], g["sgu_bcol"], g["sgu_ln_g"], g["sgu_ln_b"] = _sgu_bwd(
        z, dy_sgu, p["sgu_ln_g"], p["sgu_ln_b"], p["wcat"], p["bfull"], "sgu_bwd" + tag)
    dz = jnp.concatenate([dq_f.astype(BF16), dk_f.astype(BF16), dv_f.astype(BF16), da_c, dg_c, dq_s.astype(BF16), dk_s.astype(BF16),
                          dv_s.astype(BF16), du_g, dv_g, dgate], axis=1)
    tok = emit("w_in", _mm_wgrad(sv["h1"], dz, BF16, "mm_dwin" + tag))
    dh1 = _mm_rows(dz, wg["w_in"].get(None), "nt", D, BF16, "mm_dh1" + tag)
    dx, g["sh1"], g["sc1"], g["g_pre_mix"] = _pre_bwd(dh1, sv["x"], dx1, (p["g_pre_mix"], l), mrow(1), "pre_mix_bwd" + tag,
                                                      dep=tok)
    return dx, g


def _layer_params(l, small, conv_w_full, conv_pw_full, ffn_conv_w_full):
    bf = jnp.pad(small["b_fgate"][l][None, :], ((0, 0), (0, 124)))
    p = dict(
        bf=bf, conv_b=small["conv_b"][l][None], conv_ln_g=small["conv_ln_g"][l][None], conv_ln_b=small["conv_ln_b"][l][None],
        conv_pw_b=small["conv_pw_b"][l][None], sinks=small["swa_sinks"][l], sgu_ln_g=small["sgu_ln_g"][l][None],
        sgu_ln_b=small["sgu_ln_b"][l][None], wcat=_wcat(small["sgu_w"][l]),
        bfull=jnp.repeat(small["sgu_b"][l].T, HD, axis=1),
        ffn_conv_b=small["ffn_conv_b"][l].reshape(2, FF_NBLK, 1, FF_BLK),
        g_group=small["g_group"].reshape(N_LAYER, 1, D), g_post_mix=small["g_post_mix"].reshape(N_LAYER, 1, D),
        g_pre_ffn=small["g_pre_ffn"].reshape(N_LAYER, 1, D), g_post_ffn=small["g_post_ffn"].reshape(N_LAYER, 1, D),
        g_pre_mix=small["g_pre_mix"].reshape(N_LAYER, 1, D))
    wsmall = dict(conv_w=conv_w_full[l], conv_pw_w=conv_pw_full[l].astype(BF16),
                  ffn_conv_w=ffn_conv_w_full[l].reshape(3, 2, FF_NBLK, FF_BLK).transpose(1, 2, 0, 3))
    return p, wsmall


def _local_step(x, target, mod, small, wbig, conv_w_full, conv_pw_full, ffn_conv_w_full, emit, on_loss=None):
    ps, wgs = [], []
    for l in range(N_LAYER):
        p, wsmall = _layer_params(l, small, conv_w_full, conv_pw_full, ffn_conv_w_full)
        ps.append(p)
        wgs.append({**wbig[l], **wsmall})
    h = _rms_mod(x, (ps[0]["g_pre_mix"], 0), (mod, 1), (mod, 0), "rms_mod_l0")
    saved = []
    for l in range(N_LAYER):
        last = l == N_LAYER - 1
        nxt = None if last else ((ps[l]["g_pre_mix"], l + 1), (mod, 6 * (l + 1) + 1), (mod, 6 * (l + 1)))
        out, sv = _layer_fwd(l, x, h, mod, ps[l], wgs[l], last, target, nxt, None if last else wgs[l + 1]["w_in"])
        saved.append(sv)
        if not last:
            x, h = out
    dx, loss = out
    dep = None if on_loss is None else on_loss(loss)
    grads = [None] * N_LAYER
    for l in reversed(range(N_LAYER)):
        dx, grads[l] = _layer_bwd(l, dx, saved[l], mod, ps[l], wgs[l], functools.partial(emit, l), dep)
        dep = None
    return loss, dx, grads


_SMALL = ("b_ada", "g_pre_mix", "g_post_mix", "g_pre_ffn", "g_post_ffn", "b_fgate", "conv_b", "conv_ln_g", "conv_ln_b",
          "conv_pw_b", "swa_sinks", "sgu_ln_g", "sgu_ln_b", "sgu_w", "sgu_b", "g_group", "ffn_conv_b")
_WEIGHTS = ("w_ada", "b_ada", "g_pre_mix", "g_post_mix", "g_pre_ffn", "g_post_ffn", "w_in", "b_fgate", "conv_w", "conv_b",
            "conv_ln_g", "conv_ln_b", "conv_pw_w", "conv_pw_b", "swa_sinks", "sgu_ln_g", "sgu_ln_b", "sgu_w", "sgu_b",
            "g_group", "w_out", "ffn_w_up", "ffn_conv_w", "ffn_conv_b", "ffn_w_down")


def _pad_rows(a, mult):
    r = (-a.shape[0]) % mult
    return a if r == 0 else jnp.concatenate([a, jnp.zeros((r,) + a.shape[1:], a.dtype)], axis=0)


def _view2d(a):
    if a.ndim == 2:
        return a
    return a.reshape(-1, a.shape[-1])


def kernel(x, c, w_ada, b_ada, g_pre_mix, g_post_mix, g_pre_ffn, g_post_ffn, w_in, b_fgate, conv_w, conv_b, conv_ln_g, conv_ln_b, conv_pw_w, conv_pw_b, swa_sinks, sgu_ln_g, sgu_ln_b, sgu_w, sgu_b, g_group, w_out, ffn_w_up, ffn_conv_w, ffn_conv_b, ffn_w_down, loss_target, m_w_ada, m_b_ada, m_g_pre_mix, m_g_post_mix, m_g_pre_ffn, m_g_post_ffn, m_w_in, m_b_fgate, m_conv_w, m_conv_b, m_conv_ln_g, m_conv_ln_b, m_conv_pw_w, m_conv_pw_b, m_swa_sinks, m_sgu_ln_g, m_sgu_ln_b, m_sgu_w, m_sgu_b, m_g_group, m_w_out, m_ffn_w_up, m_ffn_conv_w, m_ffn_conv_b, m_ffn_w_down, v_w_ada, v_b_ada, v_g_pre_mix, v_g_post_mix, v_g_pre_ffn, v_g_post_ffn, v_w_in, v_b_fgate, v_conv_w, v_conv_b, v_conv_ln_g, v_conv_ln_b, v_conv_pw_w, v_conv_pw_b, v_swa_sinks, v_sgu_ln_g, v_sgu_ln_b, v_sgu_w, v_sgu_b, v_g_group, v_w_out, v_ffn_w_up, v_ffn_conv_w, v_ffn_conv_b, v_ffn_w_down):
    env = dict(locals())
    w = {n: env[n] for n in _WEIGHTS}
    mom = {n: env["m_" + n] for n in _WEIGHTS}
    var = {n: env["v_" + n] for n in _WEIGHTS}
    me = 4 * lax.axis_index("x") + 2 * lax.axis_index("y") + lax.axis_index("c")
    x2, target = x[0], loss_target[0]

    (c_all,) = _exchange([c], ["bcast"], "gather_c")
    c_all = c_all.reshape(N_DEV, D)
    (m_all,) = _exchange([_ada_fwd(c_all, w_ada)], ["bcast"], "gather_mod")
    m_mine = lax.dynamic_index_in_dim(m_all, me, axis=2, keepdims=False)
    mod, mod_token = _ada_finish(m_mine.transpose(1, 0, 2).reshape(N_LAYER, 6 * D), b_ada)
    mod = mod.reshape(6 * N_LAYER, 1, D)

    shards = [_tie(conv_w, mod_token), conv_pw_w, ffn_conv_w]
    for l in range(N_LAYER):
        shards += [_perm_in(w_in[l]).astype(BF16), w_out[l].astype(BF16), ffn_w_up[l].astype(BF16), ffn_w_down[l].astype(BF16)]
    gather = _g2_start(shards, "gather_weights_start")
    mod = _tie(mod, gather.token)
    _g2_relay(gather, [0, 1, 2, 3], mod, "gather_relay_first")
    g_cw, g_pw, g_fcw = _g2_wait(gather, [0, 1, 2], mod, "gather_small_wait")
    conv_w_full = g_cw.transpose(1, 2, 0, 3).reshape(N_LAYER, CONV_K, GW)
    conv_pw_full = g_pw.transpose(1, 0, 2, 3).reshape(N_LAYER, GW, GW)
    ffn_conv_w_full = g_fcw.transpose(1, 2, 0, 3).reshape(N_LAYER, 3, N_DEV * FF_BLK)

    def lazy(i, shape, key):
        pre = None if i == 3 else (lambda after: _g2_relay(gather, [i], after, "relay_" + key))
        return _Lazy(lambda after: _g2_wait(gather, [i], after, "wait_" + key)[0].reshape(shape), pre)

    wbig = [dict(w_in=lazy(3 + 4 * l, (D, ZW), f"w_in_l{l}"), w_out=lazy(4 + 4 * l, (D, D), f"w_out_l{l}"),
                 w_up=lazy(5 + 4 * l, (N_DEV, D, FF_BLK), f"w_up_l{l}"),
                 w_down=lazy(6 + 4 * l, (FF_NBLK * FF_BLK, D), f"w_down_l{l}")) for l in range(N_LAYER)]

    grad_flights = []

    def emit(l, key, arr):
        fl = _xchg_start([arr.reshape(N_DEV, -1, arr.shape[-1])], ["a2a"], f"grad_start_{key}_l{l}")
        grad_flights.append(((l, key), fl))
        return fl.token

    small = {n: w[n] for n in _SMALL}
    total = []

    def on_loss(loss8):
        total.append(lax.psum(loss8[0, 0], ("x", "y", "c")))
        return total[0].reshape(1, 1)

    _, dx, grads = _local_step(x2, target, mod, small, wbig, conv_w_full, conv_pw_full, ffn_conv_w_full, emit, on_loss)
    loss = total[0]
    grad_x = dx[None]


    st = lambda key: jnp.stack([grads[l][key] for l in range(N_LAYER)])
    d_conv_w = st("conv_w")[:, :CONV_K, :].reshape(N_LAYER, CONV_K, N_DEV, GW // N_DEV).transpose(2, 0, 1, 3)
    d_pw_w = st("conv_pw_w").reshape(N_LAYER, N_DEV, GW // N_DEV, GW).transpose(1, 0, 2, 3)
    d_fcw = st("ffn_conv_w").reshape(N_LAYER, N_DEV, 3, FF_BLK).transpose(1, 0, 2, 3)
    rows_d = _pad_rows(jnp.concatenate(
        [grads[l][k] for l in range(N_LAYER) for k in ("sh1", "sc1", "ga1", "sh2", "sc2", "ga2")]
        + [grads[l][k] for k in ("g_pre_mix", "g_post_mix", "g_pre_ffn", "g_post_ffn", "g_group") for l in range(N_LAYER)],
        axis=0), 8)
    rows_gw = _pad_rows(jnp.concatenate(
        [grads[l][k] for k in ("conv_b", "conv_ln_g", "conv_ln_b", "conv_pw_b", "sgu_ln_g", "sgu_ln_b") for l in range(N_LAYER)],
        axis=0), 8)
    rows_128 = jnp.concatenate([_pad_rows(jnp.concatenate([grads[l]["bf"] for l in range(N_LAYER)]
                                                          + [grads[l]["sinks"] for l in range(N_LAYER)], axis=0), 8)]
                               + [grads[l]["sgu_bcol"] for l in range(N_LAYER)], axis=0)
    rows_w = jnp.concatenate([grads[l]["wcat"] for l in range(N_LAYER)], axis=0)
    rows_fb = st("ffn_conv_b").reshape(N_LAYER * N_DEV, FF_BLK)
    small_flight = _xchg_start([d_conv_w, d_pw_w, d_fcw, rows_d, rows_gw, rows_128, rows_w, rows_fb],
                               ["a2a"] * 3 + ["bcast"] * 5, "small_grads_start")

    to_mem = {"w_in": lambda a: a.transpose(2, 0, 1), "ffn_w_up": lambda a: a.transpose(0, 2, 1)}
    from_mem = {"w_in": lambda a: a.transpose(1, 2, 0), "ffn_w_up": lambda a: a.transpose(0, 2, 1)}
    flights = dict(grad_flights)
    gr, delta, new_m, new_v = {}, {}, {}, {}

    def adamw_big(n):
        view, back = to_mem.get(n, lambda a: a), from_mem.get(n, lambda a: a)
        shape = view(w[n]).shape
        d, m2, v2 = _adamw(_view2d(view(w[n])), _view2d(gr[n]), _view2d(view(mom[n])), _view2d(view(var[n])), "adamw_" + n)
        delta[n], new_m[n], new_v[n] = back(d.reshape(shape)), back(m2.reshape(shape)), back(v2.reshape(shape))
        gr[n] = back(gr[n].reshape(shape))

    after = small_flight.token
    for n, key in (("ffn_w_down", "w_down"), ("ffn_w_up", "w_up"), ("w_out", "w_out"), ("w_in", "w_in")):
        parts = [_xchg_wait(flights[(l, key)], [0], after, f"grad_wait_{key}_l{l}")[0] for l in reversed(range(N_LAYER))]
        g = _sum8(parts[::-1], "sum_" + key)
        gr[n] = to_mem["w_in"](_unperm_in(g)) if n == "w_in" else g
        adamw_big(n)
        after = new_v[n]

    small_parts = _xchg_wait(small_flight, list(range(8)), after, "small_grads_wait")
    s_conv_w, s_pw_w, s_fcw, s_d, s_gw, s_128, s_w, s_fb = _sum8_small(
        [p.reshape(N_DEV, -1, p.shape[-1]) for p in small_parts], "sum_small_grads")
    gr["conv_w"] = s_conv_w.reshape(N_LAYER, CONV_K, GW // N_DEV)
    gr["conv_pw_w"] = s_pw_w.reshape(N_LAYER, GW // N_DEV, GW)
    gr["ffn_conv_w"] = s_fcw.reshape(N_LAYER, 3, FF_BLK)
    gr["b_ada"] = s_d[:6 * N_LAYER].reshape(N_LAYER, 6 * D)
    for i, k in enumerate(("g_pre_mix", "g_post_mix", "g_pre_ffn", "g_post_ffn", "g_group")):
        gr[k] = s_d[6 * N_LAYER + 2 * i:6 * N_LAYER + 2 * i + 2]
    for i, k in enumerate(("conv_b", "conv_ln_g", "conv_ln_b", "conv_pw_b", "sgu_ln_g", "sgu_ln_b")):
        gr[k] = s_gw[2 * i:2 * i + 2]
    gr["b_fgate"] = s_128[0:2, :4]
    gr["swa_sinks"] = s_128[2:4, :4]
    gr["sgu_b"] = s_128[8:].reshape(N_LAYER, WIN, 128)[:, :, :4].transpose(0, 2, 1)
    gr["sgu_w"] = s_w.reshape(N_LAYER, WIN, 4, WIN).transpose(0, 2, 1, 3)
    gr["ffn_conv_b"] = s_fb.reshape(N_LAYER, N_DEV * FF_BLK)
    dmod_all = small_parts[3][:, :6 * N_LAYER, :].reshape(N_DEV, N_LAYER, 6 * D)
    ncol = 6 * D // N_DEV
    dmod_cols = lax.dynamic_slice_in_dim(dmod_all, me * ncol, ncol, axis=2).transpose(1, 0, 2)
    gr["w_ada"] = _ada_bwd(c_all, dmod_cols)

    adamw_big("w_ada")
    smalls = [n for n in _WEIGHTS if n not in ("w_ada", "w_in", "w_out", "ffn_w_up", "ffn_w_down")]
    ds, ms, vs = _adamw_small([_view2d(w[n]) for n in smalls], [_view2d(gr[n]) for n in smalls],
                              [_view2d(mom[n]) for n in smalls], [_view2d(var[n]) for n in smalls], "adamw_small")
    for i, n in enumerate(smalls):
        delta[n], new_m[n], new_v[n] = ds[i].reshape(w[n].shape), ms[i].reshape(w[n].shape), vs[i].reshape(w[n].shape)

    return (loss, grad_x, *[gr[n].reshape(w[n].shape) for n in _WEIGHTS], *[delta[n] for n in _WEIGHTS],
            *[new_m[n] for n in _WEIGHTS], *[new_v[n] for n in _WEIGHTS])
```

```python
import functools

import jax
import jax.numpy as jnp
from jax import lax
from jax.experimental import pallas as pl
from jax.experimental.pallas import tpu as pltpu

F32, BF16 = jnp.float32, jnp.bfloat16
SDS = jax.ShapeDtypeStruct
MESH = pl.DeviceIdType.MESH

N_DEV = 8
D = 1024
GW = 256
HD = 64
N_LAYER = 2
ZW = 2432
FF_BLK = 704
FF_NBLK = 4
CONV_K = 31
CONV_HALO = 32
FFN_HALO = 16
FFN_SUB = 128
EPS = 1e-6
NEG = -1e30
SCALE = HD ** -0.5
VMEM_LIMIT_V7X = 56 * 1024 * 1024
TM = 512
WGRAD_ROWS = 256
TQ = 256
FOX_FWD_PAIR = 2
WIN = 128

ADAM_LR, ADAM_B1, ADAM_B2, ADAM_EPS, ADAM_WD, ADAM_STEP = 0.001, 0.9, 0.999, 1e-08, 0.01, 10

Z_FQ, Z_FK, Z_FV, Z_CA, Z_CG, Z_SQ = 0, 1, 2, 3, 4, 5
Z_SK, Z_SV = 12, 13
Z_GU, Z_GV = 7, 8
Z_FG = 18


def _cp(sem=None):
    return pltpu.CompilerParams(dimension_semantics=sem, vmem_limit_bytes=VMEM_LIMIT_V7X)


def _vec(arr3, idx, ngrid):
    w = arr3.shape[-1]
    if ngrid == 1:
        return pl.BlockSpec((None, 1, w), lambda i: (idx, 0, 0))
    return pl.BlockSpec((None, 1, w), lambda i, j: (idx, 0, 0))


def _sigmoid(x):
    return jax.nn.sigmoid(x)


def _silu(x):
    return x * _sigmoid(x)


def _dsilu(x):
    s = _sigmoid(x)
    return s * (1.0 + x * (1.0 - s))


_G0, _G1 = 0.7978845608028654, 0.044715


def _gelu(x):
    return 0.5 * x * (1.0 + jnp.tanh(_G0 * (x + _G1 * x * x * x)))


def _dgelu(x):
    t = jnp.tanh(_G0 * (x + _G1 * x * x * x))
    return 0.5 * (1.0 + t) + 0.5 * x * (1.0 - t * t) * (_G0 * (1.0 + 3.0 * _G1 * x * x))


def _rstd(x):
    return lax.rsqrt(jnp.mean(x * x, axis=-1, keepdims=True) + EPS)


def _rms_bwd(xh, r, t):
    return r * (t - xh * jnp.mean(t * xh, axis=-1, keepdims=True))


def _ln_stats(x):
    mu = jnp.mean(x, axis=-1, keepdims=True)
    xc = x - mu
    rstd = lax.rsqrt(jnp.mean(xc * xc, axis=-1, keepdims=True) + EPS)
    return xc * rstd, rstd


def _ln_bwd(xh, rstd, dxh):
    return rstd * (dxh - jnp.mean(dxh, axis=-1, keepdims=True) - xh * jnp.mean(dxh * xh, axis=-1, keepdims=True))


def _colsum(x):
    return jnp.sum(x, axis=0, keepdims=True)


def _dot(a, b, kind):
    dn = {"nn": (((1,), (0,)), ((), ())), "nt": (((1,), (1,)), ((), ())), "tn": (((0,), (0,)), ((), ()))}[kind]
    return lax.dot_general(a.astype(BF16), b.astype(BF16), dn, preferred_element_type=F32)


def _exchange(arrs, modes, name):
    n = len(arrs)
    outs = [SDS((N_DEV,) + a.shape, a.dtype) if m == "bcast" else SDS(a.shape, a.dtype) for a, m in zip(arrs, modes)]

    def body(*refs):
        ins, dst = refs[:n], refs[n:2 * n]
        send, recv, loc = refs[2 * n:]
        x, y, c = lax.axis_index("x"), lax.axis_index("y"), lax.axis_index("c")
        me = 4 * x + 2 * y + c

        def src(a, j):
            return ins[a] if modes[a] == "bcast" else ins[a].at[j]

        local = [pltpu.make_async_copy(src(a, me), dst[a].at[me], loc.at[a]) for a in range(n)]
        for cp in local:
            cp.start()
        sent, landed = [], []
        for k in (2, 4, 6, 3, 5, 7, 1):
            px = 1 - x if k & 4 else x
            py = 1 - y if k & 2 else y
            pc = 1 - c if k & 1 else c
            peer = 4 * px + 2 * py + pc
            for a in range(n):
                cp = pltpu.make_async_remote_copy(src_ref=src(a, peer), dst_ref=dst[a].at[me], send_sem=send.at[a, k - 1],
                                                  recv_sem=recv.at[a, k - 1], device_id=(px, py, pc), device_id_type=MESH)
                cp.start()
                sent.append(cp)
                landed.append(pltpu.make_async_remote_copy(src_ref=src(a, peer), dst_ref=dst[a].at[peer],
                                                           send_sem=send.at[a, k - 1], recv_sem=recv.at[a, k - 1],
                                                           device_id=(px, py, pc), device_id_type=MESH))
        for cp in landed:
            cp.wait_recv()
        for cp in sent:
            cp.wait_send()
        for cp in local:
            cp.wait()

    hbm = pl.BlockSpec(memory_space=pltpu.HBM)
    return pl.pallas_call(
        body, name=name, out_shape=outs, in_specs=[hbm] * n, out_specs=[hbm] * n,
        scratch_shapes=[pltpu.SemaphoreType.DMA((n, N_DEV - 1)), pltpu.SemaphoreType.DMA((n, N_DEV - 1)),
                        pltpu.SemaphoreType.DMA((n,))],
        compiler_params=pltpu.CompilerParams(has_side_effects=True),
    )(*arrs)


_PEER_ORDER = (2, 4, 6, 3, 5, 7, 1)
_HBM = pl.BlockSpec(memory_space=pltpu.HBM)
_SEM = pl.BlockSpec(memory_space=pltpu.SEMAPHORE)
_EFFECT = pltpu.SideEffectType.DATAFLOW_SIDE_EFFECTING


def _peer(k):
    x, y, c = lax.axis_index("x"), lax.axis_index("y"), lax.axis_index("c")
    px = 1 - x if k & 4 else x
    py = 1 - y if k & 2 else y
    pc = 1 - c if k & 1 else c
    return (px, py, pc), 4 * px + 2 * py + pc


def _my_id():
    return 4 * lax.axis_index("x") + 2 * lax.axis_index("y") + lax.axis_index("c")


def _split_copies(src_ref, land_ref, send, recv, loc, mode):
    me = _my_id()
    pick = (lambda j: src_ref) if mode == "bcast" else (lambda j: src_ref.at[j])
    local = pltpu.make_async_copy(pick(me), land_ref.at[me], loc)
    remote = []
    for k in _PEER_ORDER:
        dev, peer = _peer(k)
        out = pltpu.make_async_remote_copy(src_ref=pick(peer), dst_ref=land_ref.at[me], send_sem=send.at[k - 1],
                                           recv_sem=recv.at[k - 1], device_id=dev, device_id_type=MESH)
        arrive = pltpu.make_async_remote_copy(src_ref=pick(peer), dst_ref=land_ref.at[peer], send_sem=send.at[k - 1],
                                              recv_sem=recv.at[k - 1], device_id=dev, device_id_type=MESH)
        remote.append((out, arrive))
    return local, remote


class _Flight:
    def __init__(self, srcs, lands, sends, recvs, locs, modes, token):
        self.srcs, self.lands, self.sends, self.recvs, self.locs, self.modes, self.token = (
            list(srcs), list(lands), list(sends), list(recvs), list(locs), list(modes), token)


def _xchg_start(arrs, modes, name):
    n = len(arrs)
    lands = [lax.empty((N_DEV,) + a.shape if m == "bcast" else a.shape, a.dtype) for a, m in zip(arrs, modes)]

    def body(*refs):
        srcs, lnds = refs[:n], refs[n:2 * n]
        outs = refs[2 * n:]
        sends, recvs, locs, token = outs[:n], outs[n:2 * n], outs[2 * n:3 * n], outs[5 * n]
        for a in range(n):
            local, remote = _split_copies(srcs[a], lnds[a], sends[a], recvs[a], locs[a], modes[a])
            local.start()
            for out, _ in remote:
                out.start()
        token[...] = jnp.zeros_like(token)

    sem7 = pltpu.SemaphoreType.DMA((N_DEV - 1,))
    res = pl.pallas_call(
        body, name=name,
        out_shape=[sem7] * (2 * n) + [pltpu.SemaphoreType.DMA(())] * n + [pltpu.HBM(a.shape, a.dtype) for a in arrs]
        + [pltpu.HBM(b.shape, b.dtype) for b in lands] + [SDS((8, 128), F32)],
        in_specs=[_HBM] * (2 * n), out_specs=[_SEM] * (3 * n) + [_HBM] * (2 * n) + [pl.BlockSpec(memory_space=pltpu.VMEM)],
        input_output_aliases={i: 3 * n + i for i in range(2 * n)},
        compiler_params=pltpu.CompilerParams(has_side_effects=_EFFECT),
    )(*[pltpu.with_memory_space_constraint(a, pltpu.HBM) for a in arrs],
      *[pltpu.with_memory_space_constraint(b, pltpu.HBM) for b in lands])
    return _Flight(res[3 * n:4 * n], res[4 * n:5 * n], res[:n], res[n:2 * n], res[2 * n:3 * n], modes, res[5 * n])


def _xchg_wait(flight, idx, after, name):
    n = len(idx)
    modes = [flight.modes[i] for i in idx]

    def body(*refs):
        srcs, lnds = refs[:n], refs[n:2 * n]
        sends, recvs, locs = refs[2 * n:3 * n], refs[3 * n:4 * n], refs[4 * n:5 * n]
        for a in range(n):
            local, remote = _split_copies(srcs[a], lnds[a], sends[a], recvs[a], locs[a], modes[a])
            local.wait()
            for _, arrive in remote:
                arrive.wait_send()
                arrive.wait_recv()

    ops = ([flight.srcs[i] for i in idx] + [flight.lands[i] for i in idx] + [flight.sends[i] for i in idx]
           + [flight.recvs[i] for i in idx] + [flight.locs[i] for i in idx])
    res = pl.pallas_call(
        body, name=name, out_shape=[pltpu.HBM(o.shape, o.dtype) for o in ops[:2 * n]],
        in_specs=[_HBM] * (2 * n) + [_SEM] * (3 * n) + [pl.BlockSpec(memory_space=pl.ANY)], out_specs=[_HBM] * (2 * n),
        input_output_aliases={i: i for i in range(2 * n)},
        compiler_params=pltpu.CompilerParams(has_side_effects=_EFFECT),
    )(*ops, after)
    return res[n:]


class _Lazy:
    def __init__(self, fn, pre=None):
        self.fn, self.pre, self.val, self.started = fn, pre, None, False

    def prefetch(self, after):
        token = self.pre(after) if self.pre is not None and not self.started else None
        self.started = True
        return token

    def get(self, after):
        self.prefetch(after)
        if self.val is None:
            self.val = self.fn(after)
        return self.val


_CHIP_PEERS = (2, 4, 6)


def _g2_copies_a(src_ref, land_ref, send, recv, loc):
    me = _my_id()
    local = pltpu.make_async_copy(src_ref, land_ref.at[me], loc)
    remote = []
    for j, k in enumerate(_CHIP_PEERS + (1,)):
        dev, peer = _peer(k)
        out = pltpu.make_async_remote_copy(src_ref=src_ref, dst_ref=land_ref.at[me], send_sem=send.at[j], recv_sem=recv.at[j],
                                           device_id=dev, device_id_type=MESH)
        arrive = pltpu.make_async_remote_copy(src_ref=src_ref, dst_ref=land_ref.at[peer], send_sem=send.at[j],
                                              recv_sem=recv.at[j], device_id=dev, device_id_type=MESH)
        remote.append((out, arrive))
    return local, remote


def _g2_copies_b(land_ref, send, recv):
    sib, _ = _peer(1)
    pairs = []
    for j, k in enumerate(_CHIP_PEERS):
        _, same_core = _peer(k)
        _, other_core = _peer(k | 1)
        out = pltpu.make_async_remote_copy(src_ref=land_ref.at[same_core], dst_ref=land_ref.at[same_core], send_sem=send.at[j],
                                           recv_sem=recv.at[j], device_id=sib, device_id_type=MESH)
        arrive = pltpu.make_async_remote_copy(src_ref=land_ref.at[same_core], dst_ref=land_ref.at[other_core],
                                              send_sem=send.at[j], recv_sem=recv.at[j], device_id=sib, device_id_type=MESH)
        pairs.append((out, arrive))
    return pairs


class _Gather2:
    def __init__(self, srcs, lands, sends, recvs, locs, token):
        self.srcs, self.lands, self.sends, self.recvs, self.locs, self.token = (
            list(srcs), list(lands), list(sends), list(recvs), list(locs), token)
        self.sends_b, self.recvs_b = [None] * len(self.srcs), [None] * len(self.srcs)


def _g2_start(arrs, name):
    n = len(arrs)
    lands = [lax.empty((N_DEV,) + a.shape, a.dtype) for a in arrs]

    def body(*refs):
        srcs, lnds = refs[:n], refs[n:2 * n]
        outs = refs[2 * n:]
        sends, recvs, locs, token = outs[:n], outs[n:2 * n], outs[2 * n:3 * n], outs[5 * n]
        for a in range(n):
            local, remote = _g2_copies_a(srcs[a], lnds[a], sends[a], recvs[a], locs[a])
            local.start()
            for out, _ in remote:
                out.start()
        token[...] = jnp.zeros_like(token)

    sem4 = pltpu.SemaphoreType.DMA((4,))
    res = pl.pallas_call(
        body, name=name,
        out_shape=[sem4] * (2 * n) + [pltpu.SemaphoreType.DMA(())] * n + [pltpu.HBM(a.shape, a.dtype) for a in arrs]
        + [pltpu.HBM(b.shape, b.dtype) for b in lands] + [SDS((8, 128), F32)],
        in_specs=[_HBM] * (2 * n), out_specs=[_SEM] * (3 * n) + [_HBM] * (2 * n) + [pl.BlockSpec(memory_space=pltpu.VMEM)],
        input_output_aliases={i: 3 * n + i for i in range(2 * n)},
        compiler_params=pltpu.CompilerParams(has_side_effects=_EFFECT),
    )(*[pltpu.with_memory_space_constraint(a, pltpu.HBM) for a in arrs],
      *[pltpu.with_memory_space_constraint(b, pltpu.HBM) for b in lands])
    return _Gather2(res[3 * n:4 * n], res[4 * n:5 * n], res[:n], res[n:2 * n], res[2 * n:3 * n], res[5 * n])


def _g2_relay(g, idx, after, name):
    n = len(idx)

    def body(*refs):
        srcs, lnds = refs[:n], refs[n:2 * n]
        sends, recvs, locs = refs[2 * n:3 * n], refs[3 * n:4 * n], refs[4 * n:5 * n]
        outs = refs[5 * n + 1:]
        sends_b, recvs_b = outs[2 * n:3 * n], outs[3 * n:4 * n]
        for a in range(n):
            local, remote = _g2_copies_a(srcs[a], lnds[a], sends[a], recvs[a], locs[a])
            local.wait()
            for _, arrive in remote:
                arrive.wait_send()
                arrive.wait_recv()
        for a in range(n):
            for out, _ in _g2_copies_b(lnds[a], sends_b[a], recvs_b[a]):
                out.start()
        outs[4 * n][...] = jnp.zeros_like(outs[4 * n])

    ops = ([g.srcs[i] for i in idx] + [g.lands[i] for i in idx] + [g.sends[i] for i in idx] + [g.recvs[i] for i in idx]
           + [g.locs[i] for i in idx])
    sem3 = pltpu.SemaphoreType.DMA((3,))
    res = pl.pallas_call(
        body, name=name, out_shape=[pltpu.HBM(o.shape, o.dtype) for o in ops[:2 * n]] + [sem3] * (2 * n) + [SDS((8, 128), F32)],
        in_specs=[_HBM] * (2 * n) + [_SEM] * (3 * n) + [pl.BlockSpec(memory_space=pl.ANY)],
        out_specs=[_HBM] * (2 * n) + [_SEM] * (2 * n) + [pl.BlockSpec(memory_space=pltpu.VMEM)],
        input_output_aliases={i: i for i in range(2 * n)},
        compiler_params=pltpu.CompilerParams(has_side_effects=_EFFECT),
    )(*ops, after)
    for a, i in enumerate(idx):
        g.srcs[i], g.lands[i] = res[a], res[n + a]
        g.sends_b[i], g.recvs_b[i] = res[2 * n + a], res[3 * n + a]
    return res[4 * n]


def _g2_wait(g, idx, after, name):
    n = len(idx)

    def body(*refs):
        lnds, sends_b, recvs_b = refs[:n], refs[n:2 * n], refs[2 * n:3 * n]
        for a in range(n):
            for _, arrive in _g2_copies_b(lnds[a], sends_b[a], recvs_b[a]):
                arrive.wait_send()
                arrive.wait_recv()

    ops = [g.lands[i] for i in idx] + [g.sends_b[i] for i in idx] + [g.recvs_b[i] for i in idx]
    res = pl.pallas_call(
        body, name=name, out_shape=[pltpu.HBM(o.shape, o.dtype) for o in ops[:n]],
        in_specs=[_HBM] * n + [_SEM] * (2 * n) + [pl.BlockSpec(memory_space=pl.ANY)], out_specs=[_HBM] * n,
        input_output_aliases={i: i for i in range(n)},
        compiler_params=pltpu.CompilerParams(has_side_effects=_EFFECT),
    )(*ops, after)
    return list(res)


def _matmul(a, b, kind, out_shape, out_dtype, grid, a_spec, b_spec, o_spec, acc_shape, name):
    nk = grid[2]

    def body(a_ref, b_ref, o_ref, *scratch):
        prod = _dot(a_ref[...], b_ref[...], kind)
        if nk == 1:
            o_ref[...] = prod.astype(out_dtype)
        else:
            acc = scratch[0]
            k = pl.program_id(2)

            @pl.when(k == 0)
            def _():
                acc[...] = prod

            @pl.when(k > 0)
            def _():
                acc[...] += prod

            @pl.when(k == nk - 1)
            def _():
                o_ref[...] = acc[...].astype(out_dtype)

    return pl.pallas_call(
        body, name=name, grid=grid, in_specs=[a_spec, b_spec], out_specs=o_spec, out_shape=SDS(out_shape, out_dtype),
        scratch_shapes=[] if nk == 1 else [pltpu.VMEM(acc_shape, F32)],
        compiler_params=_cp(("parallel", "parallel", "arbitrary")))(a, b)


def _bs(shape, fn):
    return pl.BlockSpec(shape, fn)


def _mm_rows(a, w, kind, n_out, out_dtype, name):
    s, k = a.shape
    tm = min(TM, s)
    return _matmul(a, w, kind, (s, n_out), out_dtype, (s // tm, 1, 1),
                   _bs((tm, k), lambda i, j, kk: (i, 0)), _bs(w.shape, lambda i, j, kk: (0, 0)),
                   _bs((tm, n_out), lambda i, j, kk: (i, 0)), None, name)


def _mm_wgrad(a, dy, out_dtype, name):
    s, k = a.shape
    n = dy.shape[1]
    tko = min(WGRAD_ROWS, k)
    return _matmul(a, dy, "tn", (k, n), out_dtype, (k // tko, 1, 1),
                   _bs((s, tko), lambda i, j, kk: (0, i)), _bs((s, n), lambda i, j, kk: (0, 0)),
                   _bs((tko, n), lambda i, j, kk: (i, 0)), None, name)


def _ada_fwd(c_all, w_ada):
    ncol = w_ada.shape[2]

    def body(c_ref, w_ref, o_ref):
        ca = _silu(c_ref[...])
        ca = jnp.concatenate([ca, jnp.zeros_like(ca)], axis=0)
        o_ref[...] = _dot(ca, w_ref[...], "nn")[:N_DEV, :]

    return pl.pallas_call(
        body, name="ada_fwd", grid=(N_LAYER,),
        in_specs=[pl.BlockSpec((N_DEV, D), lambda l: (0, 0)), pl.BlockSpec((None, D, ncol), lambda l: (l, 0, 0))],
        out_specs=pl.BlockSpec((None, N_DEV, ncol), lambda l: (l, 0, 0)),
        out_shape=SDS((N_LAYER, N_DEV, ncol), F32), compiler_params=_cp(("parallel",)))(c_all, w_ada)


def _ada_finish(m_mine, b_ada):
    def body(m_ref, b_ref, o_ref, t_ref):
        o_ref[...] = m_ref[...] + b_ref[...]
        t_ref[...] = jnp.zeros_like(t_ref)

    return pl.pallas_call(body, name="ada_finish", out_shape=[SDS(b_ada.shape, F32), SDS((8, 128), F32)])(m_mine, b_ada)


def _ada_bwd(c_all, dmod_cols):
    ncol = dmod_cols.shape[2]

    def body(c_ref, d_ref, o_ref):
        ca = _silu(c_ref[...])
        ca = jnp.concatenate([ca, jnp.zeros_like(ca)], axis=0)
        dm = d_ref[...]
        dm = jnp.concatenate([dm, jnp.zeros_like(dm)], axis=0)
        o_ref[...] = _dot(ca, dm, "tn")

    return pl.pallas_call(
        body, name="ada_bwd", grid=(N_LAYER,),
        in_specs=[pl.BlockSpec((N_DEV, D), lambda l: (0, 0)), pl.BlockSpec((None, N_DEV, ncol), lambda l: (l, 0, 0))],
        out_specs=pl.BlockSpec((None, D, ncol), lambda l: (l, 0, 0)),
        out_shape=SDS((N_LAYER, D, ncol), F32), compiler_params=_cp(("parallel",)))(c_all, dmod_cols)


def _rows(s):
    tm = min(TM, s)
    return tm, pl.BlockSpec((tm, D), lambda i: (i, 0))


def _pcall(body, operands, dep, **kw):
    if dep is None:
        return pl.pallas_call(body, **kw)(*operands)
    n = len(operands)

    def body_dep(*refs):
        body(*refs[:n], *refs[n + 1:])

    kw["in_specs"] = list(kw["in_specs"]) + [pl.BlockSpec(memory_space=pl.ANY)]
    return pl.pallas_call(body_dep, **kw)(*operands, dep)


def _rms_mod(x, g, sc, sh, name):
    s = x.shape[0]
    tm, row = _rows(s)

    def body(x_ref, g_ref, sc_ref, sh_ref, h_ref):
        xf = x_ref[...]
        h_ref[...] = (xf * _rstd(xf) * (g_ref[...] * (1.0 + sc_ref[...])) + sh_ref[...]).astype(BF16)

    return pl.pallas_call(
        body, name=name, grid=(s // tm,), in_specs=[row, _vec(*g, 1), _vec(*sc, 1), _vec(*sh, 1)], out_specs=row,
        out_shape=SDS((s, D), BF16), compiler_params=_cp(("parallel",)))(x, g[0], sc[0], sh[0])


def _post(xres, o, ga, gpost, gn, scn, shn, name, dep=None):
    s = xres.shape[0]
    tm, row = _rows(s)

    def body(x_ref, o_ref, ga_ref, gp_ref, gn_ref, sc_ref, sh_ref, xn_ref, h_ref):
        of = o_ref[...].astype(F32)
        xn = x_ref[...] + ga_ref[...] * (of * _rstd(of) * gp_ref[...])
        xn_ref[...] = xn
        h_ref[...] = (xn * _rstd(xn) * (gn_ref[...] * (1.0 + sc_ref[...])) + sh_ref[...]).astype(BF16)

    return _pcall(
        body, (xres, o, ga[0], gpost[0], gn[0], scn[0], shn[0]), dep, name=name, grid=(s // tm,),
        in_specs=[row, row, _vec(*ga, 1), _vec(*gpost, 1), _vec(*gn, 1), _vec(*scn, 1), _vec(*shn, 1)],
        out_specs=[row, row], out_shape=[SDS((s, D), F32), SDS((s, D), BF16)], compiler_params=_cp(("parallel",)))


def _post_loss(xres, o, ga, gpost, target, name, dep=None):
    s = xres.shape[0]
    tm, row = _rows(s)

    def body(x_ref, o_ref, ga_ref, gp_ref, t_ref, dy_ref, loss_ref):
        of = o_ref[...].astype(F32)
        err = x_ref[...] + ga_ref[...] * (of * _rstd(of) * gp_ref[...]) - t_ref[...]
        dy_ref[...] = err * (1.0 / D)

        @pl.when(pl.program_id(0) == 0)
        def _():
            loss_ref[...] = jnp.zeros_like(loss_ref)

        loss_ref[...] += jnp.sum(jnp.mean(err * err, axis=-1, keepdims=True), axis=0, keepdims=True) * 0.5

    return _pcall(
        body, (xres, o, ga[0], gpost[0], target), dep, name=name, grid=(s // tm,),
        in_specs=[row, row, _vec(*ga, 1), _vec(*gpost, 1), row],
        out_specs=[row, pl.BlockSpec((8, 128), lambda i: (0, 0))], out_shape=[SDS((s, D), F32), SDS((8, 128), F32)],
        compiler_params=_cp(("arbitrary",)))


def _acc(ref, val, first):
    @pl.when(first)
    def _():
        ref[...] = val

    @pl.when(jnp.logical_not(first))
    def _():
        ref[...] += val


def _post_bwd(dxn, o, ga, gpost, name, dep=None):
    s = dxn.shape[0]
    tm, row = _rows(s)
    vec = pl.BlockSpec((1, D), lambda i: (0, 0))

    def body(d_ref, o_ref, ga_ref, gp_ref, do_ref, dga_ref, dgp_ref):
        of, dx = o_ref[...].astype(F32), d_ref[...]
        r = _rstd(of)
        oh = of * r
        do_ref[...] = _rms_bwd(oh, r, dx * (ga_ref[...] * gp_ref[...])).astype(BF16)
        cs = _colsum(dx * oh)
        first = pl.program_id(0) == 0
        _acc(dga_ref, cs * gp_ref[...], first)
        _acc(dgp_ref, cs * ga_ref[...], first)

    return _pcall(
        body, (dxn, o, ga[0], gpost[0]), dep, name=name, grid=(s // tm,),
        in_specs=[row, row, _vec(*ga, 1), _vec(*gpost, 1)], out_specs=[row, vec, vec],
        out_shape=[SDS((s, D), BF16), SDS((1, D), F32), SDS((1, D), F32)], compiler_params=_cp(("arbitrary",)))


def _pre_bwd(dh, x, dres, g, sc, name, dep=None):
    s = x.shape[0]
    tm, row = _rows(s)
    vec = pl.BlockSpec((1, D), lambda i: (0, 0))

    def body(dh_ref, x_ref, dr_ref, g_ref, sc_ref, dx_ref, dsh_ref, dsc_ref, dg_ref):
        xf, d = x_ref[...], dh_ref[...].astype(F32)
        r = _rstd(xf)
        xh = xf * r
        dx_ref[...] = dr_ref[...] + _rms_bwd(xh, r, d * (g_ref[...] * (1.0 + sc_ref[...])))
        cs = _colsum(d * xh)
        first = pl.program_id(0) == 0
        _acc(dsh_ref, _colsum(d), first)
        _acc(dsc_ref, cs * g_ref[...], first)
        _acc(dg_ref, cs * (1.0 + sc_ref[...]), first)

    return _pcall(
        body, (dh, x, dres, g[0], sc[0]), dep, name=name, grid=(s // tm,),
        in_specs=[row, row, row, _vec(*g, 1), _vec(*sc, 1)], out_specs=[row, vec, vec, vec],
        out_shape=[SDS((s, D), F32), SDS((1, D), F32), SDS((1, D), F32), SDS((1, D), F32)],
        compiler_params=_cp(("arbitrary",)))


def _gnorm(ys, gg, name):
    s = ys[0].shape[0]
    tm = min(TM, s)
    yb = pl.BlockSpec((tm, GW), lambda i: (i, 0))

    def body(y0, y1, y2, y3, g_ref, o_ref):
        for i, yr in enumerate((y0, y1, y2, y3)):
            y = yr[...]
            o_ref[:, GW * i:GW * (i + 1)] = (y * _rstd(y) * g_ref[:, GW * i:GW * (i + 1)]).astype(BF16)

    return pl.pallas_call(
        body, name=name, grid=(s // tm,), in_specs=[yb] * 4 + [_vec(*gg, 1)], out_specs=pl.BlockSpec((tm, D), lambda i: (i, 0)),
        out_shape=SDS((s, D), BF16), compiler_params=_cp(("parallel",)))(*ys, gg[0])


def _gnorm_bwd(dyn, ys, gg, name, dep=None):
    s = ys[0].shape[0]
    tm = min(TM, s)
    yb = pl.BlockSpec((tm, GW), lambda i: (i, 0))

    def body(d_ref, y0, y1, y2, y3, g_ref, o0, o1, o2, o3, dg_ref):
        first = pl.program_id(0) == 0
        for i, (yr, orf) in enumerate(zip((y0, y1, y2, y3), (o0, o1, o2, o3))):
            y = yr[...]
            d = d_ref[:, GW * i:GW * (i + 1)].astype(F32)
            r = _rstd(y)
            yh = y * r
            orf[...] = _rms_bwd(yh, r, d * g_ref[:, GW * i:GW * (i + 1)]).astype(BF16)
            cs = _colsum(d * yh)

            @pl.when(first)
            def _():
                dg_ref[:, GW * i:GW * (i + 1)] = cs

            @pl.when(jnp.logical_not(first))
            def _():
                dg_ref[:, GW * i:GW * (i + 1)] += cs

    return _pcall(
        body, (dyn, *ys, gg[0]), dep, name=name, grid=(s // tm,),
        in_specs=[pl.BlockSpec((tm, D), lambda i: (i, 0))] + [yb] * 4 + [_vec(*gg, 1)],
        out_specs=[yb] * 4 + [pl.BlockSpec((1, D), lambda i: (0, 0))],
        out_shape=[SDS((s, GW), BF16)] * 4 + [SDS((1, D), F32)], compiler_params=_cp(("arbitrary",)))


def _lane_put(acc, col, h):
    lane = lax.broadcasted_iota(jnp.int32, acc.shape, 1)
    return jnp.where(lane == h, col, acc)


def _fgate(z, bf, name, dep=None):
    s = z.shape[0]

    def body(z_ref, b_ref, fc_ref, fr_ref):
        xg = z_ref[...].astype(F32) + b_ref[...]
        lf = jnp.minimum(xg, 0.0) - jnp.log(1.0 + jnp.exp(-jnp.abs(xg)))
        lane = lax.broadcasted_iota(jnp.int32, lf.shape, 1)
        row = lax.broadcasted_iota(jnp.int32, lf.shape, 0)
        f = jnp.where(lane < 4, lf, 0.0)
        sh = 1
        while sh < s:
            f = f + jnp.where(row >= sh, pltpu.roll(f, sh, 0), 0.0)
            sh *= 2
        fc_ref[...] = f
        fr_ref[...] = f.T[:8, :]

    return _pcall(
        body, (z, bf), dep, name=name, grid=(1,),
        in_specs=[pl.BlockSpec((s, 128), lambda i: (0, Z_FG)), pl.BlockSpec((1, 128), lambda i: (0, 0))],
        out_specs=[pl.BlockSpec((s, 128), lambda i: (0, 0)), pl.BlockSpec((8, s), lambda i: (0, 0))],
        out_shape=[SDS((s, 128), F32), SDS((8, s), F32)], compiler_params=_cp(("arbitrary",)))


def _fgate_bwd(z, bf, dfrow, dfcol, name):
    s = z.shape[0]

    def body(z_ref, b_ref, d_ref, dc_ref, dz_ref, db_ref):
        d = jnp.concatenate([d_ref[...], jnp.zeros((120, s), F32)], axis=0).T + dc_ref[...]
        row = lax.broadcasted_iota(jnp.int32, d.shape, 0)
        lane = lax.broadcasted_iota(jnp.int32, d.shape, 1)
        sh = 1
        while sh < s:
            d = d + jnp.where(row < s - sh, pltpu.roll(d, s - sh, 0), 0.0)
            sh *= 2
        xg = z_ref[...].astype(F32) + b_ref[...]
        dz = jnp.where(lane < 4, d * _sigmoid(-xg), 0.0)
        dz_ref[...] = dz.astype(BF16)
        db_ref[...] = _colsum(dz)

    return pl.pallas_call(
        body, name=name, grid=(1,),
        in_specs=[pl.BlockSpec((s, 128), lambda i: (0, Z_FG)), pl.BlockSpec((1, 128), lambda i: (0, 0)),
                  pl.BlockSpec((8, s), lambda i: (0, 0)), pl.BlockSpec((s, 128), lambda i: (0, 0))],
        out_specs=[pl.BlockSpec((s, 128), lambda i: (0, 0)), pl.BlockSpec((1, 128), lambda i: (0, 0))],
        out_shape=[SDS((s, 128), BF16), SDS((1, 128), F32)], compiler_params=_cp(("arbitrary",)))(z, bf, dfrow, dfcol)


def _fox_scores(q_ref, k_ref, fc_ref, fr_ref, h, i, nk, tq):
    kw = nk * tq
    q = q_ref[:, HD * h:HD * (h + 1)] * SCALE
    sc = _dot(q, k_ref[0:kw, HD * h:HD * (h + 1)], "nt") + fc_ref[:, h:h + 1] - fr_ref[h:h + 1, 0:kw]
    qpos = i * tq + lax.broadcasted_iota(jnp.int32, (tq, kw), 0)
    kpos = lax.broadcasted_iota(jnp.int32, (tq, kw), 1)
    return q, jnp.where(kpos <= qpos, sc, NEG)


def _fox_fwd(z, fcol, frow, name):
    s = z.shape[0]
    tq = min(TQ, s)
    nc = s // tq

    def body(q_ref, k_ref, v_ref, fc_ref, fr_ref, y_ref, l_ref):
        i = pl.program_id(0)
        for n in range(0, nc, FOX_FWD_PAIR):
            nk = min(n + FOX_FWD_PAIR, nc)

            @pl.when((i >= n) & (i < n + FOX_FWD_PAIR))
            def _():
                kw = nk * tq
                lse = jnp.zeros((tq, 128), F32)
                for h in range(4):
                    _, sc = _fox_scores(q_ref, k_ref, fc_ref, fr_ref, h, i, nk, tq)
                    m = jnp.max(sc, axis=-1, keepdims=True)
                    p = jnp.exp(sc - m)
                    l = jnp.sum(p, axis=-1, keepdims=True)
                    y_ref[:, HD * h:HD * (h + 1)] = _dot(p, v_ref[0:kw, HD * h:HD * (h + 1)], "nn") / l
                    lse = _lane_put(lse, m + jnp.log(l), h)
                l_ref[...] = lse

    return pl.pallas_call(
        body, name=name, grid=(nc,),
        in_specs=[pl.BlockSpec((tq, GW), lambda i: (i, Z_FQ)), pl.BlockSpec((s, GW), lambda i: (0, Z_FK)),
                  pl.BlockSpec((s, GW), lambda i: (0, Z_FV)), pl.BlockSpec((tq, 128), lambda i: (i, 0)),
                  pl.BlockSpec((8, s), lambda i: (0, 0))],
        out_specs=[pl.BlockSpec((tq, GW), lambda i: (i, 0)), pl.BlockSpec((tq, 128), lambda i: (i, 0))],
        out_shape=[SDS((s, GW), F32), SDS((s, 128), F32)], compiler_params=_cp(("parallel",)))(z, z, z, fcol, frow)


def _fox_bwd(z, fcol, frow, lse, y, dy, name):
    s = z.shape[0]
    tq = min(TQ, s)
    nc = s // tq
    half = max(nc // 2, 1)

    def body(q_ref, k_ref, v_ref, fc_ref, fr_ref, l_ref, y_ref, dy_ref, dq_ref, dk_ref, dv_ref, df_ref, dfq_ref):
        @pl.when(pl.program_id(0) == 0)
        def _():
            dk_ref[...] = jnp.zeros_like(dk_ref)
            dv_ref[...] = jnp.zeros_like(dv_ref)
            df_ref[...] = jnp.zeros_like(df_ref)

        i = pl.program_id(0)
        for cond, nk in ((i < half, half), (i >= half, nc)):
            @pl.when(cond)
            def _():
                kw = nk * tq
                dfq = jnp.zeros((tq, 128), F32)
                dyf = dy_ref[...].astype(F32)
                for h in range(4):
                    hs = slice(HD * h, HD * (h + 1))
                    q, sc = _fox_scores(q_ref, k_ref, fc_ref, fr_ref, h, i, nk, tq)
                    p = jnp.exp(sc - l_ref[:, h:h + 1])
                    dyh = dyf[:, hs]
                    dd = jnp.sum(dyh * y_ref[:, hs], axis=-1, keepdims=True)
                    ds = p * (_dot(dyh, v_ref[0:kw, hs], "nt") - dd)
                    dq_ref[:, hs] = _dot(ds, k_ref[0:kw, hs], "nn") * SCALE
                    dk_ref[0:kw, hs] += _dot(ds, q, "tn")
                    dv_ref[0:kw, hs] += _dot(p, dyh, "tn")
                    df_ref[h:h + 1, 0:kw] -= _colsum(ds)
                    dfq = _lane_put(dfq, jnp.sum(ds, axis=-1, keepdims=True), h)
                dfq_ref[...] = dfq

    tile = lambda w: pl.BlockSpec((tq, w), lambda i: (i, 0))
    full = pl.BlockSpec((s, GW), lambda i: (0, 0))
    rows8 = pl.BlockSpec((8, s), lambda i: (0, 0))
    return pl.pallas_call(
        body, name=name, grid=(nc,),
        in_specs=[pl.BlockSpec((tq, GW), lambda i: (i, Z_FQ)), pl.BlockSpec((s, GW), lambda i: (0, Z_FK)),
                  pl.BlockSpec((s, GW), lambda i: (0, Z_FV)), tile(128), rows8, tile(128), tile(GW), tile(GW)],
        out_specs=[tile(GW), full, full, rows8, tile(128)],
        out_shape=[SDS((s, GW), F32), SDS((s, GW), F32), SDS((s, GW), F32), SDS((8, s), F32), SDS((s, 128), F32)],
        compiler_params=_cp(("arbitrary",)))(z, z, z, fcol, frow, lse, y, dy)


def _swa_block(q_ref, k_ref, v_ref, n):
    qs = pl.multiple_of(n * WIN, WIN)
    ks = pl.multiple_of(jnp.maximum(n - 1, 0) * WIN, WIN)
    qb = q_ref[pl.ds(qs, WIN), :]
    kb = k_ref[pl.ds(ks, 2 * WIN), :]
    vb = v_ref[pl.ds(ks, 2 * WIN), :]
    rows = lax.broadcasted_iota(jnp.int32, (2 * WIN, 2 * WIN), 0) & (WIN - 1)
    dist = (qs + rows) - (ks + lax.broadcasted_iota(jnp.int32, (2 * WIN, 2 * WIN), 1))
    return qs, ks, qb, kb, vb, (dist >= 0) & (dist < WIN)


def _stack2(x, kvh):
    return jnp.concatenate([x[:, HD * (2 * kvh):HD * (2 * kvh + 1)], x[:, HD * (2 * kvh + 1):HD * (2 * kvh + 2)]], axis=0)


def _sink2(sink_ref, kvh):
    top = lax.broadcasted_iota(jnp.int32, (2 * WIN, 1), 0) < WIN
    return jnp.where(top, sink_ref[2 * kvh], sink_ref[2 * kvh + 1])


def _swa_fwd(z, sinks, name):
    s = z.shape[0]

    def body(sink_ref, q_ref, k_ref, v_ref, y_ref, l_ref):
        def step(n, carry):
            qs, ks, qb, kb, vb, valid = _swa_block(q_ref, k_ref, v_ref, n)
            lse = jnp.zeros((WIN, 128), F32)
            for kvh in range(2):
                kv = slice(HD * kvh, HD * (kvh + 1))
                sc = jnp.where(valid, _dot(_stack2(qb, kvh) * SCALE, kb[:, kv], "nt"), NEG)
                sink = _sink2(sink_ref, kvh)
                m = jnp.maximum(jnp.max(sc, axis=-1, keepdims=True), sink)
                p = jnp.exp(sc - m)
                den = jnp.sum(p, axis=-1, keepdims=True) + jnp.exp(sink - m)
                o = _dot(p, vb[:, kv], "nn") / den
                lrow = m + jnp.log(den)
                for j in range(2):
                    h = 2 * kvh + j
                    y_ref[pl.ds(qs, WIN), HD * h:HD * (h + 1)] = o[WIN * j:WIN * (j + 1), :]
                    lse = _lane_put(lse, lrow[WIN * j:WIN * (j + 1), :], h)
            l_ref[pl.ds(qs, WIN), :] = lse
            return carry

        lax.fori_loop(0, s // WIN, step, 0, unroll=2)

    return pl.pallas_call(
        body, name=name, grid=(1,),
        in_specs=[pl.BlockSpec(memory_space=pltpu.SMEM), pl.BlockSpec((s, GW), lambda i: (0, Z_SQ)),
                  pl.BlockSpec((s, 128), lambda i: (0, Z_SK)), pl.BlockSpec((s, 128), lambda i: (0, Z_SV))],
        out_specs=[pl.BlockSpec((s, GW), lambda i: (0, 0)), pl.BlockSpec((s, 128), lambda i: (0, 0))],
        out_shape=[SDS((s, GW), F32), SDS((s, 128), F32)], compiler_params=_cp(("arbitrary",)))(sinks, z, z, z)


def _swa_bwd(z, sinks, lse, y, dy, name):
    s = z.shape[0]

    def body(sink_ref, q_ref, k_ref, v_ref, l_ref, y_ref, dy_ref, dq_ref, dk_ref, dv_ref, dsink_ref):
        dk_ref[...] = jnp.zeros_like(dk_ref)
        dv_ref[...] = jnp.zeros_like(dv_ref)
        dsink_ref[...] = jnp.zeros_like(dsink_ref)

        def step(n, carry):
            qs, ks, qb, kb, vb, valid = _swa_block(q_ref, k_ref, v_ref, n)
            lse_b = l_ref[pl.ds(qs, WIN), :]
            yb = y_ref[pl.ds(qs, WIN), :]
            dyb = dy_ref[pl.ds(qs, WIN), :].astype(F32)
            dsink = jnp.zeros((1, 128), F32)
            for kvh in range(2):
                kv = slice(HD * kvh, HD * (kvh + 1))
                q = _stack2(qb, kvh) * SCALE
                dy2 = _stack2(dyb, kvh)
                sc = jnp.where(valid, _dot(q, kb[:, kv], "nt"), NEG)
                lh = jnp.concatenate([lse_b[:, 2 * kvh:2 * kvh + 1], lse_b[:, 2 * kvh + 1:2 * kvh + 2]], axis=0)
                p = jnp.exp(sc - lh)
                dd = jnp.sum(dy2 * _stack2(yb, kvh), axis=-1, keepdims=True)
                ds = p * (_dot(dy2, vb[:, kv], "nt") - dd)
                dq = _dot(ds, kb[:, kv], "nn") * SCALE
                dk_ref[pl.ds(ks, 2 * WIN), kv] += _dot(ds, q, "tn")
                dv_ref[pl.ds(ks, 2 * WIN), kv] += _dot(p, dy2, "tn")
                dsk = jnp.exp(_sink2(sink_ref, kvh) - lh) * dd
                for j in range(2):
                    h = 2 * kvh + j
                    dq_ref[pl.ds(qs, WIN), HD * h:HD * (h + 1)] = dq[WIN * j:WIN * (j + 1), :]
                    dsink = _lane_put(dsink, dsink[:, h:h + 1] - jnp.sum(dsk[WIN * j:WIN * (j + 1), :], axis=0, keepdims=True), h)
            dsink_ref[...] += dsink
            return carry

        lax.fori_loop(0, s // WIN, step, 0, unroll=2)

    full = lambda w: pl.BlockSpec((s, w), lambda i: (0, 0))
    return pl.pallas_call(
        body, name=name, grid=(1,),
        in_specs=[pl.BlockSpec(memory_space=pltpu.SMEM), pl.BlockSpec((s, GW), lambda i: (0, Z_SQ)),
                  pl.BlockSpec((s, 128), lambda i: (0, Z_SK)), pl.BlockSpec((s, 128), lambda i: (0, Z_SV)),
                  full(128), full(GW), full(GW)],
        out_specs=[full(GW), full(128), full(128), pl.BlockSpec((1, 128), lambda i: (0, 0))],
        out_shape=[SDS((s, GW), F32), SDS((s, 128), F32), SDS((s, 128), F32), SDS((1, 128), F32)],
        compiler_params=_cp(("arbitrary",)))(sinks, z, z, z, lse, y, dy)


_SUBLANES = 8


def _rotations(win, advance=False):
    n = win.shape[0]
    return [win] + [pltpu.roll(win, (n - b) if advance else b, 0) for b in range(1, _SUBLANES)]


def _delayed(rots, shift, halo, tm):
    a, b = divmod(shift, _SUBLANES)
    return rots[b][halo - _SUBLANES * a:halo - _SUBLANES * a + tm, :]


def _advanced(rots, shift, tm):
    a, b = divmod(shift, _SUBLANES)
    return rots[b][_SUBLANES * a:_SUBLANES * a + tm, :]


def _prev_halo(width, halo, tm, col):
    return pl.BlockSpec((halo, width), lambda i: (jnp.maximum(i * (tm // halo) - 1, 0), col))


def _glu_window(a_ref, g_ref, ah_ref, gh_ref):
    keep = (pl.program_id(0) > 0).astype(F32)
    a = jnp.concatenate([ah_ref[...].astype(F32) * keep, a_ref[...].astype(F32)], axis=0)
    g = jnp.concatenate([gh_ref[...].astype(F32), g_ref[...].astype(F32)], axis=0)
    return a * _sigmoid(g)


def _conv_fwd(z, cw, cb, lg, lb, pw, pb, name):
    s = z.shape[0]
    tm = min(TM, s)

    def body(a_ref, g_ref, ah_ref, gh_ref, w_ref, b_ref, lg_ref, lb_ref, pw_ref, pb_ref, y_ref, hc_ref):
        rots = _rotations(_glu_window(a_ref, g_ref, ah_ref, gh_ref))
        hc = jnp.zeros((tm, GW), F32) + b_ref[...]
        for k in range(CONV_K):
            hc = hc + w_ref[k:k + 1, :] * _delayed(rots, CONV_K - 1 - k, CONV_HALO, tm)
        hc_ref[...] = hc
        xh, _ = _ln_stats(hc)
        y_ref[...] = _dot(_silu(xh * lg_ref[...] + lb_ref[...]), pw_ref[...], "nn") + pb_ref[...]

    tile = lambda col: pl.BlockSpec((tm, GW), lambda i: (i, col))
    whole = lambda a: pl.BlockSpec(a.shape, lambda i: (0, 0))
    return pl.pallas_call(
        body, name=name, grid=(s // tm,),
        in_specs=[tile(Z_CA), tile(Z_CG), _prev_halo(GW, CONV_HALO, tm, Z_CA), _prev_halo(GW, CONV_HALO, tm, Z_CG),
                  whole(cw), whole(cb), whole(lg), whole(lb), whole(pw), whole(pb)],
        out_specs=[tile(0), tile(0)], out_shape=[SDS((s, GW), F32), SDS((s, GW), F32)],
        compiler_params=_cp(("parallel",)))(z, z, z, z, cw, cb, lg, lb, pw, pb)


def _conv_bwd_a(z, hc, dy, cw, lg, lb, pw, name):
    s = z.shape[0]
    tm = min(TM, s)

    def body(a_ref, g_ref, ah_ref, gh_ref, hc_ref, dy_ref, lg_ref, lb_ref, pw_ref,
             dhc_ref, dpw_ref, dpb_ref, dlg_ref, dlb_ref, dcw_ref, dcb_ref):
        first = pl.program_id(0) == 0
        dy = dy_ref[...].astype(F32)
        xh, rstd = _ln_stats(hc_ref[...])
        hn = xh * lg_ref[...] + lb_ref[...]
        dhn = _dot(dy, pw_ref[...], "nt") * _dsilu(hn)
        dhc = _ln_bwd(xh, rstd, dhn * lg_ref[...])
        dhc_ref[...] = dhc
        _acc(dpw_ref, _dot(_silu(hn), dy, "tn"), first)
        _acc(dpb_ref, _colsum(dy), first)
        _acc(dlg_ref, _colsum(dhn * xh), first)
        _acc(dlb_ref, _colsum(dhn), first)
        _acc(dcb_ref, _colsum(dhc), first)
        rots = _rotations(_glu_window(a_ref, g_ref, ah_ref, gh_ref))

        @pl.when(first)
        def _():
            dcw_ref[...] = jnp.zeros_like(dcw_ref)

        for k in range(CONV_K):
            dcw_ref[k:k + 1, :] += _colsum(dhc * _delayed(rots, CONV_K - 1 - k, CONV_HALO, tm))

    tile = lambda col: pl.BlockSpec((tm, GW), lambda i: (i, col))
    whole = lambda shape: pl.BlockSpec(shape, lambda i: (0, 0))
    return pl.pallas_call(
        body, name=name, grid=(s // tm,),
        in_specs=[tile(Z_CA), tile(Z_CG), _prev_halo(GW, CONV_HALO, tm, Z_CA), _prev_halo(GW, CONV_HALO, tm, Z_CG),
                  tile(0), tile(0), whole(lg.shape), whole(lb.shape), whole(pw.shape)],
        out_specs=[tile(0), whole((GW, GW)), whole((1, GW)), whole((1, GW)), whole((1, GW)), whole((32, GW)), whole((1, GW))],
        out_shape=[SDS((s, GW), F32), SDS((GW, GW), F32), SDS((1, GW), F32), SDS((1, GW), F32), SDS((1, GW), F32),
                   SDS((32, GW), F32), SDS((1, GW), F32)],
        compiler_params=_cp(("arbitrary",)))(z, z, z, z, hc, dy, lg, lb, pw)


def _conv_bwd_b(z, dhc, cw, name):
    s = z.shape[0]
    tm = min(TM, s)
    nt = s // tm

    def body(a_ref, g_ref, d_ref, dn_ref, w_ref, da_ref, dg_ref):
        keep = (pl.program_id(0) < nt - 1).astype(F32)
        rots = _rotations(jnp.concatenate([d_ref[...], dn_ref[...] * keep], axis=0), advance=True)
        dhg = jnp.zeros((tm, GW), F32)
        for k in range(CONV_K):
            dhg = dhg + w_ref[k:k + 1, :] * _advanced(rots, CONV_K - 1 - k, tm)
        sg = _sigmoid(g_ref[...].astype(F32))
        da_ref[...] = (dhg * sg).astype(BF16)
        dg_ref[...] = (dhg * a_ref[...].astype(F32) * sg * (1.0 - sg)).astype(BF16)

    tile = lambda col: pl.BlockSpec((tm, GW), lambda i: (i, col))
    nxt = pl.BlockSpec((CONV_HALO, GW), lambda i: (jnp.minimum((i + 1) * (tm // CONV_HALO), s // CONV_HALO - 1), 0))
    return pl.pallas_call(
        body, name=name, grid=(nt,),
        in_specs=[tile(Z_CA), tile(Z_CG), tile(0), nxt, pl.BlockSpec(cw.shape, lambda i: (0, 0))],
        out_specs=[tile(0), tile(0)], out_shape=[SDS((s, GW), BF16), SDS((s, GW), BF16)],
        compiler_params=_cp(("parallel",)))(z, z, dhc, dhc, cw)


def _sgu_chunk(zu, zv, lg, lb, wcat, bfull):
    u, v = _gelu(zu), _gelu(zv)
    xh, rstd = _ln_stats(v)
    vn = xh * lg + lb
    lane = lax.shift_right_logical(lax.broadcasted_iota(jnp.int32, (WIN, GW), 1), 6)
    r = jnp.concatenate([jnp.where(lane == g, vn, 0.0) for g in range(4)], axis=0)
    mix = _dot(wcat, r, "nn") + bfull
    return u, xh, rstd, r, mix, lane


def _tril4(w):
    t = lax.broadcasted_iota(jnp.int32, w.shape, 0)
    sidx = lax.broadcasted_iota(jnp.int32, w.shape, 1) & (WIN - 1)
    return jnp.where(sidx <= t, w, 0.0)


def _sgu_fwd(z, lg, lb, wcat, bfull, name):
    s = z.shape[0]
    tm = min(TM, s)

    def body(u_ref, v_ref, lg_ref, lb_ref, w_ref, b_ref, y_ref):
        w = _tril4(w_ref[...])
        for n in range(tm // WIN):
            rows = slice(WIN * n, WIN * (n + 1))
            u, _, _, _, mix, _ = _sgu_chunk(u_ref[rows, :].astype(F32), v_ref[rows, :].astype(F32), lg_ref[...], lb_ref[...], w, b_ref[...])
            y_ref[rows, :] = u * mix

    tile = lambda col: pl.BlockSpec((tm, GW), lambda i: (i, col))
    whole = lambda a: pl.BlockSpec(a.shape, lambda i: (0, 0))
    return pl.pallas_call(
        body, name=name, grid=(s // tm,), in_specs=[tile(Z_GU), tile(Z_GV), whole(lg), whole(lb), whole(wcat), whole(bfull)],
        out_specs=tile(0), out_shape=SDS((s, GW), F32), compiler_params=_cp(("parallel",)))(z, z, lg, lb, wcat, bfull)


def _sgu_bwd(z, dy, lg, lb, wcat, bfull, name):
    s = z.shape[0]
    tm = min(TM, s)

    def body(u_ref, v_ref, dy_ref, lg_ref, lb_ref, w_ref, b_ref, du_ref, dv_ref, dw_ref, db_ref, dlg_ref, dlb_ref):
        first = pl.program_id(0) == 0
        w = _tril4(w_ref[...])
        wt = w.T
        dw = jnp.zeros((WIN, 4 * WIN), F32)
        db = jnp.zeros((WIN, 128), F32)
        dlg = jnp.zeros((1, GW), F32)
        dlb = jnp.zeros((1, GW), F32)
        for n in range(tm // WIN):
            rows = slice(WIN * n, WIN * (n + 1))
            zu, zv, dout = u_ref[rows, :].astype(F32), v_ref[rows, :].astype(F32), dy_ref[rows, :].astype(F32)
            u, xh, rstd, r, mix, lane = _sgu_chunk(zu, zv, lg_ref[...], lb_ref[...], w, b_ref[...])
            dmix = dout * u
            du_ref[rows, :] = (dout * mix * _dgelu(zu)).astype(BF16)
            dw = dw + _dot(dmix, r, "nt")
            for g in range(4):
                db = _lane_put(db, db[:, g:g + 1] + jnp.sum(dmix[:, HD * g:HD * (g + 1)], axis=1, keepdims=True), g)
            dr = _dot(wt, dmix, "nn")
            dvn = jnp.zeros((WIN, GW), F32)
            for g in range(4):
                dvn = dvn + jnp.where(lane == g, dr[WIN * g:WIN * (g + 1), :], 0.0)
            dlg = dlg + _colsum(dvn * xh)
            dlb = dlb + _colsum(dvn)
            dv_ref[rows, :] = (_ln_bwd(xh, rstd, dvn * lg_ref[...]) * _dgelu(zv)).astype(BF16)
        _acc(dw_ref, _tril4(dw), first)
        _acc(db_ref, db, first)
        _acc(dlg_ref, dlg, first)
        _acc(dlb_ref, dlb, first)

    tile = lambda col: pl.BlockSpec((tm, GW), lambda i: (i, col))
    whole = lambda shape: pl.BlockSpec(shape, lambda i: (0, 0))
    return pl.pallas_call(
        body, name=name, grid=(s // tm,),
        in_specs=[tile(Z_GU), tile(Z_GV), tile(0), whole(lg.shape), whole(lb.shape), whole(wcat.shape), whole(bfull.shape)],
        out_specs=[tile(0), tile(0), whole((WIN, 4 * WIN)), whole((WIN, 128)), whole((1, GW)), whole((1, GW))],
        out_shape=[SDS((s, GW), BF16), SDS((s, GW), BF16), SDS((WIN, 4 * WIN), F32), SDS((WIN, 128), F32),
                   SDS((1, GW), F32), SDS((1, GW), F32)],
        compiler_params=_cp(("arbitrary",)))(z, z, dy, lg, lb, wcat, bfull)


def _conv3(win, w, b):
    return (w[2:3, :] * win[FFN_HALO:, :] + w[1:2, :] * pltpu.roll(win, 1, 0)[FFN_HALO:, :]
            + w[0:1, :] * pltpu.roll(win, 2, 0)[FFN_HALO:, :] + b)


def _ffn_specs(s, tm):
    main = pl.BlockSpec((2, None, tm, FF_BLK), lambda j, i: (0, j, i, 0))
    prev = pl.BlockSpec((2, None, FFN_HALO, FF_BLK), lambda j, i: (0, j, jnp.maximum(i * (tm // FFN_HALO) - 1, 0), 0))
    nxt = pl.BlockSpec((2, None, FFN_HALO, FF_BLK),
                       lambda j, i: (0, j, jnp.minimum((i + 1) * (tm // FFN_HALO), s // FFN_HALO - 1), 0))
    wsp = pl.BlockSpec((2, None, 3, FF_BLK), lambda j, i: (0, j, 0, 0))
    bsp = pl.BlockSpec((2, None, 1, FF_BLK), lambda j, i: (0, j, 0, 0))
    return main, prev, nxt, wsp, bsp


def _ffn_act(u4, w4, b4, name, dep=None):
    s = u4.shape[2]
    tm = min(TM, s)
    main, prev, _, wsp, bsp = _ffn_specs(s, tm)

    def body(u_ref, uh_ref, w_ref, b_ref, o_ref, c_ref):
        keep = (pl.program_id(1) > 0).astype(F32)
        gw, vw = [jnp.concatenate([uh_ref[p].astype(F32) * keep, u_ref[p].astype(F32)], axis=0) for p in range(2)]
        gc, vc = _conv3(gw, w_ref[0], b_ref[0]), _conv3(vw, w_ref[1], b_ref[1])
        o_ref[...] = (_silu(gc) * vc).astype(BF16)
        c_ref[0] = gc.astype(BF16)
        c_ref[1] = vc.astype(BF16)

    return _pcall(
        body, (u4, u4, w4, b4), dep, name=name, grid=(FF_NBLK, s // tm), in_specs=[main, prev, wsp, bsp],
        out_specs=[pl.BlockSpec((None, tm, FF_BLK), lambda j, i: (j, i, 0)), main],
        out_shape=[SDS((FF_NBLK, s, FF_BLK), BF16), SDS(u4.shape, BF16)], compiler_params=_cp(("parallel", "parallel")))


def _ffn_bwd(u4, cv4, dact, w4, w_up, name, dep=None):
    s = u4.shape[2]
    tm = min(TM, s)
    nt = s // tm
    main = pl.BlockSpec((2, None, tm, FF_BLK), lambda i, j: (0, j, i, 0))
    nxt = pl.BlockSpec((2, None, FFN_HALO, FF_BLK),
                       lambda i, j: (0, j, jnp.minimum((i + 1) * (tm // FFN_HALO), s // FFN_HALO - 1), 0))
    dmain = pl.BlockSpec((None, tm, FF_BLK), lambda i, j: (j, i, 0))
    dnext = pl.BlockSpec((None, FFN_HALO, FF_BLK), lambda i, j: (j, jnp.minimum((i + 1) * (tm // FFN_HALO), s // FFN_HALO - 1), 0))
    wsp = pl.BlockSpec((2, None, 3, FF_BLK), lambda i, j: (0, j, 0, 0))
    wup = pl.BlockSpec((2, None, D, FF_BLK), lambda i, j: (0, j, 0, 0))
    all_w = pl.BlockSpec((2, FF_NBLK, 3, FF_BLK), lambda i, j: (0, 0, 0, 0))
    all_b = pl.BlockSpec((2, FF_NBLK, 1, FF_BLK), lambda i, j: (0, 0, 0, 0))

    def body(u_ref, c_ref, cn_ref, d_ref, dn_ref, w_ref, wup_ref, du_ref, dw_ref, db_ref, dh_ref, acc_ref):
        i, j = pl.program_id(0), pl.program_id(1)
        first = i == 0
        keep_next = (i < nt - 1).astype(F32)
        taps = [[jnp.zeros((1, FF_BLK), F32) for _ in range(3)] for _ in range(2)]
        bias = [jnp.zeros((1, FF_BLK), F32) for _ in range(2)]
        def matmul(rows, dus):
            prod = _dot(dus[0], wup_ref[0], "nt") + _dot(dus[1], wup_ref[1], "nt")
            acc_ref[rows, :] = jnp.where(j == 0, prod, acc_ref[rows, :] + prod)

        pending = []
        for r0 in range(0, tm, FFN_SUB):
            rows, wide = slice(r0, r0 + FFN_SUB), slice(r0, r0 + FFN_SUB + FFN_HALO)
            if r0 + FFN_SUB < tm:
                gc, vc = [c_ref[p, wide, :].astype(F32) for p in range(2)]
                d = d_ref[wide, :].astype(F32)
            else:
                gc, vc = [jnp.concatenate([c_ref[p, rows, :].astype(F32), cn_ref[p].astype(F32)], axis=0) for p in range(2)]
                d = jnp.concatenate([d_ref[rows, :].astype(F32), dn_ref[...].astype(F32) * keep_next], axis=0)
            sg = _sigmoid(gc)
            duc = (d * vc * (sg * (1.0 + gc * (1.0 - sg))), d * (gc * sg))
            dus = []
            for p in range(2):
                w = w_ref[p]
                own = duc[p][:FFN_SUB, :]
                adv = (pltpu.roll(duc[p], FFN_SUB + FFN_HALO - 2, 0)[:FFN_SUB, :],
                       pltpu.roll(duc[p], FFN_SUB + FFN_HALO - 1, 0)[:FFN_SUB, :], own)
                du = (w[2:3, :] * adv[2] + w[1:2, :] * adv[1] + w[0:1, :] * adv[0]).astype(BF16)
                du_ref[p, rows, :] = du
                dus.append(du)
                ut = u_ref[p, rows, :].astype(F32)
                for k in range(3):
                    taps[p][k] = taps[p][k] + _colsum(adv[k] * ut)
                bias[p] = bias[p] + _colsum(own)
            pending.append((rows, dus))
            if len(pending) > 1:
                matmul(*pending.pop(0))
        matmul(*pending.pop(0))

        for p in range(2):
            @pl.when(first)
            def _():
                db_ref[p, j] = bias[p]
                for k in range(3):
                    dw_ref[p, j, k:k + 1, :] = taps[p][k]

            @pl.when(jnp.logical_not(first))
            def _():
                db_ref[p, j] += bias[p]
                for k in range(3):
                    dw_ref[p, j, k:k + 1, :] += taps[p][k]

        @pl.when(j == FF_NBLK - 1)
        def _():
            dh_ref[...] = acc_ref[...].astype(BF16)

    return _pcall(
        body, (u4, cv4, cv4, dact, dact, w4, w_up.reshape(2, FF_NBLK, D, FF_BLK)), dep, name=name, grid=(nt, FF_NBLK),
        in_specs=[main, main, nxt, dmain, dnext, wsp, wup],
        out_specs=[main, all_w, all_b, pl.BlockSpec((tm, D), lambda i, j: (i, 0))],
        out_shape=[SDS(u4.shape, BF16), SDS((2, FF_NBLK, 3, FF_BLK), F32), SDS((2, FF_NBLK, 1, FF_BLK), F32), SDS((s, D), BF16)],
        scratch_shapes=[pltpu.VMEM((tm, D), F32)], compiler_params=_cp(("arbitrary", "arbitrary")))


def _sum8(parts, name):
    _, r, c = parts[0].shape
    tr = r
    for cand in (512, 256, 128, 64, 32, 16):
        if r % cand == 0 and r > cand:
            tr = cand
            break
    nb = r // tr

    def body(*refs):
        o_ref = refs[-1]
        for l, p_ref in enumerate(refs[:-1]):
            @pl.when(pl.program_id(0) == l)
            def _():
                acc = p_ref[0].astype(F32)
                for j in range(1, N_DEV):
                    acc = acc + p_ref[j].astype(F32)
                o_ref[...] = acc

    def spec(l):
        return pl.BlockSpec((N_DEV, tr, c), lambda ll, i: (0, jnp.where(ll == l, i, jnp.where(ll < l, 0, nb - 1)), 0))

    return pl.pallas_call(
        body, name=name, grid=(len(parts), nb), in_specs=[spec(l) for l in range(len(parts))],
        out_specs=pl.BlockSpec((None, tr, c), lambda ll, i: (ll, i, 0)), out_shape=SDS((len(parts), r, c), F32),
        compiler_params=_cp(("arbitrary", "arbitrary")))(*parts)


def _sum8_small(parts, name):
    n = len(parts)

    def body(*refs):
        for p_ref, o_ref in zip(refs[:n], refs[n:]):
            acc = p_ref[0]
            for j in range(1, N_DEV):
                acc = acc + p_ref[j]
            o_ref[...] = acc

    return pl.pallas_call(body, name=name, out_shape=[SDS(p.shape[1:], F32) for p in parts], compiler_params=_cp())(*parts)


def _adamw_math(w, g, m, v):
    m = ADAM_B1 * m + (1.0 - ADAM_B1) * g
    v = ADAM_B2 * v + (1.0 - ADAM_B2) * (g * g)
    m_hat = m / (1.0 - ADAM_B1 ** ADAM_STEP)
    v_hat = v / (1.0 - ADAM_B2 ** ADAM_STEP)
    return -ADAM_LR * (m_hat / (jnp.sqrt(v_hat) + ADAM_EPS) + ADAM_WD * w), m, v


def _adamw(w, g, m, v, name):
    r, c = w.shape
    tr = r
    for cand in (256, 128, 64):
        if r % cand == 0 and r > cand:
            tr = cand
            break

    def body(w_ref, g_ref, m_ref, v_ref, d_ref, mo_ref, vo_ref):
        d_ref[...], mo_ref[...], vo_ref[...] = _adamw_math(w_ref[...], g_ref[...], m_ref[...], v_ref[...])

    blk = pl.BlockSpec((tr, c), lambda i: (i, 0))
    return pl.pallas_call(body, name=name, grid=(r // tr,), in_specs=[blk] * 4, out_specs=[blk] * 3,
                          out_shape=[SDS((r, c), F32)] * 3, compiler_params=_cp(("parallel",)))(w, g, m, v)


def _adamw_small(ws, gs, ms, vs, name):
    n = len(ws)

    def body(*refs):
        ins, outs = refs[:4 * n], refs[4 * n:]
        for i in range(n):
            d, m, v = _adamw_math(ins[i][...], ins[n + i][...], ins[2 * n + i][...], ins[3 * n + i][...])
            outs[i][...], outs[n + i][...], outs[2 * n + i][...] = d, m, v

    shapes = [SDS(w.shape, F32) for w in ws]
    res = pl.pallas_call(body, name=name, out_shape=shapes * 3, compiler_params=_cp())(*ws, *gs, *ms, *vs)
    return res[:n], res[n:2 * n], res[2 * n:]


def _perm_in(w):
    pad = jnp.zeros(w.shape[:-1] + (ZW - 2308,), w.dtype)
    return jnp.concatenate([w[..., :768], w[..., 772:], w[..., 768:772], pad], axis=-1)


def _unperm_in(g):
    return jnp.concatenate([g[..., :768], g[..., 2304:2308], g[..., 768:2304]], axis=-1)


def _wcat(sgu_w):
    return sgu_w.transpose(1, 0, 2).reshape(WIN, 4 * WIN)


def _layer_fwd(l, x, h1, mod, p, wg, last, target, nxt, w_in_next):
    s = x.shape[0]
    tag = f"_l{l}"
    mrow = lambda k: (mod, 6 * l + k)
    z = _mm_rows(h1, wg["w_in"].get(h1), "nn", ZW, BF16, "mm_z" + tag)
    fcol, frow = _fgate(z, p["bf"], "fgate" + tag, dep=wg["w_out"].prefetch(z))
    y_fox, lse_fox = _fox_fwd(z, fcol, frow, "fox_fwd" + tag)
    y_conv, hc = _conv_fwd(z, wg["conv_w"], p["conv_b"], p["conv_ln_g"], p["conv_ln_b"], wg["conv_pw_w"], p["conv_pw_b"],
                           "conv_fwd" + tag)
    y_swa, lse_swa = _swa_fwd(z, p["sinks"], "swa_fwd" + tag)
    y_sgu = _sgu_fwd(z, p["sgu_ln_g"], p["sgu_ln_b"], p["wcat"], p["bfull"], "sgu_fwd" + tag)
    ys = (y_fox, y_conv, y_swa, y_sgu)
    yn = _gnorm(ys, (p["g_group"], l), "gnorm" + tag)
    tok = wg["w_up"].prefetch(yn)
    o = _mm_rows(yn, wg["w_out"].get(yn), "nn", D, BF16, "mm_o" + tag)
    x1, h2 = _post(x, o, mrow(2), (p["g_post_mix"], l), (p["g_pre_ffn"], l), mrow(4), mrow(3), "post_mix" + tag, dep=tok)
    tok = wg["w_down"].prefetch(h2)
    u = _matmul(h2, wg["w_up"].get(h2), "nn", (N_DEV, s, FF_BLK), BF16, (N_DEV, 1, 1),
                _bs((s, D), lambda j, i, k: (0, 0)), _bs((None, D, FF_BLK), lambda j, i, k: (j, 0, 0)),
                _bs((None, s, FF_BLK), lambda j, i, k: (j, 0, 0)), None, "mm_u" + tag)
    u4 = u.reshape(2, FF_NBLK, s, FF_BLK)
    act, cv4 = _ffn_act(u4, wg["ffn_conv_w"], p["ffn_conv_b"], "ffn_act" + tag, dep=tok)
    tok = None if w_in_next is None else w_in_next.prefetch(act)
    f = _matmul(act, wg["w_down"].get(act), "nn", (s, D), BF16, (1, 1, FF_NBLK),
                _bs((None, s, FF_BLK), lambda i, j, k: (k, 0, 0)), _bs((FF_BLK, D), lambda i, j, k: (k, 0)),
                _bs((s, D), lambda i, j, k: (0, 0)), (s, D), "mm_f" + tag)
    if last:
        out = _post_loss(x1, f, mrow(5), (p["g_post_ffn"], l), target, "post_loss")
    else:
        out = _post(x1, f, mrow(5), (p["g_post_ffn"], l), *nxt, "post_ffn" + tag, dep=tok)
    saved = dict(x=x, h1=h1, z=z, fcol=fcol, frow=frow, lse_fox=lse_fox, hc=hc, lse_swa=lse_swa, ys=ys, yn=yn, o=o, x1=x1,
                 h2=h2, u4=u4, cv4=cv4, act=act, f=f)
    return out, saved


def _tie(a, token):
    return a if token is None else a + token[0, 0]


def _layer_bwd(l, dx2, sv, mod, p, wg, emit, dep=None):
    s = dx2.shape[0]
    tm = min(TM, s)
    tag = f"_l{l}"
    mrow = lambda k: (mod, 6 * l + k)
    g = {}
    df, g["ga2"], g["g_post_ffn"] = _post_bwd(dx2, sv["f"], mrow(5), (p["g_post_ffn"], l), "post_ffn_bwd" + tag, dep=dep)
    dact = _matmul(df, wg["w_down"].get(None), "nt", (FF_NBLK, s, FF_BLK), BF16, (FF_NBLK, 1, 1),
                   _bs((s, D), lambda j, i, k: (0, 0)), _bs((FF_BLK, D), lambda j, i, k: (j, 0)),
                   _bs((None, s, FF_BLK), lambda j, i, k: (j, 0, 0)), None, "mm_dact" + tag)
    tok = emit("w_down", _matmul(sv["act"], df, "tn", (FF_NBLK * FF_BLK, D), BF16, (FF_NBLK, 1, 1),
                                 _bs((None, s, FF_BLK), lambda j, i, k: (j, 0, 0)), _bs((s, D), lambda j, i, k: (0, 0)),
                                 _bs((FF_BLK, D), lambda j, i, k: (j, 0)), None, "mm_dwdown" + tag))
    du, g["ffn_conv_w"], g["ffn_conv_b"], dh2 = _ffn_bwd(sv["u4"], sv["cv4"], dact, wg["ffn_conv_w"], wg["w_up"].get(None),
                                                         "ffn_bwd" + tag, dep=tok)
    du = du.reshape(N_DEV, s, FF_BLK)
    tok = emit("w_up", _matmul(du, sv["h2"], "tn", (N_DEV, FF_BLK, D), BF16, (N_DEV, 1, 1),
                               _bs((None, s, FF_BLK), lambda j, i, k: (j, 0, 0)), _bs((s, D), lambda j, i, k: (0, 0)),
                               _bs((None, FF_BLK, D), lambda j, i, k: (j, 0, 0)), None, "mm_dwup" + tag))
    dx1, g["sh2"], g["sc2"], g["g_pre_ffn"] = _pre_bwd(dh2, sv["x1"], dx2, (p["g_pre_ffn"], l), mrow(4), "pre_ffn_bwd" + tag,
                                                       dep=tok)
    do, g["ga1"], g["g_post_mix"] = _post_bwd(dx1, sv["o"], mrow(2), (p["g_post_mix"], l), "post_mix_bwd" + tag)
    dyn = _mm_rows(do, wg["w_out"].get(None), "nt", D, BF16, "mm_dyn" + tag)
    tok = emit("w_out", _mm_wgrad(sv["yn"], do, BF16, "mm_dwout" + tag))
    dy_fox, dy_conv, dy_swa, dy_sgu, g["g_group"] = _gnorm_bwd(dyn, sv["ys"], (p["g_group"], l), "gnorm_bwd" + tag, dep=tok)
    z = sv["z"]
    dq_f, dk_f, dv_f, dfrow, dfcol = _fox_bwd(z, sv["fcol"], sv["frow"], sv["lse_fox"], sv["ys"][0], dy_fox, "fox_bwd" + tag)
    dgate, g["bf"] = _fgate_bwd(z, p["bf"], dfrow, dfcol, "fgate_bwd" + tag)
    dhc, g["conv_pw_w"], g["conv_pw_b"], g["conv_ln_g"], g["conv_ln_b"], g["conv_w"], g["conv_b"] = _conv_bwd_a(
        z, sv["hc"], dy_conv, wg["conv_w"], p["conv_ln_g"], p["conv_ln_b"], wg["conv_pw_w"], "conv_bwd_a" + tag)
    da_c, dg_c = _conv_bwd_b(z, dhc, wg["conv_w"], "conv_bwd_b" + tag)
    dq_s, dk_s, dv_s, g["sinks"] = _swa_bwd(z, p["sinks"], sv["lse_swa"], sv["ys"][2], dy_swa, "swa_bwd" + tag)
    du_g, dv_g, g["wcat"], g["sgu_bcol"], g["sgu_ln_g"], g["sgu_ln_b"] = _sgu_bwd(
        z, dy_sgu, p["sgu_ln_g"], p["sgu_ln_b"], p["wcat"], p["bfull"], "sgu_bwd" + tag)
    dz = jnp.concatenate([dq_f.astype(BF16), dk_f.astype(BF16), dv_f.astype(BF16), da_c, dg_c, dq_s.astype(BF16), dk_s.astype(BF16),
                          dv_s.astype(BF16), du_g, dv_g, dgate], axis=1)
    tok = emit("w_in", _mm_wgrad(sv["h1"], dz, BF16, "mm_dwin" + tag))
    dh1 = _mm_rows(dz, wg["w_in"].get(None), "nt", D, BF16, "mm_dh1" + tag)
    dx, g["sh1"], g["sc1"], g["g_pre_mix"] = _pre_bwd(dh1, sv["x"], dx1, (p["g_pre_mix"], l), mrow(1), "pre_mix_bwd" + tag,
                                                      dep=tok)
    return dx, g


def _layer_params(l, small, conv_w_full, conv_pw_full, ffn_conv_w_full):
    bf = jnp.pad(small["b_fgate"][l][None, :], ((0, 0), (0, 124)))
    p = dict(
        bf=bf, conv_b=small["conv_b"][l][None], conv_ln_g=small["conv_ln_g"][l][None], conv_ln_b=small["conv_ln_b"][l][None],
        conv_pw_b=small["conv_pw_b"][l][None], sinks=small["swa_sinks"][l], sgu_ln_g=small["sgu_ln_g"][l][None],
        sgu_ln_b=small["sgu_ln_b"][l][None], wcat=_wcat(small["sgu_w"][l]),
        bfull=jnp.repeat(small["sgu_b"][l].T, HD, axis=1),
        ffn_conv_b=small["ffn_conv_b"][l].reshape(2, FF_NBLK, 1, FF_BLK),
        g_group=small["g_group"].reshape(N_LAYER, 1, D), g_post_mix=small["g_post_mix"].reshape(N_LAYER, 1, D),
        g_pre_ffn=small["g_pre_ffn"].reshape(N_LAYER, 1, D), g_post_ffn=small["g_post_ffn"].reshape(N_LAYER, 1, D),
        g_pre_mix=small["g_pre_mix"].reshape(N_LAYER, 1, D))
    wsmall = dict(conv_w=conv_w_full[l], conv_pw_w=conv_pw_full[l].astype(BF16),
                  ffn_conv_w=ffn_conv_w_full[l].reshape(3, 2, FF_NBLK, FF_BLK).transpose(1, 2, 0, 3))
    return p, wsmall


def _local_step(x, target, mod, small, wbig, conv_w_full, conv_pw_full, ffn_conv_w_full, emit, on_loss=None):
    ps, wgs = [], []
    for l in range(N_LAYER):
        p, wsmall = _layer_params(l, small, conv_w_full, conv_pw_full, ffn_conv_w_full)
        ps.append(p)
        wgs.append({**wbig[l], **wsmall})
    h = _rms_mod(x, (ps[0]["g_pre_mix"], 0), (mod, 1), (mod, 0), "rms_mod_l0")
    saved = []
    for l in range(N_LAYER):
        last = l == N_LAYER - 1
        nxt = None if last else ((ps[l]["g_pre_mix"], l + 1), (mod, 6 * (l + 1) + 1), (mod, 6 * (l + 1)))
        out, sv = _layer_fwd(l, x, h, mod, ps[l], wgs[l], last, target, nxt, None if last else wgs[l + 1]["w_in"])
        saved.append(sv)
        if not last:
            x, h = out
    dx, loss = out
    dep = None if on_loss is None else on_loss(loss)
    grads = [None] * N_LAYER
    for l in reversed(range(N_LAYER)):
        dx, grads[l] = _layer_bwd(l, dx, saved[l], mod, ps[l], wgs[l], functools.partial(emit, l), dep)
        dep = None
    return loss, dx, grads


_SMALL = ("b_ada", "g_pre_mix", "g_post_mix", "g_pre_ffn", "g_post_ffn", "b_fgate", "conv_b", "conv_ln_g", "conv_ln_b",
          "conv_pw_b", "swa_sinks", "sgu_ln_g", "sgu_ln_b", "sgu_w", "sgu_b", "g_group", "ffn_conv_b")
_WEIGHTS = ("w_ada", "b_ada", "g_pre_mix", "g_post_mix", "g_pre_ffn", "g_post_ffn", "w_in", "b_fgate", "conv_w", "conv_b",
            "conv_ln_g", "conv_ln_b", "conv_pw_w", "conv_pw_b", "swa_sinks", "sgu_ln_g", "sgu_ln_b", "sgu_w", "sgu_b",
            "g_group", "w_out", "ffn_w_up", "ffn_conv_w", "ffn_conv_b", "ffn_w_down")


def _pad_rows(a, mult):
    r = (-a.shape[0]) % mult
    return a if r == 0 else jnp.concatenate([a, jnp.zeros((r,) + a.shape[1:], a.dtype)], axis=0)


def _view2d(a):
    if a.ndim == 2:
        return a
    return a.reshape(-1, a.shape[-1])


def kernel(x, c, w_ada, b_ada, g_pre_mix, g_post_mix, g_pre_ffn, g_post_ffn, w_in, b_fgate, conv_w, conv_b, conv_ln_g, conv_ln_b, conv_pw_w, conv_pw_b, swa_sinks, sgu_ln_g, sgu_ln_b, sgu_w, sgu_b, g_group, w_out, ffn_w_up, ffn_conv_w, ffn_conv_b, ffn_w_down, loss_target, m_w_ada, m_b_ada, m_g_pre_mix, m_g_post_mix, m_g_pre_ffn, m_g_post_ffn, m_w_in, m_b_fgate, m_conv_w, m_conv_b, m_conv_ln_g, m_conv_ln_b, m_conv_pw_w, m_conv_pw_b, m_swa_sinks, m_sgu_ln_g, m_sgu_ln_b, m_sgu_w, m_sgu_b, m_g_group, m_w_out, m_ffn_w_up, m_ffn_conv_w, m_ffn_conv_b, m_ffn_w_down, v_w_ada, v_b_ada, v_g_pre_mix, v_g_post_mix, v_g_pre_ffn, v_g_post_ffn, v_w_in, v_b_fgate, v_conv_w, v_conv_b, v_conv_ln_g, v_conv_ln_b, v_conv_pw_w, v_conv_pw_b, v_swa_sinks, v_sgu_ln_g, v_sgu_ln_b, v_sgu_w, v_sgu_b, v_g_group, v_w_out, v_ffn_w_up, v_ffn_conv_w, v_ffn_conv_b, v_ffn_w_down):
    env = dict(locals())
    w = {n: env[n] for n in _WEIGHTS}
    mom = {n: env["m_" + n] for n in _WEIGHTS}
    var = {n: env["v_" + n] for n in _WEIGHTS}
    me = 4 * lax.axis_index("x") + 2 * lax.axis_index("y") + lax.axis_index("c")
    x2, target = x[0], loss_target[0]

    (c_all,) = _exchange([c], ["bcast"], "gather_c")
    c_all = c_all.reshape(N_DEV, D)
    (m_all,) = _exchange([_ada_fwd(c_all, w_ada)], ["bcast"], "gather_mod")
    m_mine = lax.dynamic_index_in_dim(m_all, me, axis=2, keepdims=False)
    mod, mod_token = _ada_finish(m_mine.transpose(1, 0, 2).reshape(N_LAYER, 6 * D), b_ada)
    mod = mod.reshape(6 * N_LAYER, 1, D)

    shards = [_tie(conv_w, mod_token), conv_pw_w, ffn_conv_w]
    for l in range(N_LAYER):
        shards += [_perm_in(w_in[l]).astype(BF16), w_out[l].astype(BF16), ffn_w_up[l].astype(BF16), ffn_w_down[l].astype(BF16)]
    gather = _g2_start(shards, "gather_weights_start")
    mod = _tie(mod, gather.token)
    _g2_relay(gather, [0, 1, 2, 3], mod, "gather_relay_first")
    g_cw, g_pw, g_fcw = _g2_wait(gather, [0, 1, 2], mod, "gather_small_wait")
    conv_w_full = g_cw.transpose(1, 2, 0, 3).reshape(N_LAYER, CONV_K, GW)
    conv_pw_full = g_pw.transpose(1, 0, 2, 3).reshape(N_LAYER, GW, GW)
    ffn_conv_w_full = g_fcw.transpose(1, 2, 0, 3).reshape(N_LAYER, 3, N_DEV * FF_BLK)

    def lazy(i, shape, key):
        pre = None if i == 3 else (lambda after: _g2_relay(gather, [i], after, "relay_" + key))
        return _Lazy(lambda after: _g2_wait(gather, [i], after, "wait_" + key)[0].reshape(shape), pre)

    wbig = [dict(w_in=lazy(3 + 4 * l, (D, ZW), f"w_in_l{l}"), w_out=lazy(4 + 4 * l, (D, D), f"w_out_l{l}"),
                 w_up=lazy(5 + 4 * l, (N_DEV, D, FF_BLK), f"w_up_l{l}"),
                 w_down=lazy(6 + 4 * l, (FF_NBLK * FF_BLK, D), f"w_down_l{l}")) for l in range(N_LAYER)]

    grad_flights = []

    def emit(l, key, arr):
        fl = _xchg_start([arr.reshape(N_DEV, -1, arr.shape[-1])], ["a2a"], f"grad_start_{key}_l{l}")
        grad_flights.append(((l, key), fl))
        return fl.token

    small = {n: w[n] for n in _SMALL}
    total = []

    def on_loss(loss8):
        total.append(lax.psum(loss8[0, 0], ("x", "y", "c")))
        return total[0].reshape(1, 1)

    _, dx, grads = _local_step(x2, target, mod, small, wbig, conv_w_full, conv_pw_full, ffn_conv_w_full, emit, on_loss)
    loss = total[0]
    grad_x = dx[None]


    st = lambda key: jnp.stack([grads[l][key] for l in range(N_LAYER)])
    d_conv_w = st("conv_w")[:, :CONV_K, :].reshape(N_LAYER, CONV_K, N_DEV, GW // N_DEV).transpose(2, 0, 1, 3)
    d_pw_w = st("conv_pw_w").reshape(N_LAYER, N_DEV, GW // N_DEV, GW).transpose(1, 0, 2, 3)
    d_fcw = st("ffn_conv_w").reshape(N_LAYER, N_DEV, 3, FF_BLK).transpose(1, 0, 2, 3)
    rows_d = _pad_rows(jnp.concatenate(
        [grads[l][k] for l in range(N_LAYER) for k in ("sh1", "sc1", "ga1", "sh2", "sc2", "ga2")]
        + [grads[l][k] for k in ("g_pre_mix", "g_post_mix", "g_pre_ffn", "g_post_ffn", "g_group") for l in range(N_LAYER)],
        axis=0), 8)
    rows_gw = _pad_rows(jnp.concatenate(
        [grads[l][k] for k in ("conv_b", "conv_ln_g", "conv_ln_b", "conv_pw_b", "sgu_ln_g", "sgu_ln_b") for l in range(N_LAYER)],
        axis=0), 8)
    rows_128 = jnp.concatenate([_pad_rows(jnp.concatenate([grads[l]["bf"] for l in range(N_LAYER)]
                                                          + [grads[l]["sinks"] for l in range(N_LAYER)], axis=0), 8)]
                               + [grads[l]["sgu_bcol"] for l in range(N_LAYER)], axis=0)
    rows_w = jnp.concatenate([grads[l]["wcat"] for l in range(N_LAYER)], axis=0)
    rows_fb = st("ffn_conv_b").reshape(N_LAYER * N_DEV, FF_BLK)
    small_flight = _xchg_start([d_conv_w, d_pw_w, d_fcw, rows_d, rows_gw, rows_128, rows_w, rows_fb],
                               ["a2a"] * 3 + ["bcast"] * 5, "small_grads_start")

    to_mem = {"w_in": lambda a: a.transpose(2, 0, 1), "ffn_w_up": lambda a: a.transpose(0, 2, 1)}
    from_mem = {"w_in": lambda a: a.transpose(1, 2, 0), "ffn_w_up": lambda a: a.transpose(0, 2, 1)}
    flights = dict(grad_flights)
    gr, delta, new_m, new_v = {}, {}, {}, {}

    def adamw_big(n):
        view, back = to_mem.get(n, lambda a: a), from_mem.get(n, lambda a: a)
        shape = view(w[n]).shape
        d, m2, v2 = _adamw(_view2d(view(w[n])), _view2d(gr[n]), _view2d(view(mom[n])), _view2d(view(var[n])), "adamw_" + n)
        delta[n], new_m[n], new_v[n] = back(d.reshape(shape)), back(m2.reshape(shape)), back(v2.reshape(shape))
        gr[n] = back(gr[n].reshape(shape))

    after = small_flight.token
    for n, key in (("ffn_w_down", "w_down"), ("ffn_w_up", "w_up"), ("w_out", "w_out"), ("w_in", "w_in")):
        parts = [_xchg_wait(flights[(l, key)], [0], after, f"grad_wait_{key}_l{l}")[0] for l in reversed(range(N_LAYER))]
        g = _sum8(parts[::-1], "sum_" + key)
        gr[n] = to_mem["w_in"](_unperm_in(g)) if n == "w_in" else g
        adamw_big(n)
        after = new_v[n]

    small_parts = _xchg_wait(small_flight, list(range(8)), after, "small_grads_wait")
    s_conv_w, s_pw_w, s_fcw, s_d, s_gw, s_128, s_w, s_fb = _sum8_small(
        [p.reshape(N_DEV, -1, p.shape[-1]) for p in small_parts], "sum_small_grads")
    gr["conv_w"] = s_conv_w.reshape(N_LAYER, CONV_K, GW // N_DEV)
    gr["conv_pw_w"] = s_pw_w.reshape(N_LAYER, GW // N_DEV, GW)
    gr["ffn_conv_w"] = s_fcw.reshape(N_LAYER, 3, FF_BLK)
    gr["b_ada"] = s_d[:6 * N_LAYER].reshape(N_LAYER, 6 * D)
    for i, k in enumerate(("g_pre_mix", "g_post_mix", "g_pre_ffn", "g_post_ffn", "g_group")):
        gr[k] = s_d[6 * N_LAYER + 2 * i:6 * N_LAYER + 2 * i + 2]
    for i, k in enumerate(("conv_b", "conv_ln_g", "conv_ln_b", "conv_pw_b", "sgu_ln_g", "sgu_ln_b")):
        gr[k] = s_gw[2 * i:2 * i + 2]
    gr["b_fgate"] = s_128[0:2, :4]
    gr["swa_sinks"] = s_128[2:4, :4]
    gr["sgu_b"] = s_128[8:].reshape(N_LAYER, WIN, 128)[:, :, :4].transpose(0, 2, 1)
    gr["sgu_w"] = s_w.reshape(N_LAYER, WIN, 4, WIN).transpose(0, 2, 1, 3)
    gr["ffn_conv_b"] = s_fb.reshape(N_LAYER, N_DEV * FF_BLK)
    dmod_all = small_parts[3][:, :6 * N_LAYER, :].reshape(N_DEV, N_LAYER, 6 * D)
    ncol = 6 * D // N_DEV
    dmod_cols = lax.dynamic_slice_in_dim(dmod_all, me * ncol, ncol, axis=2).transpose(1, 0, 2)
    gr["w_ada"] = _ada_bwd(c_all, dmod_cols)

    adamw_big("w_ada")
    smalls = [n for n in _WEIGHTS if n not in ("w_ada", "w_in", "w_out", "ffn_w_up", "ffn_w_down")]
    ds, ms, vs = _adamw_small([_view2d(w[n]) for n in smalls], [_view2d(gr[n]) for n in smalls],
                              [_view2d(mom[n]) for n in smalls], [_view2d(var[n]) for n in smalls], "adamw_small")
    for i, n in enumerate(smalls):
        delta[n], new_m[n], new_v[n] = ds[i].reshape(w[n].shape), ms[i].reshape(w[n].shape), vs[i].reshape(w[n].shape)

    return (loss, grad_x, *[gr[n].reshape(w[n].shape) for n in _WEIGHTS], *[delta[n] for n in _WEIGHTS],
            *[new_m[n] for n in _WEIGHTS], *[new_v[n] for n in _WEIGHTS])
```

```python
import functools

import jax
import jax.numpy as jnp
from jax import lax
from jax.experimental import pallas as pl
from jax.experimental.pallas import tpu as pltpu

F32, BF16 = jnp.float32, jnp.bfloat16
SDS = jax.ShapeDtypeStruct
MESH = pl.DeviceIdType.MESH

N_DEV = 8
D = 1024
GW = 256
HD = 64
N_LAYER = 2
ZW = 2432
FF_BLK = 704
FF_NBLK = 4
CONV_K = 31
CONV_HALO = 32
FFN_HALO = 16
FFN_SUB = 128
EPS = 1e-6
NEG = -1e30
SCALE = HD ** -0.5
VMEM_LIMIT_V7X = 56 * 1024 * 1024
TM = 512
WGRAD_ROWS = 256
TQ = 256
FOX_FWD_PAIR = 2
WIN = 128

ADAM_LR, ADAM_B1, ADAM_B2, ADAM_EPS, ADAM_WD, ADAM_STEP = 0.001, 0.9, 0.999, 1e-08, 0.01, 10

Z_FQ, Z_FK, Z_FV, Z_CA, Z_CG, Z_SQ = 0, 1, 2, 3, 4, 5
Z_SK, Z_SV = 12, 13
Z_GU, Z_GV = 7, 8
Z_FG = 18


def _cp(sem=None):
    return pltpu.CompilerParams(dimension_semantics=sem, vmem_limit_bytes=VMEM_LIMIT_V7X)


def _vec(arr3, idx, ngrid):
    w = arr3.shape[-1]
    if ngrid == 1:
        return pl.BlockSpec((None, 1, w), lambda i: (idx, 0, 0))
    return pl.BlockSpec((None, 1, w), lambda i, j: (idx, 0, 0))


def _sigmoid(x):
    return jax.nn.sigmoid(x)


def _silu(x):
    return x * _sigmoid(x)


def _dsilu(x):
    s = _sigmoid(x)
    return s * (1.0 + x * (1.0 - s))


_G0, _G1 = 0.7978845608028654, 0.044715


def _gelu(x):
    return 0.5 * x * (1.0 + jnp.tanh(_G0 * (x + _G1 * x * x * x)))


def _dgelu(x):
    t = jnp.tanh(_G0 * (x + _G1 * x * x * x))
    return 0.5 * (1.0 + t) + 0.5 * x * (1.0 - t * t) * (_G0 * (1.0 + 3.0 * _G1 * x * x))


def _rstd(x):
    return lax.rsqrt(jnp.mean(x * x, axis=-1, keepdims=True) + EPS)


def _rms_bwd(xh, r, t):
    return r * (t - xh * jnp.mean(t * xh, axis=-1, keepdims=True))


def _ln_stats(x):
    mu = jnp.mean(x, axis=-1, keepdims=True)
    xc = x - mu
    rstd = lax.rsqrt(jnp.mean(xc * xc, axis=-1, keepdims=True) + EPS)
    return xc * rstd, rstd


def _ln_bwd(xh, rstd, dxh):
    return rstd * (dxh - jnp.mean(dxh, axis=-1, keepdims=True) - xh * jnp.mean(dxh * xh, axis=-1, keepdims=True))


def _colsum(x):
    return jnp.sum(x, axis=0, keepdims=True)


def _dot(a, b, kind):
    dn = {"nn": (((1,), (0,)), ((), ())), "nt": (((1,), (1,)), ((), ())), "tn": (((0,), (0,)), ((), ()))}[kind]
    return lax.dot_general(a.astype(BF16), b.astype(BF16), dn, preferred_element_type=F32)


def _exchange(arrs, modes, name):
    n = len(arrs)
    outs = [SDS((N_DEV,) + a.shape, a.dtype) if m == "bcast" else SDS(a.shape, a.dtype) for a, m in zip(arrs, modes)]

    def body(*refs):
        ins, dst = refs[:n], refs[n:2 * n]
        send, recv, loc = refs[2 * n:]
        x, y, c = lax.axis_index("x"), lax.axis_index("y"), lax.axis_index("c")
        me = 4 * x + 2 * y + c

        def src(a, j):
            return ins[a] if modes[a] == "bcast" else ins[a].at[j]

        local = [pltpu.make_async_copy(src(a, me), dst[a].at[me], loc.at[a]) for a in range(n)]
        for cp in local:
            cp.start()
        sent, landed = [], []
        for k in (2, 4, 6, 3, 5, 7, 1):
            px = 1 - x if k & 4 else x
            py = 1 - y if k & 2 else y
            pc = 1 - c if k & 1 else c
            peer = 4 * px + 2 * py + pc
            for a in range(n):
                cp = pltpu.make_async_remote_copy(src_ref=src(a, peer), dst_ref=dst[a].at[me], send_sem=send.at[a, k - 1],
                                                  recv_sem=recv.at[a, k - 1], device_id=(px, py, pc), device_id_type=MESH)
                cp.start()
                sent.append(cp)
                landed.append(pltpu.make_async_remote_copy(src_ref=src(a, peer), dst_ref=dst[a].at[peer],
                                                           send_sem=send.at[a, k - 1], recv_sem=recv.at[a, k - 1],
                                                           device_id=(px, py, pc), device_id_type=MESH))
        for cp in landed:
            cp.wait_recv()
        for cp in sent:
            cp.wait_send()
        for cp in local:
            cp.wait()

    hbm = pl.BlockSpec(memory_space=pltpu.HBM)
    return pl.pallas_call(
        body, name=name, out_shape=outs, in_specs=[hbm] * n, out_specs=[hbm] * n,
        scratch_shapes=[pltpu.SemaphoreType.DMA((n, N_DEV - 1)), pltpu.SemaphoreType.DMA((n, N_DEV - 1)),
                        pltpu.SemaphoreType.DMA((n,))],
        compiler_params=pltpu.CompilerParams(has_side_effects=True),
    )(*arrs)


_PEER_ORDER = (2, 4, 6, 3, 5, 7, 1)
_HBM = pl.BlockSpec(memory_space=pltpu.HBM)
_SEM = pl.BlockSpec(memory_space=pltpu.SEMAPHORE)
_EFFECT = pltpu.SideEffectType.DATAFLOW_SIDE_EFFECTING


def _peer(k):
    x, y, c = lax.axis_index("x"), lax.axis_index("y"), lax.axis_index("c")
    px = 1 - x if k & 4 else x
    py = 1 - y if k & 2 else y
    pc = 1 - c if k & 1 else c
    return (px, py, pc), 4 * px + 2 * py + pc


def _my_id():
    return 4 * lax.axis_index("x") + 2 * lax.axis_index("y") + lax.axis_index("c")


def _split_copies(src_ref, land_ref, send, recv, loc, mode):
    me = _my_id()
    pick = (lambda j: src_ref) if mode == "bcast" else (lambda j: src_ref.at[j])
    local = pltpu.make_async_copy(pick(me), land_ref.at[me], loc)
    remote = []
    for k in _PEER_ORDER:
        dev, peer = _peer(k)
        out = pltpu.make_async_remote_copy(src_ref=pick(peer), dst_ref=land_ref.at[me], send_sem=send.at[k - 1],
                                           recv_sem=recv.at[k - 1], device_id=dev, device_id_type=MESH)
        arrive = pltpu.make_async_remote_copy(src_ref=pick(peer), dst_ref=land_ref.at[peer], send_sem=send.at[k - 1],
                                              recv_sem=recv.at[k - 1], device_id=dev, device_id_type=MESH)
        remote.append((out, arrive))
    return local, remote


class _Flight:
    def __init__(self, srcs, lands, sends, recvs, locs, modes, token):
        self.srcs, self.lands, self.sends, self.recvs, self.locs, self.modes, self.token = (
            list(srcs), list(lands), list(sends), list(recvs), list(locs), list(modes), token)


def _xchg_start(arrs, modes, name):
    n = len(arrs)
    lands = [lax.empty((N_DEV,) + a.shape if m == "bcast" else a.shape, a.dtype) for a, m in zip(arrs, modes)]

    def body(*refs):
        srcs, lnds = refs[:n], refs[n:2 * n]
        outs = refs[2 * n:]
        sends, recvs, locs, token = outs[:n], outs[n:2 * n], outs[2 * n:3 * n], outs[5 * n]
        for a in range(n):
            local, remote = _split_copies(srcs[a], lnds[a], sends[a], recvs[a], locs[a], modes[a])
            local.start()
            for out, _ in remote:
                out.start()
        token[...] = jnp.zeros_like(token)

    sem7 = pltpu.SemaphoreType.DMA((N_DEV - 1,))
    res = pl.pallas_call(
        body, name=name,
        out_shape=[sem7] * (2 * n) + [pltpu.SemaphoreType.DMA(())] * n + [pltpu.HBM(a.shape, a.dtype) for a in arrs]
        + [pltpu.HBM(b.shape, b.dtype) for b in lands] + [SDS((8, 128), F32)],
        in_specs=[_HBM] * (2 * n), out_specs=[_SEM] * (3 * n) + [_HBM] * (2 * n) + [pl.BlockSpec(memory_space=pltpu.VMEM)],
        input_output_aliases={i: 3 * n + i for i in range(2 * n)},
        compiler_params=pltpu.CompilerParams(has_side_effects=_EFFECT),
    )(*[pltpu.with_memory_space_constraint(a, pltpu.HBM) for a in arrs],
      *[pltpu.with_memory_space_constraint(b, pltpu.HBM) for b in lands])
    return _Flight(res[3 * n:4 * n], res[4 * n:5 * n], res[:n], res[n:2 * n], res[2 * n:3 * n], modes, res[5 * n])


def _xchg_wait(flight, idx, after, name):
    n = len(idx)
    modes = [flight.modes[i] for i in idx]

    def body(*refs):
        srcs, lnds = refs[:n], refs[n:2 * n]
        sends, recvs, locs = refs[2 * n:3 * n], refs[3 * n:4 * n], refs[4 * n:5 * n]
        for a in range(n):
            local, remote = _split_copies(srcs[a], lnds[a], sends[a], recvs[a], locs[a], modes[a])
            local.wait()
            for _, arrive in remote:
                arrive.wait_send()
                arrive.wait_recv()

    ops = ([flight.srcs[i] for i in idx] + [flight.lands[i] for i in idx] + [flight.sends[i] for i in idx]
           + [flight.recvs[i] for i in idx] + [flight.locs[i] for i in idx])
    res = pl.pallas_call(
        body, name=name, out_shape=[pltpu.HBM(o.shape, o.dtype) for o in ops[:2 * n]],
        in_specs=[_HBM] * (2 * n) + [_SEM] * (3 * n) + [pl.BlockSpec(memory_space=pl.ANY)], out_specs=[_HBM] * (2 * n),
        input_output_aliases={i: i for i in range(2 * n)},
        compiler_params=pltpu.CompilerParams(has_side_effects=_EFFECT),
    )(*ops, after)
    return res[n:]


class _Lazy:
    def __init__(self, fn, pre=None):
        self.fn, self.pre, self.val, self.started = fn, pre, None, False

    def prefetch(self, after):
        token = self.pre(after) if self.pre is not None and not self.started else None
        self.started = True
        return token

    def get(self, after):
        self.prefetch(after)
        if self.val is None:
            self.val = self.fn(after)
        return self.val


_CHIP_PEERS = (2, 4, 6)


def _g2_copies_a(src_ref, land_ref, send, recv, loc):
    me = _my_id()
    local = pltpu.make_async_copy(src_ref, land_ref.at[me], loc)
    remote = []
    for j, k in enumerate(_CHIP_PEERS + (1,)):
        dev, peer = _peer(k)
        out = pltpu.make_async_remote_copy(src_ref=src_ref, dst_ref=land_ref.at[me], send_sem=send.at[j], recv_sem=recv.at[j],
                                           device_id=dev, device_id_type=MESH)
        arrive = pltpu.make_async_remote_copy(src_ref=src_ref, dst_ref=land_ref.at[peer], send_sem=send.at[j],
                                              recv_sem=recv.at[j], device_id=dev, device_id_type=MESH)
        remote.append((out, arrive))
    return local, remote


def _g2_copies_b(land_ref, send, recv):
    sib, _ = _peer(1)
    pairs = []
    for j, k in enumerate(_CHIP_PEERS):
        _, same_core = _peer(k)
        _, other_core = _peer(k | 1)
        out = pltpu.make_async_remote_copy(src_ref=land_ref.at[same_core], dst_ref=land_ref.at[same_core], send_sem=send.at[j],
                                           recv_sem=recv.at[j], device_id=sib, device_id_type=MESH)
        arrive = pltpu.make_async_remote_copy(src_ref=land_ref.at[same_core], dst_ref=land_ref.at[other_core],
                                              send_sem=send.at[j], recv_sem=recv.at[j], device_id=sib, device_id_type=MESH)
        pairs.append((out, arrive))
    return pairs


class _Gather2:
    def __init__(self, srcs, lands, sends, recvs, locs, token):
        self.srcs, self.lands, self.sends, self.recvs, self.locs, self.token = (
            list(srcs), list(lands), list(sends), list(recvs), list(locs), token)
        self.sends_b, self.recvs_b = [None] * len(self.srcs), [None] * len(self.srcs)


def _g2_start(arrs, name):
    n = len(arrs)
    lands = [lax.empty((N_DEV,) + a.shape, a.dtype) for a in arrs]

    def body(*refs):
        srcs, lnds = refs[:n], refs[n:2 * n]
        outs = refs[2 * n:]
        sends, recvs, locs, token = outs[:n], outs[n:2 * n], outs[2 * n:3 * n], outs[5 * n]
        for a in range(n):
            local, remote = _g2_copies_a(srcs[a], lnds[a], sends[a], recvs[a], locs[a])
            local.start()
            for out, _ in remote:
                out.start()
        token[...] = jnp.zeros_like(token)

    sem4 = pltpu.SemaphoreType.DMA((4,))
    res = pl.pallas_call(
        body, name=name,
        out_shape=[sem4] * (2 * n) + [pltpu.SemaphoreType.DMA(())] * n + [pltpu.HBM(a.shape, a.dtype) for a in arrs]
        + [pltpu.HBM(b.shape, b.dtype) for b in lands] + [SDS((8, 128), F32)],
        in_specs=[_HBM] * (2 * n), out_specs=[_SEM] * (3 * n) + [_HBM] * (2 * n) + [pl.BlockSpec(memory_space=pltpu.VMEM)],
        input_output_aliases={i: 3 * n + i for i in range(2 * n)},
        compiler_params=pltpu.CompilerParams(has_side_effects=_EFFECT),
    )(*[pltpu.with_memory_space_constraint(a, pltpu.HBM) for a in arrs],
      *[pltpu.with_memory_space_constraint(b, pltpu.HBM) for b in lands])
    return _Gather2(res[3 * n:4 * n], res[4 * n:5 * n], res[:n], res[n:2 * n], res[2 * n:3 * n], res[5 * n])


def _g2_relay(g, idx, after, name):
    n = len(idx)

    def body(*refs):
        srcs, lnds = refs[:n], refs[n:2 * n]
        sends, recvs, locs = refs[2 * n:3 * n], refs[3 * n:4 * n], refs[4 * n:5 * n]
        outs = refs[5 * n + 1:]
        sends_b, recvs_b = outs[2 * n:3 * n], outs[3 * n:4 * n]
        for a in range(n):
            local, remote = _g2_copies_a(srcs[a], lnds[a], sends[a], recvs[a], locs[a])
            local.wait()
            for _, arrive in remote:
                arrive.wait_send()
                arrive.wait_recv()
        for a in range(n):
            for out, _ in _g2_copies_b(lnds[a], sends_b[a], recvs_b[a]):
                out.start()
        outs[4 * n][...] = jnp.zeros_like(outs[4 * n])

    ops = ([g.srcs[i] for i in idx] + [g.lands[i] for i in idx] + [g.sends[i] for i in idx] + [g.recvs[i] for i in idx]
           + [g.locs[i] for i in idx])
    sem3 = pltpu.SemaphoreType.DMA((3,))
    res = pl.pallas_call(
        body, name=name, out_shape=[pltpu.HBM(o.shape, o.dtype) for o in ops[:2 * n]] + [sem3] * (2 * n) + [SDS((8, 128), F32)],
        in_specs=[_HBM] * (2 * n) + [_SEM] * (3 * n) + [pl.BlockSpec(memory_space=pl.ANY)],
        out_specs=[_HBM] * (2 * n) + [_SEM] * (2 * n) + [pl.BlockSpec(memory_space=pltpu.VMEM)],
        input_output_aliases={i: i for i in range(2 * n)},
        compiler_params=pltpu.CompilerParams(has_side_effects=_EFFECT),
    )(*ops, after)
    for a, i in enumerate(idx):
        g.srcs[i], g.lands[i] = res[a], res[n + a]
        g.sends_b[i], g.recvs_b[i] = res[2 * n + a], res[3 * n + a]
    return res[4 * n]


def _g2_wait(g, idx, after, name):
    n = len(idx)

    def body(*refs):
        lnds, sends_b, recvs_b = refs[:n], refs[n:2 * n], refs[2 * n:3 * n]
        for a in range(n):
            for _, arrive in _g2_copies_b(lnds[a], sends_b[a], recvs_b[a]):
                arrive.wait_send()
                arrive.wait_recv()

    ops = [g.lands[i] for i in idx] + [g.sends_b[i] for i in idx] + [g.recvs_b[i] for i in idx]
    res = pl.pallas_call(
        body, name=name, out_shape=[pltpu.HBM(o.shape, o.dtype) for o in ops[:n]],
        in_specs=[_HBM] * n + [_SEM] * (2 * n) + [pl.BlockSpec(memory_space=pl.ANY)], out_specs=[_HBM] * n,
        input_output_aliases={i: i for i in range(n)},
        compiler_params=pltpu.CompilerParams(has_side_effects=_EFFECT),
    )(*ops, after)
    return list(res)


def _matmul(a, b, kind, out_shape, out_dtype, grid, a_spec, b_spec, o_spec, acc_shape, name):
    nk = grid[2]

    def body(a_ref, b_ref, o_ref, *scratch):
        prod = _dot(a_ref[...], b_ref[...], kind)
        if nk == 1:
            o_ref[...] = prod.astype(out_dtype)
        else:
            acc = scratch[0]
            k = pl.program_id(2)

            @pl.when(k == 0)
            def _():
                acc[...] = prod

            @pl.when(k > 0)
            def _():
                acc[...] += prod

            @pl.when(k == nk - 1)
            def _():
                o_ref[...] = acc[...].astype(out_dtype)

    return pl.pallas_call(
        body, name=name, grid=grid, in_specs=[a_spec, b_spec], out_specs=o_spec, out_shape=SDS(out_shape, out_dtype),
        scratch_shapes=[] if nk == 1 else [pltpu.VMEM(acc_shape, F32)],
        compiler_params=_cp(("parallel", "parallel", "arbitrary")))(a, b)


def _bs(shape, fn):
    return pl.BlockSpec(shape, fn)


def _mm_rows(a, w, kind, n_out, out_dtype, name):
    s, k = a.shape
    tm = min(TM, s)
    return _matmul(a, w, kind, (s, n_out), out_dtype, (s // tm, 1, 1),
                   _bs((tm, k), lambda i, j, kk: (i, 0)), _bs(w.shape, lambda i, j, kk: (0, 0)),
                   _bs((tm, n_out), lambda i, j, kk: (i, 0)), None, name)


def _mm_wgrad(a, dy, out_dtype, name):
    s, k = a.shape
    n = dy.shape[1]
    tko = min(WGRAD_ROWS, k)
    return _matmul(a, dy, "tn", (k, n), out_dtype, (k // tko, 1, 1),
                   _bs((s, tko), lambda i, j, kk: (0, i)), _bs((s, n), lambda i, j, kk: (0, 0)),
                   _bs((tko, n), lambda i, j, kk: (i, 0)), None, name)


def _ada_fwd(c_all, w_ada):
    ncol = w_ada.shape[2]

    def body(c_ref, w_ref, o_ref):
        ca = _silu(c_ref[...])
        ca = jnp.concatenate([ca, jnp.zeros_like(ca)], axis=0)
        o_ref[...] = _dot(ca, w_ref[...], "nn")[:N_DEV, :]

    return pl.pallas_call(
        body, name="ada_fwd", grid=(N_LAYER,),
        in_specs=[pl.BlockSpec((N_DEV, D), lambda l: (0, 0)), pl.BlockSpec((None, D, ncol), lambda l: (l, 0, 0))],
        out_specs=pl.BlockSpec((None, N_DEV, ncol), lambda l: (l, 0, 0)),
        out_shape=SDS((N_LAYER, N_DEV, ncol), F32), compiler_params=_cp(("parallel",)))(c_all, w_ada)


def _ada_finish(m_mine, b_ada):
    def body(m_ref, b_ref, o_ref, t_ref):
        o_ref[...] = m_ref[...] + b_ref[...]
        t_ref[...] = jnp.zeros_like(t_ref)

    return pl.pallas_call(body, name="ada_finish", out_shape=[SDS(b_ada.shape, F32), SDS((8, 128), F32)])(m_mine, b_ada)


def _ada_bwd(c_all, dmod_cols):
    ncol = dmod_cols.shape[2]

    def body(c_ref, d_ref, o_ref):
        ca = _silu(c_ref[...])
        ca = jnp.concatenate([ca, jnp.zeros_like(ca)], axis=0)
        dm = d_ref[...]
        dm = jnp.concatenate([dm, jnp.zeros_like(dm)], axis=0)
        o_ref[...] = _dot(ca, dm, "tn")

    return pl.pallas_call(
        body, name="ada_bwd", grid=(N_LAYER,),
        in_specs=[pl.BlockSpec((N_DEV, D), lambda l: (0, 0)), pl.BlockSpec((None, N_DEV, ncol), lambda l: (l, 0, 0))],
        out_specs=pl.BlockSpec((None, D, ncol), lambda l: (l, 0, 0)),
        out_shape=SDS((N_LAYER, D, ncol), F32), compiler_params=_cp(("parallel",)))(c_all, dmod_cols)


def _rows(s):
    tm = min(TM, s)
    return tm, pl.BlockSpec((tm, D), lambda i: (i, 0))


def _pcall(body, operands, dep, **kw):
    if dep is None:
        return pl.pallas_call(body, **kw)(*operands)
    n = len(operands)

    def body_dep(*refs):
        body(*refs[:n], *refs[n + 1:])

    kw["in_specs"] = list(kw["in_specs"]) + [pl.BlockSpec(memory_space=pl.ANY)]
    return pl.pallas_call(body_dep, **kw)(*operands, dep)


def _rms_mod(x, g, sc, sh, name):
    s = x.shape[0]
    tm, row = _rows(s)

    def body(x_ref, g_ref, sc_ref, sh_ref, h_ref):
        xf = x_ref[...]
        h_ref[...] = (xf * _rstd(xf) * (g_ref[...] * (1.0 + sc_ref[...])) + sh_ref[...]).astype(BF16)

    return pl.pallas_call(
        body, name=name, grid=(s // tm,), in_specs=[row, _vec(*g, 1), _vec(*sc, 1), _vec(*sh, 1)], out_specs=row,
        out_shape=SDS((s, D), BF16), compiler_params=_cp(("parallel",)))(x, g[0], sc[0], sh[0])


def _post(xres, o, ga, gpost, gn, scn, shn, name, dep=None):
    s = xres.shape[0]
    tm, row = _rows(s)

    def body(x_ref, o_ref, ga_ref, gp_ref, gn_ref, sc_ref, sh_ref, xn_ref, h_ref):
        of = o_ref[...].astype(F32)
        xn = x_ref[...] + ga_ref[...] * (of * _rstd(of) * gp_ref[...])
        xn_ref[...] = xn
        h_ref[...] = (xn * _rstd(xn) * (gn_ref[...] * (1.0 + sc_ref[...])) + sh_ref[...]).astype(BF16)

    return _pcall(
        body, (xres, o, ga[0], gpost[0], gn[0], scn[0], shn[0]), dep, name=name, grid=(s // tm,),
        in_specs=[row, row, _vec(*ga, 1), _vec(*gpost, 1), _vec(*gn, 1), _vec(*scn, 1), _vec(*shn, 1)],
        out_specs=[row, row], out_shape=[SDS((s, D), F32), SDS((s, D), BF16)], compiler_params=_cp(("parallel",)))


def _post_loss(xres, o, ga, gpost, target, name, dep=None):
    s = xres.shape[0]
    tm, row = _rows(s)

    def body(x_ref, o_ref, ga_ref, gp_ref, t_ref, dy_ref, loss_ref):
        of = o_ref[...].astype(F32)
        err = x_ref[...] + ga_ref[...] * (of * _rstd(of) * gp_ref[...]) - t_ref[...]
        dy_ref[...] = err * (1.0 / D)

        @pl.when(pl.program_id(0) == 0)
        def _():
            loss_ref[...] = jnp.zeros_like(loss_ref)

        loss_ref[...] += jnp.sum(jnp.mean(err * err, axis=-1, keepdims=True), axis=0, keepdims=True) * 0.5

    return _pcall(
        body, (xres, o, ga[0], gpost[0], target), dep, name=name, grid=(s // tm,),
        in_specs=[row, row, _vec(*ga, 1), _vec(*gpost, 1), row],
        out_specs=[row, pl.BlockSpec((8, 128), lambda i: (0, 0))], out_shape=[SDS((s, D), F32), SDS((8, 128), F32)],
        compiler_params=_cp(("arbitrary",)))


def _acc(ref, val, first):
    @pl.when(first)
    def _():
        ref[...] = val

    @pl.when(jnp.logical_not(first))
    def _():
        ref[...] += val


def _post_bwd(dxn, o, ga, gpost, name, dep=None):
    s = dxn.shape[0]
    tm, row = _rows(s)
    vec = pl.BlockSpec((1, D), lambda i: (0, 0))

    def body(d_ref, o_ref, ga_ref, gp_ref, do_ref, dga_ref, dgp_ref):
        of, dx = o_ref[...].astype(F32), d_ref[...]
        r = _rstd(of)
        oh = of * r
        do_ref[...] = _rms_bwd(oh, r, dx * (ga_ref[...] * gp_ref[...])).astype(BF16)
        cs = _colsum(dx * oh)
        first = pl.program_id(0) == 0
        _acc(dga_ref, cs * gp_ref[...], first)
        _acc(dgp_ref, cs * ga_ref[...], first)

    return _pcall(
        body, (dxn, o, ga[0], gpost[0]), dep, name=name, grid=(s // tm,),
        in_specs=[row, row, _vec(*ga, 1), _vec(*gpost, 1)], out_specs=[row, vec, vec],
        out_shape=[SDS((s, D), BF16), SDS((1, D), F32), SDS((1, D), F32)], compiler_params=_cp(("arbitrary",)))


def _pre_bwd(dh, x, dres, g, sc, name, dep=None):
    s = x.shape[0]
    tm, row = _rows(s)
    vec = pl.BlockSpec((1, D), lambda i: (0, 0))

    def body(dh_ref, x_ref, dr_ref, g_ref, sc_ref, dx_ref, dsh_ref, dsc_ref, dg_ref):
        xf, d = x_ref[...], dh_ref[...].astype(F32)
        r = _rstd(xf)
        xh = xf * r
        dx_ref[...] = dr_ref[...] + _rms_bwd(xh, r, d * (g_ref[...] * (1.0 + sc_ref[...])))
        cs = _colsum(d * xh)
        first = pl.program_id(0) == 0
        _acc(dsh_ref, _colsum(d), first)
        _acc(dsc_ref, cs * g_ref[...], first)
        _acc(dg_ref, cs * (1.0 + sc_ref[...]), first)

    return _pcall(
        body, (dh, x, dres, g[0], sc[0]), dep, name=name, grid=(s // tm,),
        in_specs=[row, row, row, _vec(*g, 1), _vec(*sc, 1)], out_specs=[row, vec, vec, vec],
        out_shape=[SDS((s, D), F32), SDS((1, D), F32), SDS((1, D), F32), SDS((1, D), F32)],
        compiler_params=_cp(("arbitrary",)))


def _gnorm(ys, gg, name):
    s = ys[0].shape[0]
    tm = min(TM, s)
    yb = pl.BlockSpec((tm, GW), lambda i: (i, 0))

    def body(y0, y1, y2, y3, g_ref, o_ref):
        for i, yr in enumerate((y0, y1, y2, y3)):
            y = yr[...]
            o_ref[:, GW * i:GW * (i + 1)] = (y * _rstd(y) * g_ref[:, GW * i:GW * (i + 1)]).astype(BF16)

    return pl.pallas_call(
        body, name=name, grid=(s // tm,), in_specs=[yb] * 4 + [_vec(*gg, 1)], out_specs=pl.BlockSpec((tm, D), lambda i: (i, 0)),
        out_shape=SDS((s, D), BF16), compiler_params=_cp(("parallel",)))(*ys, gg[0])


def _gnorm_bwd(dyn, ys, gg, name, dep=None):
    s = ys[0].shape[0]
    tm = min(TM, s)
    yb = pl.BlockSpec((tm, GW), lambda i: (i, 0))

    def body(d_ref, y0, y1, y2, y3, g_ref, o0, o1, o2, o3, dg_ref):
        first = pl.program_id(0) == 0
        for i, (yr, orf) in enumerate(zip((y0, y1, y2, y3), (o0, o1, o2, o3))):
            y = yr[...]
            d = d_ref[:, GW * i:GW * (i + 1)].astype(F32)
            r = _rstd(y)
            yh = y * r
            orf[...] = _rms_bwd(yh, r, d * g_ref[:, GW * i:GW * (i + 1)]).astype(BF16)
            cs = _colsum(d * yh)

            @pl.when(first)
            def _():
                dg_ref[:, GW * i:GW * (i + 1)] = cs

            @pl.when(jnp.logical_not(first))
            def _():
                dg_ref[:, GW * i:GW * (i + 1)] += cs

    return _pcall(
        body, (dyn, *ys, gg[0]), dep, name=name, grid=(s // tm,),
        in_specs=[pl.BlockSpec((tm, D), lambda i: (i, 0))] + [yb] * 4 + [_vec(*gg, 1)],
        out_specs=[yb] * 4 + [pl.BlockSpec((1, D), lambda i: (0, 0))],
        out_shape=[SDS((s, GW), BF16)] * 4 + [SDS((1, D), F32)], compiler_params=_cp(("arbitrary",)))


def _lane_put(acc, col, h):
    lane = lax.broadcasted_iota(jnp.int32, acc.shape, 1)
    return jnp.where(lane == h, col, acc)


def _fgate(z, bf, name, dep=None):
    s = z.shape[0]

    def body(z_ref, b_ref, fc_ref, fr_ref):
        xg = z_ref[...].astype(F32) + b_ref[...]
        lf = jnp.minimum(xg, 0.0) - jnp.log(1.0 + jnp.exp(-jnp.abs(xg)))
        lane = lax.broadcasted_iota(jnp.int32, lf.shape, 1)
        row = lax.broadcasted_iota(jnp.int32, lf.shape, 0)
        f = jnp.where(lane < 4, lf, 0.0)
        sh = 1
        while sh < s:
            f = f + jnp.where(row >= sh, pltpu.roll(f, sh, 0), 0.0)
            sh *= 2
        fc_ref[...] = f
        fr_ref[...] = f.T[:8, :]

    return _pcall(
        body, (z, bf), dep, name=name, grid=(1,),
        in_specs=[pl.BlockSpec((s, 128), lambda i: (0, Z_FG)), pl.BlockSpec((1, 128), lambda i: (0, 0))],
        out_specs=[pl.BlockSpec((s, 128), lambda i: (0, 0)), pl.BlockSpec((8, s), lambda i: (0, 0))],
        out_shape=[SDS((s, 128), F32), SDS((8, s), F32)], compiler_params=_cp(("arbitrary",)))


def _fgate_bwd(z, bf, dfrow, dfcol, name):
    s = z.shape[0]

    def body(z_ref, b_ref, d_ref, dc_ref, dz_ref, db_ref):
        d = jnp.concatenate([d_ref[...], jnp.zeros((120, s), F32)], axis=0).T + dc_ref[...]
        row = lax.broadcasted_iota(jnp.int32, d.shape, 0)
        lane = lax.broadcasted_iota(jnp.int32, d.shape, 1)
        sh = 1
        while sh < s:
            d = d + jnp.where(row < s - sh, pltpu.roll(d, s - sh, 0), 0.0)
            sh *= 2
        xg = z_ref[...].astype(F32) + b_ref[...]
        dz = jnp.where(lane < 4, d * _sigmoid(-xg), 0.0)
        dz_ref[...] = dz.astype(BF16)
        db_ref[...] = _colsum(dz)

    return pl.pallas_call(
        body, name=name, grid=(1,),
        in_specs=[pl.BlockSpec((s, 128), lambda i: (0, Z_FG)), pl.BlockSpec((1, 128), lambda i: (0, 0)),
                  pl.BlockSpec((8, s), lambda i: (0, 0)), pl.BlockSpec((s, 128), lambda i: (0, 0))],
        out_specs=[pl.BlockSpec((s, 128), lambda i: (0, 0)), pl.BlockSpec((1, 128), lambda i: (0, 0))],
        out_shape=[SDS((s, 128), BF16), SDS((1, 128), F32)], compiler_params=_cp(("arbitrary",)))(z, bf, dfrow, dfcol)


def _fox_scores(q_ref, k_ref, fc_ref, fr_ref, h, i, nk, tq):
    kw = nk * tq
    q = q_ref[:, HD * h:HD * (h + 1)] * SCALE
    sc = _dot(q, k_ref[0:kw, HD * h:HD * (h + 1)], "nt") + fc_ref[:, h:h + 1] - fr_ref[h:h + 1, 0:kw]
    qpos = i * tq + lax.broadcasted_iota(jnp.int32, (tq, kw), 0)
    kpos = lax.broadcasted_iota(jnp.int32, (tq, kw), 1)
    return q, jnp.where(kpos <= qpos, sc, NEG)


def _fox_fwd(z, fcol, frow, name):
    s = z.shape[0]
    tq = min(TQ, s)
    nc = s // tq

    def body(q_ref, k_ref, v_ref, fc_ref, fr_ref, y_ref, l_ref):
        i = pl.program_id(0)
        for n in range(0, nc, FOX_FWD_PAIR):
            nk = min(n + FOX_FWD_PAIR, nc)

            @pl.when((i >= n) & (i < n + FOX_FWD_PAIR))
            def _():
                kw = nk * tq
                lse = jnp.zeros((tq, 128), F32)
                for h in range(4):
                    _, sc = _fox_scores(q_ref, k_ref, fc_ref, fr_ref, h, i, nk, tq)
                    m = jnp.max(sc, axis=-1, keepdims=True)
                    p = jnp.exp(sc - m)
                    l = jnp.sum(p, axis=-1, keepdims=True)
                    y_ref[:, HD * h:HD * (h + 1)] = _dot(p, v_ref[0:kw, HD * h:HD * (h + 1)], "nn") / l
                    lse = _lane_put(lse, m + jnp.log(l), h)
                l_ref[...] = lse

    return pl.pallas_call(
        body, name=name, grid=(nc,),
        in_specs=[pl.BlockSpec((tq, GW), lambda i: (i, Z_FQ)), pl.BlockSpec((s, GW), lambda i: (0, Z_FK)),
                  pl.BlockSpec((s, GW), lambda i: (0, Z_FV)), pl.BlockSpec((tq, 128), lambda i: (i, 0)),
                  pl.BlockSpec((8, s), lambda i: (0, 0))],
        out_specs=[pl.BlockSpec((tq, GW), lambda i: (i, 0)), pl.BlockSpec((tq, 128), lambda i: (i, 0))],
        out_shape=[SDS((s, GW), F32), SDS((s, 128), F32)], compiler_params=_cp(("parallel",)))(z, z, z, fcol, frow)


def _fox_bwd(z, fcol, frow, lse, y, dy, name):
    s = z.shape[0]
    tq = min(TQ, s)
    nc = s // tq
    half = max(nc // 2, 1)

    def body(q_ref, k_ref, v_ref, fc_ref, fr_ref, l_ref, y_ref, dy_ref, dq_ref, dk_ref, dv_ref, df_ref, dfq_ref):
        @pl.when(pl.program_id(0) == 0)
        def _():
            dk_ref[...] = jnp.zeros_like(dk_ref)
            dv_ref[...] = jnp.zeros_like(dv_ref)
            df_ref[...] = jnp.zeros_like(df_ref)

        i = pl.program_id(0)
        for cond, nk in ((i < half, half), (i >= half, nc)):
            @pl.when(cond)
            def _():
                kw = nk * tq
                dfq = jnp.zeros((tq, 128), F32)
                dyf = dy_ref[...].astype(F32)
                for h in range(4):
                    hs = slice(HD * h, HD * (h + 1))
                    q, sc = _fox_scores(q_ref, k_ref, fc_ref, fr_ref, h, i, nk, tq)
                    p = jnp.exp(sc - l_ref[:, h:h + 1])
                    dyh = dyf[:, hs]
                    dd = jnp.sum(dyh * y_ref[:, hs], axis=-1, keepdims=True)
                    ds = p * (_dot(dyh, v_ref[0:kw, hs], "nt") - dd)
                    dq_ref[:, hs] = _dot(ds, k_ref[0:kw, hs], "nn") * SCALE
                    dk_ref[0:kw, hs] += _dot(ds, q, "tn")
                    dv_ref[0:kw, hs] += _dot(p, dyh, "tn")
                    df_ref[h:h + 1, 0:kw] -= _colsum(ds)
                    dfq = _lane_put(dfq, jnp.sum(ds, axis=-1, keepdims=True), h)
                dfq_ref[...] = dfq

    tile = lambda w: pl.BlockSpec((tq, w), lambda i: (i, 0))
    full = pl.BlockSpec((s, GW), lambda i: (0, 0))
    rows8 = pl.BlockSpec((8, s), lambda i: (0, 0))
    return pl.pallas_call(
        body, name=name, grid=(nc,),
        in_specs=[pl.BlockSpec((tq, GW), lambda i: (i, Z_FQ)), pl.BlockSpec((s, GW), lambda i: (0, Z_FK)),
                  pl.BlockSpec((s, GW), lambda i: (0, Z_FV)), tile(128), rows8, tile(128), tile(GW), tile(GW)],
        out_specs=[tile(GW), full, full, rows8, tile(128)],
        out_shape=[SDS((s, GW), F32), SDS((s, GW), F32), SDS((s, GW), F32), SDS((8, s), F32), SDS((s, 128), F32)],
        compiler_params=_cp(("arbitrary",)))(z, z, z, fcol, frow, lse, y, dy)


def _swa_block(q_ref, k_ref, v_ref, n):
    qs = pl.multiple_of(n * WIN, WIN)
    ks = pl.multiple_of(jnp.maximum(n - 1, 0) * WIN, WIN)
    qb = q_ref[pl.ds(qs, WIN), :]
    kb = k_ref[pl.ds(ks, 2 * WIN), :]
    vb = v_ref[pl.ds(ks, 2 * WIN), :]
    rows = lax.broadcasted_iota(jnp.int32, (2 * WIN, 2 * WIN), 0) & (WIN - 1)
    dist = (qs + rows) - (ks + lax.broadcasted_iota(jnp.int32, (2 * WIN, 2 * WIN), 1))
    return qs, ks, qb, kb, vb, (dist >= 0) & (dist < WIN)


def _stack2(x, kvh):
    return jnp.concatenate([x[:, HD * (2 * kvh):HD * (2 * kvh + 1)], x[:, HD * (2 * kvh + 1):HD * (2 * kvh + 2)]], axis=0)


def _sink2(sink_ref, kvh):
    top = lax.broadcasted_iota(jnp.int32, (2 * WIN, 1), 0) < WIN
    return jnp.where(top, sink_ref[2 * kvh], sink_ref[2 * kvh + 1])


def _swa_fwd(z, sinks, name):
    s = z.shape[0]

    def body(sink_ref, q_ref, k_ref, v_ref, y_ref, l_ref):
        def step(n, carry):
            qs, ks, qb, kb, vb, valid = _swa_block(q_ref, k_ref, v_ref, n)
            lse = jnp.zeros((WIN, 128), F32)
            for kvh in range(2):
                kv = slice(HD * kvh, HD * (kvh + 1))
                sc = jnp.where(valid, _dot(_stack2(qb, kvh) * SCALE, kb[:, kv], "nt"), NEG)
                sink = _sink2(sink_ref, kvh)
                m = jnp.maximum(jnp.max(sc, axis=-1, keepdims=True), sink)
                p = jnp.exp(sc - m)
                den = jnp.sum(p, axis=-1, keepdims=True) + jnp.exp(sink - m)
                o = _dot(p, vb[:, kv], "nn") / den
                lrow = m + jnp.log(den)
                for j in range(2):
                    h = 2 * kvh + j
                    y_ref[pl.ds(qs, WIN), HD * h:HD * (h + 1)] = o[WIN * j:WIN * (j + 1), :]
                    lse = _lane_put(lse, lrow[WIN * j:WIN * (j + 1), :], h)
            l_ref[pl.ds(qs, WIN), :] = lse
            return carry

        lax.fori_loop(0, s // WIN, step, 0, unroll=2)

    return pl.pallas_call(
        body, name=name, grid=(1,),
        in_specs=[pl.BlockSpec(memory_space=pltpu.SMEM), pl.BlockSpec((s, GW), lambda i: (0, Z_SQ)),
                  pl.BlockSpec((s, 128), lambda i: (0, Z_SK)), pl.BlockSpec((s, 128), lambda i: (0, Z_SV))],
        out_specs=[pl.BlockSpec((s, GW), lambda i: (0, 0)), pl.BlockSpec((s, 128), lambda i: (0, 0))],
        out_shape=[SDS((s, GW), F32), SDS((s, 128), F32)], compiler_params=_cp(("arbitrary",)))(sinks, z, z, z)


def _swa_bwd(z, sinks, lse, y, dy, name):
    s = z.shape[0]

    def body(sink_ref, q_ref, k_ref, v_ref, l_ref, y_ref, dy_ref, dq_ref, dk_ref, dv_ref, dsink_ref):
        dk_ref[...] = jnp.zeros_like(dk_ref)
        dv_ref[...] = jnp.zeros_like(dv_ref)
        dsink_ref[...] = jnp.zeros_like(dsink_ref)

        def step(n, carry):
            qs, ks, qb, kb, vb, valid = _swa_block(q_ref, k_ref, v_ref, n)
            lse_b = l_ref[pl.ds(qs, WIN), :]
            yb = y_ref[pl.ds(qs, WIN), :]
            dyb = dy_ref[pl.ds(qs, WIN), :].astype(F32)
            dsink = jnp.zeros((1, 128), F32)
            for kvh in range(2):
                kv = slice(HD * kvh, HD * (kvh + 1))
                q = _stack2(qb, kvh) * SCALE
                dy2 = _stack2(dyb, kvh)
                sc = jnp.where(valid, _dot(q, kb[:, kv], "nt"), NEG)
                lh = jnp.concatenate([lse_b[:, 2 * kvh:2 * kvh + 1], lse_b[:, 2 * kvh + 1:2 * kvh + 2]], axis=0)
                p = jnp.exp(sc - lh)
                dd = jnp.sum(dy2 * _stack2(yb, kvh), axis=-1, keepdims=True)
                ds = p * (_dot(dy2, vb[:, kv], "nt") - dd)
                dq = _dot(ds, kb[:, kv], "nn") * SCALE
                dk_ref[pl.ds(ks, 2 * WIN), kv] += _dot(ds, q, "tn")
                dv_ref[pl.ds(ks, 2 * WIN), kv] += _dot(p, dy2, "tn")
                dsk = jnp.exp(_sink2(sink_ref, kvh) - lh) * dd
                for j in range(2):
                    h = 2 * kvh + j
                    dq_ref[pl.ds(qs, WIN), HD * h:HD * (h + 1)] = dq[WIN * j:WIN * (j + 1), :]
                    dsink = _lane_put(dsink, dsink[:, h:h + 1] - jnp.sum(dsk[WIN * j:WIN * (j + 1), :], axis=0, keepdims=True), h)
            dsink_ref[...] += dsink
            return carry

        lax.fori_loop(0, s // WIN, step, 0, unroll=2)

    full = lambda w: pl.BlockSpec((s, w), lambda i: (0, 0))
    return pl.pallas_call(
        body, name=name, grid=(1,),
        in_specs=[pl.BlockSpec(memory_space=pltpu.SMEM), pl.BlockSpec((s, GW), lambda i: (0, Z_SQ)),
                  pl.BlockSpec((s, 128), lambda i: (0, Z_SK)), pl.BlockSpec((s, 128), lambda i: (0, Z_SV)),
                  full(128), full(GW), full(GW)],
        out_specs=[full(GW), full(128), full(128), pl.BlockSpec((1, 128), lambda i: (0, 0))],
        out_shape=[SDS((s, GW), F32), SDS((s, 128), F32), SDS((s, 128), F32), SDS((1, 128), F32)],
        compiler_params=_cp(("arbitrary",)))(sinks, z, z, z, lse, y, dy)


_SUBLANES = 8


def _rotations(win, advance=False):
    n = win.shape[0]
    return [win] + [pltpu.roll(win, (n - b) if advance else b, 0) for b in range(1, _SUBLANES)]


def _delayed(rots, shift, halo, tm):
    a, b = divmod(shift, _SUBLANES)
    return rots[b][halo - _SUBLANES * a:halo - _SUBLANES * a + tm, :]


def _advanced(rots, shift, tm):
    a, b = divmod(shift, _SUBLANES)
    return rots[b][_SUBLANES * a:_SUBLANES * a + tm, :]


def _prev_halo(width, halo, tm, col):
    return pl.BlockSpec((halo, width), lambda i: (jnp.maximum(i * (tm // halo) - 1, 0), col))


def _glu_window(a_ref, g_ref, ah_ref, gh_ref):
    keep = (pl.program_id(0) > 0).astype(F32)
    a = jnp.concatenate([ah_ref[...].astype(F32) * keep, a_ref[...].astype(F32)], axis=0)
    g = jnp.concatenate([gh_ref[...].astype(F32), g_ref[...].astype(F32)], axis=0)
    return a * _sigmoid(g)


def _conv_fwd(z, cw, cb, lg, lb, pw, pb, name):
    s = z.shape[0]
    tm = min(TM, s)

    def body(a_ref, g_ref, ah_ref, gh_ref, w_ref, b_ref, lg_ref, lb_ref, pw_ref, pb_ref, y_ref, hc_ref):
        rots = _rotations(_glu_window(a_ref, g_ref, ah_ref, gh_ref))
        hc = jnp.zeros((tm, GW), F32) + b_ref[...]
        for k in range(CONV_K):
            hc = hc + w_ref[k:k + 1, :] * _delayed(rots, CONV_K - 1 - k, CONV_HALO, tm)
        hc_ref[...] = hc
        xh, _ = _ln_stats(hc)
        y_ref[...] = _dot(_silu(xh * lg_ref[...] + lb_ref[...]), pw_ref[...], "nn") + pb_ref[...]

    tile = lambda col: pl.BlockSpec((tm, GW), lambda i: (i, col))
    whole = lambda a: pl.BlockSpec(a.shape, lambda i: (0, 0))
    return pl.pallas_call(
        body, name=name, grid=(s // tm,),
        in_specs=[tile(Z_CA), tile(Z_CG), _prev_halo(GW, CONV_HALO, tm, Z_CA), _prev_halo(GW, CONV_HALO, tm, Z_CG),
                  whole(cw), whole(cb), whole(lg), whole(lb), whole(pw), whole(pb)],
        out_specs=[tile(0), tile(0)], out_shape=[SDS((s, GW), F32), SDS((s, GW), F32)],
        compiler_params=_cp(("parallel",)))(z, z, z, z, cw, cb, lg, lb, pw, pb)


def _conv_bwd_a(z, hc, dy, cw, lg, lb, pw, name):
    s = z.shape[0]
    tm = min(TM, s)

    def body(a_ref, g_ref, ah_ref, gh_ref, hc_ref, dy_ref, lg_ref, lb_ref, pw_ref,
             dhc_ref, dpw_ref, dpb_ref, dlg_ref, dlb_ref, dcw_ref, dcb_ref):
        first = pl.program_id(0) == 0
        dy = dy_ref[...].astype(F32)
        xh, rstd = _ln_stats(hc_ref[...])
        hn = xh * lg_ref[...] + lb_ref[...]
        dhn = _dot(dy, pw_ref[...], "nt") * _dsilu(hn)
        dhc = _ln_bwd(xh, rstd, dhn * lg_ref[...])
        dhc_ref[...] = dhc
        _acc(dpw_ref, _dot(_silu(hn), dy, "tn"), first)
        _acc(dpb_ref, _colsum(dy), first)
        _acc(dlg_ref, _colsum(dhn * xh), first)
        _acc(dlb_ref, _colsum(dhn), first)
        _acc(dcb_ref, _colsum(dhc), first)
        rots = _rotations(_glu_window(a_ref, g_ref, ah_ref, gh_ref))

        @pl.when(first)
        def _():
            dcw_ref[...] = jnp.zeros_like(dcw_ref)

        for k in range(CONV_K):
            dcw_ref[k:k + 1, :] += _colsum(dhc * _delayed(rots, CONV_K - 1 - k, CONV_HALO, tm))

    tile = lambda col: pl.BlockSpec((tm, GW), lambda i: (i, col))
    whole = lambda shape: pl.BlockSpec(shape, lambda i: (0, 0))
    return pl.pallas_call(
        body, name=name, grid=(s // tm,),
        in_specs=[tile(Z_CA), tile(Z_CG), _prev_halo(GW, CONV_HALO, tm, Z_CA), _prev_halo(GW, CONV_HALO, tm, Z_CG),
                  tile(0), tile(0), whole(lg.shape), whole(lb.shape), whole(pw.shape)],
        out_specs=[tile(0), whole((GW, GW)), whole((1, GW)), whole((1, GW)), whole((1, GW)), whole((32, GW)), whole((1, GW))],
        out_shape=[SDS((s, GW), F32), SDS((GW, GW), F32), SDS((1, GW), F32), SDS((1, GW), F32), SDS((1, GW), F32),
                   SDS((32, GW), F32), SDS((1, GW), F32)],
        compiler_params=_cp(("arbitrary",)))(z, z, z, z, hc, dy, lg, lb, pw)


def _conv_bwd_b(z, dhc, cw, name):
    s = z.shape[0]
    tm = min(TM, s)
    nt = s // tm

    def body(a_ref, g_ref, d_ref, dn_ref, w_ref, da_ref, dg_ref):
        keep = (pl.program_id(0) < nt - 1).astype(F32)
        rots = _rotations(jnp.concatenate([d_ref[...], dn_ref[...] * keep], axis=0), advance=True)
        dhg = jnp.zeros((tm, GW), F32)
        for k in range(CONV_K):
            dhg = dhg + w_ref[k:k + 1, :] * _advanced(rots, CONV_K - 1 - k, tm)
        sg = _sigmoid(g_ref[...].astype(F32))
        da_ref[...] = (dhg * sg).astype(BF16)
        dg_ref[...] = (dhg * a_ref[...].astype(F32) * sg * (1.0 - sg)).astype(BF16)

    tile = lambda col: pl.BlockSpec((tm, GW), lambda i: (i, col))
    nxt = pl.BlockSpec((CONV_HALO, GW), lambda i: (jnp.minimum((i + 1) * (tm // CONV_HALO), s // CONV_HALO - 1), 0))
    return pl.pallas_call(
        body, name=name, grid=(nt,),
        in_specs=[tile(Z_CA), tile(Z_CG), tile(0), nxt, pl.BlockSpec(cw.shape, lambda i: (0, 0))],
        out_specs=[tile(0), tile(0)], out_shape=[SDS((s, GW), BF16), SDS((s, GW), BF16)],
        compiler_params=_cp(("parallel",)))(z, z, dhc, dhc, cw)


def _sgu_chunk(zu, zv, lg, lb, wcat, bfull):
    u, v = _gelu(zu), _gelu(zv)
    xh, rstd = _ln_stats(v)
    vn = xh * lg + lb
    lane = lax.shift_right_logical(lax.broadcasted_iota(jnp.int32, (WIN, GW), 1), 6)
    r = jnp.concatenate([jnp.where(lane == g, vn, 0.0) for g in range(4)], axis=0)
    mix = _dot(wcat, r, "nn") + bfull
    return u, xh, rstd, r, mix, lane


def _tril4(w):
    t = lax.broadcasted_iota(jnp.int32, w.shape, 0)
    sidx = lax.broadcasted_iota(jnp.int32, w.shape, 1) & (WIN - 1)
    return jnp.where(sidx <= t, w, 0.0)


def _sgu_fwd(z, lg, lb, wcat, bfull, name):
    s = z.shape[0]
    tm = min(TM, s)

    def body(u_ref, v_ref, lg_ref, lb_ref, w_ref, b_ref, y_ref):
        w = _tril4(w_ref[...])
        for n in range(tm // WIN):
            rows = slice(WIN * n, WIN * (n + 1))
            u, _, _, _, mix, _ = _sgu_chunk(u_ref[rows, :].astype(F32), v_ref[rows, :].astype(F32), lg_ref[...], lb_ref[...], w, b_ref[...])
            y_ref[rows, :] = u * mix

    tile = lambda col: pl.BlockSpec((tm, GW), lambda i: (i, col))
    whole = lambda a: pl.BlockSpec(a.shape, lambda i: (0, 0))
    return pl.pallas_call(
        body, name=name, grid=(s // tm,), in_specs=[tile(Z_GU), tile(Z_GV), whole(lg), whole(lb), whole(wcat), whole(bfull)],
        out_specs=tile(0), out_shape=SDS((s, GW), F32), compiler_params=_cp(("parallel",)))(z, z, lg, lb, wcat, bfull)


def _sgu_bwd(z, dy, lg, lb, wcat, bfull, name):
    s = z.shape[0]
    tm = min(TM, s)

    def body(u_ref, v_ref, dy_ref, lg_ref, lb_ref, w_ref, b_ref, du_ref, dv_ref, dw_ref, db_ref, dlg_ref, dlb_ref):
        first = pl.program_id(0) == 0
        w = _tril4(w_ref[...])
        wt = w.T
        dw = jnp.zeros((WIN, 4 * WIN), F32)
        db = jnp.zeros((WIN, 128), F32)
        dlg = jnp.zeros((1, GW), F32)
        dlb = jnp.zeros((1, GW), F32)
        for n in range(tm // WIN):
            rows = slice(WIN * n, WIN * (n + 1))
            zu, zv, dout = u_ref[rows, :].astype(F32), v_ref[rows, :].astype(F32), dy_ref[rows, :].astype(F32)
            u, xh, rstd, r, mix, lane = _sgu_chunk(zu, zv, lg_ref[...], lb_ref[...], w, b_ref[...])
            dmix = dout * u
            du_ref[rows, :] = (dout * mix * _dgelu(zu)).astype(BF16)
            dw = dw + _dot(dmix, r, "nt")
            for g in range(4):
                db = _lane_put(db, db[:, g:g + 1] + jnp.sum(dmix[:, HD * g:HD * (g + 1)], axis=1, keepdims=True), g)
            dr = _dot(wt, dmix, "nn")
            dvn = jnp.zeros((WIN, GW), F32)
            for g in range(4):
                dvn = dvn + jnp.where(lane == g, dr[WIN * g:WIN * (g + 1), :], 0.0)
            dlg = dlg + _colsum(dvn * xh)
            dlb = dlb + _colsum(dvn)
            dv_ref[rows, :] = (_ln_bwd(xh, rstd, dvn * lg_ref[...]) * _dgelu(zv)).astype(BF16)
        _acc(dw_ref, _tril4(dw), first)
        _acc(db_ref, db, first)
        _acc(dlg_ref, dlg, first)
        _acc(dlb_ref, dlb, first)

    tile = lambda col: pl.BlockSpec((tm, GW), lambda i: (i, col))
    whole = lambda shape: pl.BlockSpec(shape, lambda i: (0, 0))
    return pl.pallas_call(
        body, name=name, grid=(s // tm,),
        in_specs=[tile(Z_GU), tile(Z_GV), tile(0), whole(lg.shape), whole(lb.shape), whole(wcat.shape), whole(bfull.shape)],
        out_specs=[tile(0), tile(0), whole((WIN, 4 * WIN)), whole((WIN, 128)), whole((1, GW)), whole((1, GW))],
        out_shape=[SDS((s, GW), BF16), SDS((s, GW), BF16), SDS((WIN, 4 * WIN), F32), SDS((WIN, 128), F32),
                   SDS((1, GW), F32), SDS((1, GW), F32)],
        compiler_params=_cp(("arbitrary",)))(z, z, dy, lg, lb, wcat, bfull)


def _conv3(win, w, b):
    return (w[2:3, :] * win[FFN_HALO:, :] + w[1:2, :] * pltpu.roll(win, 1, 0)[FFN_HALO:, :]
            + w[0:1, :] * pltpu.roll(win, 2, 0)[FFN_HALO:, :] + b)


def _ffn_specs(s, tm):
    main = pl.BlockSpec((2, None, tm, FF_BLK), lambda j, i: (0, j, i, 0))
    prev = pl.BlockSpec((2, None, FFN_HALO, FF_BLK), lambda j, i: (0, j, jnp.maximum(i * (tm // FFN_HALO) - 1, 0), 0))
    nxt = pl.BlockSpec((2, None, FFN_HALO, FF_BLK),
                       lambda j, i: (0, j, jnp.minimum((i + 1) * (tm // FFN_HALO), s // FFN_HALO - 1), 0))
    wsp = pl.BlockSpec((2, None, 3, FF_BLK), lambda j, i: (0, j, 0, 0))
    bsp = pl.BlockSpec((2, None, 1, FF_BLK), lambda j, i: (0, j, 0, 0))
    return main, prev, nxt, wsp, bsp


def _ffn_act(u4, w4, b4, name, dep=None):
    s = u4.shape[2]
    tm = min(TM, s)
    main, prev, _, wsp, bsp = _ffn_specs(s, tm)

    def body(u_ref, uh_ref, w_ref, b_ref, o_ref, c_ref):
        keep = (pl.program_id(1) > 0).astype(F32)
        gw, vw = [jnp.concatenate([uh_ref[p].astype(F32) * keep, u_ref[p].astype(F32)], axis=0) for p in range(2)]
        gc, vc = _conv3(gw, w_ref[0], b_ref[0]), _conv3(vw, w_ref[1], b_ref[1])
        o_ref[...] = (_silu(gc) * vc).astype(BF16)
        c_ref[0] = gc.astype(BF16)
        c_ref[1] = vc.astype(BF16)

    return _pcall(
        body, (u4, u4, w4, b4), dep, name=name, grid=(FF_NBLK, s // tm), in_specs=[main, prev, wsp, bsp],
        out_specs=[pl.BlockSpec((None, tm, FF_BLK), lambda j, i: (j, i, 0)), main],
        out_shape=[SDS((FF_NBLK, s, FF_BLK), BF16), SDS(u4.shape, BF16)], compiler_params=_cp(("parallel", "parallel")))


def _ffn_bwd(u4, cv4, dact, w4, w_up, name, dep=None):
    s = u4.shape[2]
    tm = min(TM, s)
    nt = s // tm
    nsteps = nt * FF_NBLK

    def split(tt):
        return lax.shift_right_logical(tt, FF_NBLK.bit_length() - 1), tt & (FF_NBLK - 1)

    def cur(t):
        return split(jnp.minimum(t, nsteps - 1))

    def prev(t):
        return split(jnp.maximum(t - 1, 0))

    halo_row = lambda t: jnp.minimum((cur(t)[0] + 1) * (tm // FFN_HALO), s // FFN_HALO - 1)
    main = pl.BlockSpec((2, None, tm, FF_BLK), lambda t: (0, cur(t)[1], cur(t)[0], 0))
    nxt = pl.BlockSpec((2, None, FFN_HALO, FF_BLK), lambda t: (0, cur(t)[1], halo_row(t), 0))
    dmain = pl.BlockSpec((None, tm, FF_BLK), lambda t: (cur(t)[1], cur(t)[0], 0))
    dnext = pl.BlockSpec((None, FFN_HALO, FF_BLK), lambda t: (cur(t)[1], halo_row(t), 0))
    wsp = pl.BlockSpec((2, None, 3, FF_BLK), lambda t: (0, cur(t)[1], 0, 0))
    wup = pl.BlockSpec((2, None, D, FF_BLK), lambda t: (0, prev(t)[1], 0, 0))
    all_w = pl.BlockSpec((2, FF_NBLK, 3, FF_BLK), lambda t: (0, 0, 0, 0))
    all_b = pl.BlockSpec((2, FF_NBLK, 1, FF_BLK), lambda t: (0, 0, 0, 0))

    def body(u_ref, c_ref, cn_ref, d_ref, dn_ref, w_ref, wup_ref, du_ref, dw_ref, db_ref, dh_ref, acc_ref, stage_ref):
        t = pl.program_id(0)
        (i, j), (_, jp) = cur(t), prev(t)
        live = (t < nsteps).astype(F32)
        first = i == 0
        slot, pslot = t & 1, (t + 1) & 1
        keep_next = (i < nt - 1).astype(F32)

        @pl.when(t == 0)
        def _():
            stage_ref[...] = jnp.zeros_like(stage_ref)

        taps = [[jnp.zeros((1, FF_BLK), F32) for _ in range(3)] for _ in range(2)]
        bias = [jnp.zeros((1, FF_BLK), F32) for _ in range(2)]
        for r0 in range(0, tm, FFN_SUB):
            rows = slice(r0, r0 + FFN_SUB)
            prod = _dot(stage_ref[pslot, 0, rows, :], wup_ref[0], "nt") + _dot(stage_ref[pslot, 1, rows, :], wup_ref[1], "nt")
            acc_ref[rows, :] = jnp.where(jp == 0, prod, acc_ref[rows, :] + prod)

        for r0 in range(0, tm, FFN_SUB):
            rows, wide = slice(r0, r0 + FFN_SUB), slice(r0, r0 + FFN_SUB + FFN_HALO)
            if r0 + FFN_SUB < tm:
                gc, vc = [c_ref[p, wide, :].astype(F32) for p in range(2)]
                d = d_ref[wide, :].astype(F32)
            else:
                gc, vc = [jnp.concatenate([c_ref[p, rows, :].astype(F32), cn_ref[p].astype(F32)], axis=0) for p in range(2)]
                d = jnp.concatenate([d_ref[rows, :].astype(F32), dn_ref[...].astype(F32) * keep_next], axis=0)
            sg = _sigmoid(gc)
            duc = (d * vc * (sg * (1.0 + gc * (1.0 - sg))), d * (gc * sg))
            dus = []
            for p in range(2):
                w = w_ref[p]
                own = duc[p][:FFN_SUB, :]
                adv = (pltpu.roll(duc[p], FFN_SUB + FFN_HALO - 2, 0)[:FFN_SUB, :],
                       pltpu.roll(duc[p], FFN_SUB + FFN_HALO - 1, 0)[:FFN_SUB, :], own)
                du = (w[2:3, :] * adv[2] + w[1:2, :] * adv[1] + w[0:1, :] * adv[0]).astype(BF16)
                du_ref[p, rows, :] = du
                stage_ref[slot, p, rows, :] = du
                ut = u_ref[p, rows, :].astype(F32)
                for k in range(3):
                    taps[p][k] = taps[p][k] + _colsum(adv[k] * ut)
                bias[p] = bias[p] + _colsum(own)

        for p in range(2):
            @pl.when(first)
            def _():
                db_ref[p, j] = bias[p]
                for k in range(3):
                    dw_ref[p, j, k:k + 1, :] = taps[p][k]

            @pl.when(jnp.logical_not(first))
            def _():
                db_ref[p, j] += bias[p] * live
                for k in range(3):
                    dw_ref[p, j, k:k + 1, :] += taps[p][k] * live

        @pl.when((jp == FF_NBLK - 1) & (t > 0))
        def _():
            dh_ref[...] = acc_ref[...].astype(BF16)

    return _pcall(
        body, (u4, cv4, cv4, dact, dact, w4, w_up.reshape(2, FF_NBLK, D, FF_BLK)), dep, name=name, grid=(nsteps + 1,),
        in_specs=[main, main, nxt, dmain, dnext, wsp, wup],
        out_specs=[main, all_w, all_b, pl.BlockSpec((tm, D), lambda t: (prev(t)[0], 0))],
        out_shape=[SDS(u4.shape, BF16), SDS((2, FF_NBLK, 3, FF_BLK), F32), SDS((2, FF_NBLK, 1, FF_BLK), F32), SDS((s, D), BF16)],
        scratch_shapes=[pltpu.VMEM((tm, D), F32), pltpu.VMEM((2, 2, tm, FF_BLK), BF16)], compiler_params=_cp(("arbitrary",)))


def _sum8(parts, name):
    _, r, c = parts[0].shape
    tr = r
    for cand in (512, 256, 128, 64, 32, 16):
        if r % cand == 0 and r > cand:
            tr = cand
            break
    nb = r // tr

    def body(*refs):
        o_ref = refs[-1]
        for l, p_ref in enumerate(refs[:-1]):
            @pl.when(pl.program_id(0) == l)
            def _():
                acc = p_ref[0].astype(F32)
                for j in range(1, N_DEV):
                    acc = acc + p_ref[j].astype(F32)
                o_ref[...] = acc

    def spec(l):
        return pl.BlockSpec((N_DEV, tr, c), lambda ll, i: (0, jnp.where(ll == l, i, jnp.where(ll < l, 0, nb - 1)), 0))

    return pl.pallas_call(
        body, name=name, grid=(len(parts), nb), in_specs=[spec(l) for l in range(len(parts))],
        out_specs=pl.BlockSpec((None, tr, c), lambda ll, i: (ll, i, 0)), out_shape=SDS((len(parts), r, c), F32),
        compiler_params=_cp(("arbitrary", "arbitrary")))(*parts)


def _sum8_small(parts, name):
    n = len(parts)

    def body(*refs):
        for p_ref, o_ref in zip(refs[:n], refs[n:]):
            acc = p_ref[0]
            for j in range(1, N_DEV):
                acc = acc + p_ref[j]
            o_ref[...] = acc

    return pl.pallas_call(body, name=name, out_shape=[SDS(p.shape[1:], F32) for p in parts], compiler_params=_cp())(*parts)


def _adamw_math(w, g, m, v):
    m = ADAM_B1 * m + (1.0 - ADAM_B1) * g
    v = ADAM_B2 * v + (1.0 - ADAM_B2) * (g * g)
    m_hat = m / (1.0 - ADAM_B1 ** ADAM_STEP)
    v_hat = v / (1.0 - ADAM_B2 ** ADAM_STEP)
    return -ADAM_LR * (m_hat / (jnp.sqrt(v_hat) + ADAM_EPS) + ADAM_WD * w), m, v


def _adamw(w, g, m, v, name):
    r, c = w.shape
    tr = r
    for cand in (256, 128, 64):
        if r % cand == 0 and r > cand:
            tr = cand
            break

    def body(w_ref, g_ref, m_ref, v_ref, d_ref, mo_ref, vo_ref):
        d_ref[...], mo_ref[...], vo_ref[...] = _adamw_math(w_ref[...], g_ref[...], m_ref[...], v_ref[...])

    blk = pl.BlockSpec((tr, c), lambda i: (i, 0))
    return pl.pallas_call(body, name=name, grid=(r // tr,), in_specs=[blk] * 4, out_specs=[blk] * 3,
                          out_shape=[SDS((r, c), F32)] * 3, compiler_params=_cp(("parallel",)))(w, g, m, v)


def _adamw_small(ws, gs, ms, vs, name):
    n = len(ws)

    def body(*refs):
        ins, outs = refs[:4 * n], refs[4 * n:]
        for i in range(n):
            d, m, v = _adamw_math(ins[i][...], ins[n + i][...], ins[2 * n + i][...], ins[3 * n + i][...])
            outs[i][...], outs[n + i][...], outs[2 * n + i][...] = d, m, v

    shapes = [SDS(w.shape, F32) for w in ws]
    res = pl.pallas_call(body, name=name, out_shape=shapes * 3, compiler_params=_cp())(*ws, *gs, *ms, *vs)
    return res[:n], res[n:2 * n], res[2 * n:]


def _perm_in(w):
    pad = jnp.zeros(w.shape[:-1] + (ZW - 2308,), w.dtype)
    return jnp.concatenate([w[..., :768], w[..., 772:], w[..., 768:772], pad], axis=-1)


def _unperm_in(g):
    return jnp.concatenate([g[..., :768], g[..., 2304:2308], g[..., 768:2304]], axis=-1)


def _wcat(sgu_w):
    return sgu_w.transpose(1, 0, 2).reshape(WIN, 4 * WIN)


def _layer_fwd(l, x, h1, mod, p, wg, last, target, nxt, w_in_next):
    s = x.shape[0]
    tag = f"_l{l}"
    mrow = lambda k: (mod, 6 * l + k)
    z = _mm_rows(h1, wg["w_in"].get(h1), "nn", ZW, BF16, "mm_z" + tag)
    fcol, frow = _fgate(z, p["bf"], "fgate" + tag, dep=wg["w_out"].prefetch(z))
    y_fox, lse_fox = _fox_fwd(z, fcol, frow, "fox_fwd" + tag)
    y_conv, hc = _conv_fwd(z, wg["conv_w"], p["conv_b"], p["conv_ln_g"], p["conv_ln_b"], wg["conv_pw_w"], p["conv_pw_b"],
                           "conv_fwd" + tag)
    y_swa, lse_swa = _swa_fwd(z, p["sinks"], "swa_fwd" + tag)
    y_sgu = _sgu_fwd(z, p["sgu_ln_g"], p["sgu_ln_b"], p["wcat"], p["bfull"], "sgu_fwd" + tag)
    ys = (y_fox, y_conv, y_swa, y_sgu)
    yn = _gnorm(ys, (p["g_group"], l), "gnorm" + tag)
    tok = wg["w_up"].prefetch(yn)
    o = _mm_rows(yn, wg["w_out"].get(yn), "nn", D, BF16, "mm_o" + tag)
    x1, h2 = _post(x, o, mrow(2), (p["g_post_mix"], l), (p["g_pre_ffn"], l), mrow(4), mrow(3), "post_mix" + tag, dep=tok)
    tok = wg["w_down"].prefetch(h2)
    u = _matmul(h2, wg["w_up"].get(h2), "nn", (N_DEV, s, FF_BLK), BF16, (N_DEV, 1, 1),
                _bs((s, D), lambda j, i, k: (0, 0)), _bs((None, D, FF_BLK), lambda j, i, k: (j, 0, 0)),
                _bs((None, s, FF_BLK), lambda j, i, k: (j, 0, 0)), None, "mm_u" + tag)
    u4 = u.reshape(2, FF_NBLK, s, FF_BLK)
    act, cv4 = _ffn_act(u4, wg["ffn_conv_w"], p["ffn_conv_b"], "ffn_act" + tag, dep=tok)
    tok = None if w_in_next is None else w_in_next.prefetch(act)
    f = _matmul(act, wg["w_down"].get(act), "nn", (s, D), BF16, (1, 1, FF_NBLK),
                _bs((None, s, FF_BLK), lambda i, j, k: (k, 0, 0)), _bs((FF_BLK, D), lambda i, j, k: (k, 0)),
                _bs((s, D), lambda i, j, k: (0, 0)), (s, D), "mm_f" + tag)
    if last:
        out = _post_loss(x1, f, mrow(5), (p["g_post_ffn"], l), target, "post_loss")
    else:
        out = _post(x1, f, mrow(5), (p["g_post_ffn"], l), *nxt, "post_ffn" + tag, dep=tok)
    saved = dict(x=x, h1=h1, z=z, fcol=fcol, frow=frow, lse_fox=lse_fox, hc=hc, lse_swa=lse_swa, ys=ys, yn=yn, o=o, x1=x1,
                 h2=h2, u4=u4, cv4=cv4, act=act, f=f)
    return out, saved


def _tie(a, token):
    return a if token is None else a + token[0, 0]


def _layer_bwd(l, dx2, sv, mod, p, wg, emit, dep=None):
    s = dx2.shape[0]
    tm = min(TM, s)
    tag = f"_l{l}"
    mrow = lambda k: (mod, 6 * l + k)
    g = {}
    df, g["ga2"], g["g_post_ffn"] = _post_bwd(dx2, sv["f"], mrow(5), (p["g_post_ffn"], l), "post_ffn_bwd" + tag, dep=dep)
    dact = _matmul(df, wg["w_down"].get(None), "nt", (FF_NBLK, s, FF_BLK), BF16, (FF_NBLK, 1, 1),
                   _bs((s, D), lambda j, i, k: (0, 0)), _bs((FF_BLK, D), lambda j, i, k: (j, 0)),
                   _bs((None, s, FF_BLK), lambda j, i, k: (j, 0, 0)), None, "mm_dact" + tag)
    tok = emit("w_down", _matmul(sv["act"], df, "tn", (FF_NBLK * FF_BLK, D), BF16, (FF_NBLK, 1, 1),
                                 _bs((None, s, FF_BLK), lambda j, i, k: (j, 0, 0)), _bs((s, D), lambda j, i, k: (0, 0)),
                                 _bs((FF_BLK, D), lambda j, i, k: (j, 0)), None, "mm_dwdown" + tag))
    du, g["ffn_conv_w"], g["ffn_conv_b"], dh2 = _ffn_bwd(sv["u4"], sv["cv4"], dact, wg["ffn_conv_w"], wg["w_up"].get(None),
                                                         "ffn_bwd" + tag, dep=tok)
    du = du.reshape(N_DEV, s, FF_BLK)
    tok = emit("w_up", _matmul(du, sv["h2"], "tn", (N_DEV, FF_BLK, D), BF16, (N_DEV, 1, 1),
                               _bs((None, s, FF_BLK), lambda j, i, k: (j, 0, 0)), _bs((s, D), lambda j, i, k: (0, 0)),
                               _bs((None, FF_BLK, D), lambda j, i, k: (j, 0, 0)), None, "mm_dwup" + tag))
    dx1, g["sh2"], g["sc2"], g["g_pre_ffn"] = _pre_bwd(dh2, sv["x1"], dx2, (p["g_pre_ffn"], l), mrow(4), "pre_ffn_bwd" + tag,
                                                       dep=tok)
    do, g["ga1"], g["g_post_mix"] = _post_bwd(dx1, sv["o"], mrow(2), (p["g_post_mix"], l), "post_mix_bwd" + tag)
    dyn = _mm_rows(do, wg["w_out"].get(None), "nt", D, BF16, "mm_dyn" + tag)
    tok = emit("w_out", _mm_wgrad(sv["yn"], do, BF16, "mm_dwout" + tag))
    dy_fox, dy_conv, dy_swa, dy_sgu, g["g_group"] = _gnorm_bwd(dyn, sv["ys"], (p["g_group"], l), "gnorm_bwd" + tag, dep=tok)
    z = sv["z"]
    dq_f, dk_f, dv_f, dfrow, dfcol = _fox_bwd(z, sv["fcol"], sv["frow"], sv["lse_fox"], sv["ys"][0], dy_fox, "fox_bwd" + tag)
    dgate, g["bf"] = _fgate_bwd(z, p["bf"], dfrow, dfcol, "fgate_bwd" + tag)
    dhc, g["conv_pw_w"], g["conv_pw_b"], g["conv_ln_g"], g["conv_ln_b"], g["conv_w"], g["conv_b"] = _conv_bwd_a(
        z, sv["hc"], dy_conv, wg["conv_w"], p["conv_ln_g"], p["conv_ln_b"], wg["conv_pw_w"], "conv_bwd_a" + tag)
    da_c, dg_c = _conv_bwd_b(z, dhc, wg["conv_w"], "conv_bwd_b" + tag)
    dq_s, dk_s, dv_s, g["sinks"] = _swa_bwd(z, p["sinks"], sv["lse_swa"], sv["ys"][2], dy_swa, "swa_bwd" + tag)
    du_g, dv_g, g["wcat"], g["sgu_bcol"], g["sgu_ln_g"], g["sgu_ln_b"] = _sgu_bwd(
        z, dy_sgu, p["sgu_ln_g"], p["sgu_ln_b"], p["wcat"], p["bfull"], "sgu_bwd" + tag)
    dz = jnp.concatenate([dq_f.astype(BF16), dk_f.astype(BF16), dv_f.astype(BF16), da_c, dg_c, dq_s.astype(BF16), dk_s.astype(BF16),
                          dv_s.astype(BF16), du_g, dv_g, dgate], axis=1)
    tok = emit("w_in", _mm_wgrad(sv["h1"], dz, BF16, "mm_dwin" + tag))
    dh1 = _mm_rows(dz, wg["w_in"].get(None), "nt", D, BF16, "mm_dh1" + tag)
    dx, g["sh1"], g["sc1"], g["g_pre_mix"] = _pre_bwd(dh1, sv["x"], dx1, (p["g_pre_mix"], l), mrow(1), "pre_mix_bwd" + tag,
                                                      dep=tok)
    return dx, g


def _layer_params(l, small, conv_w_full, conv_pw_full, ffn_conv_w_full):
    bf = jnp.pad(small["b_fgate"][l][None, :], ((0, 0), (0, 124)))
    p = dict(
        bf=bf, conv_b=small["conv_b"][l][None], conv_ln_g=small["conv_ln_g"][l][None], conv_ln_b=small["conv_ln_b"][l][None],
        conv_pw_b=small["conv_pw_b"][l][None], sinks=small["swa_sinks"][l], sgu_ln_g=small["sgu_ln_g"][l][None],
        sgu_ln_b=small["sgu_ln_b"][l][None], wcat=_wcat(small["sgu_w"][l]),
        bfull=jnp.repeat(small["sgu_b"][l].T, HD, axis=1),
        ffn_conv_b=small["ffn_conv_b"][l].reshape(2, FF_NBLK, 1, FF_BLK),
        g_group=small["g_group"].reshape(N_LAYER, 1, D), g_post_mix=small["g_post_mix"].reshape(N_LAYER, 1, D),
        g_pre_ffn=small["g_pre_ffn"].reshape(N_LAYER, 1, D), g_post_ffn=small["g_post_ffn"].reshape(N_LAYER, 1, D),
        g_pre_mix=small["g_pre_mix"].reshape(N_LAYER, 1, D))
    wsmall = dict(conv_w=conv_w_full[l], conv_pw_w=conv_pw_full[l].astype(BF16),
                  ffn_conv_w=ffn_conv_w_full[l].reshape(3, 2, FF_NBLK, FF_BLK).transpose(1, 2, 0, 3))
    return p, wsmall


def _local_step(x, target, mod, small, wbig, conv_w_full, conv_pw_full, ffn_conv_w_full, emit, on_loss=None):
    ps, wgs = [], []
    for l in range(N_LAYER):
        p, wsmall = _layer_params(l, small, conv_w_full, conv_pw_full, ffn_conv_w_full)
        ps.append(p)
        wgs.append({**wbig[l], **wsmall})
    h = _rms_mod(x, (ps[0]["g_pre_mix"], 0), (mod, 1), (mod, 0), "rms_mod_l0")
    saved = []
    for l in range(N_LAYER):
        last = l == N_LAYER - 1
        nxt = None if last else ((ps[l]["g_pre_mix"], l + 1), (mod, 6 * (l + 1) + 1), (mod, 6 * (l + 1)))
        out, sv = _layer_fwd(l, x, h, mod, ps[l], wgs[l], last, target, nxt, None if last else wgs[l + 1]["w_in"])
        saved.append(sv)
        if not last:
            x, h = out
    dx, loss = out
    dep = None if on_loss is None else on_loss(loss)
    grads = [None] * N_LAYER
    for l in reversed(range(N_LAYER)):
        dx, grads[l] = _layer_bwd(l, dx, saved[l], mod, ps[l], wgs[l], functools.partial(emit, l), dep)
        dep = None
    return loss, dx, grads


_SMALL = ("b_ada", "g_pre_mix", "g_post_mix", "g_pre_ffn", "g_post_ffn", "b_fgate", "conv_b", "conv_ln_g", "conv_ln_b",
          "conv_pw_b", "swa_sinks", "sgu_ln_g", "sgu_ln_b", "sgu_w", "sgu_b", "g_group", "ffn_conv_b")
_WEIGHTS = ("w_ada", "b_ada", "g_pre_mix", "g_post_mix", "g_pre_ffn", "g_post_ffn", "w_in", "b_fgate", "conv_w", "conv_b",
            "conv_ln_g", "conv_ln_b", "conv_pw_w", "conv_pw_b", "swa_sinks", "sgu_ln_g", "sgu_ln_b", "sgu_w", "sgu_b",
            "g_group", "w_out", "ffn_w_up", "ffn_conv_w", "ffn_conv_b", "ffn_w_down")


def _pad_rows(a, mult):
    r = (-a.shape[0]) % mult
    return a if r == 0 else jnp.concatenate([a, jnp.zeros((r,) + a.shape[1:], a.dtype)], axis=0)


def _view2d(a):
    if a.ndim == 2:
        return a
    return a.reshape(-1, a.shape[-1])


def kernel(x, c, w_ada, b_ada, g_pre_mix, g_post_mix, g_pre_ffn, g_post_ffn, w_in, b_fgate, conv_w, conv_b, conv_ln_g, conv_ln_b, conv_pw_w, conv_pw_b, swa_sinks, sgu_ln_g, sgu_ln_b, sgu_w, sgu_b, g_group, w_out, ffn_w_up, ffn_conv_w, ffn_conv_b, ffn_w_down, loss_target, m_w_ada, m_b_ada, m_g_pre_mix, m_g_post_mix, m_g_pre_ffn, m_g_post_ffn, m_w_in, m_b_fgate, m_conv_w, m_conv_b, m_conv_ln_g, m_conv_ln_b, m_conv_pw_w, m_conv_pw_b, m_swa_sinks, m_sgu_ln_g, m_sgu_ln_b, m_sgu_w, m_sgu_b, m_g_group, m_w_out, m_ffn_w_up, m_ffn_conv_w, m_ffn_conv_b, m_ffn_w_down, v_w_ada, v_b_ada, v_g_pre_mix, v_g_post_mix, v_g_pre_ffn, v_g_post_ffn, v_w_in, v_b_fgate, v_conv_w, v_conv_b, v_conv_ln_g, v_conv_ln_b, v_conv_pw_w, v_conv_pw_b, v_swa_sinks, v_sgu_ln_g, v_sgu_ln_b, v_sgu_w, v_sgu_b, v_g_group, v_w_out, v_ffn_w_up, v_ffn_conv_w, v_ffn_conv_b, v_ffn_w_down):
    env = dict(locals())
    w = {n: env[n] for n in _WEIGHTS}
    mom = {n: env["m_" + n] for n in _WEIGHTS}
    var = {n: env["v_" + n] for n in _WEIGHTS}
    me = 4 * lax.axis_index("x") + 2 * lax.axis_index("y") + lax.axis_index("c")
    x2, target = x[0], loss_target[0]

    (c_all,) = _exchange([c], ["bcast"], "gather_c")
    c_all = c_all.reshape(N_DEV, D)
    (m_all,) = _exchange([_ada_fwd(c_all, w_ada)], ["bcast"], "gather_mod")
    m_mine = lax.dynamic_index_in_dim(m_all, me, axis=2, keepdims=False)
    mod, mod_token = _ada_finish(m_mine.transpose(1, 0, 2).reshape(N_LAYER, 6 * D), b_ada)
    mod = mod.reshape(6 * N_LAYER, 1, D)

    shards = [_tie(conv_w, mod_token), conv_pw_w, ffn_conv_w]
    for l in range(N_LAYER):
        shards += [_perm_in(w_in[l]).astype(BF16), w_out[l].astype(BF16), ffn_w_up[l].astype(BF16), ffn_w_down[l].astype(BF16)]
    gather = _g2_start(shards, "gather_weights_start")
    mod = _tie(mod, gather.token)
    _g2_relay(gather, [0, 1, 2, 3], mod, "gather_relay_first")
    g_cw, g_pw, g_fcw = _g2_wait(gather, [0, 1, 2], mod, "gather_small_wait")
    conv_w_full = g_cw.transpose(1, 2, 0, 3).reshape(N_LAYER, CONV_K, GW)
    conv_pw_full = g_pw.transpose(1, 0, 2, 3).reshape(N_LAYER, GW, GW)
    ffn_conv_w_full = g_fcw.transpose(1, 2, 0, 3).reshape(N_LAYER, 3, N_DEV * FF_BLK)

    def lazy(i, shape, key):
        pre = None if i == 3 else (lambda after: _g2_relay(gather, [i], after, "relay_" + key))
        return _Lazy(lambda after: _g2_wait(gather, [i], after, "wait_" + key)[0].reshape(shape), pre)

    wbig = [dict(w_in=lazy(3 + 4 * l, (D, ZW), f"w_in_l{l}"), w_out=lazy(4 + 4 * l, (D, D), f"w_out_l{l}"),
                 w_up=lazy(5 + 4 * l, (N_DEV, D, FF_BLK), f"w_up_l{l}"),
                 w_down=lazy(6 + 4 * l, (FF_NBLK * FF_BLK, D), f"w_down_l{l}")) for l in range(N_LAYER)]

    grad_flights = []

    def emit(l, key, arr):
        fl = _xchg_start([arr.reshape(N_DEV, -1, arr.shape[-1])], ["a2a"], f"grad_start_{key}_l{l}")
        grad_flights.append(((l, key), fl))
        return fl.token

    small = {n: w[n] for n in _SMALL}
    total = []

    def on_loss(loss8):
        total.append(lax.psum(loss8[0, 0], ("x", "y", "c")))
        return total[0].reshape(1, 1)

    _, dx, grads = _local_step(x2, target, mod, small, wbig, conv_w_full, conv_pw_full, ffn_conv_w_full, emit, on_loss)
    loss = total[0]
    grad_x = dx[None]


    st = lambda key: jnp.stack([grads[l][key] for l in range(N_LAYER)])
    d_conv_w = st("conv_w")[:, :CONV_K, :].reshape(N_LAYER, CONV_K, N_DEV, GW // N_DEV).transpose(2, 0, 1, 3)
    d_pw_w = st("conv_pw_w").reshape(N_LAYER, N_DEV, GW // N_DEV, GW).transpose(1, 0, 2, 3)
    d_fcw = st("ffn_conv_w").reshape(N_LAYER, N_DEV, 3, FF_BLK).transpose(1, 0, 2, 3)
    rows_d = _pad_rows(jnp.concatenate(
        [grads[l][k] for l in range(N_LAYER) for k in ("sh1", "sc1", "ga1", "sh2", "sc2", "ga2")]
        + [grads[l][k] for k in ("g_pre_mix", "g_post_mix", "g_pre_ffn", "g_post_ffn", "g_group") for l in range(N_LAYER)],
        axis=0), 8)
    rows_gw = _pad_rows(jnp.concatenate(
        [grads[l][k] for k in ("conv_b", "conv_ln_g", "conv_ln_b", "conv_pw_b", "sgu_ln_g", "sgu_ln_b") for l in range(N_LAYER)],
        axis=0), 8)
    rows_128 = jnp.concatenate([_pad_rows(jnp.concatenate([grads[l]["bf"] for l in range(N_LAYER)]
                                                          + [grads[l]["sinks"] for l in range(N_LAYER)], axis=0), 8)]
                               + [grads[l]["sgu_bcol"] for l in range(N_LAYER)], axis=0)
    rows_w = jnp.concatenate([grads[l]["wcat"] for l in range(N_LAYER)], axis=0)
    rows_fb = st("ffn_conv_b").reshape(N_LAYER * N_DEV, FF_BLK)
    small_flight = _xchg_start([d_conv_w, d_pw_w, d_fcw, rows_d, rows_gw, rows_128, rows_w, rows_fb],
                               ["a2a"] * 3 + ["bcast"] * 5, "small_grads_start")

    to_mem = {"w_in": lambda a: a.transpose(2, 0, 1), "ffn_w_up": lambda a: a.transpose(0, 2, 1)}
    from_mem = {"w_in": lambda a: a.transpose(1, 2, 0), "ffn_w_up": lambda a: a.transpose(0, 2, 1)}
    flights = dict(grad_flights)
    gr, delta, new_m, new_v = {}, {}, {}, {}

    def adamw_big(n):
        view, back = to_mem.get(n, lambda a: a), from_mem.get(n, lambda a: a)
        shape = view(w[n]).shape
        d, m2, v2 = _adamw(_view2d(view(w[n])), _view2d(gr[n]), _view2d(view(mom[n])), _view2d(view(var[n])), "adamw_" + n)
        delta[n], new_m[n], new_v[n] = back(d.reshape(shape)), back(m2.reshape(shape)), back(v2.reshape(shape))
        gr[n] = back(gr[n].reshape(shape))

    after = small_flight.token
    for n, key in (("ffn_w_down", "w_down"), ("ffn_w_up", "w_up"), ("w_out", "w_out"), ("w_in", "w_in")):
        parts = [_xchg_wait(flights[(l, key)], [0], after, f"grad_wait_{key}_l{l}")[0] for l in reversed(range(N_LAYER))]
        g = _sum8(parts[::-1], "sum_" + key)
        gr[n] = to_mem["w_in"](_unperm_in(g)) if n == "w_in" else g
        adamw_big(n)
        after = new_v[n]

    small_parts = _xchg_wait(small_flight, list(range(8)), after, "small_grads_wait")
    s_conv_w, s_pw_w, s_fcw, s_d, s_gw, s_128, s_w, s_fb = _sum8_small(
        [p.reshape(N_DEV, -1, p.shape[-1]) for p in small_parts], "sum_small_grads")
    gr["conv_w"] = s_conv_w.reshape(N_LAYER, CONV_K, GW // N_DEV)
    gr["conv_pw_w"] = s_pw_w.reshape(N_LAYER, GW // N_DEV, GW)
    gr["ffn_conv_w"] = s_fcw.reshape(N_LAYER, 3, FF_BLK)
    gr["b_ada"] = s_d[:6 * N_LAYER].reshape(N_LAYER, 6 * D)
    for i, k in enumerate(("g_pre_mix", "g_post_mix", "g_pre_ffn", "g_post_ffn", "g_group")):
        gr[k] = s_d[6 * N_LAYER + 2 * i:6 * N_LAYER + 2 * i + 2]
    for i, k in enumerate(("conv_b", "conv_ln_g", "conv_ln_b", "conv_pw_b", "sgu_ln_g", "sgu_ln_b")):
        gr[k] = s_gw[2 * i:2 * i + 2]
    gr["b_fgate"] = s_128[0:2, :4]
    gr["swa_sinks"] = s_128[2:4, :4]
    gr["sgu_b"] = s_128[8:].reshape(N_LAYER, WIN, 128)[:, :, :4].transpose(0, 2, 1)
    gr["sgu_w"] = s_w.reshape(N_LAYER, WIN, 4, WIN).transpose(0, 2, 1, 3)
    gr["ffn_conv_b"] = s_fb.reshape(N_LAYER, N_DEV * FF_BLK)
    dmod_all = small_parts[3][:, :6 * N_LAYER, :].reshape(N_DEV, N_LAYER, 6 * D)
    ncol = 6 * D // N_DEV
    dmod_cols = lax.dynamic_slice_in_dim(dmod_all, me * ncol, ncol, axis=2).transpose(1, 0, 2)
    gr["w_ada"] = _ada_bwd(c_all, dmod_cols)

    adamw_big("w_ada")
    smalls = [n for n in _WEIGHTS if n not in ("w_ada", "w_in", "w_out", "ffn_w_up", "ffn_w_down")]
    ds, ms, vs = _adamw_small([_view2d(w[n]) for n in smalls], [_view2d(gr[n]) for n in smalls],
                              [_view2d(mom[n]) for n in smalls], [_view2d(var[n]) for n in smalls], "adamw_small")
    for i, n in enumerate(smalls):
        delta[n], new_m[n], new_v[n] = ds[i].reshape(w[n].shape), ms[i].reshape(w[n].shape), vs[i].reshape(w[n].shape)

    return (loss, grad_x, *[gr[n].reshape(w[n].shape) for n in _WEIGHTS], *[delta[n] for n in _WEIGHTS],
            *[new_m[n] for n in _WEIGHTS], *[new_v[n] for n in _WEIGHTS])
```

```python
import functools

import jax
import jax.numpy as jnp
from jax import lax
from jax.experimental import pallas as pl
from jax.experimental.pallas import tpu as pltpu

F32, BF16 = jnp.float32, jnp.bfloat16
SDS = jax.ShapeDtypeStruct
MESH = pl.DeviceIdType.MESH

N_DEV = 8
D = 1024
GW = 256
HD = 64
N_LAYER = 2
ZW = 2432
FF_BLK = 704
FF_NBLK = 4
CONV_K = 31
CONV_HALO = 32
FFN_HALO = 16
FFN_SUB = 128
EPS = 1e-6
NEG = -1e30
SCALE = HD ** -0.5
VMEM_LIMIT_V7X = 56 * 1024 * 1024
TM = 512
WGRAD_ROWS = 256
TQ = 256
FOX_FWD_PAIR = 2
FOX_BWD_PATHS = (3, 5)
WIN = 128

ADAM_LR, ADAM_B1, ADAM_B2, ADAM_EPS, ADAM_WD, ADAM_STEP = 0.001, 0.9, 0.999, 1e-08, 0.01, 10

Z_FQ, Z_FK, Z_FV, Z_CA, Z_CG, Z_SQ = 0, 1, 2, 3, 4, 5
Z_SK, Z_SV = 12, 13
Z_GU, Z_GV = 7, 8
Z_FG = 18


def _cp(sem=None):
    return pltpu.CompilerParams(dimension_semantics=sem, vmem_limit_bytes=VMEM_LIMIT_V7X)


def _vec(arr3, idx, ngrid):
    w = arr3.shape[-1]
    if ngrid == 1:
        return pl.BlockSpec((None, 1, w), lambda i: (idx, 0, 0))
    return pl.BlockSpec((None, 1, w), lambda i, j: (idx, 0, 0))


def _sigmoid(x):
    return jax.nn.sigmoid(x)


def _silu(x):
    return x * _sigmoid(x)


def _dsilu(x):
    s = _sigmoid(x)
    return s * (1.0 + x * (1.0 - s))


_G0, _G1 = 0.7978845608028654, 0.044715


def _gelu(x):
    return 0.5 * x * (1.0 + jnp.tanh(_G0 * (x + _G1 * x * x * x)))


def _dgelu(x):
    t = jnp.tanh(_G0 * (x + _G1 * x * x * x))
    return 0.5 * (1.0 + t) + 0.5 * x * (1.0 - t * t) * (_G0 * (1.0 + 3.0 * _G1 * x * x))


def _rstd(x):
    return lax.rsqrt(jnp.mean(x * x, axis=-1, keepdims=True) + EPS)


def _rms_bwd(xh, r, t):
    return r * (t - xh * jnp.mean(t * xh, axis=-1, keepdims=True))


def _ln_stats(x):
    mu = jnp.mean(x, axis=-1, keepdims=True)
    xc = x - mu
    rstd = lax.rsqrt(jnp.mean(xc * xc, axis=-1, keepdims=True) + EPS)
    return xc * rstd, rstd


def _ln_bwd(xh, rstd, dxh):
    return rstd * (dxh - jnp.mean(dxh, axis=-1, keepdims=True) - xh * jnp.mean(dxh * xh, axis=-1, keepdims=True))


def _colsum(x):
    return jnp.sum(x, axis=0, keepdims=True)


def _dot(a, b, kind):
    dn = {"nn": (((1,), (0,)), ((), ())), "nt": (((1,), (1,)), ((), ())), "tn": (((0,), (0,)), ((), ()))}[kind]
    return lax.dot_general(a.astype(BF16), b.astype(BF16), dn, preferred_element_type=F32)


def _exchange(arrs, modes, name):
    n = len(arrs)
    outs = [SDS((N_DEV,) + a.shape, a.dtype) if m == "bcast" else SDS(a.shape, a.dtype) for a, m in zip(arrs, modes)]

    def body(*refs):
        ins, dst = refs[:n], refs[n:2 * n]
        send, recv, loc = refs[2 * n:]
        x, y, c = lax.axis_index("x"), lax.axis_index("y"), lax.axis_index("c")
        me = 4 * x + 2 * y + c

        def src(a, j):
            return ins[a] if modes[a] == "bcast" else ins[a].at[j]

        local = [pltpu.make_async_copy(src(a, me), dst[a].at[me], loc.at[a]) for a in range(n)]
        for cp in local:
            cp.start()
        sent, landed = [], []
        for k in (2, 4, 6, 3, 5, 7, 1):
            px = 1 - x if k & 4 else x
            py = 1 - y if k & 2 else y
            pc = 1 - c if k & 1 else c
            peer = 4 * px + 2 * py + pc
            for a in range(n):
                cp = pltpu.make_async_remote_copy(src_ref=src(a, peer), dst_ref=dst[a].at[me], send_sem=send.at[a, k - 1],
                                                  recv_sem=recv.at[a, k - 1], device_id=(px, py, pc), device_id_type=MESH)
                cp.start()
                sent.append(cp)
                landed.append(pltpu.make_async_remote_copy(src_ref=src(a, peer), dst_ref=dst[a].at[peer],
                                                           send_sem=send.at[a, k - 1], recv_sem=recv.at[a, k - 1],
                                                           device_id=(px, py, pc), device_id_type=MESH))
        for cp in landed:
            cp.wait_recv()
        for cp in sent:
            cp.wait_send()
        for cp in local:
            cp.wait()

    hbm = pl.BlockSpec(memory_space=pltpu.HBM)
    return pl.pallas_call(
        body, name=name, out_shape=outs, in_specs=[hbm] * n, out_specs=[hbm] * n,
        scratch_shapes=[pltpu.SemaphoreType.DMA((n, N_DEV - 1)), pltpu.SemaphoreType.DMA((n, N_DEV - 1)),
                        pltpu.SemaphoreType.DMA((n,))],
        compiler_params=pltpu.CompilerParams(has_side_effects=True),
    )(*arrs)


_PEER_ORDER = (2, 4, 6, 3, 5, 7, 1)
_HBM = pl.BlockSpec(memory_space=pltpu.HBM)
_SEM = pl.BlockSpec(memory_space=pltpu.SEMAPHORE)
_EFFECT = pltpu.SideEffectType.DATAFLOW_SIDE_EFFECTING


def _peer(k):
    x, y, c = lax.axis_index("x"), lax.axis_index("y"), lax.axis_index("c")
    px = 1 - x if k & 4 else x
    py = 1 - y if k & 2 else y
    pc = 1 - c if k & 1 else c
    return (px, py, pc), 4 * px + 2 * py + pc


def _my_id():
    return 4 * lax.axis_index("x") + 2 * lax.axis_index("y") + lax.axis_index("c")


def _split_copies(src_ref, land_ref, send, recv, loc, mode):
    me = _my_id()
    pick = (lambda j: src_ref) if mode == "bcast" else (lambda j: src_ref.at[j])
    local = pltpu.make_async_copy(pick(me), land_ref.at[me], loc)
    remote = []
    for k in _PEER_ORDER:
        dev, peer = _peer(k)
        out = pltpu.make_async_remote_copy(src_ref=pick(peer), dst_ref=land_ref.at[me], send_sem=send.at[k - 1],
                                           recv_sem=recv.at[k - 1], device_id=dev, device_id_type=MESH)
        arrive = pltpu.make_async_remote_copy(src_ref=pick(peer), dst_ref=land_ref.at[peer], send_sem=send.at[k - 1],
                                              recv_sem=recv.at[k - 1], device_id=dev, device_id_type=MESH)
        remote.append((out, arrive))
    return local, remote


class _Flight:
    def __init__(self, srcs, lands, sends, recvs, locs, modes, token):
        self.srcs, self.lands, self.sends, self.recvs, self.locs, self.modes, self.token = (
            list(srcs), list(lands), list(sends), list(recvs), list(locs), list(modes), token)


def _xchg_start(arrs, modes, name):
    n = len(arrs)
    lands = [lax.empty((N_DEV,) + a.shape if m == "bcast" else a.shape, a.dtype) for a, m in zip(arrs, modes)]

    def body(*refs):
        srcs, lnds = refs[:n], refs[n:2 * n]
        outs = refs[2 * n:]
        sends, recvs, locs, token = outs[:n], outs[n:2 * n], outs[2 * n:3 * n], outs[5 * n]
        for a in range(n):
            local, remote = _split_copies(srcs[a], lnds[a], sends[a], recvs[a], locs[a], modes[a])
            local.start()
            for out, _ in remote:
                out.start()
        token[...] = jnp.zeros_like(token)

    sem7 = pltpu.SemaphoreType.DMA((N_DEV - 1,))
    res = pl.pallas_call(
        body, name=name,
        out_shape=[sem7] * (2 * n) + [pltpu.SemaphoreType.DMA(())] * n + [pltpu.HBM(a.shape, a.dtype) for a in arrs]
        + [pltpu.HBM(b.shape, b.dtype) for b in lands] + [SDS((8, 128), F32)],
        in_specs=[_HBM] * (2 * n), out_specs=[_SEM] * (3 * n) + [_HBM] * (2 * n) + [pl.BlockSpec(memory_space=pltpu.VMEM)],
        input_output_aliases={i: 3 * n + i for i in range(2 * n)},
        compiler_params=pltpu.CompilerParams(has_side_effects=_EFFECT),
    )(*[pltpu.with_memory_space_constraint(a, pltpu.HBM) for a in arrs],
      *[pltpu.with_memory_space_constraint(b, pltpu.HBM) for b in lands])
    return _Flight(res[3 * n:4 * n], res[4 * n:5 * n], res[:n], res[n:2 * n], res[2 * n:3 * n], modes, res[5 * n])


def _xchg_wait(flight, idx, after, name):
    n = len(idx)
    modes = [flight.modes[i] for i in idx]

    def body(*refs):
        srcs, lnds = refs[:n], refs[n:2 * n]
        sends, recvs, locs = refs[2 * n:3 * n], refs[3 * n:4 * n], refs[4 * n:5 * n]
        for a in range(n):
            local, remote = _split_copies(srcs[a], lnds[a], sends[a], recvs[a], locs[a], modes[a])
            local.wait()
            for _, arrive in remote:
                arrive.wait_send()
                arrive.wait_recv()

    ops = ([flight.srcs[i] for i in idx] + [flight.lands[i] for i in idx] + [flight.sends[i] for i in idx]
           + [flight.recvs[i] for i in idx] + [flight.locs[i] for i in idx])
    res = pl.pallas_call(
        body, name=name, out_shape=[pltpu.HBM(o.shape, o.dtype) for o in ops[:2 * n]],
        in_specs=[_HBM] * (2 * n) + [_SEM] * (3 * n) + [pl.BlockSpec(memory_space=pl.ANY)], out_specs=[_HBM] * (2 * n),
        input_output_aliases={i: i for i in range(2 * n)},
        compiler_params=pltpu.CompilerParams(has_side_effects=_EFFECT),
    )(*ops, after)
    return res[n:]


class _Lazy:
    def __init__(self, fn, pre=None):
        self.fn, self.pre, self.val, self.started = fn, pre, None, False

    def prefetch(self, after):
        token = self.pre(after) if self.pre is not None and not self.started else None
        self.started = True
        return token

    def get(self, after):
        self.prefetch(after)
        if self.val is None:
            self.val = self.fn(after)
        return self.val


_CHIP_PEERS = (2, 4, 6)


def _g2_copies_a(src_ref, land_ref, send, recv, loc):
    me = _my_id()
    local = pltpu.make_async_copy(src_ref, land_ref.at[me], loc)
    remote = []
    for j, k in enumerate(_CHIP_PEERS + (1,)):
        dev, peer = _peer(k)
        out = pltpu.make_async_remote_copy(src_ref=src_ref, dst_ref=land_ref.at[me], send_sem=send.at[j], recv_sem=recv.at[j],
                                           device_id=dev, device_id_type=MESH)
        arrive = pltpu.make_async_remote_copy(src_ref=src_ref, dst_ref=land_ref.at[peer], send_sem=send.at[j],
                                              recv_sem=recv.at[j], device_id=dev, device_id_type=MESH)
        remote.append((out, arrive))
    return local, remote


def _g2_copies_b(land_ref, send, recv):
    sib, _ = _peer(1)
    pairs = []
    for j, k in enumerate(_CHIP_PEERS):
        _, same_core = _peer(k)
        _, other_core = _peer(k | 1)
        out = pltpu.make_async_remote_copy(src_ref=land_ref.at[same_core], dst_ref=land_ref.at[same_core], send_sem=send.at[j],
                                           recv_sem=recv.at[j], device_id=sib, device_id_type=MESH)
        arrive = pltpu.make_async_remote_copy(src_ref=land_ref.at[same_core], dst_ref=land_ref.at[other_core],
                                              send_sem=send.at[j], recv_sem=recv.at[j], device_id=sib, device_id_type=MESH)
        pairs.append((out, arrive))
    return pairs


class _Gather2:
    def __init__(self, srcs, lands, sends, recvs, locs, token):
        self.srcs, self.lands, self.sends, self.recvs, self.locs, self.token = (
            list(srcs), list(lands), list(sends), list(recvs), list(locs), token)
        self.sends_b, self.recvs_b = [None] * len(self.srcs), [None] * len(self.srcs)


def _g2_start(arrs, name):
    n = len(arrs)
    lands = [lax.empty((N_DEV,) + a.shape, a.dtype) for a in arrs]

    def body(*refs):
        srcs, lnds = refs[:n], refs[n:2 * n]
        outs = refs[2 * n:]
        sends, recvs, locs, token = outs[:n], outs[n:2 * n], outs[2 * n:3 * n], outs[5 * n]
        for a in range(n):
            local, remote = _g2_copies_a(srcs[a], lnds[a], sends[a], recvs[a], locs[a])
            local.start()
            for out, _ in remote:
                out.start()
        token[...] = jnp.zeros_like(token)

    sem4 = pltpu.SemaphoreType.DMA((4,))
    res = pl.pallas_call(
        body, name=name,
        out_shape=[sem4] * (2 * n) + [pltpu.SemaphoreType.DMA(())] * n + [pltpu.HBM(a.shape, a.dtype) for a in arrs]
        + [pltpu.HBM(b.shape, b.dtype) for b in lands] + [SDS((8, 128), F32)],
        in_specs=[_HBM] * (2 * n), out_specs=[_SEM] * (3 * n) + [_HBM] * (2 * n) + [pl.BlockSpec(memory_space=pltpu.VMEM)],
        input_output_aliases={i: 3 * n + i for i in range(2 * n)},
        compiler_params=pltpu.CompilerParams(has_side_effects=_EFFECT),
    )(*[pltpu.with_memory_space_constraint(a, pltpu.HBM) for a in arrs],
      *[pltpu.with_memory_space_constraint(b, pltpu.HBM) for b in lands])
    return _Gather2(res[3 * n:4 * n], res[4 * n:5 * n], res[:n], res[n:2 * n], res[2 * n:3 * n], res[5 * n])


def _g2_relay(g, idx, after, name):
    n = len(idx)

    def body(*refs):
        srcs, lnds = refs[:n], refs[n:2 * n]
        sends, recvs, locs = refs[2 * n:3 * n], refs[3 * n:4 * n], refs[4 * n:5 * n]
        outs = refs[5 * n + 1:]
        sends_b, recvs_b = outs[2 * n:3 * n], outs[3 * n:4 * n]
        for a in range(n):
            local, remote = _g2_copies_a(srcs[a], lnds[a], sends[a], recvs[a], locs[a])
            local.wait()
            for _, arrive in remote:
                arrive.wait_send()
                arrive.wait_recv()
        for a in range(n):
            for out, _ in _g2_copies_b(lnds[a], sends_b[a], recvs_b[a]):
                out.start()
        outs[4 * n][...] = jnp.zeros_like(outs[4 * n])

    ops = ([g.srcs[i] for i in idx] + [g.lands[i] for i in idx] + [g.sends[i] for i in idx] + [g.recvs[i] for i in idx]
           + [g.locs[i] for i in idx])
    sem3 = pltpu.SemaphoreType.DMA((3,))
    res = pl.pallas_call(
        body, name=name, out_shape=[pltpu.HBM(o.shape, o.dtype) for o in ops[:2 * n]] + [sem3] * (2 * n) + [SDS((8, 128), F32)],
        in_specs=[_HBM] * (2 * n) + [_SEM] * (3 * n) + [pl.BlockSpec(memory_space=pl.ANY)],
        out_specs=[_HBM] * (2 * n) + [_SEM] * (2 * n) + [pl.BlockSpec(memory_space=pltpu.VMEM)],
        input_output_aliases={i: i for i in range(2 * n)},
        compiler_params=pltpu.CompilerParams(has_side_effects=_EFFECT),
    )(*ops, after)
    for a, i in enumerate(idx):
        g.srcs[i], g.lands[i] = res[a], res[n + a]
        g.sends_b[i], g.recvs_b[i] = res[2 * n + a], res[3 * n + a]
    return res[4 * n]


def _g2_wait(g, idx, after, name):
    n = len(idx)

    def body(*refs):
        lnds, sends_b, recvs_b = refs[:n], refs[n:2 * n], refs[2 * n:3 * n]
        for a in range(n):
            for _, arrive in _g2_copies_b(lnds[a], sends_b[a], recvs_b[a]):
                arrive.wait_send()
                arrive.wait_recv()

    ops = [g.lands[i] for i in idx] + [g.sends_b[i] for i in idx] + [g.recvs_b[i] for i in idx]
    res = pl.pallas_call(
        body, name=name, out_shape=[pltpu.HBM(o.shape, o.dtype) for o in ops[:n]],
        in_specs=[_HBM] * n + [_SEM] * (2 * n) + [pl.BlockSpec(memory_space=pl.ANY)], out_specs=[_HBM] * n,
        input_output_aliases={i: i for i in range(n)},
        compiler_params=pltpu.CompilerParams(has_side_effects=_EFFECT),
    )(*ops, after)
    return list(res)


def _matmul(a, b, kind, out_shape, out_dtype, grid, a_spec, b_spec, o_spec, acc_shape, name):
    nk = grid[2]

    def body(a_ref, b_ref, o_ref, *scratch):
        prod = _dot(a_ref[...], b_ref[...], kind)
        if nk == 1:
            o_ref[...] = prod.astype(out_dtype)
        else:
            acc = scratch[0]
            k = pl.program_id(2)

            @pl.when(k == 0)
            def _():
                acc[...] = prod

            @pl.when(k > 0)
            def _():
                acc[...] += prod

            @pl.when(k == nk - 1)
            def _():
                o_ref[...] = acc[...].astype(out_dtype)

    return pl.pallas_call(
        body, name=name, grid=grid, in_specs=[a_spec, b_spec], out_specs=o_spec, out_shape=SDS(out_shape, out_dtype),
        scratch_shapes=[] if nk == 1 else [pltpu.VMEM(acc_shape, F32)],
        compiler_params=_cp(("parallel", "parallel", "arbitrary")))(a, b)


def _bs(shape, fn):
    return pl.BlockSpec(shape, fn)


def _mm_rows(a, w, kind, n_out, out_dtype, name):
    s, k = a.shape
    tm = min(TM, s)
    return _matmul(a, w, kind, (s, n_out), out_dtype, (s // tm, 1, 1),
                   _bs((tm, k), lambda i, j, kk: (i, 0)), _bs(w.shape, lambda i, j, kk: (0, 0)),
                   _bs((tm, n_out), lambda i, j, kk: (i, 0)), None, name)


def _mm_wgrad(a, dy, out_dtype, name):
    s, k = a.shape
    n = dy.shape[1]
    tko = min(WGRAD_ROWS, k)
    return _matmul(a, dy, "tn", (k, n), out_dtype, (k // tko, 1, 1),
                   _bs((s, tko), lambda i, j, kk: (0, i)), _bs((s, n), lambda i, j, kk: (0, 0)),
                   _bs((tko, n), lambda i, j, kk: (i, 0)), None, name)


def _ada_fwd(c_all, w_ada):
    ncol = w_ada.shape[2]

    def body(c_ref, w_ref, o_ref):
        ca = _silu(c_ref[...])
        ca = jnp.concatenate([ca, jnp.zeros_like(ca)], axis=0)
        o_ref[...] = _dot(ca, w_ref[...], "nn")[:N_DEV, :]

    return pl.pallas_call(
        body, name="ada_fwd", grid=(N_LAYER,),
        in_specs=[pl.BlockSpec((N_DEV, D), lambda l: (0, 0)), pl.BlockSpec((None, D, ncol), lambda l: (l, 0, 0))],
        out_specs=pl.BlockSpec((None, N_DEV, ncol), lambda l: (l, 0, 0)),
        out_shape=SDS((N_LAYER, N_DEV, ncol), F32), compiler_params=_cp(("parallel",)))(c_all, w_ada)


def _ada_finish(m_mine, b_ada):
    def body(m_ref, b_ref, o_ref, t_ref):
        o_ref[...] = m_ref[...] + b_ref[...]
        t_ref[...] = jnp.zeros_like(t_ref)

    return pl.pallas_call(body, name="ada_finish", out_shape=[SDS(b_ada.shape, F32), SDS((8, 128), F32)])(m_mine, b_ada)


def _ada_bwd(c_all, dmod_cols):
    ncol = dmod_cols.shape[2]

    def body(c_ref, d_ref, o_ref):
        ca = _silu(c_ref[...])
        ca = jnp.concatenate([ca, jnp.zeros_like(ca)], axis=0)
        dm = d_ref[...]
        dm = jnp.concatenate([dm, jnp.zeros_like(dm)], axis=0)
        o_ref[...] = _dot(ca, dm, "tn")

    return pl.pallas_call(
        body, name="ada_bwd", grid=(N_LAYER,),
        in_specs=[pl.BlockSpec((N_DEV, D), lambda l: (0, 0)), pl.BlockSpec((None, N_DEV, ncol), lambda l: (l, 0, 0))],
        out_specs=pl.BlockSpec((None, D, ncol), lambda l: (l, 0, 0)),
        out_shape=SDS((N_LAYER, D, ncol), F32), compiler_params=_cp(("parallel",)))(c_all, dmod_cols)


def _rows(s):
    tm = min(TM, s)
    return tm, pl.BlockSpec((tm, D), lambda i: (i, 0))


def _pcall(body, operands, dep, **kw):
    if dep is None:
        return pl.pallas_call(body, **kw)(*operands)
    n = len(operands)

    def body_dep(*refs):
        body(*refs[:n], *refs[n + 1:])

    kw["in_specs"] = list(kw["in_specs"]) + [pl.BlockSpec(memory_space=pl.ANY)]
    return pl.pallas_call(body_dep, **kw)(*operands, dep)


def _rms_mod(x, g, sc, sh, name):
    s = x.shape[0]
    tm, row = _rows(s)

    def body(x_ref, g_ref, sc_ref, sh_ref, h_ref):
        xf = x_ref[...]
        h_ref[...] = (xf * _rstd(xf) * (g_ref[...] * (1.0 + sc_ref[...])) + sh_ref[...]).astype(BF16)

    return pl.pallas_call(
        body, name=name, grid=(s // tm,), in_specs=[row, _vec(*g, 1), _vec(*sc, 1), _vec(*sh, 1)], out_specs=row,
        out_shape=SDS((s, D), BF16), compiler_params=_cp(("parallel",)))(x, g[0], sc[0], sh[0])


def _post(xres, o, ga, gpost, gn, scn, shn, name, dep=None):
    s = xres.shape[0]
    tm, row = _rows(s)

    def body(x_ref, o_ref, ga_ref, gp_ref, gn_ref, sc_ref, sh_ref, xn_ref, h_ref):
        of = o_ref[...].astype(F32)
        xn = x_ref[...] + ga_ref[...] * (of * _rstd(of) * gp_ref[...])
        xn_ref[...] = xn
        h_ref[...] = (xn * _rstd(xn) * (gn_ref[...] * (1.0 + sc_ref[...])) + sh_ref[...]).astype(BF16)

    return _pcall(
        body, (xres, o, ga[0], gpost[0], gn[0], scn[0], shn[0]), dep, name=name, grid=(s // tm,),
        in_specs=[row, row, _vec(*ga, 1), _vec(*gpost, 1), _vec(*gn, 1), _vec(*scn, 1), _vec(*shn, 1)],
        out_specs=[row, row], out_shape=[SDS((s, D), F32), SDS((s, D), BF16)], compiler_params=_cp(("parallel",)))


def _post_loss(xres, o, ga, gpost, target, name, dep=None):
    s = xres.shape[0]
    tm, row = _rows(s)

    def body(x_ref, o_ref, ga_ref, gp_ref, t_ref, dy_ref, loss_ref):
        of = o_ref[...].astype(F32)
        err = x_ref[...] + ga_ref[...] * (of * _rstd(of) * gp_ref[...]) - t_ref[...]
        dy_ref[...] = err * (1.0 / D)

        @pl.when(pl.program_id(0) == 0)
        def _():
            loss_ref[...] = jnp.zeros_like(loss_ref)

        loss_ref[...] += jnp.sum(jnp.mean(err * err, axis=-1, keepdims=True), axis=0, keepdims=True) * 0.5

    return _pcall(
        body, (xres, o, ga[0], gpost[0], target), dep, name=name, grid=(s // tm,),
        in_specs=[row, row, _vec(*ga, 1), _vec(*gpost, 1), row],
        out_specs=[row, pl.BlockSpec((8, 128), lambda i: (0, 0))], out_shape=[SDS((s, D), F32), SDS((8, 128), F32)],
        compiler_params=_cp(("arbitrary",)))


def _acc(ref, val, first):
    @pl.when(first)
    def _():
        ref[...] = val

    @pl.when(jnp.logical_not(first))
    def _():
        ref[...] += val


def _post_bwd(dxn, o, ga, gpost, name, dep=None):
    s = dxn.shape[0]
    tm, row = _rows(s)
    vec = pl.BlockSpec((1, D), lambda i: (0, 0))

    def body(d_ref, o_ref, ga_ref, gp_ref, do_ref, dga_ref, dgp_ref):
        of, dx = o_ref[...].astype(F32), d_ref[...]
        r = _rstd(of)
        oh = of * r
        do_ref[...] = _rms_bwd(oh, r, dx * (ga_ref[...] * gp_ref[...])).astype(BF16)
        cs = _colsum(dx * oh)
        first = pl.program_id(0) == 0
        _acc(dga_ref, cs * gp_ref[...], first)
        _acc(dgp_ref, cs * ga_ref[...], first)

    return _pcall(
        body, (dxn, o, ga[0], gpost[0]), dep, name=name, grid=(s // tm,),
        in_specs=[row, row, _vec(*ga, 1), _vec(*gpost, 1)], out_specs=[row, vec, vec],
        out_shape=[SDS((s, D), BF16), SDS((1, D), F32), SDS((1, D), F32)], compiler_params=_cp(("arbitrary",)))


def _pre_bwd(dh, x, dres, g, sc, name, dep=None):
    s = x.shape[0]
    tm, row = _rows(s)
    vec = pl.BlockSpec((1, D), lambda i: (0, 0))

    def body(dh_ref, x_ref, dr_ref, g_ref, sc_ref, dx_ref, dsh_ref, dsc_ref, dg_ref):
        xf, d = x_ref[...], dh_ref[...].astype(F32)
        r = _rstd(xf)
        xh = xf * r
        dx_ref[...] = dr_ref[...] + _rms_bwd(xh, r, d * (g_ref[...] * (1.0 + sc_ref[...])))
        cs = _colsum(d * xh)
        first = pl.program_id(0) == 0
        _acc(dsh_ref, _colsum(d), first)
        _acc(dsc_ref, cs * g_ref[...], first)
        _acc(dg_ref, cs * (1.0 + sc_ref[...]), first)

    return _pcall(
        body, (dh, x, dres, g[0], sc[0]), dep, name=name, grid=(s // tm,),
        in_specs=[row, row, row, _vec(*g, 1), _vec(*sc, 1)], out_specs=[row, vec, vec, vec],
        out_shape=[SDS((s, D), F32), SDS((1, D), F32), SDS((1, D), F32), SDS((1, D), F32)],
        compiler_params=_cp(("arbitrary",)))


def _gnorm(ys, gg, name):
    s = ys[0].shape[0]
    tm = min(TM, s)
    yb = pl.BlockSpec((tm, GW), lambda i: (i, 0))

    def body(y0, y1, y2, y3, g_ref, o_ref):
        for i, yr in enumerate((y0, y1, y2, y3)):
            y = yr[...]
            o_ref[:, GW * i:GW * (i + 1)] = (y * _rstd(y) * g_ref[:, GW * i:GW * (i + 1)]).astype(BF16)

    return pl.pallas_call(
        body, name=name, grid=(s // tm,), in_specs=[yb] * 4 + [_vec(*gg, 1)], out_specs=pl.BlockSpec((tm, D), lambda i: (i, 0)),
        out_shape=SDS((s, D), BF16), compiler_params=_cp(("parallel",)))(*ys, gg[0])


def _gnorm_bwd(dyn, ys, gg, name, dep=None):
    s = ys[0].shape[0]
    tm = min(TM, s)
    yb = pl.BlockSpec((tm, GW), lambda i: (i, 0))

    def body(d_ref, y0, y1, y2, y3, g_ref, o0, o1, o2, o3, dg_ref):
        first = pl.program_id(0) == 0
        for i, (yr, orf) in enumerate(zip((y0, y1, y2, y3), (o0, o1, o2, o3))):
            y = yr[...]
            d = d_ref[:, GW * i:GW * (i + 1)].astype(F32)
            r = _rstd(y)
            yh = y * r
            orf[...] = _rms_bwd(yh, r, d * g_ref[:, GW * i:GW * (i + 1)]).astype(BF16)
            cs = _colsum(d * yh)

            @pl.when(first)
            def _():
                dg_ref[:, GW * i:GW * (i + 1)] = cs

            @pl.when(jnp.logical_not(first))
            def _():
                dg_ref[:, GW * i:GW * (i + 1)] += cs

    return _pcall(
        body, (dyn, *ys, gg[0]), dep, name=name, grid=(s // tm,),
        in_specs=[pl.BlockSpec((tm, D), lambda i: (i, 0))] + [yb] * 4 + [_vec(*gg, 1)],
        out_specs=[yb] * 4 + [pl.BlockSpec((1, D), lambda i: (0, 0))],
        out_shape=[SDS((s, GW), BF16)] * 4 + [SDS((1, D), F32)], compiler_params=_cp(("arbitrary",)))


def _lane_put(acc, col, h):
    lane = lax.broadcasted_iota(jnp.int32, acc.shape, 1)
    return jnp.where(lane == h, col, acc)


def _fgate(z, bf, name, dep=None):
    s = z.shape[0]

    def body(z_ref, b_ref, fc_ref, fr_ref):
        xg = z_ref[...].astype(F32) + b_ref[...]
        lf = jnp.minimum(xg, 0.0) - jnp.log(1.0 + jnp.exp(-jnp.abs(xg)))
        lane = lax.broadcasted_iota(jnp.int32, lf.shape, 1)
        row = lax.broadcasted_iota(jnp.int32, lf.shape, 0)
        f = jnp.where(lane < 4, lf, 0.0)
        sh = 1
        while sh < s:
            f = f + jnp.where(row >= sh, pltpu.roll(f, sh, 0), 0.0)
            sh *= 2
        fc_ref[...] = f
        fr_ref[...] = f.T[:8, :]

    return _pcall(
        body, (z, bf), dep, name=name, grid=(1,),
        in_specs=[pl.BlockSpec((s, 128), lambda i: (0, Z_FG)), pl.BlockSpec((1, 128), lambda i: (0, 0))],
        out_specs=[pl.BlockSpec((s, 128), lambda i: (0, 0)), pl.BlockSpec((8, s), lambda i: (0, 0))],
        out_shape=[SDS((s, 128), F32), SDS((8, s), F32)], compiler_params=_cp(("arbitrary",)))


def _fgate_bwd(z, bf, dfrow, dfcol, name):
    s = z.shape[0]

    def body(z_ref, b_ref, d_ref, dc_ref, dz_ref, db_ref):
        d = jnp.concatenate([d_ref[...], jnp.zeros((120, s), F32)], axis=0).T + dc_ref[...]
        row = lax.broadcasted_iota(jnp.int32, d.shape, 0)
        lane = lax.broadcasted_iota(jnp.int32, d.shape, 1)
        sh = 1
        while sh < s:
            d = d + jnp.where(row < s - sh, pltpu.roll(d, s - sh, 0), 0.0)
            sh *= 2
        xg = z_ref[...].astype(F32) + b_ref[...]
        dz = jnp.where(lane < 4, d * _sigmoid(-xg), 0.0)
        dz_ref[...] = dz.astype(BF16)
        db_ref[...] = _colsum(dz)

    return pl.pallas_call(
        body, name=name, grid=(1,),
        in_specs=[pl.BlockSpec((s, 128), lambda i: (0, Z_FG)), pl.BlockSpec((1, 128), lambda i: (0, 0)),
                  pl.BlockSpec((8, s), lambda i: (0, 0)), pl.BlockSpec((s, 128), lambda i: (0, 0))],
        out_specs=[pl.BlockSpec((s, 128), lambda i: (0, 0)), pl.BlockSpec((1, 128), lambda i: (0, 0))],
        out_shape=[SDS((s, 128), BF16), SDS((1, 128), F32)], compiler_params=_cp(("arbitrary",)))(z, bf, dfrow, dfcol)


def _fox_scores(q_ref, k_ref, fc_ref, fr_ref, h, i, nk, tq):
    kw = nk * tq
    q = q_ref[:, HD * h:HD * (h + 1)] * SCALE
    sc = _dot(q, k_ref[0:kw, HD * h:HD * (h + 1)], "nt") + fc_ref[:, h:h + 1] - fr_ref[h:h + 1, 0:kw]
    qpos = i * tq + lax.broadcasted_iota(jnp.int32, (tq, kw), 0)
    kpos = lax.broadcasted_iota(jnp.int32, (tq, kw), 1)
    return q, jnp.where(kpos <= qpos, sc, NEG)


def _fox_fwd(z, fcol, frow, name):
    s = z.shape[0]
    tq = min(TQ, s)
    nc = s // tq

    def body(q_ref, k_ref, v_ref, fc_ref, fr_ref, y_ref, l_ref):
        i = pl.program_id(0)
        for n in range(0, nc, FOX_FWD_PAIR):
            nk = min(n + FOX_FWD_PAIR, nc)

            @pl.when((i >= n) & (i < n + FOX_FWD_PAIR))
            def _():
                kw = nk * tq
                lse = jnp.zeros((tq, 128), F32)
                for h in range(4):
                    _, sc = _fox_scores(q_ref, k_ref, fc_ref, fr_ref, h, i, nk, tq)
                    m = jnp.max(sc, axis=-1, keepdims=True)
                    p = jnp.exp(sc - m)
                    l = jnp.sum(p, axis=-1, keepdims=True)
                    y_ref[:, HD * h:HD * (h + 1)] = _dot(p, v_ref[0:kw, HD * h:HD * (h + 1)], "nn") / l
                    lse = _lane_put(lse, m + jnp.log(l), h)
                l_ref[...] = lse

    return pl.pallas_call(
        body, name=name, grid=(nc,),
        in_specs=[pl.BlockSpec((tq, GW), lambda i: (i, Z_FQ)), pl.BlockSpec((s, GW), lambda i: (0, Z_FK)),
                  pl.BlockSpec((s, GW), lambda i: (0, Z_FV)), pl.BlockSpec((tq, 128), lambda i: (i, 0)),
                  pl.BlockSpec((8, s), lambda i: (0, 0))],
        out_specs=[pl.BlockSpec((tq, GW), lambda i: (i, 0)), pl.BlockSpec((tq, 128), lambda i: (i, 0))],
        out_shape=[SDS((s, GW), F32), SDS((s, 128), F32)], compiler_params=_cp(("parallel",)))(z, z, z, fcol, frow)


def _fox_bwd(z, fcol, frow, lse, y, dy, name):
    s = z.shape[0]
    tq = min(TQ, s)
    nc = s // tq
    bounds = tuple(sorted({min(b, nc) for b in FOX_BWD_PATHS} | {nc}))

    def body(q_ref, k_ref, v_ref, fc_ref, fr_ref, l_ref, y_ref, dy_ref, dq_ref, dk_ref, dv_ref, df_ref, dfq_ref):
        @pl.when(pl.program_id(0) == 0)
        def _():
            dk_ref[...] = jnp.zeros_like(dk_ref)
            dv_ref[...] = jnp.zeros_like(dv_ref)
            df_ref[...] = jnp.zeros_like(df_ref)

        i = pl.program_id(0)
        for lo, nk in zip((0,) + bounds[:-1], bounds):
            @pl.when((i >= lo) & (i < nk))
            def _():
                kw = nk * tq
                dfq = jnp.zeros((tq, 128), F32)
                dyf = dy_ref[...].astype(F32)
                for h in range(4):
                    hs = slice(HD * h, HD * (h + 1))
                    q, sc = _fox_scores(q_ref, k_ref, fc_ref, fr_ref, h, i, nk, tq)
                    p = jnp.exp(sc - l_ref[:, h:h + 1])
                    dyh = dyf[:, hs]
                    dd = jnp.sum(dyh * y_ref[:, hs], axis=-1, keepdims=True)
                    ds = p * (_dot(dyh, v_ref[0:kw, hs], "nt") - dd)
                    dq_ref[:, hs] = _dot(ds, k_ref[0:kw, hs], "nn") * SCALE
                    dk_ref[0:kw, hs] += _dot(ds, q, "tn")
                    dv_ref[0:kw, hs] += _dot(p, dyh, "tn")
                    df_ref[h:h + 1, 0:kw] -= _colsum(ds)
                    dfq = _lane_put(dfq, jnp.sum(ds, axis=-1, keepdims=True), h)
                dfq_ref[...] = dfq

    tile = lambda w: pl.BlockSpec((tq, w), lambda i: (i, 0))
    full = pl.BlockSpec((s, GW), lambda i: (0, 0))
    rows8 = pl.BlockSpec((8, s), lambda i: (0, 0))
    return pl.pallas_call(
        body, name=name, grid=(nc,),
        in_specs=[pl.BlockSpec((tq, GW), lambda i: (i, Z_FQ)), pl.BlockSpec((s, GW), lambda i: (0, Z_FK)),
                  pl.BlockSpec((s, GW), lambda i: (0, Z_FV)), tile(128), rows8, tile(128), tile(GW), tile(GW)],
        out_specs=[tile(GW), full, full, rows8, tile(128)],
        out_shape=[SDS((s, GW), F32), SDS((s, GW), F32), SDS((s, GW), F32), SDS((8, s), F32), SDS((s, 128), F32)],
        compiler_params=_cp(("arbitrary",)))(z, z, z, fcol, frow, lse, y, dy)


def _swa_block(q_ref, k_ref, v_ref, n):
    qs = pl.multiple_of(n * WIN, WIN)
    ks = pl.multiple_of(jnp.maximum(n - 1, 0) * WIN, WIN)
    qb = q_ref[pl.ds(qs, WIN), :]
    kb = k_ref[pl.ds(ks, 2 * WIN), :]
    vb = v_ref[pl.ds(ks, 2 * WIN), :]
    rows = lax.broadcasted_iota(jnp.int32, (2 * WIN, 2 * WIN), 0) & (WIN - 1)
    dist = (qs + rows) - (ks + lax.broadcasted_iota(jnp.int32, (2 * WIN, 2 * WIN), 1))
    return qs, ks, qb, kb, vb, (dist >= 0) & (dist < WIN)


def _stack2(x, kvh):
    return jnp.concatenate([x[:, HD * (2 * kvh):HD * (2 * kvh + 1)], x[:, HD * (2 * kvh + 1):HD * (2 * kvh + 2)]], axis=0)


def _sink2(sink_ref, kvh):
    top = lax.broadcasted_iota(jnp.int32, (2 * WIN, 1), 0) < WIN
    return jnp.where(top, sink_ref[2 * kvh], sink_ref[2 * kvh + 1])


def _swa_fwd(z, sinks, name):
    s = z.shape[0]

    def body(sink_ref, q_ref, k_ref, v_ref, y_ref, l_ref):
        def step(n, carry):
            qs, ks, qb, kb, vb, valid = _swa_block(q_ref, k_ref, v_ref, n)
            lse = jnp.zeros((WIN, 128), F32)
            for kvh in range(2):
                kv = slice(HD * kvh, HD * (kvh + 1))
                sc = jnp.where(valid, _dot(_stack2(qb, kvh) * SCALE, kb[:, kv], "nt"), NEG)
                sink = _sink2(sink_ref, kvh)
                m = jnp.maximum(jnp.max(sc, axis=-1, keepdims=True), sink)
                p = jnp.exp(sc - m)
                den = jnp.sum(p, axis=-1, keepdims=True) + jnp.exp(sink - m)
                o = _dot(p, vb[:, kv], "nn") / den
                lrow = m + jnp.log(den)
                for j in range(2):
                    h = 2 * kvh + j
                    y_ref[pl.ds(qs, WIN), HD * h:HD * (h + 1)] = o[WIN * j:WIN * (j + 1), :]
                    lse = _lane_put(lse, lrow[WIN * j:WIN * (j + 1), :], h)
            l_ref[pl.ds(qs, WIN), :] = lse
            return carry

        lax.fori_loop(0, s // WIN, step, 0, unroll=2)

    return pl.pallas_call(
        body, name=name, grid=(1,),
        in_specs=[pl.BlockSpec(memory_space=pltpu.SMEM), pl.BlockSpec((s, GW), lambda i: (0, Z_SQ)),
                  pl.BlockSpec((s, 128), lambda i: (0, Z_SK)), pl.BlockSpec((s, 128), lambda i: (0, Z_SV))],
        out_specs=[pl.BlockSpec((s, GW), lambda i: (0, 0)), pl.BlockSpec((s, 128), lambda i: (0, 0))],
        out_shape=[SDS((s, GW), F32), SDS((s, 128), F32)], compiler_params=_cp(("arbitrary",)))(sinks, z, z, z)


def _swa_bwd(z, sinks, lse, y, dy, name):
    s = z.shape[0]

    def body(sink_ref, q_ref, k_ref, v_ref, l_ref, y_ref, dy_ref, dq_ref, dk_ref, dv_ref, dsink_ref):
        dk_ref[...] = jnp.zeros_like(dk_ref)
        dv_ref[...] = jnp.zeros_like(dv_ref)
        dsink_ref[...] = jnp.zeros_like(dsink_ref)

        def step(n, carry):
            qs, ks, qb, kb, vb, valid = _swa_block(q_ref, k_ref, v_ref, n)
            lse_b = l_ref[pl.ds(qs, WIN), :]
            yb = y_ref[pl.ds(qs, WIN), :]
            dyb = dy_ref[pl.ds(qs, WIN), :].astype(F32)
            dsink = jnp.zeros((1, 128), F32)
            for kvh in range(2):
                kv = slice(HD * kvh, HD * (kvh + 1))
                q = _stack2(qb, kvh) * SCALE
                dy2 = _stack2(dyb, kvh)
                sc = jnp.where(valid, _dot(q, kb[:, kv], "nt"), NEG)
                lh = jnp.concatenate([lse_b[:, 2 * kvh:2 * kvh + 1], lse_b[:, 2 * kvh + 1:2 * kvh + 2]], axis=0)
                p = jnp.exp(sc - lh)
                dd = jnp.sum(dy2 * _stack2(yb, kvh), axis=-1, keepdims=True)
                ds = p * (_dot(dy2, vb[:, kv], "nt") - dd)
                dq = _dot(ds, kb[:, kv], "nn") * SCALE
                dk_ref[pl.ds(ks, 2 * WIN), kv] += _dot(ds, q, "tn")
                dv_ref[pl.ds(ks, 2 * WIN), kv] += _dot(p, dy2, "tn")
                dsk = jnp.exp(_sink2(sink_ref, kvh) - lh) * dd
                for j in range(2):
                    h = 2 * kvh + j
                    dq_ref[pl.ds(qs, WIN), HD * h:HD * (h + 1)] = dq[WIN * j:WIN * (j + 1), :]
                    dsink = _lane_put(dsink, dsink[:, h:h + 1] - jnp.sum(dsk[WIN * j:WIN * (j + 1), :], axis=0, keepdims=True), h)
            dsink_ref[...] += dsink
            return carry

        lax.fori_loop(0, s // WIN, step, 0, unroll=2)

    full = lambda w: pl.BlockSpec((s, w), lambda i: (0, 0))
    return pl.pallas_call(
        body, name=name, grid=(1,),
        in_specs=[pl.BlockSpec(memory_space=pltpu.SMEM), pl.BlockSpec((s, GW), lambda i: (0, Z_SQ)),
                  pl.BlockSpec((s, 128), lambda i: (0, Z_SK)), pl.BlockSpec((s, 128), lambda i: (0, Z_SV)),
                  full(128), full(GW), full(GW)],
        out_specs=[full(GW), full(128), full(128), pl.BlockSpec((1, 128), lambda i: (0, 0))],
        out_shape=[SDS((s, GW), F32), SDS((s, 128), F32), SDS((s, 128), F32), SDS((1, 128), F32)],
        compiler_params=_cp(("arbitrary",)))(sinks, z, z, z, lse, y, dy)


_SUBLANES = 8


def _rotations(win, advance=False):
    n = win.shape[0]
    return [win] + [pltpu.roll(win, (n - b) if advance else b, 0) for b in range(1, _SUBLANES)]


def _delayed(rots, shift, halo, tm):
    a, b = divmod(shift, _SUBLANES)
    return rots[b][halo - _SUBLANES * a:halo - _SUBLANES * a + tm, :]


def _advanced(rots, shift, tm):
    a, b = divmod(shift, _SUBLANES)
    return rots[b][_SUBLANES * a:_SUBLANES * a + tm, :]


def _prev_halo(width, halo, tm, col):
    return pl.BlockSpec((halo, width), lambda i: (jnp.maximum(i * (tm // halo) - 1, 0), col))


def _glu_window(a_ref, g_ref, ah_ref, gh_ref):
    keep = (pl.program_id(0) > 0).astype(F32)
    a = jnp.concatenate([ah_ref[...].astype(F32) * keep, a_ref[...].astype(F32)], axis=0)
    g = jnp.concatenate([gh_ref[...].astype(F32), g_ref[...].astype(F32)], axis=0)
    return a * _sigmoid(g)


def _conv_fwd(z, cw, cb, lg, lb, pw, pb, name):
    s = z.shape[0]
    tm = min(TM, s)

    def body(a_ref, g_ref, ah_ref, gh_ref, w_ref, b_ref, lg_ref, lb_ref, pw_ref, pb_ref, y_ref, hc_ref):
        rots = _rotations(_glu_window(a_ref, g_ref, ah_ref, gh_ref))
        hc = jnp.zeros((tm, GW), F32) + b_ref[...]
        for k in range(CONV_K):
            hc = hc + w_ref[k:k + 1, :] * _delayed(rots, CONV_K - 1 - k, CONV_HALO, tm)
        hc_ref[...] = hc
        xh, _ = _ln_stats(hc)
        y_ref[...] = _dot(_silu(xh * lg_ref[...] + lb_ref[...]), pw_ref[...], "nn") + pb_ref[...]

    tile = lambda col: pl.BlockSpec((tm, GW), lambda i: (i, col))
    whole = lambda a: pl.BlockSpec(a.shape, lambda i: (0, 0))
    return pl.pallas_call(
        body, name=name, grid=(s // tm,),
        in_specs=[tile(Z_CA), tile(Z_CG), _prev_halo(GW, CONV_HALO, tm, Z_CA), _prev_halo(GW, CONV_HALO, tm, Z_CG),
                  whole(cw), whole(cb), whole(lg), whole(lb), whole(pw), whole(pb)],
        out_specs=[tile(0), tile(0)], out_shape=[SDS((s, GW), F32), SDS((s, GW), F32)],
        compiler_params=_cp(("parallel",)))(z, z, z, z, cw, cb, lg, lb, pw, pb)


def _conv_bwd_a(z, hc, dy, cw, lg, lb, pw, name):
    s = z.shape[0]
    tm = min(TM, s)

    def body(a_ref, g_ref, ah_ref, gh_ref, hc_ref, dy_ref, lg_ref, lb_ref, pw_ref,
             dhc_ref, dpw_ref, dpb_ref, dlg_ref, dlb_ref, dcw_ref, dcb_ref):
        first = pl.program_id(0) == 0
        dy = dy_ref[...].astype(F32)
        xh, rstd = _ln_stats(hc_ref[...])
        hn = xh * lg_ref[...] + lb_ref[...]
        dhn = _dot(dy, pw_ref[...], "nt") * _dsilu(hn)
        dhc = _ln_bwd(xh, rstd, dhn * lg_ref[...])
        dhc_ref[...] = dhc
        _acc(dpw_ref, _dot(_silu(hn), dy, "tn"), first)
        _acc(dpb_ref, _colsum(dy), first)
        _acc(dlg_ref, _colsum(dhn * xh), first)
        _acc(dlb_ref, _colsum(dhn), first)
        _acc(dcb_ref, _colsum(dhc), first)
        rots = _rotations(_glu_window(a_ref, g_ref, ah_ref, gh_ref))

        @pl.when(first)
        def _():
            dcw_ref[...] = jnp.zeros_like(dcw_ref)

        for k in range(CONV_K):
            dcw_ref[k:k + 1, :] += _colsum(dhc * _delayed(rots, CONV_K - 1 - k, CONV_HALO, tm))

    tile = lambda col: pl.BlockSpec((tm, GW), lambda i: (i, col))
    whole = lambda shape: pl.BlockSpec(shape, lambda i: (0, 0))
    return pl.pallas_call(
        body, name=name, grid=(s // tm,),
        in_specs=[tile(Z_CA), tile(Z_CG), _prev_halo(GW, CONV_HALO, tm, Z_CA), _prev_halo(GW, CONV_HALO, tm, Z_CG),
                  tile(0), tile(0), whole(lg.shape), whole(lb.shape), whole(pw.shape)],
        out_specs=[tile(0), whole((GW, GW)), whole((1, GW)), whole((1, GW)), whole((1, GW)), whole((32, GW)), whole((1, GW))],
        out_shape=[SDS((s, GW), F32), SDS((GW, GW), F32), SDS((1, GW), F32), SDS((1, GW), F32), SDS((1, GW), F32),
                   SDS((32, GW), F32), SDS((1, GW), F32)],
        compiler_params=_cp(("arbitrary",)))(z, z, z, z, hc, dy, lg, lb, pw)


def _conv_bwd_b(z, dhc, cw, name):
    s = z.shape[0]
    tm = min(TM, s)
    nt = s // tm

    def body(a_ref, g_ref, d_ref, dn_ref, w_ref, da_ref, dg_ref):
        keep = (pl.program_id(0) < nt - 1).astype(F32)
        rots = _rotations(jnp.concatenate([d_ref[...], dn_ref[...] * keep], axis=0), advance=True)
        dhg = jnp.zeros((tm, GW), F32)
        for k in range(CONV_K):
            dhg = dhg + w_ref[k:k + 1, :] * _advanced(rots, CONV_K - 1 - k, tm)
        sg = _sigmoid(g_ref[...].astype(F32))
        da_ref[...] = (dhg * sg).astype(BF16)
        dg_ref[...] = (dhg * a_ref[...].astype(F32) * sg * (1.0 - sg)).astype(BF16)

    tile = lambda col: pl.BlockSpec((tm, GW), lambda i: (i, col))
    nxt = pl.BlockSpec((CONV_HALO, GW), lambda i: (jnp.minimum((i + 1) * (tm // CONV_HALO), s // CONV_HALO - 1), 0))
    return pl.pallas_call(
        body, name=name, grid=(nt,),
        in_specs=[tile(Z_CA), tile(Z_CG), tile(0), nxt, pl.BlockSpec(cw.shape, lambda i: (0, 0))],
        out_specs=[tile(0), tile(0)], out_shape=[SDS((s, GW), BF16), SDS((s, GW), BF16)],
        compiler_params=_cp(("parallel",)))(z, z, dhc, dhc, cw)


def _sgu_chunk(zu, zv, lg, lb, wcat, bfull):
    u, v = _gelu(zu), _gelu(zv)
    xh, rstd = _ln_stats(v)
    vn = xh * lg + lb
    lane = lax.shift_right_logical(lax.broadcasted_iota(jnp.int32, (WIN, GW), 1), 6)
    r = jnp.concatenate([jnp.where(lane == g, vn, 0.0) for g in range(4)], axis=0)
    mix = _dot(wcat, r, "nn") + bfull
    return u, xh, rstd, r, mix, lane


def _tril4(w):
    t = lax.broadcasted_iota(jnp.int32, w.shape, 0)
    sidx = lax.broadcasted_iota(jnp.int32, w.shape, 1) & (WIN - 1)
    return jnp.where(sidx <= t, w, 0.0)


def _sgu_fwd(z, lg, lb, wcat, bfull, name):
    s = z.shape[0]
    tm = min(TM, s)

    def body(u_ref, v_ref, lg_ref, lb_ref, w_ref, b_ref, y_ref):
        w = _tril4(w_ref[...])
        for n in range(tm // WIN):
            rows = slice(WIN * n, WIN * (n + 1))
            u, _, _, _, mix, _ = _sgu_chunk(u_ref[rows, :].astype(F32), v_ref[rows, :].astype(F32), lg_ref[...], lb_ref[...], w, b_ref[...])
            y_ref[rows, :] = u * mix

    tile = lambda col: pl.BlockSpec((tm, GW), lambda i: (i, col))
    whole = lambda a: pl.BlockSpec(a.shape, lambda i: (0, 0))
    return pl.pallas_call(
        body, name=name, grid=(s // tm,), in_specs=[tile(Z_GU), tile(Z_GV), whole(lg), whole(lb), whole(wcat), whole(bfull)],
        out_specs=tile(0), out_shape=SDS((s, GW), F32), compiler_params=_cp(("parallel",)))(z, z, lg, lb, wcat, bfull)


def _sgu_bwd(z, dy, lg, lb, wcat, bfull, name):
    s = z.shape[0]
    tm = min(TM, s)

    def body(u_ref, v_ref, dy_ref, lg_ref, lb_ref, w_ref, b_ref, du_ref, dv_ref, dw_ref, db_ref, dlg_ref, dlb_ref):
        first = pl.program_id(0) == 0
        w = _tril4(w_ref[...])
        wt = w.T
        dw = jnp.zeros((WIN, 4 * WIN), F32)
        db = jnp.zeros((WIN, 128), F32)
        dlg = jnp.zeros((1, GW), F32)
        dlb = jnp.zeros((1, GW), F32)
        for n in range(tm // WIN):
            rows = slice(WIN * n, WIN * (n + 1))
            zu, zv, dout = u_ref[rows, :].astype(F32), v_ref[rows, :].astype(F32), dy_ref[rows, :].astype(F32)
            u, xh, rstd, r, mix, lane = _sgu_chunk(zu, zv, lg_ref[...], lb_ref[...], w, b_ref[...])
            dmix = dout * u
            du_ref[rows, :] = (dout * mix * _dgelu(zu)).astype(BF16)
            dw = dw + _dot(dmix, r, "nt")
            for g in range(4):
                db = _lane_put(db, db[:, g:g + 1] + jnp.sum(dmix[:, HD * g:HD * (g + 1)], axis=1, keepdims=True), g)
            dr = _dot(wt, dmix, "nn")
            dvn = jnp.zeros((WIN, GW), F32)
            for g in range(4):
                dvn = dvn + jnp.where(lane == g, dr[WIN * g:WIN * (g + 1), :], 0.0)
            dlg = dlg + _colsum(dvn * xh)
            dlb = dlb + _colsum(dvn)
            dv_ref[rows, :] = (_ln_bwd(xh, rstd, dvn * lg_ref[...]) * _dgelu(zv)).astype(BF16)
        _acc(dw_ref, _tril4(dw), first)
        _acc(db_ref, db, first)
        _acc(dlg_ref, dlg, first)
        _acc(dlb_ref, dlb, first)

    tile = lambda col: pl.BlockSpec((tm, GW), lambda i: (i, col))
    whole = lambda shape: pl.BlockSpec(shape, lambda i: (0, 0))
    return pl.pallas_call(
        body, name=name, grid=(s // tm,),
        in_specs=[tile(Z_GU), tile(Z_GV), tile(0), whole(lg.shape), whole(lb.shape), whole(wcat.shape), whole(bfull.shape)],
        out_specs=[tile(0), tile(0), whole((WIN, 4 * WIN)), whole((WIN, 128)), whole((1, GW)), whole((1, GW))],
        out_shape=[SDS((s, GW), BF16), SDS((s, GW), BF16), SDS((WIN, 4 * WIN), F32), SDS((WIN, 128), F32),
                   SDS((1, GW), F32), SDS((1, GW), F32)],
        compiler_params=_cp(("arbitrary",)))(z, z, dy, lg, lb, wcat, bfull)


def _conv3(win, w, b):
    return (w[2:3, :] * win[FFN_HALO:, :] + w[1:2, :] * pltpu.roll(win, 1, 0)[FFN_HALO:, :]
            + w[0:1, :] * pltpu.roll(win, 2, 0)[FFN_HALO:, :] + b)


def _ffn_specs(s, tm):
    main = pl.BlockSpec((2, None, tm, FF_BLK), lambda j, i: (0, j, i, 0))
    prev = pl.BlockSpec((2, None, FFN_HALO, FF_BLK), lambda j, i: (0, j, jnp.maximum(i * (tm // FFN_HALO) - 1, 0), 0))
    nxt = pl.BlockSpec((2, None, FFN_HALO, FF_BLK),
                       lambda j, i: (0, j, jnp.minimum((i + 1) * (tm // FFN_HALO), s // FFN_HALO - 1), 0))
    wsp = pl.BlockSpec((2, None, 3, FF_BLK), lambda j, i: (0, j, 0, 0))
    bsp = pl.BlockSpec((2, None, 1, FF_BLK), lambda j, i: (0, j, 0, 0))
    return main, prev, nxt, wsp, bsp


def _ffn_act(u4, w4, b4, name, dep=None):
    s = u4.shape[2]
    tm = min(TM, s)
    main, prev, _, wsp, bsp = _ffn_specs(s, tm)

    def body(u_ref, uh_ref, w_ref, b_ref, o_ref, c_ref):
        keep = (pl.program_id(1) > 0).astype(F32)
        gw, vw = [jnp.concatenate([uh_ref[p].astype(F32) * keep, u_ref[p].astype(F32)], axis=0) for p in range(2)]
        gc, vc = _conv3(gw, w_ref[0], b_ref[0]), _conv3(vw, w_ref[1], b_ref[1])
        o_ref[...] = (_silu(gc) * vc).astype(BF16)
        c_ref[0] = gc.astype(BF16)
        c_ref[1] = vc.astype(BF16)

    return _pcall(
        body, (u4, u4, w4, b4), dep, name=name, grid=(FF_NBLK, s // tm), in_specs=[main, prev, wsp, bsp],
        out_specs=[pl.BlockSpec((None, tm, FF_BLK), lambda j, i: (j, i, 0)), main],
        out_shape=[SDS((FF_NBLK, s, FF_BLK), BF16), SDS(u4.shape, BF16)], compiler_params=_cp(("parallel", "parallel")))


def _ffn_bwd(u4, cv4, dact, w4, w_up, name, dep=None):
    s = u4.shape[2]
    tm = min(TM, s)
    nt = s // tm
    main = pl.BlockSpec((2, None, tm, FF_BLK), lambda i, j: (0, j, i, 0))
    nxt = pl.BlockSpec((2, None, FFN_HALO, FF_BLK),
                       lambda i, j: (0, j, jnp.minimum((i + 1) * (tm // FFN_HALO), s // FFN_HALO - 1), 0))
    dmain = pl.BlockSpec((None, tm, FF_BLK), lambda i, j: (j, i, 0))
    dnext = pl.BlockSpec((None, FFN_HALO, FF_BLK), lambda i, j: (j, jnp.minimum((i + 1) * (tm // FFN_HALO), s // FFN_HALO - 1), 0))
    wsp = pl.BlockSpec((2, None, 3, FF_BLK), lambda i, j: (0, j, 0, 0))
    wup = pl.BlockSpec((2, None, D, FF_BLK), lambda i, j: (0, j, 0, 0))
    all_w = pl.BlockSpec((2, FF_NBLK, 3, FF_BLK), lambda i, j: (0, 0, 0, 0))
    all_b = pl.BlockSpec((2, FF_NBLK, 1, FF_BLK), lambda i, j: (0, 0, 0, 0))

    def body(u_ref, c_ref, cn_ref, d_ref, dn_ref, w_ref, wup_ref, du_ref, dw_ref, db_ref, dh_ref, acc_ref):
        i, j = pl.program_id(0), pl.program_id(1)
        first = i == 0
        keep_next = (i < nt - 1).astype(F32)
        taps = [[jnp.zeros((1, FF_BLK), F32) for _ in range(3)] for _ in range(2)]
        bias = [jnp.zeros((1, FF_BLK), F32) for _ in range(2)]
        def matmul(rows, dus):
            prod = _dot(dus[0], wup_ref[0], "nt") + _dot(dus[1], wup_ref[1], "nt")
            acc_ref[rows, :] = jnp.where(j == 0, prod, acc_ref[rows, :] + prod)

        pending = []
        for r0 in range(0, tm, FFN_SUB):
            rows, wide = slice(r0, r0 + FFN_SUB), slice(r0, r0 + FFN_SUB + FFN_HALO)
            if r0 + FFN_SUB < tm:
                gc, vc = [c_ref[p, wide, :].astype(F32) for p in range(2)]
                d = d_ref[wide, :].astype(F32)
            else:
                gc, vc = [jnp.concatenate([c_ref[p, rows, :].astype(F32), cn_ref[p].astype(F32)], axis=0) for p in range(2)]
                d = jnp.concatenate([d_ref[rows, :].astype(F32), dn_ref[...].astype(F32) * keep_next], axis=0)
            sg = _sigmoid(gc)
            duc = (d * vc * (sg * (1.0 + gc * (1.0 - sg))), d * (gc * sg))
            dus = []
            for p in range(2):
                w = w_ref[p]
                own = duc[p][:FFN_SUB, :]
                adv = (pltpu.roll(duc[p], FFN_SUB + FFN_HALO - 2, 0)[:FFN_SUB, :],
                       pltpu.roll(duc[p], FFN_SUB + FFN_HALO - 1, 0)[:FFN_SUB, :], own)
                du = (w[2:3, :] * adv[2] + w[1:2, :] * adv[1] + w[0:1, :] * adv[0]).astype(BF16)
                du_ref[p, rows, :] = du
                dus.append(du)
                ut = u_ref[p, rows, :].astype(F32)
                for k in range(3):
                    taps[p][k] = taps[p][k] + _colsum(adv[k] * ut)
                bias[p] = bias[p] + _colsum(own)
            pending.append((rows, dus))
            if len(pending) > 1:
                matmul(*pending.pop(0))
        matmul(*pending.pop(0))

        for p in range(2):
            @pl.when(first)
            def _():
                db_ref[p, j] = bias[p]
                for k in range(3):
                    dw_ref[p, j, k:k + 1, :] = taps[p][k]

            @pl.when(jnp.logical_not(first))
            def _():
                db_ref[p, j] += bias[p]
                for k in range(3):
                    dw_ref[p, j, k:k + 1, :] += taps[p][k]

        @pl.when(j == FF_NBLK - 1)
        def _():
            dh_ref[...] = acc_ref[...].astype(BF16)

    return _pcall(
        body, (u4, cv4, cv4, dact, dact, w4, w_up.reshape(2, FF_NBLK, D, FF_BLK)), dep, name=name, grid=(nt, FF_NBLK),
        in_specs=[main, main, nxt, dmain, dnext, wsp, wup],
        out_specs=[main, all_w, all_b, pl.BlockSpec((tm, D), lambda i, j: (i, 0))],
        out_shape=[SDS(u4.shape, BF16), SDS((2, FF_NBLK, 3, FF_BLK), F32), SDS((2, FF_NBLK, 1, FF_BLK), F32), SDS((s, D), BF16)],
        scratch_shapes=[pltpu.VMEM((tm, D), F32)], compiler_params=_cp(("arbitrary", "arbitrary")))


def _sum8(parts, name):
    _, r, c = parts[0].shape
    tr = r
    for cand in (512, 256, 128, 64, 32, 16):
        if r % cand == 0 and r > cand:
            tr = cand
            break
    nb = r // tr

    def body(*refs):
        o_ref = refs[-1]
        for l, p_ref in enumerate(refs[:-1]):
            @pl.when(pl.program_id(0) == l)
            def _():
                acc = p_ref[0].astype(F32)
                for j in range(1, N_DEV):
                    acc = acc + p_ref[j].astype(F32)
                o_ref[...] = acc

    def spec(l):
        return pl.BlockSpec((N_DEV, tr, c), lambda ll, i: (0, jnp.where(ll == l, i, jnp.where(ll < l, 0, nb - 1)), 0))

    return pl.pallas_call(
        body, name=name, grid=(len(parts), nb), in_specs=[spec(l) for l in range(len(parts))],
        out_specs=pl.BlockSpec((None, tr, c), lambda ll, i: (ll, i, 0)), out_shape=SDS((len(parts), r, c), F32),
        compiler_params=_cp(("arbitrary", "arbitrary")))(*parts)


def _sum8_small(parts, name):
    n = len(parts)

    def body(*refs):
        for p_ref, o_ref in zip(refs[:n], refs[n:]):
            acc = p_ref[0]
            for j in range(1, N_DEV):
                acc = acc + p_ref[j]
            o_ref[...] = acc

    return pl.pallas_call(body, name=name, out_shape=[SDS(p.shape[1:], F32) for p in parts], compiler_params=_cp())(*parts)


def _adamw_math(w, g, m, v):
    m = ADAM_B1 * m + (1.0 - ADAM_B1) * g
    v = ADAM_B2 * v + (1.0 - ADAM_B2) * (g * g)
    m_hat = m / (1.0 - ADAM_B1 ** ADAM_STEP)
    v_hat = v / (1.0 - ADAM_B2 ** ADAM_STEP)
    return -ADAM_LR * (m_hat / (jnp.sqrt(v_hat) + ADAM_EPS) + ADAM_WD * w), m, v


def _adamw(w, g, m, v, name):
    r, c = w.shape
    tr = r
    for cand in (256, 128, 64):
        if r % cand == 0 and r > cand:
            tr = cand
            break

    def body(w_ref, g_ref, m_ref, v_ref, d_ref, mo_ref, vo_ref):
        d_ref[...], mo_ref[...], vo_ref[...] = _adamw_math(w_ref[...], g_ref[...], m_ref[...], v_ref[...])

    blk = pl.BlockSpec((tr, c), lambda i: (i, 0))
    return pl.pallas_call(body, name=name, grid=(r // tr,), in_specs=[blk] * 4, out_specs=[blk] * 3,
                          out_shape=[SDS((r, c), F32)] * 3, compiler_params=_cp(("parallel",)))(w, g, m, v)


def _adamw_small(ws, gs, ms, vs, name):
    n = len(ws)

    def body(*refs):
        ins, outs = refs[:4 * n], refs[4 * n:]
        for i in range(n):
            d, m, v = _adamw_math(ins[i][...], ins[n + i][...], ins[2 * n + i][...], ins[3 * n + i][...])
            outs[i][...], outs[n + i][...], outs[2 * n + i][...] = d, m, v

    shapes = [SDS(w.shape, F32) for w in ws]
    res = pl.pallas_call(body, name=name, out_shape=shapes * 3, compiler_params=_cp())(*ws, *gs, *ms, *vs)
    return res[:n], res[n:2 * n], res[2 * n:]


def _perm_in(w):
    pad = jnp.zeros(w.shape[:-1] + (ZW - 2308,), w.dtype)
    return jnp.concatenate([w[..., :768], w[..., 772:], w[..., 768:772], pad], axis=-1)


def _unperm_in(g):
    return jnp.concatenate([g[..., :768], g[..., 2304:2308], g[..., 768:2304]], axis=-1)


def _wcat(sgu_w):
    return sgu_w.transpose(1, 0, 2).reshape(WIN, 4 * WIN)


def _layer_fwd(l, x, h1, mod, p, wg, last, target, nxt, w_in_next):
    s = x.shape[0]
    tag = f"_l{l}"
    mrow = lambda k: (mod, 6 * l + k)
    z = _mm_rows(h1, wg["w_in"].get(h1), "nn", ZW, BF16, "mm_z" + tag)
    fcol, frow = _fgate(z, p["bf"], "fgate" + tag, dep=wg["w_out"].prefetch(z))
    y_fox, lse_fox = _fox_fwd(z, fcol, frow, "fox_fwd" + tag)
    y_conv, hc = _conv_fwd(z, wg["conv_w"], p["conv_b"], p["conv_ln_g"], p["conv_ln_b"], wg["conv_pw_w"], p["conv_pw_b"],
                           "conv_fwd" + tag)
    y_swa, lse_swa = _swa_fwd(z, p["sinks"], "swa_fwd" + tag)
    y_sgu = _sgu_fwd(z, p["sgu_ln_g"], p["sgu_ln_b"], p["wcat"], p["bfull"], "sgu_fwd" + tag)
    ys = (y_fox, y_conv, y_swa, y_sgu)
    yn = _gnorm(ys, (p["g_group"], l), "gnorm" + tag)
    tok = wg["w_up"].prefetch(yn)
    o = _mm_rows(yn, wg["w_out"].get(yn), "nn", D, BF16, "mm_o" + tag)
    x1, h2 = _post(x, o, mrow(2), (p["g_post_mix"], l), (p["g_pre_ffn"], l), mrow(4), mrow(3), "post_mix" + tag, dep=tok)
    tok = wg["w_down"].prefetch(h2)
    u = _matmul(h2, wg["w_up"].get(h2), "nn", (N_DEV, s, FF_BLK), BF16, (N_DEV, 1, 1),
                _bs((s, D), lambda j, i, k: (0, 0)), _bs((None, D, FF_BLK), lambda j, i, k: (j, 0, 0)),
                _bs((None, s, FF_BLK), lambda j, i, k: (j, 0, 0)), None, "mm_u" + tag)
    u4 = u.reshape(2, FF_NBLK, s, FF_BLK)
    act, cv4 = _ffn_act(u4, wg["ffn_conv_w"], p["ffn_conv_b"], "ffn_act" + tag, dep=tok)
    tok = None if w_in_next is None else w_in_next.prefetch(act)
    f = _matmul(act, wg["w_down"].get(act), "nn", (s, D), BF16, (1, 1, FF_NBLK),
                _bs((None, s, FF_BLK), lambda i, j, k: (k, 0, 0)), _bs((FF_BLK, D), lambda i, j, k: (k, 0)),
                _bs((s, D), lambda i, j, k: (0, 0)), (s, D), "mm_f" + tag)
    if last:
        out = _post_loss(x1, f, mrow(5), (p["g_post_ffn"], l), target, "post_loss")
    else:
        out = _post(x1, f, mrow(5), (p["g_post_ffn"], l), *nxt, "post_ffn" + tag, dep=tok)
    saved = dict(x=x, h1=h1, z=z, fcol=fcol, frow=frow, lse_fox=lse_fox, hc=hc, lse_swa=lse_swa, ys=ys, yn=yn, o=o, x1=x1,
                 h2=h2, u4=u4, cv4=cv4, act=act, f=f)
    return out, saved


def _tie(a, token):
    return a if token is None else a + token[0, 0]


def _layer_bwd(l, dx2, sv, mod, p, wg, emit, dep=None):
    s = dx2.shape[0]
    tm = min(TM, s)
    tag = f"_l{l}"
    mrow = lambda k: (mod, 6 * l + k)
    g = {}
    df, g["ga2"], g["g_post_ffn"] = _post_bwd(dx2, sv["f"], mrow(5), (p["g_post_ffn"], l), "post_ffn_bwd" + tag, dep=dep)
    dact = _matmul(df, wg["w_down"].get(None), "nt", (FF_NBLK, s, FF_BLK), BF16, (FF_NBLK, 1, 1),
                   _bs((s, D), lambda j, i, k: (0, 0)), _bs((FF_BLK, D), lambda j, i, k: (j, 0)),
                   _bs((None, s, FF_BLK), lambda j, i, k: (j, 0, 0)), None, "mm_dact" + tag)
    tok = emit("w_down", _matmul(sv["act"], df, "tn", (FF_NBLK * FF_BLK, D), BF16, (FF_NBLK, 1, 1),
                                 _bs((None, s, FF_BLK), lambda j, i, k: (j, 0, 0)), _bs((s, D), lambda j, i, k: (0, 0)),
                                 _bs((FF_BLK, D), lambda j, i, k: (j, 0)), None, "mm_dwdown" + tag))
    du, g["ffn_conv_w"], g["ffn_conv_b"], dh2 = _ffn_bwd(sv["u4"], sv["cv4"], dact, wg["ffn_conv_w"], wg["w_up"].get(None),
                                                         "ffn_bwd" + tag, dep=tok)
    du = du.reshape(N_DEV, s, FF_BLK)
    tok = emit("w_up", _matmul(du, sv["h2"], "tn", (N_DEV, FF_BLK, D), BF16, (N_DEV, 1, 1),
                               _bs((None, s, FF_BLK), lambda j, i, k: (j, 0, 0)), _bs((s, D), lambda j, i, k: (0, 0)),
                               _bs((None, FF_BLK, D), lambda j, i, k: (j, 0, 0)), None, "mm_dwup" + tag))
    dx1, g["sh2"], g["sc2"], g["g_pre_ffn"] = _pre_bwd(dh2, sv["x1"], dx2, (p["g_pre_ffn"], l), mrow(4), "pre_ffn_bwd" + tag,
                                                       dep=tok)
    do, g["ga1"], g["g_post_mix"] = _post_bwd(dx1, sv["o"], mrow(2), (p["g_post_mix"], l), "post_mix_bwd" + tag)
    dyn = _mm_rows(do, wg["w_out"].get(None), "nt", D, BF16, "mm_dyn" + tag)
    tok = emit("w_out", _mm_wgrad(sv["yn"], do, BF16, "mm_dwout" + tag))
    dy_fox, dy_conv, dy_swa, dy_sgu, g["g_group"] = _gnorm_bwd(dyn, sv["ys"], (p["g_group"], l), "gnorm_bwd" + tag, dep=tok)
    z = sv["z"]
    dq_f, dk_f, dv_f, dfrow, dfcol = _fox_bwd(z, sv["fcol"], sv["frow"], sv["lse_fox"], sv["ys"][0], dy_fox, "fox_bwd" + tag)
    dgate, g["bf"] = _fgate_bwd(z, p["bf"], dfrow, dfcol, "fgate_bwd" + tag)
    dhc, g["conv_pw_w"], g["conv_pw_b"], g["conv_ln_g"], g["conv_ln_b"], g["conv_w"], g["conv_b"] = _conv_bwd_a(
        z, sv["hc"], dy_conv, wg["conv_w"], p["conv_ln_g"], p["conv_ln_b"], wg["conv_pw_w"], "conv_bwd_a" + tag)
    da_c, dg_c = _conv_bwd_b(z, dhc, wg["conv_w"], "conv_bwd_b" + tag)
    dq_s, dk_s, dv_s, g["sinks"] = _swa_bwd(z, p["sinks"], sv["lse_swa"], sv["ys"][2], dy_swa, "swa_bwd" + tag)
    du_g, dv_g, g["wcat"], g["sgu_bcol"], g["sgu_ln_g"], g["sgu_ln_b"] = _sgu_bwd(
        z, dy_sgu, p["sgu_ln_g"], p["sgu_ln_b"], p["wcat"], p["bfull"], "sgu_bwd" + tag)
    dz = jnp.concatenate([dq_f.astype(BF16), dk_f.astype(BF16), dv_f.astype(BF16), da_c, dg_c, dq_s.astype(BF16), dk_s.astype(BF16),
                          dv_s.astype(BF16), du_g, dv_g, dgate], axis=1)
    tok = emit("w_in", _mm_wgrad(sv["h1"], dz, BF16, "mm_dwin" + tag))
    dh1 = _mm_rows(dz, wg["w_in"].get(None), "nt", D, BF16, "mm_dh1" + tag)
    dx, g["sh1"], g["sc1"], g["g_pre_mix"] = _pre_bwd(dh1, sv["x"], dx1, (p["g_pre_mix"], l), mrow(1), "pre_mix_bwd" + tag,
                                                      dep=tok)
    return dx, g


def _layer_params(l, small, conv_w_full, conv_pw_full, ffn_conv_w_full):
    bf = jnp.pad(small["b_fgate"][l][None, :], ((0, 0), (0, 124)))
    p = dict(
        bf=bf, conv_b=small["conv_b"][l][None], conv_ln_g=small["conv_ln_g"][l][None], conv_ln_b=small["conv_ln_b"][l][None],
        conv_pw_b=small["conv_pw_b"][l][None], sinks=small["swa_sinks"][l], sgu_ln_g=small["sgu_ln_g"][l][None],
        sgu_ln_b=small["sgu_ln_b"][l][None], wcat=_wcat(small["sgu_w"][l]),
        bfull=jnp.repeat(small["sgu_b"][l].T, HD, axis=1),
        ffn_conv_b=small["ffn_conv_b"][l].reshape(2, FF_NBLK, 1, FF_BLK),
        g_group=small["g_group"].reshape(N_LAYER, 1, D), g_post_mix=small["g_post_mix"].reshape(N_LAYER, 1, D),
        g_pre_ffn=small["g_pre_ffn"].reshape(N_LAYER, 1, D), g_post_ffn=small["g_post_ffn"].reshape(N_LAYER, 1, D),
        g_pre_mix=small["g_pre_mix"].reshape(N_LAYER, 1, D))
    wsmall = dict(conv_w=conv_w_full[l], conv_pw_w=conv_pw_full[l].astype(BF16),
                  ffn_conv_w=ffn_conv_w_full[l].reshape(3, 2, FF_NBLK, FF_BLK).transpose(1, 2, 0, 3))
    return p, wsmall


def _local_step(x, target, mod, small, wbig, conv_w_full, conv_pw_full, ffn_conv_w_full, emit, on_loss=None):
    ps, wgs = [], []
    for l in range(N_LAYER):
        p, wsmall = _layer_params(l, small, conv_w_full, conv_pw_full, ffn_conv_w_full)
        ps.append(p)
        wgs.append({**wbig[l], **wsmall})
    h = _rms_mod(x, (ps[0]["g_pre_mix"], 0), (mod, 1), (mod, 0), "rms_mod_l0")
    saved = []
    for l in range(N_LAYER):
        last = l == N_LAYER - 1
        nxt = None if last else ((ps[l]["g_pre_mix"], l + 1), (mod, 6 * (l + 1) + 1), (mod, 6 * (l + 1)))
        out, sv = _layer_fwd(l, x, h, mod, ps[l], wgs[l], last, target, nxt, None if last else wgs[l + 1]["w_in"])
        saved.append(sv)
        if not last:
            x, h = out
    dx, loss = out
    dep = None if on_loss is None else on_loss(loss)
    grads = [None] * N_LAYER
    for l in reversed(range(N_LAYER)):
        dx, grads[l] = _layer_bwd(l, dx, saved[l], mod, ps[l], wgs[l], functools.partial(emit, l), dep)
        dep = None
    return loss, dx, grads


_SMALL = ("b_ada", "g_pre_mix", "g_post_mix", "g_pre_ffn", "g_post_ffn", "b_fgate", "conv_b", "conv_ln_g", "conv_ln_b",
          "conv_pw_b", "swa_sinks", "sgu_ln_g", "sgu_ln_b", "sgu_w", "sgu_b", "g_group", "ffn_conv_b")
_WEIGHTS = ("w_ada", "b_ada", "g_pre_mix", "g_post_mix", "g_pre_ffn", "g_post_ffn", "w_in", "b_fgate", "conv_w", "conv_b",
            "conv_ln_g", "conv_ln_b", "conv_pw_w", "conv_pw_b", "swa_sinks", "sgu_ln_g", "sgu_ln_b", "sgu_w", "sgu_b",
            "g_group", "w_out", "ffn_w_up", "ffn_conv_w", "ffn_conv_b", "ffn_w_down")


def _pad_rows(a, mult):
    r = (-a.shape[0]) % mult
    return a if r == 0 else jnp.concatenate([a, jnp.zeros((r,) + a.shape[1:], a.dtype)], axis=0)


def _view2d(a):
    if a.ndim == 2:
        return a
    return a.reshape(-1, a.shape[-1])


def kernel(x, c, w_ada, b_ada, g_pre_mix, g_post_mix, g_pre_ffn, g_post_ffn, w_in, b_fgate, conv_w, conv_b, conv_ln_g, conv_ln_b, conv_pw_w, conv_pw_b, swa_sinks, sgu_ln_g, sgu_ln_b, sgu_w, sgu_b, g_group, w_out, ffn_w_up, ffn_conv_w, ffn_conv_b, ffn_w_down, loss_target, m_w_ada, m_b_ada, m_g_pre_mix, m_g_post_mix, m_g_pre_ffn, m_g_post_ffn, m_w_in, m_b_fgate, m_conv_w, m_conv_b, m_conv_ln_g, m_conv_ln_b, m_conv_pw_w, m_conv_pw_b, m_swa_sinks, m_sgu_ln_g, m_sgu_ln_b, m_sgu_w, m_sgu_b, m_g_group, m_w_out, m_ffn_w_up, m_ffn_conv_w, m_ffn_conv_b, m_ffn_w_down, v_w_ada, v_b_ada, v_g_pre_mix, v_g_post_mix, v_g_pre_ffn, v_g_post_ffn, v_w_in, v_b_fgate, v_conv_w, v_conv_b, v_conv_ln_g, v_conv_ln_b, v_conv_pw_w, v_conv_pw_b, v_swa_sinks, v_sgu_ln_g, v_sgu_ln_b, v_sgu_w, v_sgu_b, v_g_group, v_w_out, v_ffn_w_up, v_ffn_conv_w, v_ffn_conv_b, v_ffn_w_down):
    env = dict(locals())
    w = {n: env[n] for n in _WEIGHTS}
    mom = {n: env["m_" + n] for n in _WEIGHTS}
    var = {n: env["v_" + n] for n in _WEIGHTS}
    me = 4 * lax.axis_index("x") + 2 * lax.axis_index("y") + lax.axis_index("c")
    x2, target = x[0], loss_target[0]

    (c_all,) = _exchange([c], ["bcast"], "gather_c")
    c_all = c_all.reshape(N_DEV, D)
    (m_all,) = _exchange([_ada_fwd(c_all, w_ada)], ["bcast"], "gather_mod")
    m_mine = lax.dynamic_index_in_dim(m_all, me, axis=2, keepdims=False)
    mod, mod_token = _ada_finish(m_mine.transpose(1, 0, 2).reshape(N_LAYER, 6 * D), b_ada)
    mod = mod.reshape(6 * N_LAYER, 1, D)

    shards = [_tie(conv_w, mod_token), conv_pw_w, ffn_conv_w]
    for l in range(N_LAYER):
        shards += [_perm_in(w_in[l]).astype(BF16), w_out[l].astype(BF16), ffn_w_up[l].astype(BF16), ffn_w_down[l].astype(BF16)]
    gather = _g2_start(shards, "gather_weights_start")
    mod = _tie(mod, gather.token)
    _g2_relay(gather, [0, 1, 2, 3], mod, "gather_relay_first")
    g_cw, g_pw, g_fcw = _g2_wait(gather, [0, 1, 2], mod, "gather_small_wait")
    conv_w_full = g_cw.transpose(1, 2, 0, 3).reshape(N_LAYER, CONV_K, GW)
    conv_pw_full = g_pw.transpose(1, 0, 2, 3).reshape(N_LAYER, GW, GW)
    ffn_conv_w_full = g_fcw.transpose(1, 2, 0, 3).reshape(N_LAYER, 3, N_DEV * FF_BLK)

    def lazy(i, shape, key):
        pre = None if i == 3 else (lambda after: _g2_relay(gather, [i], after, "relay_" + key))
        return _Lazy(lambda after: _g2_wait(gather, [i], after, "wait_" + key)[0].reshape(shape), pre)

    wbig = [dict(w_in=lazy(3 + 4 * l, (D, ZW), f"w_in_l{l}"), w_out=lazy(4 + 4 * l, (D, D), f"w_out_l{l}"),
                 w_up=lazy(5 + 4 * l, (N_DEV, D, FF_BLK), f"w_up_l{l}"),
                 w_down=lazy(6 + 4 * l, (FF_NBLK * FF_BLK, D), f"w_down_l{l}")) for l in range(N_LAYER)]

    grad_flights = []

    def emit(l, key, arr):
        fl = _xchg_start([arr.reshape(N_DEV, -1, arr.shape[-1])], ["a2a"], f"grad_start_{key}_l{l}")
        grad_flights.append(((l, key), fl))
        return fl.token

    small = {n: w[n] for n in _SMALL}
    total = []

    def on_loss(loss8):
        total.append(lax.psum(loss8[0, 0], ("x", "y", "c")))
        return total[0].reshape(1, 1)

    _, dx, grads = _local_step(x2, target, mod, small, wbig, conv_w_full, conv_pw_full, ffn_conv_w_full, emit, on_loss)
    loss = total[0]
    grad_x = dx[None]


    st = lambda key: jnp.stack([grads[l][key] for l in range(N_LAYER)])
    d_conv_w = st("conv_w")[:, :CONV_K, :].reshape(N_LAYER, CONV_K, N_DEV, GW // N_DEV).transpose(2, 0, 1, 3)
    d_pw_w = st("conv_pw_w").reshape(N_LAYER, N_DEV, GW // N_DEV, GW).transpose(1, 0, 2, 3)
    d_fcw = st("ffn_conv_w").reshape(N_LAYER, N_DEV, 3, FF_BLK).transpose(1, 0, 2, 3)
    rows_d = _pad_rows(jnp.concatenate(
        [grads[l][k] for l in range(N_LAYER) for k in ("sh1", "sc1", "ga1", "sh2", "sc2", "ga2")]
        + [grads[l][k] for k in ("g_pre_mix", "g_post_mix", "g_pre_ffn", "g_post_ffn", "g_group") for l in range(N_LAYER)],
        axis=0), 8)
    rows_gw = _pad_rows(jnp.concatenate(
        [grads[l][k] for k in ("conv_b", "conv_ln_g", "conv_ln_b", "conv_pw_b", "sgu_ln_g", "sgu_ln_b") for l in range(N_LAYER)],
        axis=0), 8)
    rows_128 = jnp.concatenate([_pad_rows(jnp.concatenate([grads[l]["bf"] for l in range(N_LAYER)]
                                                          + [grads[l]["sinks"] for l in range(N_LAYER)], axis=0), 8)]
                               + [grads[l]["sgu_bcol"] for l in range(N_LAYER)], axis=0)
    rows_w = jnp.concatenate([grads[l]["wcat"] for l in range(N_LAYER)], axis=0)
    rows_fb = st("ffn_conv_b").reshape(N_LAYER * N_DEV, FF_BLK)
    small_flight = _xchg_start([d_conv_w, d_pw_w, d_fcw, rows_d, rows_gw, rows_128, rows_w, rows_fb],
                               ["a2a"] * 3 + ["bcast"] * 5, "small_grads_start")

    to_mem = {"w_in": lambda a: a.transpose(2, 0, 1), "ffn_w_up": lambda a: a.transpose(0, 2, 1)}
    from_mem = {"w_in": lambda a: a.transpose(1, 2, 0), "ffn_w_up": lambda a: a.transpose(0, 2, 1)}
    flights = dict(grad_flights)
    gr, delta, new_m, new_v = {}, {}, {}, {}

    def adamw_big(n):
        view, back = to_mem.get(n, lambda a: a), from_mem.get(n, lambda a: a)
        shape = view(w[n]).shape
        d, m2, v2 = _adamw(_view2d(view(w[n])), _view2d(gr[n]), _view2d(view(mom[n])), _view2d(view(var[n])), "adamw_" + n)
        delta[n], new_m[n], new_v[n] = back(d.reshape(shape)), back(m2.reshape(shape)), back(v2.reshape(shape))
        gr[n] = back(gr[n].reshape(shape))

    after = small_flight.token
    for n, key in (("ffn_w_down", "w_down"), ("ffn_w_up", "w_up"), ("w_out", "w_out"), ("w_in", "w_in")):
        parts = [_xchg_wait(flights[(l, key)], [0], after, f"grad_wait_{key}_l{l}")[0] for l in reversed(range(N_LAYER))]
        g = _sum8(parts[::-1], "sum_" + key)
        gr[n] = to_mem["w_in"](_unperm_in(g)) if n == "w_in" else g
        adamw_big(n)
        after = new_v[n]

    small_parts = _xchg_wait(small_flight, list(range(8)), after, "small_grads_wait")
    s_conv_w, s_pw_w, s_fcw, s_d, s_gw, s_128, s_w, s_fb = _sum8_small(
        [p.reshape(N_DEV, -1, p.shape[-1]) for p in small_parts], "sum_small_grads")
    gr["conv_w"] = s_conv_w.reshape(N_LAYER, CONV_K, GW // N_DEV)
    gr["conv_pw_w"] = s_pw_w.reshape(N_LAYER, GW // N_DEV, GW)
    gr["ffn_conv_w"] = s_fcw.reshape(N_LAYER, 3, FF_BLK)
    gr["b_ada"] = s_d[:6 * N_LAYER].reshape(N_LAYER, 6 * D)
    for i, k in enumerate(("g_pre_mix", "g_post_mix", "g_pre_ffn", "g_post_ffn", "g_group")):
        gr[k] = s_d[6 * N_LAYER + 2 * i:6 * N_LAYER + 2 * i + 2]
    for i, k in enumerate(("conv_b", "conv_ln_g", "conv_ln_b", "conv_pw_b", "sgu_ln_g", "sgu_ln_b")):
        gr[k] = s_gw[2 * i:2 * i + 2]
    gr["b_fgate"] = s_128[0:2, :4]
    gr["swa_sinks"] = s_128[2:4, :4]
    gr["sgu_b"] = s_128[8:].reshape(N_LAYER, WIN, 128)[:, :, :4].transpose(0, 2, 1)
    gr["sgu_w"] = s_w.reshape(N_LAYER, WIN, 4, WIN).transpose(0, 2, 1, 3)
    gr["ffn_conv_b"] = s_fb.reshape(N_LAYER, N_DEV * FF_BLK)
    dmod_all = small_parts[3][:, :6 * N_LAYER, :].reshape(N_DEV, N_LAYER, 6 * D)
    ncol = 6 * D // N_DEV
    dmod_cols = lax.dynamic_slice_in_dim(dmod_all, me * ncol, ncol, axis=2).transpose(1, 0, 2)
    gr["w_ada"] = _ada_bwd(c_all, dmod_cols)

    adamw_big("w_ada")
    smalls = [n for n in _WEIGHTS if n not in ("w_ada", "w_in", "w_out", "ffn_w_up", "ffn_w_down")]
    ds, ms, vs = _adamw_small([_view2d(w[n]) for n in smalls], [_view2d(gr[n]) for n in smalls],
                              [_view2d(mom[n]) for n in smalls], [_view2d(var[n]) for n in smalls], "adamw_small")
    for i, n in enumerate(smalls):
        delta[n], new_m[n], new_v[n] = ds[i].reshape(w[n].shape), ms[i].reshape(w[n].shape), vs[i].reshape(w[n].shape)

    return (loss, grad_x, *[gr[n].reshape(w[n].shape) for n in _WEIGHTS], *[delta[n] for n in _WEIGHTS],
            *[new_m[n] for n in _WEIGHTS], *[new_v[n] for n in _WEIGHTS])
```

```python
import functools

import jax
import jax.numpy as jnp
from jax import lax
from jax.experimental import pallas as pl
from jax.experimental.pallas import tpu as pltpu

F32, BF16 = jnp.float32, jnp.bfloat16
SDS = jax.ShapeDtypeStruct
MESH = pl.DeviceIdType.MESH

N_DEV = 8
D = 1024
GW = 256
HD = 64
N_LAYER = 2
ZW = 2432
FF_BLK = 704
FF_NBLK = 4
CONV_K = 31
CONV_HALO = 32
FFN_HALO = 16
FFN_SUB = 256
EPS = 1e-6
NEG = -1e30
SCALE = HD ** -0.5
VMEM_LIMIT_V7X = 56 * 1024 * 1024
TM = 512
WGRAD_ROWS = 256
TQ = 256
FOX_FWD_PAIR = 2
FOX_BWD_PATHS = (3, 5)
WIN = 128

ADAM_LR, ADAM_B1, ADAM_B2, ADAM_EPS, ADAM_WD, ADAM_STEP = 0.001, 0.9, 0.999, 1e-08, 0.01, 10

Z_FQ, Z_FK, Z_FV, Z_CA, Z_CG, Z_SQ = 0, 1, 2, 3, 4, 5
Z_SK, Z_SV = 12, 13
Z_GU, Z_GV = 7, 8
Z_FG = 18


def _cp(sem=None):
    return pltpu.CompilerParams(dimension_semantics=sem, vmem_limit_bytes=VMEM_LIMIT_V7X)


def _vec(arr3, idx, ngrid):
    w = arr3.shape[-1]
    if ngrid == 1:
        return pl.BlockSpec((None, 1, w), lambda i: (idx, 0, 0))
    return pl.BlockSpec((None, 1, w), lambda i, j: (idx, 0, 0))


def _sigmoid(x):
    return jax.nn.sigmoid(x)


def _silu(x):
    return x * _sigmoid(x)


def _dsilu(x):
    s = _sigmoid(x)
    return s * (1.0 + x * (1.0 - s))


_G0, _G1 = 0.7978845608028654, 0.044715


def _gelu(x):
    return 0.5 * x * (1.0 + jnp.tanh(_G0 * (x + _G1 * x * x * x)))


def _dgelu(x):
    t = jnp.tanh(_G0 * (x + _G1 * x * x * x))
    return 0.5 * (1.0 + t) + 0.5 * x * (1.0 - t * t) * (_G0 * (1.0 + 3.0 * _G1 * x * x))


def _rstd(x):
    return lax.rsqrt(jnp.mean(x * x, axis=-1, keepdims=True) + EPS)


def _rms_bwd(xh, r, t):
    return r * (t - xh * jnp.mean(t * xh, axis=-1, keepdims=True))


def _ln_stats(x):
    mu = jnp.mean(x, axis=-1, keepdims=True)
    xc = x - mu
    rstd = lax.rsqrt(jnp.mean(xc * xc, axis=-1, keepdims=True) + EPS)
    return xc * rstd, rstd


def _ln_bwd(xh, rstd, dxh):
    return rstd * (dxh - jnp.mean(dxh, axis=-1, keepdims=True) - xh * jnp.mean(dxh * xh, axis=-1, keepdims=True))


def _colsum(x):
    return jnp.sum(x, axis=0, keepdims=True)


def _dot(a, b, kind):
    dn = {"nn": (((1,), (0,)), ((), ())), "nt": (((1,), (1,)), ((), ())), "tn": (((0,), (0,)), ((), ()))}[kind]
    return lax.dot_general(a.astype(BF16), b.astype(BF16), dn, preferred_element_type=F32)


def _exchange(arrs, modes, name):
    n = len(arrs)
    outs = [SDS((N_DEV,) + a.shape, a.dtype) if m == "bcast" else SDS(a.shape, a.dtype) for a, m in zip(arrs, modes)]

    def body(*refs):
        ins, dst = refs[:n], refs[n:2 * n]
        send, recv, loc = refs[2 * n:]
        x, y, c = lax.axis_index("x"), lax.axis_index("y"), lax.axis_index("c")
        me = 4 * x + 2 * y + c

        def src(a, j):
            return ins[a] if modes[a] == "bcast" else ins[a].at[j]

        local = [pltpu.make_async_copy(src(a, me), dst[a].at[me], loc.at[a]) for a in range(n)]
        for cp in local:
            cp.start()
        sent, landed = [], []
        for k in (2, 4, 6, 3, 5, 7, 1):
            px = 1 - x if k & 4 else x
            py = 1 - y if k & 2 else y
            pc = 1 - c if k & 1 else c
            peer = 4 * px + 2 * py + pc
            for a in range(n):
                cp = pltpu.make_async_remote_copy(src_ref=src(a, peer), dst_ref=dst[a].at[me], send_sem=send.at[a, k - 1],
                                                  recv_sem=recv.at[a, k - 1], device_id=(px, py, pc), device_id_type=MESH)
                cp.start()
                sent.append(cp)
                landed.append(pltpu.make_async_remote_copy(src_ref=src(a, peer), dst_ref=dst[a].at[peer],
                                                           send_sem=send.at[a, k - 1], recv_sem=recv.at[a, k - 1],
                                                           device_id=(px, py, pc), device_id_type=MESH))
        for cp in landed:
            cp.wait_recv()
        for cp in sent:
            cp.wait_send()
        for cp in local:
            cp.wait()

    hbm = pl.BlockSpec(memory_space=pltpu.HBM)
    return pl.pallas_call(
        body, name=name, out_shape=outs, in_specs=[hbm] * n, out_specs=[hbm] * n,
        scratch_shapes=[pltpu.SemaphoreType.DMA((n, N_DEV - 1)), pltpu.SemaphoreType.DMA((n, N_DEV - 1)),
                        pltpu.SemaphoreType.DMA((n,))],
        compiler_params=pltpu.CompilerParams(has_side_effects=True),
    )(*arrs)


_PEER_ORDER = (2, 4, 6, 3, 5, 7, 1)
_HBM = pl.BlockSpec(memory_space=pltpu.HBM)
_SEM = pl.BlockSpec(memory_space=pltpu.SEMAPHORE)
_EFFECT = pltpu.SideEffectType.DATAFLOW_SIDE_EFFECTING


def _peer(k):
    x, y, c = lax.axis_index("x"), lax.axis_index("y"), lax.axis_index("c")
    px = 1 - x if k & 4 else x
    py = 1 - y if k & 2 else y
    pc = 1 - c if k & 1 else c
    return (px, py, pc), 4 * px + 2 * py + pc


def _my_id():
    return 4 * lax.axis_index("x") + 2 * lax.axis_index("y") + lax.axis_index("c")


def _split_copies(src_ref, land_ref, send, recv, loc, mode):
    me = _my_id()
    pick = (lambda j: src_ref) if mode == "bcast" else (lambda j: src_ref.at[j])
    local = pltpu.make_async_copy(pick(me), land_ref.at[me], loc)
    remote = []
    for k in _PEER_ORDER:
        dev, peer = _peer(k)
        out = pltpu.make_async_remote_copy(src_ref=pick(peer), dst_ref=land_ref.at[me], send_sem=send.at[k - 1],
                                           recv_sem=recv.at[k - 1], device_id=dev, device_id_type=MESH)
        arrive = pltpu.make_async_remote_copy(src_ref=pick(peer), dst_ref=land_ref.at[peer], send_sem=send.at[k - 1],
                                              recv_sem=recv.at[k - 1], device_id=dev, device_id_type=MESH)
        remote.append((out, arrive))
    return local, remote


class _Flight:
    def __init__(self, srcs, lands, sends, recvs, locs, modes, token):
        self.srcs, self.lands, self.sends, self.recvs, self.locs, self.modes, self.token = (
            list(srcs), list(lands), list(sends), list(recvs), list(locs), list(modes), token)


def _xchg_start(arrs, modes, name):
    n = len(arrs)
    lands = [lax.empty((N_DEV,) + a.shape if m == "bcast" else a.shape, a.dtype) for a, m in zip(arrs, modes)]

    def body(*refs):
        srcs, lnds = refs[:n], refs[n:2 * n]
        outs = refs[2 * n:]
        sends, recvs, locs, token = outs[:n], outs[n:2 * n], outs[2 * n:3 * n], outs[5 * n]
        for a in range(n):
            local, remote = _split_copies(srcs[a], lnds[a], sends[a], recvs[a], locs[a], modes[a])
            local.start()
            for out, _ in remote:
                out.start()
        token[...] = jnp.zeros_like(token)

    sem7 = pltpu.SemaphoreType.DMA((N_DEV - 1,))
    res = pl.pallas_call(
        body, name=name,
        out_shape=[sem7] * (2 * n) + [pltpu.SemaphoreType.DMA(())] * n + [pltpu.HBM(a.shape, a.dtype) for a in arrs]
        + [pltpu.HBM(b.shape, b.dtype) for b in lands] + [SDS((8, 128), F32)],
        in_specs=[_HBM] * (2 * n), out_specs=[_SEM] * (3 * n) + [_HBM] * (2 * n) + [pl.BlockSpec(memory_space=pltpu.VMEM)],
        input_output_aliases={i: 3 * n + i for i in range(2 * n)},
        compiler_params=pltpu.CompilerParams(has_side_effects=_EFFECT),
    )(*[pltpu.with_memory_space_constraint(a, pltpu.HBM) for a in arrs],
      *[pltpu.with_memory_space_constraint(b, pltpu.HBM) for b in lands])
    return _Flight(res[3 * n:4 * n], res[4 * n:5 * n], res[:n], res[n:2 * n], res[2 * n:3 * n], modes, res[5 * n])


def _xchg_wait(flight, idx, after, name):
    n = len(idx)
    modes = [flight.modes[i] for i in idx]

    def body(*refs):
        srcs, lnds = refs[:n], refs[n:2 * n]
        sends, recvs, locs = refs[2 * n:3 * n], refs[3 * n:4 * n], refs[4 * n:5 * n]
        for a in range(n):
            local, remote = _split_copies(srcs[a], lnds[a], sends[a], recvs[a], locs[a], modes[a])
            local.wait()
            for _, arrive in remote:
                arrive.wait_send()
                arrive.wait_recv()

    ops = ([flight.srcs[i] for i in idx] + [flight.lands[i] for i in idx] + [flight.sends[i] for i in idx]
           + [flight.recvs[i] for i in idx] + [flight.locs[i] for i in idx])
    res = pl.pallas_call(
        body, name=name, out_shape=[pltpu.HBM(o.shape, o.dtype) for o in ops[:2 * n]],
        in_specs=[_HBM] * (2 * n) + [_SEM] * (3 * n) + [pl.BlockSpec(memory_space=pl.ANY)], out_specs=[_HBM] * (2 * n),
        input_output_aliases={i: i for i in range(2 * n)},
        compiler_params=pltpu.CompilerParams(has_side_effects=_EFFECT),
    )(*ops, after)
    return res[n:]


class _Lazy:
    def __init__(self, fn, pre=None):
        self.fn, self.pre, self.val, self.started = fn, pre, None, False

    def prefetch(self, after):
        token = self.pre(after) if self.pre is not None and not self.started else None
        self.started = True
        return token

    def get(self, after):
        self.prefetch(after)
        if self.val is None:
            self.val = self.fn(after)
        return self.val


_CHIP_PEERS = (2, 4, 6)


def _g2_copies_a(src_ref, land_ref, send, recv, loc):
    me = _my_id()
    local = pltpu.make_async_copy(src_ref, land_ref.at[me], loc)
    remote = []
    for j, k in enumerate(_CHIP_PEERS + (1,)):
        dev, peer = _peer(k)
        out = pltpu.make_async_remote_copy(src_ref=src_ref, dst_ref=land_ref.at[me], send_sem=send.at[j], recv_sem=recv.at[j],
                                           device_id=dev, device_id_type=MESH)
        arrive = pltpu.make_async_remote_copy(src_ref=src_ref, dst_ref=land_ref.at[peer], send_sem=send.at[j],
                                              recv_sem=recv.at[j], device_id=dev, device_id_type=MESH)
        remote.append((out, arrive))
    return local, remote


def _g2_copies_b(land_ref, send, recv):
    sib, _ = _peer(1)
    pairs = []
    for j, k in enumerate(_CHIP_PEERS):
        _, same_core = _peer(k)
        _, other_core = _peer(k | 1)
        out = pltpu.make_async_remote_copy(src_ref=land_ref.at[same_core], dst_ref=land_ref.at[same_core], send_sem=send.at[j],
                                           recv_sem=recv.at[j], device_id=sib, device_id_type=MESH)
        arrive = pltpu.make_async_remote_copy(src_ref=land_ref.at[same_core], dst_ref=land_ref.at[other_core],
                                              send_sem=send.at[j], recv_sem=recv.at[j], device_id=sib, device_id_type=MESH)
        pairs.append((out, arrive))
    return pairs


class _Gather2:
    def __init__(self, srcs, lands, sends, recvs, locs, token):
        self.srcs, self.lands, self.sends, self.recvs, self.locs, self.token = (
            list(srcs), list(lands), list(sends), list(recvs), list(locs), token)
        self.sends_b, self.recvs_b = [None] * len(self.srcs), [None] * len(self.srcs)


def _g2_start(arrs, name):
    n = len(arrs)
    lands = [lax.empty((N_DEV,) + a.shape, a.dtype) for a in arrs]

    def body(*refs):
        srcs, lnds = refs[:n], refs[n:2 * n]
        outs = refs[2 * n:]
        sends, recvs, locs, token = outs[:n], outs[n:2 * n], outs[2 * n:3 * n], outs[5 * n]
        for a in range(n):
            local, remote = _g2_copies_a(srcs[a], lnds[a], sends[a], recvs[a], locs[a])
            local.start()
            for out, _ in remote:
                out.start()
        token[...] = jnp.zeros_like(token)

    sem4 = pltpu.SemaphoreType.DMA((4,))
    res = pl.pallas_call(
        body, name=name,
        out_shape=[sem4] * (2 * n) + [pltpu.SemaphoreType.DMA(())] * n + [pltpu.HBM(a.shape, a.dtype) for a in arrs]
        + [pltpu.HBM(b.shape, b.dtype) for b in lands] + [SDS((8, 128), F32)],
        in_specs=[_HBM] * (2 * n), out_specs=[_SEM] * (3 * n) + [_HBM] * (2 * n) + [pl.BlockSpec(memory_space=pltpu.VMEM)],
        input_output_aliases={i: 3 * n + i for i in range(2 * n)},
        compiler_params=pltpu.CompilerParams(has_side_effects=_EFFECT),
    )(*[pltpu.with_memory_space_constraint(a, pltpu.HBM) for a in arrs],
      *[pltpu.with_memory_space_constraint(b, pltpu.HBM) for b in lands])
    return _Gather2(res[3 * n:4 * n], res[4 * n:5 * n], res[:n], res[n:2 * n], res[2 * n:3 * n], res[5 * n])


def _g2_relay(g, idx, after, name):
    n = len(idx)

    def body(*refs):
        srcs, lnds = refs[:n], refs[n:2 * n]
        sends, recvs, locs = refs[2 * n:3 * n], refs[3 * n:4 * n], refs[4 * n:5 * n]
        outs = refs[5 * n + 1:]
        sends_b, recvs_b = outs[2 * n:3 * n], outs[3 * n:4 * n]
        for a in range(n):
            local, remote = _g2_copies_a(srcs[a], lnds[a], sends[a], recvs[a], locs[a])
            local.wait()
            for _, arrive in remote:
                arrive.wait_send()
                arrive.wait_recv()
        for a in range(n):
            for out, _ in _g2_copies_b(lnds[a], sends_b[a], recvs_b[a]):
                out.start()
        outs[4 * n][...] = jnp.zeros_like(outs[4 * n])

    ops = ([g.srcs[i] for i in idx] + [g.lands[i] for i in idx] + [g.sends[i] for i in idx] + [g.recvs[i] for i in idx]
           + [g.locs[i] for i in idx])
    sem3 = pltpu.SemaphoreType.DMA((3,))
    res = pl.pallas_call(
        body, name=name, out_shape=[pltpu.HBM(o.shape, o.dtype) for o in ops[:2 * n]] + [sem3] * (2 * n) + [SDS((8, 128), F32)],
        in_specs=[_HBM] * (2 * n) + [_SEM] * (3 * n) + [pl.BlockSpec(memory_space=pl.ANY)],
        out_specs=[_HBM] * (2 * n) + [_SEM] * (2 * n) + [pl.BlockSpec(memory_space=pltpu.VMEM)],
        input_output_aliases={i: i for i in range(2 * n)},
        compiler_params=pltpu.CompilerParams(has_side_effects=_EFFECT),
    )(*ops, after)
    for a, i in enumerate(idx):
        g.srcs[i], g.lands[i] = res[a], res[n + a]
        g.sends_b[i], g.recvs_b[i] = res[2 * n + a], res[3 * n + a]
    return res[4 * n]


def _g2_wait(g, idx, after, name):
    n = len(idx)

    def body(*refs):
        lnds, sends_b, recvs_b = refs[:n], refs[n:2 * n], refs[2 * n:3 * n]
        for a in range(n):
            for _, arrive in _g2_copies_b(lnds[a], sends_b[a], recvs_b[a]):
                arrive.wait_send()
                arrive.wait_recv()

    ops = [g.lands[i] for i in idx] + [g.sends_b[i] for i in idx] + [g.recvs_b[i] for i in idx]
    res = pl.pallas_call(
        body, name=name, out_shape=[pltpu.HBM(o.shape, o.dtype) for o in ops[:n]],
        in_specs=[_HBM] * n + [_SEM] * (2 * n) + [pl.BlockSpec(memory_space=pl.ANY)], out_specs=[_HBM] * n,
        input_output_aliases={i: i for i in range(n)},
        compiler_params=pltpu.CompilerParams(has_side_effects=_EFFECT),
    )(*ops, after)
    return list(res)


def _matmul(a, b, kind, out_shape, out_dtype, grid, a_spec, b_spec, o_spec, acc_shape, name):
    nk = grid[2]

    def body(a_ref, b_ref, o_ref, *scratch):
        prod = _dot(a_ref[...], b_ref[...], kind)
        if nk == 1:
            o_ref[...] = prod.astype(out_dtype)
        else:
            acc = scratch[0]
            k = pl.program_id(2)

            @pl.when(k == 0)
            def _():
                acc[...] = prod

            @pl.when(k > 0)
            def _():
                acc[...] += prod

            @pl.when(k == nk - 1)
            def _():
                o_ref[...] = acc[...].astype(out_dtype)

    return pl.pallas_call(
        body, name=name, grid=grid, in_specs=[a_spec, b_spec], out_specs=o_spec, out_shape=SDS(out_shape, out_dtype),
        scratch_shapes=[] if nk == 1 else [pltpu.VMEM(acc_shape, F32)],
        compiler_params=_cp(("parallel", "parallel", "arbitrary")))(a, b)


def _bs(shape, fn):
    return pl.BlockSpec(shape, fn)


def _mm_rows(a, w, kind, n_out, out_dtype, name):
    s, k = a.shape
    tm = min(TM, s)
    return _matmul(a, w, kind, (s, n_out), out_dtype, (s // tm, 1, 1),
                   _bs((tm, k), lambda i, j, kk: (i, 0)), _bs(w.shape, lambda i, j, kk: (0, 0)),
                   _bs((tm, n_out), lambda i, j, kk: (i, 0)), None, name)


def _mm_wgrad(a, dy, out_dtype, name):
    s, k = a.shape
    n = dy.shape[1]
    tko = min(WGRAD_ROWS, k)
    return _matmul(a, dy, "tn", (k, n), out_dtype, (k // tko, 1, 1),
                   _bs((s, tko), lambda i, j, kk: (0, i)), _bs((s, n), lambda i, j, kk: (0, 0)),
                   _bs((tko, n), lambda i, j, kk: (i, 0)), None, name)


def _ada_fwd(c_all, w_ada):
    ncol = w_ada.shape[2]

    def body(c_ref, w_ref, o_ref):
        ca = _silu(c_ref[...])
        ca = jnp.concatenate([ca, jnp.zeros_like(ca)], axis=0)
        o_ref[...] = _dot(ca, w_ref[...], "nn")[:N_DEV, :]

    return pl.pallas_call(
        body, name="ada_fwd", grid=(N_LAYER,),
        in_specs=[pl.BlockSpec((N_DEV, D), lambda l: (0, 0)), pl.BlockSpec((None, D, ncol), lambda l: (l, 0, 0))],
        out_specs=pl.BlockSpec((None, N_DEV, ncol), lambda l: (l, 0, 0)),
        out_shape=SDS((N_LAYER, N_DEV, ncol), F32), compiler_params=_cp(("parallel",)))(c_all, w_ada)


def _ada_finish(m_mine, b_ada):
    def body(m_ref, b_ref, o_ref, t_ref):
        o_ref[...] = m_ref[...] + b_ref[...]
        t_ref[...] = jnp.zeros_like(t_ref)

    return pl.pallas_call(body, name="ada_finish", out_shape=[SDS(b_ada.shape, F32), SDS((8, 128), F32)])(m_mine, b_ada)


def _ada_bwd(c_all, dmod_cols):
    ncol = dmod_cols.shape[2]

    def body(c_ref, d_ref, o_ref):
        ca = _silu(c_ref[...])
        ca = jnp.concatenate([ca, jnp.zeros_like(ca)], axis=0)
        dm = d_ref[...]
        dm = jnp.concatenate([dm, jnp.zeros_like(dm)], axis=0)
        o_ref[...] = _dot(ca, dm, "tn")

    return pl.pallas_call(
        body, name="ada_bwd", grid=(N_LAYER,),
        in_specs=[pl.BlockSpec((N_DEV, D), lambda l: (0, 0)), pl.BlockSpec((None, N_DEV, ncol), lambda l: (l, 0, 0))],
        out_specs=pl.BlockSpec((None, D, ncol), lambda l: (l, 0, 0)),
        out_shape=SDS((N_LAYER, D, ncol), F32), compiler_params=_cp(("parallel",)))(c_all, dmod_cols)


def _rows(s):
    tm = min(TM, s)
    return tm, pl.BlockSpec((tm, D), lambda i: (i, 0))


def _pcall(body, operands, dep, **kw):
    if dep is None:
        return pl.pallas_call(body, **kw)(*operands)
    n = len(operands)

    def body_dep(*refs):
        body(*refs[:n], *refs[n + 1:])

    kw["in_specs"] = list(kw["in_specs"]) + [pl.BlockSpec(memory_space=pl.ANY)]
    return pl.pallas_call(body_dep, **kw)(*operands, dep)


def _rms_mod(x, g, sc, sh, name):
    s = x.shape[0]
    tm, row = _rows(s)

    def body(x_ref, g_ref, sc_ref, sh_ref, h_ref):
        xf = x_ref[...]
        h_ref[...] = (xf * _rstd(xf) * (g_ref[...] * (1.0 + sc_ref[...])) + sh_ref[...]).astype(BF16)

    return pl.pallas_call(
        body, name=name, grid=(s // tm,), in_specs=[row, _vec(*g, 1), _vec(*sc, 1), _vec(*sh, 1)], out_specs=row,
        out_shape=SDS((s, D), BF16), compiler_params=_cp(("parallel",)))(x, g[0], sc[0], sh[0])


def _post(xres, o, ga, gpost, gn, scn, shn, name, dep=None):
    s = xres.shape[0]
    tm, row = _rows(s)

    def body(x_ref, o_ref, ga_ref, gp_ref, gn_ref, sc_ref, sh_ref, xn_ref, h_ref):
        of = o_ref[...].astype(F32)
        xn = x_ref[...] + ga_ref[...] * (of * _rstd(of) * gp_ref[...])
        xn_ref[...] = xn
        h_ref[...] = (xn * _rstd(xn) * (gn_ref[...] * (1.0 + sc_ref[...])) + sh_ref[...]).astype(BF16)

    return _pcall(
        body, (xres, o, ga[0], gpost[0], gn[0], scn[0], shn[0]), dep, name=name, grid=(s // tm,),
        in_specs=[row, row, _vec(*ga, 1), _vec(*gpost, 1), _vec(*gn, 1), _vec(*scn, 1), _vec(*shn, 1)],
        out_specs=[row, row], out_shape=[SDS((s, D), F32), SDS((s, D), BF16)], compiler_params=_cp(("parallel",)))


def _post_loss(xres, o, ga, gpost, target, name, dep=None):
    s = xres.shape[0]
    tm, row = _rows(s)

    def body(x_ref, o_ref, ga_ref, gp_ref, t_ref, dy_ref, loss_ref):
        of = o_ref[...].astype(F32)
        err = x_ref[...] + ga_ref[...] * (of * _rstd(of) * gp_ref[...]) - t_ref[...]
        dy_ref[...] = err * (1.0 / D)

        @pl.when(pl.program_id(0) == 0)
        def _():
            loss_ref[...] = jnp.zeros_like(loss_ref)

        loss_ref[...] += jnp.sum(jnp.mean(err * err, axis=-1, keepdims=True), axis=0, keepdims=True) * 0.5

    return _pcall(
        body, (xres, o, ga[0], gpost[0], target), dep, name=name, grid=(s // tm,),
        in_specs=[row, row, _vec(*ga, 1), _vec(*gpost, 1), row],
        out_specs=[row, pl.BlockSpec((8, 128), lambda i: (0, 0))], out_shape=[SDS((s, D), F32), SDS((8, 128), F32)],
        compiler_params=_cp(("arbitrary",)))


def _acc(ref, val, first):
    @pl.when(first)
    def _():
        ref[...] = val

    @pl.when(jnp.logical_not(first))
    def _():
        ref[...] += val


def _post_bwd(dxn, o, ga, gpost, name, dep=None):
    s = dxn.shape[0]
    tm, row = _rows(s)
    vec = pl.BlockSpec((1, D), lambda i: (0, 0))

    def body(d_ref, o_ref, ga_ref, gp_ref, do_ref, dga_ref, dgp_ref):
        of, dx = o_ref[...].astype(F32), d_ref[...]
        r = _rstd(of)
        oh = of * r
        do_ref[...] = _rms_bwd(oh, r, dx * (ga_ref[...] * gp_ref[...])).astype(BF16)
        cs = _colsum(dx * oh)
        first = pl.program_id(0) == 0
        _acc(dga_ref, cs * gp_ref[...], first)
        _acc(dgp_ref, cs * ga_ref[...], first)

    return _pcall(
        body, (dxn, o, ga[0], gpost[0]), dep, name=name, grid=(s // tm,),
        in_specs=[row, row, _vec(*ga, 1), _vec(*gpost, 1)], out_specs=[row, vec, vec],
        out_shape=[SDS((s, D), BF16), SDS((1, D), F32), SDS((1, D), F32)], compiler_params=_cp(("arbitrary",)))


def _pre_bwd(dh, x, dres, g, sc, name, dep=None):
    s = x.shape[0]
    tm, row = _rows(s)
    vec = pl.BlockSpec((1, D), lambda i: (0, 0))

    def body(dh_ref, x_ref, dr_ref, g_ref, sc_ref, dx_ref, dsh_ref, dsc_ref, dg_ref):
        xf, d = x_ref[...], dh_ref[...].astype(F32)
        r = _rstd(xf)
        xh = xf * r
        dx_ref[...] = dr_ref[...] + _rms_bwd(xh, r, d * (g_ref[...] * (1.0 + sc_ref[...])))
        cs = _colsum(d * xh)
        first = pl.program_id(0) == 0
        _acc(dsh_ref, _colsum(d), first)
        _acc(dsc_ref, cs * g_ref[...], first)
        _acc(dg_ref, cs * (1.0 + sc_ref[...]), first)

    return _pcall(
        body, (dh, x, dres, g[0], sc[0]), dep, name=name, grid=(s // tm,),
        in_specs=[row, row, row, _vec(*g, 1), _vec(*sc, 1)], out_specs=[row, vec, vec, vec],
        out_shape=[SDS((s, D), F32), SDS((1, D), F32), SDS((1, D), F32), SDS((1, D), F32)],
        compiler_params=_cp(("arbitrary",)))


def _gnorm(ys, gg, name):
    s = ys[0].shape[0]
    tm = min(TM, s)
    yb = pl.BlockSpec((tm, GW), lambda i: (i, 0))

    def body(y0, y1, y2, y3, g_ref, o_ref):
        for i, yr in enumerate((y0, y1, y2, y3)):
            y = yr[...]
            o_ref[:, GW * i:GW * (i + 1)] = (y * _rstd(y) * g_ref[:, GW * i:GW * (i + 1)]).astype(BF16)

    return pl.pallas_call(
        body, name=name, grid=(s // tm,), in_specs=[yb] * 4 + [_vec(*gg, 1)], out_specs=pl.BlockSpec((tm, D), lambda i: (i, 0)),
        out_shape=SDS((s, D), BF16), compiler_params=_cp(("parallel",)))(*ys, gg[0])


def _gnorm_bwd(dyn, ys, gg, name, dep=None):
    s = ys[0].shape[0]
    tm = min(TM, s)
    yb = pl.BlockSpec((tm, GW), lambda i: (i, 0))

    def body(d_ref, y0, y1, y2, y3, g_ref, o0, o1, o2, o3, dg_ref):
        first = pl.program_id(0) == 0
        for i, (yr, orf) in enumerate(zip((y0, y1, y2, y3), (o0, o1, o2, o3))):
            y = yr[...]
            d = d_ref[:, GW * i:GW * (i + 1)].astype(F32)
            r = _rstd(y)
            yh = y * r
            orf[...] = _rms_bwd(yh, r, d * g_ref[:, GW * i:GW * (i + 1)]).astype(BF16)
            cs = _colsum(d * yh)

            @pl.when(first)
            def _():
                dg_ref[:, GW * i:GW * (i + 1)] = cs

            @pl.when(jnp.logical_not(first))
            def _():
                dg_ref[:, GW * i:GW * (i + 1)] += cs

    return _pcall(
        body, (dyn, *ys, gg[0]), dep, name=name, grid=(s // tm,),
        in_specs=[pl.BlockSpec((tm, D), lambda i: (i, 0))] + [yb] * 4 + [_vec(*gg, 1)],
        out_specs=[yb] * 4 + [pl.BlockSpec((1, D), lambda i: (0, 0))],
        out_shape=[SDS((s, GW), BF16)] * 4 + [SDS((1, D), F32)], compiler_params=_cp(("arbitrary",)))


def _lane_put(acc, col, h):
    lane = lax.broadcasted_iota(jnp.int32, acc.shape, 1)
    return jnp.where(lane == h, col, acc)


def _fgate(z, bf, name, dep=None):
    s = z.shape[0]

    def body(z_ref, b_ref, fc_ref, fr_ref):
        xg = z_ref[...].astype(F32) + b_ref[...]
        lf = jnp.minimum(xg, 0.0) - jnp.log(1.0 + jnp.exp(-jnp.abs(xg)))
        lane = lax.broadcasted_iota(jnp.int32, lf.shape, 1)
        row = lax.broadcasted_iota(jnp.int32, lf.shape, 0)
        f = jnp.where(lane < 4, lf, 0.0)
        sh = 1
        while sh < s:
            f = f + jnp.where(row >= sh, pltpu.roll(f, sh, 0), 0.0)
            sh *= 2
        fc_ref[...] = f
        fr_ref[...] = f.T[:8, :]

    return _pcall(
        body, (z, bf), dep, name=name, grid=(1,),
        in_specs=[pl.BlockSpec((s, 128), lambda i: (0, Z_FG)), pl.BlockSpec((1, 128), lambda i: (0, 0))],
        out_specs=[pl.BlockSpec((s, 128), lambda i: (0, 0)), pl.BlockSpec((8, s), lambda i: (0, 0))],
        out_shape=[SDS((s, 128), F32), SDS((8, s), F32)], compiler_params=_cp(("arbitrary",)))


def _fgate_bwd(z, bf, dfrow, dfcol, name):
    s = z.shape[0]

    def body(z_ref, b_ref, d_ref, dc_ref, dz_ref, db_ref):
        d = jnp.concatenate([d_ref[...], jnp.zeros((120, s), F32)], axis=0).T + dc_ref[...]
        row = lax.broadcasted_iota(jnp.int32, d.shape, 0)
        lane = lax.broadcasted_iota(jnp.int32, d.shape, 1)
        sh = 1
        while sh < s:
            d = d + jnp.where(row < s - sh, pltpu.roll(d, s - sh, 0), 0.0)
            sh *= 2
        xg = z_ref[...].astype(F32) + b_ref[...]
        dz = jnp.where(lane < 4, d * _sigmoid(-xg), 0.0)
        dz_ref[...] = dz.astype(BF16)
        db_ref[...] = _colsum(dz)

    return pl.pallas_call(
        body, name=name, grid=(1,),
        in_specs=[pl.BlockSpec((s, 128), lambda i: (0, Z_FG)), pl.BlockSpec((1, 128), lambda i: (0, 0)),
                  pl.BlockSpec((8, s), lambda i: (0, 0)), pl.BlockSpec((s, 128), lambda i: (0, 0))],
        out_specs=[pl.BlockSpec((s, 128), lambda i: (0, 0)), pl.BlockSpec((1, 128), lambda i: (0, 0))],
        out_shape=[SDS((s, 128), BF16), SDS((1, 128), F32)], compiler_params=_cp(("arbitrary",)))(z, bf, dfrow, dfcol)


def _fox_scores(q_ref, k_ref, fc_ref, fr_ref, h, i, nk, tq):
    kw = nk * tq
    q = q_ref[:, HD * h:HD * (h + 1)] * SCALE
    sc = _dot(q, k_ref[0:kw, HD * h:HD * (h + 1)], "nt") + fc_ref[:, h:h + 1] - fr_ref[h:h + 1, 0:kw]
    qpos = i * tq + lax.broadcasted_iota(jnp.int32, (tq, kw), 0)
    kpos = lax.broadcasted_iota(jnp.int32, (tq, kw), 1)
    return q, jnp.where(kpos <= qpos, sc, NEG)


def _fox_fwd(z, fcol, frow, name):
    s = z.shape[0]
    tq = min(TQ, s)
    nc = s // tq

    def body(q_ref, k_ref, v_ref, fc_ref, fr_ref, y_ref, l_ref):
        i = pl.program_id(0)
        for n in range(0, nc, FOX_FWD_PAIR):
            nk = min(n + FOX_FWD_PAIR, nc)

            @pl.when((i >= n) & (i < n + FOX_FWD_PAIR))
            def _():
                kw = nk * tq
                lse = jnp.zeros((tq, 128), F32)
                for h in range(4):
                    _, sc = _fox_scores(q_ref, k_ref, fc_ref, fr_ref, h, i, nk, tq)
                    m = jnp.max(sc, axis=-1, keepdims=True)
                    p = jnp.exp(sc - m)
                    l = jnp.sum(p, axis=-1, keepdims=True)
                    y_ref[:, HD * h:HD * (h + 1)] = _dot(p, v_ref[0:kw, HD * h:HD * (h + 1)], "nn") / l
                    lse = _lane_put(lse, m + jnp.log(l), h)
                l_ref[...] = lse

    return pl.pallas_call(
        body, name=name, grid=(nc,),
        in_specs=[pl.BlockSpec((tq, GW), lambda i: (i, Z_FQ)), pl.BlockSpec((s, GW), lambda i: (0, Z_FK)),
                  pl.BlockSpec((s, GW), lambda i: (0, Z_FV)), pl.BlockSpec((tq, 128), lambda i: (i, 0)),
                  pl.BlockSpec((8, s), lambda i: (0, 0))],
        out_specs=[pl.BlockSpec((tq, GW), lambda i: (i, 0)), pl.BlockSpec((tq, 128), lambda i: (i, 0))],
        out_shape=[SDS((s, GW), F32), SDS((s, 128), F32)], compiler_params=_cp(("parallel",)))(z, z, z, fcol, frow)


def _fox_bwd(z, fcol, frow, lse, y, dy, name):
    s = z.shape[0]
    tq = min(TQ, s)
    nc = s // tq
    bounds = tuple(sorted({min(b, nc) for b in FOX_BWD_PATHS} | {nc}))

    def body(q_ref, k_ref, v_ref, fc_ref, fr_ref, l_ref, y_ref, dy_ref, dq_ref, dk_ref, dv_ref, df_ref, dfq_ref):
        @pl.when(pl.program_id(0) == 0)
        def _():
            dk_ref[...] = jnp.zeros_like(dk_ref)
            dv_ref[...] = jnp.zeros_like(dv_ref)
            df_ref[...] = jnp.zeros_like(df_ref)

        i = pl.program_id(0)
        for lo, nk in zip((0,) + bounds[:-1], bounds):
            @pl.when((i >= lo) & (i < nk))
            def _():
                kw = nk * tq
                dfq = jnp.zeros((tq, 128), F32)
                dyf = dy_ref[...].astype(F32)
                for h in range(4):
                    hs = slice(HD * h, HD * (h + 1))
                    q, sc = _fox_scores(q_ref, k_ref, fc_ref, fr_ref, h, i, nk, tq)
                    p = jnp.exp(sc - l_ref[:, h:h + 1])
                    dyh = dyf[:, hs]
                    dd = jnp.sum(dyh * y_ref[:, hs], axis=-1, keepdims=True)
                    ds = p * (_dot(dyh, v_ref[0:kw, hs], "nt") - dd)
                    dq_ref[:, hs] = _dot(ds, k_ref[0:kw, hs], "nn") * SCALE
                    dk_ref[0:kw, hs] += _dot(ds, q, "tn")
                    dv_ref[0:kw, hs] += _dot(p, dyh, "tn")
                    df_ref[h:h + 1, 0:kw] -= _colsum(ds)
                    dfq = _lane_put(dfq, jnp.sum(ds, axis=-1, keepdims=True), h)
                dfq_ref[...] = dfq

    tile = lambda w: pl.BlockSpec((tq, w), lambda i: (i, 0))
    full = pl.BlockSpec((s, GW), lambda i: (0, 0))
    rows8 = pl.BlockSpec((8, s), lambda i: (0, 0))
    return pl.pallas_call(
        body, name=name, grid=(nc,),
        in_specs=[pl.BlockSpec((tq, GW), lambda i: (i, Z_FQ)), pl.BlockSpec((s, GW), lambda i: (0, Z_FK)),
                  pl.BlockSpec((s, GW), lambda i: (0, Z_FV)), tile(128), rows8, tile(128), tile(GW), tile(GW)],
        out_specs=[tile(GW), full, full, rows8, tile(128)],
        out_shape=[SDS((s, GW), F32), SDS((s, GW), F32), SDS((s, GW), F32), SDS((8, s), F32), SDS((s, 128), F32)],
        compiler_params=_cp(("arbitrary",)))(z, z, z, fcol, frow, lse, y, dy)


def _swa_block(q_ref, k_ref, v_ref, n):
    qs = pl.multiple_of(n * WIN, WIN)
    ks = pl.multiple_of(jnp.maximum(n - 1, 0) * WIN, WIN)
    qb = q_ref[pl.ds(qs, WIN), :]
    kb = k_ref[pl.ds(ks, 2 * WIN), :]
    vb = v_ref[pl.ds(ks, 2 * WIN), :]
    rows = lax.broadcasted_iota(jnp.int32, (2 * WIN, 2 * WIN), 0) & (WIN - 1)
    dist = (qs + rows) - (ks + lax.broadcasted_iota(jnp.int32, (2 * WIN, 2 * WIN), 1))
    return qs, ks, qb, kb, vb, (dist >= 0) & (dist < WIN)


def _stack2(x, kvh):
    return jnp.concatenate([x[:, HD * (2 * kvh):HD * (2 * kvh + 1)], x[:, HD * (2 * kvh + 1):HD * (2 * kvh + 2)]], axis=0)


def _sink2(sink_ref, kvh):
    top = lax.broadcasted_iota(jnp.int32, (2 * WIN, 1), 0) < WIN
    return jnp.where(top, sink_ref[2 * kvh], sink_ref[2 * kvh + 1])


def _swa_fwd(z, sinks, name):
    s = z.shape[0]

    def body(sink_ref, q_ref, k_ref, v_ref, y_ref, l_ref):
        def step(n, carry):
            qs, ks, qb, kb, vb, valid = _swa_block(q_ref, k_ref, v_ref, n)
            lse = jnp.zeros((WIN, 128), F32)
            for kvh in range(2):
                kv = slice(HD * kvh, HD * (kvh + 1))
                sc = jnp.where(valid, _dot(_stack2(qb, kvh) * SCALE, kb[:, kv], "nt"), NEG)
                sink = _sink2(sink_ref, kvh)
                m = jnp.maximum(jnp.max(sc, axis=-1, keepdims=True), sink)
                p = jnp.exp(sc - m)
                den = jnp.sum(p, axis=-1, keepdims=True) + jnp.exp(sink - m)
                o = _dot(p, vb[:, kv], "nn") / den
                lrow = m + jnp.log(den)
                for j in range(2):
                    h = 2 * kvh + j
                    y_ref[pl.ds(qs, WIN), HD * h:HD * (h + 1)] = o[WIN * j:WIN * (j + 1), :]
                    lse = _lane_put(lse, lrow[WIN * j:WIN * (j + 1), :], h)
            l_ref[pl.ds(qs, WIN), :] = lse
            return carry

        lax.fori_loop(0, s // WIN, step, 0, unroll=2)

    return pl.pallas_call(
        body, name=name, grid=(1,),
        in_specs=[pl.BlockSpec(memory_space=pltpu.SMEM), pl.BlockSpec((s, GW), lambda i: (0, Z_SQ)),
                  pl.BlockSpec((s, 128), lambda i: (0, Z_SK)), pl.BlockSpec((s, 128), lambda i: (0, Z_SV))],
        out_specs=[pl.BlockSpec((s, GW), lambda i: (0, 0)), pl.BlockSpec((s, 128), lambda i: (0, 0))],
        out_shape=[SDS((s, GW), F32), SDS((s, 128), F32)], compiler_params=_cp(("arbitrary",)))(sinks, z, z, z)


def _swa_bwd(z, sinks, lse, y, dy, name):
    s = z.shape[0]

    def body(sink_ref, q_ref, k_ref, v_ref, l_ref, y_ref, dy_ref, dq_ref, dk_ref, dv_ref, dsink_ref):
        dk_ref[...] = jnp.zeros_like(dk_ref)
        dv_ref[...] = jnp.zeros_like(dv_ref)
        dsink_ref[...] = jnp.zeros_like(dsink_ref)

        def step(n, carry):
            qs, ks, qb, kb, vb, valid = _swa_block(q_ref, k_ref, v_ref, n)
            lse_b = l_ref[pl.ds(qs, WIN), :]
            yb = y_ref[pl.ds(qs, WIN), :]
            dyb = dy_ref[pl.ds(qs, WIN), :].astype(F32)
            dsink = jnp.zeros((1, 128), F32)
            for kvh in range(2):
                kv = slice(HD * kvh, HD * (kvh + 1))
                q = _stack2(qb, kvh) * SCALE
                dy2 = _stack2(dyb, kvh)
                sc = jnp.where(valid, _dot(q, kb[:, kv], "nt"), NEG)
                lh = jnp.concatenate([lse_b[:, 2 * kvh:2 * kvh + 1], lse_b[:, 2 * kvh + 1:2 * kvh + 2]], axis=0)
                p = jnp.exp(sc - lh)
                dd = jnp.sum(dy2 * _stack2(yb, kvh), axis=-1, keepdims=True)
                ds = p * (_dot(dy2, vb[:, kv], "nt") - dd)
                dq = _dot(ds, kb[:, kv], "nn") * SCALE
                dk_ref[pl.ds(ks, 2 * WIN), kv] += _dot(ds, q, "tn")
                dv_ref[pl.ds(ks, 2 * WIN), kv] += _dot(p, dy2, "tn")
                dsk = jnp.exp(_sink2(sink_ref, kvh) - lh) * dd
                for j in range(2):
                    h = 2 * kvh + j
                    dq_ref[pl.ds(qs, WIN), HD * h:HD * (h + 1)] = dq[WIN * j:WIN * (j + 1), :]
                    dsink = _lane_put(dsink, dsink[:, h:h + 1] - jnp.sum(dsk[WIN * j:WIN * (j + 1), :], axis=0, keepdims=True), h)
            dsink_ref[...] += dsink
            return carry

        lax.fori_loop(0, s // WIN, step, 0, unroll=2)

    full = lambda w: pl.BlockSpec((s, w), lambda i: (0, 0))
    return pl.pallas_call(
        body, name=name, grid=(1,),
        in_specs=[pl.BlockSpec(memory_space=pltpu.SMEM), pl.BlockSpec((s, GW), lambda i: (0, Z_SQ)),
                  pl.BlockSpec((s, 128), lambda i: (0, Z_SK)), pl.BlockSpec((s, 128), lambda i: (0, Z_SV)),
                  full(128), full(GW), full(GW)],
        out_specs=[full(GW), full(128), full(128), pl.BlockSpec((1, 128), lambda i: (0, 0))],
        out_shape=[SDS((s, GW), F32), SDS((s, 128), F32), SDS((s, 128), F32), SDS((1, 128), F32)],
        compiler_params=_cp(("arbitrary",)))(sinks, z, z, z, lse, y, dy)


_SUBLANES = 8


def _rotations(win, advance=False):
    n = win.shape[0]
    return [win] + [pltpu.roll(win, (n - b) if advance else b, 0) for b in range(1, _SUBLANES)]


def _delayed(rots, shift, halo, tm):
    a, b = divmod(shift, _SUBLANES)
    return rots[b][halo - _SUBLANES * a:halo - _SUBLANES * a + tm, :]


def _advanced(rots, shift, tm):
    a, b = divmod(shift, _SUBLANES)
    return rots[b][_SUBLANES * a:_SUBLANES * a + tm, :]


def _prev_halo(width, halo, tm, col):
    return pl.BlockSpec((halo, width), lambda i: (jnp.maximum(i * (tm // halo) - 1, 0), col))


def _glu_window(a_ref, g_ref, ah_ref, gh_ref):
    keep = (pl.program_id(0) > 0).astype(F32)
    a = jnp.concatenate([ah_ref[...].astype(F32) * keep, a_ref[...].astype(F32)], axis=0)
    g = jnp.concatenate([gh_ref[...].astype(F32), g_ref[...].astype(F32)], axis=0)
    return a * _sigmoid(g)


def _conv_fwd(z, cw, cb, lg, lb, pw, pb, name):
    s = z.shape[0]
    tm = min(TM, s)

    def body(a_ref, g_ref, ah_ref, gh_ref, w_ref, b_ref, lg_ref, lb_ref, pw_ref, pb_ref, y_ref, hc_ref):
        rots = _rotations(_glu_window(a_ref, g_ref, ah_ref, gh_ref))
        hc = jnp.zeros((tm, GW), F32) + b_ref[...]
        for k in range(CONV_K):
            hc = hc + w_ref[k:k + 1, :] * _delayed(rots, CONV_K - 1 - k, CONV_HALO, tm)
        hc_ref[...] = hc
        xh, _ = _ln_stats(hc)
        y_ref[...] = _dot(_silu(xh * lg_ref[...] + lb_ref[...]), pw_ref[...], "nn") + pb_ref[...]

    tile = lambda col: pl.BlockSpec((tm, GW), lambda i: (i, col))
    whole = lambda a: pl.BlockSpec(a.shape, lambda i: (0, 0))
    return pl.pallas_call(
        body, name=name, grid=(s // tm,),
        in_specs=[tile(Z_CA), tile(Z_CG), _prev_halo(GW, CONV_HALO, tm, Z_CA), _prev_halo(GW, CONV_HALO, tm, Z_CG),
                  whole(cw), whole(cb), whole(lg), whole(lb), whole(pw), whole(pb)],
        out_specs=[tile(0), tile(0)], out_shape=[SDS((s, GW), F32), SDS((s, GW), F32)],
        compiler_params=_cp(("parallel",)))(z, z, z, z, cw, cb, lg, lb, pw, pb)


def _conv_bwd_a(z, hc, dy, cw, lg, lb, pw, name):
    s = z.shape[0]
    tm = min(TM, s)

    def body(a_ref, g_ref, ah_ref, gh_ref, hc_ref, dy_ref, lg_ref, lb_ref, pw_ref,
             dhc_ref, dpw_ref, dpb_ref, dlg_ref, dlb_ref, dcw_ref, dcb_ref):
        first = pl.program_id(0) == 0
        dy = dy_ref[...].astype(F32)
        xh, rstd = _ln_stats(hc_ref[...])
        hn = xh * lg_ref[...] + lb_ref[...]
        dhn = _dot(dy, pw_ref[...], "nt") * _dsilu(hn)
        dhc = _ln_bwd(xh, rstd, dhn * lg_ref[...])
        dhc_ref[...] = dhc
        _acc(dpw_ref, _dot(_silu(hn), dy, "tn"), first)
        _acc(dpb_ref, _colsum(dy), first)
        _acc(dlg_ref, _colsum(dhn * xh), first)
        _acc(dlb_ref, _colsum(dhn), first)
        _acc(dcb_ref, _colsum(dhc), first)
        rots = _rotations(_glu_window(a_ref, g_ref, ah_ref, gh_ref))

        @pl.when(first)
        def _():
            dcw_ref[...] = jnp.zeros_like(dcw_ref)

        for k in range(CONV_K):
            dcw_ref[k:k + 1, :] += _colsum(dhc * _delayed(rots, CONV_K - 1 - k, CONV_HALO, tm))

    tile = lambda col: pl.BlockSpec((tm, GW), lambda i: (i, col))
    whole = lambda shape: pl.BlockSpec(shape, lambda i: (0, 0))
    return pl.pallas_call(
        body, name=name, grid=(s // tm,),
        in_specs=[tile(Z_CA), tile(Z_CG), _prev_halo(GW, CONV_HALO, tm, Z_CA), _prev_halo(GW, CONV_HALO, tm, Z_CG),
                  tile(0), tile(0), whole(lg.shape), whole(lb.shape), whole(pw.shape)],
        out_specs=[tile(0), whole((GW, GW)), whole((1, GW)), whole((1, GW)), whole((1, GW)), whole((32, GW)), whole((1, GW))],
        out_shape=[SDS((s, GW), F32), SDS((GW, GW), F32), SDS((1, GW), F32), SDS((1, GW), F32), SDS((1, GW), F32),
                   SDS((32, GW), F32), SDS((1, GW), F32)],
        compiler_params=_cp(("arbitrary",)))(z, z, z, z, hc, dy, lg, lb, pw)


def _conv_bwd_b(z, dhc, cw, name):
    s = z.shape[0]
    tm = min(TM, s)
    nt = s // tm

    def body(a_ref, g_ref, d_ref, dn_ref, w_ref, da_ref, dg_ref):
        keep = (pl.program_id(0) < nt - 1).astype(F32)
        rots = _rotations(jnp.concatenate([d_ref[...], dn_ref[...] * keep], axis=0), advance=True)
        dhg = jnp.zeros((tm, GW), F32)
        for k in range(CONV_K):
            dhg = dhg + w_ref[k:k + 1, :] * _advanced(rots, CONV_K - 1 - k, tm)
        sg = _sigmoid(g_ref[...].astype(F32))
        da_ref[...] = (dhg * sg).astype(BF16)
        dg_ref[...] = (dhg * a_ref[...].astype(F32) * sg * (1.0 - sg)).astype(BF16)

    tile = lambda col: pl.BlockSpec((tm, GW), lambda i: (i, col))
    nxt = pl.BlockSpec((CONV_HALO, GW), lambda i: (jnp.minimum((i + 1) * (tm // CONV_HALO), s // CONV_HALO - 1), 0))
    return pl.pallas_call(
        body, name=name, grid=(nt,),
        in_specs=[tile(Z_CA), tile(Z_CG), tile(0), nxt, pl.BlockSpec(cw.shape, lambda i: (0, 0))],
        out_specs=[tile(0), tile(0)], out_shape=[SDS((s, GW), BF16), SDS((s, GW), BF16)],
        compiler_params=_cp(("parallel",)))(z, z, dhc, dhc, cw)


def _sgu_chunk(zu, zv, lg, lb, wcat, bfull):
    u, v = _gelu(zu), _gelu(zv)
    xh, rstd = _ln_stats(v)
    vn = xh * lg + lb
    lane = lax.shift_right_logical(lax.broadcasted_iota(jnp.int32, (WIN, GW), 1), 6)
    r = jnp.concatenate([jnp.where(lane == g, vn, 0.0) for g in range(4)], axis=0)
    mix = _dot(wcat, r, "nn") + bfull
    return u, xh, rstd, r, mix, lane


def _tril4(w):
    t = lax.broadcasted_iota(jnp.int32, w.shape, 0)
    sidx = lax.broadcasted_iota(jnp.int32, w.shape, 1) & (WIN - 1)
    return jnp.where(sidx <= t, w, 0.0)


def _sgu_fwd(z, lg, lb, wcat, bfull, name):
    s = z.shape[0]
    tm = min(TM, s)

    def body(u_ref, v_ref, lg_ref, lb_ref, w_ref, b_ref, y_ref):
        w = _tril4(w_ref[...])
        for n in range(tm // WIN):
            rows = slice(WIN * n, WIN * (n + 1))
            u, _, _, _, mix, _ = _sgu_chunk(u_ref[rows, :].astype(F32), v_ref[rows, :].astype(F32), lg_ref[...], lb_ref[...], w, b_ref[...])
            y_ref[rows, :] = u * mix

    tile = lambda col: pl.BlockSpec((tm, GW), lambda i: (i, col))
    whole = lambda a: pl.BlockSpec(a.shape, lambda i: (0, 0))
    return pl.pallas_call(
        body, name=name, grid=(s // tm,), in_specs=[tile(Z_GU), tile(Z_GV), whole(lg), whole(lb), whole(wcat), whole(bfull)],
        out_specs=tile(0), out_shape=SDS((s, GW), F32), compiler_params=_cp(("parallel",)))(z, z, lg, lb, wcat, bfull)


def _sgu_bwd(z, dy, lg, lb, wcat, bfull, name):
    s = z.shape[0]
    tm = min(TM, s)

    def body(u_ref, v_ref, dy_ref, lg_ref, lb_ref, w_ref, b_ref, du_ref, dv_ref, dw_ref, db_ref, dlg_ref, dlb_ref):
        first = pl.program_id(0) == 0
        w = _tril4(w_ref[...])
        wt = w.T
        dw = jnp.zeros((WIN, 4 * WIN), F32)
        db = jnp.zeros((WIN, 128), F32)
        dlg = jnp.zeros((1, GW), F32)
        dlb = jnp.zeros((1, GW), F32)
        for n in range(tm // WIN):
            rows = slice(WIN * n, WIN * (n + 1))
            zu, zv, dout = u_ref[rows, :].astype(F32), v_ref[rows, :].astype(F32), dy_ref[rows, :].astype(F32)
            u, xh, rstd, r, mix, lane = _sgu_chunk(zu, zv, lg_ref[...], lb_ref[...], w, b_ref[...])
            dmix = dout * u
            du_ref[rows, :] = (dout * mix * _dgelu(zu)).astype(BF16)
            dw = dw + _dot(dmix, r, "nt")
            for g in range(4):
                db = _lane_put(db, db[:, g:g + 1] + jnp.sum(dmix[:, HD * g:HD * (g + 1)], axis=1, keepdims=True), g)
            dr = _dot(wt, dmix, "nn")
            dvn = jnp.zeros((WIN, GW), F32)
            for g in range(4):
                dvn = dvn + jnp.where(lane == g, dr[WIN * g:WIN * (g + 1), :], 0.0)
            dlg = dlg + _colsum(dvn * xh)
            dlb = dlb + _colsum(dvn)
            dv_ref[rows, :] = (_ln_bwd(xh, rstd, dvn * lg_ref[...]) * _dgelu(zv)).astype(BF16)
        _acc(dw_ref, _tril4(dw), first)
        _acc(db_ref, db, first)
        _acc(dlg_ref, dlg, first)
        _acc(dlb_ref, dlb, first)

    tile = lambda col: pl.BlockSpec((tm, GW), lambda i: (i, col))
    whole = lambda shape: pl.BlockSpec(shape, lambda i: (0, 0))
    return pl.pallas_call(
        body, name=name, grid=(s // tm,),
        in_specs=[tile(Z_GU), tile(Z_GV), tile(0), whole(lg.shape), whole(lb.shape), whole(wcat.shape), whole(bfull.shape)],
        out_specs=[tile(0), tile(0), whole((WIN, 4 * WIN)), whole((WIN, 128)), whole((1, GW)), whole((1, GW))],
        out_shape=[SDS((s, GW), BF16), SDS((s, GW), BF16), SDS((WIN, 4 * WIN), F32), SDS((WIN, 128), F32),
                   SDS((1, GW), F32), SDS((1, GW), F32)],
        compiler_params=_cp(("arbitrary",)))(z, z, dy, lg, lb, wcat, bfull)


def _conv3(win, w, b):
    return (w[2:3, :] * win[FFN_HALO:, :] + w[1:2, :] * pltpu.roll(win, 1, 0)[FFN_HALO:, :]
            + w[0:1, :] * pltpu.roll(win, 2, 0)[FFN_HALO:, :] + b)


def _ffn_specs(s, tm):
    main = pl.BlockSpec((2, None, tm, FF_BLK), lambda j, i: (0, j, i, 0))
    prev = pl.BlockSpec((2, None, FFN_HALO, FF_BLK), lambda j, i: (0, j, jnp.maximum(i * (tm // FFN_HALO) - 1, 0), 0))
    nxt = pl.BlockSpec((2, None, FFN_HALO, FF_BLK),
                       lambda j, i: (0, j, jnp.minimum((i + 1) * (tm // FFN_HALO), s // FFN_HALO - 1), 0))
    wsp = pl.BlockSpec((2, None, 3, FF_BLK), lambda j, i: (0, j, 0, 0))
    bsp = pl.BlockSpec((2, None, 1, FF_BLK), lambda j, i: (0, j, 0, 0))
    return main, prev, nxt, wsp, bsp


def _ffn_act(u4, w4, b4, name, dep=None):
    s = u4.shape[2]
    tm = min(TM, s)
    main, prev, _, wsp, bsp = _ffn_specs(s, tm)

    def body(u_ref, uh_ref, w_ref, b_ref, o_ref, c_ref):
        keep = (pl.program_id(1) > 0).astype(F32)
        gw, vw = [jnp.concatenate([uh_ref[p].astype(F32) * keep, u_ref[p].astype(F32)], axis=0) for p in range(2)]
        gc, vc = _conv3(gw, w_ref[0], b_ref[0]), _conv3(vw, w_ref[1], b_ref[1])
        o_ref[...] = (_silu(gc) * vc).astype(BF16)
        c_ref[0] = gc.astype(BF16)
        c_ref[1] = vc.astype(BF16)

    return _pcall(
        body, (u4, u4, w4, b4), dep, name=name, grid=(FF_NBLK, s // tm), in_specs=[main, prev, wsp, bsp],
        out_specs=[pl.BlockSpec((None, tm, FF_BLK), lambda j, i: (j, i, 0)), main],
        out_shape=[SDS((FF_NBLK, s, FF_BLK), BF16), SDS(u4.shape, BF16)], compiler_params=_cp(("parallel", "parallel")))


def _ffn_bwd(u4, cv4, dact, w4, w_up, name, dep=None):
    s = u4.shape[2]
    tm = min(TM, s)
    nt = s // tm
    main = pl.BlockSpec((2, None, tm, FF_BLK), lambda i, j: (0, j, i, 0))
    nxt = pl.BlockSpec((2, None, FFN_HALO, FF_BLK),
                       lambda i, j: (0, j, jnp.minimum((i + 1) * (tm // FFN_HALO), s // FFN_HALO - 1), 0))
    dmain = pl.BlockSpec((None, tm, FF_BLK), lambda i, j: (j, i, 0))
    dnext = pl.BlockSpec((None, FFN_HALO, FF_BLK), lambda i, j: (j, jnp.minimum((i + 1) * (tm // FFN_HALO), s // FFN_HALO - 1), 0))
    wsp = pl.BlockSpec((2, None, 3, FF_BLK), lambda i, j: (0, j, 0, 0))
    wup = pl.BlockSpec((2, None, D, FF_BLK), lambda i, j: (0, j, 0, 0))
    all_w = pl.BlockSpec((2, FF_NBLK, 3, FF_BLK), lambda i, j: (0, 0, 0, 0))
    all_b = pl.BlockSpec((2, FF_NBLK, 1, FF_BLK), lambda i, j: (0, 0, 0, 0))

    def body(u_ref, c_ref, cn_ref, d_ref, dn_ref, w_ref, wup_ref, du_ref, dw_ref, db_ref, dh_ref, acc_ref):
        i, j = pl.program_id(0), pl.program_id(1)
        first = i == 0
        keep_next = (i < nt - 1).astype(F32)
        taps = [[jnp.zeros((1, FF_BLK), F32) for _ in range(3)] for _ in range(2)]
        bias = [jnp.zeros((1, FF_BLK), F32) for _ in range(2)]
        def matmul(rows, dus):
            prod = _dot(dus[0], wup_ref[0], "nt") + _dot(dus[1], wup_ref[1], "nt")
            acc_ref[rows, :] = jnp.where(j == 0, prod, acc_ref[rows, :] + prod)

        pending = []
        for r0 in range(0, tm, FFN_SUB):
            rows, wide = slice(r0, r0 + FFN_SUB), slice(r0, r0 + FFN_SUB + FFN_HALO)
            if r0 + FFN_SUB < tm:
                gc, vc = [c_ref[p, wide, :].astype(F32) for p in range(2)]
                d = d_ref[wide, :].astype(F32)
            else:
                gc, vc = [jnp.concatenate([c_ref[p, rows, :].astype(F32), cn_ref[p].astype(F32)], axis=0) for p in range(2)]
                d = jnp.concatenate([d_ref[rows, :].astype(F32), dn_ref[...].astype(F32) * keep_next], axis=0)
            sg = _sigmoid(gc)
            duc = (d * vc * (sg * (1.0 + gc * (1.0 - sg))), d * (gc * sg))
            dus = []
            for p in range(2):
                w = w_ref[p]
                own = duc[p][:FFN_SUB, :]
                adv = (pltpu.roll(duc[p], FFN_SUB + FFN_HALO - 2, 0)[:FFN_SUB, :],
                       pltpu.roll(duc[p], FFN_SUB + FFN_HALO - 1, 0)[:FFN_SUB, :], own)
                du = (w[2:3, :] * adv[2] + w[1:2, :] * adv[1] + w[0:1, :] * adv[0]).astype(BF16)
                du_ref[p, rows, :] = du
                dus.append(du)
                ut = u_ref[p, rows, :].astype(F32)
                for k in range(3):
                    taps[p][k] = taps[p][k] + _colsum(adv[k] * ut)
                bias[p] = bias[p] + _colsum(own)
            pending.append((rows, dus))
            if len(pending) > 1:
                matmul(*pending.pop(0))
        matmul(*pending.pop(0))

        for p in range(2):
            @pl.when(first)
            def _():
                db_ref[p, j] = bias[p]
                for k in range(3):
                    dw_ref[p, j, k:k + 1, :] = taps[p][k]

            @pl.when(jnp.logical_not(first))
            def _():
                db_ref[p, j] += bias[p]
                for k in range(3):
                    dw_ref[p, j, k:k + 1, :] += taps[p][k]

        @pl.when(j == FF_NBLK - 1)
        def _():
            dh_ref[...] = acc_ref[...].astype(BF16)

    return _pcall(
        body, (u4, cv4, cv4, dact, dact, w4, w_up.reshape(2, FF_NBLK, D, FF_BLK)), dep, name=name, grid=(nt, FF_NBLK),
        in_specs=[main, main, nxt, dmain, dnext, wsp, wup],
        out_specs=[main, all_w, all_b, pl.BlockSpec((tm, D), lambda i, j: (i, 0))],
        out_shape=[SDS(u4.shape, BF16), SDS((2, FF_NBLK, 3, FF_BLK), F32), SDS((2, FF_NBLK, 1, FF_BLK), F32), SDS((s, D), BF16)],
        scratch_shapes=[pltpu.VMEM((tm, D), F32)], compiler_params=_cp(("arbitrary", "arbitrary")))


def _sum8(parts, name):
    _, r, c = parts[0].shape
    tr = r
    for cand in (512, 256, 128, 64, 32, 16):
        if r % cand == 0 and r > cand:
            tr = cand
            break
    nb = r // tr

    def body(*refs):
        o_ref = refs[-1]
        for l, p_ref in enumerate(refs[:-1]):
            @pl.when(pl.program_id(0) == l)
            def _():
                acc = p_ref[0].astype(F32)
                for j in range(1, N_DEV):
                    acc = acc + p_ref[j].astype(F32)
                o_ref[...] = acc

    def spec(l):
        return pl.BlockSpec((N_DEV, tr, c), lambda ll, i: (0, jnp.where(ll == l, i, jnp.where(ll < l, 0, nb - 1)), 0))

    return pl.pallas_call(
        body, name=name, grid=(len(parts), nb), in_specs=[spec(l) for l in range(len(parts))],
        out_specs=pl.BlockSpec((None, tr, c), lambda ll, i: (ll, i, 0)), out_shape=SDS((len(parts), r, c), F32),
        compiler_params=_cp(("arbitrary", "arbitrary")))(*parts)


def _sum8_small(parts, name):
    n = len(parts)

    def body(*refs):
        for p_ref, o_ref in zip(refs[:n], refs[n:]):
            acc = p_ref[0]
            for j in range(1, N_DEV):
                acc = acc + p_ref[j]
            o_ref[...] = acc

    return pl.pallas_call(body, name=name, out_shape=[SDS(p.shape[1:], F32) for p in parts], compiler_params=_cp())(*parts)


def _adamw_math(w, g, m, v):
    m = ADAM_B1 * m + (1.0 - ADAM_B1) * g
    v = ADAM_B2 * v + (1.0 - ADAM_B2) * (g * g)
    m_hat = m / (1.0 - ADAM_B1 ** ADAM_STEP)
    v_hat = v / (1.0 - ADAM_B2 ** ADAM_STEP)
    return -ADAM_LR * (m_hat / (jnp.sqrt(v_hat) + ADAM_EPS) + ADAM_WD * w), m, v


def _adamw(w, g, m, v, name):
    r, c = w.shape
    tr = r
    for cand in (256, 128, 64):
        if r % cand == 0 and r > cand:
            tr = cand
            break

    def body(w_ref, g_ref, m_ref, v_ref, d_ref, mo_ref, vo_ref):
        d_ref[...], mo_ref[...], vo_ref[...] = _adamw_math(w_ref[...], g_ref[...], m_ref[...], v_ref[...])

    blk = pl.BlockSpec((tr, c), lambda i: (i, 0))
    return pl.pallas_call(body, name=name, grid=(r // tr,), in_specs=[blk] * 4, out_specs=[blk] * 3,
                          out_shape=[SDS((r, c), F32)] * 3, compiler_params=_cp(("parallel",)))(w, g, m, v)


def _adamw_small(ws, gs, ms, vs, name):
    n = len(ws)

    def body(*refs):
        ins, outs = refs[:4 * n], refs[4 * n:]
        for i in range(n):
            d, m, v = _adamw_math(ins[i][...], ins[n + i][...], ins[2 * n + i][...], ins[3 * n + i][...])
            outs[i][...], outs[n + i][...], outs[2 * n + i][...] = d, m, v

    shapes = [SDS(w.shape, F32) for w in ws]
    res = pl.pallas_call(body, name=name, out_shape=shapes * 3, compiler_params=_cp())(*ws, *gs, *ms, *vs)
    return res[:n], res[n:2 * n], res[2 * n:]


def _perm_in(w):
    pad = jnp.zeros(w.shape[:-1] + (ZW - 2308,), w.dtype)
    return jnp.concatenate([w[..., :768], w[..., 772:], w[..., 768:772], pad], axis=-1)


def _unperm_in(g):
    return jnp.concatenate([g[..., :768], g[..., 2304:2308], g[..., 768:2304]], axis=-1)


def _wcat(sgu_w):
    return sgu_w.transpose(1, 0, 2).reshape(WIN, 4 * WIN)


def _layer_fwd(l, x, h1, mod, p, wg, last, target, nxt, w_in_next):
    s = x.shape[0]
    tag = f"_l{l}"
    mrow = lambda k: (mod, 6 * l + k)
    z = _mm_rows(h1, wg["w_in"].get(h1), "nn", ZW, BF16, "mm_z" + tag)
    fcol, frow = _fgate(z, p["bf"], "fgate" + tag, dep=wg["w_out"].prefetch(z))
    y_fox, lse_fox = _fox_fwd(z, fcol, frow, "fox_fwd" + tag)
    y_conv, hc = _conv_fwd(z, wg["conv_w"], p["conv_b"], p["conv_ln_g"], p["conv_ln_b"], wg["conv_pw_w"], p["conv_pw_b"],
                           "conv_fwd" + tag)
    y_swa, lse_swa = _swa_fwd(z, p["sinks"], "swa_fwd" + tag)
    y_sgu = _sgu_fwd(z, p["sgu_ln_g"], p["sgu_ln_b"], p["wcat"], p["bfull"], "sgu_fwd" + tag)
    ys = (y_fox, y_conv, y_swa, y_sgu)
    yn = _gnorm(ys, (p["g_group"], l), "gnorm" + tag)
    tok = wg["w_up"].prefetch(yn)
    o = _mm_rows(yn, wg["w_out"].get(yn), "nn", D, BF16, "mm_o" + tag)
    x1, h2 = _post(x, o, mrow(2), (p["g_post_mix"], l), (p["g_pre_ffn"], l), mrow(4), mrow(3), "post_mix" + tag, dep=tok)
    tok = wg["w_down"].prefetch(h2)
    u = _matmul(h2, wg["w_up"].get(h2), "nn", (N_DEV, s, FF_BLK), BF16, (N_DEV, 1, 1),
                _bs((s, D), lambda j, i, k: (0, 0)), _bs((None, D, FF_BLK), lambda j, i, k: (j, 0, 0)),
                _bs((None, s, FF_BLK), lambda j, i, k: (j, 0, 0)), None, "mm_u" + tag)
    u4 = u.reshape(2, FF_NBLK, s, FF_BLK)
    act, cv4 = _ffn_act(u4, wg["ffn_conv_w"], p["ffn_conv_b"], "ffn_act" + tag, dep=tok)
    tok = None if w_in_next is None else w_in_next.prefetch(act)
    f = _matmul(act, wg["w_down"].get(act), "nn", (s, D), BF16, (1, 1, FF_NBLK),
                _bs((None, s, FF_BLK), lambda i, j, k: (k, 0, 0)), _bs((FF_BLK, D), lambda i, j, k: (k, 0)),
                _bs((s, D), lambda i, j, k: (0, 0)), (s, D), "mm_f" + tag)
    if last:
        out = _post_loss(x1, f, mrow(5), (p["g_post_ffn"], l), target, "post_loss")
    else:
        out = _post(x1, f, mrow(5), (p["g_post_ffn"], l), *nxt, "post_ffn" + tag, dep=tok)
    saved = dict(x=x, h1=h1, z=z, fcol=fcol, frow=frow, lse_fox=lse_fox, hc=hc, lse_swa=lse_swa, ys=ys, yn=yn, o=o, x1=x1,
                 h2=h2, u4=u4, cv4=cv4, act=act, f=f)
    return out, saved


def _tie(a, token):
    return a if token is None else a + token[0, 0]


def _layer_bwd(l, dx2, sv, mod, p, wg, emit, dep=None):
    s = dx2.shape[0]
    tm = min(TM, s)
    tag = f"_l{l}"
    mrow = lambda k: (mod, 6 * l + k)
    g = {}
    df, g["ga2"], g["g_post_ffn"] = _post_bwd(dx2, sv["f"], mrow(5), (p["g_post_ffn"], l), "post_ffn_bwd" + tag, dep=dep)
    dact = _matmul(df, wg["w_down"].get(None), "nt", (FF_NBLK, s, FF_BLK), BF16, (FF_NBLK, 1, 1),
                   _bs((s, D), lambda j, i, k: (0, 0)), _bs((FF_BLK, D), lambda j, i, k: (j, 0)),
                   _bs((None, s, FF_BLK), lambda j, i, k: (j, 0, 0)), None, "mm_dact" + tag)
    tok = emit("w_down", _matmul(sv["act"], df, "tn", (FF_NBLK * FF_BLK, D), BF16, (FF_NBLK, 1, 1),
                                 _bs((None, s, FF_BLK), lambda j, i, k: (j, 0, 0)), _bs((s, D), lambda j, i, k: (0, 0)),
                                 _bs((FF_BLK, D), lambda j, i, k: (j, 0)), None, "mm_dwdown" + tag))
    du, g["ffn_conv_w"], g["ffn_conv_b"], dh2 = _ffn_bwd(sv["u4"], sv["cv4"], dact, wg["ffn_conv_w"], wg["w_up"].get(None),
                                                         "ffn_bwd" + tag, dep=tok)
    du = du.reshape(N_DEV, s, FF_BLK)
    tok = emit("w_up", _matmul(du, sv["h2"], "tn", (N_DEV, FF_BLK, D), BF16, (N_DEV, 1, 1),
                               _bs((None, s, FF_BLK), lambda j, i, k: (j, 0, 0)), _bs((s, D), lambda j, i, k: (0, 0)),
                               _bs((None, FF_BLK, D), lambda j, i, k: (j, 0, 0)), None, "mm_dwup" + tag))
    dx1, g["sh2"], g["sc2"], g["g_pre_ffn"] = _pre_bwd(dh2, sv["x1"], dx2, (p["g_pre_ffn"], l), mrow(4), "pre_ffn_bwd" + tag,
                                                       dep=tok)
    do, g["ga1"], g["g_post_mix"] = _post_bwd(dx1, sv["o"], mrow(2), (p["g_post_mix"], l), "post_mix_bwd" + tag)
    dyn = _mm_rows(do, wg["w_out"].get(None), "nt", D, BF16, "mm_dyn" + tag)
    tok = emit("w_out", _mm_wgrad(sv["yn"], do, BF16, "mm_dwout" + tag))
    dy_fox, dy_conv, dy_swa, dy_sgu, g["g_group"] = _gnorm_bwd(dyn, sv["ys"], (p["g_group"], l), "gnorm_bwd" + tag, dep=tok)
    z = sv["z"]
    dq_f, dk_f, dv_f, dfrow, dfcol = _fox_bwd(z, sv["fcol"], sv["frow"], sv["lse_fox"], sv["ys"][0], dy_fox, "fox_bwd" + tag)
    dgate, g["bf"] = _fgate_bwd(z, p["bf"], dfrow, dfcol, "fgate_bwd" + tag)
    dhc, g["conv_pw_w"], g["conv_pw_b"], g["conv_ln_g"], g["conv_ln_b"], g["conv_w"], g["conv_b"] = _conv_bwd_a(
        z, sv["hc"], dy_conv, wg["conv_w"], p["conv_ln_g"], p["conv_ln_b"], wg["conv_pw_w"], "conv_bwd_a" + tag)
    da_c, dg_c = _conv_bwd_b(z, dhc, wg["conv_w"], "conv_bwd_b" + tag)
    dq_s, dk_s, dv_s, g["sinks"] = _swa_bwd(z, p["sinks"], sv["lse_swa"], sv["ys"][2], dy_swa, "swa_bwd" + tag)
    du_g, dv_g, g["wcat"], g["sgu_bcol"], g["sgu_ln_g"], g["sgu_ln_b"] = _sgu_bwd(
        z, dy_sgu, p["sgu_ln_g"], p["sgu_ln_b"], p["wcat"], p["bfull"], "sgu_bwd" + tag)
    dz = jnp.concatenate([dq_f.astype(BF16), dk_f.astype(BF16), dv_f.astype(BF16), da_c, dg_c, dq_s.astype(BF16), dk_s.astype(BF16),
                          dv_s.astype(BF16), du_g, dv_g, dgate], axis=1)
    tok = emit("w_in", _mm_wgrad(sv["h1"], dz, BF16, "mm_dwin" + tag))
    dh1 = _mm_rows(dz, wg["w_in"].get(None), "nt", D, BF16, "mm_dh1" + tag)
    dx, g["sh1"], g["sc1"], g["g_pre_mix"] = _pre_bwd(dh1, sv["x"], dx1, (p["g_pre_mix"], l), mrow(1), "pre_mix_bwd" + tag,
                                                      dep=tok)
    return dx, g


def _layer_params(l, small, conv_w_full, conv_pw_full, ffn_conv_w_full):
    bf = jnp.pad(small["b_fgate"][l][None, :], ((0, 0), (0, 124)))
    p = dict(
        bf=bf, conv_b=small["conv_b"][l][None], conv_ln_g=small["conv_ln_g"][l][None], conv_ln_b=small["conv_ln_b"][l][None],
        conv_pw_b=small["conv_pw_b"][l][None], sinks=small["swa_sinks"][l], sgu_ln_g=small["sgu_ln_g"][l][None],
        sgu_ln_b=small["sgu_ln_b"][l][None], wcat=_wcat(small["sgu_w"][l]),
        bfull=jnp.repeat(small["sgu_b"][l].T, HD, axis=1),
        ffn_conv_b=small["ffn_conv_b"][l].reshape(2, FF_NBLK, 1, FF_BLK),
        g_group=small["g_group"].reshape(N_LAYER, 1, D), g_post_mix=small["g_post_mix"].reshape(N_LAYER, 1, D),
        g_pre_ffn=small["g_pre_ffn"].reshape(N_LAYER, 1, D), g_post_ffn=small["g_post_ffn"].reshape(N_LAYER, 1, D),
        g_pre_mix=small["g_pre_mix"].reshape(N_LAYER, 1, D))
    wsmall = dict(conv_w=conv_w_full[l], conv_pw_w=conv_pw_full[l].astype(BF16),
                  ffn_conv_w=ffn_conv_w_full[l].reshape(3, 2, FF_NBLK, FF_BLK).transpose(1, 2, 0, 3))
    return p, wsmall


def _local_step(x, target, mod, small, wbig, conv_w_full, conv_pw_full, ffn_conv_w_full, emit, on_loss=None):
    ps, wgs = [], []
    for l in range(N_LAYER):
        p, wsmall = _layer_params(l, small, conv_w_full, conv_pw_full, ffn_conv_w_full)
        ps.append(p)
        wgs.append({**wbig[l], **wsmall})
    h = _rms_mod(x, (ps[0]["g_pre_mix"], 0), (mod, 1), (mod, 0), "rms_mod_l0")
    saved = []
    for l in range(N_LAYER):
        last = l == N_LAYER - 1
        nxt = None if last else ((ps[l]["g_pre_mix"], l + 1), (mod, 6 * (l + 1) + 1), (mod, 6 * (l + 1)))
        out, sv = _layer_fwd(l, x, h, mod, ps[l], wgs[l], last, target, nxt, None if last else wgs[l + 1]["w_in"])
        saved.append(sv)
        if not last:
            x, h = out
    dx, loss = out
    dep = None if on_loss is None else on_loss(loss)
    grads = [None] * N_LAYER
    for l in reversed(range(N_LAYER)):
        dx, grads[l] = _layer_bwd(l, dx, saved[l], mod, ps[l], wgs[l], functools.partial(emit, l), dep)
        dep = None
    return loss, dx, grads


_SMALL = ("b_ada", "g_pre_mix", "g_post_mix", "g_pre_ffn", "g_post_ffn", "b_fgate", "conv_b", "conv_ln_g", "conv_ln_b",
          "conv_pw_b", "swa_sinks", "sgu_ln_g", "sgu_ln_b", "sgu_w", "sgu_b", "g_group", "ffn_conv_b")
_WEIGHTS = ("w_ada", "b_ada", "g_pre_mix", "g_post_mix", "g_pre_ffn", "g_post_ffn", "w_in", "b_fgate", "conv_w", "conv_b",
            "conv_ln_g", "conv_ln_b", "conv_pw_w", "conv_pw_b", "swa_sinks", "sgu_ln_g", "sgu_ln_b", "sgu_w", "sgu_b",
            "g_group", "w_out", "ffn_w_up", "ffn_conv_w", "ffn_conv_b", "ffn_w_down")


def _pad_rows(a, mult):
    r = (-a.shape[0]) % mult
    return a if r == 0 else jnp.concatenate([a, jnp.zeros((r,) + a.shape[1:], a.dtype)], axis=0)


def _view2d(a):
    if a.ndim == 2:
        return a
    return a.reshape(-1, a.shape[-1])


def kernel(x, c, w_ada, b_ada, g_pre_mix, g_post_mix, g_pre_ffn, g_post_ffn, w_in, b_fgate, conv_w, conv_b, conv_ln_g, conv_ln_b, conv_pw_w, conv_pw_b, swa_sinks, sgu_ln_g, sgu_ln_b, sgu_w, sgu_b, g_group, w_out, ffn_w_up, ffn_conv_w, ffn_conv_b, ffn_w_down, loss_target, m_w_ada, m_b_ada, m_g_pre_mix, m_g_post_mix, m_g_pre_ffn, m_g_post_ffn, m_w_in, m_b_fgate, m_conv_w, m_conv_b, m_conv_ln_g, m_conv_ln_b, m_conv_pw_w, m_conv_pw_b, m_swa_sinks, m_sgu_ln_g, m_sgu_ln_b, m_sgu_w, m_sgu_b, m_g_group, m_w_out, m_ffn_w_up, m_ffn_conv_w, m_ffn_conv_b, m_ffn_w_down, v_w_ada, v_b_ada, v_g_pre_mix, v_g_post_mix, v_g_pre_ffn, v_g_post_ffn, v_w_in, v_b_fgate, v_conv_w, v_conv_b, v_conv_ln_g, v_conv_ln_b, v_conv_pw_w, v_conv_pw_b, v_swa_sinks, v_sgu_ln_g, v_sgu_ln_b, v_sgu_w, v_sgu_b, v_g_group, v_w_out, v_ffn_w_up, v_ffn_conv_w, v_ffn_conv_b, v_ffn_w_down):
    env = dict(locals())
    w = {n: env[n] for n in _WEIGHTS}
    mom = {n: env["m_" + n] for n in _WEIGHTS}
    var = {n: env["v_" + n] for n in _WEIGHTS}
    me = 4 * lax.axis_index("x") + 2 * lax.axis_index("y") + lax.axis_index("c")
    x2, target = x[0], loss_target[0]

    (c_all,) = _exchange([c], ["bcast"], "gather_c")
    c_all = c_all.reshape(N_DEV, D)
    (m_all,) = _exchange([_ada_fwd(c_all, w_ada)], ["bcast"], "gather_mod")
    m_mine = lax.dynamic_index_in_dim(m_all, me, axis=2, keepdims=False)
    mod, mod_token = _ada_finish(m_mine.transpose(1, 0, 2).reshape(N_LAYER, 6 * D), b_ada)
    mod = mod.reshape(6 * N_LAYER, 1, D)

    shards = [_tie(conv_w, mod_token), conv_pw_w, ffn_conv_w]
    for l in range(N_LAYER):
        shards += [_perm_in(w_in[l]).astype(BF16), w_out[l].astype(BF16), ffn_w_up[l].astype(BF16), ffn_w_down[l].astype(BF16)]
    gather = _g2_start(shards, "gather_weights_start")
    mod = _tie(mod, gather.token)
    _g2_relay(gather, [0, 1, 2, 3], mod, "gather_relay_first")
    g_cw, g_pw, g_fcw = _g2_wait(gather, [0, 1, 2], mod, "gather_small_wait")
    conv_w_full = g_cw.transpose(1, 2, 0, 3).reshape(N_LAYER, CONV_K, GW)
    conv_pw_full = g_pw.transpose(1, 0, 2, 3).reshape(N_LAYER, GW, GW)
    ffn_conv_w_full = g_fcw.transpose(1, 2, 0, 3).reshape(N_LAYER, 3, N_DEV * FF_BLK)

    def lazy(i, shape, key):
        pre = None if i == 3 else (lambda after: _g2_relay(gather, [i], after, "relay_" + key))
        return _Lazy(lambda after: _g2_wait(gather, [i], after, "wait_" + key)[0].reshape(shape), pre)

    wbig = [dict(w_in=lazy(3 + 4 * l, (D, ZW), f"w_in_l{l}"), w_out=lazy(4 + 4 * l, (D, D), f"w_out_l{l}"),
                 w_up=lazy(5 + 4 * l, (N_DEV, D, FF_BLK), f"w_up_l{l}"),
                 w_down=lazy(6 + 4 * l, (FF_NBLK * FF_BLK, D), f"w_down_l{l}")) for l in range(N_LAYER)]

    grad_flights = []

    def emit(l, key, arr):
        fl = _xchg_start([arr.reshape(N_DEV, -1, arr.shape[-1])], ["a2a"], f"grad_start_{key}_l{l}")
        grad_flights.append(((l, key), fl))
        return fl.token

    small = {n: w[n] for n in _SMALL}
    total = []

    def on_loss(loss8):
        total.append(lax.psum(loss8[0, 0], ("x", "y", "c")))
        return total[0].reshape(1, 1)

    _, dx, grads = _local_step(x2, target, mod, small, wbig, conv_w_full, conv_pw_full, ffn_conv_w_full, emit, on_loss)
    loss = total[0]
    grad_x = dx[None]


    st = lambda key: jnp.stack([grads[l][key] for l in range(N_LAYER)])
    d_conv_w = st("conv_w")[:, :CONV_K, :].reshape(N_LAYER, CONV_K, N_DEV, GW // N_DEV).transpose(2, 0, 1, 3)
    d_pw_w = st("conv_pw_w").reshape(N_LAYER, N_DEV, GW // N_DEV, GW).transpose(1, 0, 2, 3)
    d_fcw = st("ffn_conv_w").reshape(N_LAYER, N_DEV, 3, FF_BLK).transpose(1, 0, 2, 3)
    rows_d = _pad_rows(jnp.concatenate(
        [grads[l][k] for l in range(N_LAYER) for k in ("sh1", "sc1", "ga1", "sh2", "sc2", "ga2")]
        + [grads[l][k] for k in ("g_pre_mix", "g_post_mix", "g_pre_ffn", "g_post_ffn", "g_group") for l in range(N_LAYER)],
        axis=0), 8)
    rows_gw = _pad_rows(jnp.concatenate(
        [grads[l][k] for k in ("conv_b", "conv_ln_g", "conv_ln_b", "conv_pw_b", "sgu_ln_g", "sgu_ln_b") for l in range(N_LAYER)],
        axis=0), 8)
    rows_128 = jnp.concatenate([_pad_rows(jnp.concatenate([grads[l]["bf"] for l in range(N_LAYER)]
                                                          + [grads[l]["sinks"] for l in range(N_LAYER)], axis=0), 8)]
                               + [grads[l]["sgu_bcol"] for l in range(N_LAYER)], axis=0)
    rows_w = jnp.concatenate([grads[l]["wcat"] for l in range(N_LAYER)], axis=0)
    rows_fb = st("ffn_conv_b").reshape(N_LAYER * N_DEV, FF_BLK)
    small_flight = _xchg_start([d_conv_w, d_pw_w, d_fcw, rows_d, rows_gw, rows_128, rows_w, rows_fb],
                               ["a2a"] * 3 + ["bcast"] * 5, "small_grads_start")

    to_mem = {"w_in": lambda a: a.transpose(2, 0, 1), "ffn_w_up": lambda a: a.transpose(0, 2, 1)}
    from_mem = {"w_in": lambda a: a.transpose(1, 2, 0), "ffn_w_up": lambda a: a.transpose(0, 2, 1)}
    flights = dict(grad_flights)
    gr, delta, new_m, new_v = {}, {}, {}, {}

    def adamw_big(n):
        view, back = to_mem.get(n, lambda a: a), from_mem.get(n, lambda a: a)
        shape = view(w[n]).shape
        d, m2, v2 = _adamw(_view2d(view(w[n])), _view2d(gr[n]), _view2d(view(mom[n])), _view2d(view(var[n])), "adamw_" + n)
        delta[n], new_m[n], new_v[n] = back(d.reshape(shape)), back(m2.reshape(shape)), back(v2.reshape(shape))
        gr[n] = back(gr[n].reshape(shape))

    after = small_flight.token
    for n, key in (("ffn_w_down", "w_down"), ("ffn_w_up", "w_up"), ("w_out", "w_out"), ("w_in", "w_in")):
        parts = [_xchg_wait(flights[(l, key)], [0], after, f"grad_wait_{key}_l{l}")[0] for l in reversed(range(N_LAYER))]
        g = _sum8(parts[::-1], "sum_" + key)
        gr[n] = to_mem["w_in"](_unperm_in(g)) if n == "w_in" else g
        adamw_big(n)
        after = new_v[n]

    small_parts = _xchg_wait(small_flight, list(range(8)), after, "small_grads_wait")
    s_conv_w, s_pw_w, s_fcw, s_d, s_gw, s_128, s_w, s_fb = _sum8_small(
        [p.reshape(N_DEV, -1, p.shape[-1]) for p in small_parts], "sum_small_grads")
    gr["conv_w"] = s_conv_w.reshape(N_LAYER, CONV_K, GW // N_DEV)
    gr["conv_pw_w"] = s_pw_w.reshape(N_LAYER, GW // N_DEV, GW)
    gr["ffn_conv_w"] = s_fcw.reshape(N_LAYER, 3, FF_BLK)
    gr["b_ada"] = s_d[:6 * N_LAYER].reshape(N_LAYER, 6 * D)
    for i, k in enumerate(("g_pre_mix", "g_post_mix", "g_pre_ffn", "g_post_ffn", "g_group")):
        gr[k] = s_d[6 * N_LAYER + 2 * i:6 * N_LAYER + 2 * i + 2]
    for i, k in enumerate(("conv_b", "conv_ln_g", "conv_ln_b", "conv_pw_b", "sgu_ln_g", "sgu_ln_b")):
        gr[k] = s_gw[2 * i:2 * i + 2]
    gr["b_fgate"] = s_128[0:2, :4]
    gr["swa_sinks"] = s_128[2:4, :4]
    gr["sgu_b"] = s_128[8:].reshape(N_LAYER, WIN, 128)[:, :, :4].transpose(0, 2, 1)
    gr["sgu_w"] = s_w.reshape(N_LAYER, WIN, 4, WIN).transpose(0, 2, 1, 3)
    gr["ffn_conv_b"] = s_fb.reshape(N_LAYER, N_DEV * FF_BLK)
    dmod_all = small_parts[3][:, :6 * N_LAYER, :].reshape(N_DEV, N_LAYER, 6 * D)
    ncol = 6 * D // N_DEV
    dmod_cols = lax.dynamic_slice_in_dim(dmod_all, me * ncol, ncol, axis=2).transpose(1, 0, 2)
    gr["w_ada"] = _ada_bwd(c_all, dmod_cols)

    adamw_big("w_ada")
    smalls = [n for n in _WEIGHTS if n not in ("w_ada", "w_in", "w_out", "ffn_w_up", "ffn_w_down")]
    ds, ms, vs = _adamw_small([_view2d(w[n]) for n in smalls], [_view2d(gr[n]) for n in smalls],
                              [_view2d(mom[n]) for n in smalls], [_view2d(var[n]) for n in smalls], "adamw_small")
    for i, n in enumerate(smalls):
        delta[n], new_m[n], new_v[n] = ds[i].reshape(w[n].shape), ms[i].reshape(w[n].shape), vs[i].reshape(w[n].shape)

    return (loss, grad_x, *[gr[n].reshape(w[n].shape) for n in _WEIGHTS], *[delta[n] for n in _WEIGHTS],
            *[new_m[n] for n in _WEIGHTS], *[new_v[n] for n in _WEIGHTS])
```
